```python
import jax, jax.numpy as jnp
from jax import lax
import numpy as np

D_MODEL = 1024
BATCH = 8
SEQ = 8192
DEPTH = 4

A_WIDTH = D_MODEL // 2
A_GROUPS = 8
A_GROUP_DIM = A_WIDTH // A_GROUPS
CHUNK = 128
B_WIDTH = D_MODEL // 2
CONV_WIDTH = 31
AB_IN = 2 * A_WIDTH + 2 * B_WIDTH
AB_OUT = A_WIDTH + B_WIDTH
HEAD_DIM = 64
C_HEADS = D_MODEL // HEAD_DIM
C_WIDTH = C_HEADS * HEAD_DIM
DILATED_PATTERNS = ((128, 1), (512, 4), (2048, 16))
ROT_DIM = HEAD_DIM // 4
ROPE_THETA = 500000.0
D_FF = 4 * D_MODEL
EPS = 1e-6
NEG = -1e30
N_EVEN = (DEPTH + 1) // 2
N_ODD = DEPTH // 2

kernel_name = "hybrid_gmlp_conv_dilated_attn_encoder"


def rmsnorm(t, g):
    tf = t.astype(jnp.float32)
    y = tf * lax.rsqrt(jnp.mean(tf * tf, axis=-1, keepdims=True) + EPS)
    return (y * g.astype(jnp.float32)).astype(t.dtype)


def layernorm(t, g, b):
    tf = t.astype(jnp.float32)
    mu = jnp.mean(tf, axis=-1, keepdims=True)
    var = jnp.mean(jnp.square(tf - mu), axis=-1, keepdims=True)
    y = (tf - mu) * lax.rsqrt(var + EPS)
    return (y * g.astype(jnp.float32) + b.astype(jnp.float32)).astype(t.dtype)


def rope_tables(seq):
    pos = jnp.arange(seq, dtype=jnp.float32)
    inv_freq = ROPE_THETA ** (-jnp.arange(0, ROT_DIM, 2, dtype=jnp.float32) / ROT_DIM)
    ang = pos[:, None] * inv_freq[None, :]
    return jnp.cos(ang), jnp.sin(ang)


def partial_rope(t, cos, sin):
    half = ROT_DIM // 2
    t1 = t[..., :half].astype(jnp.float32)
    t2 = t[..., half:ROT_DIM].astype(jnp.float32)
    c, s = cos[:, None, :], sin[:, None, :]
    rot = jnp.concatenate([t1 * c - t2 * s, t2 * c + t1 * s], axis=-1).astype(t.dtype)
    return jnp.concatenate([rot, t[..., ROT_DIM:]], axis=-1)


def dilated_band_attention(q, k, v, window, dilation):
    B, S, H, hd = q.shape
    half = window // (2 * dilation)
    blk = half
    L = S // dilation
    nb = -(-L // blk)
    Lp = nb * blk

    def to_strided(t):
        return t.reshape(B, L, dilation, H, hd).transpose(0, 2, 3, 1, 4)

    qs, ks, vs = to_strided(q), to_strided(k), to_strided(v)
    qb = jnp.pad(qs, ((0, 0),) * 3 + ((0, Lp - L), (0, 0))).reshape(B, dilation, H, nb, blk, hd)
    pad_kv = ((0, 0),) * 3 + ((blk, Lp - L + blk), (0, 0))
    ks, vs = jnp.pad(ks, pad_kv), jnp.pad(vs, pad_kv)

    def key_blocks(t):
        return jnp.concatenate(
            [t[..., o * blk:o * blk + Lp, :].reshape(B, dilation, H, nb, blk, hd) for o in range(3)],
            axis=-2)

    kb, vb = key_blocks(ks), key_blocks(vs)
    qi = jnp.arange(nb)[:, None, None] * blk + jnp.arange(blk)[None, :, None]
    kj = jnp.arange(nb)[:, None, None] * blk + jnp.arange(3 * blk)[None, None, :] - blk
    mask = (jnp.abs(kj - qi) <= half) & (kj >= 0) & (kj < L)

    s = jnp.einsum('brhnqd,brhnkd->brhnqk', qb.astype(jnp.float32), kb.astype(jnp.float32)) * (hd ** -0.5)
    s = jnp.where(mask, s, NEG)
    m = jnp.max(s, axis=-1, keepdims=True)
    p = jnp.exp(s - m)
    denom = jnp.sum(p, axis=-1, keepdims=True)
    o = jnp.einsum('brhnqk,brhnkd->brhnqd', p, vb.astype(jnp.float32)) / denom
    lse = (m + jnp.log(denom))[..., 0]
    o = o.reshape(B, dilation, H, Lp, hd)[..., :L, :].transpose(0, 3, 1, 2, 4).reshape(B, S, H, hd)
    lse = lse.reshape(B, dilation, H, Lp)[..., :L].transpose(0, 3, 1, 2).reshape(B, S, H)
    return o, lse


def mixer_ab(h, w_in, sp_w, sp_b, v_g, v_b, conv_w, conv_b, cn_g, cn_b, w_out):
    B, S, _ = h.shape
    z = h @ w_in
    za = jax.nn.gelu(z[..., :2 * A_WIDTH])
    u, v = za[..., :A_WIDTH], za[..., A_WIDTH:]
    v = layernorm(v, v_g, v_b)
    vc = v.reshape(B, S // CHUNK, CHUNK, A_GROUPS, A_GROUP_DIM)
    sv = jnp.einsum('gpq,bnqgc->bnpgc', sp_w, vc) + sp_b.T[:, :, None]
    ya = u * sv.reshape(B, S, A_WIDTH)
    zb = z[..., 2 * A_WIDTH:]
    g = zb[..., :B_WIDTH] * jax.nn.sigmoid(zb[..., B_WIDTH:])
    g = lax.conv_general_dilated(
        g, conv_w[:, None, :].astype(g.dtype), window_strides=(1,),
        padding=((CONV_WIDTH // 2, CONV_WIDTH // 2),),
        dimension_numbers=('NWC', 'WIO', 'NWC'), feature_group_count=B_WIDTH) + conv_b
    yb = jax.nn.silu(layernorm(g, cn_g, cn_b))
    return jnp.concatenate([ya, yb], axis=-1) @ w_out


def mixer_c(h, w_qkv, q_g, k_g, w_out, cos, sin):
    B, S, _ = h.shape
    qkv = (h @ w_qkv).reshape(B, S, 3, C_HEADS, HEAD_DIM)
    q = partial_rope(rmsnorm(qkv[:, :, 0], q_g), cos, sin)
    k = partial_rope(rmsnorm(qkv[:, :, 1], k_g), cos, sin)
    v = qkv[:, :, 2]
    outs, lses = [], []
    for window, dilation in DILATED_PATTERNS:
        o, l = dilated_band_attention(q, k, v, window, dilation)
        outs.append(o)
        lses.append(l)
    wts = jax.nn.softmax(jnp.stack(lses), axis=0)
    o = jnp.einsum('pbsh,pbshd->bshd', wts, jnp.stack(outs))
    return o.reshape(B, S, C_WIDTH).astype(h.dtype) @ w_out


def squared_relu_mlp(h, w1, w2):
    return jnp.square(jax.nn.relu(h @ w1)) @ w2


def _fwd_setup_inputs(seed: int = 0) -> dict:
    key = jax.random.key(seed)
    ks = jax.random.split(key, 20)

    def nrm(k, shape, scale):
        return jax.random.normal(k, shape, jnp.float32) * scale

    res = (2 * DEPTH) ** -0.5
    return {
        "x": nrm(ks[0], (BATCH, SEQ, D_MODEL), 1.0),
        "mix_norm_g": 1.0 + nrm(ks[1], (DEPTH, D_MODEL), 0.02),
        "mlp_norm_g": 1.0 + nrm(ks[2], (DEPTH, D_MODEL), 0.02),
        "mlp_w1": nrm(ks[3], (DEPTH, D_MODEL, D_FF), D_MODEL ** -0.5),
        "mlp_w2": nrm(ks[4], (DEPTH, D_FF, D_MODEL), D_FF ** -0.5 * res),
        "ab_w_in": nrm(ks[5], (N_EVEN, D_MODEL, AB_IN), D_MODEL ** -0.5),
        "a_spatial_w": nrm(ks[6], (N_EVEN, A_GROUPS, CHUNK, CHUNK), 0.5 * CHUNK ** -0.5),
        "a_spatial_b": 1.0 + nrm(ks[7], (N_EVEN, A_GROUPS, CHUNK), 0.02),
        "a_vnorm_g": 1.0 + nrm(ks[8], (N_EVEN, A_WIDTH), 0.02),
        "a_vnorm_b": nrm(ks[9], (N_EVEN, A_WIDTH), 0.02),
        "b_conv_w": nrm(ks[10], (N_EVEN, CONV_WIDTH, B_WIDTH), CONV_WIDTH ** -0.5),
        "b_conv_b": nrm(ks[11], (N_EVEN, B_WIDTH), 0.02),
        "b_norm_g": 1.0 + nrm(ks[12], (N_EVEN, B_WIDTH), 0.02),
        "b_norm_b": nrm(ks[13], (N_EVEN, B_WIDTH), 0.02),
        "ab_w_out": nrm(ks[14], (N_EVEN, AB_OUT, D_MODEL), AB_OUT ** -0.5 * res),
        "c_w_qkv": nrm(ks[15], (N_ODD, D_MODEL, 3 * C_WIDTH), D_MODEL ** -0.5),
        "c_q_norm_g": 1.0 + nrm(ks[16], (N_ODD, HEAD_DIM), 0.02),
        "c_k_norm_g": 1.0 + nrm(ks[17], (N_ODD, HEAD_DIM), 0.02),
        "c_w_out": nrm(ks[18], (N_ODD, C_WIDTH, D_MODEL), C_WIDTH ** -0.5 * res),
    }


def _fwd_reference(x, mix_norm_g, mlp_norm_g, mlp_w1, mlp_w2, ab_w_in, a_spatial_w, a_spatial_b,
              a_vnorm_g, a_vnorm_b, b_conv_w, b_conv_b, b_norm_g, b_norm_b, ab_w_out,
              c_w_qkv, c_q_norm_g, c_k_norm_g, c_w_out):
    cos, sin = rope_tables(x.shape[1])
    for layer in range(DEPTH):
        i = layer // 2
        h = rmsnorm(x, mix_norm_g[layer])
        if layer % 2 == 0:
            x = x + mixer_ab(h, ab_w_in[i], a_spatial_w[i], a_spatial_b[i], a_vnorm_g[i], a_vnorm_b[i],
                             b_conv_w[i], b_conv_b[i], b_norm_g[i], b_norm_b[i], ab_w_out[i])
        else:
            x = x + mixer_c(h, c_w_qkv[i], c_q_norm_g[i], c_k_norm_g[i], c_w_out[i], cos, sin)
        h = rmsnorm(x, mlp_norm_g[layer])
        x = x + squared_relu_mlp(h, mlp_w1[layer], mlp_w2[layer])
    return x


import jax as _jax
import jax.numpy as _jnp

TWIN_FORMAT = 'train_step'
FWD_PARAMS = ['x', 'mix_norm_g', 'mlp_norm_g', 'mlp_w1', 'mlp_w2', 'ab_w_in', 'a_spatial_w', 'a_spatial_b', 'a_vnorm_g', 'a_vnorm_b', 'b_conv_w', 'b_conv_b', 'b_norm_g', 'b_norm_b', 'ab_w_out', 'c_w_qkv', 'c_q_norm_g', 'c_k_norm_g', 'c_w_out']
TWIN_WEIGHTS = ['mix_norm_g', 'mlp_norm_g', 'mlp_w1', 'mlp_w2', 'ab_w_in', 'a_spatial_w', 'a_spatial_b', 'a_vnorm_g', 'a_vnorm_b', 'b_conv_w', 'b_conv_b', 'b_norm_g', 'b_norm_b', 'ab_w_out', 'c_w_qkv', 'c_q_norm_g', 'c_k_norm_g', 'c_w_out']
TWIN_DIFF_INPUT = 'x'
TWIN_INPUTS = ['x', 'mix_norm_g', 'mlp_norm_g', 'mlp_w1', 'mlp_w2', 'ab_w_in', 'a_spatial_w', 'a_spatial_b', 'a_vnorm_g', 'a_vnorm_b', 'b_conv_w', 'b_conv_b', 'b_norm_g', 'b_norm_b', 'ab_w_out', 'c_w_qkv', 'c_q_norm_g', 'c_k_norm_g', 'c_w_out', 'loss_target', 'm_mix_norm_g', 'm_mlp_norm_g', 'm_mlp_w1', 'm_mlp_w2', 'm_ab_w_in', 'm_a_spatial_w', 'm_a_spatial_b', 'm_a_vnorm_g', 'm_a_vnorm_b', 'm_b_conv_w', 'm_b_conv_b', 'm_b_norm_g', 'm_b_norm_b', 'm_ab_w_out', 'm_c_w_qkv', 'm_c_q_norm_g', 'm_c_k_norm_g', 'm_c_w_out', 'v_mix_norm_g', 'v_mlp_norm_g', 'v_mlp_w1', 'v_mlp_w2', 'v_ab_w_in', 'v_a_spatial_w', 'v_a_spatial_b', 'v_a_vnorm_g', 'v_a_vnorm_b', 'v_b_conv_w', 'v_b_conv_b', 'v_b_norm_g', 'v_b_norm_b', 'v_ab_w_out', 'v_c_w_qkv', 'v_c_q_norm_g', 'v_c_k_norm_g', 'v_c_w_out']
TWIN_OUTPUTS = ['loss', 'grad_x', 'grad_mix_norm_g', 'grad_mlp_norm_g', 'grad_mlp_w1', 'grad_mlp_w2', 'grad_ab_w_in', 'grad_a_spatial_w', 'grad_a_spatial_b', 'grad_a_vnorm_g', 'grad_a_vnorm_b', 'grad_b_conv_w', 'grad_b_conv_b', 'grad_b_norm_g', 'grad_b_norm_b', 'grad_ab_w_out', 'grad_c_w_qkv', 'grad_c_q_norm_g', 'grad_c_k_norm_g', 'grad_c_w_out', 'delta_mix_norm_g', 'delta_mlp_norm_g', 'delta_mlp_w1', 'delta_mlp_w2', 'delta_ab_w_in', 'delta_a_spatial_w', 'delta_a_spatial_b', 'delta_a_vnorm_g', 'delta_a_vnorm_b', 'delta_b_conv_w', 'delta_b_conv_b', 'delta_b_norm_g', 'delta_b_norm_b', 'delta_ab_w_out', 'delta_c_w_qkv', 'delta_c_q_norm_g', 'delta_c_k_norm_g', 'delta_c_w_out', 'new_m_mix_norm_g', 'new_m_mlp_norm_g', 'new_m_mlp_w1', 'new_m_mlp_w2', 'new_m_ab_w_in', 'new_m_a_spatial_w', 'new_m_a_spatial_b', 'new_m_a_vnorm_g', 'new_m_a_vnorm_b', 'new_m_b_conv_w', 'new_m_b_conv_b', 'new_m_b_norm_g', 'new_m_b_norm_b', 'new_m_ab_w_out', 'new_m_c_w_qkv', 'new_m_c_q_norm_g', 'new_m_c_k_norm_g', 'new_m_c_w_out', 'new_v_mix_norm_g', 'new_v_mlp_norm_g', 'new_v_mlp_w1', 'new_v_mlp_w2', 'new_v_ab_w_in', 'new_v_a_spatial_w', 'new_v_a_spatial_b', 'new_v_a_vnorm_g', 'new_v_a_vnorm_b', 'new_v_b_conv_w', 'new_v_b_conv_b', 'new_v_b_norm_g', 'new_v_b_norm_b', 'new_v_ab_w_out', 'new_v_c_w_qkv', 'new_v_c_q_norm_g', 'new_v_c_k_norm_g', 'new_v_c_w_out']
TWIN_LEAF_KINDS = {'loss': 'loss', 'grad_x': 'grad_x', 'grad_mix_norm_g': 'grad_w', 'grad_mlp_norm_g': 'grad_w', 'grad_mlp_w1': 'grad_w', 'grad_mlp_w2': 'grad_w', 'grad_ab_w_in': 'grad_w', 'grad_a_spatial_w': 'grad_w', 'grad_a_spatial_b': 'grad_w', 'grad_a_vnorm_g': 'grad_w', 'grad_a_vnorm_b': 'grad_w', 'grad_b_conv_w': 'grad_w', 'grad_b_conv_b': 'grad_w', 'grad_b_norm_g': 'grad_w', 'grad_b_norm_b': 'grad_w', 'grad_ab_w_out': 'grad_w', 'grad_c_w_qkv': 'grad_w', 'grad_c_q_norm_g': 'grad_w', 'grad_c_k_norm_g': 'grad_w', 'grad_c_w_out': 'grad_w', 'delta_mix_norm_g': 'delta_w', 'delta_mlp_norm_g': 'delta_w', 'delta_mlp_w1': 'delta_w', 'delta_mlp_w2': 'delta_w', 'delta_ab_w_in': 'delta_w', 'delta_a_spatial_w': 'delta_w', 'delta_a_spatial_b': 'delta_w', 'delta_a_vnorm_g': 'delta_w', 'delta_a_vnorm_b': 'delta_w', 'delta_b_conv_w': 'delta_w', 'delta_b_conv_b': 'delta_w', 'delta_b_norm_g': 'delta_w', 'delta_b_norm_b': 'delta_w', 'delta_ab_w_out': 'delta_w', 'delta_c_w_qkv': 'delta_w', 'delta_c_q_norm_g': 'delta_w', 'delta_c_k_norm_g': 'delta_w', 'delta_c_w_out': 'delta_w', 'new_m_mix_norm_g': 'new_m', 'new_m_mlp_norm_g': 'new_m', 'new_m_mlp_w1': 'new_m', 'new_m_mlp_w2': 'new_m', 'new_m_ab_w_in': 'new_m', 'new_m_a_spatial_w': 'new_m', 'new_m_a_spatial_b': 'new_m', 'new_m_a_vnorm_g': 'new_m', 'new_m_a_vnorm_b': 'new_m', 'new_m_b_conv_w': 'new_m', 'new_m_b_conv_b': 'new_m', 'new_m_b_norm_g': 'new_m', 'new_m_b_norm_b': 'new_m', 'new_m_ab_w_out': 'new_m', 'new_m_c_w_qkv': 'new_m', 'new_m_c_q_norm_g': 'new_m', 'new_m_c_k_norm_g': 'new_m', 'new_m_c_w_out': 'new_m', 'new_v_mix_norm_g': 'new_v', 'new_v_mlp_norm_g': 'new_v', 'new_v_mlp_w1': 'new_v', 'new_v_mlp_w2': 'new_v', 'new_v_ab_w_in': 'new_v', 'new_v_a_spatial_w': 'new_v', 'new_v_a_spatial_b': 'new_v', 'new_v_a_vnorm_g': 'new_v', 'new_v_a_vnorm_b': 'new_v', 'new_v_b_conv_w': 'new_v', 'new_v_b_conv_b': 'new_v', 'new_v_b_norm_g': 'new_v', 'new_v_b_norm_b': 'new_v', 'new_v_ab_w_out': 'new_v', 'new_v_c_w_qkv': 'new_v', 'new_v_c_q_norm_g': 'new_v', 'new_v_c_k_norm_g': 'new_v', 'new_v_c_w_out': 'new_v'}


def _forward(args):
    return _fwd_reference(*[args[k] for k in FWD_PARAMS])


def _output_shape():
    def fwd():
        inp = _fwd_setup_inputs(0)
        return _fwd_reference(*[inp[k] for k in FWD_PARAMS])
    out = _jax.eval_shape(fwd)
    return out.shape, out.dtype

N_MICROBATCH = 1
ADAM_LR = 0.001
ADAM_B1 = 0.9
ADAM_B2 = 0.999
ADAM_EPS = 1e-08
ADAM_WD = 0.01
ADAM_STEP = 10
PER_EXAMPLE_BATCH_AXIS = {'x': 0, 'loss_target': 0}
SHARED_INPUTS = []
_WEIGHT_DTYPES = {'mix_norm_g': _jnp.float32, 'mlp_norm_g': _jnp.float32, 'mlp_w1': _jnp.float32, 'mlp_w2': _jnp.float32, 'ab_w_in': _jnp.float32, 'a_spatial_w': _jnp.float32, 'a_spatial_b': _jnp.float32, 'a_vnorm_g': _jnp.float32, 'a_vnorm_b': _jnp.float32, 'b_conv_w': _jnp.float32, 'b_conv_b': _jnp.float32, 'b_norm_g': _jnp.float32, 'b_norm_b': _jnp.float32, 'ab_w_out': _jnp.float32, 'c_w_qkv': _jnp.float32, 'c_q_norm_g': _jnp.float32, 'c_k_norm_g': _jnp.float32, 'c_w_out': _jnp.float32}
MOMENT_SCALE = {'mix_norm_g': 2.438649e+00, 'mlp_norm_g': 2.392703e+01, 'mlp_w1': 1.079081e+00, 'mlp_w2': 1.522588e+01, 'ab_w_in': 6.666252e-01, 'a_spatial_w': 1.079845e-01, 'a_spatial_b': 1.707652e+00, 'a_vnorm_g': 8.530446e-01, 'a_vnorm_b': 1.597024e-01, 'b_conv_w': 1.040664e+00, 'b_conv_b': 1.084662e+01, 'b_norm_g': 5.721555e+00, 'b_norm_b': 6.333708e+00, 'ab_w_out': 8.234594e+00, 'c_w_qkv': 1.294629e+00, 'c_q_norm_g': 7.439054e-01, 'c_k_norm_g': 7.463130e-01, 'c_w_out': 6.385250e+00}


def _to_microbatches(a, axis):
    t = _jnp.moveaxis(a, axis, 0)
    t = t.reshape((N_MICROBATCH, t.shape[0] // N_MICROBATCH) + t.shape[1:])
    return _jnp.moveaxis(t, 1, axis + 1)


def setup_inputs(seed: int = 0) -> dict:
    inp = _fwd_setup_inputs(seed)
    key = _jax.random.fold_in(_jax.random.key(seed), 7919)
    shape, _ = _output_shape()
    out = dict(inp)
    out["loss_target"] = _jax.random.normal(_jax.random.fold_in(key, 0), shape, _jnp.float32)
    for i, name in enumerate(TWIN_WEIGHTS):
        w = inp[name].astype(_jnp.float32)
        if MOMENT_SCALE is None:
            s = _jnp.sqrt(_jnp.mean(_jnp.square(w)) + 1e-30)
        else:
            s = MOMENT_SCALE[name]
        km, kv = _jax.random.split(_jax.random.fold_in(key, i + 1))
        out[name] = w
        out["m_" + name] = s * _jax.random.normal(km, w.shape, _jnp.float32)
        out["v_" + name] = (s * s) * _jax.random.uniform(kv, w.shape, _jnp.float32, 0.5, 1.5)
    if N_MICROBATCH > 1:
        for name, axis in PER_EXAMPLE_BATCH_AXIS.items():
            out[name] = _to_microbatches(out[name], axis)
    return {'x': out['x'], 'mix_norm_g': out['mix_norm_g'], 'mlp_norm_g': out['mlp_norm_g'], 'mlp_w1': out['mlp_w1'], 'mlp_w2': out['mlp_w2'], 'ab_w_in': out['ab_w_in'], 'a_spatial_w': out['a_spatial_w'], 'a_spatial_b': out['a_spatial_b'], 'a_vnorm_g': out['a_vnorm_g'], 'a_vnorm_b': out['a_vnorm_b'], 'b_conv_w': out['b_conv_w'], 'b_conv_b': out['b_conv_b'], 'b_norm_g': out['b_norm_g'], 'b_norm_b': out['b_norm_b'], 'ab_w_out': out['ab_w_out'], 'c_w_qkv': out['c_w_qkv'], 'c_q_norm_g': out['c_q_norm_g'], 'c_k_norm_g': out['c_k_norm_g'], 'c_w_out': out['c_w_out'], 'loss_target': out['loss_target'], 'm_mix_norm_g': out['m_mix_norm_g'], 'm_mlp_norm_g': out['m_mlp_norm_g'], 'm_mlp_w1': out['m_mlp_w1'], 'm_mlp_w2': out['m_mlp_w2'], 'm_ab_w_in': out['m_ab_w_in'], 'm_a_spatial_w': out['m_a_spatial_w'], 'm_a_spatial_b': out['m_a_spatial_b'], 'm_a_vnorm_g': out['m_a_vnorm_g'], 'm_a_vnorm_b': out['m_a_vnorm_b'], 'm_b_conv_w': out['m_b_conv_w'], 'm_b_conv_b': out['m_b_conv_b'], 'm_b_norm_g': out['m_b_norm_g'], 'm_b_norm_b': out['m_b_norm_b'], 'm_ab_w_out': out['m_ab_w_out'], 'm_c_w_qkv': out['m_c_w_qkv'], 'm_c_q_norm_g': out['m_c_q_norm_g'], 'm_c_k_norm_g': out['m_c_k_norm_g'], 'm_c_w_out': out['m_c_w_out'], 'v_mix_norm_g': out['v_mix_norm_g'], 'v_mlp_norm_g': out['v_mlp_norm_g'], 'v_mlp_w1': out['v_mlp_w1'], 'v_mlp_w2': out['v_mlp_w2'], 'v_ab_w_in': out['v_ab_w_in'], 'v_a_spatial_w': out['v_a_spatial_w'], 'v_a_spatial_b': out['v_a_spatial_b'], 'v_a_vnorm_g': out['v_a_vnorm_g'], 'v_a_vnorm_b': out['v_a_vnorm_b'], 'v_b_conv_w': out['v_b_conv_w'], 'v_b_conv_b': out['v_b_conv_b'], 'v_b_norm_g': out['v_b_norm_g'], 'v_b_norm_b': out['v_b_norm_b'], 'v_ab_w_out': out['v_ab_w_out'], 'v_c_w_qkv': out['v_c_w_qkv'], 'v_c_q_norm_g': out['v_c_q_norm_g'], 'v_c_k_norm_g': out['v_c_k_norm_g'], 'v_c_w_out': out['v_c_w_out']}


def _loss(weights, diff, rest, loss_target):
    with _jax.named_scope("forward"):
        args = {**rest, TWIN_DIFF_INPUT: diff, **{k: w.astype(_WEIGHT_DTYPES[k]) for k, w in weights.items()}}
        y = _forward(args)
    with _jax.named_scope("loss_head"):
        err = _jnp.square(y.astype(_jnp.float32) - loss_target)
        return 0.5 * _jnp.sum(_jnp.mean(err, axis=-1)) if err.ndim else 0.5 * err


def _adamw(w, g, m, v):
    m = ADAM_B1 * m + (1.0 - ADAM_B1) * g
    v = ADAM_B2 * v + (1.0 - ADAM_B2) * _jnp.square(g)
    m_hat = m / (1.0 - ADAM_B1 ** ADAM_STEP)
    v_hat = v / (1.0 - ADAM_B2 ** ADAM_STEP)
    delta = -ADAM_LR * (m_hat / (_jnp.sqrt(v_hat) + ADAM_EPS) + ADAM_WD * w)
    return delta, m, v


def reference(x, mix_norm_g, mlp_norm_g, mlp_w1, mlp_w2, ab_w_in, a_spatial_w, a_spatial_b, a_vnorm_g, a_vnorm_b, b_conv_w, b_conv_b, b_norm_g, b_norm_b, ab_w_out, c_w_qkv, c_q_norm_g, c_k_norm_g, c_w_out, loss_target, m_mix_norm_g, m_mlp_norm_g, m_mlp_w1, m_mlp_w2, m_ab_w_in, m_a_spatial_w, m_a_spatial_b, m_a_vnorm_g, m_a_vnorm_b, m_b_conv_w, m_b_conv_b, m_b_norm_g, m_b_norm_b, m_ab_w_out, m_c_w_qkv, m_c_q_norm_g, m_c_k_norm_g, m_c_w_out, v_mix_norm_g, v_mlp_norm_g, v_mlp_w1, v_mlp_w2, v_ab_w_in, v_a_spatial_w, v_a_spatial_b, v_a_vnorm_g, v_a_vnorm_b, v_b_conv_w, v_b_conv_b, v_b_norm_g, v_b_norm_b, v_ab_w_out, v_c_w_qkv, v_c_q_norm_g, v_c_k_norm_g, v_c_w_out):
    given = dict(x=x, mix_norm_g=mix_norm_g, mlp_norm_g=mlp_norm_g, mlp_w1=mlp_w1, mlp_w2=mlp_w2, ab_w_in=ab_w_in, a_spatial_w=a_spatial_w, a_spatial_b=a_spatial_b, a_vnorm_g=a_vnorm_g, a_vnorm_b=a_vnorm_b, b_conv_w=b_conv_w, b_conv_b=b_conv_b, b_norm_g=b_norm_g, b_norm_b=b_norm_b, ab_w_out=ab_w_out, c_w_qkv=c_w_qkv, c_q_norm_g=c_q_norm_g, c_k_norm_g=c_k_norm_g, c_w_out=c_w_out, loss_target=loss_target, m_mix_norm_g=m_mix_norm_g, m_mlp_norm_g=m_mlp_norm_g, m_mlp_w1=m_mlp_w1, m_mlp_w2=m_mlp_w2, m_ab_w_in=m_ab_w_in, m_a_spatial_w=m_a_spatial_w, m_a_spatial_b=m_a_spatial_b, m_a_vnorm_g=m_a_vnorm_g, m_a_vnorm_b=m_a_vnorm_b, m_b_conv_w=m_b_conv_w, m_b_conv_b=m_b_conv_b, m_b_norm_g=m_b_norm_g, m_b_norm_b=m_b_norm_b, m_ab_w_out=m_ab_w_out, m_c_w_qkv=m_c_w_qkv, m_c_q_norm_g=m_c_q_norm_g, m_c_k_norm_g=m_c_k_norm_g, m_c_w_out=m_c_w_out, v_mix_norm_g=v_mix_norm_g, v_mlp_norm_g=v_mlp_norm_g, v_mlp_w1=v_mlp_w1, v_mlp_w2=v_mlp_w2, v_ab_w_in=v_ab_w_in, v_a_spatial_w=v_a_spatial_w, v_a_spatial_b=v_a_spatial_b, v_a_vnorm_g=v_a_vnorm_g, v_a_vnorm_b=v_a_vnorm_b, v_b_conv_w=v_b_conv_w, v_b_conv_b=v_b_conv_b, v_b_norm_g=v_b_norm_g, v_b_norm_b=v_b_norm_b, v_ab_w_out=v_ab_w_out, v_c_w_qkv=v_c_w_qkv, v_c_q_norm_g=v_c_q_norm_g, v_c_k_norm_g=v_c_k_norm_g, v_c_w_out=v_c_w_out)
    weights = {n: given[n] for n in TWIN_WEIGHTS}
    shared = {n: given[n] for n in SHARED_INPUTS}
    per_example = {n: given[n] for n in ['x']}
    grad_fn = _jax.value_and_grad(_loss, argnums=(0, 1))

    def one_microbatch(ex, loss_target):
        ex = dict(ex)
        diff = ex.pop(TWIN_DIFF_INPUT)
        return grad_fn(weights, diff, {**shared, **ex}, loss_target)

    if N_MICROBATCH == 1:
        loss, (grad_w, grad_x) = one_microbatch(per_example, given["loss_target"])
    else:
        def body(carry, xs):
            loss_sum, grad_sum = carry
            l_k, (gw_k, gx_k) = one_microbatch(xs[0], xs[1])
            with _jax.named_scope("update"):
                return (loss_sum + l_k, _jax.tree.map(_jnp.add, grad_sum, gw_k)), gx_k

        init = (_jnp.zeros((), _jnp.float32), _jax.tree.map(_jnp.zeros_like, weights))
        (loss, grad_w), grad_x = _jax.lax.scan(body, init, (per_example, given["loss_target"]))
    with _jax.named_scope("update"):
        delta_w, new_m, new_v = {}, {}, {}
        for n in TWIN_WEIGHTS:
            delta_w[n], new_m[n], new_v[n] = _adamw(weights[n], grad_w[n], given["m_" + n], given["v_" + n])
    return (loss, grad_x, *[grad_w[n] for n in TWIN_WEIGHTS], *[delta_w[n] for n in TWIN_WEIGHTS],
            *[new_m[n] for n in TWIN_WEIGHTS], *[new_v[n] for n in TWIN_WEIGHTS])
```

```python
import functools

import jax
import jax.numpy as jnp
from jax import lax
from jax.experimental import pallas as pl
from jax.experimental.pallas import tpu as pltpu

F32, BF16 = jnp.float32, jnp.bfloat16
MESH = pl.DeviceIdType.MESH
ANY = pl.BlockSpec(memory_space=pl.ANY)

VMEM_LIMIT_BYTES = 56 * 1024 * 1024
LANES = 128
ELEMENTWISE_ROWS = 256

EPS = 1e-6
NEG = -1e30
HEAD_DIM = 64
N_HEADS = 16
CHUNK = 128
A_GROUPS = 8
CONV_WIDTH = 31
CONV_HALO = 16
BAND = 64
PATTERN_DILATIONS = (1, 4, 16)
ROT_DIM = 16
ROPE_THETA = 500000.0
N_SHARDS = 4

ADAM_LR, ADAM_B1, ADAM_B2, ADAM_EPS, ADAM_WD, ADAM_STEP = 0.001, 0.9, 0.999, 1e-08, 0.01, 10


def _cp(*sem):
    return pltpu.CompilerParams(dimension_semantics=sem, vmem_limit_bytes=VMEM_LIMIT_BYTES)


def _tile(n, pref):
    t = min(n, pref)
    assert n % t == 0, (n, pref)
    return t


def _dot(a, b, ca, cb):
    return lax.dot_general(a, b, (((ca,), (cb,)), ((), ())), preferred_element_type=F32)


def _mm_ngroup(name, a, w, layer, *, nt, tm, out_dtypes, extras=(), epilogue=None):
    M, K = a.shape
    G, _, R, C = w.shape
    nw = R if nt else C
    assert K == (C if nt else R)
    tm = _tile(M, tm)
    n_ex = len(extras)

    def body(a_ref, w_ref, *rest):
        acc = _dot(a_ref[...].astype(BF16), w_ref[...], 1, 1 if nt else 0)
        res = epilogue(acc, *[e[...] for e in rest[:n_ex]]) if epilogue else (acc,)
        for o_ref, r in zip(rest[n_ex:], res):
            o_ref[...] = r.astype(o_ref.dtype)

    blk = pl.BlockSpec((tm, nw), lambda m, g: (m, g))
    return pl.pallas_call(
        body, name=name, grid=(M // tm, G),
        in_specs=[pl.BlockSpec((tm, K), lambda m, g: (m, 0)),
                  pl.BlockSpec((None, None, R, C), lambda m, g: (g, layer, 0, 0))] + [blk] * n_ex,
        out_specs=[blk] * len(out_dtypes),
        out_shape=[jax.ShapeDtypeStruct((M, G * nw), dt) for dt in out_dtypes],
        compiler_params=_cp("parallel", "parallel"),
    )(a, w, *extras)


def _mm_kgroup(name, a, w, layer, *, nt, tm, out_dtypes, extras=(), epilogue=None):
    G, _, R, C = w.shape
    kw, N = (C, R) if nt else (R, C)
    if a.ndim == 3:
        M = a.shape[1]
        assert a.shape[0] == G and a.shape[2] == kw
    else:
        M = a.shape[0]
        assert a.shape[1] == G * kw
    tm = _tile(M, tm)
    n_ex = len(extras)
    a_spec = (pl.BlockSpec((None, tm, kw), lambda m, g: (g, m, 0)) if a.ndim == 3
              else pl.BlockSpec((tm, kw), lambda m, g: (m, g)))

    def body(a_ref, w_ref, *rest):
        acc_ref = rest[-1]
        g = pl.program_id(1)
        part = _dot(a_ref[...].astype(BF16), w_ref[...], 1, 1 if nt else 0)

        @pl.when(g == 0)
        def _():
            acc_ref[...] = part

        @pl.when(g > 0)
        def _():
            acc_ref[...] += part

        @pl.when(g == G - 1)
        def _():
            acc = acc_ref[...]
            res = epilogue(acc, *[e[...] for e in rest[:n_ex]]) if epilogue else (acc,)
            for o_ref, r in zip(rest[n_ex:-1], res):
                o_ref[...] = r.astype(o_ref.dtype)

    blk = pl.BlockSpec((tm, N), lambda m, g: (m, 0))
    return pl.pallas_call(
        body, name=name, grid=(M // tm, G),
        in_specs=[a_spec, pl.BlockSpec((None, None, R, C), lambda m, g: (g, layer, 0, 0))] + [blk] * n_ex,
        out_specs=[blk] * len(out_dtypes),
        out_shape=[jax.ShapeDtypeStruct((M, N), dt) for dt in out_dtypes],
        scratch_shapes=[pltpu.VMEM((tm, N), F32)],
        compiler_params=_cp("parallel", "arbitrary"),
    )(a, w, *extras)


def _wgrad(name, a, b, layer, bufs, *, a_group, tm):
    gf, gb = bufs
    G, _, R, C = gf.shape
    M = a.shape[0]
    tm = _tile(M, tm)
    n_m = M // tm

    def body(a_ref, b_ref, gf_in, gb_in, gf_ref, gb_ref):
        m = pl.program_id(1)
        part = _dot(a_ref[...].astype(BF16), b_ref[...].astype(BF16), 0, 0)

        @pl.when(m == 0)
        def _():
            gf_ref[...] = part

        @pl.when(m > 0)
        def _():
            gf_ref[...] += part

        @pl.when(m == n_m - 1)
        def _():
            gb_ref[...] = gf_ref[...].astype(BF16)

    a_spec = pl.BlockSpec((tm, R), (lambda g, m: (m, g)) if a_group else (lambda g, m: (m, 0)))
    if b.ndim == 3:
        assert not a_group
        b_spec = pl.BlockSpec((None, tm, C), lambda g, m: (g, m, 0))
    else:
        b_spec = pl.BlockSpec((tm, C), (lambda g, m: (m, 0)) if a_group else (lambda g, m: (m, g)))
    o_spec = pl.BlockSpec((None, None, R, C), lambda g, m: (g, layer, 0, 0))
    return pl.pallas_call(
        body, name=name, grid=(G, n_m),
        in_specs=[a_spec, b_spec, ANY, ANY], out_specs=[o_spec, o_spec],
        out_shape=[jax.ShapeDtypeStruct(gf.shape, F32), jax.ShapeDtypeStruct(gb.shape, BF16)],
        input_output_aliases={2: 0, 3: 1},
        compiler_params=_cp("parallel", "arbitrary"),
    )(a, b, gf, gb)


def _rms_fwd(name, x, g3, layer):
    S, D = x.shape
    tm = _tile(S, 512)

    def body(x_ref, g_ref, h_ref):
        xv = x_ref[...]
        r = lax.rsqrt(jnp.mean(xv * xv, axis=-1, keepdims=True) + EPS)
        h_ref[...] = (xv * r * g_ref[...]).astype(BF16)

    row = pl.BlockSpec((tm, D), lambda m: (m, 0))
    return pl.pallas_call(
        body, name=name, grid=(S // tm,),
        in_specs=[row, pl.BlockSpec((None, 1, D), lambda m: (layer, 0, 0))], out_specs=row,
        out_shape=jax.ShapeDtypeStruct((S, D), BF16), compiler_params=_cp("parallel"),
    )(x, g3)


def _rms_bwd(name, x, g3, layer, dh, dres):
    S, D = x.shape
    tm = _tile(S, 512)

    def body(x_ref, g_ref, dh_ref, dres_ref, dx_ref, dg_ref):
        xv = x_ref[...]
        d = dh_ref[...].astype(F32)
        r = lax.rsqrt(jnp.mean(xv * xv, axis=-1, keepdims=True) + EPS)
        xhat = xv * r
        dxhat = d * g_ref[...]
        dx_ref[...] = dres_ref[...] + r * (dxhat - xhat * jnp.mean(dxhat * xhat, axis=-1, keepdims=True))

        @pl.when(pl.program_id(0) == 0)
        def _():
            dg_ref[...] = jnp.zeros_like(dg_ref)

        dg_ref[...] += jnp.sum(d * xhat, axis=0, keepdims=True)

    row = pl.BlockSpec((tm, D), lambda m: (m, 0))
    return pl.pallas_call(
        body, name=name, grid=(S // tm,),
        in_specs=[row, pl.BlockSpec((None, 1, D), lambda m: (layer, 0, 0)), row, row],
        out_specs=[row, pl.BlockSpec((1, D), lambda m: (0, 0))],
        out_shape=[jax.ShapeDtypeStruct((S, D), F32), jax.ShapeDtypeStruct((1, D), F32)],
        compiler_params=_cp("arbitrary"),
    )(x, g3, dh, dres)


def _loss_grad(y, target):
    S, D = y.shape
    tm = _tile(S, 512)

    def body(y_ref, t_ref, dy_ref, l_ref):
        e = y_ref[...] - t_ref[...]
        dy_ref[...] = e * (1.0 / D)

        @pl.when(pl.program_id(0) == 0)
        def _():
            l_ref[...] = jnp.zeros_like(l_ref)

        l_ref[...] += (0.5 / D) * jnp.sum(jnp.sum(e * e, axis=1, keepdims=True), axis=0, keepdims=True)

    row = pl.BlockSpec((tm, D), lambda m: (m, 0))
    return pl.pallas_call(
        body, name="loss_grad", grid=(S // tm,), in_specs=[row, row],
        out_specs=[row, pl.BlockSpec((1, LANES), lambda m: (0, 0))],
        out_shape=[jax.ShapeDtypeStruct((S, D), F32), jax.ShapeDtypeStruct((1, LANES), F32)],
        compiler_params=_cp("arbitrary"),
    )(y, target)


def _adamw(name, w, m, v, g_parts):
    shape = w.shape
    C = shape[-1]
    flat = lambda t: t.reshape(-1, C)
    rows = flat(w).shape[0]
    tr = _tile(rows, ELEMENTWISE_ROWS) if rows % ELEMENTWISE_ROWS == 0 else rows
    n_g = len(g_parts)

    def body(w_ref, m_ref, v_ref, *rest):
        g = rest[0][...]
        for p in rest[1:n_g]:
            g = g + p[...]
        g_ref, d_ref, nm_ref, nv_ref = rest[n_g:]
        m2 = ADAM_B1 * m_ref[...] + (1.0 - ADAM_B1) * g
        v2 = ADAM_B2 * v_ref[...] + (1.0 - ADAM_B2) * jnp.square(g)
        m_hat = m2 / (1.0 - ADAM_B1 ** ADAM_STEP)
        v_hat = v2 / (1.0 - ADAM_B2 ** ADAM_STEP)
        g_ref[...] = g
        d_ref[...] = -ADAM_LR * (m_hat / (jnp.sqrt(v_hat) + ADAM_EPS) + ADAM_WD * w_ref[...])
        nm_ref[...] = m2
        nv_ref[...] = v2

    blk = pl.BlockSpec((tr, C), lambda i: (i, 0))
    outs = pl.pallas_call(
        body, name=name, grid=(rows // tr,), in_specs=[blk] * (3 + n_g), out_specs=[blk] * 4,
        out_shape=[jax.ShapeDtypeStruct((rows, C), F32)] * 4, compiler_params=_cp("parallel"),
    )(flat(w), flat(m), flat(v), *[flat(p) for p in g_parts])
    return [o.reshape(shape) for o in outs]


def _sum4(name, own, recv):
    shape = own.shape
    C = shape[-1]
    own2 = own.reshape(-1, C)
    rows = own2.shape[0]
    recv3 = recv.reshape(3, rows, C)
    tr = _tile(rows, ELEMENTWISE_ROWS) if rows % ELEMENTWISE_ROWS == 0 else rows

    def body(o_ref, r_ref, out_ref):
        acc = o_ref[...]
        for k in range(3):
            acc = acc + r_ref[k].astype(F32)
        out_ref[...] = acc

    blk = pl.BlockSpec((tr, C), lambda i: (i, 0))
    out = pl.pallas_call(
        body, name=name, grid=(rows // tr,),
        in_specs=[blk, pl.BlockSpec((3, tr, C), lambda i: (0, i, 0))], out_specs=blk,
        out_shape=jax.ShapeDtypeStruct((rows, C), F32), compiler_params=_cp("parallel"),
    )(own2, recv3)
    return out.reshape(shape)


def _gelu(x):
    return x * (0.5 * (1.0 + jnp.tanh(0.7978845608028654 * (x + 0.044715 * (x * x * x)))))


def _layernorm(t, g, b):
    mu = jnp.mean(t, axis=-1, keepdims=True)
    var = jnp.mean(jnp.square(t - mu), axis=-1, keepdims=True)
    return (t - mu) * lax.rsqrt(var + EPS) * g + b


def _silu(x):
    return x * jax.nn.sigmoid(x)


def _a_value(zv, g, b):
    return _layernorm(_gelu(zv), g, b)


def _b_tail(gc, g, b):
    return _silu(_layernorm(gc, g, b))


def _first_head(shape):
    return lax.broadcasted_iota(jnp.int32, shape, len(shape) - 1) < HEAD_DIM


def _spatial_mix(spw_ref, vb, tm):
    first = _first_head((CHUNK, LANES))
    rows = []
    for n in range(tm // CHUNK):
        blocks = []
        for j in range(A_GROUPS // 2):
            vblk = vb[n * CHUNK:(n + 1) * CHUNK, j * LANES:(j + 1) * LANES]
            r0 = _dot(spw_ref[2 * j], vblk, 1, 0)
            r1 = _dot(spw_ref[2 * j + 1], vblk, 1, 0)
            blocks.append(jnp.where(first, r0, r1))
        rows.append(jnp.concatenate(blocks, axis=1))
    return jnp.concatenate(rows, axis=0) if len(rows) > 1 else rows[0]


def _ab_tail_fwd(name, z, gconv, spw, bias_full, vn_g, vn_b, cn_g, cn_b, layer):
    S = z.shape[0]
    AW = 512
    tm = _tile(S, 256)

    def body(zu_ref, zv_ref, gc_ref, spw_ref, bias_ref, vg_ref, vb_ref, cg_ref, cb_ref, cat_ref):
        u = _gelu(zu_ref[...])
        v = _a_value(zv_ref[...], vg_ref[...], vb_ref[...])
        sv = _spatial_mix(spw_ref, v.astype(BF16), tm) + jnp.tile(bias_ref[...], (tm // CHUNK, 1))
        cat_ref[:, :AW] = (u * sv).astype(BF16)
        cat_ref[:, AW:] = _b_tail(gc_ref[...], cg_ref[...], cb_ref[...]).astype(BF16)

    vec = pl.BlockSpec((None, 1, AW), lambda m: (layer, 0, 0))
    return pl.pallas_call(
        body, name=name, grid=(S // tm,),
        in_specs=[pl.BlockSpec((tm, AW), lambda m: (m, 0)), pl.BlockSpec((tm, AW), lambda m: (m, 1)),
                  pl.BlockSpec((tm, AW), lambda m: (m, 0)),
                  pl.BlockSpec((None, A_GROUPS, CHUNK, CHUNK), lambda m: (layer, 0, 0, 0)),
                  pl.BlockSpec((None, CHUNK, AW), lambda m: (layer, 0, 0)), vec, vec, vec, vec],
        out_specs=pl.BlockSpec((tm, 2 * AW), lambda m: (m, 0)),
        out_shape=jax.ShapeDtypeStruct((S, 2 * AW), BF16), compiler_params=_cp("parallel"),
    )(z, z, gconv, spw, bias_full, vn_g, vn_b, cn_g, cn_b)


def _ab_tail_bwd(name, z, gconv, dcat, spw, spw_t, bias_full, vn_g, vn_b, cn_g, cn_b, layer):
    S = z.shape[0]
    AW = 512
    tm = _tile(S, 256)
    n_chunks = tm // CHUNK

    def body(zu_ref, zv_ref, gc_ref, dcat_ref, spw_ref, spwt_ref, bias_ref, vg_ref, vb_ref, cg_ref, cb_ref,
             dz_ref, dgc_ref, dspw_ref, dbias_ref, dvg_ref, dvb_ref, dcg_ref, dcb_ref):
        @pl.when(pl.program_id(0) == 0)
        def _():
            for r in (dspw_ref, dbias_ref, dvg_ref, dvb_ref, dcg_ref, dcb_ref):
                r[...] = jnp.zeros_like(r)

        dya = dcat_ref[:, :AW]
        dyb = dcat_ref[:, AW:]
        u, u_vjp = jax.vjp(_gelu, zu_ref[...])
        v, v_vjp = jax.vjp(_a_value, zv_ref[...], vg_ref[...], vb_ref[...])
        vb16 = v.astype(BF16)
        sv = _spatial_mix(spw_ref, vb16, tm) + jnp.tile(bias_ref[...], (n_chunks, 1))
        (dzu,) = u_vjp(dya * sv)
        dsv = dya * u
        dsv16 = dsv.astype(BF16)
        dv = _spatial_mix(spwt_ref, dsv16, tm)
        dzv, dvg, dvb = v_vjp(dv)
        dz_ref[0] = dzu
        dz_ref[1] = dzv
        dvg_ref[...] += dvg
        dvb_ref[...] += dvb

        first = _first_head((CHUNK, LANES))
        zero = jnp.zeros((), BF16)
        dbias = jnp.zeros((CHUNK, AW), F32)
        for n in range(n_chunks):
            rows = slice(n * CHUNK, (n + 1) * CHUNK)
            dbias = dbias + dsv[rows]
            for j in range(A_GROUPS // 2):
                cols = slice(j * LANES, (j + 1) * LANES)
                dblk, vblk = dsv16[rows, cols], vb16[rows, cols]
                dspw_ref[2 * j] += _dot(jnp.where(first, dblk, zero), vblk, 1, 1)
                dspw_ref[2 * j + 1] += _dot(jnp.where(first, zero, dblk), vblk, 1, 1)
        dbias_ref[...] += dbias

        _, t_vjp = jax.vjp(_b_tail, gc_ref[...], cg_ref[...], cb_ref[...])
        dgc, dcg, dcb = t_vjp(dyb)
        dgc_ref[...] = dgc
        dcg_ref[...] += dcg
        dcb_ref[...] += dcb

    vec = pl.BlockSpec((None, 1, AW), lambda m: (layer, 0, 0))
    spw_spec = pl.BlockSpec((None, A_GROUPS, CHUNK, CHUNK), lambda m: (layer, 0, 0, 0))
    ovec = pl.BlockSpec((1, AW), lambda m: (0, 0))
    return pl.pallas_call(
        body, name=name, grid=(S // tm,),
        in_specs=[pl.BlockSpec((tm, AW), lambda m: (m, 0)), pl.BlockSpec((tm, AW), lambda m: (m, 1)),
                  pl.BlockSpec((tm, AW), lambda m: (m, 0)), pl.BlockSpec((tm, 2 * AW), lambda m: (m, 0)),
                  spw_spec, spw_spec, pl.BlockSpec((None, CHUNK, AW), lambda m: (layer, 0, 0)), vec, vec, vec, vec],
        out_specs=[pl.BlockSpec((2, tm, AW), lambda m: (0, m, 0)), pl.BlockSpec((tm, AW), lambda m: (m, 0)),
                   pl.BlockSpec((A_GROUPS, CHUNK, CHUNK), lambda m: (0, 0, 0)),
                   pl.BlockSpec((CHUNK, AW), lambda m: (0, 0)), ovec, ovec, ovec, ovec],
        out_shape=[jax.ShapeDtypeStruct((4, S, AW), F32), jax.ShapeDtypeStruct((S, AW), F32),
                   jax.ShapeDtypeStruct((A_GROUPS, CHUNK, CHUNK), F32), jax.ShapeDtypeStruct((CHUNK, AW), F32)]
                  + [jax.ShapeDtypeStruct((1, AW), F32)] * 4,
        compiler_params=_cp("arbitrary"),
    )(z, z, gconv, dcat, spw, spw_t, bias_full, vn_g, vn_b, cn_g, cn_b)


def _fold_bias(dbias_full):
    def body(d_ref, o_ref):
        d = d_ref[...]
        hi = d.astype(BF16)
        lo = (d - hi.astype(F32)).astype(BF16)
        r = lax.broadcasted_iota(jnp.int32, (512, LANES), 0)
        c = lax.broadcasted_iota(jnp.int32, (512, LANES), 1)
        fold = jnp.where(lax.shift_right_logical(r, 6) == c, 1.0, 0.0).astype(BF16)
        o_ref[...] = _dot(hi, fold, 1, 0) + _dot(lo, fold, 1, 0)

    return pl.pallas_call(body, name="fold_spatial_bias", out_shape=jax.ShapeDtypeStruct((CHUNK, LANES), F32))(dbias_full)


def _halo_specs(tm, n_halo_blocks, col):
    r = tm // CONV_HALO
    prev = pl.BlockSpec((CONV_HALO, LANES), lambda j, i: (jnp.maximum(i * r - 1, 0), col + j))
    cur = pl.BlockSpec((tm, LANES), lambda j, i: (i, col + j))
    nxt = pl.BlockSpec((CONV_HALO, LANES), lambda j, i: (jnp.minimum((i + 1) * r, n_halo_blocks - 1), col + j))
    return [prev, cur, nxt]


def _fill_halo(scr, prev, cur, nxt, tm, i, n_i):
    scr[0:CONV_HALO, :] = jnp.where(i > 0, prev, 0.0)
    scr[CONV_HALO:CONV_HALO + tm, :] = cur
    scr[CONV_HALO + tm:2 * CONV_HALO + tm, :] = jnp.where(i < n_i - 1, nxt, 0.0)


def _glu_conv_fwd(name, z, cw, cb3, layer):
    S = z.shape[0]
    tm = _tile(S, 512)
    n_i = S // tm
    pad = CONV_WIDTH // 2

    def body(vp, vc, vn, gp, gc, gn, w_ref, b_ref, out_ref, scr):
        i = pl.program_id(1)
        glu = lambda a, b: a[...] * jax.nn.sigmoid(b[...])
        _fill_halo(scr, glu(vp, gp), glu(vc, gc), glu(vn, gn), tm, i, n_i)
        acc = jnp.zeros((tm, LANES), F32)
        for j in range(CONV_WIDTH):
            acc = acc + w_ref[j:j + 1, :] * scr[pl.ds(CONV_HALO - pad + j, tm), :]
        out_ref[...] = acc + b_ref[...]

    return pl.pallas_call(
        body, name=name, grid=(4, n_i),
        in_specs=_halo_specs(tm, S // CONV_HALO, 8) + _halo_specs(tm, S // CONV_HALO, 12)
        + [pl.BlockSpec((None, None, CONV_WIDTH, LANES), lambda j, i: (j, layer, 0, 0)),
           pl.BlockSpec((None, 1, LANES), lambda j, i: (layer, 0, j))],
        out_specs=pl.BlockSpec((tm, LANES), lambda j, i: (i, j)),
        out_shape=jax.ShapeDtypeStruct((S, 4 * LANES), F32),
        scratch_shapes=[pltpu.VMEM((tm + 2 * CONV_HALO, LANES), F32)],
        compiler_params=_cp("parallel", "parallel"),
    )(z, z, z, z, z, z, cw, cb3)


def _glu_conv_bwd(name, z, dgconv, dz, cw, layer, bufs):
    S = z.shape[0]
    tm = _tile(S, 512)
    n_i = S // tm
    pad = CONV_WIDTH // 2
    gf, gb = bufs

    def body(vp, vc, vn, gp, gc, gn, dp, dc, dn, w_ref, dz_in, gf_in, gb_in,
             dz_ref, gf_ref, gb_ref, db_ref, g_scr, d_scr):
        i = pl.program_id(1)
        sig = jax.nn.sigmoid(gc[...])
        _fill_halo(g_scr, vp[...] * jax.nn.sigmoid(gp[...]), vc[...] * sig, vn[...] * jax.nn.sigmoid(gn[...]), tm, i, n_i)
        _fill_halo(d_scr, dp[...], dc[...], dn[...], tm, i, n_i)

        @pl.when(i == 0)
        def _():
            gf_ref[...] = jnp.zeros_like(gf_ref)
            db_ref[...] = jnp.zeros_like(db_ref)

        d_cur = dc[...]
        dglu = jnp.zeros((tm, LANES), F32)
        for j in range(CONV_WIDTH):
            dglu = dglu + w_ref[j:j + 1, :] * d_scr[pl.ds(CONV_HALO + pad - j, tm), :]
            gf_ref[j:j + 1, :] += jnp.sum(d_cur * g_scr[pl.ds(CONV_HALO - pad + j, tm), :], axis=0, keepdims=True)
        db_ref[...] += jnp.sum(d_cur, axis=0, keepdims=True)
        dz_ref[0] = dglu * sig
        dz_ref[1] = dglu * vc[...] * sig * (1.0 - sig)

        @pl.when(i == n_i - 1)
        def _():
            gb_ref[...] = gf_ref[...].astype(BF16)

    w_spec = pl.BlockSpec((None, None, CONV_WIDTH, LANES), lambda j, i: (j, layer, 0, 0))
    return pl.pallas_call(
        body, name=name, grid=(4, n_i),
        in_specs=_halo_specs(tm, S // CONV_HALO, 8) + _halo_specs(tm, S // CONV_HALO, 12)
        + _halo_specs(tm, S // CONV_HALO, 0) + [w_spec, ANY, ANY, ANY],
        out_specs=[pl.BlockSpec((2, tm, LANES), lambda j, i: (1, i, j)),
                   w_spec, w_spec, pl.BlockSpec((1, LANES), lambda j, i: (0, j))],
        out_shape=[jax.ShapeDtypeStruct(dz.shape, F32), jax.ShapeDtypeStruct(gf.shape, F32),
                   jax.ShapeDtypeStruct(gb.shape, BF16), jax.ShapeDtypeStruct((1, 4 * LANES), F32)],
        input_output_aliases={10: 0, 11: 1, 12: 2},
        scratch_shapes=[pltpu.VMEM((tm + 2 * CONV_HALO, LANES), F32)] * 2,
        compiler_params=_cp("parallel", "arbitrary"),
    )(z, z, z, z, z, z, dgconv, dgconv, dgconv, cw, dz, gf, gb)


def _seg_matrix(scale):
    r = lax.broadcasted_iota(jnp.int32, (LANES, LANES), 0)
    c = lax.broadcasted_iota(jnp.int32, (LANES, LANES), 1)
    return jnp.where(lax.shift_right_logical(r, 6) == lax.shift_right_logical(c, 6), scale, 0.0).astype(BF16)


def _seg_sum(x, seg):
    hi = x.astype(BF16)
    lo = (x - hi.astype(F32)).astype(BF16)
    return _dot(hi, seg, 1, 0) + _dot(lo, seg, 1, 0)


def _rope_tables(S):
    pos = jnp.arange(S, dtype=F32)
    inv_freq = ROPE_THETA ** (-jnp.arange(0, ROT_DIM, 2, dtype=F32) / ROT_DIM)
    ang = pos[:, None] * inv_freq[None, :]
    cos, sin = jnp.cos(ang), jnp.sin(ang)
    half = ROT_DIM // 2
    rest = HEAD_DIM - ROT_DIM
    one, zero = jnp.ones((S, rest), F32), jnp.zeros((S, rest), F32)
    zh = jnp.zeros((S, half), F32)
    c = jnp.concatenate([cos, cos, one], axis=1)
    sa = jnp.concatenate([-sin, zh, zero], axis=1)
    sb = jnp.concatenate([zh, sin, zero], axis=1)
    return [jnp.tile(t, (1, 2)) for t in (c, sa, sb)]


def _qk_fwd(name, qkv, gq, gk, tables):
    S = qkv.shape[0]
    W = N_HEADS * HEAD_DIM
    tm = _tile(S, 256)
    half = ROT_DIM // 2

    def body(q_ref, k_ref, v_ref, gq_ref, gk_ref, c_ref, sa_ref, sb_ref, qn_ref, kn_ref, vb_ref):
        seg = _seg_matrix(1.0 / HEAD_DIM)
        c, sa, sb = c_ref[...], sa_ref[...], sb_ref[...]
        for t_ref, g_ref, o_ref in ((q_ref, gq_ref, qn_ref), (k_ref, gk_ref, kn_ref)):
            for blk in range(W // LANES):
                cols = slice(blk * LANES, (blk + 1) * LANES)
                t = t_ref[:, cols]
                y = t * lax.rsqrt(_seg_sum(t * t, seg) + EPS) * g_ref[...]
                out = y * c + pltpu.roll(y, LANES - half, 1) * sa + pltpu.roll(y, half, 1) * sb
                o_ref[:, cols] = out.astype(BF16)
        vb_ref[...] = v_ref[...].astype(BF16)

    row = lambda k: pl.BlockSpec((tm, W), lambda m: (m, k))
    gain = pl.BlockSpec((1, LANES), lambda m: (0, 0))
    tab = pl.BlockSpec((tm, LANES), lambda m: (m, 0))
    return pl.pallas_call(
        body, name=name, grid=(S // tm,),
        in_specs=[row(0), row(1), row(2), gain, gain, tab, tab, tab], out_specs=[row(0)] * 3,
        out_shape=[jax.ShapeDtypeStruct((S, W), BF16)] * 3, compiler_params=_cp("parallel"),
    )(qkv, qkv, qkv, gq, gk, *tables)


def _qk_bwd(name, qkv, gq, gk, tables, dqs, dks, dvs):
    S = qkv.shape[0]
    W = N_HEADS * HEAD_DIM
    tm = _tile(S, 256)
    half = ROT_DIM // 2
    n_p = len(dqs)

    def body(q_ref, k_ref, gq_ref, gk_ref, c_ref, sa_ref, sb_ref, *rest):
        dq_refs, dk_refs, dv_refs = rest[:n_p], rest[n_p:2 * n_p], rest[2 * n_p:3 * n_p]
        dqkv_ref, dgq_ref, dgk_ref = rest[3 * n_p:]

        @pl.when(pl.program_id(0) == 0)
        def _():
            dgq_ref[...] = jnp.zeros_like(dgq_ref)
            dgk_ref[...] = jnp.zeros_like(dgk_ref)

        seg = _seg_matrix(1.0 / HEAD_DIM)
        r_i = lax.broadcasted_iota(jnp.int32, (LANES, LANES), 0)
        c_i = lax.broadcasted_iota(jnp.int32, (LANES, LANES), 1)
        same_dim = jnp.where((r_i & (HEAD_DIM - 1)) == (c_i & (HEAD_DIM - 1)), 1.0, 0.0).astype(BF16)
        c, sa, sb = c_ref[...], sa_ref[...], sb_ref[...]
        for idx, (t_ref, g_ref, d_refs, dg_ref) in enumerate(((q_ref, gq_ref, dq_refs, dgq_ref),
                                                              (k_ref, gk_ref, dk_refs, dgk_ref))):
            dg = jnp.zeros((1, LANES), F32)
            for blk in range(W // LANES):
                cols = slice(blk * LANES, (blk + 1) * LANES)
                dout = d_refs[0][:, cols]
                for r in d_refs[1:]:
                    dout = dout + r[:, cols]
                dy = dout * c + pltpu.roll(dout * sa, half, 1) + pltpu.roll(dout * sb, LANES - half, 1)
                t = t_ref[:, cols]
                r_ = lax.rsqrt(_seg_sum(t * t, seg) + EPS)
                xhat = t * r_
                dg = dg + jnp.sum(dy * xhat, axis=0, keepdims=True)
                dxhat = dy * g_ref[...]
                dt = r_ * (dxhat - xhat * _seg_sum(dxhat * xhat, seg))
                dqkv_ref[:, idx * W + blk * LANES: idx * W + (blk + 1) * LANES] = dt.astype(BF16)
            dg_ref[...] += _seg_sum(jnp.broadcast_to(dg, (8, LANES)), same_dim)[0:1]
        dv = dv_refs[0][...]
        for r in dv_refs[1:]:
            dv = dv + r[...]
        dqkv_ref[:, 2 * W:] = dv.astype(BF16)

    row = lambda k: pl.BlockSpec((tm, W), lambda m: (m, k))
    gain = pl.BlockSpec((1, LANES), lambda m: (0, 0))
    tab = pl.BlockSpec((tm, LANES), lambda m: (m, 0))
    return pl.pallas_call(
        body, name=name, grid=(S // tm,),
        in_specs=[row(0), row(1), gain, gain, tab, tab, tab] + [row(0)] * (3 * n_p),
        out_specs=[pl.BlockSpec((tm, 3 * W), lambda m: (m, 0)), gain, gain],
        out_shape=[jax.ShapeDtypeStruct((S, 3 * W), BF16), jax.ShapeDtypeStruct((1, LANES), F32),
                   jax.ShapeDtypeStruct((1, LANES), F32)],
        compiler_params=_cp("arbitrary"),
    )(qkv, qkv, gq, gk, *tables, *dqs, *dks, *dvs)


def _band_specs(bq, L):
    r = bq // BAND
    n_hb = L // BAND
    prev = pl.BlockSpec((BAND, LANES), lambda j, i: (jnp.maximum(i * r - 1, 0), j))
    cur = pl.BlockSpec((bq, LANES), lambda j, i: (i, j))
    nxt = pl.BlockSpec((BAND, LANES), lambda j, i: (jnp.minimum((i + 1) * r, n_hb - 1), j))
    return [prev, cur, nxt]


def _window(p, c, n):
    return jnp.concatenate([p[...], c[...], n[...]], axis=0)


def _attn_fwd(name, q, k, v, d):
    S, W = q.shape
    L, cols = S // d, d * W
    bq = min(2 * BAND, L)
    wk = bq + 2 * BAND

    def body(q_ref, kp, kc, kn, vp, vc, vn, o_ref, lse_ref):
        i = pl.program_id(1)
        kw, vw = _window(kp, kc, kn), _window(vp, vc, vn)
        qv = q_ref[...]
        first = _first_head((bq, LANES))
        qi = i * bq + lax.broadcasted_iota(jnp.int32, (bq, wk), 0)
        kj = i * bq - BAND + lax.broadcasted_iota(jnp.int32, (bq, wk), 1)
        valid = (jnp.abs(kj - qi) <= BAND) & (kj >= 0) & (kj < L)
        zero = jnp.zeros((), BF16)
        o_h, lse_h = [], []
        for hm in (first, jnp.logical_not(first)):
            s = jnp.where(valid, _dot(jnp.where(hm, qv, zero), kw, 1, 1) * (HEAD_DIM ** -0.5), NEG)
            mx = jnp.max(s, axis=-1, keepdims=True)
            p = jnp.exp(s - mx)
            den = jnp.sum(p, axis=-1, keepdims=True)
            o_h.append(_dot(p.astype(BF16), vw, 1, 0) / den)
            lse_h.append(mx + jnp.log(den))
        o_ref[...] = jnp.where(first, o_h[0], o_h[1])
        lse_ref[...] = jnp.where(first, lse_h[0], lse_h[1])

    view = lambda t: t.reshape(L, cols)
    o, lse = pl.pallas_call(
        body, name=name, grid=(cols // LANES, L // bq),
        in_specs=[_band_specs(bq, L)[1]] + _band_specs(bq, L) * 2,
        out_specs=[_band_specs(bq, L)[1]] * 2,
        out_shape=[jax.ShapeDtypeStruct((L, cols), F32)] * 2,
        compiler_params=_cp("parallel", "parallel"),
    )(view(q), view(k), view(k), view(k), view(v), view(v), view(v))
    return o.reshape(S, W), lse.reshape(S, W)


def _attn_merge(os, lses):
    S, W = os[0].shape
    tm = _tile(S, 256)
    n_p = len(os)

    def body(*refs):
        o_refs, l_refs = refs[:n_p], refs[n_p:2 * n_p]
        o_ref, lt_ref = refs[2 * n_p:]
        ls = [r[...] for r in l_refs]
        mx = functools.reduce(jnp.maximum, ls)
        es = [jnp.exp(l - mx) for l in ls]
        den = functools.reduce(lambda a, b: a + b, es)
        acc = es[0] * o_refs[0][...]
        for e, r in zip(es[1:], o_refs[1:]):
            acc = acc + e * r[...]
        o_ref[...] = (acc / den).astype(BF16)
        lt_ref[...] = mx + jnp.log(den)

    row = pl.BlockSpec((tm, W), lambda m: (m, 0))
    return pl.pallas_call(
        body, name="attn_merge", grid=(S // tm,), in_specs=[row] * (2 * n_p), out_specs=[row, row],
        out_shape=[jax.ShapeDtypeStruct((S, W), BF16), jax.ShapeDtypeStruct((S, W), F32)],
        compiler_params=_cp("parallel"),
    )(*os, *lses)


def _attn_delta(do, o):
    S, W = do.shape
    tm = _tile(S, 256)

    def body(do_ref, o_ref, dl_ref):
        seg = _seg_matrix(1.0)
        for blk in range(W // LANES):
            cols = slice(blk * LANES, (blk + 1) * LANES)
            dl_ref[:, cols] = _seg_sum(do_ref[:, cols] * o_ref[:, cols].astype(F32), seg)

    row = pl.BlockSpec((tm, W), lambda m: (m, 0))
    return pl.pallas_call(
        body, name="attn_delta", grid=(S // tm,), in_specs=[row, row], out_specs=row,
        out_shape=jax.ShapeDtypeStruct((S, W), F32), compiler_params=_cp("parallel"),
    )(do, o)


def _attn_bwd(name, q, k, v, do, lse, delta, d):
    S, W = q.shape
    L, cols = S // d, d * W
    bq = min(2 * BAND, L)
    wk = bq + 2 * BAND
    scale = HEAD_DIM ** -0.5

    def body(qp, qc, qn, kp, kc, kn, vp, vc, vn, dop, doc, don, lp, lc, ln, dp_, dc_, dn_, dq_ref, dk_ref, dv_ref):
        i = pl.program_id(1)
        kw, vw, qw = _window(kp, kc, kn), _window(vp, vc, vn), _window(qp, qc, qn)
        dow = _window(dop, doc, don).astype(BF16)
        lw, dlw = _window(lp, lc, ln), _window(dp_, dc_, dn_)
        q_c, k_c, v_c = qc[...], kc[...], vc[...]
        do_c = doc[...].astype(BF16)
        l_c, dl_c = lc[...], dc_[...]
        zero = jnp.zeros((), BF16)
        first = _first_head((bq, LANES))
        first_w = _first_head((wk, LANES))
        qi = i * bq + lax.broadcasted_iota(jnp.int32, (bq, wk), 0)
        kj = i * bq - BAND + lax.broadcasted_iota(jnp.int32, (bq, wk), 1)
        valid_q = (jnp.abs(kj - qi) <= BAND) & (kj >= 0) & (kj < L)
        qi2 = i * bq - BAND + lax.broadcasted_iota(jnp.int32, (wk, bq), 0)
        kj2 = i * bq + lax.broadcasted_iota(jnp.int32, (wk, bq), 1)
        valid_k = (jnp.abs(kj2 - qi2) <= BAND) & (qi2 >= 0) & (qi2 < L)
        dq_h, dk_h, dv_h = [], [], []
        for hh in range(2):
            hm, hm_w = (first, first_w) if hh == 0 else (jnp.logical_not(first), jnp.logical_not(first_w))
            lane0 = hh * HEAD_DIM
            s = jnp.where(valid_q, _dot(jnp.where(hm, q_c, zero), kw, 1, 1) * scale, NEG)
            p = jnp.exp(s - l_c[:, lane0:lane0 + 1])
            dp = _dot(jnp.where(hm, do_c, zero), vw, 1, 1)
            ds = p * (dp - dl_c[:, lane0:lane0 + 1]) * scale
            dq_h.append(_dot(ds.astype(BF16), kw, 1, 0))
            st = jnp.where(valid_k, _dot(jnp.where(hm_w, qw, zero), k_c, 1, 1) * scale, NEG)
            pt = jnp.exp(st - lw[:, lane0:lane0 + 1])
            dv_h.append(_dot(pt.astype(BF16), dow, 0, 0))
            dpt = _dot(jnp.where(hm_w, dow, zero), v_c, 1, 1)
            dst = pt * (dpt - dlw[:, lane0:lane0 + 1]) * scale
            dk_h.append(_dot(dst.astype(BF16), qw, 0, 0))
        dq_ref[...] = jnp.where(first, dq_h[0], dq_h[1])
        dk_ref[...] = jnp.where(first, dk_h[0], dk_h[1])
        dv_ref[...] = jnp.where(first, dv_h[0], dv_h[1])

    view = lambda t: t.reshape(L, cols)
    ops = []
    for t in (q, k, v, do, lse, delta):
        ops += [view(t)] * 3
    outs = pl.pallas_call(
        body, name=name, grid=(cols // LANES, L // bq),
        in_specs=_band_specs(bq, L) * 6, out_specs=[_band_specs(bq, L)[1]] * 3,
        out_shape=[jax.ShapeDtypeStruct((L, cols), F32)] * 3,
        compiler_params=_cp("parallel", "parallel"),
    )(*ops)
    return [o.reshape(S, W) for o in outs]


def _place():
    x, y, c = lax.axis_index("x"), lax.axis_index("y"), lax.axis_index("c")
    chips = [(1 - x, y), (x, 1 - y), (1 - x, 1 - y)]
    return x, y, c, chips


def _gather_shards(shards):
    n = len(shards)

    def body(*refs):
        ins, outs = refs[:n], refs[n:2 * n]
        send_sems, recv_sems, local_sems = refs[2 * n:]
        x, y, c, chips = _place()
        mine = 2 * x + y
        local = [pltpu.make_async_copy(ins[t], outs[t].at[mine], local_sems.at[t]) for t in range(n)]
        for cp in local:
            cp.start()
        sends = []
        for t in range(n):
            for k, (px, py) in enumerate(chips):
                cp = pltpu.make_async_remote_copy(
                    src_ref=ins[t], dst_ref=outs[t].at[mine], send_sem=send_sems.at[t, k], recv_sem=recv_sems.at[t, k],
                    device_id=(px, py, c), device_id_type=MESH)
                cp.start()
                sends.append(cp)
        for t in range(n):
            for k, (px, py) in enumerate(chips):
                pltpu.make_async_remote_copy(
                    src_ref=ins[t], dst_ref=outs[t].at[2 * px + py], send_sem=send_sems.at[t, k],
                    recv_sem=recv_sems.at[t, k], device_id=(px, py, c), device_id_type=MESH).wait_recv()
        for cp in sends:
            cp.wait_send()
        for cp in local:
            cp.wait()

    return pl.pallas_call(
        body, name="gather_weight_shards", in_specs=[ANY] * n, out_specs=[ANY] * n,
        out_shape=[jax.ShapeDtypeStruct((N_SHARDS,) + s.shape, s.dtype) for s in shards],
        scratch_shapes=[pltpu.SemaphoreType.DMA((n, 3)), pltpu.SemaphoreType.DMA((n, 3)), pltpu.SemaphoreType.DMA((n,))],
    )(*shards)


def _scatter_grads(bufs):
    n = len(bufs)

    def body(*refs):
        gfs, gbs = refs[:n], refs[n:2 * n]
        owns, recvs = refs[2 * n:3 * n], refs[3 * n:4 * n]
        send_sems, recv_sems, local_sems = refs[4 * n:]
        x, y, c, chips = _place()
        mine = 2 * x + y
        local = [pltpu.make_async_copy(gfs[t].at[mine], owns[t], local_sems.at[t]) for t in range(n)]
        for cp in local:
            cp.start()
        sends = []
        for t in range(n):
            for k, (px, py) in enumerate(chips):
                cp = pltpu.make_async_remote_copy(
                    src_ref=gbs[t].at[2 * px + py], dst_ref=recvs[t].at[k], send_sem=send_sems.at[t, k],
                    recv_sem=recv_sems.at[t, k], device_id=(px, py, c), device_id_type=MESH)
                cp.start()
                sends.append(cp)
        for cp in sends:
            cp.wait_recv()
        for cp in sends:
            cp.wait_send()
        for cp in local:
            cp.wait()

    shp = [b[0].shape[1:] for b in bufs]
    outs = pl.pallas_call(
        body, name="scatter_weight_grads", in_specs=[ANY] * (2 * n), out_specs=[ANY] * (2 * n),
        out_shape=[jax.ShapeDtypeStruct(s, F32) for s in shp] + [jax.ShapeDtypeStruct((3,) + s, BF16) for s in shp],
        scratch_shapes=[pltpu.SemaphoreType.DMA((n, 3)), pltpu.SemaphoreType.DMA((n, 3)), pltpu.SemaphoreType.DMA((n,))],
    )(*[b[0] for b in bufs], *[b[1] for b in bufs])
    return outs[:n], outs[n:]


def _swap_with_sibling(parts):
    n = len(parts)

    def body(*refs):
        ins, outs = refs[:n], refs[n:2 * n]
        send_sems, recv_sems = refs[2 * n:]
        x, y, c, _ = _place()
        cps = [pltpu.make_async_remote_copy(src_ref=ins[t], dst_ref=outs[t], send_sem=send_sems.at[t], recv_sem=recv_sems.at[t],
                                            device_id=(x, y, 1 - c), device_id_type=MESH) for t in range(n)]
        for cp in cps:
            cp.start()
        for cp in cps:
            cp.wait_recv()
        for cp in cps:
            cp.wait_send()

    return pl.pallas_call(
        body, name="swap_partial_grads", in_specs=[ANY] * n, out_specs=[ANY] * n,
        out_shape=[jax.ShapeDtypeStruct(p.shape, p.dtype) for p in parts],
        scratch_shapes=[pltpu.SemaphoreType.DMA((n,)), pltpu.SemaphoreType.DMA((n,))],
    )(*parts)


def _allreduce_small(v):
    rows = v.shape[0]

    def body(v_ref, out_ref, buf, send_sems, recv_sems):
        x, y, c, chips = _place()
        me, sibling = (x, y, c), (x, y, 1 - c)

        def slot(px, py, pc):
            return buf.at[4 * px + 2 * py + pc]

        def copy(k, block, to, src=None):
            return pltpu.make_async_remote_copy(
                src_ref=slot(*block) if src is None else src, dst_ref=slot(*block), send_sem=send_sems.at[k],
                recv_sem=recv_sems.at[k], device_id=to, device_id_type=MESH)

        slot(*me)[...] = v_ref[...]
        first = [copy(0, me, sibling, src=v_ref)] + [copy(1 + j, me, (*chip, c), src=v_ref) for j, chip in enumerate(chips)]
        for cp in first:
            cp.start()
        passed = [copy(4 + j, (*chip, c), sibling) for j, chip in enumerate(chips)]
        for j, chip in enumerate(chips):
            copy(1 + j, (*chip, c), me).wait_recv()
            passed[j].start()
        copy(0, sibling, me).wait_recv()
        for j, chip in enumerate(chips):
            copy(4 + j, (*chip, 1 - c), me).wait_recv()
        for cp in first + passed:
            cp.wait_send()
        acc = buf[0]
        for k in range(1, 8):
            acc = acc + buf[k]
        out_ref[...] = acc

    return pl.pallas_call(
        body, name="allreduce_small_grads",
        in_specs=[pl.BlockSpec(memory_space=pltpu.VMEM)], out_specs=pl.BlockSpec(memory_space=pltpu.VMEM),
        out_shape=jax.ShapeDtypeStruct((rows, LANES), F32),
        scratch_shapes=[pltpu.VMEM((8, rows, LANES), F32), pltpu.SemaphoreType.DMA((7,)), pltpu.SemaphoreType.DMA((7,))],
        compiler_params=pltpu.CompilerParams(vmem_limit_bytes=VMEM_LIMIT_BYTES),
    )(v)


MM_TM = 1024


def _sq_relu_epilogue(acc):
    r = jnp.maximum(acc, 0.0)
    return acc, r * r


def _add_epilogue(acc, x):
    return (acc + x,)


def _sq_relu_grad_epilogue(acc, a):
    return (acc * (2.0 * jnp.maximum(a.astype(F32), 0.0)),)


def _local_step(x, target, p, wg):
    S, D = x.shape
    depth = p["mix_norm_g"].shape[0]
    n_even = (depth + 1) // 2
    mix_g3 = p["mix_norm_g"].reshape(depth, 1, D)
    mlp_g3 = p["mlp_norm_g"].reshape(depth, 1, D)
    vec3 = lambda t: t.reshape(t.shape[0], 1, t.shape[1])
    spw16 = p["a_spatial_w"].astype(BF16)
    spw16_t = jnp.swapaxes(spw16, 2, 3)
    bias_full = jnp.repeat(jnp.swapaxes(p["a_spatial_b"], 1, 2), HEAD_DIM, axis=2)
    vn_g, vn_b, cn_g, cn_b, cb3 = (vec3(p[k]) for k in ("a_vnorm_g", "a_vnorm_b", "b_norm_g", "b_norm_b", "b_conv_b"))
    tables = _rope_tables(S)
    gq = jnp.tile(p["c_q_norm_g"], (1, 2))
    gk = jnp.tile(p["c_k_norm_g"], (1, 2))

    saved = []
    for layer in range(depth):
        i = layer // 2
        rec = {"x_mix": x}
        h = _rms_fwd(f"mix_norm_{layer}", x, mix_g3, layer)
        rec["h_mix"] = h
        if layer % 2 == 0:
            (z,) = _mm_ngroup(f"ab_in_{layer}", h, wg["ab_w_in"], i, nt=False, tm=MM_TM, out_dtypes=[F32])
            gconv = _glu_conv_fwd(f"glu_conv_{layer}", z, wg["b_conv_w"], cb3, i)
            cat = _ab_tail_fwd(f"ab_tail_{layer}", z, gconv, spw16, bias_full, vn_g, vn_b, cn_g, cn_b, i)
            (x,) = _mm_kgroup(f"ab_out_{layer}", cat, wg["ab_w_out"], i, nt=False, tm=MM_TM, out_dtypes=[F32],
                              extras=(x,), epilogue=_add_epilogue)
            rec.update(z=z, gconv=gconv, cat=cat)
        else:
            (qkv,) = _mm_ngroup(f"c_qkv_{layer}", h, wg["c_w_qkv"], i, nt=False, tm=MM_TM, out_dtypes=[F32])
            qn, kn, vb = _qk_fwd(f"qk_norm_rope_{layer}", qkv, gq[i:i + 1], gk[i:i + 1], tables)
            os, lses = zip(*[_attn_fwd(f"attn_d{d}_{layer}", qn, kn, vb, d) for d in PATTERN_DILATIONS])
            o, lse = _attn_merge(os, lses)
            (x,) = _mm_kgroup(f"c_out_{layer}", o, wg["c_w_out"], i, nt=False, tm=MM_TM, out_dtypes=[F32],
                              extras=(x,), epilogue=_add_epilogue)
            rec.update(qkv=qkv, qn=qn, kn=kn, vb=vb, o=o, lse=lse)
        rec["x_mlp"] = x
        h = _rms_fwd(f"mlp_norm_{layer}", x, mlp_g3, layer)
        a, hsq = _mm_ngroup(f"mlp_up_{layer}", h, wg["mlp_w1"], layer, nt=False, tm=MM_TM, out_dtypes=[BF16, BF16],
                            epilogue=_sq_relu_epilogue)
        (x,) = _mm_kgroup(f"mlp_down_{layer}", hsq, wg["mlp_w2"], layer, nt=False, tm=MM_TM, out_dtypes=[F32],
                          extras=(x,), epilogue=_add_epilogue)
        rec.update(h_mlp=h, a=a, hsq=hsq)
        saved.append(rec)

    dx, loss_row = _loss_grad(x, target)

    bufs = {k: (lax.empty(w.shape, F32), lax.empty(w.shape, BF16)) for k, w in wg.items()}
    small = {k: [None] * v.shape[0] for k, v in p.items()}
    for layer in reversed(range(depth)):
        i = layer // 2
        rec = saved[layer]
        (da,) = _mm_ngroup(f"mlp_down_dgrad_{layer}", dx, wg["mlp_w2"], layer, nt=True, tm=MM_TM, out_dtypes=[BF16],
                           extras=(rec["a"],), epilogue=_sq_relu_grad_epilogue)
        bufs["mlp_w2"] = _wgrad(f"mlp_down_wgrad_{layer}", rec["hsq"], dx, layer, bufs["mlp_w2"], a_group=True, tm=MM_TM)
        bufs["mlp_w1"] = _wgrad(f"mlp_up_wgrad_{layer}", rec["h_mlp"], da, layer, bufs["mlp_w1"], a_group=False, tm=MM_TM)
        (dh,) = _mm_kgroup(f"mlp_up_dgrad_{layer}", da, wg["mlp_w1"], layer, nt=True, tm=MM_TM, out_dtypes=[F32])
        dx, small["mlp_norm_g"][layer] = _rms_bwd(f"mlp_norm_bwd_{layer}", rec["x_mlp"], mlp_g3, layer, dh, dx)
        if layer % 2 == 0:
            (dcat,) = _mm_ngroup(f"ab_out_dgrad_{layer}", dx, wg["ab_w_out"], i, nt=True, tm=MM_TM, out_dtypes=[F32])
            bufs["ab_w_out"] = _wgrad(f"ab_out_wgrad_{layer}", rec["cat"], dx, i, bufs["ab_w_out"], a_group=True, tm=MM_TM)
            dz, dgconv, dspw, dbias, dvg, dvb, dcg, dcb = _ab_tail_bwd(
                f"ab_tail_bwd_{layer}", rec["z"], rec["gconv"], dcat, spw16, spw16_t, bias_full, vn_g, vn_b, cn_g, cn_b, i)
            dz, gf, gb, dcbias = _glu_conv_bwd(f"glu_conv_bwd_{layer}", rec["z"], dgconv, dz, wg["b_conv_w"], i, bufs["b_conv_w"])
            bufs["b_conv_w"] = (gf, gb)
            small["a_spatial_w"][i] = dspw
            small["a_spatial_b"][i] = _fold_bias(dbias)[:, :A_GROUPS].T
            for k, val in (("a_vnorm_g", dvg), ("a_vnorm_b", dvb), ("b_norm_g", dcg), ("b_norm_b", dcb), ("b_conv_b", dcbias)):
                small[k][i] = val
            bufs["ab_w_in"] = _wgrad(f"ab_in_wgrad_{layer}", rec["h_mix"], dz, i, bufs["ab_w_in"], a_group=False, tm=MM_TM)
            (dh,) = _mm_kgroup(f"ab_in_dgrad_{layer}", dz, wg["ab_w_in"], i, nt=True, tm=MM_TM, out_dtypes=[F32])
        else:
            (do,) = _mm_ngroup(f"c_out_dgrad_{layer}", dx, wg["c_w_out"], i, nt=True, tm=MM_TM, out_dtypes=[F32])
            bufs["c_w_out"] = _wgrad(f"c_out_wgrad_{layer}", rec["o"], dx, i, bufs["c_w_out"], a_group=True, tm=MM_TM)
            delta = _attn_delta(do, rec["o"])
            dqs, dks, dvs = zip(*[_attn_bwd(f"attn_bwd_d{d}_{layer}", rec["qn"], rec["kn"], rec["vb"], do, rec["lse"], delta, d)
                                  for d in PATTERN_DILATIONS])
            dqkv, dgq, dgk = _qk_bwd(f"qk_norm_rope_bwd_{layer}", rec["qkv"], gq[i:i + 1], gk[i:i + 1], tables, dqs, dks, dvs)
            small["c_q_norm_g"][i] = dgq[:, :HEAD_DIM]
            small["c_k_norm_g"][i] = dgk[:, :HEAD_DIM]
            bufs["c_w_qkv"] = _wgrad(f"c_qkv_wgrad_{layer}", rec["h_mix"], dqkv, i, bufs["c_w_qkv"], a_group=False, tm=MM_TM)
            (dh,) = _mm_kgroup(f"c_qkv_dgrad_{layer}", dqkv, wg["c_w_qkv"], i, nt=True, tm=MM_TM, out_dtypes=[F32])
        dx, small["mix_norm_g"][layer] = _rms_bwd(f"mix_norm_bwd_{layer}", rec["x_mix"], mix_g3, layer, dh, dx)

    small = {k: jnp.stack([g.reshape(p[k].shape[1:]) for g in v]) for k, v in small.items()}
    return loss_row, dx, small, bufs


SHARDED = ("mlp_w1", "mlp_w2", "ab_w_in", "b_conv_w", "ab_w_out", "c_w_qkv", "c_w_out")
SMALL = ("mix_norm_g", "mlp_norm_g", "a_spatial_w", "a_spatial_b", "a_vnorm_g", "a_vnorm_b", "b_conv_b", "b_norm_g",
         "b_norm_b", "c_q_norm_g", "c_k_norm_g")
WEIGHTS = ("mix_norm_g", "mlp_norm_g", "mlp_w1", "mlp_w2", "ab_w_in", "a_spatial_w", "a_spatial_b", "a_vnorm_g",
           "a_vnorm_b", "b_conv_w", "b_conv_b", "b_norm_g", "b_norm_b", "ab_w_out", "c_w_qkv", "c_q_norm_g",
           "c_k_norm_g", "c_w_out")


def _pack(parts):
    flat = jnp.concatenate([parts[k].reshape(-1) for k in SMALL])
    rows = -(-flat.shape[0] // (256 * LANES)) * 256
    return jnp.pad(flat, (0, rows * LANES - flat.shape[0])).reshape(rows, LANES)


def _unpack(packed, like):
    flat = packed.reshape(-1)
    out, off = {}, 0
    for k in SMALL:
        n = like[k].size
        out[k] = flat[off:off + n].reshape(like[k].shape)
        off += n
    return out


def kernel(x, mix_norm_g, mlp_norm_g, mlp_w1, mlp_w2, ab_w_in, a_spatial_w, a_spatial_b, a_vnorm_g, a_vnorm_b, b_conv_w, b_conv_b, b_norm_g, b_norm_b, ab_w_out, c_w_qkv, c_q_norm_g, c_k_norm_g, c_w_out, loss_target, m_mix_norm_g, m_mlp_norm_g, m_mlp_w1, m_mlp_w2, m_ab_w_in, m_a_spatial_w, m_a_spatial_b, m_a_vnorm_g, m_a_vnorm_b, m_b_conv_w, m_b_conv_b, m_b_norm_g, m_b_norm_b, m_ab_w_out, m_c_w_qkv, m_c_q_norm_g, m_c_k_norm_g, m_c_w_out, v_mix_norm_g, v_mlp_norm_g, v_mlp_w1, v_mlp_w2, v_ab_w_in, v_a_spatial_w, v_a_spatial_b, v_a_vnorm_g, v_a_vnorm_b, v_b_conv_w, v_b_conv_b, v_b_norm_g, v_b_norm_b, v_ab_w_out, v_c_w_qkv, v_c_q_norm_g, v_c_k_norm_g, v_c_w_out):
    w = dict(mix_norm_g=mix_norm_g, mlp_norm_g=mlp_norm_g, mlp_w1=mlp_w1, mlp_w2=mlp_w2, ab_w_in=ab_w_in,
             a_spatial_w=a_spatial_w, a_spatial_b=a_spatial_b, a_vnorm_g=a_vnorm_g, a_vnorm_b=a_vnorm_b,
             b_conv_w=b_conv_w, b_conv_b=b_conv_b, b_norm_g=b_norm_g, b_norm_b=b_norm_b, ab_w_out=ab_w_out,
             c_w_qkv=c_w_qkv, c_q_norm_g=c_q_norm_g, c_k_norm_g=c_k_norm_g, c_w_out=c_w_out)
    m = dict(mix_norm_g=m_mix_norm_g, mlp_norm_g=m_mlp_norm_g, mlp_w1=m_mlp_w1, mlp_w2=m_mlp_w2, ab_w_in=m_ab_w_in,
             a_spatial_w=m_a_spatial_w, a_spatial_b=m_a_spatial_b, a_vnorm_g=m_a_vnorm_g, a_vnorm_b=m_a_vnorm_b,
             b_conv_w=m_b_conv_w, b_conv_b=m_b_conv_b, b_norm_g=m_b_norm_g, b_norm_b=m_b_norm_b, ab_w_out=m_ab_w_out,
             c_w_qkv=m_c_w_qkv, c_q_norm_g=m_c_q_norm_g, c_k_norm_g=m_c_k_norm_g, c_w_out=m_c_w_out)
    v = dict(mix_norm_g=v_mix_norm_g, mlp_norm_g=v_mlp_norm_g, mlp_w1=v_mlp_w1, mlp_w2=v_mlp_w2, ab_w_in=v_ab_w_in,
             a_spatial_w=v_a_spatial_w, a_spatial_b=v_a_spatial_b, a_vnorm_g=v_a_vnorm_g, a_vnorm_b=v_a_vnorm_b,
             b_conv_w=v_b_conv_w, b_conv_b=v_b_conv_b, b_norm_g=v_b_norm_g, b_norm_b=v_b_norm_b, ab_w_out=v_ab_w_out,
             c_w_qkv=v_c_w_qkv, c_q_norm_g=v_c_q_norm_g, c_k_norm_g=v_c_k_norm_g, c_w_out=v_c_w_out)

    S, D = x.shape[1], x.shape[2]
    shards = [w[k] if k == "b_conv_w" else w[k].astype(BF16) for k in SHARDED]
    wg = dict(zip(SHARDED, _gather_shards(shards)))
    small_params = {k: w[k] for k in SMALL}
    loss_row, dx, small_grads, bufs = _local_step(x.reshape(S, D), loss_target.reshape(S, D), small_params, wg)

    loss = lax.psum(loss_row[0, 0], ("x", "y", "c"))

    owns, recvs = _scatter_grads([bufs[k] for k in SHARDED])
    partial = [_sum4(f"sum_chips_{k}", o, r) for k, o, r in zip(SHARDED, owns, recvs)]
    other = _swap_with_sibling(partial)
    grads, deltas, new_m, new_v = {}, {}, {}, {}
    for k, mine, theirs in zip(SHARDED, partial, other):
        grads[k], deltas[k], new_m[k], new_v[k] = _adamw(f"adamw_{k}", w[k], m[k], v[k], [mine, theirs])

    g_small = _allreduce_small(_pack(small_grads))
    outs = _adamw("adamw_small", _pack(small_params), _pack({k: m[k] for k in SMALL}), _pack({k: v[k] for k in SMALL}), [g_small])
    for d_, packed in zip((grads, deltas, new_m, new_v), outs):
        d_.update(_unpack(packed, small_params))

    return (loss, dx.reshape(1, S, D), *[grads[k] for k in WEIGHTS], *[deltas[k] for k in WEIGHTS],
            *[new_m[k] for k in WEIGHTS], *[new_v[k] for k in WEIGHTS])
```

```python
import functools

import jax
import jax.numpy as jnp
from jax import lax
from jax.experimental import pallas as pl
from jax.experimental.pallas import tpu as pltpu

F32, BF16 = jnp.float32, jnp.bfloat16
MESH = pl.DeviceIdType.MESH
ANY = pl.BlockSpec(memory_space=pl.ANY)

VMEM_LIMIT_BYTES = 56 * 1024 * 1024
LANES = 128
ELEMENTWISE_ROWS = 256

EPS = 1e-6
NEG = -1e30
HEAD_DIM = 64
N_HEADS = 16
CHUNK = 128
A_GROUPS = 8
CONV_WIDTH = 31
CONV_HALO = 16
BAND = 64
PATTERN_DILATIONS = (1, 4, 16)
ROT_DIM = 16
ROPE_THETA = 500000.0
N_SHARDS = 4

ADAM_LR, ADAM_B1, ADAM_B2, ADAM_EPS, ADAM_WD, ADAM_STEP = 0.001, 0.9, 0.999, 1e-08, 0.01, 10


def _cp(*sem):
    return pltpu.CompilerParams(dimension_semantics=sem, vmem_limit_bytes=VMEM_LIMIT_BYTES)


def _tile(n, pref):
    t = min(n, pref)
    assert n % t == 0, (n, pref)
    return t


def _dot(a, b, ca, cb):
    return lax.dot_general(a, b, (((ca,), (cb,)), ((), ())), preferred_element_type=F32)


def _mm_ngroup(name, a, w, layer, *, nt, tm, out_dtypes, extras=(), epilogue=None):
    M, K = a.shape
    G, _, R, C = w.shape
    nw = R if nt else C
    assert K == (C if nt else R)
    tm = _tile(M, tm)
    n_ex = len(extras)

    def body(a_ref, w_ref, *rest):
        acc = _dot(a_ref[...].astype(BF16), w_ref[...], 1, 1 if nt else 0)
        res = epilogue(acc, *[e[...] for e in rest[:n_ex]]) if epilogue else (acc,)
        for o_ref, r in zip(rest[n_ex:], res):
            o_ref[...] = r.astype(o_ref.dtype)

    blk = pl.BlockSpec((tm, nw), lambda m, g: (m, g))
    return pl.pallas_call(
        body, name=name, grid=(M // tm, G),
        in_specs=[pl.BlockSpec((tm, K), lambda m, g: (m, 0)),
                  pl.BlockSpec((None, None, R, C), lambda m, g: (g, layer, 0, 0))] + [blk] * n_ex,
        out_specs=[blk] * len(out_dtypes),
        out_shape=[jax.ShapeDtypeStruct((M, G * nw), dt) for dt in out_dtypes],
        compiler_params=_cp("parallel", "parallel"),
    )(a, w, *extras)


def _mm_kgroup(name, a, w, layer, *, nt, tm, out_dtypes, extras=(), epilogue=None):
    G, _, R, C = w.shape
    kw, N = (C, R) if nt else (R, C)
    if a.ndim == 3:
        M = a.shape[1]
        assert a.shape[0] == G and a.shape[2] == kw
    else:
        M = a.shape[0]
        assert a.shape[1] == G * kw
    tm = _tile(M, tm)
    n_ex = len(extras)
    a_spec = (pl.BlockSpec((None, tm, kw), lambda m, g: (g, m, 0)) if a.ndim == 3
              else pl.BlockSpec((tm, kw), lambda m, g: (m, g)))

    def body(a_ref, w_ref, *rest):
        acc_ref = rest[-1]
        g = pl.program_id(1)
        part = _dot(a_ref[...].astype(BF16), w_ref[...], 1, 1 if nt else 0)

        @pl.when(g == 0)
        def _():
            acc_ref[...] = part

        @pl.when(g > 0)
        def _():
            acc_ref[...] += part

        @pl.when(g == G - 1)
        def _():
            acc = acc_ref[...]
            res = epilogue(acc, *[e[...] for e in rest[:n_ex]]) if epilogue else (acc,)
            for o_ref, r in zip(rest[n_ex:-1], res):
                o_ref[...] = r.astype(o_ref.dtype)

    blk = pl.BlockSpec((tm, N), lambda m, g: (m, 0))
    return pl.pallas_call(
        body, name=name, grid=(M // tm, G),
        in_specs=[a_spec, pl.BlockSpec((None, None, R, C), lambda m, g: (g, layer, 0, 0))] + [blk] * n_ex,
        out_specs=[blk] * len(out_dtypes),
        out_shape=[jax.ShapeDtypeStruct((M, N), dt) for dt in out_dtypes],
        scratch_shapes=[pltpu.VMEM((tm, N), F32)],
        compiler_params=_cp("parallel", "arbitrary"),
    )(a, w, *extras)


def _wgrad(name, a, b, layer, bufs, *, a_group, tm):
    gf, gb = bufs
    G, _, R, C = gf.shape
    M = a.shape[0]
    tm = _tile(M, tm)
    n_m = M // tm

    def body(a_ref, b_ref, gf_in, gb_in, gf_ref, gb_ref):
        m = pl.program_id(1)
        part = _dot(a_ref[...].astype(BF16), b_ref[...].astype(BF16), 0, 0)

        @pl.when(m == 0)
        def _():
            gf_ref[...] = part

        @pl.when(m > 0)
        def _():
            gf_ref[...] += part

        @pl.when(m == n_m - 1)
        def _():
            gb_ref[...] = gf_ref[...].astype(BF16)

    a_spec = pl.BlockSpec((tm, R), (lambda g, m: (m, g)) if a_group else (lambda g, m: (m, 0)))
    if b.ndim == 3:
        assert not a_group
        b_spec = pl.BlockSpec((None, tm, C), lambda g, m: (g, m, 0))
    else:
        b_spec = pl.BlockSpec((tm, C), (lambda g, m: (m, 0)) if a_group else (lambda g, m: (m, g)))
    o_spec = pl.BlockSpec((None, None, R, C), lambda g, m: (g, layer, 0, 0))
    return pl.pallas_call(
        body, name=name, grid=(G, n_m),
        in_specs=[a_spec, b_spec, ANY, ANY], out_specs=[o_spec, o_spec],
        out_shape=[jax.ShapeDtypeStruct(gf.shape, F32), jax.ShapeDtypeStruct(gb.shape, BF16)],
        input_output_aliases={2: 0, 3: 1},
        compiler_params=_cp("parallel", "arbitrary"),
    )(a, b, gf, gb)


def _rms_fwd(name, x, g3, layer):
    S, D = x.shape
    tm = _tile(S, 512)

    def body(x_ref, g_ref, h_ref):
        xv = x_ref[...]
        r = lax.rsqrt(jnp.mean(xv * xv, axis=-1, keepdims=True) + EPS)
        h_ref[...] = (xv * r * g_ref[...]).astype(BF16)

    row = pl.BlockSpec((tm, D), lambda m: (m, 0))
    return pl.pallas_call(
        body, name=name, grid=(S // tm,),
        in_specs=[row, pl.BlockSpec((None, 1, D), lambda m: (layer, 0, 0))], out_specs=row,
        out_shape=jax.ShapeDtypeStruct((S, D), BF16), compiler_params=_cp("parallel"),
    )(x, g3)


def _rms_bwd(name, x, g3, layer, dh, dres):
    S, D = x.shape
    tm = _tile(S, 512)

    def body(x_ref, g_ref, dh_ref, dres_ref, dx_ref, dg_ref):
        xv = x_ref[...]
        d = dh_ref[...].astype(F32)
        r = lax.rsqrt(jnp.mean(xv * xv, axis=-1, keepdims=True) + EPS)
        xhat = xv * r
        dxhat = d * g_ref[...]
        dx_ref[...] = dres_ref[...] + r * (dxhat - xhat * jnp.mean(dxhat * xhat, axis=-1, keepdims=True))

        @pl.when(pl.program_id(0) == 0)
        def _():
            dg_ref[...] = jnp.zeros_like(dg_ref)

        dg_ref[...] += jnp.sum(d * xhat, axis=0, keepdims=True)

    row = pl.BlockSpec((tm, D), lambda m: (m, 0))
    return pl.pallas_call(
        body, name=name, grid=(S // tm,),
        in_specs=[row, pl.BlockSpec((None, 1, D), lambda m: (layer, 0, 0)), row, row],
        out_specs=[row, pl.BlockSpec((1, D), lambda m: (0, 0))],
        out_shape=[jax.ShapeDtypeStruct((S, D), F32), jax.ShapeDtypeStruct((1, D), F32)],
        compiler_params=_cp("arbitrary"),
    )(x, g3, dh, dres)


def _loss_grad(y, target):
    S, D = y.shape
    tm = _tile(S, 512)

    def body(y_ref, t_ref, dy_ref, l_ref):
        e = y_ref[...] - t_ref[...]
        dy_ref[...] = e * (1.0 / D)

        @pl.when(pl.program_id(0) == 0)
        def _():
            l_ref[...] = jnp.zeros_like(l_ref)

        l_ref[...] += (0.5 / D) * jnp.sum(jnp.sum(e * e, axis=1, keepdims=True), axis=0, keepdims=True)

    row = pl.BlockSpec((tm, D), lambda m: (m, 0))
    return pl.pallas_call(
        body, name="loss_grad", grid=(S // tm,), in_specs=[row, row],
        out_specs=[row, pl.BlockSpec((1, LANES), lambda m: (0, 0))],
        out_shape=[jax.ShapeDtypeStruct((S, D), F32), jax.ShapeDtypeStruct((1, LANES), F32)],
        compiler_params=_cp("arbitrary"),
    )(y, target)


def _adamw(name, w, m, v, g_parts):
    shape = w.shape
    C = shape[-1]
    flat = lambda t: t.reshape(-1, C)
    rows = flat(w).shape[0]
    tr = _tile(rows, ELEMENTWISE_ROWS) if rows % ELEMENTWISE_ROWS == 0 else rows
    n_g = len(g_parts)

    def body(w_ref, m_ref, v_ref, *rest):
        g = rest[0][...]
        for p in rest[1:n_g]:
            g = g + p[...]
        g_ref, d_ref, nm_ref, nv_ref = rest[n_g:]
        m2 = ADAM_B1 * m_ref[...] + (1.0 - ADAM_B1) * g
        v2 = ADAM_B2 * v_ref[...] + (1.0 - ADAM_B2) * jnp.square(g)
        m_hat = m2 / (1.0 - ADAM_B1 ** ADAM_STEP)
        v_hat = v2 / (1.0 - ADAM_B2 ** ADAM_STEP)
        g_ref[...] = g
        d_ref[...] = -ADAM_LR * (m_hat / (jnp.sqrt(v_hat) + ADAM_EPS) + ADAM_WD * w_ref[...])
        nm_ref[...] = m2
        nv_ref[...] = v2

    blk = pl.BlockSpec((tr, C), lambda i: (i, 0))
    outs = pl.pallas_call(
        body, name=name, grid=(rows // tr,), in_specs=[blk] * (3 + n_g), out_specs=[blk] * 4,
        out_shape=[jax.ShapeDtypeStruct((rows, C), F32)] * 4, compiler_params=_cp("parallel"),
    )(flat(w), flat(m), flat(v), *[flat(p) for p in g_parts])
    return [o.reshape(shape) for o in outs]


def _sum4(name, own, recv):
    shape = own.shape
    C = shape[-1]
    own2 = own.reshape(-1, C)
    rows = own2.shape[0]
    recv3 = recv.reshape(3, rows, C)
    tr = _tile(rows, ELEMENTWISE_ROWS) if rows % ELEMENTWISE_ROWS == 0 else rows

    def body(o_ref, r_ref, out_ref):
        acc = o_ref[...]
        for k in range(3):
            acc = acc + r_ref[k].astype(F32)
        out_ref[...] = acc

    blk = pl.BlockSpec((tr, C), lambda i: (i, 0))
    out = pl.pallas_call(
        body, name=name, grid=(rows // tr,),
        in_specs=[blk, pl.BlockSpec((3, tr, C), lambda i: (0, i, 0))], out_specs=blk,
        out_shape=jax.ShapeDtypeStruct((rows, C), F32), compiler_params=_cp("parallel"),
    )(own2, recv3)
    return out.reshape(shape)


def _gelu(x):
    return x * (0.5 * (1.0 + jnp.tanh(0.7978845608028654 * (x + 0.044715 * (x * x * x)))))


def _layernorm(t, g, b):
    mu = jnp.mean(t, axis=-1, keepdims=True)
    var = jnp.mean(jnp.square(t - mu), axis=-1, keepdims=True)
    return (t - mu) * lax.rsqrt(var + EPS) * g + b


def _silu(x):
    return x * jax.nn.sigmoid(x)


def _a_value(zv, g, b):
    return _layernorm(_gelu(zv), g, b)


def _b_tail(gc, g, b):
    return _silu(_layernorm(gc, g, b))


def _first_head(shape):
    return lax.broadcasted_iota(jnp.int32, shape, len(shape) - 1) < HEAD_DIM


def _spatial_mix(spw_ref, vb, tm):
    first = _first_head((CHUNK, LANES))
    rows = []
    for n in range(tm // CHUNK):
        blocks = []
        for j in range(A_GROUPS // 2):
            vblk = vb[n * CHUNK:(n + 1) * CHUNK, j * LANES:(j + 1) * LANES]
            r0 = _dot(spw_ref[2 * j], vblk, 1, 0)
            r1 = _dot(spw_ref[2 * j + 1], vblk, 1, 0)
            blocks.append(jnp.where(first, r0, r1))
        rows.append(jnp.concatenate(blocks, axis=1))
    return jnp.concatenate(rows, axis=0) if len(rows) > 1 else rows[0]


def _ab_tail_fwd(name, z, gconv, spw, bias_full, vn_g, vn_b, cn_g, cn_b, layer):
    S = z.shape[0]
    AW = 512
    tm = _tile(S, 256)

    def body(zu_ref, zv_ref, gc_ref, spw_ref, bias_ref, vg_ref, vb_ref, cg_ref, cb_ref, cat_ref):
        u = _gelu(zu_ref[...])
        v = _a_value(zv_ref[...], vg_ref[...], vb_ref[...])
        sv = _spatial_mix(spw_ref, v.astype(BF16), tm) + jnp.tile(bias_ref[...], (tm // CHUNK, 1))
        cat_ref[:, :AW] = (u * sv).astype(BF16)
        cat_ref[:, AW:] = _b_tail(gc_ref[...], cg_ref[...], cb_ref[...]).astype(BF16)

    vec = pl.BlockSpec((None, 1, AW), lambda m: (layer, 0, 0))
    return pl.pallas_call(
        body, name=name, grid=(S // tm,),
        in_specs=[pl.BlockSpec((tm, AW), lambda m: (m, 0)), pl.BlockSpec((tm, AW), lambda m: (m, 1)),
                  pl.BlockSpec((tm, AW), lambda m: (m, 0)),
                  pl.BlockSpec((None, A_GROUPS, CHUNK, CHUNK), lambda m: (layer, 0, 0, 0)),
                  pl.BlockSpec((None, CHUNK, AW), lambda m: (layer, 0, 0)), vec, vec, vec, vec],
        out_specs=pl.BlockSpec((tm, 2 * AW), lambda m: (m, 0)),
        out_shape=jax.ShapeDtypeStruct((S, 2 * AW), BF16), compiler_params=_cp("parallel"),
    )(z, z, gconv, spw, bias_full, vn_g, vn_b, cn_g, cn_b)


def _ab_tail_bwd(name, z, gconv, dcat, spw, spw_t, bias_full, vn_g, vn_b, cn_g, cn_b, layer):
    S = z.shape[0]
    AW = 512
    tm = _tile(S, 256)
    n_chunks = tm // CHUNK

    def body(zu_ref, zv_ref, gc_ref, dcat_ref, spw_ref, spwt_ref, bias_ref, vg_ref, vb_ref, cg_ref, cb_ref,
             dz_ref, dgc_ref, dspw_ref, dbias_ref, dvg_ref, dvb_ref, dcg_ref, dcb_ref):
        @pl.when(pl.program_id(0) == 0)
        def _():
            for r in (dspw_ref, dbias_ref, dvg_ref, dvb_ref, dcg_ref, dcb_ref):
                r[...] = jnp.zeros_like(r)

        dya = dcat_ref[:, :AW]
        dyb = dcat_ref[:, AW:]
        u, u_vjp = jax.vjp(_gelu, zu_ref[...])
        v, v_vjp = jax.vjp(_a_value, zv_ref[...], vg_ref[...], vb_ref[...])
        vb16 = v.astype(BF16)
        sv = _spatial_mix(spw_ref, vb16, tm) + jnp.tile(bias_ref[...], (n_chunks, 1))
        (dzu,) = u_vjp(dya * sv)
        dsv = dya * u
        dsv16 = dsv.astype(BF16)
        dv = _spatial_mix(spwt_ref, dsv16, tm)
        dzv, dvg, dvb = v_vjp(dv)
        dz_ref[0] = dzu
        dz_ref[1] = dzv
        dvg_ref[...] += dvg
        dvb_ref[...] += dvb

        first = _first_head((CHUNK, LANES))
        zero = jnp.zeros((), BF16)
        dbias = jnp.zeros((CHUNK, AW), F32)
        for n in range(n_chunks):
            rows = slice(n * CHUNK, (n + 1) * CHUNK)
            dbias = dbias + dsv[rows]
            for j in range(A_GROUPS // 2):
                cols = slice(j * LANES, (j + 1) * LANES)
                dblk, vblk = dsv16[rows, cols], vb16[rows, cols]
                dspw_ref[2 * j] += _dot(jnp.where(first, dblk, zero), vblk, 1, 1)
                dspw_ref[2 * j + 1] += _dot(jnp.where(first, zero, dblk), vblk, 1, 1)
        dbias_ref[...] += dbias

        _, t_vjp = jax.vjp(_b_tail, gc_ref[...], cg_ref[...], cb_ref[...])
        dgc, dcg, dcb = t_vjp(dyb)
        dgc_ref[...] = dgc
        dcg_ref[...] += dcg
        dcb_ref[...] += dcb

    vec = pl.BlockSpec((None, 1, AW), lambda m: (layer, 0, 0))
    spw_spec = pl.BlockSpec((None, A_GROUPS, CHUNK, CHUNK), lambda m: (layer, 0, 0, 0))
    ovec = pl.BlockSpec((1, AW), lambda m: (0, 0))
    return pl.pallas_call(
        body, name=name, grid=(S // tm,),
        in_specs=[pl.BlockSpec((tm, AW), lambda m: (m, 0)), pl.BlockSpec((tm, AW), lambda m: (m, 1)),
                  pl.BlockSpec((tm, AW), lambda m: (m, 0)), pl.BlockSpec((tm, 2 * AW), lambda m: (m, 0)),
                  spw_spec, spw_spec, pl.BlockSpec((None, CHUNK, AW), lambda m: (layer, 0, 0)), vec, vec, vec, vec],
        out_specs=[pl.BlockSpec((2, tm, AW), lambda m: (0, m, 0)), pl.BlockSpec((tm, AW), lambda m: (m, 0)),
                   pl.BlockSpec((A_GROUPS, CHUNK, CHUNK), lambda m: (0, 0, 0)),
                   pl.BlockSpec((CHUNK, AW), lambda m: (0, 0)), ovec, ovec, ovec, ovec],
        out_shape=[jax.ShapeDtypeStruct((4, S, AW), F32), jax.ShapeDtypeStruct((S, AW), F32),
                   jax.ShapeDtypeStruct((A_GROUPS, CHUNK, CHUNK), F32), jax.ShapeDtypeStruct((CHUNK, AW), F32)]
                  + [jax.ShapeDtypeStruct((1, AW), F32)] * 4,
        compiler_params=_cp("arbitrary"),
    )(z, z, gconv, dcat, spw, spw_t, bias_full, vn_g, vn_b, cn_g, cn_b)


def _fold_bias(dbias_full):
    def body(d_ref, o_ref):
        d = d_ref[...]
        hi = d.astype(BF16)
        lo = (d - hi.astype(F32)).astype(BF16)
        r = lax.broadcasted_iota(jnp.int32, (512, LANES), 0)
        c = lax.broadcasted_iota(jnp.int32, (512, LANES), 1)
        fold = jnp.where(lax.shift_right_logical(r, 6) == c, 1.0, 0.0).astype(BF16)
        o_ref[...] = _dot(hi, fold, 1, 0) + _dot(lo, fold, 1, 0)

    return pl.pallas_call(body, name="fold_spatial_bias", out_shape=jax.ShapeDtypeStruct((CHUNK, LANES), F32))(dbias_full)


def _halo_specs(tm, n_halo_blocks, col):
    r = tm // CONV_HALO
    prev = pl.BlockSpec((CONV_HALO, LANES), lambda j, i: (jnp.maximum(i * r - 1, 0), col + j))
    cur = pl.BlockSpec((tm, LANES), lambda j, i: (i, col + j))
    nxt = pl.BlockSpec((CONV_HALO, LANES), lambda j, i: (jnp.minimum((i + 1) * r, n_halo_blocks - 1), col + j))
    return [prev, cur, nxt]


def _fill_halo(scr, prev, cur, nxt, tm, i, n_i):
    scr[0:CONV_HALO, :] = jnp.where(i > 0, prev, 0.0)
    scr[CONV_HALO:CONV_HALO + tm, :] = cur
    scr[CONV_HALO + tm:2 * CONV_HALO + tm, :] = jnp.where(i < n_i - 1, nxt, 0.0)


def _glu_conv_fwd(name, z, cw, cb3, layer):
    S = z.shape[0]
    tm = _tile(S, 512)
    n_i = S // tm
    pad = CONV_WIDTH // 2

    def body(vp, vc, vn, gp, gc, gn, w_ref, b_ref, out_ref, scr):
        i = pl.program_id(1)
        glu = lambda a, b: a[...] * jax.nn.sigmoid(b[...])
        _fill_halo(scr, glu(vp, gp), glu(vc, gc), glu(vn, gn), tm, i, n_i)
        acc = jnp.zeros((tm, LANES), F32)
        for j in range(CONV_WIDTH):
            acc = acc + w_ref[j:j + 1, :] * scr[pl.ds(CONV_HALO - pad + j, tm), :]
        out_ref[...] = acc + b_ref[...]

    return pl.pallas_call(
        body, name=name, grid=(4, n_i),
        in_specs=_halo_specs(tm, S // CONV_HALO, 8) + _halo_specs(tm, S // CONV_HALO, 12)
        + [pl.BlockSpec((None, None, CONV_WIDTH, LANES), lambda j, i: (j, layer, 0, 0)),
           pl.BlockSpec((None, 1, LANES), lambda j, i: (layer, 0, j))],
        out_specs=pl.BlockSpec((tm, LANES), lambda j, i: (i, j)),
        out_shape=jax.ShapeDtypeStruct((S, 4 * LANES), F32),
        scratch_shapes=[pltpu.VMEM((tm + 2 * CONV_HALO, LANES), F32)],
        compiler_params=_cp("parallel", "parallel"),
    )(z, z, z, z, z, z, cw, cb3)


def _glu_conv_bwd(name, z, dgconv, dz, cw, layer, bufs):
    S = z.shape[0]
    tm = _tile(S, 512)
    n_i = S // tm
    pad = CONV_WIDTH // 2
    gf, gb = bufs

    def body(vp, vc, vn, gp, gc, gn, dp, dc, dn, w_ref, dz_in, gf_in, gb_in,
             dz_ref, gf_ref, gb_ref, db_ref, g_scr, d_scr):
        i = pl.program_id(1)
        sig = jax.nn.sigmoid(gc[...])
        _fill_halo(g_scr, vp[...] * jax.nn.sigmoid(gp[...]), vc[...] * sig, vn[...] * jax.nn.sigmoid(gn[...]), tm, i, n_i)
        _fill_halo(d_scr, dp[...], dc[...], dn[...], tm, i, n_i)

        @pl.when(i == 0)
        def _():
            gf_ref[...] = jnp.zeros_like(gf_ref)
            db_ref[...] = jnp.zeros_like(db_ref)

        d_cur = dc[...]
        dglu = jnp.zeros((tm, LANES), F32)
        for j in range(CONV_WIDTH):
            dglu = dglu + w_ref[j:j + 1, :] * d_scr[pl.ds(CONV_HALO + pad - j, tm), :]
            gf_ref[j:j + 1, :] += jnp.sum(d_cur * g_scr[pl.ds(CONV_HALO - pad + j, tm), :], axis=0, keepdims=True)
        db_ref[...] += jnp.sum(d_cur, axis=0, keepdims=True)
        dz_ref[0] = dglu * sig
        dz_ref[1] = dglu * vc[...] * sig * (1.0 - sig)

        @pl.when(i == n_i - 1)
        def _():
            gb_ref[...] = gf_ref[...].astype(BF16)

    w_spec = pl.BlockSpec((None, None, CONV_WIDTH, LANES), lambda j, i: (j, layer, 0, 0))
    return pl.pallas_call(
        body, name=name, grid=(4, n_i),
        in_specs=_halo_specs(tm, S // CONV_HALO, 8) + _halo_specs(tm, S // CONV_HALO, 12)
        + _halo_specs(tm, S // CONV_HALO, 0) + [w_spec, ANY, ANY, ANY],
        out_specs=[pl.BlockSpec((2, tm, LANES), lambda j, i: (1, i, j)),
                   w_spec, w_spec, pl.BlockSpec((1, LANES), lambda j, i: (0, j))],
        out_shape=[jax.ShapeDtypeStruct(dz.shape, F32), jax.ShapeDtypeStruct(gf.shape, F32),
                   jax.ShapeDtypeStruct(gb.shape, BF16), jax.ShapeDtypeStruct((1, 4 * LANES), F32)],
        input_output_aliases={10: 0, 11: 1, 12: 2},
        scratch_shapes=[pltpu.VMEM((tm + 2 * CONV_HALO, LANES), F32)] * 2,
        compiler_params=_cp("parallel", "arbitrary"),
    )(z, z, z, z, z, z, dgconv, dgconv, dgconv, cw, dz, gf, gb)


def _seg_matrix(scale):
    r = lax.broadcasted_iota(jnp.int32, (LANES, LANES), 0)
    c = lax.broadcasted_iota(jnp.int32, (LANES, LANES), 1)
    return jnp.where(lax.shift_right_logical(r, 6) == lax.shift_right_logical(c, 6), scale, 0.0).astype(BF16)


def _seg_sum(x, seg):
    hi = x.astype(BF16)
    lo = (x - hi.astype(F32)).astype(BF16)
    return _dot(hi, seg, 1, 0) + _dot(lo, seg, 1, 0)


def _rope_tables(S):
    pos = jnp.arange(S, dtype=F32)
    inv_freq = ROPE_THETA ** (-jnp.arange(0, ROT_DIM, 2, dtype=F32) / ROT_DIM)
    ang = pos[:, None] * inv_freq[None, :]
    cos, sin = jnp.cos(ang), jnp.sin(ang)
    half = ROT_DIM // 2
    rest = HEAD_DIM - ROT_DIM
    one, zero = jnp.ones((S, rest), F32), jnp.zeros((S, rest), F32)
    zh = jnp.zeros((S, half), F32)
    c = jnp.concatenate([cos, cos, one], axis=1)
    sa = jnp.concatenate([-sin, zh, zero], axis=1)
    sb = jnp.concatenate([zh, sin, zero], axis=1)
    return [jnp.tile(t, (1, 2)) for t in (c, sa, sb)]


def _qk_fwd(name, qkv, gq, gk, tables):
    S = qkv.shape[0]
    W = N_HEADS * HEAD_DIM
    tm = _tile(S, 256)
    half = ROT_DIM // 2

    def body(q_ref, k_ref, gq_ref, gk_ref, c_ref, sa_ref, sb_ref, qn_ref, kn_ref):
        seg = _seg_matrix(1.0 / HEAD_DIM)
        c, sa, sb = c_ref[...], sa_ref[...], sb_ref[...]
        for t_ref, g_ref, o_ref in ((q_ref, gq_ref, qn_ref), (k_ref, gk_ref, kn_ref)):
            for blk in range(W // LANES):
                cols = slice(blk * LANES, (blk + 1) * LANES)
                t = t_ref[:, cols]
                y = t * lax.rsqrt(_seg_sum(t * t, seg) + EPS) * g_ref[...]
                o_ref[:, cols] = y * c + pltpu.roll(y, LANES - half, 1) * sa + pltpu.roll(y, half, 1) * sb

    row = lambda k: pl.BlockSpec((tm, W), lambda m: (m, k))
    gain = pl.BlockSpec((1, LANES), lambda m: (0, 0))
    tab = pl.BlockSpec((tm, LANES), lambda m: (m, 0))
    return pl.pallas_call(
        body, name=name, grid=(S // tm,),
        in_specs=[row(0), row(1), gain, gain, tab, tab, tab], out_specs=[row(0)] * 2,
        out_shape=[jax.ShapeDtypeStruct((S, W), F32)] * 2, compiler_params=_cp("parallel"),
    )(qkv, qkv, gq, gk, *tables)


def _qk_bwd(name, qkv, gq, gk, tables, dqs, dks, dvs):
    S = qkv.shape[0]
    W = N_HEADS * HEAD_DIM
    tm = _tile(S, 256)
    half = ROT_DIM // 2
    n_p = len(dqs)

    def body(q_ref, k_ref, gq_ref, gk_ref, c_ref, sa_ref, sb_ref, *rest):
        dq_refs, dk_refs, dv_refs = rest[:n_p], rest[n_p:2 * n_p], rest[2 * n_p:3 * n_p]
        dqkv_ref, dgq_ref, dgk_ref = rest[3 * n_p:]

        @pl.when(pl.program_id(0) == 0)
        def _():
            dgq_ref[...] = jnp.zeros_like(dgq_ref)
            dgk_ref[...] = jnp.zeros_like(dgk_ref)

        seg = _seg_matrix(1.0 / HEAD_DIM)
        r_i = lax.broadcasted_iota(jnp.int32, (LANES, LANES), 0)
        c_i = lax.broadcasted_iota(jnp.int32, (LANES, LANES), 1)
        same_dim = jnp.where((r_i & (HEAD_DIM - 1)) == (c_i & (HEAD_DIM - 1)), 1.0, 0.0).astype(BF16)
        c, sa, sb = c_ref[...], sa_ref[...], sb_ref[...]
        for idx, (t_ref, g_ref, d_refs, dg_ref) in enumerate(((q_ref, gq_ref, dq_refs, dgq_ref),
                                                              (k_ref, gk_ref, dk_refs, dgk_ref))):
            dg = jnp.zeros((1, LANES), F32)
            for blk in range(W // LANES):
                cols = slice(blk * LANES, (blk + 1) * LANES)
                dout = d_refs[0][:, cols]
                for r in d_refs[1:]:
                    dout = dout + r[:, cols]
                dy = dout * c + pltpu.roll(dout * sa, half, 1) + pltpu.roll(dout * sb, LANES - half, 1)
                t = t_ref[:, cols]
                r_ = lax.rsqrt(_seg_sum(t * t, seg) + EPS)
                xhat = t * r_
                dg = dg + jnp.sum(dy * xhat, axis=0, keepdims=True)
                dxhat = dy * g_ref[...]
                dt = r_ * (dxhat - xhat * _seg_sum(dxhat * xhat, seg))
                dqkv_ref[:, idx * W + blk * LANES: idx * W + (blk + 1) * LANES] = dt.astype(BF16)
            dg_ref[...] += _seg_sum(jnp.broadcast_to(dg, (8, LANES)), same_dim)[0:1]
        dv = dv_refs[0][...]
        for r in dv_refs[1:]:
            dv = dv + r[...]
        dqkv_ref[:, 2 * W:] = dv.astype(BF16)

    row = lambda k: pl.BlockSpec((tm, W), lambda m: (m, k))
    gain = pl.BlockSpec((1, LANES), lambda m: (0, 0))
    tab = pl.BlockSpec((tm, LANES), lambda m: (m, 0))
    return pl.pallas_call(
        body, name=name, grid=(S // tm,),
        in_specs=[row(0), row(1), gain, gain, tab, tab, tab] + [row(0)] * (3 * n_p),
        out_specs=[pl.BlockSpec((tm, 3 * W), lambda m: (m, 0)), gain, gain],
        out_shape=[jax.ShapeDtypeStruct((S, 3 * W), BF16), jax.ShapeDtypeStruct((1, LANES), F32),
                   jax.ShapeDtypeStruct((1, LANES), F32)],
        compiler_params=_cp("arbitrary"),
    )(qkv, qkv, gq, gk, *tables, *dqs, *dks, *dvs)


ATTN_BQ = 2 * BAND
ATTN_ROWS = 16 * ATTN_BQ
V_COL = 2 * N_HEADS * HEAD_DIM // LANES


def _attn_geometry(S, d):
    rows = min(ATTN_ROWS, S)
    halo = BAND * d
    assert rows % (ATTN_BQ * d) == 0 and S % rows == 0, (S, d)
    return rows, halo, rows // (ATTN_BQ * d)


def _attn_specs(S, d, col):
    rows, halo, _ = _attn_geometry(S, d)
    r = rows // halo
    n_h = S // halo
    prev = pl.BlockSpec((halo, LANES), lambda j, i: (jnp.maximum(i * r - 1, 0), col + j))
    cur = pl.BlockSpec((rows, LANES), lambda j, i: (i, col + j))
    nxt = pl.BlockSpec((halo, LANES), lambda j, i: (jnp.minimum((i + 1) * r, n_h - 1), col + j))
    return [prev, cur, nxt]


def _fill_window(scr, prev, cur, nxt, rows, halo):
    scr[0:halo, :] = prev[...]
    scr[halo:halo + rows, :] = cur[...]
    scr[halo + rows:2 * halo + rows, :] = nxt[...]


def _strided(ref, start, size, d):
    return ref[pl.ds(start, size, stride=d) if d > 1 else pl.ds(start, size), :]


def _band_masks(i, S, d, sb):
    rows, _, _ = _attn_geometry(S, d)
    L = S // d
    base = i * (rows // d) + sb * ATTN_BQ
    wk = ATTN_BQ + 2 * BAND
    row = lax.broadcasted_iota(jnp.int32, (ATTN_BQ, wk), 0)
    col = lax.broadcasted_iota(jnp.int32, (ATTN_BQ, wk), 1)
    lj = base - BAND + col
    valid = (jnp.abs(col - BAND - row) <= BAND) & (lj >= 0) & (lj < L)
    row_t = lax.broadcasted_iota(jnp.int32, (wk, ATTN_BQ), 0)
    col_t = lax.broadcasted_iota(jnp.int32, (wk, ATTN_BQ), 1)
    li = base - BAND + row_t
    valid_t = (jnp.abs(row_t - BAND - col_t) <= BAND) & (li >= 0) & (li < L)
    return valid, valid_t


def _attn_fwd(name, q, k, v, v_col, d):
    S, W = q.shape
    rows, halo, n_sb = _attn_geometry(S, d)
    wk = ATTN_BQ + 2 * BAND
    scale = HEAD_DIM ** -0.5

    def body(q_ref, kp, kc, kn, vp, vc, vn, o_ref, lse_ref, kw, vw):
        i = pl.program_id(1)
        _fill_window(kw, kp, kc, kn, rows, halo)
        _fill_window(vw, vp, vc, vn, rows, halo)
        first = _first_head((ATTN_BQ, LANES))
        zero = jnp.zeros((), BF16)
        for sb in range(n_sb):
            valid, _ = _band_masks(i, S, d, sb)
            for r in range(d):
                start = r + d * sb * ATTN_BQ
                qv = _strided(q_ref, start, ATTN_BQ, d).astype(BF16)
                kv = _strided(kw, start, wk, d).astype(BF16)
                vv = _strided(vw, start, wk, d).astype(BF16)
                o_h, lse_h = [], []
                for hm in (first, jnp.logical_not(first)):
                    s = jnp.where(valid, _dot(jnp.where(hm, qv, zero), kv, 1, 1) * scale, NEG)
                    mx = jnp.max(s, axis=-1, keepdims=True)
                    p = jnp.exp(s - mx)
                    den = jnp.sum(p, axis=-1, keepdims=True)
                    o_h.append(_dot(p.astype(BF16), vv, 1, 0) / den)
                    lse_h.append(mx + jnp.log(den))
                dst = pl.ds(start, ATTN_BQ, stride=d) if d > 1 else pl.ds(start, ATTN_BQ)
                o_ref[dst, :] = jnp.where(first, o_h[0], o_h[1])
                lse_ref[dst, :] = jnp.where(first, lse_h[0], lse_h[1])

    cur = _attn_specs(S, d, 0)[1]
    return pl.pallas_call(
        body, name=name, grid=(W // LANES, S // rows),
        in_specs=[cur] + _attn_specs(S, d, 0) + _attn_specs(S, d, v_col), out_specs=[cur, cur],
        out_shape=[jax.ShapeDtypeStruct((S, W), F32)] * 2,
        scratch_shapes=[pltpu.VMEM((rows + 2 * halo, LANES), F32)] * 2,
        compiler_params=_cp("parallel", "parallel"),
    )(q, k, k, k, v, v, v)


def _attn_merge(os, lses):
    S, W = os[0].shape
    tm = _tile(S, 256)
    n_p = len(os)

    def body(*refs):
        o_refs, l_refs = refs[:n_p], refs[n_p:2 * n_p]
        o_ref, lt_ref = refs[2 * n_p:]
        ls = [r[...] for r in l_refs]
        mx = functools.reduce(jnp.maximum, ls)
        es = [jnp.exp(l - mx) for l in ls]
        den = functools.reduce(lambda a, b: a + b, es)
        acc = es[0] * o_refs[0][...]
        for e, r in zip(es[1:], o_refs[1:]):
            acc = acc + e * r[...]
        o_ref[...] = (acc / den).astype(BF16)
        lt_ref[...] = mx + jnp.log(den)

    row = pl.BlockSpec((tm, W), lambda m: (m, 0))
    return pl.pallas_call(
        body, name="attn_merge", grid=(S // tm,), in_specs=[row] * (2 * n_p), out_specs=[row, row],
        out_shape=[jax.ShapeDtypeStruct((S, W), BF16), jax.ShapeDtypeStruct((S, W), F32)],
        compiler_params=_cp("parallel"),
    )(*os, *lses)


def _attn_delta(do, o):
    S, W = do.shape
    tm = _tile(S, 256)

    def body(do_ref, o_ref, dl_ref):
        seg = _seg_matrix(1.0)
        for blk in range(W // LANES):
            cols = slice(blk * LANES, (blk + 1) * LANES)
            dl_ref[:, cols] = _seg_sum(do_ref[:, cols] * o_ref[:, cols].astype(F32), seg)

    row = pl.BlockSpec((tm, W), lambda m: (m, 0))
    return pl.pallas_call(
        body, name="attn_delta", grid=(S // tm,), in_specs=[row, row], out_specs=row,
        out_shape=jax.ShapeDtypeStruct((S, W), F32), compiler_params=_cp("parallel"),
    )(do, o)


def _attn_bwd_q(name, q, k, v, v_col, do, lse, delta, d):
    S, W = q.shape
    rows, halo, n_sb = _attn_geometry(S, d)
    wk = ATTN_BQ + 2 * BAND
    scale = HEAD_DIM ** -0.5

    def body(q_ref, do_ref, l_ref, dl_ref, kp, kc, kn, vp, vc, vn, dq_ref, kw, vw):
        i = pl.program_id(1)
        _fill_window(kw, kp, kc, kn, rows, halo)
        _fill_window(vw, vp, vc, vn, rows, halo)
        first = _first_head((ATTN_BQ, LANES))
        zero = jnp.zeros((), BF16)
        for sb in range(n_sb):
            valid, _ = _band_masks(i, S, d, sb)
            for r in range(d):
                start = r + d * sb * ATTN_BQ
                qv = _strided(q_ref, start, ATTN_BQ, d).astype(BF16)
                dov = _strided(do_ref, start, ATTN_BQ, d).astype(BF16)
                lv = _strided(l_ref, start, ATTN_BQ, d)
                dlv = _strided(dl_ref, start, ATTN_BQ, d)
                kv = _strided(kw, start, wk, d).astype(BF16)
                vv = _strided(vw, start, wk, d).astype(BF16)
                dq_h = []
                for hh, hm in enumerate((first, jnp.logical_not(first))):
                    lane0 = hh * HEAD_DIM
                    s = jnp.where(valid, _dot(jnp.where(hm, qv, zero), kv, 1, 1) * scale, NEG)
                    p = jnp.exp(s - lv[:, lane0:lane0 + 1])
                    dp = _dot(jnp.where(hm, dov, zero), vv, 1, 1)
                    ds = p * (dp - dlv[:, lane0:lane0 + 1]) * scale
                    dq_h.append(_dot(ds.astype(BF16), kv, 1, 0))
                dst = pl.ds(start, ATTN_BQ, stride=d) if d > 1 else pl.ds(start, ATTN_BQ)
                dq_ref[dst, :] = jnp.where(first, dq_h[0], dq_h[1])

    cur = _attn_specs(S, d, 0)[1]
    return pl.pallas_call(
        body, name=name, grid=(W // LANES, S // rows),
        in_specs=[cur] * 4 + _attn_specs(S, d, 0) + _attn_specs(S, d, v_col), out_specs=cur,
        out_shape=jax.ShapeDtypeStruct((S, W), F32),
        scratch_shapes=[pltpu.VMEM((rows + 2 * halo, LANES), F32)] * 2,
        compiler_params=_cp("parallel", "parallel"),
    )(q, do, lse, delta, k, k, k, v, v, v)


def _attn_bwd_kv(name, q, k, v, v_col, do, lse, delta, d):
    S, W = q.shape
    rows, halo, n_sb = _attn_geometry(S, d)
    wk = ATTN_BQ + 2 * BAND
    scale = HEAD_DIM ** -0.5

    def body(k_ref, v_ref, qp, qc, qn, dop, doc, don, lp, lc, ln, dlp, dlc, dln, dk_ref, dv_ref, qw, dow, lw, dlw):
        i = pl.program_id(1)
        _fill_window(qw, qp, qc, qn, rows, halo)
        _fill_window(dow, dop, doc, don, rows, halo)
        _fill_window(lw, lp, lc, ln, rows, halo)
        _fill_window(dlw, dlp, dlc, dln, rows, halo)
        first = _first_head((ATTN_BQ, LANES))
        first_w = _first_head((wk, LANES))
        zero = jnp.zeros((), BF16)
        for sb in range(n_sb):
            _, valid_t = _band_masks(i, S, d, sb)
            for r in range(d):
                start = r + d * sb * ATTN_BQ
                kv = _strided(k_ref, start, ATTN_BQ, d).astype(BF16)
                vv = _strided(v_ref, start, ATTN_BQ, d).astype(BF16)
                qv = _strided(qw, start, wk, d).astype(BF16)
                dov = _strided(dow, start, wk, d).astype(BF16)
                lv = _strided(lw, start, wk, d)
                dlv = _strided(dlw, start, wk, d)
                dk_h, dv_h = [], []
                for hh, hm_w in enumerate((first_w, jnp.logical_not(first_w))):
                    lane0 = hh * HEAD_DIM
                    st = jnp.where(valid_t, _dot(jnp.where(hm_w, qv, zero), kv, 1, 1) * scale, NEG)
                    pt = jnp.exp(st - lv[:, lane0:lane0 + 1])
                    dv_h.append(_dot(pt.astype(BF16), dov, 0, 0))
                    dpt = _dot(jnp.where(hm_w, dov, zero), vv, 1, 1)
                    dst_ = pt * (dpt - dlv[:, lane0:lane0 + 1]) * scale
                    dk_h.append(_dot(dst_.astype(BF16), qv, 0, 0))
                dst = pl.ds(start, ATTN_BQ, stride=d) if d > 1 else pl.ds(start, ATTN_BQ)
                dk_ref[dst, :] = jnp.where(first, dk_h[0], dk_h[1])
                dv_ref[dst, :] = jnp.where(first, dv_h[0], dv_h[1])

    cur = _attn_specs(S, d, 0)[1]
    win = _attn_specs(S, d, 0)
    return pl.pallas_call(
        body, name=name, grid=(W // LANES, S // rows),
        in_specs=[cur, _attn_specs(S, d, v_col)[1]] + win * 4, out_specs=[cur, cur],
        out_shape=[jax.ShapeDtypeStruct((S, W), F32)] * 2,
        scratch_shapes=[pltpu.VMEM((rows + 2 * halo, LANES), F32)] * 4,
        compiler_params=_cp("parallel", "parallel"),
    )(k, v, q, q, q, do, do, do, lse, lse, lse, delta, delta, delta)


def _place():
    x, y, c = lax.axis_index("x"), lax.axis_index("y"), lax.axis_index("c")
    chips = [(1 - x, y), (x, 1 - y), (1 - x, 1 - y)]
    return x, y, c, chips


def _gather_shards(shards):
    n = len(shards)

    def body(*refs):
        ins, outs = refs[:n], refs[n:2 * n]
        send_sems, recv_sems, local_sems = refs[2 * n:]
        x, y, c, chips = _place()
        mine = 2 * x + y
        local = [pltpu.make_async_copy(ins[t], outs[t].at[mine], local_sems.at[t]) for t in range(n)]
        for cp in local:
            cp.start()
        sends = []
        for t in range(n):
            for k, (px, py) in enumerate(chips):
                cp = pltpu.make_async_remote_copy(
                    src_ref=ins[t], dst_ref=outs[t].at[mine], send_sem=send_sems.at[t, k], recv_sem=recv_sems.at[t, k],
                    device_id=(px, py, c), device_id_type=MESH)
                cp.start()
                sends.append(cp)
        for t in range(n):
            for k, (px, py) in enumerate(chips):
                pltpu.make_async_remote_copy(
                    src_ref=ins[t], dst_ref=outs[t].at[2 * px + py], send_sem=send_sems.at[t, k],
                    recv_sem=recv_sems.at[t, k], device_id=(px, py, c), device_id_type=MESH).wait_recv()
        for cp in sends:
            cp.wait_send()
        for cp in local:
            cp.wait()

    return pl.pallas_call(
        body, name="gather_weight_shards", in_specs=[ANY] * n, out_specs=[ANY] * n,
        out_shape=[jax.ShapeDtypeStruct((N_SHARDS,) + s.shape, s.dtype) for s in shards],
        scratch_shapes=[pltpu.SemaphoreType.DMA((n, 3)), pltpu.SemaphoreType.DMA((n, 3)), pltpu.SemaphoreType.DMA((n,))],
    )(*shards)


def _scatter_grads(bufs):
    n = len(bufs)

    def body(*refs):
        gfs, gbs = refs[:n], refs[n:2 * n]
        owns, recvs = refs[2 * n:3 * n], refs[3 * n:4 * n]
        send_sems, recv_sems, local_sems = refs[4 * n:]
        x, y, c, chips = _place()
        mine = 2 * x + y
        local = [pltpu.make_async_copy(gfs[t].at[mine], owns[t], local_sems.at[t]) for t in range(n)]
        for cp in local:
            cp.start()
        sends = []
        for t in range(n):
            for k, (px, py) in enumerate(chips):
                cp = pltpu.make_async_remote_copy(
                    src_ref=gbs[t].at[2 * px + py], dst_ref=recvs[t].at[k], send_sem=send_sems.at[t, k],
                    recv_sem=recv_sems.at[t, k], device_id=(px, py, c), device_id_type=MESH)
                cp.start()
                sends.append(cp)
        for cp in sends:
            cp.wait_recv()
        for cp in sends:
            cp.wait_send()
        for cp in local:
            cp.wait()

    shp = [b[0].shape[1:] for b in bufs]
    outs = pl.pallas_call(
        body, name="scatter_weight_grads", in_specs=[ANY] * (2 * n), out_specs=[ANY] * (2 * n),
        out_shape=[jax.ShapeDtypeStruct(s, F32) for s in shp] + [jax.ShapeDtypeStruct((3,) + s, BF16) for s in shp],
        scratch_shapes=[pltpu.SemaphoreType.DMA((n, 3)), pltpu.SemaphoreType.DMA((n, 3)), pltpu.SemaphoreType.DMA((n,))],
    )(*[b[0] for b in bufs], *[b[1] for b in bufs])
    return outs[:n], outs[n:]


def _swap_with_sibling(parts):
    n = len(parts)

    def body(*refs):
        ins, outs = refs[:n], refs[n:2 * n]
        send_sems, recv_sems = refs[2 * n:]
        x, y, c, _ = _place()
        cps = [pltpu.make_async_remote_copy(src_ref=ins[t], dst_ref=outs[t], send_sem=send_sems.at[t], recv_sem=recv_sems.at[t],
                                            device_id=(x, y, 1 - c), device_id_type=MESH) for t in range(n)]
        for cp in cps:
            cp.start()
        for cp in cps:
            cp.wait_recv()
        for cp in cps:
            cp.wait_send()

    return pl.pallas_call(
        body, name="swap_partial_grads", in_specs=[ANY] * n, out_specs=[ANY] * n,
        out_shape=[jax.ShapeDtypeStruct(p.shape, p.dtype) for p in parts],
        scratch_shapes=[pltpu.SemaphoreType.DMA((n,)), pltpu.SemaphoreType.DMA((n,))],
    )(*parts)


def _allreduce_small(v):
    rows = v.shape[0]

    def body(v_ref, out_ref, buf, send_sems, recv_sems):
        x, y, c, chips = _place()
        me, sibling = (x, y, c), (x, y, 1 - c)

        def slot(px, py, pc):
            return buf.at[4 * px + 2 * py + pc]

        def copy(k, block, to, src=None):
            return pltpu.make_async_remote_copy(
                src_ref=slot(*block) if src is None else src, dst_ref=slot(*block), send_sem=send_sems.at[k],
                recv_sem=recv_sems.at[k], device_id=to, device_id_type=MESH)

        slot(*me)[...] = v_ref[...]
        first = [copy(0, me, sibling, src=v_ref)] + [copy(1 + j, me, (*chip, c), src=v_ref) for j, chip in enumerate(chips)]
        for cp in first:
            cp.start()
        passed = [copy(4 + j, (*chip, c), sibling) for j, chip in enumerate(chips)]
        for j, chip in enumerate(chips):
            copy(1 + j, (*chip, c), me).wait_recv()
            passed[j].start()
        copy(0, sibling, me).wait_recv()
        for j, chip in enumerate(chips):
            copy(4 + j, (*chip, 1 - c), me).wait_recv()
        for cp in first + passed:
            cp.wait_send()
        acc = buf[0]
        for k in range(1, 8):
            acc = acc + buf[k]
        out_ref[...] = acc

    return pl.pallas_call(
        body, name="allreduce_small_grads",
        in_specs=[pl.BlockSpec(memory_space=pltpu.VMEM)], out_specs=pl.BlockSpec(memory_space=pltpu.VMEM),
        out_shape=jax.ShapeDtypeStruct((rows, LANES), F32),
        scratch_shapes=[pltpu.VMEM((8, rows, LANES), F32), pltpu.SemaphoreType.DMA((7,)), pltpu.SemaphoreType.DMA((7,))],
        compiler_params=pltpu.CompilerParams(vmem_limit_bytes=VMEM_LIMIT_BYTES),
    )(v)


MM_TM = 1024


def _sq_relu_epilogue(acc):
    r = jnp.maximum(acc, 0.0)
    return acc, r * r


def _add_epilogue(acc, x):
    return (acc + x,)


def _sq_relu_grad_epilogue(acc, a):
    return (acc * (2.0 * jnp.maximum(a.astype(F32), 0.0)),)


def _local_step(x, target, p, wg):
    S, D = x.shape
    depth = p["mix_norm_g"].shape[0]
    n_even = (depth + 1) // 2
    mix_g3 = p["mix_norm_g"].reshape(depth, 1, D)
    mlp_g3 = p["mlp_norm_g"].reshape(depth, 1, D)
    vec3 = lambda t: t.reshape(t.shape[0], 1, t.shape[1])
    spw16 = p["a_spatial_w"].astype(BF16)
    spw16_t = jnp.swapaxes(spw16, 2, 3)
    bias_full = jnp.repeat(jnp.swapaxes(p["a_spatial_b"], 1, 2), HEAD_DIM, axis=2)
    vn_g, vn_b, cn_g, cn_b, cb3 = (vec3(p[k]) for k in ("a_vnorm_g", "a_vnorm_b", "b_norm_g", "b_norm_b", "b_conv_b"))
    tables = _rope_tables(S)
    gq = jnp.tile(p["c_q_norm_g"], (1, 2))
    gk = jnp.tile(p["c_k_norm_g"], (1, 2))

    saved = []
    for layer in range(depth):
        i = layer // 2
        rec = {"x_mix": x}
        h = _rms_fwd(f"mix_norm_{layer}", x, mix_g3, layer)
        rec["h_mix"] = h
        if layer % 2 == 0:
            (z,) = _mm_ngroup(f"ab_in_{layer}", h, wg["ab_w_in"], i, nt=False, tm=MM_TM, out_dtypes=[F32])
            gconv = _glu_conv_fwd(f"glu_conv_{layer}", z, wg["b_conv_w"], cb3, i)
            cat = _ab_tail_fwd(f"ab_tail_{layer}", z, gconv, spw16, bias_full, vn_g, vn_b, cn_g, cn_b, i)
            (x,) = _mm_kgroup(f"ab_out_{layer}", cat, wg["ab_w_out"], i, nt=False, tm=MM_TM, out_dtypes=[F32],
                              extras=(x,), epilogue=_add_epilogue)
            rec.update(z=z, gconv=gconv, cat=cat)
        else:
            (qkv,) = _mm_ngroup(f"c_qkv_{layer}", h, wg["c_w_qkv"], i, nt=False, tm=MM_TM, out_dtypes=[F32])
            qn, kn = _qk_fwd(f"qk_norm_rope_{layer}", qkv, gq[i:i + 1], gk[i:i + 1], tables)
            os, lses = zip(*[_attn_fwd(f"attn_d{d}_{layer}", qn, kn, qkv, V_COL, d) for d in PATTERN_DILATIONS])
            o, lse = _attn_merge(os, lses)
            (x,) = _mm_kgroup(f"c_out_{layer}", o, wg["c_w_out"], i, nt=False, tm=MM_TM, out_dtypes=[F32],
                              extras=(x,), epilogue=_add_epilogue)
            rec.update(qkv=qkv, qn=qn, kn=kn, o=o, lse=lse)
        rec["x_mlp"] = x
        h = _rms_fwd(f"mlp_norm_{layer}", x, mlp_g3, layer)
        a, hsq = _mm_ngroup(f"mlp_up_{layer}", h, wg["mlp_w1"], layer, nt=False, tm=MM_TM, out_dtypes=[BF16, BF16],
                            epilogue=_sq_relu_epilogue)
        (x,) = _mm_kgroup(f"mlp_down_{layer}", hsq, wg["mlp_w2"], layer, nt=False, tm=MM_TM, out_dtypes=[F32],
                          extras=(x,), epilogue=_add_epilogue)
        rec.update(h_mlp=h, a=a, hsq=hsq)
        saved.append(rec)

    dx, loss_row = _loss_grad(x, target)

    bufs = {k: (lax.empty(w.shape, F32), lax.empty(w.shape, BF16)) for k, w in wg.items()}
    small = {k: [None] * v.shape[0] for k, v in p.items()}
    for layer in reversed(range(depth)):
        i = layer // 2
        rec = saved[layer]
        (da,) = _mm_ngroup(f"mlp_down_dgrad_{layer}", dx, wg["mlp_w2"], layer, nt=True, tm=MM_TM, out_dtypes=[BF16],
                           extras=(rec["a"],), epilogue=_sq_relu_grad_epilogue)
        bufs["mlp_w2"] = _wgrad(f"mlp_down_wgrad_{layer}", rec["hsq"], dx, layer, bufs["mlp_w2"], a_group=True, tm=MM_TM)
        bufs["mlp_w1"] = _wgrad(f"mlp_up_wgrad_{layer}", rec["h_mlp"], da, layer, bufs["mlp_w1"], a_group=False, tm=MM_TM)
        (dh,) = _mm_kgroup(f"mlp_up_dgrad_{layer}", da, wg["mlp_w1"], layer, nt=True, tm=MM_TM, out_dtypes=[F32])
        dx, small["mlp_norm_g"][layer] = _rms_bwd(f"mlp_norm_bwd_{layer}", rec["x_mlp"], mlp_g3, layer, dh, dx)
        if layer % 2 == 0:
            (dcat,) = _mm_ngroup(f"ab_out_dgrad_{layer}", dx, wg["ab_w_out"], i, nt=True, tm=MM_TM, out_dtypes=[F32])
            bufs["ab_w_out"] = _wgrad(f"ab_out_wgrad_{layer}", rec["cat"], dx, i, bufs["ab_w_out"], a_group=True, tm=MM_TM)
            dz, dgconv, dspw, dbias, dvg, dvb, dcg, dcb = _ab_tail_bwd(
                f"ab_tail_bwd_{layer}", rec["z"], rec["gconv"], dcat, spw16, spw16_t, bias_full, vn_g, vn_b, cn_g, cn_b, i)
            dz, gf, gb, dcbias = _glu_conv_bwd(f"glu_conv_bwd_{layer}", rec["z"], dgconv, dz, wg["b_conv_w"], i, bufs["b_conv_w"])
            bufs["b_conv_w"] = (gf, gb)
            small["a_spatial_w"][i] = dspw
            small["a_spatial_b"][i] = _fold_bias(dbias)[:, :A_GROUPS].T
            for k, val in (("a_vnorm_g", dvg), ("a_vnorm_b", dvb), ("b_norm_g", dcg), ("b_norm_b", dcb), ("b_conv_b", dcbias)):
                small[k][i] = val
            bufs["ab_w_in"] = _wgrad(f"ab_in_wgrad_{layer}", rec["h_mix"], dz, i, bufs["ab_w_in"], a_group=False, tm=MM_TM)
            (dh,) = _mm_kgroup(f"ab_in_dgrad_{layer}", dz, wg["ab_w_in"], i, nt=True, tm=MM_TM, out_dtypes=[F32])
        else:
            (do,) = _mm_ngroup(f"c_out_dgrad_{layer}", dx, wg["c_w_out"], i, nt=True, tm=MM_TM, out_dtypes=[F32])
            bufs["c_w_out"] = _wgrad(f"c_out_wgrad_{layer}", rec["o"], dx, i, bufs["c_w_out"], a_group=True, tm=MM_TM)
            delta = _attn_delta(do, rec["o"])
            attn_args = (rec["qn"], rec["kn"], rec["qkv"], V_COL, do, rec["lse"], delta)
            dqs = [_attn_bwd_q(f"attn_bwd_q_d{d}_{layer}", *attn_args, d) for d in PATTERN_DILATIONS]
            dks, dvs = zip(*[_attn_bwd_kv(f"attn_bwd_kv_d{d}_{layer}", *attn_args, d) for d in PATTERN_DILATIONS])
            dqkv, dgq, dgk = _qk_bwd(f"qk_norm_rope_bwd_{layer}", rec["qkv"], gq[i:i + 1], gk[i:i + 1], tables, dqs, dks, dvs)
            small["c_q_norm_g"][i] = dgq[:, :HEAD_DIM]
            small["c_k_norm_g"][i] = dgk[:, :HEAD_DIM]
            bufs["c_w_qkv"] = _wgrad(f"c_qkv_wgrad_{layer}", rec["h_mix"], dqkv, i, bufs["c_w_qkv"], a_group=False, tm=MM_TM)
            (dh,) = _mm_kgroup(f"c_qkv_dgrad_{layer}", dqkv, wg["c_w_qkv"], i, nt=True, tm=MM_TM, out_dtypes=[F32])
        dx, small["mix_norm_g"][layer] = _rms_bwd(f"mix_norm_bwd_{layer}", rec["x_mix"], mix_g3, layer, dh, dx)

    small = {k: jnp.stack([g.reshape(p[k].shape[1:]) for g in v]) for k, v in small.items()}
    return loss_row, dx, small, bufs


SHARDED = ("mlp_w1", "mlp_w2", "ab_w_in", "b_conv_w", "ab_w_out", "c_w_qkv", "c_w_out")
SMALL = ("mix_norm_g", "mlp_norm_g", "a_spatial_w", "a_spatial_b", "a_vnorm_g", "a_vnorm_b", "b_conv_b", "b_norm_g",
         "b_norm_b", "c_q_norm_g", "c_k_norm_g")
WEIGHTS = ("mix_norm_g", "mlp_norm_g", "mlp_w1", "mlp_w2", "ab_w_in", "a_spatial_w", "a_spatial_b", "a_vnorm_g",
           "a_vnorm_b", "b_conv_w", "b_conv_b", "b_norm_g", "b_norm_b", "ab_w_out", "c_w_qkv", "c_q_norm_g",
           "c_k_norm_g", "c_w_out")


def _pack(parts):
    flat = jnp.concatenate([parts[k].reshape(-1) for k in SMALL])
    rows = -(-flat.shape[0] // (256 * LANES)) * 256
    return jnp.pad(flat, (0, rows * LANES - flat.shape[0])).reshape(rows, LANES)


def _unpack(packed, like):
    flat = packed.reshape(-1)
    out, off = {}, 0
    for k in SMALL:
        n = like[k].size
        out[k] = flat[off:off + n].reshape(like[k].shape)
        off += n
    return out


def kernel(x, mix_norm_g, mlp_norm_g, mlp_w1, mlp_w2, ab_w_in, a_spatial_w, a_spatial_b, a_vnorm_g, a_vnorm_b, b_conv_w, b_conv_b, b_norm_g, b_norm_b, ab_w_out, c_w_qkv, c_q_norm_g, c_k_norm_g, c_w_out, loss_target, m_mix_norm_g, m_mlp_norm_g, m_mlp_w1, m_mlp_w2, m_ab_w_in, m_a_spatial_w, m_a_spatial_b, m_a_vnorm_g, m_a_vnorm_b, m_b_conv_w, m_b_conv_b, m_b_norm_g, m_b_norm_b, m_ab_w_out, m_c_w_qkv, m_c_q_norm_g, m_c_k_norm_g, m_c_w_out, v_mix_norm_g, v_mlp_norm_g, v_mlp_w1, v_mlp_w2, v_ab_w_in, v_a_spatial_w, v_a_spatial_b, v_a_vnorm_g, v_a_vnorm_b, v_b_conv_w, v_b_conv_b, v_b_norm_g, v_b_norm_b, v_ab_w_out, v_c_w_qkv, v_c_q_norm_g, v_c_k_norm_g, v_c_w_out):
    w = dict(mix_norm_g=mix_norm_g, mlp_norm_g=mlp_norm_g, mlp_w1=mlp_w1, mlp_w2=mlp_w2, ab_w_in=ab_w_in,
             a_spatial_w=a_spatial_w, a_spatial_b=a_spatial_b, a_vnorm_g=a_vnorm_g, a_vnorm_b=a_vnorm_b,
             b_conv_w=b_conv_w, b_conv_b=b_conv_b, b_norm_g=b_norm_g, b_norm_b=b_norm_b, ab_w_out=ab_w_out,
             c_w_qkv=c_w_qkv, c_q_norm_g=c_q_norm_g, c_k_norm_g=c_k_norm_g, c_w_out=c_w_out)
    m = dict(mix_norm_g=m_mix_norm_g, mlp_norm_g=m_mlp_norm_g, mlp_w1=m_mlp_w1, mlp_w2=m_mlp_w2, ab_w_in=m_ab_w_in,
             a_spatial_w=m_a_spatial_w, a_spatial_b=m_a_spatial_b, a_vnorm_g=m_a_vnorm_g, a_vnorm_b=m_a_vnorm_b,
             b_conv_w=m_b_conv_w, b_conv_b=m_b_conv_b, b_norm_g=m_b_norm_g, b_norm_b=m_b_norm_b, ab_w_out=m_ab_w_out,
             c_w_qkv=m_c_w_qkv, c_q_norm_g=m_c_q_norm_g, c_k_norm_g=m_c_k_norm_g, c_w_out=m_c_w_out)
    v = dict(mix_norm_g=v_mix_norm_g, mlp_norm_g=v_mlp_norm_g, mlp_w1=v_mlp_w1, mlp_w2=v_mlp_w2, ab_w_in=v_ab_w_in,
             a_spatial_w=v_a_spatial_w, a_spatial_b=v_a_spatial_b, a_vnorm_g=v_a_vnorm_g, a_vnorm_b=v_a_vnorm_b,
             b_conv_w=v_b_conv_w, b_conv_b=v_b_conv_b, b_norm_g=v_b_norm_g, b_norm_b=v_b_norm_b, ab_w_out=v_ab_w_out,
             c_w_qkv=v_c_w_qkv, c_q_norm_g=v_c_q_norm_g, c_k_norm_g=v_c_k_norm_g, c_w_out=v_c_w_out)

    S, D = x.shape[1], x.shape[2]
    shards = [w[k] if k == "b_conv_w" else w[k].astype(BF16) for k in SHARDED]
    wg = dict(zip(SHARDED, _gather_shards(shards)))
    small_params = {k: w[k] for k in SMALL}
    loss_row, dx, small_grads, bufs = _local_step(x.reshape(S, D), loss_target.reshape(S, D), small_params, wg)

    loss = lax.psum(loss_row[0, 0], ("x", "y", "c"))

    owns, recvs = _scatter_grads([bufs[k] for k in SHARDED])
    partial = [_sum4(f"sum_chips_{k}", o, r) for k, o, r in zip(SHARDED, owns, recvs)]
    other = _swap_with_sibling(partial)
    grads, deltas, new_m, new_v = {}, {}, {}, {}
    for k, mine, theirs in zip(SHARDED, partial, other):
        grads[k], deltas[k], new_m[k], new_v[k] = _adamw(f"adamw_{k}", w[k], m[k], v[k], [mine, theirs])

    g_small = _allreduce_small(_pack(small_grads))
    outs = _adamw("adamw_small", _pack(small_params), _pack({k: m[k] for k in SMALL}), _pack({k: v[k] for k in SMALL}), [g_small])
    for d_, packed in zip((grads, deltas, new_m, new_v), outs):
        d_.update(_unpack(packed, small_params))

    return (loss, dx.reshape(1, S, D), *[grads[k] for k in WEIGHTS], *[deltas[k] for k in WEIGHTS],
            *[new_m[k] for k in WEIGHTS], *[new_v[k] for k in WEIGHTS])
```

```python
import functools

import jax
import jax.numpy as jnp
from jax import lax
from jax.experimental import pallas as pl
from jax.experimental.pallas import tpu as pltpu

F32, BF16 = jnp.float32, jnp.bfloat16
MESH = pl.DeviceIdType.MESH
ANY = pl.BlockSpec(memory_space=pl.ANY)

VMEM_LIMIT_BYTES = 56 * 1024 * 1024
LANES = 128
ELEMENTWISE_ROWS = 256

EPS = 1e-6
NEG = -1e30
HEAD_DIM = 64
N_HEADS = 16
CHUNK = 128
A_GROUPS = 8
CONV_WIDTH = 31
CONV_HALO = 16
BAND = 64
PATTERN_DILATIONS = (1, 4, 16)
ROT_DIM = 16
ROPE_THETA = 500000.0
N_SHARDS = 4

ADAM_LR, ADAM_B1, ADAM_B2, ADAM_EPS, ADAM_WD, ADAM_STEP = 0.001, 0.9, 0.999, 1e-08, 0.01, 10


def _cp(*sem):
    return pltpu.CompilerParams(dimension_semantics=sem, vmem_limit_bytes=VMEM_LIMIT_BYTES)


def _tile(n, pref):
    t = min(n, pref)
    assert n % t == 0, (n, pref)
    return t


def _dot(a, b, ca, cb):
    return lax.dot_general(a, b, (((ca,), (cb,)), ((), ())), preferred_element_type=F32)


def _mm_ngroup(name, a, w, *, nt, tm, out_dtypes, extras=(), epilogue=None, anchor=None):
    M, K = a.shape
    G, R, C = w.shape
    nw = R if nt else C
    assert K == (C if nt else R)
    tm = _tile(M, tm)
    n_ex = len(extras)
    anchors = [] if anchor is None else [anchor]

    def body(a_ref, w_ref, *rest):
        rest = rest[len(anchors):]
        acc = _dot(a_ref[...].astype(BF16), w_ref[...], 1, 1 if nt else 0)
        res = epilogue(acc, *[e[...] for e in rest[:n_ex]]) if epilogue else (acc,)
        for o_ref, r in zip(rest[n_ex:], res):
            o_ref[...] = r.astype(o_ref.dtype)

    blk = pl.BlockSpec((tm, nw), lambda m, g: (m, g))
    return pl.pallas_call(
        body, name=name, grid=(M // tm, G),
        in_specs=[pl.BlockSpec((tm, K), lambda m, g: (m, 0)), pl.BlockSpec((None, R, C), lambda m, g: (g, 0, 0))]
        + [pl.BlockSpec((8, LANES), lambda m, g: (0, 0))] * len(anchors) + [blk] * n_ex,
        out_specs=[blk] * len(out_dtypes),
        out_shape=[jax.ShapeDtypeStruct((M, G * nw), dt) for dt in out_dtypes],
        compiler_params=_cp("parallel", "parallel"),
    )(a, w, *anchors, *extras)


def _mm_kgroup(name, a, w, *, nt, tm, out_dtypes, extras=(), epilogue=None):
    G, R, C = w.shape
    kw, N = (C, R) if nt else (R, C)
    if a.ndim == 3:
        M = a.shape[1]
        assert a.shape[0] == G and a.shape[2] == kw
    else:
        M = a.shape[0]
        assert a.shape[1] == G * kw
    tm = _tile(M, tm)
    n_ex = len(extras)
    a_spec = (pl.BlockSpec((None, tm, kw), lambda m, g: (g, m, 0)) if a.ndim == 3
              else pl.BlockSpec((tm, kw), lambda m, g: (m, g)))

    def body(a_ref, w_ref, *rest):
        acc_ref = rest[-1]
        g = pl.program_id(1)
        part = _dot(a_ref[...].astype(BF16), w_ref[...], 1, 1 if nt else 0)

        @pl.when(g == 0)
        def _():
            acc_ref[...] = part

        @pl.when(g > 0)
        def _():
            acc_ref[...] += part

        @pl.when(g == G - 1)
        def _():
            acc = acc_ref[...]
            res = epilogue(acc, *[e[...] for e in rest[:n_ex]]) if epilogue else (acc,)
            for o_ref, r in zip(rest[n_ex:-1], res):
                o_ref[...] = r.astype(o_ref.dtype)

    blk = pl.BlockSpec((tm, N), lambda m, g: (m, 0))
    return pl.pallas_call(
        body, name=name, grid=(M // tm, G),
        in_specs=[a_spec, pl.BlockSpec((None, R, C), lambda m, g: (g, 0, 0))] + [blk] * n_ex,
        out_specs=[blk] * len(out_dtypes),
        out_shape=[jax.ShapeDtypeStruct((M, N), dt) for dt in out_dtypes],
        scratch_shapes=[pltpu.VMEM((tm, N), F32)],
        compiler_params=_cp("parallel", "arbitrary"),
    )(a, w, *extras)


def _wgrad(name, a, b, shape, *, a_group, tm):
    G, R, C = shape
    M = a.shape[0]
    tm = _tile(M, tm)
    n_m = M // tm

    def body(a_ref, b_ref, gf_ref, gb_ref):
        m = pl.program_id(1)
        part = _dot(a_ref[...].astype(BF16), b_ref[...].astype(BF16), 0, 0)

        @pl.when(m == 0)
        def _():
            gf_ref[...] = part

        @pl.when(m > 0)
        def _():
            gf_ref[...] += part

        @pl.when(m == n_m - 1)
        def _():
            gb_ref[...] = gf_ref[...].astype(BF16)

    a_spec = pl.BlockSpec((tm, R), (lambda g, m: (m, g)) if a_group else (lambda g, m: (m, 0)))
    if b.ndim == 3:
        assert not a_group
        b_spec = pl.BlockSpec((None, tm, C), lambda g, m: (g, m, 0))
    else:
        b_spec = pl.BlockSpec((tm, C), (lambda g, m: (m, 0)) if a_group else (lambda g, m: (m, g)))
    o_spec = pl.BlockSpec((None, R, C), lambda g, m: (g, 0, 0))
    return pl.pallas_call(
        body, name=name, grid=(G, n_m),
        in_specs=[a_spec, b_spec], out_specs=[o_spec, o_spec],
        out_shape=[jax.ShapeDtypeStruct(shape, F32), jax.ShapeDtypeStruct(shape, BF16)],
        compiler_params=_cp("parallel", "arbitrary"),
    )(a, b)


def _rms_fwd(name, x, g3, layer):
    S, D = x.shape
    tm = _tile(S, 512)

    def body(x_ref, g_ref, h_ref):
        xv = x_ref[...]
        r = lax.rsqrt(jnp.mean(xv * xv, axis=-1, keepdims=True) + EPS)
        h_ref[...] = (xv * r * g_ref[...]).astype(BF16)

    row = pl.BlockSpec((tm, D), lambda m: (m, 0))
    return pl.pallas_call(
        body, name=name, grid=(S // tm,),
        in_specs=[row, pl.BlockSpec((None, 1, D), lambda m: (layer, 0, 0))], out_specs=row,
        out_shape=jax.ShapeDtypeStruct((S, D), BF16), compiler_params=_cp("parallel"),
    )(x, g3)


def _rms_bwd(name, x, g3, layer, dh, dres):
    S, D = x.shape
    tm = _tile(S, 512)

    def body(x_ref, g_ref, dh_ref, dres_ref, dx_ref, dg_ref):
        xv = x_ref[...]
        d = dh_ref[...].astype(F32)
        r = lax.rsqrt(jnp.mean(xv * xv, axis=-1, keepdims=True) + EPS)
        xhat = xv * r
        dxhat = d * g_ref[...]
        dx_ref[...] = dres_ref[...] + r * (dxhat - xhat * jnp.mean(dxhat * xhat, axis=-1, keepdims=True))

        @pl.when(pl.program_id(0) == 0)
        def _():
            dg_ref[...] = jnp.zeros_like(dg_ref)

        dg_ref[...] += jnp.sum(d * xhat, axis=0, keepdims=True)

    row = pl.BlockSpec((tm, D), lambda m: (m, 0))
    return pl.pallas_call(
        body, name=name, grid=(S // tm,),
        in_specs=[row, pl.BlockSpec((None, 1, D), lambda m: (layer, 0, 0)), row, row],
        out_specs=[row, pl.BlockSpec((1, D), lambda m: (0, 0))],
        out_shape=[jax.ShapeDtypeStruct((S, D), F32), jax.ShapeDtypeStruct((1, D), F32)],
        compiler_params=_cp("arbitrary"),
    )(x, g3, dh, dres)


def _loss_grad(y, target):
    S, D = y.shape
    tm = _tile(S, 512)

    def body(y_ref, t_ref, dy_ref, l_ref):
        e = y_ref[...] - t_ref[...]
        dy_ref[...] = e * (1.0 / D)

        @pl.when(pl.program_id(0) == 0)
        def _():
            l_ref[...] = jnp.zeros_like(l_ref)

        l_ref[...] += (0.5 / D) * jnp.sum(jnp.sum(e * e, axis=1, keepdims=True), axis=0, keepdims=True)

    row = pl.BlockSpec((tm, D), lambda m: (m, 0))
    return pl.pallas_call(
        body, name="loss_grad", grid=(S // tm,), in_specs=[row, row],
        out_specs=[row, pl.BlockSpec((1, LANES), lambda m: (0, 0))],
        out_shape=[jax.ShapeDtypeStruct((S, D), F32), jax.ShapeDtypeStruct((1, LANES), F32)],
        compiler_params=_cp("arbitrary"),
    )(y, target)


def _adamw_math(w, m, v, g):
    m2 = ADAM_B1 * m + (1.0 - ADAM_B1) * g
    v2 = ADAM_B2 * v + (1.0 - ADAM_B2) * jnp.square(g)
    m_hat = m2 / (1.0 - ADAM_B1 ** ADAM_STEP)
    v_hat = v2 / (1.0 - ADAM_B2 ** ADAM_STEP)
    return g, -ADAM_LR * (m_hat / (jnp.sqrt(v_hat) + ADAM_EPS) + ADAM_WD * w), m2, v2


def _row_tile(rows):
    return _tile(rows, ELEMENTWISE_ROWS) if rows % ELEMENTWISE_ROWS == 0 else rows


def _adamw(name, w, m, v, g):
    rows, C = w.shape
    tr = _row_tile(rows)

    def body(w_ref, m_ref, v_ref, g_in, g_ref, d_ref, nm_ref, nv_ref):
        for o_ref, val in zip((g_ref, d_ref, nm_ref, nv_ref), _adamw_math(w_ref[...], m_ref[...], v_ref[...], g_in[...])):
            o_ref[...] = val

    blk = pl.BlockSpec((tr, C), lambda i: (i, 0))
    return pl.pallas_call(
        body, name=name, grid=(rows // tr,), in_specs=[blk] * 4, out_specs=[blk] * 4,
        out_shape=[jax.ShapeDtypeStruct((rows, C), F32)] * 4, compiler_params=_cp("parallel"),
    )(w, m, v, g)


def _adamw_layer(name, w, m, v, layer, mine, theirs, outs):
    _, R, C = w.shape
    tr = _row_tile(R)

    def body(w_ref, m_ref, v_ref, a_ref, b_ref, *rest):
        g = a_ref[...] + b_ref[...]
        for o_ref, val in zip(rest[4:], _adamw_math(w_ref[...], m_ref[...], v_ref[...], g)):
            o_ref[...] = val

    st = pl.BlockSpec((None, tr, C), lambda i: (layer, i, 0))
    part = pl.BlockSpec((tr, C), lambda i: (i, 0))
    return pl.pallas_call(
        body, name=name, grid=(R // tr,), in_specs=[st] * 3 + [part] * 2 + [ANY] * 4, out_specs=[st] * 4,
        out_shape=[jax.ShapeDtypeStruct(w.shape, F32)] * 4, input_output_aliases={5 + j: j for j in range(4)},
        compiler_params=_cp("parallel"),
    )(w, m, v, mine, theirs, *outs)


def _sum4(name, gf, recv, mine):
    _, R, C = gf.shape
    tr = _row_tile(R)

    def body(mine_ref, o_ref, r_ref, out_ref):
        acc = o_ref[...]
        for k in range(3):
            acc = acc + r_ref[k].astype(F32)
        out_ref[...] = acc

    return pl.pallas_call(
        body, name=name,
        grid_spec=pltpu.PrefetchScalarGridSpec(
            num_scalar_prefetch=1, grid=(R // tr,),
            in_specs=[pl.BlockSpec((None, tr, C), lambda i, s: (s[0], i, 0)), pl.BlockSpec((3, tr, C), lambda i, s: (0, i, 0))],
            out_specs=pl.BlockSpec((tr, C), lambda i, s: (i, 0))),
        out_shape=jax.ShapeDtypeStruct((R, C), F32), compiler_params=_cp("parallel"),
    )(mine, gf, recv)


def _gelu(x):
    return x * (0.5 * (1.0 + jnp.tanh(0.7978845608028654 * (x + 0.044715 * (x * x * x)))))


def _layernorm(t, g, b):
    mu = jnp.mean(t, axis=-1, keepdims=True)
    var = jnp.mean(jnp.square(t - mu), axis=-1, keepdims=True)
    return (t - mu) * lax.rsqrt(var + EPS) * g + b


def _silu(x):
    return x * jax.nn.sigmoid(x)


def _a_value(zv, g, b):
    return _layernorm(_gelu(zv), g, b)


def _b_tail(gc, g, b):
    return _silu(_layernorm(gc, g, b))


def _first_head(shape):
    return lax.broadcasted_iota(jnp.int32, shape, len(shape) - 1) < HEAD_DIM


def _spatial_mix(spw_ref, vb, tm):
    first = _first_head((CHUNK, LANES))
    rows = []
    for n in range(tm // CHUNK):
        blocks = []
        for j in range(A_GROUPS // 2):
            vblk = vb[n * CHUNK:(n + 1) * CHUNK, j * LANES:(j + 1) * LANES]
            r0 = _dot(spw_ref[2 * j], vblk, 1, 0)
            r1 = _dot(spw_ref[2 * j + 1], vblk, 1, 0)
            blocks.append(jnp.where(first, r0, r1))
        rows.append(jnp.concatenate(blocks, axis=1))
    return jnp.concatenate(rows, axis=0) if len(rows) > 1 else rows[0]


def _ab_tail_fwd(name, z, gconv, spw, bias_full, vn_g, vn_b, cn_g, cn_b, layer):
    S = z.shape[0]
    AW = 512
    tm = _tile(S, 256)

    def body(zu_ref, zv_ref, gc_ref, spw_ref, bias_ref, vg_ref, vb_ref, cg_ref, cb_ref, cat_ref):
        u = _gelu(zu_ref[...])
        v = _a_value(zv_ref[...], vg_ref[...], vb_ref[...])
        sv = _spatial_mix(spw_ref, v.astype(BF16), tm) + jnp.tile(bias_ref[...], (tm // CHUNK, 1))
        cat_ref[:, :AW] = (u * sv).astype(BF16)
        cat_ref[:, AW:] = _b_tail(gc_ref[...], cg_ref[...], cb_ref[...]).astype(BF16)

    vec = pl.BlockSpec((None, 1, AW), lambda m: (layer, 0, 0))
    return pl.pallas_call(
        body, name=name, grid=(S // tm,),
        in_specs=[pl.BlockSpec((tm, AW), lambda m: (m, 0)), pl.BlockSpec((tm, AW), lambda m: (m, 1)),
                  pl.BlockSpec((tm, AW), lambda m: (m, 0)),
                  pl.BlockSpec((None, A_GROUPS, CHUNK, CHUNK), lambda m: (layer, 0, 0, 0)),
                  pl.BlockSpec((None, CHUNK, AW), lambda m: (layer, 0, 0)), vec, vec, vec, vec],
        out_specs=pl.BlockSpec((tm, 2 * AW), lambda m: (m, 0)),
        out_shape=jax.ShapeDtypeStruct((S, 2 * AW), BF16), compiler_params=_cp("parallel"),
    )(z, z, gconv, spw, bias_full, vn_g, vn_b, cn_g, cn_b)


def _ab_tail_bwd(name, z, gconv, dcat, spw, spw_t, bias_full, vn_g, vn_b, cn_g, cn_b, layer):
    S = z.shape[0]
    AW = 512
    tm = _tile(S, 256)
    n_chunks = tm // CHUNK

    def body(zu_ref, zv_ref, gc_ref, dcat_ref, spw_ref, spwt_ref, bias_ref, vg_ref, vb_ref, cg_ref, cb_ref,
             dz_ref, dgc_ref, dspw_ref, dbias_ref, dvg_ref, dvb_ref, dcg_ref, dcb_ref):
        @pl.when(pl.program_id(0) == 0)
        def _():
            for r in (dspw_ref, dbias_ref, dvg_ref, dvb_ref, dcg_ref, dcb_ref):
                r[...] = jnp.zeros_like(r)

        dya = dcat_ref[:, :AW]
        dyb = dcat_ref[:, AW:]
        u, u_vjp = jax.vjp(_gelu, zu_ref[...])
        v, v_vjp = jax.vjp(_a_value, zv_ref[...], vg_ref[...], vb_ref[...])
        vb16 = v.astype(BF16)
        sv = _spatial_mix(spw_ref, vb16, tm) + jnp.tile(bias_ref[...], (n_chunks, 1))
        (dzu,) = u_vjp(dya * sv)
        dsv = dya * u
        dsv16 = dsv.astype(BF16)
        dv = _spatial_mix(spwt_ref, dsv16, tm)
        dzv, dvg, dvb = v_vjp(dv)
        dz_ref[0] = dzu
        dz_ref[1] = dzv
        dvg_ref[...] += dvg
        dvb_ref[...] += dvb

        first = _first_head((CHUNK, LANES))
        zero = jnp.zeros((), BF16)
        dbias = jnp.zeros((CHUNK, AW), F32)
        for n in range(n_chunks):
            rows = slice(n * CHUNK, (n + 1) * CHUNK)
            dbias = dbias + dsv[rows]
            for j in range(A_GROUPS // 2):
                cols = slice(j * LANES, (j + 1) * LANES)
                dblk, vblk = dsv16[rows, cols], vb16[rows, cols]
                dspw_ref[2 * j] += _dot(jnp.where(first, dblk, zero), vblk, 1, 1)
                dspw_ref[2 * j + 1] += _dot(jnp.where(first, zero, dblk), vblk, 1, 1)
        dbias_ref[...] += dbias

        _, t_vjp = jax.vjp(_b_tail, gc_ref[...], cg_ref[...], cb_ref[...])
        dgc, dcg, dcb = t_vjp(dyb)
        dgc_ref[...] = dgc
        dcg_ref[...] += dcg
        dcb_ref[...] += dcb

    vec = pl.BlockSpec((None, 1, AW), lambda m: (layer, 0, 0))
    spw_spec = pl.BlockSpec((None, A_GROUPS, CHUNK, CHUNK), lambda m: (layer, 0, 0, 0))
    ovec = pl.BlockSpec((1, AW), lambda m: (0, 0))
    return pl.pallas_call(
        body, name=name, grid=(S // tm,),
        in_specs=[pl.BlockSpec((tm, AW), lambda m: (m, 0)), pl.BlockSpec((tm, AW), lambda m: (m, 1)),
                  pl.BlockSpec((tm, AW), lambda m: (m, 0)), pl.BlockSpec((tm, 2 * AW), lambda m: (m, 0)),
                  spw_spec, spw_spec, pl.BlockSpec((None, CHUNK, AW), lambda m: (layer, 0, 0)), vec, vec, vec, vec],
        out_specs=[pl.BlockSpec((2, tm, AW), lambda m: (0, m, 0)), pl.BlockSpec((tm, AW), lambda m: (m, 0)),
                   pl.BlockSpec((A_GROUPS, CHUNK, CHUNK), lambda m: (0, 0, 0)),
                   pl.BlockSpec((CHUNK, AW), lambda m: (0, 0)), ovec, ovec, ovec, ovec],
        out_shape=[jax.ShapeDtypeStruct((4, S, AW), F32), jax.ShapeDtypeStruct((S, AW), F32),
                   jax.ShapeDtypeStruct((A_GROUPS, CHUNK, CHUNK), F32), jax.ShapeDtypeStruct((CHUNK, AW), F32)]
                  + [jax.ShapeDtypeStruct((1, AW), F32)] * 4,
        compiler_params=_cp("arbitrary"),
    )(z, z, gconv, dcat, spw, spw_t, bias_full, vn_g, vn_b, cn_g, cn_b)


def _fold_bias(dbias_full):
    def body(d_ref, o_ref):
        d = d_ref[...]
        hi = d.astype(BF16)
        lo = (d - hi.astype(F32)).astype(BF16)
        r = lax.broadcasted_iota(jnp.int32, (512, LANES), 0)
        c = lax.broadcasted_iota(jnp.int32, (512, LANES), 1)
        fold = jnp.where(lax.shift_right_logical(r, 6) == c, 1.0, 0.0).astype(BF16)
        o_ref[...] = _dot(hi, fold, 1, 0) + _dot(lo, fold, 1, 0)

    return pl.pallas_call(body, name="fold_spatial_bias", out_shape=jax.ShapeDtypeStruct((CHUNK, LANES), F32))(dbias_full)


def _halo_specs(tm, n_halo_blocks, col):
    r = tm // CONV_HALO
    prev = pl.BlockSpec((CONV_HALO, LANES), lambda j, i: (jnp.maximum(i * r - 1, 0), col + j))
    cur = pl.BlockSpec((tm, LANES), lambda j, i: (i, col + j))
    nxt = pl.BlockSpec((CONV_HALO, LANES), lambda j, i: (jnp.minimum((i + 1) * r, n_halo_blocks - 1), col + j))
    return [prev, cur, nxt]


def _fill_halo(scr, prev, cur, nxt, tm, i, n_i):
    scr[0:CONV_HALO, :] = jnp.where(i > 0, prev, 0.0)
    scr[CONV_HALO:CONV_HALO + tm, :] = cur
    scr[CONV_HALO + tm:2 * CONV_HALO + tm, :] = jnp.where(i < n_i - 1, nxt, 0.0)


def _glu_conv_fwd(name, z, cw, cb3, layer):
    S = z.shape[0]
    tm = _tile(S, 512)
    n_i = S // tm
    pad = CONV_WIDTH // 2

    def body(vp, vc, vn, gp, gc, gn, w_ref, b_ref, out_ref, scr):
        i = pl.program_id(1)
        glu = lambda a, b: a[...] * jax.nn.sigmoid(b[...])
        _fill_halo(scr, glu(vp, gp), glu(vc, gc), glu(vn, gn), tm, i, n_i)
        acc = jnp.zeros((tm, LANES), F32)
        for j in range(CONV_WIDTH):
            acc = acc + w_ref[j:j + 1, :] * scr[pl.ds(CONV_HALO - pad + j, tm), :]
        out_ref[...] = acc + b_ref[...]

    return pl.pallas_call(
        body, name=name, grid=(4, n_i),
        in_specs=_halo_specs(tm, S // CONV_HALO, 8) + _halo_specs(tm, S // CONV_HALO, 12)
        + [pl.BlockSpec((None, CONV_WIDTH, LANES), lambda j, i: (j, 0, 0)),
           pl.BlockSpec((None, 1, LANES), lambda j, i: (layer, 0, j))],
        out_specs=pl.BlockSpec((tm, LANES), lambda j, i: (i, j)),
        out_shape=jax.ShapeDtypeStruct((S, 4 * LANES), F32),
        scratch_shapes=[pltpu.VMEM((tm + 2 * CONV_HALO, LANES), F32)],
        compiler_params=_cp("parallel", "parallel"),
    )(z, z, z, z, z, z, cw, cb3)


def _glu_conv_bwd(name, z, dgconv, dz, cw):
    S = z.shape[0]
    tm = _tile(S, 512)
    n_i = S // tm
    pad = CONV_WIDTH // 2

    def body(vp, vc, vn, gp, gc, gn, dp, dc, dn, w_ref, dz_in, dz_ref, gf_ref, gb_ref, db_ref, g_scr, d_scr):
        i = pl.program_id(1)
        sig = jax.nn.sigmoid(gc[...])
        _fill_halo(g_scr, vp[...] * jax.nn.sigmoid(gp[...]), vc[...] * sig, vn[...] * jax.nn.sigmoid(gn[...]), tm, i, n_i)
        _fill_halo(d_scr, dp[...], dc[...], dn[...], tm, i, n_i)

        @pl.when(i == 0)
        def _():
            gf_ref[...] = jnp.zeros_like(gf_ref)
            db_ref[...] = jnp.zeros_like(db_ref)

        d_cur = dc[...]
        dglu = jnp.zeros((tm, LANES), F32)
        for j in range(CONV_WIDTH):
            dglu = dglu + w_ref[j:j + 1, :] * d_scr[pl.ds(CONV_HALO + pad - j, tm), :]
            gf_ref[j:j + 1, :] += jnp.sum(d_cur * g_scr[pl.ds(CONV_HALO - pad + j, tm), :], axis=0, keepdims=True)
        db_ref[...] += jnp.sum(d_cur, axis=0, keepdims=True)
        dz_ref[0] = dglu * sig
        dz_ref[1] = dglu * vc[...] * sig * (1.0 - sig)

        @pl.when(i == n_i - 1)
        def _():
            gb_ref[...] = gf_ref[...].astype(BF16)

    w_spec = pl.BlockSpec((None, CONV_WIDTH, LANES), lambda j, i: (j, 0, 0))
    return pl.pallas_call(
        body, name=name, grid=(4, n_i),
        in_specs=_halo_specs(tm, S // CONV_HALO, 8) + _halo_specs(tm, S // CONV_HALO, 12)
        + _halo_specs(tm, S // CONV_HALO, 0) + [w_spec, ANY],
        out_specs=[pl.BlockSpec((2, tm, LANES), lambda j, i: (1, i, j)),
                   w_spec, w_spec, pl.BlockSpec((1, LANES), lambda j, i: (0, j))],
        out_shape=[jax.ShapeDtypeStruct(dz.shape, F32), jax.ShapeDtypeStruct(cw.shape, F32),
                   jax.ShapeDtypeStruct(cw.shape, BF16), jax.ShapeDtypeStruct((1, 4 * LANES), F32)],
        input_output_aliases={10: 0},
        scratch_shapes=[pltpu.VMEM((tm + 2 * CONV_HALO, LANES), F32)] * 2,
        compiler_params=_cp("parallel", "arbitrary"),
    )(z, z, z, z, z, z, dgconv, dgconv, dgconv, cw, dz)


def _seg_matrix(scale):
    r = lax.broadcasted_iota(jnp.int32, (LANES, LANES), 0)
    c = lax.broadcasted_iota(jnp.int32, (LANES, LANES), 1)
    return jnp.where(lax.shift_right_logical(r, 6) == lax.shift_right_logical(c, 6), scale, 0.0).astype(BF16)


def _seg_sum(x, seg):
    hi = x.astype(BF16)
    lo = (x - hi.astype(F32)).astype(BF16)
    return _dot(hi, seg, 1, 0) + _dot(lo, seg, 1, 0)


def _rope_tables(S):
    pos = jnp.arange(S, dtype=F32)
    inv_freq = ROPE_THETA ** (-jnp.arange(0, ROT_DIM, 2, dtype=F32) / ROT_DIM)
    ang = pos[:, None] * inv_freq[None, :]
    cos, sin = jnp.cos(ang), jnp.sin(ang)
    half = ROT_DIM // 2
    rest = HEAD_DIM - ROT_DIM
    one, zero = jnp.ones((S, rest), F32), jnp.zeros((S, rest), F32)
    zh = jnp.zeros((S, half), F32)
    c = jnp.concatenate([cos, cos, one], axis=1)
    sa = jnp.concatenate([-sin, zh, zero], axis=1)
    sb = jnp.concatenate([zh, sin, zero], axis=1)
    return [jnp.tile(t, (1, 2)) for t in (c, sa, sb)]


def _qk_fwd(name, qkv, gq, gk, tables):
    S = qkv.shape[0]
    W = N_HEADS * HEAD_DIM
    tm = _tile(S, 256)
    half = ROT_DIM // 2

    def body(q_ref, k_ref, gq_ref, gk_ref, c_ref, sa_ref, sb_ref, qn_ref, kn_ref):
        seg = _seg_matrix(1.0 / HEAD_DIM)
        c, sa, sb = c_ref[...], sa_ref[...], sb_ref[...]
        for t_ref, g_ref, o_ref in ((q_ref, gq_ref, qn_ref), (k_ref, gk_ref, kn_ref)):
            for blk in range(W // LANES):
                cols = slice(blk * LANES, (blk + 1) * LANES)
                t = t_ref[:, cols]
                y = t * lax.rsqrt(_seg_sum(t * t, seg) + EPS) * g_ref[...]
                o_ref[:, cols] = y * c + pltpu.roll(y, LANES - half, 1) * sa + pltpu.roll(y, half, 1) * sb

    row = lambda k: pl.BlockSpec((tm, W), lambda m: (m, k))
    gain = pl.BlockSpec((1, LANES), lambda m: (0, 0))
    tab = pl.BlockSpec((tm, LANES), lambda m: (m, 0))
    return pl.pallas_call(
        body, name=name, grid=(S // tm,),
        in_specs=[row(0), row(1), gain, gain, tab, tab, tab], out_specs=[row(0)] * 2,
        out_shape=[jax.ShapeDtypeStruct((S, W), F32)] * 2, compiler_params=_cp("parallel"),
    )(qkv, qkv, gq, gk, *tables)


def _qk_bwd(name, qkv, gq, gk, tables, dqs, dks, dvs):
    S = qkv.shape[0]
    W = N_HEADS * HEAD_DIM
    tm = _tile(S, 256)
    half = ROT_DIM // 2
    n_p = len(dqs)

    def body(q_ref, k_ref, gq_ref, gk_ref, c_ref, sa_ref, sb_ref, *rest):
        dq_refs, dk_refs, dv_refs = rest[:n_p], rest[n_p:2 * n_p], rest[2 * n_p:3 * n_p]
        dqkv_ref, dgq_ref, dgk_ref = rest[3 * n_p:]

        @pl.when(pl.program_id(0) == 0)
        def _():
            dgq_ref[...] = jnp.zeros_like(dgq_ref)
            dgk_ref[...] = jnp.zeros_like(dgk_ref)

        seg = _seg_matrix(1.0 / HEAD_DIM)
        r_i = lax.broadcasted_iota(jnp.int32, (LANES, LANES), 0)
        c_i = lax.broadcasted_iota(jnp.int32, (LANES, LANES), 1)
        same_dim = jnp.where((r_i & (HEAD_DIM - 1)) == (c_i & (HEAD_DIM - 1)), 1.0, 0.0).astype(BF16)
        c, sa, sb = c_ref[...], sa_ref[...], sb_ref[...]
        for idx, (t_ref, g_ref, d_refs, dg_ref) in enumerate(((q_ref, gq_ref, dq_refs, dgq_ref),
                                                              (k_ref, gk_ref, dk_refs, dgk_ref))):
            dg = jnp.zeros((1, LANES), F32)
            for blk in range(W // LANES):
                cols = slice(blk * LANES, (blk + 1) * LANES)
                dout = d_refs[0][:, cols]
                for r in d_refs[1:]:
                    dout = dout + r[:, cols]
                dy = dout * c + pltpu.roll(dout * sa, half, 1) + pltpu.roll(dout * sb, LANES - half, 1)
                t = t_ref[:, cols]
                r_ = lax.rsqrt(_seg_sum(t * t, seg) + EPS)
                xhat = t * r_
                dg = dg + jnp.sum(dy * xhat, axis=0, keepdims=True)
                dxhat = dy * g_ref[...]
                dt = r_ * (dxhat - xhat * _seg_sum(dxhat * xhat, seg))
                dqkv_ref[:, idx * W + blk * LANES: idx * W + (blk + 1) * LANES] = dt.astype(BF16)
            dg_ref[...] += _seg_sum(jnp.broadcast_to(dg, (8, LANES)), same_dim)[0:1]
        dv = dv_refs[0][...]
        for r in dv_refs[1:]:
            dv = dv + r[...]
        dqkv_ref[:, 2 * W:] = dv.astype(BF16)

    row = lambda k: pl.BlockSpec((tm, W), lambda m: (m, k))
    gain = pl.BlockSpec((1, LANES), lambda m: (0, 0))
    tab = pl.BlockSpec((tm, LANES), lambda m: (m, 0))
    return pl.pallas_call(
        body, name=name, grid=(S // tm,),
        in_specs=[row(0), row(1), gain, gain, tab, tab, tab] + [row(0)] * (3 * n_p),
        out_specs=[pl.BlockSpec((tm, 3 * W), lambda m: (m, 0)), gain, gain],
        out_shape=[jax.ShapeDtypeStruct((S, 3 * W), BF16), jax.ShapeDtypeStruct((1, LANES), F32),
                   jax.ShapeDtypeStruct((1, LANES), F32)],
        compiler_params=_cp("arbitrary"),
    )(qkv, qkv, gq, gk, *tables, *dqs, *dks, *dvs)


ATTN_BQ = 2 * BAND
ATTN_ROWS = 16 * ATTN_BQ
V_COL = 2 * N_HEADS * HEAD_DIM // LANES


def _attn_geometry(S, d):
    rows = min(ATTN_ROWS, S)
    halo = BAND * d
    assert rows % (ATTN_BQ * d) == 0 and S % rows == 0, (S, d)
    return rows, halo, rows // (ATTN_BQ * d)


def _attn_specs(S, d, col):
    rows, halo, _ = _attn_geometry(S, d)
    r = rows // halo
    n_h = S // halo
    prev = pl.BlockSpec((halo, LANES), lambda j, i: (jnp.maximum(i * r - 1, 0), col + j))
    cur = pl.BlockSpec((rows, LANES), lambda j, i: (i, col + j))
    nxt = pl.BlockSpec((halo, LANES), lambda j, i: (jnp.minimum((i + 1) * r, n_h - 1), col + j))
    return [prev, cur, nxt]


def _fill_window(scr, prev, cur, nxt, rows, halo):
    scr[0:halo, :] = prev[...]
    scr[halo:halo + rows, :] = cur[...]
    scr[halo + rows:2 * halo + rows, :] = nxt[...]


def _strided(ref, start, size, d):
    return ref[pl.ds(start, size, stride=d) if d > 1 else pl.ds(start, size), :]


def _band_masks(i, S, d, sb):
    rows, _, _ = _attn_geometry(S, d)
    L = S // d
    base = i * (rows // d) + sb * ATTN_BQ
    wk = ATTN_BQ + 2 * BAND
    row = lax.broadcasted_iota(jnp.int32, (ATTN_BQ, wk), 0)
    col = lax.broadcasted_iota(jnp.int32, (ATTN_BQ, wk), 1)
    lj = base - BAND + col
    valid = (jnp.abs(col - BAND - row) <= BAND) & (lj >= 0) & (lj < L)
    row_t = lax.broadcasted_iota(jnp.int32, (wk, ATTN_BQ), 0)
    col_t = lax.broadcasted_iota(jnp.int32, (wk, ATTN_BQ), 1)
    li = base - BAND + row_t
    valid_t = (jnp.abs(row_t - BAND - col_t) <= BAND) & (li >= 0) & (li < L)
    return valid, valid_t


def _attn_fwd(name, q, k, v, v_col, d):
    S, W = q.shape
    rows, halo, n_sb = _attn_geometry(S, d)
    wk = ATTN_BQ + 2 * BAND
    scale = HEAD_DIM ** -0.5

    def body(q_ref, kp, kc, kn, vp, vc, vn, o_ref, lse_ref, kw, vw):
        i = pl.program_id(1)
        _fill_window(kw, kp, kc, kn, rows, halo)
        _fill_window(vw, vp, vc, vn, rows, halo)
        first = _first_head((ATTN_BQ, LANES))
        zero = jnp.zeros((), BF16)
        for sb in range(n_sb):
            valid, _ = _band_masks(i, S, d, sb)
            for r in range(d):
                start = r + d * sb * ATTN_BQ
                qv = _strided(q_ref, start, ATTN_BQ, d).astype(BF16)
                kv = _strided(kw, start, wk, d).astype(BF16)
                vv = _strided(vw, start, wk, d).astype(BF16)
                o_h, lse_h = [], []
                for hm in (first, jnp.logical_not(first)):
                    s = jnp.where(valid, _dot(jnp.where(hm, qv, zero), kv, 1, 1) * scale, NEG)
                    mx = jnp.max(s, axis=-1, keepdims=True)
                    p = jnp.exp(s - mx)
                    den = jnp.sum(p, axis=-1, keepdims=True)
                    o_h.append(_dot(p.astype(BF16), vv, 1, 0) / den)
                    lse_h.append(mx + jnp.log(den))
                dst = pl.ds(start, ATTN_BQ, stride=d) if d > 1 else pl.ds(start, ATTN_BQ)
                o_ref[dst, :] = jnp.where(first, o_h[0], o_h[1])
                lse_ref[dst, :] = jnp.where(first, lse_h[0], lse_h[1])

    cur = _attn_specs(S, d, 0)[1]
    return pl.pallas_call(
        body, name=name, grid=(W // LANES, S // rows),
        in_specs=[cur] + _attn_specs(S, d, 0) + _attn_specs(S, d, v_col), out_specs=[cur, cur],
        out_shape=[jax.ShapeDtypeStruct((S, W), F32)] * 2,
        scratch_shapes=[pltpu.VMEM((rows + 2 * halo, LANES), F32)] * 2,
        compiler_params=_cp("parallel", "parallel"),
    )(q, k, k, k, v, v, v)


def _attn_merge(os, lses):
    S, W = os[0].shape
    tm = _tile(S, 256)
    n_p = len(os)

    def body(*refs):
        o_refs, l_refs = refs[:n_p], refs[n_p:2 * n_p]
        o_ref, lt_ref = refs[2 * n_p:]
        ls = [r[...] for r in l_refs]
        mx = functools.reduce(jnp.maximum, ls)
        es = [jnp.exp(l - mx) for l in ls]
        den = functools.reduce(lambda a, b: a + b, es)
        acc = es[0] * o_refs[0][...]
        for e, r in zip(es[1:], o_refs[1:]):
            acc = acc + e * r[...]
        o_ref[...] = (acc / den).astype(BF16)
        lt_ref[...] = mx + jnp.log(den)

    row = pl.BlockSpec((tm, W), lambda m: (m, 0))
    return pl.pallas_call(
        body, name="attn_merge", grid=(S // tm,), in_specs=[row] * (2 * n_p), out_specs=[row, row],
        out_shape=[jax.ShapeDtypeStruct((S, W), BF16), jax.ShapeDtypeStruct((S, W), F32)],
        compiler_params=_cp("parallel"),
    )(*os, *lses)


def _attn_delta(do, o):
    S, W = do.shape
    tm = _tile(S, 256)

    def body(do_ref, o_ref, dl_ref):
        seg = _seg_matrix(1.0)
        for blk in range(W // LANES):
            cols = slice(blk * LANES, (blk + 1) * LANES)
            dl_ref[:, cols] = _seg_sum(do_ref[:, cols] * o_ref[:, cols].astype(F32), seg)

    row = pl.BlockSpec((tm, W), lambda m: (m, 0))
    return pl.pallas_call(
        body, name="attn_delta", grid=(S // tm,), in_specs=[row, row], out_specs=row,
        out_shape=jax.ShapeDtypeStruct((S, W), F32), compiler_params=_cp("parallel"),
    )(do, o)


def _attn_bwd_q(name, q, k, v, v_col, do, lse, delta, d):
    S, W = q.shape
    rows, halo, n_sb = _attn_geometry(S, d)
    wk = ATTN_BQ + 2 * BAND
    scale = HEAD_DIM ** -0.5

    def body(q_ref, do_ref, l_ref, dl_ref, kp, kc, kn, vp, vc, vn, dq_ref, kw, vw):
        i = pl.program_id(1)
        _fill_window(kw, kp, kc, kn, rows, halo)
        _fill_window(vw, vp, vc, vn, rows, halo)
        first = _first_head((ATTN_BQ, LANES))
        zero = jnp.zeros((), BF16)
        for sb in range(n_sb):
            valid, _ = _band_masks(i, S, d, sb)
            for r in range(d):
                start = r + d * sb * ATTN_BQ
                qv = _strided(q_ref, start, ATTN_BQ, d).astype(BF16)
                dov = _strided(do_ref, start, ATTN_BQ, d).astype(BF16)
                lv = _strided(l_ref, start, ATTN_BQ, d)
                dlv = _strided(dl_ref, start, ATTN_BQ, d)
                kv = _strided(kw, start, wk, d).astype(BF16)
                vv = _strided(vw, start, wk, d).astype(BF16)
                dq_h = []
                for hh, hm in enumerate((first, jnp.logical_not(first))):
                    lane0 = hh * HEAD_DIM
                    s = jnp.where(valid, _dot(jnp.where(hm, qv, zero), kv, 1, 1) * scale, NEG)
                    p = jnp.exp(s - lv[:, lane0:lane0 + 1])
                    dp = _dot(jnp.where(hm, dov, zero), vv, 1, 1)
                    ds = p * (dp - dlv[:, lane0:lane0 + 1]) * scale
                    dq_h.append(_dot(ds.astype(BF16), kv, 1, 0))
                dst = pl.ds(start, ATTN_BQ, stride=d) if d > 1 else pl.ds(start, ATTN_BQ)
                dq_ref[dst, :] = jnp.where(first, dq_h[0], dq_h[1])

    cur = _attn_specs(S, d, 0)[1]
    return pl.pallas_call(
        body, name=name, grid=(W // LANES, S // rows),
        in_specs=[cur] * 4 + _attn_specs(S, d, 0) + _attn_specs(S, d, v_col), out_specs=cur,
        out_shape=jax.ShapeDtypeStruct((S, W), F32),
        scratch_shapes=[pltpu.VMEM((rows + 2 * halo, LANES), F32)] * 2,
        compiler_params=_cp("parallel", "parallel"),
    )(q, do, lse, delta, k, k, k, v, v, v)


def _attn_bwd_kv(name, q, k, v, v_col, do, lse, delta, d):
    S, W = q.shape
    rows, halo, n_sb = _attn_geometry(S, d)
    wk = ATTN_BQ + 2 * BAND
    scale = HEAD_DIM ** -0.5

    def body(k_ref, v_ref, qp, qc, qn, dop, doc, don, lp, lc, ln, dlp, dlc, dln, dk_ref, dv_ref, qw, dow, lw, dlw):
        i = pl.program_id(1)
        _fill_window(qw, qp, qc, qn, rows, halo)
        _fill_window(dow, dop, doc, don, rows, halo)
        _fill_window(lw, lp, lc, ln, rows, halo)
        _fill_window(dlw, dlp, dlc, dln, rows, halo)
        first = _first_head((ATTN_BQ, LANES))
        first_w = _first_head((wk, LANES))
        zero = jnp.zeros((), BF16)
        for sb in range(n_sb):
            _, valid_t = _band_masks(i, S, d, sb)
            for r in range(d):
                start = r + d * sb * ATTN_BQ
                kv = _strided(k_ref, start, ATTN_BQ, d).astype(BF16)
                vv = _strided(v_ref, start, ATTN_BQ, d).astype(BF16)
                qv = _strided(qw, start, wk, d).astype(BF16)
                dov = _strided(dow, start, wk, d).astype(BF16)
                lv = _strided(lw, start, wk, d)
                dlv = _strided(dlw, start, wk, d)
                dk_h, dv_h = [], []
                for hh, hm_w in enumerate((first_w, jnp.logical_not(first_w))):
                    lane0 = hh * HEAD_DIM
                    st = jnp.where(valid_t, _dot(jnp.where(hm_w, qv, zero), kv, 1, 1) * scale, NEG)
                    pt = jnp.exp(st - lv[:, lane0:lane0 + 1])
                    dv_h.append(_dot(pt.astype(BF16), dov, 0, 0))
                    dpt = _dot(jnp.where(hm_w, dov, zero), vv, 1, 1)
                    dst_ = pt * (dpt - dlv[:, lane0:lane0 + 1]) * scale
                    dk_h.append(_dot(dst_.astype(BF16), qv, 0, 0))
                dst = pl.ds(start, ATTN_BQ, stride=d) if d > 1 else pl.ds(start, ATTN_BQ)
                dk_ref[dst, :] = jnp.where(first, dk_h[0], dk_h[1])
                dv_ref[dst, :] = jnp.where(first, dv_h[0], dv_h[1])

    cur = _attn_specs(S, d, 0)[1]
    win = _attn_specs(S, d, 0)
    return pl.pallas_call(
        body, name=name, grid=(W // LANES, S // rows),
        in_specs=[cur, _attn_specs(S, d, v_col)[1]] + win * 4, out_specs=[cur, cur],
        out_shape=[jax.ShapeDtypeStruct((S, W), F32)] * 2,
        scratch_shapes=[pltpu.VMEM((rows + 2 * halo, LANES), F32)] * 4,
        compiler_params=_cp("parallel", "parallel"),
    )(k, v, q, q, q, do, do, do, lse, lse, lse, delta, delta, delta)


def _place():
    x, y, c = lax.axis_index("x"), lax.axis_index("y"), lax.axis_index("c")
    chips = [(1 - x, y), (x, 1 - y), (1 - x, 1 - y)]
    return x, y, c, chips


HBM = pl.BlockSpec(memory_space=pltpu.HBM)
SEM = pl.BlockSpec(memory_space=pltpu.SEMAPHORE)
DATAFLOW = pltpu.SideEffectType.DATAFLOW_SIDE_EFFECTING


def _exchange_copies(kind, srcs, dsts, send_sems, recv_sems):
    x, y, c, chips = _place()
    mine = 2 * x + y
    cps = []
    for t in range(len(srcs)):
        for k, (px, py) in enumerate(chips):
            src = srcs[t] if kind == "gather" else srcs[t].at[2 * px + py]
            dst = dsts[t].at[mine] if kind == "gather" else dsts[t].at[k]
            cps.append(pltpu.make_async_remote_copy(src_ref=src, dst_ref=dst, send_sem=send_sems.at[3 * t + k],
                                                    recv_sem=recv_sems.at[3 * t + k], device_id=(px, py, c), device_id_type=MESH))
    return cps


def _exchange_start(name, kind, groups):
    sizes = [len(g) for g in groups]
    n, n_g = sum(sizes), len(groups)

    def body(*refs):
        srcs, dsts = refs[:n], refs[n:2 * n]
        sems = refs[2 * n:2 * n + 2 * n_g]
        token = refs[4 * n + 2 * n_g]
        off = 0
        for gi, size in enumerate(sizes):
            for cp in _exchange_copies(kind, srcs[off:off + size], dsts[off:off + size], sems[2 * gi], sems[2 * gi + 1]):
                cp.start()
            off += size
        token[...] = jnp.zeros_like(token)

    arrays = [pltpu.with_memory_space_constraint(a, pltpu.HBM) for a in
              [s for g in groups for s, _ in g] + [d for g in groups for _, d in g]]
    sem_shapes = []
    for size in sizes:
        sem_shapes += [pltpu.SemaphoreType.DMA((3 * size,))] * 2
    outs = pl.pallas_call(
        body, name=name,
        in_specs=[HBM] * (2 * n),
        out_specs=[SEM] * (2 * n_g) + [HBM] * (2 * n) + [pl.BlockSpec(memory_space=pltpu.VMEM)],
        out_shape=sem_shapes + [pltpu.HBM(a.shape, a.dtype) for a in arrays] + [jax.ShapeDtypeStruct((8, LANES), F32)],
        input_output_aliases={t: 2 * n_g + t for t in range(2 * n)},
        compiler_params=pltpu.CompilerParams(has_side_effects=DATAFLOW),
    )(*arrays)
    sems, thru, token = outs[:2 * n_g], outs[2 * n_g:-1], outs[-1]
    handles, off = [], 0
    for gi, size in enumerate(sizes):
        handles.append((sems[2 * gi], sems[2 * gi + 1], thru[off:off + size], thru[n + off:n + off + size]))
        off += size
    return handles, token


def _exchange_wait(name, kind, handle, after):
    send_sems, recv_sems, srcs, dsts = handle
    n = len(srcs)

    def body(*refs):
        for cp in _exchange_copies(kind, refs[:n], refs[n:2 * n], refs[2 * n], refs[2 * n + 1]):
            cp.wait_send()
            cp.wait_recv()

    outs = pl.pallas_call(
        body, name=name,
        in_specs=[HBM] * (2 * n) + [SEM, SEM, ANY], out_specs=[HBM] * (2 * n),
        out_shape=[pltpu.HBM(a.shape, a.dtype) for a in (*srcs, *dsts)],
        input_output_aliases={t: t for t in range(2 * n)},
        compiler_params=pltpu.CompilerParams(has_side_effects=DATAFLOW),
    )(*srcs, *dsts, send_sems, recv_sems, after)
    return outs[n:]


def _own_shards(name, shards, lands, mine):
    n = len(shards)

    def body(mine_ref, *refs):
        ins, outs, sems = refs[:n], refs[2 * n:3 * n], refs[3 * n]
        cps = [pltpu.make_async_copy(ins[t], outs[t].at[mine_ref[0]], sems.at[t]) for t in range(n)]
        for cp in cps:
            cp.start()
        for cp in cps:
            cp.wait()

    return pl.pallas_call(
        body, name=name,
        grid_spec=pltpu.PrefetchScalarGridSpec(num_scalar_prefetch=1, grid=(1,), in_specs=[ANY] * (2 * n), out_specs=[ANY] * n,
                                               scratch_shapes=[pltpu.SemaphoreType.DMA((n,))]),
        out_shape=[jax.ShapeDtypeStruct(l.shape, l.dtype) for l in lands],
        input_output_aliases={1 + n + t: t for t in range(n)},
        compiler_params=_cp("arbitrary"),
    )(mine, *shards, *lands)


def _swap_with_sibling(parts):
    n = len(parts)

    def body(*refs):
        ins, outs = refs[:n], refs[n:2 * n]
        send_sems, recv_sems = refs[2 * n:]
        x, y, c, _ = _place()
        cps = [pltpu.make_async_remote_copy(src_ref=ins[t], dst_ref=outs[t], send_sem=send_sems.at[t], recv_sem=recv_sems.at[t],
                                            device_id=(x, y, 1 - c), device_id_type=MESH) for t in range(n)]
        for cp in cps:
            cp.start()
        for cp in cps:
            cp.wait_recv()
        for cp in cps:
            cp.wait_send()

    return pl.pallas_call(
        body, name="swap_partial_grads", in_specs=[ANY] * n, out_specs=[ANY] * n,
        out_shape=[jax.ShapeDtypeStruct(p.shape, p.dtype) for p in parts],
        scratch_shapes=[pltpu.SemaphoreType.DMA((n,)), pltpu.SemaphoreType.DMA((n,))],
    )(*parts)


def _allreduce_small(v):
    rows = v.shape[0]

    def body(v_ref, out_ref, buf, send_sems, recv_sems):
        x, y, c, chips = _place()
        me, sibling = (x, y, c), (x, y, 1 - c)

        def slot(px, py, pc):
            return buf.at[4 * px + 2 * py + pc]

        def copy(k, block, to, src=None):
            return pltpu.make_async_remote_copy(
                src_ref=slot(*block) if src is None else src, dst_ref=slot(*block), send_sem=send_sems.at[k],
                recv_sem=recv_sems.at[k], device_id=to, device_id_type=MESH)

        slot(*me)[...] = v_ref[...]
        first = [copy(0, me, sibling, src=v_ref)] + [copy(1 + j, me, (*chip, c), src=v_ref) for j, chip in enumerate(chips)]
        for cp in first:
            cp.start()
        passed = [copy(4 + j, (*chip, c), sibling) for j, chip in enumerate(chips)]
        for j, chip in enumerate(chips):
            copy(1 + j, (*chip, c), me).wait_recv()
            passed[j].start()
        copy(0, sibling, me).wait_recv()
        for j, chip in enumerate(chips):
            copy(4 + j, (*chip, 1 - c), me).wait_recv()
        for cp in first + passed:
            cp.wait_send()
        acc = buf[0]
        for k in range(1, 8):
            acc = acc + buf[k]
        out_ref[...] = acc

    return pl.pallas_call(
        body, name="allreduce_small_grads",
        in_specs=[pl.BlockSpec(memory_space=pltpu.VMEM)], out_specs=pl.BlockSpec(memory_space=pltpu.VMEM),
        out_shape=jax.ShapeDtypeStruct((rows, LANES), F32),
        scratch_shapes=[pltpu.VMEM((8, rows, LANES), F32), pltpu.SemaphoreType.DMA((7,)), pltpu.SemaphoreType.DMA((7,))],
        compiler_params=pltpu.CompilerParams(vmem_limit_bytes=VMEM_LIMIT_BYTES),
    )(v)


MM_TM = 1024


def _sq_relu_epilogue(acc):
    r = jnp.maximum(acc, 0.0)
    return acc, r * r


def _add_epilogue(acc, x):
    return (acc + x,)


def _sq_relu_grad_epilogue(acc, a):
    return (acc * (2.0 * jnp.maximum(a.astype(F32), 0.0)),)


def _layer_tensors(layer):
    i = layer // 2
    mixer = [("ab_w_in", i), ("b_conv_w", i), ("ab_w_out", i)] if layer % 2 == 0 else [("c_w_qkv", i), ("c_w_out", i)]
    return mixer + [("mlp_w1", layer), ("mlp_w2", layer)]


def _local_step(x, target, p, weights_of, grads_done):
    S, D = x.shape
    depth = p["mix_norm_g"].shape[0]
    n_even = (depth + 1) // 2
    mix_g3 = p["mix_norm_g"].reshape(depth, 1, D)
    mlp_g3 = p["mlp_norm_g"].reshape(depth, 1, D)
    vec3 = lambda t: t.reshape(t.shape[0], 1, t.shape[1])
    spw16 = p["a_spatial_w"].astype(BF16)
    spw16_t = jnp.swapaxes(spw16, 2, 3)
    bias_full = jnp.repeat(jnp.swapaxes(p["a_spatial_b"], 1, 2), HEAD_DIM, axis=2)
    vn_g, vn_b, cn_g, cn_b, cb3 = (vec3(p[k]) for k in ("a_vnorm_g", "a_vnorm_b", "b_norm_g", "b_norm_b", "b_conv_b"))
    tables = _rope_tables(S)
    gq = jnp.tile(p["c_q_norm_g"], (1, 2))
    gk = jnp.tile(p["c_k_norm_g"], (1, 2))

    saved = []
    for layer in range(depth):
        i = layer // 2
        wl = weights_of(layer, x)
        rec = {"x_mix": x, "w": wl}
        h = _rms_fwd(f"mix_norm_{layer}", x, mix_g3, layer)
        rec["h_mix"] = h
        if layer % 2 == 0:
            (z,) = _mm_ngroup(f"ab_in_{layer}", h, wl["ab_w_in"], nt=False, tm=MM_TM, out_dtypes=[F32])
            gconv = _glu_conv_fwd(f"glu_conv_{layer}", z, wl["b_conv_w"], cb3, i)
            cat = _ab_tail_fwd(f"ab_tail_{layer}", z, gconv, spw16, bias_full, vn_g, vn_b, cn_g, cn_b, i)
            (x,) = _mm_kgroup(f"ab_out_{layer}", cat, wl["ab_w_out"], nt=False, tm=MM_TM, out_dtypes=[F32],
                              extras=(x,), epilogue=_add_epilogue)
            rec.update(z=z, gconv=gconv, cat=cat)
        else:
            (qkv,) = _mm_ngroup(f"c_qkv_{layer}", h, wl["c_w_qkv"], nt=False, tm=MM_TM, out_dtypes=[F32])
            qn, kn = _qk_fwd(f"qk_norm_rope_{layer}", qkv, gq[i:i + 1], gk[i:i + 1], tables)
            os, lses = zip(*[_attn_fwd(f"attn_d{d}_{layer}", qn, kn, qkv, V_COL, d) for d in PATTERN_DILATIONS])
            o, lse = _attn_merge(os, lses)
            (x,) = _mm_kgroup(f"c_out_{layer}", o, wl["c_w_out"], nt=False, tm=MM_TM, out_dtypes=[F32],
                              extras=(x,), epilogue=_add_epilogue)
            rec.update(qkv=qkv, qn=qn, kn=kn, o=o, lse=lse)
        rec["x_mlp"] = x
        h = _rms_fwd(f"mlp_norm_{layer}", x, mlp_g3, layer)
        a, hsq = _mm_ngroup(f"mlp_up_{layer}", h, wl["mlp_w1"], nt=False, tm=MM_TM, out_dtypes=[BF16, BF16],
                            epilogue=_sq_relu_epilogue)
        (x,) = _mm_kgroup(f"mlp_down_{layer}", hsq, wl["mlp_w2"], nt=False, tm=MM_TM, out_dtypes=[F32],
                          extras=(x,), epilogue=_add_epilogue)
        rec.update(h_mlp=h, a=a, hsq=hsq)
        saved.append(rec)

    dx, loss_row = _loss_grad(x, target)

    small = {k: [None] * v.shape[0] for k, v in p.items()}
    token = None
    for layer in reversed(range(depth)):
        i = layer // 2
        rec = saved[layer]
        wl = rec["w"]
        g = {}
        (da,) = _mm_ngroup(f"mlp_down_dgrad_{layer}", dx, wl["mlp_w2"], nt=True, tm=MM_TM, out_dtypes=[BF16],
                           extras=(rec["a"],), epilogue=_sq_relu_grad_epilogue, anchor=token)
        g["mlp_w2"] = _wgrad(f"mlp_down_wgrad_{layer}", rec["hsq"], dx, wl["mlp_w2"].shape, a_group=True, tm=MM_TM)
        g["mlp_w1"] = _wgrad(f"mlp_up_wgrad_{layer}", rec["h_mlp"], da, wl["mlp_w1"].shape, a_group=False, tm=MM_TM)
        (dh,) = _mm_kgroup(f"mlp_up_dgrad_{layer}", da, wl["mlp_w1"], nt=True, tm=MM_TM, out_dtypes=[F32])
        dx, small["mlp_norm_g"][layer] = _rms_bwd(f"mlp_norm_bwd_{layer}", rec["x_mlp"], mlp_g3, layer, dh, dx)
        if layer % 2 == 0:
            (dcat,) = _mm_ngroup(f"ab_out_dgrad_{layer}", dx, wl["ab_w_out"], nt=True, tm=MM_TM, out_dtypes=[F32])
            g["ab_w_out"] = _wgrad(f"ab_out_wgrad_{layer}", rec["cat"], dx, wl["ab_w_out"].shape, a_group=True, tm=MM_TM)
            dz, dgconv, dspw, dbias, dvg, dvb, dcg, dcb = _ab_tail_bwd(
                f"ab_tail_bwd_{layer}", rec["z"], rec["gconv"], dcat, spw16, spw16_t, bias_full, vn_g, vn_b, cn_g, cn_b, i)
            dz, gf, gb, dcbias = _glu_conv_bwd(f"glu_conv_bwd_{layer}", rec["z"], dgconv, dz, wl["b_conv_w"])
            g["b_conv_w"] = (gf, gb)
            small["a_spatial_w"][i] = dspw
            small["a_spatial_b"][i] = _fold_bias(dbias)[:, :A_GROUPS].T
            for k, val in (("a_vnorm_g", dvg), ("a_vnorm_b", dvb), ("b_norm_g", dcg), ("b_norm_b", dcb), ("b_conv_b", dcbias)):
                small[k][i] = val
            g["ab_w_in"] = _wgrad(f"ab_in_wgrad_{layer}", rec["h_mix"], dz, wl["ab_w_in"].shape, a_group=False, tm=MM_TM)
            (dh,) = _mm_kgroup(f"ab_in_dgrad_{layer}", dz, wl["ab_w_in"], nt=True, tm=MM_TM, out_dtypes=[F32])
        else:
            (do,) = _mm_ngroup(f"c_out_dgrad_{layer}", dx, wl["c_w_out"], nt=True, tm=MM_TM, out_dtypes=[F32])
            g["c_w_out"] = _wgrad(f"c_out_wgrad_{layer}", rec["o"], dx, wl["c_w_out"].shape, a_group=True, tm=MM_TM)
            delta = _attn_delta(do, rec["o"])
            attn_args = (rec["qn"], rec["kn"], rec["qkv"], V_COL, do, rec["lse"], delta)
            dqs = [_attn_bwd_q(f"attn_bwd_q_d{d}_{layer}", *attn_args, d) for d in PATTERN_DILATIONS]
            dks, dvs = zip(*[_attn_bwd_kv(f"attn_bwd_kv_d{d}_{layer}", *attn_args, d) for d in PATTERN_DILATIONS])
            dqkv, dgq, dgk = _qk_bwd(f"qk_norm_rope_bwd_{layer}", rec["qkv"], gq[i:i + 1], gk[i:i + 1], tables, dqs, dks, dvs)
            small["c_q_norm_g"][i] = dgq[:, :HEAD_DIM]
            small["c_k_norm_g"][i] = dgk[:, :HEAD_DIM]
            g["c_w_qkv"] = _wgrad(f"c_qkv_wgrad_{layer}", rec["h_mix"], dqkv, wl["c_w_qkv"].shape, a_group=False, tm=MM_TM)
            (dh,) = _mm_kgroup(f"c_qkv_dgrad_{layer}", dqkv, wl["c_w_qkv"], nt=True, tm=MM_TM, out_dtypes=[F32])
        dx, small["mix_norm_g"][layer] = _rms_bwd(f"mix_norm_bwd_{layer}", rec["x_mix"], mix_g3, layer, dh, dx)
        token = grads_done(layer, g)

    small = {k: jnp.stack([t.reshape(p[k].shape[1:]) for t in v]) for k, v in small.items()}
    return loss_row, dx, small


SHARDED = ("mlp_w1", "mlp_w2", "ab_w_in", "b_conv_w", "ab_w_out", "c_w_qkv", "c_w_out")
SMALL = ("mix_norm_g", "mlp_norm_g", "a_spatial_w", "a_spatial_b", "a_vnorm_g", "a_vnorm_b", "b_conv_b", "b_norm_g",
         "b_norm_b", "c_q_norm_g", "c_k_norm_g")
WEIGHTS = ("mix_norm_g", "mlp_norm_g", "mlp_w1", "mlp_w2", "ab_w_in", "a_spatial_w", "a_spatial_b", "a_vnorm_g",
           "a_vnorm_b", "b_conv_w", "b_conv_b", "b_norm_g", "b_norm_b", "ab_w_out", "c_w_qkv", "c_q_norm_g",
           "c_k_norm_g", "c_w_out")


def _pack(parts):
    flat = jnp.concatenate([parts[k].reshape(-1) for k in SMALL])
    rows = -(-flat.shape[0] // (256 * LANES)) * 256
    return jnp.pad(flat, (0, rows * LANES - flat.shape[0])).reshape(rows, LANES)


def _unpack(packed, like):
    flat = packed.reshape(-1)
    out, off = {}, 0
    for k in SMALL:
        n = like[k].size
        out[k] = flat[off:off + n].reshape(like[k].shape)
        off += n
    return out


def kernel(x, mix_norm_g, mlp_norm_g, mlp_w1, mlp_w2, ab_w_in, a_spatial_w, a_spatial_b, a_vnorm_g, a_vnorm_b, b_conv_w, b_conv_b, b_norm_g, b_norm_b, ab_w_out, c_w_qkv, c_q_norm_g, c_k_norm_g, c_w_out, loss_target, m_mix_norm_g, m_mlp_norm_g, m_mlp_w1, m_mlp_w2, m_ab_w_in, m_a_spatial_w, m_a_spatial_b, m_a_vnorm_g, m_a_vnorm_b, m_b_conv_w, m_b_conv_b, m_b_norm_g, m_b_norm_b, m_ab_w_out, m_c_w_qkv, m_c_q_norm_g, m_c_k_norm_g, m_c_w_out, v_mix_norm_g, v_mlp_norm_g, v_mlp_w1, v_mlp_w2, v_ab_w_in, v_a_spatial_w, v_a_spatial_b, v_a_vnorm_g, v_a_vnorm_b, v_b_conv_w, v_b_conv_b, v_b_norm_g, v_b_norm_b, v_ab_w_out, v_c_w_qkv, v_c_q_norm_g, v_c_k_norm_g, v_c_w_out):
    w = dict(mix_norm_g=mix_norm_g, mlp_norm_g=mlp_norm_g, mlp_w1=mlp_w1, mlp_w2=mlp_w2, ab_w_in=ab_w_in,
             a_spatial_w=a_spatial_w, a_spatial_b=a_spatial_b, a_vnorm_g=a_vnorm_g, a_vnorm_b=a_vnorm_b,
             b_conv_w=b_conv_w, b_conv_b=b_conv_b, b_norm_g=b_norm_g, b_norm_b=b_norm_b, ab_w_out=ab_w_out,
             c_w_qkv=c_w_qkv, c_q_norm_g=c_q_norm_g, c_k_norm_g=c_k_norm_g, c_w_out=c_w_out)
    m = dict(mix_norm_g=m_mix_norm_g, mlp_norm_g=m_mlp_norm_g, mlp_w1=m_mlp_w1, mlp_w2=m_mlp_w2, ab_w_in=m_ab_w_in,
             a_spatial_w=m_a_spatial_w, a_spatial_b=m_a_spatial_b, a_vnorm_g=m_a_vnorm_g, a_vnorm_b=m_a_vnorm_b,
             b_conv_w=m_b_conv_w, b_conv_b=m_b_conv_b, b_norm_g=m_b_norm_g, b_norm_b=m_b_norm_b, ab_w_out=m_ab_w_out,
             c_w_qkv=m_c_w_qkv, c_q_norm_g=m_c_q_norm_g, c_k_norm_g=m_c_k_norm_g, c_w_out=m_c_w_out)
    v = dict(mix_norm_g=v_mix_norm_g, mlp_norm_g=v_mlp_norm_g, mlp_w1=v_mlp_w1, mlp_w2=v_mlp_w2, ab_w_in=v_ab_w_in,
             a_spatial_w=v_a_spatial_w, a_spatial_b=v_a_spatial_b, a_vnorm_g=v_a_vnorm_g, a_vnorm_b=v_a_vnorm_b,
             b_conv_w=v_b_conv_w, b_conv_b=v_b_conv_b, b_norm_g=v_b_norm_g, b_norm_b=v_b_norm_b, ab_w_out=v_ab_w_out,
             c_w_qkv=v_c_w_qkv, c_q_norm_g=v_c_q_norm_g, c_k_norm_g=v_c_k_norm_g, c_w_out=v_c_w_out)

    S, D = x.shape[1], x.shape[2]
    depth = mix_norm_g.shape[0]
    mine = (2 * lax.axis_index("x") + lax.axis_index("y")).astype(jnp.int32).reshape(1)

    layers = [_layer_tensors(layer) for layer in range(depth)]
    flat = [t for ts in layers for t in ts]
    shards = [w[k][i] if k == "b_conv_w" else w[k][i].astype(BF16) for k, i in flat]
    lands = _own_shards("own_weight_shards", shards, [lax.empty((N_SHARDS,) + s.shape, s.dtype) for s in shards], mine)
    pairs = iter(zip(shards, lands))
    handles, gather_token = _exchange_start("gather_weights_start", "gather", [[next(pairs) for _ in ts] for ts in layers])

    def weights_of(layer, after):
        got = _exchange_wait(f"gather_weights_wait_{layer}", "gather", handles[layer], gather_token if layer == 0 else after)
        return {k: a for (k, _), a in zip(layers[layer], got)}

    scattered = {}

    def grads_done(layer, g):
        names = [k for k, _ in layers[layer]]
        group = [(g[k][1], lax.empty((3,) + g[k][1].shape[1:], BF16)) for k in names]
        (handle,), token = _exchange_start(f"scatter_grads_start_{layer}", "scatter", [group])
        scattered[layer] = (handle, [g[k][0] for k in names])
        return token

    small_params = {k: w[k] for k in SMALL}
    loss_row, dx, small_grads = _local_step(x.reshape(S, D), loss_target.reshape(S, D), small_params, weights_of, grads_done)

    loss = lax.psum(loss_row[0, 0], ("x", "y", "c"))

    partial = []
    for layer in reversed(range(depth)):
        handle, gfs = scattered[layer]
        recvs = _exchange_wait(f"scatter_grads_wait_{layer}", "scatter", handle, dx)
        partial += [_sum4(f"sum_chips_{k}_{i}", gf, r, mine) for (k, i), gf, r in zip(layers[layer], gfs, recvs)]
    order = [t for layer in reversed(range(depth)) for t in layers[layer]]
    other = _swap_with_sibling(partial)
    stacked = {k: [lax.empty(w[k].shape, F32) for _ in range(4)] for k in SHARDED}
    for (k, i), a, b in zip(order, partial, other):
        stacked[k] = _adamw_layer(f"adamw_{k}_{i}", w[k], m[k], v[k], i, a, b, stacked[k])
    grads, deltas, new_m, new_v = ({k: stacked[k][j] for k in SHARDED} for j in range(4))

    g_small = _allreduce_small(_pack(small_grads))
    outs = _adamw("adamw_small", _pack(small_params), _pack({k: m[k] for k in SMALL}), _pack({k: v[k] for k in SMALL}), g_small)
    for d_, packed in zip((grads, deltas, new_m, new_v), outs):
        d_.update(_unpack(packed, small_params))

    return (loss, dx.reshape(1, S, D), *[grads[k] for k in WEIGHTS], *[deltas[k] for k in WEIGHTS],
            *[new_m[k] for k in WEIGHTS], *[new_v[k] for k in WEIGHTS])
```

```python
import functools

import jax
import jax.numpy as jnp
from jax import lax
from jax.experimental import pallas as pl
from jax.experimental.pallas import tpu as pltpu

F32, BF16 = jnp.float32, jnp.bfloat16
MESH = pl.DeviceIdType.MESH
ANY = pl.BlockSpec(memory_space=pl.ANY)

VMEM_LIMIT_BYTES = 56 * 1024 * 1024
LANES = 128
ELEMENTWISE_ROWS = 256

EPS = 1e-6
NEG = -1e30
HEAD_DIM = 64
N_HEADS = 16
CHUNK = 128
A_GROUPS = 8
CONV_WIDTH = 31
CONV_HALO = 16
BAND = 64
PATTERN_DILATIONS = (1, 4, 16)
ROT_DIM = 16
ROPE_THETA = 500000.0
N_SHARDS = 4

ADAM_LR, ADAM_B1, ADAM_B2, ADAM_EPS, ADAM_WD, ADAM_STEP = 0.001, 0.9, 0.999, 1e-08, 0.01, 10


def _cp(*sem):
    return pltpu.CompilerParams(dimension_semantics=sem, vmem_limit_bytes=VMEM_LIMIT_BYTES)


def _tile(n, pref):
    t = min(n, pref)
    assert n % t == 0, (n, pref)
    return t


def _dot(a, b, ca, cb):
    return lax.dot_general(a, b, (((ca,), (cb,)), ((), ())), preferred_element_type=F32)


def _mm_ngroup(name, a, w, *, nt, tm, out_dtypes, extras=(), epilogue=None, anchor=None):
    M, K = a.shape
    G, R, C = w.shape
    nw = R if nt else C
    assert K == (C if nt else R)
    tm = _tile(M, tm)
    n_ex = len(extras)
    anchors = [] if anchor is None else [anchor]

    def body(a_ref, w_ref, *rest):
        rest = rest[len(anchors):]
        acc = _dot(a_ref[...].astype(BF16), w_ref[...], 1, 1 if nt else 0)
        res = epilogue(acc, *[e[...] for e in rest[:n_ex]]) if epilogue else (acc,)
        for o_ref, r in zip(rest[n_ex:], res):
            o_ref[...] = r.astype(o_ref.dtype)

    blk = pl.BlockSpec((tm, nw), lambda m, g: (m, g))
    return pl.pallas_call(
        body, name=name, grid=(M // tm, G),
        in_specs=[pl.BlockSpec((tm, K), lambda m, g: (m, 0)), pl.BlockSpec((None, R, C), lambda m, g: (g, 0, 0))]
        + [pl.BlockSpec((8, LANES), lambda m, g: (0, 0))] * len(anchors) + [blk] * n_ex,
        out_specs=[blk] * len(out_dtypes),
        out_shape=[jax.ShapeDtypeStruct((M, G * nw), dt) for dt in out_dtypes],
        compiler_params=_cp("parallel", "parallel"),
    )(a, w, *anchors, *extras)


def _mm_kgroup(name, a, w, *, nt, tm, out_dtypes, extras=(), epilogue=None):
    G, R, C = w.shape
    kw, N = (C, R) if nt else (R, C)
    if a.ndim == 3:
        M = a.shape[1]
        assert a.shape[0] == G and a.shape[2] == kw
    else:
        M = a.shape[0]
        assert a.shape[1] == G * kw
    tm = _tile(M, tm)
    n_ex = len(extras)
    a_spec = (pl.BlockSpec((None, tm, kw), lambda m, g: (g, m, 0)) if a.ndim == 3
              else pl.BlockSpec((tm, kw), lambda m, g: (m, g)))

    def body(a_ref, w_ref, *rest):
        acc_ref = rest[-1]
        g = pl.program_id(1)
        part = _dot(a_ref[...].astype(BF16), w_ref[...], 1, 1 if nt else 0)

        @pl.when(g == 0)
        def _():
            acc_ref[...] = part

        @pl.when(g > 0)
        def _():
            acc_ref[...] += part

        @pl.when(g == G - 1)
        def _():
            acc = acc_ref[...]
            res = epilogue(acc, *[e[...] for e in rest[:n_ex]]) if epilogue else (acc,)
            for o_ref, r in zip(rest[n_ex:-1], res):
                o_ref[...] = r.astype(o_ref.dtype)

    blk = pl.BlockSpec((tm, N), lambda m, g: (m, 0))
    return pl.pallas_call(
        body, name=name, grid=(M // tm, G),
        in_specs=[a_spec, pl.BlockSpec((None, R, C), lambda m, g: (g, 0, 0))] + [blk] * n_ex,
        out_specs=[blk] * len(out_dtypes),
        out_shape=[jax.ShapeDtypeStruct((M, N), dt) for dt in out_dtypes],
        scratch_shapes=[pltpu.VMEM((tm, N), F32)],
        compiler_params=_cp("parallel", "arbitrary"),
    )(a, w, *extras)


def _wgrad(name, a, b, shape, *, a_group, tm):
    G, R, C = shape
    M = a.shape[0]
    tm = _tile(M, tm)
    n_m = M // tm

    def body(a_ref, b_ref, gf_ref, gb_ref):
        m = pl.program_id(1)
        part = _dot(a_ref[...].astype(BF16), b_ref[...].astype(BF16), 0, 0)

        @pl.when(m == 0)
        def _():
            gf_ref[...] = part

        @pl.when(m > 0)
        def _():
            gf_ref[...] += part

        @pl.when(m == n_m - 1)
        def _():
            gb_ref[...] = gf_ref[...].astype(BF16)

    a_spec = pl.BlockSpec((tm, R), (lambda g, m: (m, g)) if a_group else (lambda g, m: (m, 0)))
    if b.ndim == 3:
        assert not a_group
        b_spec = pl.BlockSpec((None, tm, C), lambda g, m: (g, m, 0))
    else:
        b_spec = pl.BlockSpec((tm, C), (lambda g, m: (m, 0)) if a_group else (lambda g, m: (m, g)))
    o_spec = pl.BlockSpec((None, R, C), lambda g, m: (g, 0, 0))
    return pl.pallas_call(
        body, name=name, grid=(G, n_m),
        in_specs=[a_spec, b_spec], out_specs=[o_spec, o_spec],
        out_shape=[jax.ShapeDtypeStruct(shape, F32), jax.ShapeDtypeStruct(shape, BF16)],
        compiler_params=_cp("parallel", "arbitrary"),
    )(a, b)


def _rms_fwd(name, x, g3, layer):
    S, D = x.shape
    tm = _tile(S, 512)

    def body(x_ref, g_ref, h_ref):
        xv = x_ref[...]
        r = lax.rsqrt(jnp.mean(xv * xv, axis=-1, keepdims=True) + EPS)
        h_ref[...] = (xv * r * g_ref[...]).astype(BF16)

    row = pl.BlockSpec((tm, D), lambda m: (m, 0))
    return pl.pallas_call(
        body, name=name, grid=(S // tm,),
        in_specs=[row, pl.BlockSpec((None, 1, D), lambda m: (layer, 0, 0))], out_specs=row,
        out_shape=jax.ShapeDtypeStruct((S, D), BF16), compiler_params=_cp("parallel"),
    )(x, g3)


def _rms_bwd(name, x, g3, layer, dh, dres):
    S, D = x.shape
    tm = _tile(S, 512)

    def body(x_ref, g_ref, dh_ref, dres_ref, dx_ref, dg_ref):
        xv = x_ref[...]
        d = dh_ref[...].astype(F32)
        r = lax.rsqrt(jnp.mean(xv * xv, axis=-1, keepdims=True) + EPS)
        xhat = xv * r
        dxhat = d * g_ref[...]
        dx_ref[...] = dres_ref[...] + r * (dxhat - xhat * jnp.mean(dxhat * xhat, axis=-1, keepdims=True))

        @pl.when(pl.program_id(0) == 0)
        def _():
            dg_ref[...] = jnp.zeros_like(dg_ref)

        dg_ref[...] += jnp.sum(d * xhat, axis=0, keepdims=True)

    row = pl.BlockSpec((tm, D), lambda m: (m, 0))
    return pl.pallas_call(
        body, name=name, grid=(S // tm,),
        in_specs=[row, pl.BlockSpec((None, 1, D), lambda m: (layer, 0, 0)), row, row],
        out_specs=[row, pl.BlockSpec((1, D), lambda m: (0, 0))],
        out_shape=[jax.ShapeDtypeStruct((S, D), F32), jax.ShapeDtypeStruct((1, D), F32)],
        compiler_params=_cp("arbitrary"),
    )(x, g3, dh, dres)


def _loss_grad(y, target):
    S, D = y.shape
    tm = _tile(S, 512)

    def body(y_ref, t_ref, dy_ref, l_ref):
        e = y_ref[...] - t_ref[...]
        dy_ref[...] = e * (1.0 / D)

        @pl.when(pl.program_id(0) == 0)
        def _():
            l_ref[...] = jnp.zeros_like(l_ref)

        l_ref[...] += (0.5 / D) * jnp.sum(jnp.sum(e * e, axis=1, keepdims=True), axis=0, keepdims=True)

    row = pl.BlockSpec((tm, D), lambda m: (m, 0))
    return pl.pallas_call(
        body, name="loss_grad", grid=(S // tm,), in_specs=[row, row],
        out_specs=[row, pl.BlockSpec((1, LANES), lambda m: (0, 0))],
        out_shape=[jax.ShapeDtypeStruct((S, D), F32), jax.ShapeDtypeStruct((1, LANES), F32)],
        compiler_params=_cp("arbitrary"),
    )(y, target)


def _adamw_math(w, m, v, g):
    m2 = ADAM_B1 * m + (1.0 - ADAM_B1) * g
    v2 = ADAM_B2 * v + (1.0 - ADAM_B2) * jnp.square(g)
    m_hat = m2 / (1.0 - ADAM_B1 ** ADAM_STEP)
    v_hat = v2 / (1.0 - ADAM_B2 ** ADAM_STEP)
    return g, -ADAM_LR * (m_hat / (jnp.sqrt(v_hat) + ADAM_EPS) + ADAM_WD * w), m2, v2


def _row_tile(rows):
    return _tile(rows, ELEMENTWISE_ROWS) if rows % ELEMENTWISE_ROWS == 0 else rows


def _adamw(name, w, m, v, g):
    rows, C = w.shape
    tr = _row_tile(rows)

    def body(w_ref, m_ref, v_ref, g_in, g_ref, d_ref, nm_ref, nv_ref):
        for o_ref, val in zip((g_ref, d_ref, nm_ref, nv_ref), _adamw_math(w_ref[...], m_ref[...], v_ref[...], g_in[...])):
            o_ref[...] = val

    blk = pl.BlockSpec((tr, C), lambda i: (i, 0))
    return pl.pallas_call(
        body, name=name, grid=(rows // tr,), in_specs=[blk] * 4, out_specs=[blk] * 4,
        out_shape=[jax.ShapeDtypeStruct((rows, C), F32)] * 4, compiler_params=_cp("parallel"),
    )(w, m, v, g)


def _adamw_layer(name, w, m, v, layer, mine, theirs, outs):
    _, R, C = w.shape
    tr = _row_tile(R)

    def body(w_ref, m_ref, v_ref, a_ref, b_ref, *rest):
        g = a_ref[...] + b_ref[...]
        for o_ref, val in zip(rest[4:], _adamw_math(w_ref[...], m_ref[...], v_ref[...], g)):
            o_ref[...] = val

    st = pl.BlockSpec((None, tr, C), lambda i: (layer, i, 0))
    part = pl.BlockSpec((tr, C), lambda i: (i, 0))
    return pl.pallas_call(
        body, name=name, grid=(R // tr,), in_specs=[st] * 3 + [part] * 2 + [ANY] * 4, out_specs=[st] * 4,
        out_shape=[jax.ShapeDtypeStruct(w.shape, F32)] * 4, input_output_aliases={5 + j: j for j in range(4)},
        compiler_params=_cp("parallel"),
    )(w, m, v, mine, theirs, *outs)


def _sum4(name, gf, recv, mine):
    _, R, C = gf.shape
    tr = _row_tile(R)

    def body(mine_ref, o_ref, r_ref, out_ref):
        acc = o_ref[...]
        for k in range(3):
            acc = acc + r_ref[k].astype(F32)
        out_ref[...] = acc

    return pl.pallas_call(
        body, name=name,
        grid_spec=pltpu.PrefetchScalarGridSpec(
            num_scalar_prefetch=1, grid=(R // tr,),
            in_specs=[pl.BlockSpec((None, tr, C), lambda i, s: (s[0], i, 0)), pl.BlockSpec((3, tr, C), lambda i, s: (0, i, 0))],
            out_specs=pl.BlockSpec((tr, C), lambda i, s: (i, 0))),
        out_shape=jax.ShapeDtypeStruct((R, C), F32), compiler_params=_cp("parallel"),
    )(mine, gf, recv)


def _gelu(x):
    return x * (0.5 * (1.0 + jnp.tanh(0.7978845608028654 * (x + 0.044715 * (x * x * x)))))


def _layernorm(t, g, b):
    mu = jnp.mean(t, axis=-1, keepdims=True)
    var = jnp.mean(jnp.square(t - mu), axis=-1, keepdims=True)
    return (t - mu) * lax.rsqrt(var + EPS) * g + b


def _silu(x):
    return x * jax.nn.sigmoid(x)


def _a_value(zv, g, b):
    return _layernorm(_gelu(zv), g, b)


def _b_tail(gc, g, b):
    return _silu(_layernorm(gc, g, b))


def _first_head(shape):
    return lax.broadcasted_iota(jnp.int32, shape, len(shape) - 1) < HEAD_DIM


def _spatial_mix(spw_ref, vb, tm):
    first = _first_head((CHUNK, LANES))
    rows = []
    for n in range(tm // CHUNK):
        blocks = []
        for j in range(A_GROUPS // 2):
            vblk = vb[n * CHUNK:(n + 1) * CHUNK, j * LANES:(j + 1) * LANES]
            r0 = _dot(spw_ref[2 * j], vblk, 1, 0)
            r1 = _dot(spw_ref[2 * j + 1], vblk, 1, 0)
            blocks.append(jnp.where(first, r0, r1))
        rows.append(jnp.concatenate(blocks, axis=1))
    return jnp.concatenate(rows, axis=0) if len(rows) > 1 else rows[0]


def _ab_tail_fwd(name, z, gconv, spw, bias_full, vn_g, vn_b, cn_g, cn_b, layer):
    S = z.shape[0]
    AW = 512
    tm = _tile(S, 256)

    def body(zu_ref, zv_ref, gc_ref, spw_ref, bias_ref, vg_ref, vb_ref, cg_ref, cb_ref, cat_ref):
        u = _gelu(zu_ref[...])
        v = _a_value(zv_ref[...], vg_ref[...], vb_ref[...])
        sv = _spatial_mix(spw_ref, v.astype(BF16), tm) + jnp.tile(bias_ref[...], (tm // CHUNK, 1))
        cat_ref[:, :AW] = (u * sv).astype(BF16)
        cat_ref[:, AW:] = _b_tail(gc_ref[...], cg_ref[...], cb_ref[...]).astype(BF16)

    vec = pl.BlockSpec((None, 1, AW), lambda m: (layer, 0, 0))
    return pl.pallas_call(
        body, name=name, grid=(S // tm,),
        in_specs=[pl.BlockSpec((tm, AW), lambda m: (m, 0)), pl.BlockSpec((tm, AW), lambda m: (m, 1)),
                  pl.BlockSpec((tm, AW), lambda m: (m, 0)),
                  pl.BlockSpec((None, A_GROUPS, CHUNK, CHUNK), lambda m: (layer, 0, 0, 0)),
                  pl.BlockSpec((None, CHUNK, AW), lambda m: (layer, 0, 0)), vec, vec, vec, vec],
        out_specs=pl.BlockSpec((tm, 2 * AW), lambda m: (m, 0)),
        out_shape=jax.ShapeDtypeStruct((S, 2 * AW), BF16), compiler_params=_cp("parallel"),
    )(z, z, gconv, spw, bias_full, vn_g, vn_b, cn_g, cn_b)


def _ab_tail_bwd(name, z, gconv, dcat, spw, spw_t, bias_full, vn_g, vn_b, cn_g, cn_b, layer):
    S = z.shape[0]
    AW = 512
    tm = _tile(S, 256)
    n_chunks = tm // CHUNK

    def body(zu_ref, zv_ref, gc_ref, dcat_ref, spw_ref, spwt_ref, bias_ref, vg_ref, vb_ref, cg_ref, cb_ref,
             dz_ref, dgc_ref, dspw_ref, dbias_ref, dvg_ref, dvb_ref, dcg_ref, dcb_ref):
        @pl.when(pl.program_id(0) == 0)
        def _():
            for r in (dspw_ref, dbias_ref, dvg_ref, dvb_ref, dcg_ref, dcb_ref):
                r[...] = jnp.zeros_like(r)

        dya = dcat_ref[:, :AW]
        dyb = dcat_ref[:, AW:]
        u, u_vjp = jax.vjp(_gelu, zu_ref[...])
        v, v_vjp = jax.vjp(_a_value, zv_ref[...], vg_ref[...], vb_ref[...])
        vb16 = v.astype(BF16)
        sv = _spatial_mix(spw_ref, vb16, tm) + jnp.tile(bias_ref[...], (n_chunks, 1))
        (dzu,) = u_vjp(dya * sv)
        dsv = dya * u
        dsv16 = dsv.astype(BF16)
        dv = _spatial_mix(spwt_ref, dsv16, tm)
        dzv, dvg, dvb = v_vjp(dv)
        dz_ref[0] = dzu
        dz_ref[1] = dzv
        dvg_ref[...] += dvg
        dvb_ref[...] += dvb

        first = _first_head((CHUNK, LANES))
        zero = jnp.zeros((), BF16)
        dbias = jnp.zeros((CHUNK, AW), F32)
        for n in range(n_chunks):
            rows = slice(n * CHUNK, (n + 1) * CHUNK)
            dbias = dbias + dsv[rows]
            for j in range(A_GROUPS // 2):
                cols = slice(j * LANES, (j + 1) * LANES)
                dblk, vblk = dsv16[rows, cols], vb16[rows, cols]
                dspw_ref[2 * j] += _dot(jnp.where(first, dblk, zero), vblk, 1, 1)
                dspw_ref[2 * j + 1] += _dot(jnp.where(first, zero, dblk), vblk, 1, 1)
        dbias_ref[...] += dbias

        _, t_vjp = jax.vjp(_b_tail, gc_ref[...], cg_ref[...], cb_ref[...])
        dgc, dcg, dcb = t_vjp(dyb)
        dgc_ref[...] = dgc
        dcg_ref[...] += dcg
        dcb_ref[...] += dcb

    vec = pl.BlockSpec((None, 1, AW), lambda m: (layer, 0, 0))
    spw_spec = pl.BlockSpec((None, A_GROUPS, CHUNK, CHUNK), lambda m: (layer, 0, 0, 0))
    ovec = pl.BlockSpec((1, AW), lambda m: (0, 0))
    return pl.pallas_call(
        body, name=name, grid=(S // tm,),
        in_specs=[pl.BlockSpec((tm, AW), lambda m: (m, 0)), pl.BlockSpec((tm, AW), lambda m: (m, 1)),
                  pl.BlockSpec((tm, AW), lambda m: (m, 0)), pl.BlockSpec((tm, 2 * AW), lambda m: (m, 0)),
                  spw_spec, spw_spec, pl.BlockSpec((None, CHUNK, AW), lambda m: (layer, 0, 0)), vec, vec, vec, vec],
        out_specs=[pl.BlockSpec((2, tm, AW), lambda m: (0, m, 0)), pl.BlockSpec((tm, AW), lambda m: (m, 0)),
                   pl.BlockSpec((A_GROUPS, CHUNK, CHUNK), lambda m: (0, 0, 0)),
                   pl.BlockSpec((CHUNK, AW), lambda m: (0, 0)), ovec, ovec, ovec, ovec],
        out_shape=[jax.ShapeDtypeStruct((4, S, AW), F32), jax.ShapeDtypeStruct((S, AW), F32),
                   jax.ShapeDtypeStruct((A_GROUPS, CHUNK, CHUNK), F32), jax.ShapeDtypeStruct((CHUNK, AW), F32)]
                  + [jax.ShapeDtypeStruct((1, AW), F32)] * 4,
        compiler_params=_cp("arbitrary"),
    )(z, z, gconv, dcat, spw, spw_t, bias_full, vn_g, vn_b, cn_g, cn_b)


def _fold_bias(dbias_full):
    def body(d_ref, o_ref):
        d = d_ref[...]
        hi = d.astype(BF16)
        lo = (d - hi.astype(F32)).astype(BF16)
        r = lax.broadcasted_iota(jnp.int32, (512, LANES), 0)
        c = lax.broadcasted_iota(jnp.int32, (512, LANES), 1)
        fold = jnp.where(lax.shift_right_logical(r, 6) == c, 1.0, 0.0).astype(BF16)
        o_ref[...] = _dot(hi, fold, 1, 0) + _dot(lo, fold, 1, 0)

    return pl.pallas_call(body, name="fold_spatial_bias", out_shape=jax.ShapeDtypeStruct((CHUNK, LANES), F32))(dbias_full)


def _halo_specs(tm, n_halo_blocks, col):
    r = tm // CONV_HALO
    prev = pl.BlockSpec((CONV_HALO, LANES), lambda j, i: (jnp.maximum(i * r - 1, 0), col + j))
    cur = pl.BlockSpec((tm, LANES), lambda j, i: (i, col + j))
    nxt = pl.BlockSpec((CONV_HALO, LANES), lambda j, i: (jnp.minimum((i + 1) * r, n_halo_blocks - 1), col + j))
    return [prev, cur, nxt]


def _fill_halo(scr, prev, cur, nxt, tm, i, n_i):
    scr[0:CONV_HALO, :] = jnp.where(i > 0, prev, 0.0)
    scr[CONV_HALO:CONV_HALO + tm, :] = cur
    scr[CONV_HALO + tm:2 * CONV_HALO + tm, :] = jnp.where(i < n_i - 1, nxt, 0.0)


def _glu_conv_fwd(name, z, cw, cb3, layer):
    S = z.shape[0]
    tm = _tile(S, 512)
    n_i = S // tm
    pad = CONV_WIDTH // 2

    def body(vp, vc, vn, gp, gc, gn, w_ref, b_ref, out_ref, scr):
        i = pl.program_id(1)
        glu = lambda a, b: a[...] * jax.nn.sigmoid(b[...])
        _fill_halo(scr, glu(vp, gp), glu(vc, gc), glu(vn, gn), tm, i, n_i)
        acc = jnp.zeros((tm, LANES), F32)
        for j in range(CONV_WIDTH):
            acc = acc + w_ref[j:j + 1, :] * scr[pl.ds(CONV_HALO - pad + j, tm), :]
        out_ref[...] = acc + b_ref[...]

    return pl.pallas_call(
        body, name=name, grid=(4, n_i),
        in_specs=_halo_specs(tm, S // CONV_HALO, 8) + _halo_specs(tm, S // CONV_HALO, 12)
        + [pl.BlockSpec((None, CONV_WIDTH, LANES), lambda j, i: (j, 0, 0)),
           pl.BlockSpec((None, 1, LANES), lambda j, i: (layer, 0, j))],
        out_specs=pl.BlockSpec((tm, LANES), lambda j, i: (i, j)),
        out_shape=jax.ShapeDtypeStruct((S, 4 * LANES), F32),
        scratch_shapes=[pltpu.VMEM((tm + 2 * CONV_HALO, LANES), F32)],
        compiler_params=_cp("parallel", "parallel"),
    )(z, z, z, z, z, z, cw, cb3)


def _glu_conv_bwd(name, z, dgconv, dz, cw):
    S = z.shape[0]
    tm = _tile(S, 512)
    n_i = S // tm
    pad = CONV_WIDTH // 2

    def body(vp, vc, vn, gp, gc, gn, dp, dc, dn, w_ref, dz_in, dz_ref, gf_ref, gb_ref, db_ref, g_scr, d_scr):
        i = pl.program_id(1)
        sig = jax.nn.sigmoid(gc[...])
        _fill_halo(g_scr, vp[...] * jax.nn.sigmoid(gp[...]), vc[...] * sig, vn[...] * jax.nn.sigmoid(gn[...]), tm, i, n_i)
        _fill_halo(d_scr, dp[...], dc[...], dn[...], tm, i, n_i)

        @pl.when(i == 0)
        def _():
            gf_ref[...] = jnp.zeros_like(gf_ref)
            db_ref[...] = jnp.zeros_like(db_ref)

        d_cur = dc[...]
        dglu = jnp.zeros((tm, LANES), F32)
        for j in range(CONV_WIDTH):
            dglu = dglu + w_ref[j:j + 1, :] * d_scr[pl.ds(CONV_HALO + pad - j, tm), :]
            gf_ref[j:j + 1, :] += jnp.sum(d_cur * g_scr[pl.ds(CONV_HALO - pad + j, tm), :], axis=0, keepdims=True)
        db_ref[...] += jnp.sum(d_cur, axis=0, keepdims=True)
        dz_ref[0] = dglu * sig
        dz_ref[1] = dglu * vc[...] * sig * (1.0 - sig)

        @pl.when(i == n_i - 1)
        def _():
            gb_ref[...] = gf_ref[...].astype(BF16)

    w_spec = pl.BlockSpec((None, CONV_WIDTH, LANES), lambda j, i: (j, 0, 0))
    return pl.pallas_call(
        body, name=name, grid=(4, n_i),
        in_specs=_halo_specs(tm, S // CONV_HALO, 8) + _halo_specs(tm, S // CONV_HALO, 12)
        + _halo_specs(tm, S // CONV_HALO, 0) + [w_spec, ANY],
        out_specs=[pl.BlockSpec((2, tm, LANES), lambda j, i: (1, i, j)),
                   w_spec, w_spec, pl.BlockSpec((1, LANES), lambda j, i: (0, j))],
        out_shape=[jax.ShapeDtypeStruct(dz.shape, F32), jax.ShapeDtypeStruct(cw.shape, F32),
                   jax.ShapeDtypeStruct(cw.shape, BF16), jax.ShapeDtypeStruct((1, 4 * LANES), F32)],
        input_output_aliases={10: 0},
        scratch_shapes=[pltpu.VMEM((tm + 2 * CONV_HALO, LANES), F32)] * 2,
        compiler_params=_cp("parallel", "arbitrary"),
    )(z, z, z, z, z, z, dgconv, dgconv, dgconv, cw, dz)


def _seg_matrix(scale):
    r = lax.broadcasted_iota(jnp.int32, (LANES, LANES), 0)
    c = lax.broadcasted_iota(jnp.int32, (LANES, LANES), 1)
    return jnp.where(lax.shift_right_logical(r, 6) == lax.shift_right_logical(c, 6), scale, 0.0).astype(BF16)


def _seg_sum(x, seg):
    hi = x.astype(BF16)
    lo = (x - hi.astype(F32)).astype(BF16)
    return _dot(hi, seg, 1, 0) + _dot(lo, seg, 1, 0)


def _rope_tables(S):
    pos = jnp.arange(S, dtype=F32)
    inv_freq = ROPE_THETA ** (-jnp.arange(0, ROT_DIM, 2, dtype=F32) / ROT_DIM)
    ang = pos[:, None] * inv_freq[None, :]
    cos, sin = jnp.cos(ang), jnp.sin(ang)
    half = ROT_DIM // 2
    rest = HEAD_DIM - ROT_DIM
    one, zero = jnp.ones((S, rest), F32), jnp.zeros((S, rest), F32)
    zh = jnp.zeros((S, half), F32)
    c = jnp.concatenate([cos, cos, one], axis=1)
    sa = jnp.concatenate([-sin, zh, zero], axis=1)
    sb = jnp.concatenate([zh, sin, zero], axis=1)
    return [jnp.tile(t, (1, 2)) for t in (c, sa, sb)]


def _qk_fwd(name, qkv, gq, gk, tables):
    S = qkv.shape[0]
    W = N_HEADS * HEAD_DIM
    tm = _tile(S, 256)
    half = ROT_DIM // 2

    def body(q_ref, k_ref, gq_ref, gk_ref, c_ref, sa_ref, sb_ref, qn_ref, kn_ref):
        seg = _seg_matrix(1.0 / HEAD_DIM)
        c, sa, sb = c_ref[...], sa_ref[...], sb_ref[...]
        for t_ref, g_ref, o_ref in ((q_ref, gq_ref, qn_ref), (k_ref, gk_ref, kn_ref)):
            for blk in range(W // LANES):
                cols = slice(blk * LANES, (blk + 1) * LANES)
                t = t_ref[:, cols]
                y = t * lax.rsqrt(_seg_sum(t * t, seg) + EPS) * g_ref[...]
                o_ref[:, cols] = y * c + pltpu.roll(y, LANES - half, 1) * sa + pltpu.roll(y, half, 1) * sb

    row = lambda k: pl.BlockSpec((tm, W), lambda m: (m, k))
    gain = pl.BlockSpec((1, LANES), lambda m: (0, 0))
    tab = pl.BlockSpec((tm, LANES), lambda m: (m, 0))
    return pl.pallas_call(
        body, name=name, grid=(S // tm,),
        in_specs=[row(0), row(1), gain, gain, tab, tab, tab], out_specs=[row(0)] * 2,
        out_shape=[jax.ShapeDtypeStruct((S, W), F32)] * 2, compiler_params=_cp("parallel"),
    )(qkv, qkv, gq, gk, *tables)


def _qk_bwd(name, qkv, gq, gk, tables, dqs, dks, dvs):
    S = qkv.shape[0]
    W = N_HEADS * HEAD_DIM
    tm = _tile(S, 256)
    half = ROT_DIM // 2
    n_p = len(dqs)

    def body(q_ref, k_ref, gq_ref, gk_ref, c_ref, sa_ref, sb_ref, *rest):
        dq_refs, dk_refs, dv_refs = rest[:n_p], rest[n_p:2 * n_p], rest[2 * n_p:3 * n_p]
        dqkv_ref, dgq_ref, dgk_ref = rest[3 * n_p:]

        @pl.when(pl.program_id(0) == 0)
        def _():
            dgq_ref[...] = jnp.zeros_like(dgq_ref)
            dgk_ref[...] = jnp.zeros_like(dgk_ref)

        seg = _seg_matrix(1.0 / HEAD_DIM)
        r_i = lax.broadcasted_iota(jnp.int32, (LANES, LANES), 0)
        c_i = lax.broadcasted_iota(jnp.int32, (LANES, LANES), 1)
        same_dim = jnp.where((r_i & (HEAD_DIM - 1)) == (c_i & (HEAD_DIM - 1)), 1.0, 0.0).astype(BF16)
        c, sa, sb = c_ref[...], sa_ref[...], sb_ref[...]
        for idx, (t_ref, g_ref, d_refs, dg_ref) in enumerate(((q_ref, gq_ref, dq_refs, dgq_ref),
                                                              (k_ref, gk_ref, dk_refs, dgk_ref))):
            dg = jnp.zeros((1, LANES), F32)
            for blk in range(W // LANES):
                cols = slice(blk * LANES, (blk + 1) * LANES)
                dout = d_refs[0][:, cols]
                for r in d_refs[1:]:
                    dout = dout + r[:, cols]
                dy = dout * c + pltpu.roll(dout * sa, half, 1) + pltpu.roll(dout * sb, LANES - half, 1)
                t = t_ref[:, cols]
                r_ = lax.rsqrt(_seg_sum(t * t, seg) + EPS)
                xhat = t * r_
                dg = dg + jnp.sum(dy * xhat, axis=0, keepdims=True)
                dxhat = dy * g_ref[...]
                dt = r_ * (dxhat - xhat * _seg_sum(dxhat * xhat, seg))
                dqkv_ref[:, idx * W + blk * LANES: idx * W + (blk + 1) * LANES] = dt.astype(BF16)
            dg_ref[...] += _seg_sum(jnp.broadcast_to(dg, (8, LANES)), same_dim)[0:1]
        dv = dv_refs[0][...]
        for r in dv_refs[1:]:
            dv = dv + r[...]
        dqkv_ref[:, 2 * W:] = dv.astype(BF16)

    row = lambda k: pl.BlockSpec((tm, W), lambda m: (m, k))
    gain = pl.BlockSpec((1, LANES), lambda m: (0, 0))
    tab = pl.BlockSpec((tm, LANES), lambda m: (m, 0))
    return pl.pallas_call(
        body, name=name, grid=(S // tm,),
        in_specs=[row(0), row(1), gain, gain, tab, tab, tab] + [row(0)] * (3 * n_p),
        out_specs=[pl.BlockSpec((tm, 3 * W), lambda m: (m, 0)), gain, gain],
        out_shape=[jax.ShapeDtypeStruct((S, 3 * W), BF16), jax.ShapeDtypeStruct((1, LANES), F32),
                   jax.ShapeDtypeStruct((1, LANES), F32)],
        compiler_params=_cp("arbitrary"),
    )(qkv, qkv, gq, gk, *tables, *dqs, *dks, *dvs)


ATTN_BQ = 2 * BAND
ATTN_ROWS = 16 * ATTN_BQ
V_COL = 2 * N_HEADS * HEAD_DIM // LANES


def _attn_geometry(S, d):
    rows = min(ATTN_ROWS, S)
    halo = BAND * d
    assert rows % (ATTN_BQ * d) == 0 and S % rows == 0, (S, d)
    return rows, halo, rows // (ATTN_BQ * d)


def _attn_specs(S, d, col):
    rows, halo, _ = _attn_geometry(S, d)
    r = rows // halo
    n_h = S // halo
    prev = pl.BlockSpec((halo, LANES), lambda j, i: (jnp.maximum(i * r - 1, 0), col + j))
    cur = pl.BlockSpec((rows, LANES), lambda j, i: (i, col + j))
    nxt = pl.BlockSpec((halo, LANES), lambda j, i: (jnp.minimum((i + 1) * r, n_h - 1), col + j))
    return [prev, cur, nxt]


def _fill_window(scr, prev, cur, nxt, rows, halo):
    scr[0:halo, :] = prev[...]
    scr[halo:halo + rows, :] = cur[...]
    scr[halo + rows:2 * halo + rows, :] = nxt[...]


def _strided(ref, start, size, d):
    return ref[pl.ds(start, size, stride=d) if d > 1 else pl.ds(start, size), :]


def _band_masks(i, S, d, sb):
    rows, _, _ = _attn_geometry(S, d)
    L = S // d
    base = i * (rows // d) + sb * ATTN_BQ
    wk = ATTN_BQ + 2 * BAND
    row = lax.broadcasted_iota(jnp.int32, (ATTN_BQ, wk), 0)
    col = lax.broadcasted_iota(jnp.int32, (ATTN_BQ, wk), 1)
    lj = base - BAND + col
    valid = (jnp.abs(col - BAND - row) <= BAND) & (lj >= 0) & (lj < L)
    row_t = lax.broadcasted_iota(jnp.int32, (wk, ATTN_BQ), 0)
    col_t = lax.broadcasted_iota(jnp.int32, (wk, ATTN_BQ), 1)
    li = base - BAND + row_t
    valid_t = (jnp.abs(row_t - BAND - col_t) <= BAND) & (li >= 0) & (li < L)
    return valid, valid_t


def _attn_fwd(name, q, k, v, v_col, d):
    S, W = q.shape
    rows, halo, n_sb = _attn_geometry(S, d)
    wk = ATTN_BQ + 2 * BAND
    scale = HEAD_DIM ** -0.5

    def body(q_ref, kp, kc, kn, vp, vc, vn, o_ref, lse_ref, kw, vw):
        i = pl.program_id(1)
        _fill_window(kw, kp, kc, kn, rows, halo)
        _fill_window(vw, vp, vc, vn, rows, halo)
        first = _first_head((ATTN_BQ, LANES))
        zero = jnp.zeros((), BF16)
        for sb in range(n_sb):
            valid, _ = _band_masks(i, S, d, sb)
            for r in range(d):
                start = r + d * sb * ATTN_BQ
                qv = _strided(q_ref, start, ATTN_BQ, d).astype(BF16)
                kv = _strided(kw, start, wk, d).astype(BF16)
                vv = _strided(vw, start, wk, d).astype(BF16)
                o_h, lse_h = [], []
                for hm in (first, jnp.logical_not(first)):
                    s = jnp.where(valid, _dot(jnp.where(hm, qv, zero), kv, 1, 1) * scale, NEG)
                    mx = jnp.max(s, axis=-1, keepdims=True)
                    p = jnp.exp(s - mx)
                    den = jnp.sum(p, axis=-1, keepdims=True)
                    o_h.append(_dot(p.astype(BF16), vv, 1, 0) / den)
                    lse_h.append(mx + jnp.log(den))
                dst = pl.ds(start, ATTN_BQ, stride=d) if d > 1 else pl.ds(start, ATTN_BQ)
                o_ref[dst, :] = jnp.where(first, o_h[0], o_h[1])
                lse_ref[dst, :] = jnp.where(first, lse_h[0], lse_h[1])

    cur = _attn_specs(S, d, 0)[1]
    return pl.pallas_call(
        body, name=name, grid=(W // LANES, S // rows),
        in_specs=[cur] + _attn_specs(S, d, 0) + _attn_specs(S, d, v_col), out_specs=[cur, cur],
        out_shape=[jax.ShapeDtypeStruct((S, W), F32)] * 2,
        scratch_shapes=[pltpu.VMEM((rows + 2 * halo, LANES), F32)] * 2,
        compiler_params=_cp("parallel", "parallel"),
    )(q, k, k, k, v, v, v)


def _attn_merge(os, lses):
    S, W = os[0].shape
    tm = _tile(S, 256)
    n_p = len(os)

    def body(*refs):
        o_refs, l_refs = refs[:n_p], refs[n_p:2 * n_p]
        o_ref, lt_ref = refs[2 * n_p:]
        ls = [r[...] for r in l_refs]
        mx = functools.reduce(jnp.maximum, ls)
        es = [jnp.exp(l - mx) for l in ls]
        den = functools.reduce(lambda a, b: a + b, es)
        acc = es[0] * o_refs[0][...]
        for e, r in zip(es[1:], o_refs[1:]):
            acc = acc + e * r[...]
        o_ref[...] = (acc / den).astype(BF16)
        lt_ref[...] = mx + jnp.log(den)

    row = pl.BlockSpec((tm, W), lambda m: (m, 0))
    return pl.pallas_call(
        body, name="attn_merge", grid=(S // tm,), in_specs=[row] * (2 * n_p), out_specs=[row, row],
        out_shape=[jax.ShapeDtypeStruct((S, W), BF16), jax.ShapeDtypeStruct((S, W), F32)],
        compiler_params=_cp("parallel"),
    )(*os, *lses)


def _attn_delta(do, o):
    S, W = do.shape
    tm = _tile(S, 256)

    def body(do_ref, o_ref, dl_ref):
        seg = _seg_matrix(1.0)
        for blk in range(W // LANES):
            cols = slice(blk * LANES, (blk + 1) * LANES)
            dl_ref[:, cols] = _seg_sum(do_ref[:, cols] * o_ref[:, cols].astype(F32), seg)

    row = pl.BlockSpec((tm, W), lambda m: (m, 0))
    return pl.pallas_call(
        body, name="attn_delta", grid=(S // tm,), in_specs=[row, row], out_specs=row,
        out_shape=jax.ShapeDtypeStruct((S, W), F32), compiler_params=_cp("parallel"),
    )(do, o)


def _attn_bwd_q(name, q, k, v, v_col, do, lse, delta, d):
    S, W = q.shape
    rows, halo, n_sb = _attn_geometry(S, d)
    wk = ATTN_BQ + 2 * BAND
    scale = HEAD_DIM ** -0.5

    def body(q_ref, do_ref, l_ref, dl_ref, kp, kc, kn, vp, vc, vn, dq_ref, kw, vw):
        i = pl.program_id(1)
        _fill_window(kw, kp, kc, kn, rows, halo)
        _fill_window(vw, vp, vc, vn, rows, halo)
        first = _first_head((ATTN_BQ, LANES))
        zero = jnp.zeros((), BF16)
        for sb in range(n_sb):
            valid, _ = _band_masks(i, S, d, sb)
            for r in range(d):
                start = r + d * sb * ATTN_BQ
                qv = _strided(q_ref, start, ATTN_BQ, d).astype(BF16)
                dov = _strided(do_ref, start, ATTN_BQ, d).astype(BF16)
                lv = _strided(l_ref, start, ATTN_BQ, d)
                dlv = _strided(dl_ref, start, ATTN_BQ, d)
                kv = _strided(kw, start, wk, d).astype(BF16)
                vv = _strided(vw, start, wk, d).astype(BF16)
                dq_h = []
                for hh, hm in enumerate((first, jnp.logical_not(first))):
                    lane0 = hh * HEAD_DIM
                    s = jnp.where(valid, _dot(jnp.where(hm, qv, zero), kv, 1, 1) * scale, NEG)
                    p = jnp.exp(s - lv[:, lane0:lane0 + 1])
                    dp = _dot(jnp.where(hm, dov, zero), vv, 1, 1)
                    ds = p * (dp - dlv[:, lane0:lane0 + 1]) * scale
                    dq_h.append(_dot(ds.astype(BF16), kv, 1, 0))
                dst = pl.ds(start, ATTN_BQ, stride=d) if d > 1 else pl.ds(start, ATTN_BQ)
                dq_ref[dst, :] = jnp.where(first, dq_h[0], dq_h[1])

    cur = _attn_specs(S, d, 0)[1]
    return pl.pallas_call(
        body, name=name, grid=(W // LANES, S // rows),
        in_specs=[cur] * 4 + _attn_specs(S, d, 0) + _attn_specs(S, d, v_col), out_specs=cur,
        out_shape=jax.ShapeDtypeStruct((S, W), F32),
        scratch_shapes=[pltpu.VMEM((rows + 2 * halo, LANES), F32)] * 2,
        compiler_params=_cp("parallel", "parallel"),
    )(q, do, lse, delta, k, k, k, v, v, v)


def _attn_bwd_kv(name, q, k, v, v_col, do, lse, delta, d):
    S, W = q.shape
    rows, halo, n_sb = _attn_geometry(S, d)
    wk = ATTN_BQ + 2 * BAND
    scale = HEAD_DIM ** -0.5

    def body(k_ref, v_ref, qp, qc, qn, dop, doc, don, lp, lc, ln, dlp, dlc, dln, dk_ref, dv_ref, qw, dow, lw, dlw):
        i = pl.program_id(1)
        _fill_window(qw, qp, qc, qn, rows, halo)
        _fill_window(dow, dop, doc, don, rows, halo)
        _fill_window(lw, lp, lc, ln, rows, halo)
        _fill_window(dlw, dlp, dlc, dln, rows, halo)
        first = _first_head((ATTN_BQ, LANES))
        first_w = _first_head((wk, LANES))
        zero = jnp.zeros((), BF16)
        for sb in range(n_sb):
            _, valid_t = _band_masks(i, S, d, sb)
            for r in range(d):
                start = r + d * sb * ATTN_BQ
                kv = _strided(k_ref, start, ATTN_BQ, d).astype(BF16)
                vv = _strided(v_ref, start, ATTN_BQ, d).astype(BF16)
                qv = _strided(qw, start, wk, d).astype(BF16)
                dov = _strided(dow, start, wk, d).astype(BF16)
                lv = _strided(lw, start, wk, d)
                dlv = _strided(dlw, start, wk, d)
                dk_h, dv_h = [], []
                for hh, hm_w in enumerate((first_w, jnp.logical_not(first_w))):
                    lane0 = hh * HEAD_DIM
                    st = jnp.where(valid_t, _dot(jnp.where(hm_w, qv, zero), kv, 1, 1) * scale, NEG)
                    pt = jnp.exp(st - lv[:, lane0:lane0 + 1])
                    dv_h.append(_dot(pt.astype(BF16), dov, 0, 0))
                    dpt = _dot(jnp.where(hm_w, dov, zero), vv, 1, 1)
                    dst_ = pt * (dpt - dlv[:, lane0:lane0 + 1]) * scale
                    dk_h.append(_dot(dst_.astype(BF16), qv, 0, 0))
                dst = pl.ds(start, ATTN_BQ, stride=d) if d > 1 else pl.ds(start, ATTN_BQ)
                dk_ref[dst, :] = jnp.where(first, dk_h[0], dk_h[1])
                dv_ref[dst, :] = jnp.where(first, dv_h[0], dv_h[1])

    cur = _attn_specs(S, d, 0)[1]
    win = _attn_specs(S, d, 0)
    return pl.pallas_call(
        body, name=name, grid=(W // LANES, S // rows),
        in_specs=[cur, _attn_specs(S, d, v_col)[1]] + win * 4, out_specs=[cur, cur],
        out_shape=[jax.ShapeDtypeStruct((S, W), F32)] * 2,
        scratch_shapes=[pltpu.VMEM((rows + 2 * halo, LANES), F32)] * 4,
        compiler_params=_cp("parallel", "parallel"),
    )(k, v, q, q, q, do, do, do, lse, lse, lse, delta, delta, delta)


def _place():
    x, y, c = lax.axis_index("x"), lax.axis_index("y"), lax.axis_index("c")
    chips = [(1 - x, y), (x, 1 - y), (1 - x, 1 - y)]
    return x, y, c, chips


HBM = pl.BlockSpec(memory_space=pltpu.HBM)
SEM = pl.BlockSpec(memory_space=pltpu.SEMAPHORE)
DATAFLOW = pltpu.SideEffectType.DATAFLOW_SIDE_EFFECTING


def _exchange_copies(kind, srcs, dsts, send_sems, recv_sems):
    x, y, c, chips = _place()
    mine = 2 * x + y
    cps = []
    for t in range(len(srcs)):
        for k, (px, py) in enumerate(chips):
            src = srcs[t] if kind == "gather" else srcs[t].at[2 * px + py]
            dst = dsts[t].at[mine] if kind == "gather" else dsts[t].at[k]
            cps.append(pltpu.make_async_remote_copy(src_ref=src, dst_ref=dst, send_sem=send_sems.at[3 * t + k],
                                                    recv_sem=recv_sems.at[3 * t + k], device_id=(px, py, c), device_id_type=MESH))
    return cps


def _exchange_start(name, kind, groups):
    sizes = [len(g) for g in groups]
    n, n_g = sum(sizes), len(groups)

    def body(*refs):
        srcs, dsts = refs[:n], refs[n:2 * n]
        sems = refs[2 * n:2 * n + 2 * n_g]
        token = refs[4 * n + 2 * n_g]
        off = 0
        for gi, size in enumerate(sizes):
            for cp in _exchange_copies(kind, srcs[off:off + size], dsts[off:off + size], sems[2 * gi], sems[2 * gi + 1]):
                cp.start()
            off += size
        token[...] = jnp.zeros_like(token)

    arrays = [pltpu.with_memory_space_constraint(a, pltpu.HBM) for a in
              [s for g in groups for s, _ in g] + [d for g in groups for _, d in g]]
    sem_shapes = []
    for size in sizes:
        sem_shapes += [pltpu.SemaphoreType.DMA((3 * size,))] * 2
    outs = pl.pallas_call(
        body, name=name,
        in_specs=[HBM] * (2 * n),
        out_specs=[SEM] * (2 * n_g) + [HBM] * (2 * n) + [pl.BlockSpec(memory_space=pltpu.VMEM)],
        out_shape=sem_shapes + [pltpu.HBM(a.shape, a.dtype) for a in arrays] + [jax.ShapeDtypeStruct((8, LANES), F32)],
        input_output_aliases={t: 2 * n_g + t for t in range(2 * n)},
        compiler_params=pltpu.CompilerParams(has_side_effects=DATAFLOW),
    )(*arrays)
    sems, thru, token = outs[:2 * n_g], outs[2 * n_g:-1], outs[-1]
    handles, off = [], 0
    for gi, size in enumerate(sizes):
        handles.append((sems[2 * gi], sems[2 * gi + 1], thru[off:off + size], thru[n + off:n + off + size]))
        off += size
    return handles, token


def _exchange_wait(name, kind, handle, after):
    send_sems, recv_sems, srcs, dsts = handle
    n = len(srcs)

    def body(*refs):
        for cp in _exchange_copies(kind, refs[:n], refs[n:2 * n], refs[2 * n], refs[2 * n + 1]):
            cp.wait_send()
            cp.wait_recv()

    outs = pl.pallas_call(
        body, name=name,
        in_specs=[HBM] * (2 * n) + [SEM, SEM, ANY], out_specs=[HBM] * (2 * n),
        out_shape=[pltpu.HBM(a.shape, a.dtype) for a in (*srcs, *dsts)],
        input_output_aliases={t: t for t in range(2 * n)},
        compiler_params=pltpu.CompilerParams(has_side_effects=DATAFLOW),
    )(*srcs, *dsts, send_sems, recv_sems, after)
    return outs[n:]


def _prepare_shard(name, w, idx, dtype, mine):
    _, R, C = w.shape
    tr = _row_tile(R)

    def body(mine_ref, w_ref, src_ref, land_ref):
        val = w_ref[...].astype(dtype)
        src_ref[...] = val
        land_ref[...] = val

    return pl.pallas_call(
        body, name=name,
        grid_spec=pltpu.PrefetchScalarGridSpec(
            num_scalar_prefetch=1, grid=(R // tr,),
            in_specs=[pl.BlockSpec((None, tr, C), lambda i, s: (idx, i, 0))],
            out_specs=[pl.BlockSpec((tr, C), lambda i, s: (i, 0)), pl.BlockSpec((None, tr, C), lambda i, s: (s[0], i, 0))]),
        out_shape=[jax.ShapeDtypeStruct((R, C), dtype), jax.ShapeDtypeStruct((N_SHARDS, R, C), dtype)],
        compiler_params=_cp("parallel"),
    )(mine, w)


def _swap_with_sibling(parts):
    n = len(parts)

    def body(*refs):
        ins, outs = refs[:n], refs[n:2 * n]
        send_sems, recv_sems = refs[2 * n:]
        x, y, c, _ = _place()
        cps = [pltpu.make_async_remote_copy(src_ref=ins[t], dst_ref=outs[t], send_sem=send_sems.at[t], recv_sem=recv_sems.at[t],
                                            device_id=(x, y, 1 - c), device_id_type=MESH) for t in range(n)]
        for cp in cps:
            cp.start()
        for cp in cps:
            cp.wait_recv()
        for cp in cps:
            cp.wait_send()

    return pl.pallas_call(
        body, name="swap_partial_grads", in_specs=[ANY] * n, out_specs=[ANY] * n,
        out_shape=[jax.ShapeDtypeStruct(p.shape, p.dtype) for p in parts],
        scratch_shapes=[pltpu.SemaphoreType.DMA((n,)), pltpu.SemaphoreType.DMA((n,))],
    )(*parts)


def _allreduce_small(v):
    rows = v.shape[0]

    def body(v_ref, out_ref, buf, send_sems, recv_sems):
        x, y, c, chips = _place()
        me, sibling = (x, y, c), (x, y, 1 - c)

        def slot(px, py, pc):
            return buf.at[4 * px + 2 * py + pc]

        def copy(k, block, to, src=None):
            return pltpu.make_async_remote_copy(
                src_ref=slot(*block) if src is None else src, dst_ref=slot(*block), send_sem=send_sems.at[k],
                recv_sem=recv_sems.at[k], device_id=to, device_id_type=MESH)

        slot(*me)[...] = v_ref[...]
        first = [copy(0, me, sibling, src=v_ref)] + [copy(1 + j, me, (*chip, c), src=v_ref) for j, chip in enumerate(chips)]
        for cp in first:
            cp.start()
        passed = [copy(4 + j, (*chip, c), sibling) for j, chip in enumerate(chips)]
        for j, chip in enumerate(chips):
            copy(1 + j, (*chip, c), me).wait_recv()
            passed[j].start()
        copy(0, sibling, me).wait_recv()
        for j, chip in enumerate(chips):
            copy(4 + j, (*chip, 1 - c), me).wait_recv()
        for cp in first + passed:
            cp.wait_send()
        acc = buf[0]
        for k in range(1, 8):
            acc = acc + buf[k]
        out_ref[...] = acc

    return pl.pallas_call(
        body, name="allreduce_small_grads",
        in_specs=[pl.BlockSpec(memory_space=pltpu.VMEM)], out_specs=pl.BlockSpec(memory_space=pltpu.VMEM),
        out_shape=jax.ShapeDtypeStruct((rows, LANES), F32),
        scratch_shapes=[pltpu.VMEM((8, rows, LANES), F32), pltpu.SemaphoreType.DMA((7,)), pltpu.SemaphoreType.DMA((7,))],
        compiler_params=pltpu.CompilerParams(vmem_limit_bytes=VMEM_LIMIT_BYTES),
    )(v)


MM_TM = 1024


def _sq_relu_epilogue(acc):
    r = jnp.maximum(acc, 0.0)
    return acc, r * r


def _add_epilogue(acc, x):
    return (acc + x,)


def _sq_relu_grad_epilogue(acc, a):
    return (acc * (2.0 * jnp.maximum(a.astype(F32), 0.0)),)


STAGES = ("mixer", "mlp")


def _stage_tensors(layer, stage):
    i = layer // 2
    if stage == "mlp":
        return [("mlp_w1", layer), ("mlp_w2", layer)]
    return [("ab_w_in", i), ("b_conv_w", i), ("ab_w_out", i)] if layer % 2 == 0 else [("c_w_qkv", i), ("c_w_out", i)]


def _local_step(x, target, p, weights_of, grads_done):
    S, D = x.shape
    depth = p["mix_norm_g"].shape[0]
    n_even = (depth + 1) // 2
    mix_g3 = p["mix_norm_g"].reshape(depth, 1, D)
    mlp_g3 = p["mlp_norm_g"].reshape(depth, 1, D)
    vec3 = lambda t: t.reshape(t.shape[0], 1, t.shape[1])
    spw16 = p["a_spatial_w"].astype(BF16)
    spw16_t = jnp.swapaxes(spw16, 2, 3)
    bias_full = jnp.repeat(jnp.swapaxes(p["a_spatial_b"], 1, 2), HEAD_DIM, axis=2)
    vn_g, vn_b, cn_g, cn_b, cb3 = (vec3(p[k]) for k in ("a_vnorm_g", "a_vnorm_b", "b_norm_g", "b_norm_b", "b_conv_b"))
    tables = _rope_tables(S)
    gq = jnp.tile(p["c_q_norm_g"], (1, 2))
    gk = jnp.tile(p["c_k_norm_g"], (1, 2))

    saved = []
    for layer in range(depth):
        i = layer // 2
        wl = dict(weights_of(layer, "mixer", x))
        rec = {"x_mix": x, "w": wl}
        h = _rms_fwd(f"mix_norm_{layer}", x, mix_g3, layer)
        rec["h_mix"] = h
        if layer % 2 == 0:
            (z,) = _mm_ngroup(f"ab_in_{layer}", h, wl["ab_w_in"], nt=False, tm=MM_TM, out_dtypes=[F32])
            gconv = _glu_conv_fwd(f"glu_conv_{layer}", z, wl["b_conv_w"], cb3, i)
            cat = _ab_tail_fwd(f"ab_tail_{layer}", z, gconv, spw16, bias_full, vn_g, vn_b, cn_g, cn_b, i)
            (x,) = _mm_kgroup(f"ab_out_{layer}", cat, wl["ab_w_out"], nt=False, tm=MM_TM, out_dtypes=[F32],
                              extras=(x,), epilogue=_add_epilogue)
            rec.update(z=z, gconv=gconv, cat=cat)
        else:
            (qkv,) = _mm_ngroup(f"c_qkv_{layer}", h, wl["c_w_qkv"], nt=False, tm=MM_TM, out_dtypes=[F32])
            qn, kn = _qk_fwd(f"qk_norm_rope_{layer}", qkv, gq[i:i + 1], gk[i:i + 1], tables)
            os, lses = zip(*[_attn_fwd(f"attn_d{d}_{layer}", qn, kn, qkv, V_COL, d) for d in PATTERN_DILATIONS])
            o, lse = _attn_merge(os, lses)
            (x,) = _mm_kgroup(f"c_out_{layer}", o, wl["c_w_out"], nt=False, tm=MM_TM, out_dtypes=[F32],
                              extras=(x,), epilogue=_add_epilogue)
            rec.update(qkv=qkv, qn=qn, kn=kn, o=o, lse=lse)
        rec["x_mlp"] = x
        wl.update(weights_of(layer, "mlp", x))
        h = _rms_fwd(f"mlp_norm_{layer}", x, mlp_g3, layer)
        a, hsq = _mm_ngroup(f"mlp_up_{layer}", h, wl["mlp_w1"], nt=False, tm=MM_TM, out_dtypes=[BF16, BF16],
                            epilogue=_sq_relu_epilogue)
        (x,) = _mm_kgroup(f"mlp_down_{layer}", hsq, wl["mlp_w2"], nt=False, tm=MM_TM, out_dtypes=[F32],
                          extras=(x,), epilogue=_add_epilogue)
        rec.update(h_mlp=h, a=a, hsq=hsq)
        saved.append(rec)

    dx, loss_row = _loss_grad(x, target)

    small = {k: [None] * v.shape[0] for k, v in p.items()}
    token = None
    for layer in reversed(range(depth)):
        i = layer // 2
        rec = saved[layer]
        wl = rec["w"]
        g = {}
        (da,) = _mm_ngroup(f"mlp_down_dgrad_{layer}", dx, wl["mlp_w2"], nt=True, tm=MM_TM, out_dtypes=[BF16],
                           extras=(rec["a"],), epilogue=_sq_relu_grad_epilogue, anchor=token)
        g["mlp_w2"] = _wgrad(f"mlp_down_wgrad_{layer}", rec["hsq"], dx, wl["mlp_w2"].shape, a_group=True, tm=MM_TM)
        g["mlp_w1"] = _wgrad(f"mlp_up_wgrad_{layer}", rec["h_mlp"], da, wl["mlp_w1"].shape, a_group=False, tm=MM_TM)
        (dh,) = _mm_kgroup(f"mlp_up_dgrad_{layer}", da, wl["mlp_w1"], nt=True, tm=MM_TM, out_dtypes=[F32])
        dx, small["mlp_norm_g"][layer] = _rms_bwd(f"mlp_norm_bwd_{layer}", rec["x_mlp"], mlp_g3, layer, dh, dx)
        token = grads_done(layer, "mlp", g)
        g = {}
        if layer % 2 == 0:
            (dcat,) = _mm_ngroup(f"ab_out_dgrad_{layer}", dx, wl["ab_w_out"], nt=True, tm=MM_TM, out_dtypes=[F32], anchor=token)
            g["ab_w_out"] = _wgrad(f"ab_out_wgrad_{layer}", rec["cat"], dx, wl["ab_w_out"].shape, a_group=True, tm=MM_TM)
            dz, dgconv, dspw, dbias, dvg, dvb, dcg, dcb = _ab_tail_bwd(
                f"ab_tail_bwd_{layer}", rec["z"], rec["gconv"], dcat, spw16, spw16_t, bias_full, vn_g, vn_b, cn_g, cn_b, i)
            dz, gf, gb, dcbias = _glu_conv_bwd(f"glu_conv_bwd_{layer}", rec["z"], dgconv, dz, wl["b_conv_w"])
            g["b_conv_w"] = (gf, gb)
            small["a_spatial_w"][i] = dspw
            small["a_spatial_b"][i] = _fold_bias(dbias)[:, :A_GROUPS].T
            for k, val in (("a_vnorm_g", dvg), ("a_vnorm_b", dvb), ("b_norm_g", dcg), ("b_norm_b", dcb), ("b_conv_b", dcbias)):
                small[k][i] = val
            g["ab_w_in"] = _wgrad(f"ab_in_wgrad_{layer}", rec["h_mix"], dz, wl["ab_w_in"].shape, a_group=False, tm=MM_TM)
            (dh,) = _mm_kgroup(f"ab_in_dgrad_{layer}", dz, wl["ab_w_in"], nt=True, tm=MM_TM, out_dtypes=[F32])
        else:
            (do,) = _mm_ngroup(f"c_out_dgrad_{layer}", dx, wl["c_w_out"], nt=True, tm=MM_TM, out_dtypes=[F32], anchor=token)
            g["c_w_out"] = _wgrad(f"c_out_wgrad_{layer}", rec["o"], dx, wl["c_w_out"].shape, a_group=True, tm=MM_TM)
            delta = _attn_delta(do, rec["o"])
            attn_args = (rec["qn"], rec["kn"], rec["qkv"], V_COL, do, rec["lse"], delta)
            dqs = [_attn_bwd_q(f"attn_bwd_q_d{d}_{layer}", *attn_args, d) for d in PATTERN_DILATIONS]
            dks, dvs = zip(*[_attn_bwd_kv(f"attn_bwd_kv_d{d}_{layer}", *attn_args, d) for d in PATTERN_DILATIONS])
            dqkv, dgq, dgk = _qk_bwd(f"qk_norm_rope_bwd_{layer}", rec["qkv"], gq[i:i + 1], gk[i:i + 1], tables, dqs, dks, dvs)
            small["c_q_norm_g"][i] = dgq[:, :HEAD_DIM]
            small["c_k_norm_g"][i] = dgk[:, :HEAD_DIM]
            g["c_w_qkv"] = _wgrad(f"c_qkv_wgrad_{layer}", rec["h_mix"], dqkv, wl["c_w_qkv"].shape, a_group=False, tm=MM_TM)
            (dh,) = _mm_kgroup(f"c_qkv_dgrad_{layer}", dqkv, wl["c_w_qkv"], nt=True, tm=MM_TM, out_dtypes=[F32])
        dx, small["mix_norm_g"][layer] = _rms_bwd(f"mix_norm_bwd_{layer}", rec["x_mix"], mix_g3, layer, dh, dx)
        token = grads_done(layer, "mixer", g)

    small = {k: jnp.stack([t.reshape(p[k].shape[1:]) for t in v]) for k, v in small.items()}
    return loss_row, dx, small


SHARDED = ("mlp_w1", "mlp_w2", "ab_w_in", "b_conv_w", "ab_w_out", "c_w_qkv", "c_w_out")
SMALL = ("mix_norm_g", "mlp_norm_g", "a_spatial_w", "a_spatial_b", "a_vnorm_g", "a_vnorm_b", "b_conv_b", "b_norm_g",
         "b_norm_b", "c_q_norm_g", "c_k_norm_g")
WEIGHTS = ("mix_norm_g", "mlp_norm_g", "mlp_w1", "mlp_w2", "ab_w_in", "a_spatial_w", "a_spatial_b", "a_vnorm_g",
           "a_vnorm_b", "b_conv_w", "b_conv_b", "b_norm_g", "b_norm_b", "ab_w_out", "c_w_qkv", "c_q_norm_g",
           "c_k_norm_g", "c_w_out")


def _pack(parts):
    flat = jnp.concatenate([parts[k].reshape(-1) for k in SMALL])
    rows = -(-flat.shape[0] // (256 * LANES)) * 256
    return jnp.pad(flat, (0, rows * LANES - flat.shape[0])).reshape(rows, LANES)


def _unpack(packed, like):
    flat = packed.reshape(-1)
    out, off = {}, 0
    for k in SMALL:
        n = like[k].size
        out[k] = flat[off:off + n].reshape(like[k].shape)
        off += n
    return out


def kernel(x, mix_norm_g, mlp_norm_g, mlp_w1, mlp_w2, ab_w_in, a_spatial_w, a_spatial_b, a_vnorm_g, a_vnorm_b, b_conv_w, b_conv_b, b_norm_g, b_norm_b, ab_w_out, c_w_qkv, c_q_norm_g, c_k_norm_g, c_w_out, loss_target, m_mix_norm_g, m_mlp_norm_g, m_mlp_w1, m_mlp_w2, m_ab_w_in, m_a_spatial_w, m_a_spatial_b, m_a_vnorm_g, m_a_vnorm_b, m_b_conv_w, m_b_conv_b, m_b_norm_g, m_b_norm_b, m_ab_w_out, m_c_w_qkv, m_c_q_norm_g, m_c_k_norm_g, m_c_w_out, v_mix_norm_g, v_mlp_norm_g, v_mlp_w1, v_mlp_w2, v_ab_w_in, v_a_spatial_w, v_a_spatial_b, v_a_vnorm_g, v_a_vnorm_b, v_b_conv_w, v_b_conv_b, v_b_norm_g, v_b_norm_b, v_ab_w_out, v_c_w_qkv, v_c_q_norm_g, v_c_k_norm_g, v_c_w_out):
    w = dict(mix_norm_g=mix_norm_g, mlp_norm_g=mlp_norm_g, mlp_w1=mlp_w1, mlp_w2=mlp_w2, ab_w_in=ab_w_in,
             a_spatial_w=a_spatial_w, a_spatial_b=a_spatial_b, a_vnorm_g=a_vnorm_g, a_vnorm_b=a_vnorm_b,
             b_conv_w=b_conv_w, b_conv_b=b_conv_b, b_norm_g=b_norm_g, b_norm_b=b_norm_b, ab_w_out=ab_w_out,
             c_w_qkv=c_w_qkv, c_q_norm_g=c_q_norm_g, c_k_norm_g=c_k_norm_g, c_w_out=c_w_out)
    m = dict(mix_norm_g=m_mix_norm_g, mlp_norm_g=m_mlp_norm_g, mlp_w1=m_mlp_w1, mlp_w2=m_mlp_w2, ab_w_in=m_ab_w_in,
             a_spatial_w=m_a_spatial_w, a_spatial_b=m_a_spatial_b, a_vnorm_g=m_a_vnorm_g, a_vnorm_b=m_a_vnorm_b,
             b_conv_w=m_b_conv_w, b_conv_b=m_b_conv_b, b_norm_g=m_b_norm_g, b_norm_b=m_b_norm_b, ab_w_out=m_ab_w_out,
             c_w_qkv=m_c_w_qkv, c_q_norm_g=m_c_q_norm_g, c_k_norm_g=m_c_k_norm_g, c_w_out=m_c_w_out)
    v = dict(mix_norm_g=v_mix_norm_g, mlp_norm_g=v_mlp_norm_g, mlp_w1=v_mlp_w1, mlp_w2=v_mlp_w2, ab_w_in=v_ab_w_in,
             a_spatial_w=v_a_spatial_w, a_spatial_b=v_a_spatial_b, a_vnorm_g=v_a_vnorm_g, a_vnorm_b=v_a_vnorm_b,
             b_conv_w=v_b_conv_w, b_conv_b=v_b_conv_b, b_norm_g=v_b_norm_g, b_norm_b=v_b_norm_b, ab_w_out=v_ab_w_out,
             c_w_qkv=v_c_w_qkv, c_q_norm_g=v_c_q_norm_g, c_k_norm_g=v_c_k_norm_g, c_w_out=v_c_w_out)

    S, D = x.shape[1], x.shape[2]
    depth = mix_norm_g.shape[0]
    mine = (2 * lax.axis_index("x") + lax.axis_index("y")).astype(jnp.int32).reshape(1)

    stages = [(layer, stage) for layer in range(depth) for stage in STAGES]
    groups = [[(k, i) + tuple(_prepare_shard(f"prepare_{k}_{i}", w[k], i, F32 if k == "b_conv_w" else BF16, mine))
               for k, i in _stage_tensors(*st)] for st in stages]
    handles, gather_token = _exchange_start("gather_weights_start", "gather", [[(s, l) for _, _, s, l in g] for g in groups])
    handles = dict(zip(stages, handles))

    def weights_of(layer, stage, after):
        got = _exchange_wait(f"gather_weights_wait_{layer}_{stage}", "gather", handles[layer, stage],
                             gather_token if (layer, stage) == stages[0] else after)
        return {k: a for (k, _), a in zip(_stage_tensors(layer, stage), got)}

    scattered = {}

    def grads_done(layer, stage, g):
        names = [k for k, _ in _stage_tensors(layer, stage)]
        group = [(g[k][1], lax.empty((3,) + g[k][1].shape[1:], BF16)) for k in names]
        (handle,), token = _exchange_start(f"scatter_grads_start_{layer}_{stage}", "scatter", [group])
        scattered[layer, stage] = (handle, [g[k][0] for k in names])
        return token

    small_params = {k: w[k] for k in SMALL}
    loss_row, dx, small_grads = _local_step(x.reshape(S, D), loss_target.reshape(S, D), small_params, weights_of, grads_done)

    loss = lax.psum(loss_row[0, 0], ("x", "y", "c"))

    partial, order = [], []
    for layer, stage in reversed(stages):
        handle, gfs = scattered[layer, stage]
        recvs = _exchange_wait(f"scatter_grads_wait_{layer}_{stage}", "scatter", handle, dx)
        tensors = _stage_tensors(layer, stage)
        partial += [_sum4(f"sum_chips_{k}_{i}", gf, r, mine) for (k, i), gf, r in zip(tensors, gfs, recvs)]
        order += tensors
    other = _swap_with_sibling(partial)
    stacked = {k: [lax.empty(w[k].shape, F32) for _ in range(4)] for k in SHARDED}
    for (k, i), a, b in zip(order, partial, other):
        stacked[k] = _adamw_layer(f"adamw_{k}_{i}", w[k], m[k], v[k], i, a, b, stacked[k])
    grads, deltas, new_m, new_v = ({k: stacked[k][j] for k in SHARDED} for j in range(4))

    g_small = _allreduce_small(_pack(small_grads))
    outs = _adamw("adamw_small", _pack(small_params), _pack({k: m[k] for k in SMALL}), _pack({k: v[k] for k in SMALL}), g_small)
    for d_, packed in zip((grads, deltas, new_m, new_v), outs):
        d_.update(_unpack(packed, small_params))

    return (loss, dx.reshape(1, S, D), *[grads[k] for k in WEIGHTS], *[deltas[k] for k in WEIGHTS],
            *[new_m[k] for k in WEIGHTS], *[new_v[k] for k in WEIGHTS])
```

```python
import functools

import jax
import jax.numpy as jnp
from jax import lax
from jax.experimental import pallas as pl
from jax.experimental.pallas import tpu as pltpu

F32, BF16 = jnp.float32, jnp.bfloat16
MESH = pl.DeviceIdType.MESH
ANY = pl.BlockSpec(memory_space=pl.ANY)

VMEM_LIMIT_BYTES = 56 * 1024 * 1024
LANES = 128
ELEMENTWISE_ROWS = 256

EPS = 1e-6
NEG = -1e30
HEAD_DIM = 64
N_HEADS = 16
CHUNK = 128
A_GROUPS = 8
CONV_WIDTH = 31
CONV_HALO = 16
BAND = 64
PATTERN_DILATIONS = (1, 4, 16)
ROT_DIM = 16
ROPE_THETA = 500000.0
N_SHARDS = 4

ADAM_LR, ADAM_B1, ADAM_B2, ADAM_EPS, ADAM_WD, ADAM_STEP = 0.001, 0.9, 0.999, 1e-08, 0.01, 10


def _cp(*sem):
    return pltpu.CompilerParams(dimension_semantics=sem, vmem_limit_bytes=VMEM_LIMIT_BYTES)


def _tile(n, pref):
    t = min(n, pref)
    assert n % t == 0, (n, pref)
    return t


def _dot(a, b, ca, cb):
    return lax.dot_general(a, b, (((ca,), (cb,)), ((), ())), preferred_element_type=F32)


def _mm_ngroup(name, a, w, *, nt, tm, out_dtypes, extras=(), epilogue=None, anchor=None):
    M, K = a.shape
    G, R, C = w.shape
    nw = R if nt else C
    assert K == (C if nt else R)
    tm = _tile(M, tm)
    n_ex = len(extras)
    anchors = [] if anchor is None else [anchor]

    def body(a_ref, w_ref, *rest):
        rest = rest[len(anchors):]
        acc = _dot(a_ref[...].astype(BF16), w_ref[...], 1, 1 if nt else 0)
        res = epilogue(acc, *[e[...] for e in rest[:n_ex]]) if epilogue else (acc,)
        for o_ref, r in zip(rest[n_ex:], res):
            o_ref[...] = r.astype(o_ref.dtype)

    blk = pl.BlockSpec((tm, nw), lambda m, g: (m, g))
    return pl.pallas_call(
        body, name=name, grid=(M // tm, G),
        in_specs=[pl.BlockSpec((tm, K), lambda m, g: (m, 0)), pl.BlockSpec((None, R, C), lambda m, g: (g, 0, 0))]
        + [pl.BlockSpec((8, LANES), lambda m, g: (0, 0))] * len(anchors) + [blk] * n_ex,
        out_specs=[blk] * len(out_dtypes),
        out_shape=[jax.ShapeDtypeStruct((M, G * nw), dt) for dt in out_dtypes],
        compiler_params=_cp("parallel", "parallel"),
    )(a, w, *anchors, *extras)


def _mm_kgroup(name, a, w, *, nt, tm, out_dtypes, extras=(), vecs=(), n_sums=0, epilogue=None):
    G, R, C = w.shape
    kw, N = (C, R) if nt else (R, C)
    if a.ndim == 3:
        M = a.shape[1]
        assert a.shape[0] == G and a.shape[2] == kw
    else:
        M = a.shape[0]
        assert a.shape[1] == G * kw
    tm = _tile(M, tm)
    n_ex = len(extras)
    a_spec = (pl.BlockSpec((G, tm, kw), lambda m: (0, m, 0)) if a.ndim == 3 else pl.BlockSpec((tm, G * kw), lambda m: (m, 0)))

    def body(a_ref, w_ref, *rest):
        acc = None
        for g in range(G):
            a_g = a_ref[g] if a.ndim == 3 else a_ref[:, g * kw:(g + 1) * kw]
            part = _dot(a_g.astype(BF16), w_ref[g], 1, 1 if nt else 0)
            acc = part if acc is None else acc + part
        n_in = n_ex + len(vecs)
        res = epilogue(acc, *[e[...] for e in rest[:n_in]]) if epilogue else (acc,)
        outs = rest[n_in:]
        n_tiles = len(outs) - n_sums
        for o_ref, r in zip(outs[:n_tiles], res[:n_tiles]):
            o_ref[...] = r.astype(o_ref.dtype)
        if n_sums:
            @pl.when(pl.program_id(0) == 0)
            def _():
                for s_ref in outs[n_tiles:]:
                    s_ref[...] = jnp.zeros_like(s_ref)

            for s_ref, r in zip(outs[n_tiles:], res[n_tiles:]):
                s_ref[...] += r

    blk = pl.BlockSpec((tm, N), lambda m: (m, 0))
    row = pl.BlockSpec((1, N), lambda m: (0, 0))
    return pl.pallas_call(
        body, name=name, grid=(M // tm,),
        in_specs=[a_spec, pl.BlockSpec((G, R, C), lambda m: (0, 0, 0))] + [blk] * n_ex
        + [pl.BlockSpec((None, 1, N), lambda m, i=i: (i, 0, 0)) for _, i in vecs],
        out_specs=[blk] * len(out_dtypes) + [row] * n_sums,
        out_shape=[jax.ShapeDtypeStruct((M, N), dt) for dt in out_dtypes] + [jax.ShapeDtypeStruct((1, N), F32)] * n_sums,
        compiler_params=_cp("arbitrary" if n_sums else "parallel"),
    )(a, w, *extras, *[v for v, _ in vecs])


def _wgrad(name, a, b, shape, *, a_group, tm):
    G, R, C = shape
    M = a.shape[0]
    tm = _tile(M, tm)
    n_m = M // tm

    def body(a_ref, b_ref, gf_ref, gb_ref):
        m = pl.program_id(1)
        part = _dot(a_ref[...].astype(BF16), b_ref[...].astype(BF16), 0, 0)

        @pl.when(m == 0)
        def _():
            gf_ref[...] = part

        @pl.when(m > 0)
        def _():
            gf_ref[...] += part

        @pl.when(m == n_m - 1)
        def _():
            gb_ref[...] = gf_ref[...].astype(BF16)

    a_spec = pl.BlockSpec((tm, R), (lambda g, m: (m, g)) if a_group else (lambda g, m: (m, 0)))
    if b.ndim == 3:
        assert not a_group
        b_spec = pl.BlockSpec((None, tm, C), lambda g, m: (g, m, 0))
    else:
        b_spec = pl.BlockSpec((tm, C), (lambda g, m: (m, 0)) if a_group else (lambda g, m: (m, g)))
    o_spec = pl.BlockSpec((None, R, C), lambda g, m: (g, 0, 0))
    return pl.pallas_call(
        body, name=name, grid=(G, n_m),
        in_specs=[a_spec, b_spec], out_specs=[o_spec, o_spec],
        out_shape=[jax.ShapeDtypeStruct(shape, F32), jax.ShapeDtypeStruct(shape, BF16)],
        compiler_params=_cp("parallel", "arbitrary"),
    )(a, b)


def _rms_fwd(name, x, g3, layer):
    S, D = x.shape
    tm = _tile(S, 512)

    def body(x_ref, g_ref, h_ref):
        xv = x_ref[...]
        r = lax.rsqrt(jnp.mean(xv * xv, axis=-1, keepdims=True) + EPS)
        h_ref[...] = (xv * r * g_ref[...]).astype(BF16)

    row = pl.BlockSpec((tm, D), lambda m: (m, 0))
    return pl.pallas_call(
        body, name=name, grid=(S // tm,),
        in_specs=[row, pl.BlockSpec((None, 1, D), lambda m: (layer, 0, 0))], out_specs=row,
        out_shape=jax.ShapeDtypeStruct((S, D), BF16), compiler_params=_cp("parallel"),
    )(x, g3)


def _loss_grad(y, target):
    S, D = y.shape
    tm = _tile(S, 512)

    def body(y_ref, t_ref, dy_ref, l_ref):
        e = y_ref[...] - t_ref[...]
        dy_ref[...] = e * (1.0 / D)

        @pl.when(pl.program_id(0) == 0)
        def _():
            l_ref[...] = jnp.zeros_like(l_ref)

        l_ref[...] += (0.5 / D) * jnp.sum(jnp.sum(e * e, axis=1, keepdims=True), axis=0, keepdims=True)

    row = pl.BlockSpec((tm, D), lambda m: (m, 0))
    return pl.pallas_call(
        body, name="loss_grad", grid=(S // tm,), in_specs=[row, row],
        out_specs=[row, pl.BlockSpec((1, LANES), lambda m: (0, 0))],
        out_shape=[jax.ShapeDtypeStruct((S, D), F32), jax.ShapeDtypeStruct((1, LANES), F32)],
        compiler_params=_cp("arbitrary"),
    )(y, target)


def _adamw_math(w, m, v, g):
    m2 = ADAM_B1 * m + (1.0 - ADAM_B1) * g
    v2 = ADAM_B2 * v + (1.0 - ADAM_B2) * jnp.square(g)
    m_hat = m2 / (1.0 - ADAM_B1 ** ADAM_STEP)
    v_hat = v2 / (1.0 - ADAM_B2 ** ADAM_STEP)
    return g, -ADAM_LR * (m_hat / (jnp.sqrt(v_hat) + ADAM_EPS) + ADAM_WD * w), m2, v2


def _row_tile(rows):
    return _tile(rows, ELEMENTWISE_ROWS) if rows % ELEMENTWISE_ROWS == 0 else rows


def _adamw(name, w, m, v, g):
    rows, C = w.shape
    tr = _row_tile(rows)

    def body(w_ref, m_ref, v_ref, g_in, g_ref, d_ref, nm_ref, nv_ref):
        for o_ref, val in zip((g_ref, d_ref, nm_ref, nv_ref), _adamw_math(w_ref[...], m_ref[...], v_ref[...], g_in[...])):
            o_ref[...] = val

    blk = pl.BlockSpec((tr, C), lambda i: (i, 0))
    return pl.pallas_call(
        body, name=name, grid=(rows // tr,), in_specs=[blk] * 4, out_specs=[blk] * 4,
        out_shape=[jax.ShapeDtypeStruct((rows, C), F32)] * 4, compiler_params=_cp("parallel"),
    )(w, m, v, g)


def _adamw_layer(name, w, m, v, layer, mine, theirs, outs):
    _, R, C = w.shape
    tr = _row_tile(R)

    def body(w_ref, m_ref, v_ref, a_ref, b_ref, *rest):
        g = a_ref[...] + b_ref[...]
        for o_ref, val in zip(rest[4:], _adamw_math(w_ref[...], m_ref[...], v_ref[...], g)):
            o_ref[...] = val

    st = pl.BlockSpec((None, tr, C), lambda i: (layer, i, 0))
    part = pl.BlockSpec((tr, C), lambda i: (i, 0))
    return pl.pallas_call(
        body, name=name, grid=(R // tr,), in_specs=[st] * 3 + [part] * 2 + [ANY] * 4, out_specs=[st] * 4,
        out_shape=[jax.ShapeDtypeStruct(w.shape, F32)] * 4, input_output_aliases={5 + j: j for j in range(4)},
        compiler_params=_cp("parallel"),
    )(w, m, v, mine, theirs, *outs)


def _sum4(name, gf, recv, mine):
    _, R, C = gf.shape
    tr = _row_tile(R)

    def body(mine_ref, o_ref, r_ref, out_ref):
        acc = o_ref[...]
        for k in range(3):
            acc = acc + r_ref[k].astype(F32)
        out_ref[...] = acc

    return pl.pallas_call(
        body, name=name,
        grid_spec=pltpu.PrefetchScalarGridSpec(
            num_scalar_prefetch=1, grid=(R // tr,),
            in_specs=[pl.BlockSpec((None, tr, C), lambda i, s: (s[0], i, 0)), pl.BlockSpec((3, tr, C), lambda i, s: (0, i, 0))],
            out_specs=pl.BlockSpec((tr, C), lambda i, s: (i, 0))),
        out_shape=jax.ShapeDtypeStruct((R, C), F32), compiler_params=_cp("parallel"),
    )(mine, gf, recv)


def _gelu(x):
    return x * (0.5 * (1.0 + jnp.tanh(0.7978845608028654 * (x + 0.044715 * (x * x * x)))))


def _layernorm(t, g, b):
    mu = jnp.mean(t, axis=-1, keepdims=True)
    var = jnp.mean(jnp.square(t - mu), axis=-1, keepdims=True)
    return (t - mu) * lax.rsqrt(var + EPS) * g + b


def _silu(x):
    return x * jax.nn.sigmoid(x)


def _a_value(zv, g, b):
    return _layernorm(_gelu(zv), g, b)


def _b_tail(gc, g, b):
    return _silu(_layernorm(gc, g, b))


def _first_head(shape):
    return lax.broadcasted_iota(jnp.int32, shape, len(shape) - 1) < HEAD_DIM


def _spatial_mix(spw_ref, vb, tm):
    first = _first_head((CHUNK, LANES))
    rows = []
    for n in range(tm // CHUNK):
        blocks = []
        for j in range(A_GROUPS // 2):
            vblk = vb[n * CHUNK:(n + 1) * CHUNK, j * LANES:(j + 1) * LANES]
            r0 = _dot(spw_ref[2 * j], vblk, 1, 0)
            r1 = _dot(spw_ref[2 * j + 1], vblk, 1, 0)
            blocks.append(jnp.where(first, r0, r1))
        rows.append(jnp.concatenate(blocks, axis=1))
    return jnp.concatenate(rows, axis=0) if len(rows) > 1 else rows[0]


def _ab_tail_fwd(name, z, gconv, spw, bias_full, vn_g, vn_b, cn_g, cn_b, layer):
    S = z.shape[0]
    AW = 512
    tm = _tile(S, 256)

    def body(zu_ref, zv_ref, gc_ref, spw_ref, bias_ref, vg_ref, vb_ref, cg_ref, cb_ref, cat_ref):
        u = _gelu(zu_ref[...])
        v = _a_value(zv_ref[...], vg_ref[...], vb_ref[...])
        sv = _spatial_mix(spw_ref, v.astype(BF16), tm) + jnp.tile(bias_ref[...], (tm // CHUNK, 1))
        cat_ref[:, :AW] = (u * sv).astype(BF16)
        cat_ref[:, AW:] = _b_tail(gc_ref[...], cg_ref[...], cb_ref[...]).astype(BF16)

    vec = pl.BlockSpec((None, 1, AW), lambda m: (layer, 0, 0))
    return pl.pallas_call(
        body, name=name, grid=(S // tm,),
        in_specs=[pl.BlockSpec((tm, AW), lambda m: (m, 0)), pl.BlockSpec((tm, AW), lambda m: (m, 1)),
                  pl.BlockSpec((tm, AW), lambda m: (m, 0)),
                  pl.BlockSpec((None, A_GROUPS, CHUNK, CHUNK), lambda m: (layer, 0, 0, 0)),
                  pl.BlockSpec((None, CHUNK, AW), lambda m: (layer, 0, 0)), vec, vec, vec, vec],
        out_specs=pl.BlockSpec((tm, 2 * AW), lambda m: (m, 0)),
        out_shape=jax.ShapeDtypeStruct((S, 2 * AW), BF16), compiler_params=_cp("parallel"),
    )(z, z, gconv, spw, bias_full, vn_g, vn_b, cn_g, cn_b)


def _ab_tail_bwd(name, z, gconv, dcat, spw, spw_t, bias_full, vn_g, vn_b, cn_g, cn_b, layer):
    S = z.shape[0]
    AW = 512
    tm = _tile(S, 256)
    n_chunks = tm // CHUNK

    def body(zu_ref, zv_ref, gc_ref, dcat_ref, spw_ref, spwt_ref, bias_ref, vg_ref, vb_ref, cg_ref, cb_ref,
             dz_ref, dgc_ref, dspw_ref, dbias_ref, dvg_ref, dvb_ref, dcg_ref, dcb_ref):
        @pl.when(pl.program_id(0) == 0)
        def _():
            for r in (dspw_ref, dbias_ref, dvg_ref, dvb_ref, dcg_ref, dcb_ref):
                r[...] = jnp.zeros_like(r)

        dya = dcat_ref[:, :AW]
        dyb = dcat_ref[:, AW:]
        u, u_vjp = jax.vjp(_gelu, zu_ref[...])
        v, v_vjp = jax.vjp(_a_value, zv_ref[...], vg_ref[...], vb_ref[...])
        vb16 = v.astype(BF16)
        sv = _spatial_mix(spw_ref, vb16, tm) + jnp.tile(bias_ref[...], (n_chunks, 1))
        (dzu,) = u_vjp(dya * sv)
        dsv = dya * u
        dsv16 = dsv.astype(BF16)
        dv = _spatial_mix(spwt_ref, dsv16, tm)
        dzv, dvg, dvb = v_vjp(dv)
        dz_ref[0] = dzu
        dz_ref[1] = dzv
        dvg_ref[...] += dvg
        dvb_ref[...] += dvb

        first = _first_head((CHUNK, LANES))
        zero = jnp.zeros((), BF16)
        dbias = jnp.zeros((CHUNK, AW), F32)
        for n in range(n_chunks):
            rows = slice(n * CHUNK, (n + 1) * CHUNK)
            dbias = dbias + dsv[rows]
            for j in range(A_GROUPS // 2):
                cols = slice(j * LANES, (j + 1) * LANES)
                dblk, vblk = dsv16[rows, cols], vb16[rows, cols]
                dspw_ref[2 * j] += _dot(jnp.where(first, dblk, zero), vblk, 1, 1)
                dspw_ref[2 * j + 1] += _dot(jnp.where(first, zero, dblk), vblk, 1, 1)
        dbias_ref[...] += dbias

        _, t_vjp = jax.vjp(_b_tail, gc_ref[...], cg_ref[...], cb_ref[...])
        dgc, dcg, dcb = t_vjp(dyb)
        dgc_ref[...] = dgc
        dcg_ref[...] += dcg
        dcb_ref[...] += dcb

    vec = pl.BlockSpec((None, 1, AW), lambda m: (layer, 0, 0))
    spw_spec = pl.BlockSpec((None, A_GROUPS, CHUNK, CHUNK), lambda m: (layer, 0, 0, 0))
    ovec = pl.BlockSpec((1, AW), lambda m: (0, 0))
    return pl.pallas_call(
        body, name=name, grid=(S // tm,),
        in_specs=[pl.BlockSpec((tm, AW), lambda m: (m, 0)), pl.BlockSpec((tm, AW), lambda m: (m, 1)),
                  pl.BlockSpec((tm, AW), lambda m: (m, 0)), pl.BlockSpec((tm, 2 * AW), lambda m: (m, 0)),
                  spw_spec, spw_spec, pl.BlockSpec((None, CHUNK, AW), lambda m: (layer, 0, 0)), vec, vec, vec, vec],
        out_specs=[pl.BlockSpec((2, tm, AW), lambda m: (0, m, 0)), pl.BlockSpec((tm, AW), lambda m: (m, 0)),
                   pl.BlockSpec((A_GROUPS, CHUNK, CHUNK), lambda m: (0, 0, 0)),
                   pl.BlockSpec((CHUNK, AW), lambda m: (0, 0)), ovec, ovec, ovec, ovec],
        out_shape=[jax.ShapeDtypeStruct((4, S, AW), F32), jax.ShapeDtypeStruct((S, AW), F32),
                   jax.ShapeDtypeStruct((A_GROUPS, CHUNK, CHUNK), F32), jax.ShapeDtypeStruct((CHUNK, AW), F32)]
                  + [jax.ShapeDtypeStruct((1, AW), F32)] * 4,
        compiler_params=_cp("arbitrary"),
    )(z, z, gconv, dcat, spw, spw_t, bias_full, vn_g, vn_b, cn_g, cn_b)


def _fold_bias(dbias_full):
    def body(d_ref, o_ref):
        d = d_ref[...]
        hi = d.astype(BF16)
        lo = (d - hi.astype(F32)).astype(BF16)
        r = lax.broadcasted_iota(jnp.int32, (512, LANES), 0)
        c = lax.broadcasted_iota(jnp.int32, (512, LANES), 1)
        fold = jnp.where(lax.shift_right_logical(r, 6) == c, 1.0, 0.0).astype(BF16)
        o_ref[...] = _dot(hi, fold, 1, 0) + _dot(lo, fold, 1, 0)

    return pl.pallas_call(body, name="fold_spatial_bias", out_shape=jax.ShapeDtypeStruct((CHUNK, LANES), F32))(dbias_full)


def _halo_specs(tm, n_halo_blocks, col):
    r = tm // CONV_HALO
    prev = pl.BlockSpec((CONV_HALO, LANES), lambda j, i: (jnp.maximum(i * r - 1, 0), col + j))
    cur = pl.BlockSpec((tm, LANES), lambda j, i: (i, col + j))
    nxt = pl.BlockSpec((CONV_HALO, LANES), lambda j, i: (jnp.minimum((i + 1) * r, n_halo_blocks - 1), col + j))
    return [prev, cur, nxt]


def _fill_halo(scr, prev, cur, nxt, tm, i, n_i):
    scr[0:CONV_HALO, :] = jnp.where(i > 0, prev, 0.0)
    scr[CONV_HALO:CONV_HALO + tm, :] = cur
    scr[CONV_HALO + tm:2 * CONV_HALO + tm, :] = jnp.where(i < n_i - 1, nxt, 0.0)


def _glu_conv_fwd(name, z, cw, cb3, layer):
    S = z.shape[0]
    tm = _tile(S, 512)
    n_i = S // tm
    pad = CONV_WIDTH // 2

    def body(vp, vc, vn, gp, gc, gn, w_ref, b_ref, out_ref, scr):
        i = pl.program_id(1)
        glu = lambda a, b: a[...] * jax.nn.sigmoid(b[...])
        _fill_halo(scr, glu(vp, gp), glu(vc, gc), glu(vn, gn), tm, i, n_i)
        acc = jnp.zeros((tm, LANES), F32)
        for j in range(CONV_WIDTH):
            acc = acc + w_ref[j:j + 1, :] * scr[pl.ds(CONV_HALO - pad + j, tm), :]
        out_ref[...] = acc + b_ref[...]

    return pl.pallas_call(
        body, name=name, grid=(4, n_i),
        in_specs=_halo_specs(tm, S // CONV_HALO, 8) + _halo_specs(tm, S // CONV_HALO, 12)
        + [pl.BlockSpec((None, CONV_WIDTH, LANES), lambda j, i: (j, 0, 0)),
           pl.BlockSpec((None, 1, LANES), lambda j, i: (layer, 0, j))],
        out_specs=pl.BlockSpec((tm, LANES), lambda j, i: (i, j)),
        out_shape=jax.ShapeDtypeStruct((S, 4 * LANES), F32),
        scratch_shapes=[pltpu.VMEM((tm + 2 * CONV_HALO, LANES), F32)],
        compiler_params=_cp("parallel", "parallel"),
    )(z, z, z, z, z, z, cw, cb3)


def _glu_conv_bwd(name, z, dgconv, dz, cw):
    S = z.shape[0]
    tm = _tile(S, 512)
    n_i = S // tm
    pad = CONV_WIDTH // 2

    def body(vp, vc, vn, gp, gc, gn, dp, dc, dn, w_ref, dz_in, dz_ref, gf_ref, gb_ref, db_ref, g_scr, d_scr):
        i = pl.program_id(1)
        sig = jax.nn.sigmoid(gc[...])
        _fill_halo(g_scr, vp[...] * jax.nn.sigmoid(gp[...]), vc[...] * sig, vn[...] * jax.nn.sigmoid(gn[...]), tm, i, n_i)
        _fill_halo(d_scr, dp[...], dc[...], dn[...], tm, i, n_i)

        @pl.when(i == 0)
        def _():
            gf_ref[...] = jnp.zeros_like(gf_ref)
            db_ref[...] = jnp.zeros_like(db_ref)

        d_cur = dc[...]
        dglu = jnp.zeros((tm, LANES), F32)
        for j in range(CONV_WIDTH):
            dglu = dglu + w_ref[j:j + 1, :] * d_scr[pl.ds(CONV_HALO + pad - j, tm), :]
            gf_ref[j:j + 1, :] += jnp.sum(d_cur * g_scr[pl.ds(CONV_HALO - pad + j, tm), :], axis=0, keepdims=True)
        db_ref[...] += jnp.sum(d_cur, axis=0, keepdims=True)
        dz_ref[0] = dglu * sig
        dz_ref[1] = dglu * vc[...] * sig * (1.0 - sig)

        @pl.when(i == n_i - 1)
        def _():
            gb_ref[...] = gf_ref[...].astype(BF16)

    w_spec = pl.BlockSpec((None, CONV_WIDTH, LANES), lambda j, i: (j, 0, 0))
    return pl.pallas_call(
        body, name=name, grid=(4, n_i),
        in_specs=_halo_specs(tm, S // CONV_HALO, 8) + _halo_specs(tm, S // CONV_HALO, 12)
        + _halo_specs(tm, S // CONV_HALO, 0) + [w_spec, ANY],
        out_specs=[pl.BlockSpec((2, tm, LANES), lambda j, i: (1, i, j)),
                   w_spec, w_spec, pl.BlockSpec((1, LANES), lambda j, i: (0, j))],
        out_shape=[jax.ShapeDtypeStruct(dz.shape, F32), jax.ShapeDtypeStruct(cw.shape, F32),
                   jax.ShapeDtypeStruct(cw.shape, BF16), jax.ShapeDtypeStruct((1, 4 * LANES), F32)],
        input_output_aliases={10: 0},
        scratch_shapes=[pltpu.VMEM((tm + 2 * CONV_HALO, LANES), F32)] * 2,
        compiler_params=_cp("parallel", "arbitrary"),
    )(z, z, z, z, z, z, dgconv, dgconv, dgconv, cw, dz)


def _seg_matrix(scale):
    r = lax.broadcasted_iota(jnp.int32, (LANES, LANES), 0)
    c = lax.broadcasted_iota(jnp.int32, (LANES, LANES), 1)
    return jnp.where(lax.shift_right_logical(r, 6) == lax.shift_right_logical(c, 6), scale, 0.0).astype(BF16)


def _seg_sum(x, seg):
    hi = x.astype(BF16)
    lo = (x - hi.astype(F32)).astype(BF16)
    return _dot(hi, seg, 1, 0) + _dot(lo, seg, 1, 0)


def _rope_tables(S):
    pos = jnp.arange(S, dtype=F32)
    inv_freq = ROPE_THETA ** (-jnp.arange(0, ROT_DIM, 2, dtype=F32) / ROT_DIM)
    ang = pos[:, None] * inv_freq[None, :]
    cos, sin = jnp.cos(ang), jnp.sin(ang)
    half = ROT_DIM // 2
    rest = HEAD_DIM - ROT_DIM
    one, zero = jnp.ones((S, rest), F32), jnp.zeros((S, rest), F32)
    zh = jnp.zeros((S, half), F32)
    c = jnp.concatenate([cos, cos, one], axis=1)
    sa = jnp.concatenate([-sin, zh, zero], axis=1)
    sb = jnp.concatenate([zh, sin, zero], axis=1)
    return [jnp.tile(t, (1, 2)) for t in (c, sa, sb)]


def _qk_fwd(name, qkv, gq, gk, tables):
    S = qkv.shape[0]
    W = N_HEADS * HEAD_DIM
    tm = _tile(S, 256)
    half = ROT_DIM // 2

    def body(q_ref, k_ref, gq_ref, gk_ref, c_ref, sa_ref, sb_ref, qn_ref, kn_ref):
        seg = _seg_matrix(1.0 / HEAD_DIM)
        c, sa, sb = c_ref[...], sa_ref[...], sb_ref[...]
        for t_ref, g_ref, o_ref in ((q_ref, gq_ref, qn_ref), (k_ref, gk_ref, kn_ref)):
            for blk in range(W // LANES):
                cols = slice(blk * LANES, (blk + 1) * LANES)
                t = t_ref[:, cols]
                y = t * lax.rsqrt(_seg_sum(t * t, seg) + EPS) * g_ref[...]
                o_ref[:, cols] = y * c + pltpu.roll(y, LANES - half, 1) * sa + pltpu.roll(y, half, 1) * sb

    row = lambda k: pl.BlockSpec((tm, W), lambda m: (m, k))
    gain = pl.BlockSpec((1, LANES), lambda m: (0, 0))
    tab = pl.BlockSpec((tm, LANES), lambda m: (m, 0))
    return pl.pallas_call(
        body, name=name, grid=(S // tm,),
        in_specs=[row(0), row(1), gain, gain, tab, tab, tab], out_specs=[row(0)] * 2,
        out_shape=[jax.ShapeDtypeStruct((S, W), F32)] * 2, compiler_params=_cp("parallel"),
    )(qkv, qkv, gq, gk, *tables)


def _qk_bwd(name, qkv, gq, gk, tables, dqs, dks, dvs):
    S = qkv.shape[0]
    W = N_HEADS * HEAD_DIM
    tm = _tile(S, 256)
    half = ROT_DIM // 2
    n_p = len(dqs)

    def body(q_ref, k_ref, gq_ref, gk_ref, c_ref, sa_ref, sb_ref, *rest):
        dq_refs, dk_refs, dv_refs = rest[:n_p], rest[n_p:2 * n_p], rest[2 * n_p:3 * n_p]
        dqkv_ref, dgq_ref, dgk_ref = rest[3 * n_p:]

        @pl.when(pl.program_id(0) == 0)
        def _():
            dgq_ref[...] = jnp.zeros_like(dgq_ref)
            dgk_ref[...] = jnp.zeros_like(dgk_ref)

        seg = _seg_matrix(1.0 / HEAD_DIM)
        r_i = lax.broadcasted_iota(jnp.int32, (LANES, LANES), 0)
        c_i = lax.broadcasted_iota(jnp.int32, (LANES, LANES), 1)
        same_dim = jnp.where((r_i & (HEAD_DIM - 1)) == (c_i & (HEAD_DIM - 1)), 1.0, 0.0).astype(BF16)
        c, sa, sb = c_ref[...], sa_ref[...], sb_ref[...]
        for idx, (t_ref, g_ref, d_refs, dg_ref) in enumerate(((q_ref, gq_ref, dq_refs, dgq_ref),
                                                              (k_ref, gk_ref, dk_refs, dgk_ref))):
            dg = jnp.zeros((1, LANES), F32)
            for blk in range(W // LANES):
                cols = slice(blk * LANES, (blk + 1) * LANES)
                dout = d_refs[0][:, cols]
                for r in d_refs[1:]:
                    dout = dout + r[:, cols]
                dy = dout * c + pltpu.roll(dout * sa, half, 1) + pltpu.roll(dout * sb, LANES - half, 1)
                t = t_ref[:, cols]
                r_ = lax.rsqrt(_seg_sum(t * t, seg) + EPS)
                xhat = t * r_
                dg = dg + jnp.sum(dy * xhat, axis=0, keepdims=True)
                dxhat = dy * g_ref[...]
                dt = r_ * (dxhat - xhat * _seg_sum(dxhat * xhat, seg))
                dqkv_ref[:, idx * W + blk * LANES: idx * W + (blk + 1) * LANES] = dt.astype(BF16)
            dg_ref[...] += _seg_sum(jnp.broadcast_to(dg, (8, LANES)), same_dim)[0:1]
        dv = dv_refs[0][...]
        for r in dv_refs[1:]:
            dv = dv + r[...]
        dqkv_ref[:, 2 * W:] = dv.astype(BF16)

    row = lambda k: pl.BlockSpec((tm, W), lambda m: (m, k))
    gain = pl.BlockSpec((1, LANES), lambda m: (0, 0))
    tab = pl.BlockSpec((tm, LANES), lambda m: (m, 0))
    return pl.pallas_call(
        body, name=name, grid=(S // tm,),
        in_specs=[row(0), row(1), gain, gain, tab, tab, tab] + [row(0)] * (3 * n_p),
        out_specs=[pl.BlockSpec((tm, 3 * W), lambda m: (m, 0)), gain, gain],
        out_shape=[jax.ShapeDtypeStruct((S, 3 * W), BF16), jax.ShapeDtypeStruct((1, LANES), F32),
                   jax.ShapeDtypeStruct((1, LANES), F32)],
        compiler_params=_cp("arbitrary"),
    )(qkv, qkv, gq, gk, *tables, *dqs, *dks, *dvs)


ATTN_BQ = 2 * BAND
ATTN_ROWS = 16 * ATTN_BQ
V_COL = 2 * N_HEADS * HEAD_DIM // LANES


def _attn_geometry(S, d):
    rows = min(ATTN_ROWS, S)
    halo = BAND * d
    assert rows % (ATTN_BQ * d) == 0 and S % rows == 0, (S, d)
    return rows, halo, rows // (ATTN_BQ * d)


def _attn_specs(S, d, col):
    rows, halo, _ = _attn_geometry(S, d)
    r = rows // halo
    n_h = S // halo
    prev = pl.BlockSpec((halo, LANES), lambda j, i: (jnp.maximum(i * r - 1, 0), col + j))
    cur = pl.BlockSpec((rows, LANES), lambda j, i: (i, col + j))
    nxt = pl.BlockSpec((halo, LANES), lambda j, i: (jnp.minimum((i + 1) * r, n_h - 1), col + j))
    return [prev, cur, nxt]


def _fill_window(scr, prev, cur, nxt, rows, halo):
    scr[0:halo, :] = prev[...]
    scr[halo:halo + rows, :] = cur[...]
    scr[halo + rows:2 * halo + rows, :] = nxt[...]


def _strided(ref, start, size, d):
    return ref[pl.ds(start, size, stride=d) if d > 1 else pl.ds(start, size), :]


def _band_masks(i, S, d, sb):
    rows, _, _ = _attn_geometry(S, d)
    L = S // d
    base = i * (rows // d) + sb * ATTN_BQ
    wk = ATTN_BQ + 2 * BAND
    row = lax.broadcasted_iota(jnp.int32, (ATTN_BQ, wk), 0)
    col = lax.broadcasted_iota(jnp.int32, (ATTN_BQ, wk), 1)
    lj = base - BAND + col
    valid = (jnp.abs(col - BAND - row) <= BAND) & (lj >= 0) & (lj < L)
    row_t = lax.broadcasted_iota(jnp.int32, (wk, ATTN_BQ), 0)
    col_t = lax.broadcasted_iota(jnp.int32, (wk, ATTN_BQ), 1)
    li = base - BAND + row_t
    valid_t = (jnp.abs(row_t - BAND - col_t) <= BAND) & (li >= 0) & (li < L)
    return valid, valid_t


def _attn_fwd(name, q, k, v, v_col, d):
    S, W = q.shape
    rows, halo, n_sb = _attn_geometry(S, d)
    wk = ATTN_BQ + 2 * BAND
    scale = HEAD_DIM ** -0.5

    def body(q_ref, kp, kc, kn, vp, vc, vn, o_ref, lse_ref, kw, vw):
        i = pl.program_id(1)
        _fill_window(kw, kp, kc, kn, rows, halo)
        _fill_window(vw, vp, vc, vn, rows, halo)
        first = _first_head((ATTN_BQ, LANES))
        zero = jnp.zeros((), BF16)
        for sb in range(n_sb):
            valid, _ = _band_masks(i, S, d, sb)
            for r in range(d):
                start = r + d * sb * ATTN_BQ
                qv = _strided(q_ref, start, ATTN_BQ, d).astype(BF16)
                kv = _strided(kw, start, wk, d).astype(BF16)
                vv = _strided(vw, start, wk, d).astype(BF16)
                o_h, lse_h = [], []
                for hm in (first, jnp.logical_not(first)):
                    s = jnp.where(valid, _dot(jnp.where(hm, qv, zero), kv, 1, 1) * scale, NEG)
                    mx = jnp.max(s, axis=-1, keepdims=True)
                    p = jnp.exp(s - mx)
                    den = jnp.sum(p, axis=-1, keepdims=True)
                    o_h.append(_dot(p.astype(BF16), vv, 1, 0) / den)
                    lse_h.append(mx + jnp.log(den))
                dst = pl.ds(start, ATTN_BQ, stride=d) if d > 1 else pl.ds(start, ATTN_BQ)
                o_ref[dst, :] = jnp.where(first, o_h[0], o_h[1])
                lse_ref[dst, :] = jnp.where(first, lse_h[0], lse_h[1])

    cur = _attn_specs(S, d, 0)[1]
    return pl.pallas_call(
        body, name=name, grid=(W // LANES, S // rows),
        in_specs=[cur] + _attn_specs(S, d, 0) + _attn_specs(S, d, v_col), out_specs=[cur, cur],
        out_shape=[jax.ShapeDtypeStruct((S, W), F32)] * 2,
        scratch_shapes=[pltpu.VMEM((rows + 2 * halo, LANES), F32)] * 2,
        compiler_params=_cp("parallel", "parallel"),
    )(q, k, k, k, v, v, v)


def _attn_merge(os, lses):
    S, W = os[0].shape
    tm = _tile(S, 256)
    n_p = len(os)

    def body(*refs):
        o_refs, l_refs = refs[:n_p], refs[n_p:2 * n_p]
        o_ref, lt_ref = refs[2 * n_p:]
        ls = [r[...] for r in l_refs]
        mx = functools.reduce(jnp.maximum, ls)
        es = [jnp.exp(l - mx) for l in ls]
        den = functools.reduce(lambda a, b: a + b, es)
        acc = es[0] * o_refs[0][...]
        for e, r in zip(es[1:], o_refs[1:]):
            acc = acc + e * r[...]
        o_ref[...] = (acc / den).astype(BF16)
        lt_ref[...] = mx + jnp.log(den)

    row = pl.BlockSpec((tm, W), lambda m: (m, 0))
    return pl.pallas_call(
        body, name="attn_merge", grid=(S // tm,), in_specs=[row] * (2 * n_p), out_specs=[row, row],
        out_shape=[jax.ShapeDtypeStruct((S, W), BF16), jax.ShapeDtypeStruct((S, W), F32)],
        compiler_params=_cp("parallel"),
    )(*os, *lses)


def _attn_delta(do, o):
    S, W = do.shape
    tm = _tile(S, 256)

    def body(do_ref, o_ref, dl_ref):
        seg = _seg_matrix(1.0)
        for blk in range(W // LANES):
            cols = slice(blk * LANES, (blk + 1) * LANES)
            dl_ref[:, cols] = _seg_sum(do_ref[:, cols] * o_ref[:, cols].astype(F32), seg)

    row = pl.BlockSpec((tm, W), lambda m: (m, 0))
    return pl.pallas_call(
        body, name="attn_delta", grid=(S // tm,), in_specs=[row, row], out_specs=row,
        out_shape=jax.ShapeDtypeStruct((S, W), F32), compiler_params=_cp("parallel"),
    )(do, o)


def _attn_bwd_q(name, q, k, v, v_col, do, lse, delta, d):
    S, W = q.shape
    rows, halo, n_sb = _attn_geometry(S, d)
    wk = ATTN_BQ + 2 * BAND
    scale = HEAD_DIM ** -0.5

    def body(q_ref, do_ref, l_ref, dl_ref, kp, kc, kn, vp, vc, vn, dq_ref, kw, vw):
        i = pl.program_id(1)
        _fill_window(kw, kp, kc, kn, rows, halo)
        _fill_window(vw, vp, vc, vn, rows, halo)
        first = _first_head((ATTN_BQ, LANES))
        zero = jnp.zeros((), BF16)
        for sb in range(n_sb):
            valid, _ = _band_masks(i, S, d, sb)
            for r in range(d):
                start = r + d * sb * ATTN_BQ
                qv = _strided(q_ref, start, ATTN_BQ, d).astype(BF16)
                dov = _strided(do_ref, start, ATTN_BQ, d).astype(BF16)
                lv = _strided(l_ref, start, ATTN_BQ, d)
                dlv = _strided(dl_ref, start, ATTN_BQ, d)
                kv = _strided(kw, start, wk, d).astype(BF16)
                vv = _strided(vw, start, wk, d).astype(BF16)
                dq_h = []
                for hh, hm in enumerate((first, jnp.logical_not(first))):
                    lane0 = hh * HEAD_DIM
                    s = jnp.where(valid, _dot(jnp.where(hm, qv, zero), kv, 1, 1) * scale, NEG)
                    p = jnp.exp(s - lv[:, lane0:lane0 + 1])
                    dp = _dot(jnp.where(hm, dov, zero), vv, 1, 1)
                    ds = p * (dp - dlv[:, lane0:lane0 + 1]) * scale
                    dq_h.append(_dot(ds.astype(BF16), kv, 1, 0))
                dst = pl.ds(start, ATTN_BQ, stride=d) if d > 1 else pl.ds(start, ATTN_BQ)
                dq_ref[dst, :] = jnp.where(first, dq_h[0], dq_h[1])

    cur = _attn_specs(S, d, 0)[1]
    return pl.pallas_call(
        body, name=name, grid=(W // LANES, S // rows),
        in_specs=[cur] * 4 + _attn_specs(S, d, 0) + _attn_specs(S, d, v_col), out_specs=cur,
        out_shape=jax.ShapeDtypeStruct((S, W), F32),
        scratch_shapes=[pltpu.VMEM((rows + 2 * halo, LANES), F32)] * 2,
        compiler_params=_cp("parallel", "parallel"),
    )(q, do, lse, delta, k, k, k, v, v, v)


def _attn_bwd_kv(name, q, k, v, v_col, do, lse, delta, d):
    S, W = q.shape
    rows, halo, n_sb = _attn_geometry(S, d)
    wk = ATTN_BQ + 2 * BAND
    scale = HEAD_DIM ** -0.5

    def body(k_ref, v_ref, qp, qc, qn, dop, doc, don, lp, lc, ln, dlp, dlc, dln, dk_ref, dv_ref, qw, dow, lw, dlw):
        i = pl.program_id(1)
        _fill_window(qw, qp, qc, qn, rows, halo)
        _fill_window(dow, dop, doc, don, rows, halo)
        _fill_window(lw, lp, lc, ln, rows, halo)
        _fill_window(dlw, dlp, dlc, dln, rows, halo)
        first = _first_head((ATTN_BQ, LANES))
        first_w = _first_head((wk, LANES))
        zero = jnp.zeros((), BF16)
        for sb in range(n_sb):
            _, valid_t = _band_masks(i, S, d, sb)
            for r in range(d):
                start = r + d * sb * ATTN_BQ
                kv = _strided(k_ref, start, ATTN_BQ, d).astype(BF16)
                vv = _strided(v_ref, start, ATTN_BQ, d).astype(BF16)
                qv = _strided(qw, start, wk, d).astype(BF16)
                dov = _strided(dow, start, wk, d).astype(BF16)
                lv = _strided(lw, start, wk, d)
                dlv = _strided(dlw, start, wk, d)
                dk_h, dv_h = [], []
                for hh, hm_w in enumerate((first_w, jnp.logical_not(first_w))):
                    lane0 = hh * HEAD_DIM
                    st = jnp.where(valid_t, _dot(jnp.where(hm_w, qv, zero), kv, 1, 1) * scale, NEG)
                    pt = jnp.exp(st - lv[:, lane0:lane0 + 1])
                    dv_h.append(_dot(pt.astype(BF16), dov, 0, 0))
                    dpt = _dot(jnp.where(hm_w, dov, zero), vv, 1, 1)
                    dst_ = pt * (dpt - dlv[:, lane0:lane0 + 1]) * scale
                    dk_h.append(_dot(dst_.astype(BF16), qv, 0, 0))
                dst = pl.ds(start, ATTN_BQ, stride=d) if d > 1 else pl.ds(start, ATTN_BQ)
                dk_ref[dst, :] = jnp.where(first, dk_h[0], dk_h[1])
                dv_ref[dst, :] = jnp.where(first, dv_h[0], dv_h[1])

    cur = _attn_specs(S, d, 0)[1]
    win = _attn_specs(S, d, 0)
    return pl.pallas_call(
        body, name=name, grid=(W // LANES, S // rows),
        in_specs=[cur, _attn_specs(S, d, v_col)[1]] + win * 4, out_specs=[cur, cur],
        out_shape=[jax.ShapeDtypeStruct((S, W), F32)] * 2,
        scratch_shapes=[pltpu.VMEM((rows + 2 * halo, LANES), F32)] * 4,
        compiler_params=_cp("parallel", "parallel"),
    )(k, v, q, q, q, do, do, do, lse, lse, lse, delta, delta, delta)


def _place():
    x, y, c = lax.axis_index("x"), lax.axis_index("y"), lax.axis_index("c")
    chips = [(1 - x, y), (x, 1 - y), (1 - x, 1 - y)]
    return x, y, c, chips


HBM = pl.BlockSpec(memory_space=pltpu.HBM)
SEM = pl.BlockSpec(memory_space=pltpu.SEMAPHORE)
DATAFLOW = pltpu.SideEffectType.DATAFLOW_SIDE_EFFECTING


def _exchange_copies(kind, srcs, dsts, send_sems, recv_sems):
    x, y, c, chips = _place()
    mine = 2 * x + y
    cps = []
    for t in range(len(srcs)):
        for k, (px, py) in enumerate(chips):
            src = srcs[t] if kind == "gather" else srcs[t].at[2 * px + py]
            dst = dsts[t].at[mine] if kind == "gather" else dsts[t].at[k]
            cps.append(pltpu.make_async_remote_copy(src_ref=src, dst_ref=dst, send_sem=send_sems.at[3 * t + k],
                                                    recv_sem=recv_sems.at[3 * t + k], device_id=(px, py, c), device_id_type=MESH))
    return cps


def _exchange_start(name, kind, groups):
    sizes = [len(g) for g in groups]
    n, n_g = sum(sizes), len(groups)

    def body(*refs):
        srcs, dsts = refs[:n], refs[n:2 * n]
        sems = refs[2 * n:2 * n + 2 * n_g]
        token = refs[4 * n + 2 * n_g]
        off = 0
        for gi, size in enumerate(sizes):
            for cp in _exchange_copies(kind, srcs[off:off + size], dsts[off:off + size], sems[2 * gi], sems[2 * gi + 1]):
                cp.start()
            off += size
        token[...] = jnp.zeros_like(token)

    arrays = [pltpu.with_memory_space_constraint(a, pltpu.HBM) for a in
              [s for g in groups for s, _ in g] + [d for g in groups for _, d in g]]
    sem_shapes = []
    for size in sizes:
        sem_shapes += [pltpu.SemaphoreType.DMA((3 * size,))] * 2
    outs = pl.pallas_call(
        body, name=name,
        in_specs=[HBM] * (2 * n),
        out_specs=[SEM] * (2 * n_g) + [HBM] * (2 * n) + [pl.BlockSpec(memory_space=pltpu.VMEM)],
        out_shape=sem_shapes + [pltpu.HBM(a.shape, a.dtype) for a in arrays] + [jax.ShapeDtypeStruct((8, LANES), F32)],
        input_output_aliases={t: 2 * n_g + t for t in range(2 * n)},
        compiler_params=pltpu.CompilerParams(has_side_effects=DATAFLOW),
    )(*arrays)
    sems, thru, token = outs[:2 * n_g], outs[2 * n_g:-1], outs[-1]
    handles, off = [], 0
    for gi, size in enumerate(sizes):
        handles.append((sems[2 * gi], sems[2 * gi + 1], thru[off:off + size], thru[n + off:n + off + size]))
        off += size
    return handles, token


def _exchange_wait(name, kind, handle, after):
    send_sems, recv_sems, srcs, dsts = handle
    n = len(srcs)

    def body(*refs):
        for cp in _exchange_copies(kind, refs[:n], refs[n:2 * n], refs[2 * n], refs[2 * n + 1]):
            cp.wait_send()
            cp.wait_recv()

    outs = pl.pallas_call(
        body, name=name,
        in_specs=[HBM] * (2 * n) + [SEM, SEM, ANY], out_specs=[HBM] * (2 * n),
        out_shape=[pltpu.HBM(a.shape, a.dtype) for a in (*srcs, *dsts)],
        input_output_aliases={t: t for t in range(2 * n)},
        compiler_params=pltpu.CompilerParams(has_side_effects=DATAFLOW),
    )(*srcs, *dsts, send_sems, recv_sems, after)
    return outs[n:]


def _prepare_shard(name, w, idx, dtype, mine):
    _, R, C = w.shape
    tr = _row_tile(R)

    def body(mine_ref, w_ref, src_ref, land_ref):
        val = w_ref[...].astype(dtype)
        src_ref[...] = val
        land_ref[...] = val

    return pl.pallas_call(
        body, name=name,
        grid_spec=pltpu.PrefetchScalarGridSpec(
            num_scalar_prefetch=1, grid=(R // tr,),
            in_specs=[pl.BlockSpec((None, tr, C), lambda i, s: (idx, i, 0))],
            out_specs=[pl.BlockSpec((tr, C), lambda i, s: (i, 0)), pl.BlockSpec((None, tr, C), lambda i, s: (s[0], i, 0))]),
        out_shape=[jax.ShapeDtypeStruct((R, C), dtype), jax.ShapeDtypeStruct((N_SHARDS, R, C), dtype)],
        compiler_params=_cp("parallel"),
    )(mine, w)


def _swap_with_sibling(parts):
    n = len(parts)

    def body(*refs):
        ins, outs = refs[:n], refs[n:2 * n]
        send_sems, recv_sems = refs[2 * n:]
        x, y, c, _ = _place()
        cps = [pltpu.make_async_remote_copy(src_ref=ins[t], dst_ref=outs[t], send_sem=send_sems.at[t], recv_sem=recv_sems.at[t],
                                            device_id=(x, y, 1 - c), device_id_type=MESH) for t in range(n)]
        for cp in cps:
            cp.start()
        for cp in cps:
            cp.wait_recv()
        for cp in cps:
            cp.wait_send()

    return pl.pallas_call(
        body, name="swap_partial_grads", in_specs=[ANY] * n, out_specs=[ANY] * n,
        out_shape=[jax.ShapeDtypeStruct(p.shape, p.dtype) for p in parts],
        scratch_shapes=[pltpu.SemaphoreType.DMA((n,)), pltpu.SemaphoreType.DMA((n,))],
    )(*parts)


def _allreduce_small(v):
    rows = v.shape[0]

    def body(v_ref, out_ref, buf, send_sems, recv_sems):
        x, y, c, chips = _place()
        me, sibling = (x, y, c), (x, y, 1 - c)

        def slot(px, py, pc):
            return buf.at[4 * px + 2 * py + pc]

        def copy(k, block, to, src=None):
            return pltpu.make_async_remote_copy(
                src_ref=slot(*block) if src is None else src, dst_ref=slot(*block), send_sem=send_sems.at[k],
                recv_sem=recv_sems.at[k], device_id=to, device_id_type=MESH)

        slot(*me)[...] = v_ref[...]
        first = [copy(0, me, sibling, src=v_ref)] + [copy(1 + j, me, (*chip, c), src=v_ref) for j, chip in enumerate(chips)]
        for cp in first:
            cp.start()
        passed = [copy(4 + j, (*chip, c), sibling) for j, chip in enumerate(chips)]
        for j, chip in enumerate(chips):
            copy(1 + j, (*chip, c), me).wait_recv()
            passed[j].start()
        copy(0, sibling, me).wait_recv()
        for j, chip in enumerate(chips):
            copy(4 + j, (*chip, 1 - c), me).wait_recv()
        for cp in first + passed:
            cp.wait_send()
        acc = buf[0]
        for k in range(1, 8):
            acc = acc + buf[k]
        out_ref[...] = acc

    return pl.pallas_call(
        body, name="allreduce_small_grads",
        in_specs=[pl.BlockSpec(memory_space=pltpu.VMEM)], out_specs=pl.BlockSpec(memory_space=pltpu.VMEM),
        out_shape=jax.ShapeDtypeStruct((rows, LANES), F32),
        scratch_shapes=[pltpu.VMEM((8, rows, LANES), F32), pltpu.SemaphoreType.DMA((7,)), pltpu.SemaphoreType.DMA((7,))],
        compiler_params=pltpu.CompilerParams(vmem_limit_bytes=VMEM_LIMIT_BYTES),
    )(v)


MM_TM = 1024
MM_TM_K = 512


def _rows_merged(w):
    return w.reshape(1, w.shape[0] * w.shape[1], w.shape[2])


def _sq_relu_epilogue(acc):
    r = jnp.maximum(acc, 0.0)
    return acc, r * r


def _add_epilogue(acc, x):
    return (acc + x,)


def _add_norm_epilogue(acc, x, g):
    y = acc + x
    r = lax.rsqrt(jnp.mean(y * y, axis=-1, keepdims=True) + EPS)
    return y, y * r * g


def _norm_bwd_epilogue(dh, x, dres, g):
    r = lax.rsqrt(jnp.mean(x * x, axis=-1, keepdims=True) + EPS)
    xhat = x * r
    dxhat = dh * g
    dx = dres + r * (dxhat - xhat * jnp.mean(dxhat * xhat, axis=-1, keepdims=True))
    return dx, jnp.sum(dh * xhat, axis=0, keepdims=True)


def _sq_relu_grad_epilogue(acc, a):
    return (acc * (2.0 * jnp.maximum(a.astype(F32), 0.0)),)


STAGES = ("mixer", "mlp")


def _stage_tensors(layer, stage):
    i = layer // 2
    if stage == "mlp":
        return [("mlp_w1", layer), ("mlp_w2", layer)]
    return [("ab_w_in", i), ("b_conv_w", i), ("ab_w_out", i)] if layer % 2 == 0 else [("c_w_qkv", i), ("c_w_out", i)]


def _local_step(x, target, p, weights_of, grads_done):
    S, D = x.shape
    depth = p["mix_norm_g"].shape[0]
    n_even = (depth + 1) // 2
    mix_g3 = p["mix_norm_g"].reshape(depth, 1, D)
    mlp_g3 = p["mlp_norm_g"].reshape(depth, 1, D)
    vec3 = lambda t: t.reshape(t.shape[0], 1, t.shape[1])
    spw16 = p["a_spatial_w"].astype(BF16)
    spw16_t = jnp.swapaxes(spw16, 2, 3)
    bias_full = jnp.repeat(jnp.swapaxes(p["a_spatial_b"], 1, 2), HEAD_DIM, axis=2)
    vn_g, vn_b, cn_g, cn_b, cb3 = (vec3(p[k]) for k in ("a_vnorm_g", "a_vnorm_b", "b_norm_g", "b_norm_b", "b_conv_b"))
    tables = _rope_tables(S)
    gq = jnp.tile(p["c_q_norm_g"], (1, 2))
    gk = jnp.tile(p["c_k_norm_g"], (1, 2))

    saved = []
    h = _rms_fwd("mix_norm_0", x, mix_g3, 0)
    for layer in range(depth):
        i = layer // 2
        wl = dict(weights_of(layer, "mixer", x))
        rec = {"x_mix": x, "w": wl, "h_mix": h}
        if layer % 2 == 0:
            (z,) = _mm_ngroup(f"ab_in_{layer}", h, wl["ab_w_in"], nt=False, tm=MM_TM, out_dtypes=[F32])
            gconv = _glu_conv_fwd(f"glu_conv_{layer}", z, wl["b_conv_w"], cb3, i)
            cat = _ab_tail_fwd(f"ab_tail_{layer}", z, gconv, spw16, bias_full, vn_g, vn_b, cn_g, cn_b, i)
            x, h = _mm_kgroup(f"ab_out_{layer}", cat, _rows_merged(wl["ab_w_out"]), nt=False, tm=MM_TM_K,
                              out_dtypes=[F32, BF16], extras=(x,), vecs=[(mlp_g3, layer)], epilogue=_add_norm_epilogue)
            rec.update(z=z, gconv=gconv, cat=cat)
        else:
            (qkv,) = _mm_ngroup(f"c_qkv_{layer}", h, wl["c_w_qkv"], nt=False, tm=MM_TM, out_dtypes=[F32])
            qn, kn = _qk_fwd(f"qk_norm_rope_{layer}", qkv, gq[i:i + 1], gk[i:i + 1], tables)
            os, lses = zip(*[_attn_fwd(f"attn_d{d}_{layer}", qn, kn, qkv, V_COL, d) for d in PATTERN_DILATIONS])
            o, lse = _attn_merge(os, lses)
            x, h = _mm_kgroup(f"c_out_{layer}", o, _rows_merged(wl["c_w_out"]), nt=False, tm=MM_TM_K,
                              out_dtypes=[F32, BF16], extras=(x,), vecs=[(mlp_g3, layer)], epilogue=_add_norm_epilogue)
            rec.update(qkv=qkv, qn=qn, kn=kn, o=o, lse=lse)
        rec["x_mlp"] = x
        wl.update(weights_of(layer, "mlp", x))
        a, hsq = _mm_ngroup(f"mlp_up_{layer}", h, wl["mlp_w1"], nt=False, tm=MM_TM, out_dtypes=[BF16, BF16],
                            epilogue=_sq_relu_epilogue)
        rec.update(h_mlp=h, a=a, hsq=hsq)
        if layer + 1 < depth:
            x, h = _mm_kgroup(f"mlp_down_{layer}", hsq, _rows_merged(wl["mlp_w2"]), nt=False, tm=MM_TM_K,
                              out_dtypes=[F32, BF16], extras=(x,), vecs=[(mix_g3, layer + 1)], epilogue=_add_norm_epilogue)
        else:
            (x,) = _mm_kgroup(f"mlp_down_{layer}", hsq, _rows_merged(wl["mlp_w2"]), nt=False, tm=MM_TM_K, out_dtypes=[F32],
                              extras=(x,), epilogue=_add_epilogue)
        saved.append(rec)

    dx, loss_row = _loss_grad(x, target)

    small = {k: [None] * v.shape[0] for k, v in p.items()}
    token = None
    for layer in reversed(range(depth)):
        i = layer // 2
        rec = saved[layer]
        wl = rec["w"]
        g = {}
        (da,) = _mm_ngroup(f"mlp_down_dgrad_{layer}", dx, wl["mlp_w2"], nt=True, tm=MM_TM, out_dtypes=[BF16],
                           extras=(rec["a"],), epilogue=_sq_relu_grad_epilogue, anchor=token)
        g["mlp_w2"] = _wgrad(f"mlp_down_wgrad_{layer}", rec["hsq"], dx, wl["mlp_w2"].shape, a_group=True, tm=MM_TM)
        g["mlp_w1"] = _wgrad(f"mlp_up_wgrad_{layer}", rec["h_mlp"], da, wl["mlp_w1"].shape, a_group=False, tm=MM_TM)
        dx, small["mlp_norm_g"][layer] = _mm_kgroup(
            f"mlp_up_dgrad_{layer}", da, wl["mlp_w1"], nt=True, tm=MM_TM_K, out_dtypes=[F32], extras=(rec["x_mlp"], dx),
            vecs=[(mlp_g3, layer)], n_sums=1, epilogue=_norm_bwd_epilogue)
        token = grads_done(layer, "mlp", g)
        g = {}
        if layer % 2 == 0:
            w_out = _rows_merged(wl["ab_w_out"])
            (dcat,) = _mm_ngroup(f"ab_out_dgrad_{layer}", dx, w_out, nt=True, tm=MM_TM, out_dtypes=[F32], anchor=token)
            g["ab_w_out"] = [t.reshape(wl["ab_w_out"].shape) for t in
                             _wgrad(f"ab_out_wgrad_{layer}", rec["cat"], dx, w_out.shape, a_group=True, tm=MM_TM)]
            dz, dgconv, dspw, dbias, dvg, dvb, dcg, dcb = _ab_tail_bwd(
                f"ab_tail_bwd_{layer}", rec["z"], rec["gconv"], dcat, spw16, spw16_t, bias_full, vn_g, vn_b, cn_g, cn_b, i)
            dz, gf, gb, dcbias = _glu_conv_bwd(f"glu_conv_bwd_{layer}", rec["z"], dgconv, dz, wl["b_conv_w"])
            g["b_conv_w"] = (gf, gb)
            small["a_spatial_w"][i] = dspw
            small["a_spatial_b"][i] = _fold_bias(dbias)[:, :A_GROUPS].T
            for k, val in (("a_vnorm_g", dvg), ("a_vnorm_b", dvb), ("b_norm_g", dcg), ("b_norm_b", dcb), ("b_conv_b", dcbias)):
                small[k][i] = val
            g["ab_w_in"] = _wgrad(f"ab_in_wgrad_{layer}", rec["h_mix"], dz, wl["ab_w_in"].shape, a_group=False, tm=MM_TM)
            dgrad = (f"ab_in_dgrad_{layer}", dz, wl["ab_w_in"])
        else:
            w_out = _rows_merged(wl["c_w_out"])
            (do,) = _mm_ngroup(f"c_out_dgrad_{layer}", dx, w_out, nt=True, tm=MM_TM, out_dtypes=[F32], anchor=token)
            g["c_w_out"] = [t.reshape(wl["c_w_out"].shape) for t in
                            _wgrad(f"c_out_wgrad_{layer}", rec["o"], dx, w_out.shape, a_group=True, tm=MM_TM)]
            delta = _attn_delta(do, rec["o"])
            attn_args = (rec["qn"], rec["kn"], rec["qkv"], V_COL, do, rec["lse"], delta)
            dqs = [_attn_bwd_q(f"attn_bwd_q_d{d}_{layer}", *attn_args, d) for d in PATTERN_DILATIONS]
            dks, dvs = zip(*[_attn_bwd_kv(f"attn_bwd_kv_d{d}_{layer}", *attn_args, d) for d in PATTERN_DILATIONS])
            dqkv, dgq, dgk = _qk_bwd(f"qk_norm_rope_bwd_{layer}", rec["qkv"], gq[i:i + 1], gk[i:i + 1], tables, dqs, dks, dvs)
            small["c_q_norm_g"][i] = dgq[:, :HEAD_DIM]
            small["c_k_norm_g"][i] = dgk[:, :HEAD_DIM]
            g["c_w_qkv"] = _wgrad(f"c_qkv_wgrad_{layer}", rec["h_mix"], dqkv, wl["c_w_qkv"].shape, a_group=False, tm=MM_TM)
            dgrad = (f"c_qkv_dgrad_{layer}", dqkv, wl["c_w_qkv"])
        dx, small["mix_norm_g"][layer] = _mm_kgroup(
            *dgrad, nt=True, tm=MM_TM_K, out_dtypes=[F32], extras=(rec["x_mix"], dx), vecs=[(mix_g3, layer)], n_sums=1,
            epilogue=_norm_bwd_epilogue)
        token = grads_done(layer, "mixer", g)

    small = {k: jnp.stack([t.reshape(p[k].shape[1:]) for t in v]) for k, v in small.items()}
    return loss_row, dx, small


SHARDED = ("mlp_w1", "mlp_w2", "ab_w_in", "b_conv_w", "ab_w_out", "c_w_qkv", "c_w_out")
SMALL = ("mix_norm_g", "mlp_norm_g", "a_spatial_w", "a_spatial_b", "a_vnorm_g", "a_vnorm_b", "b_conv_b", "b_norm_g",
         "b_norm_b", "c_q_norm_g", "c_k_norm_g")
WEIGHTS = ("mix_norm_g", "mlp_norm_g", "mlp_w1", "mlp_w2", "ab_w_in", "a_spatial_w", "a_spatial_b", "a_vnorm_g",
           "a_vnorm_b", "b_conv_w", "b_conv_b", "b_norm_g", "b_norm_b", "ab_w_out", "c_w_qkv", "c_q_norm_g",
           "c_k_norm_g", "c_w_out")


def _pack(parts):
    flat = jnp.concatenate([parts[k].reshape(-1) for k in SMALL])
    rows = -(-flat.shape[0] // (256 * LANES)) * 256
    return jnp.pad(flat, (0, rows * LANES - flat.shape[0])).reshape(rows, LANES)


def _unpack(packed, like):
    flat = packed.reshape(-1)
    out, off = {}, 0
    for k in SMALL:
        n = like[k].size
        out[k] = flat[off:off + n].reshape(like[k].shape)
        off += n
    return out


def kernel(x, mix_norm_g, mlp_norm_g, mlp_w1, mlp_w2, ab_w_in, a_spatial_w, a_spatial_b, a_vnorm_g, a_vnorm_b, b_conv_w, b_conv_b, b_norm_g, b_norm_b, ab_w_out, c_w_qkv, c_q_norm_g, c_k_norm_g, c_w_out, loss_target, m_mix_norm_g, m_mlp_norm_g, m_mlp_w1, m_mlp_w2, m_ab_w_in, m_a_spatial_w, m_a_spatial_b, m_a_vnorm_g, m_a_vnorm_b, m_b_conv_w, m_b_conv_b, m_b_norm_g, m_b_norm_b, m_ab_w_out, m_c_w_qkv, m_c_q_norm_g, m_c_k_norm_g, m_c_w_out, v_mix_norm_g, v_mlp_norm_g, v_mlp_w1, v_mlp_w2, v_ab_w_in, v_a_spatial_w, v_a_spatial_b, v_a_vnorm_g, v_a_vnorm_b, v_b_conv_w, v_b_conv_b, v_b_norm_g, v_b_norm_b, v_ab_w_out, v_c_w_qkv, v_c_q_norm_g, v_c_k_norm_g, v_c_w_out):
    w = dict(mix_norm_g=mix_norm_g, mlp_norm_g=mlp_norm_g, mlp_w1=mlp_w1, mlp_w2=mlp_w2, ab_w_in=ab_w_in,
             a_spatial_w=a_spatial_w, a_spatial_b=a_spatial_b, a_vnorm_g=a_vnorm_g, a_vnorm_b=a_vnorm_b,
             b_conv_w=b_conv_w, b_conv_b=b_conv_b, b_norm_g=b_norm_g, b_norm_b=b_norm_b, ab_w_out=ab_w_out,
             c_w_qkv=c_w_qkv, c_q_norm_g=c_q_norm_g, c_k_norm_g=c_k_norm_g, c_w_out=c_w_out)
    m = dict(mix_norm_g=m_mix_norm_g, mlp_norm_g=m_mlp_norm_g, mlp_w1=m_mlp_w1, mlp_w2=m_mlp_w2, ab_w_in=m_ab_w_in,
             a_spatial_w=m_a_spatial_w, a_spatial_b=m_a_spatial_b, a_vnorm_g=m_a_vnorm_g, a_vnorm_b=m_a_vnorm_b,
             b_conv_w=m_b_conv_w, b_conv_b=m_b_conv_b, b_norm_g=m_b_norm_g, b_norm_b=m_b_norm_b, ab_w_out=m_ab_w_out,
             c_w_qkv=m_c_w_qkv, c_q_norm_g=m_c_q_norm_g, c_k_norm_g=m_c_k_norm_g, c_w_out=m_c_w_out)
    v = dict(mix_norm_g=v_mix_norm_g, mlp_norm_g=v_mlp_norm_g, mlp_w1=v_mlp_w1, mlp_w2=v_mlp_w2, ab_w_in=v_ab_w_in,
             a_spatial_w=v_a_spatial_w, a_spatial_b=v_a_spatial_b, a_vnorm_g=v_a_vnorm_g, a_vnorm_b=v_a_vnorm_b,
             b_conv_w=v_b_conv_w, b_conv_b=v_b_conv_b, b_norm_g=v_b_norm_g, b_norm_b=v_b_norm_b, ab_w_out=v_ab_w_out,
             c_w_qkv=v_c_w_qkv, c_q_norm_g=v_c_q_norm_g, c_k_norm_g=v_c_k_norm_g, c_w_out=v_c_w_out)

    S, D = x.shape[1], x.shape[2]
    depth = mix_norm_g.shape[0]
    mine = (2 * lax.axis_index("x") + lax.axis_index("y")).astype(jnp.int32).reshape(1)

    stages = [(layer, stage) for layer in range(depth) for stage in STAGES]
    groups = [[(k, i) + tuple(_prepare_shard(f"prepare_{k}_{i}", w[k], i, F32 if k == "b_conv_w" else BF16, mine))
               for k, i in _stage_tensors(*st)] for st in stages]
    handles, gather_token = _exchange_start("gather_weights_start", "gather", [[(s, l) for _, _, s, l in g] for g in groups])
    handles = dict(zip(stages, handles))

    def weights_of(layer, stage, after):
        got = _exchange_wait(f"gather_weights_wait_{layer}_{stage}", "gather", handles[layer, stage],
                             gather_token if (layer, stage) == stages[0] else after)
        return {k: a for (k, _), a in zip(_stage_tensors(layer, stage), got)}

    scattered = {}

    def grads_done(layer, stage, g):
        names = [k for k, _ in _stage_tensors(layer, stage)]
        group = [(g[k][1], lax.empty((3,) + g[k][1].shape[1:], BF16)) for k in names]
        (handle,), token = _exchange_start(f"scatter_grads_start_{layer}_{stage}", "scatter", [group])
        scattered[layer, stage] = (handle, [g[k][0] for k in names])
        return token

    small_params = {k: w[k] for k in SMALL}
    loss_row, dx, small_grads = _local_step(x.reshape(S, D), loss_target.reshape(S, D), small_params, weights_of, grads_done)

    loss = lax.psum(loss_row[0, 0], ("x", "y", "c"))

    partial, order = [], []
    for layer, stage in reversed(stages):
        handle, gfs = scattered[layer, stage]
        recvs = _exchange_wait(f"scatter_grads_wait_{layer}_{stage}", "scatter", handle, dx)
        tensors = _stage_tensors(layer, stage)
        partial += [_sum4(f"sum_chips_{k}_{i}", gf, r, mine) for (k, i), gf, r in zip(tensors, gfs, recvs)]
        order += tensors
    other = _swap_with_sibling(partial)
    stacked = {k: [lax.empty(w[k].shape, F32) for _ in range(4)] for k in SHARDED}
    for (k, i), a, b in zip(order, partial, other):
        stacked[k] = _adamw_layer(f"adamw_{k}_{i}", w[k], m[k], v[k], i, a, b, stacked[k])
    grads, deltas, new_m, new_v = ({k: stacked[k][j] for k in SHARDED} for j in range(4))

    g_small = _allreduce_small(_pack(small_grads))
    outs = _adamw("adamw_small", _pack(small_params), _pack({k: m[k] for k in SMALL}), _pack({k: v[k] for k in SMALL}), g_small)
    for d_, packed in zip((grads, deltas, new_m, new_v), outs):
        d_.update(_unpack(packed, small_params))

    return (loss, dx.reshape(1, S, D), *[grads[k] for k in WEIGHTS], *[deltas[k] for k in WEIGHTS],
            *[new_m[k] for k in WEIGHTS], *[new_v[k] for k in WEIGHTS])
```

```python
import functools

import jax
import jax.numpy as jnp
from jax import lax
from jax.experimental import pallas as pl
from jax.experimental.pallas import tpu as pltpu

F32, BF16 = jnp.float32, jnp.bfloat16
MESH = pl.DeviceIdType.MESH
ANY = pl.BlockSpec(memory_space=pl.ANY)

VMEM_LIMIT_BYTES = 56 * 1024 * 1024
LANES = 128
ELEMENTWISE_ROWS = 256

EPS = 1e-6
NEG = -1e30
HEAD_DIM = 64
N_HEADS = 16
CHUNK = 128
A_GROUPS = 8
CONV_WIDTH = 31
CONV_HALO = 16
BAND = 64
PATTERN_DILATIONS = (1, 4, 16)
ROT_DIM = 16
ROPE_THETA = 500000.0
N_SHARDS = 4

ADAM_LR, ADAM_B1, ADAM_B2, ADAM_EPS, ADAM_WD, ADAM_STEP = 0.001, 0.9, 0.999, 1e-08, 0.01, 10


def _cp(*sem):
    return pltpu.CompilerParams(dimension_semantics=sem, vmem_limit_bytes=VMEM_LIMIT_BYTES)


def _tile(n, pref):
    t = min(n, pref)
    assert n % t == 0, (n, pref)
    return t


def _dot(a, b, ca, cb):
    return lax.dot_general(a, b, (((ca,), (cb,)), ((), ())), preferred_element_type=F32)


def _mm_ngroup(name, a, w, *, nt, tm, out_dtypes, extras=(), epilogue=None, anchor=None):
    M, K = a.shape
    G, R, C = w.shape
    nw = R if nt else C
    assert K == (C if nt else R)
    tm = _tile(M, tm)
    n_ex = len(extras)
    anchors = [] if anchor is None else [anchor]

    def body(a_ref, w_ref, *rest):
        rest = rest[len(anchors):]
        av = a_ref[...].astype(BF16)
        for g in range(G):
            cols = slice(g * nw, (g + 1) * nw)
            acc = _dot(av, w_ref[g], 1, 1 if nt else 0)
            res = epilogue(acc, *[e[:, cols] for e in rest[:n_ex]]) if epilogue else (acc,)
            for o_ref, r in zip(rest[n_ex:], res):
                o_ref[:, cols] = r.astype(o_ref.dtype)

    blk = pl.BlockSpec((tm, G * nw), lambda m: (m, 0))
    return pl.pallas_call(
        body, name=name, grid=(M // tm,),
        in_specs=[pl.BlockSpec((tm, K), lambda m: (m, 0)), pl.BlockSpec((G, R, C), lambda m: (0, 0, 0))]
        + [pl.BlockSpec((8, LANES), lambda m: (0, 0))] * len(anchors) + [blk] * n_ex,
        out_specs=[blk] * len(out_dtypes),
        out_shape=[jax.ShapeDtypeStruct((M, G * nw), dt) for dt in out_dtypes],
        compiler_params=_cp("parallel"),
    )(a, w, *anchors, *extras)


def _mm_kgroup(name, a, w, *, nt, tm, out_dtypes, extras=(), vecs=(), n_sums=0, epilogue=None):
    G, R, C = w.shape
    kw, N = (C, R) if nt else (R, C)
    if a.ndim == 3:
        M = a.shape[1]
        assert a.shape[0] == G and a.shape[2] == kw
    else:
        M = a.shape[0]
        assert a.shape[1] == G * kw
    tm = _tile(M, tm)
    n_ex = len(extras)
    a_spec = (pl.BlockSpec((G, tm, kw), lambda m: (0, m, 0)) if a.ndim == 3 else pl.BlockSpec((tm, G * kw), lambda m: (m, 0)))

    def body(a_ref, w_ref, *rest):
        acc = None
        for g in range(G):
            a_g = a_ref[g] if a.ndim == 3 else a_ref[:, g * kw:(g + 1) * kw]
            part = _dot(a_g.astype(BF16), w_ref[g], 1, 1 if nt else 0)
            acc = part if acc is None else acc + part
        n_in = n_ex + len(vecs)
        res = epilogue(acc, *[e[...] for e in rest[:n_in]]) if epilogue else (acc,)
        outs = rest[n_in:]
        n_tiles = len(outs) - n_sums
        for o_ref, r in zip(outs[:n_tiles], res[:n_tiles]):
            o_ref[...] = r.astype(o_ref.dtype)
        if n_sums:
            @pl.when(pl.program_id(0) == 0)
            def _():
                for s_ref in outs[n_tiles:]:
                    s_ref[...] = jnp.zeros_like(s_ref)

            for s_ref, r in zip(outs[n_tiles:], res[n_tiles:]):
                s_ref[...] += r

    blk = pl.BlockSpec((tm, N), lambda m: (m, 0))
    row = pl.BlockSpec((1, N), lambda m: (0, 0))
    return pl.pallas_call(
        body, name=name, grid=(M // tm,),
        in_specs=[a_spec, pl.BlockSpec((G, R, C), lambda m: (0, 0, 0))] + [blk] * n_ex
        + [pl.BlockSpec((None, 1, N), lambda m, i=i: (i, 0, 0)) for _, i in vecs],
        out_specs=[blk] * len(out_dtypes) + [row] * n_sums,
        out_shape=[jax.ShapeDtypeStruct((M, N), dt) for dt in out_dtypes] + [jax.ShapeDtypeStruct((1, N), F32)] * n_sums,
        compiler_params=_cp("arbitrary" if n_sums else "parallel"),
    )(a, w, *extras, *[v for v, _ in vecs])


def _wgrad(name, a, b, shape, *, a_group, tm):
    G, R, C = shape
    M = a.shape[0]
    tm = _tile(M, tm)
    n_m = M // tm

    def body(a_ref, b_ref, gf_ref, gb_ref):
        m = pl.program_id(1)
        part = _dot(a_ref[...].astype(BF16), b_ref[...].astype(BF16), 0, 0)

        @pl.when(m == 0)
        def _():
            gf_ref[...] = part

        @pl.when(m > 0)
        def _():
            gf_ref[...] += part

        @pl.when(m == n_m - 1)
        def _():
            gb_ref[...] = gf_ref[...].astype(BF16)

    a_spec = pl.BlockSpec((tm, R), (lambda g, m: (m, g)) if a_group else (lambda g, m: (m, 0)))
    if b.ndim == 3:
        assert not a_group
        b_spec = pl.BlockSpec((None, tm, C), lambda g, m: (g, m, 0))
    else:
        b_spec = pl.BlockSpec((tm, C), (lambda g, m: (m, 0)) if a_group else (lambda g, m: (m, g)))
    o_spec = pl.BlockSpec((None, R, C), lambda g, m: (g, 0, 0))
    return pl.pallas_call(
        body, name=name, grid=(G, n_m),
        in_specs=[a_spec, b_spec], out_specs=[o_spec, o_spec],
        out_shape=[jax.ShapeDtypeStruct(shape, F32), jax.ShapeDtypeStruct(shape, BF16)],
        compiler_params=_cp("parallel", "arbitrary"),
    )(a, b)


def _rms_fwd(name, x, g3, layer):
    S, D = x.shape
    tm = _tile(S, 512)

    def body(x_ref, g_ref, h_ref):
        xv = x_ref[...]
        r = lax.rsqrt(jnp.mean(xv * xv, axis=-1, keepdims=True) + EPS)
        h_ref[...] = (xv * r * g_ref[...]).astype(BF16)

    row = pl.BlockSpec((tm, D), lambda m: (m, 0))
    return pl.pallas_call(
        body, name=name, grid=(S // tm,),
        in_specs=[row, pl.BlockSpec((None, 1, D), lambda m: (layer, 0, 0))], out_specs=row,
        out_shape=jax.ShapeDtypeStruct((S, D), BF16), compiler_params=_cp("parallel"),
    )(x, g3)


def _loss_grad(y, target):
    S, D = y.shape
    tm = _tile(S, 512)

    def body(y_ref, t_ref, dy_ref, l_ref):
        e = y_ref[...] - t_ref[...]
        dy_ref[...] = e * (1.0 / D)

        @pl.when(pl.program_id(0) == 0)
        def _():
            l_ref[...] = jnp.zeros_like(l_ref)

        l_ref[...] += (0.5 / D) * jnp.sum(jnp.sum(e * e, axis=1, keepdims=True), axis=0, keepdims=True)

    row = pl.BlockSpec((tm, D), lambda m: (m, 0))
    return pl.pallas_call(
        body, name="loss_grad", grid=(S // tm,), in_specs=[row, row],
        out_specs=[row, pl.BlockSpec((1, LANES), lambda m: (0, 0))],
        out_shape=[jax.ShapeDtypeStruct((S, D), F32), jax.ShapeDtypeStruct((1, LANES), F32)],
        compiler_params=_cp("arbitrary"),
    )(y, target)


def _adamw_math(w, m, v, g):
    m2 = ADAM_B1 * m + (1.0 - ADAM_B1) * g
    v2 = ADAM_B2 * v + (1.0 - ADAM_B2) * jnp.square(g)
    m_hat = m2 / (1.0 - ADAM_B1 ** ADAM_STEP)
    v_hat = v2 / (1.0 - ADAM_B2 ** ADAM_STEP)
    return g, -ADAM_LR * (m_hat / (jnp.sqrt(v_hat) + ADAM_EPS) + ADAM_WD * w), m2, v2


def _row_tile(rows):
    return _tile(rows, ELEMENTWISE_ROWS) if rows % ELEMENTWISE_ROWS == 0 else rows


def _adamw(name, w, m, v, g):
    rows, C = w.shape
    tr = _row_tile(rows)

    def body(w_ref, m_ref, v_ref, g_in, g_ref, d_ref, nm_ref, nv_ref):
        for o_ref, val in zip((g_ref, d_ref, nm_ref, nv_ref), _adamw_math(w_ref[...], m_ref[...], v_ref[...], g_in[...])):
            o_ref[...] = val

    blk = pl.BlockSpec((tr, C), lambda i: (i, 0))
    return pl.pallas_call(
        body, name=name, grid=(rows // tr,), in_specs=[blk] * 4, out_specs=[blk] * 4,
        out_shape=[jax.ShapeDtypeStruct((rows, C), F32)] * 4, compiler_params=_cp("parallel"),
    )(w, m, v, g)


def _adamw_layer(name, w, m, v, layer, mine, theirs, outs):
    _, R, C = w.shape
    tr = _row_tile(R)

    def body(w_ref, m_ref, v_ref, a_ref, b_ref, *rest):
        g = a_ref[...] + b_ref[...]
        for o_ref, val in zip(rest[4:], _adamw_math(w_ref[...], m_ref[...], v_ref[...], g)):
            o_ref[...] = val

    st = pl.BlockSpec((None, tr, C), lambda i: (layer, i, 0))
    part = pl.BlockSpec((tr, C), lambda i: (i, 0))
    return pl.pallas_call(
        body, name=name, grid=(R // tr,), in_specs=[st] * 3 + [part] * 2 + [ANY] * 4, out_specs=[st] * 4,
        out_shape=[jax.ShapeDtypeStruct(w.shape, F32)] * 4, input_output_aliases={5 + j: j for j in range(4)},
        compiler_params=_cp("parallel"),
    )(w, m, v, mine, theirs, *outs)


def _sum4(name, gf, recv, mine):
    _, R, C = gf.shape
    tr = _row_tile(R)

    def body(mine_ref, o_ref, r_ref, out_ref):
        acc = o_ref[...]
        for k in range(3):
            acc = acc + r_ref[k].astype(F32)
        out_ref[...] = acc

    return pl.pallas_call(
        body, name=name,
        grid_spec=pltpu.PrefetchScalarGridSpec(
            num_scalar_prefetch=1, grid=(R // tr,),
            in_specs=[pl.BlockSpec((None, tr, C), lambda i, s: (s[0], i, 0)), pl.BlockSpec((3, tr, C), lambda i, s: (0, i, 0))],
            out_specs=pl.BlockSpec((tr, C), lambda i, s: (i, 0))),
        out_shape=jax.ShapeDtypeStruct((R, C), F32), compiler_params=_cp("parallel"),
    )(mine, gf, recv)


def _gelu(x):
    return x * (0.5 * (1.0 + jnp.tanh(0.7978845608028654 * (x + 0.044715 * (x * x * x)))))


def _layernorm(t, g, b):
    mu = jnp.mean(t, axis=-1, keepdims=True)
    var = jnp.mean(jnp.square(t - mu), axis=-1, keepdims=True)
    return (t - mu) * lax.rsqrt(var + EPS) * g + b


def _silu(x):
    return x * jax.nn.sigmoid(x)


def _a_value(zv, g, b):
    return _layernorm(_gelu(zv), g, b)


def _b_tail(gc, g, b):
    return _silu(_layernorm(gc, g, b))


def _first_head(shape):
    return lax.broadcasted_iota(jnp.int32, shape, len(shape) - 1) < HEAD_DIM


def _spatial_mix(spw_ref, vb, tm):
    first = _first_head((CHUNK, LANES))
    rows = []
    for n in range(tm // CHUNK):
        blocks = []
        for j in range(A_GROUPS // 2):
            vblk = vb[n * CHUNK:(n + 1) * CHUNK, j * LANES:(j + 1) * LANES]
            r0 = _dot(spw_ref[2 * j], vblk, 1, 0)
            r1 = _dot(spw_ref[2 * j + 1], vblk, 1, 0)
            blocks.append(jnp.where(first, r0, r1))
        rows.append(jnp.concatenate(blocks, axis=1))
    return jnp.concatenate(rows, axis=0) if len(rows) > 1 else rows[0]


def _ab_tail_fwd(name, z, gconv, spw, bias_full, vn_g, vn_b, cn_g, cn_b, layer):
    S = z.shape[0]
    AW = 512
    tm = _tile(S, 256)

    def body(zu_ref, zv_ref, gc_ref, spw_ref, bias_ref, vg_ref, vb_ref, cg_ref, cb_ref, cat_ref):
        u = _gelu(zu_ref[...])
        v = _a_value(zv_ref[...], vg_ref[...], vb_ref[...])
        sv = _spatial_mix(spw_ref, v.astype(BF16), tm) + jnp.tile(bias_ref[...], (tm // CHUNK, 1))
        cat_ref[:, :AW] = (u * sv).astype(BF16)
        cat_ref[:, AW:] = _b_tail(gc_ref[...], cg_ref[...], cb_ref[...]).astype(BF16)

    vec = pl.BlockSpec((None, 1, AW), lambda m: (layer, 0, 0))
    return pl.pallas_call(
        body, name=name, grid=(S // tm,),
        in_specs=[pl.BlockSpec((tm, AW), lambda m: (m, 0)), pl.BlockSpec((tm, AW), lambda m: (m, 1)),
                  pl.BlockSpec((tm, AW), lambda m: (m, 0)),
                  pl.BlockSpec((None, A_GROUPS, CHUNK, CHUNK), lambda m: (layer, 0, 0, 0)),
                  pl.BlockSpec((None, CHUNK, AW), lambda m: (layer, 0, 0)), vec, vec, vec, vec],
        out_specs=pl.BlockSpec((tm, 2 * AW), lambda m: (m, 0)),
        out_shape=jax.ShapeDtypeStruct((S, 2 * AW), BF16), compiler_params=_cp("parallel"),
    )(z, z, gconv, spw, bias_full, vn_g, vn_b, cn_g, cn_b)


def _ab_tail_bwd(name, z, gconv, dcat, spw, spw_t, bias_full, vn_g, vn_b, cn_g, cn_b, layer):
    S = z.shape[0]
    AW = 512
    tm = _tile(S, 256)
    n_chunks = tm // CHUNK

    def body(zu_ref, zv_ref, gc_ref, dcat_ref, spw_ref, spwt_ref, bias_ref, vg_ref, vb_ref, cg_ref, cb_ref,
             dz_ref, dgc_ref, dspw_ref, dbias_ref, dvg_ref, dvb_ref, dcg_ref, dcb_ref):
        @pl.when(pl.program_id(0) == 0)
        def _():
            for r in (dspw_ref, dbias_ref, dvg_ref, dvb_ref, dcg_ref, dcb_ref):
                r[...] = jnp.zeros_like(r)

        dya = dcat_ref[:, :AW]
        dyb = dcat_ref[:, AW:]
        u, u_vjp = jax.vjp(_gelu, zu_ref[...])
        v, v_vjp = jax.vjp(_a_value, zv_ref[...], vg_ref[...], vb_ref[...])
        vb16 = v.astype(BF16)
        sv = _spatial_mix(spw_ref, vb16, tm) + jnp.tile(bias_ref[...], (n_chunks, 1))
        (dzu,) = u_vjp(dya * sv)
        dsv = dya * u
        dsv16 = dsv.astype(BF16)
        dv = _spatial_mix(spwt_ref, dsv16, tm)
        dzv, dvg, dvb = v_vjp(dv)
        dz_ref[0] = dzu
        dz_ref[1] = dzv
        dvg_ref[...] += dvg
        dvb_ref[...] += dvb

        first = _first_head((CHUNK, LANES))
        zero = jnp.zeros((), BF16)
        dbias = jnp.zeros((CHUNK, AW), F32)
        for n in range(n_chunks):
            rows = slice(n * CHUNK, (n + 1) * CHUNK)
            dbias = dbias + dsv[rows]
            for j in range(A_GROUPS // 2):
                cols = slice(j * LANES, (j + 1) * LANES)
                dblk, vblk = dsv16[rows, cols], vb16[rows, cols]
                dspw_ref[2 * j] += _dot(jnp.where(first, dblk, zero), vblk, 1, 1)
                dspw_ref[2 * j + 1] += _dot(jnp.where(first, zero, dblk), vblk, 1, 1)
        dbias_ref[...] += dbias

        _, t_vjp = jax.vjp(_b_tail, gc_ref[...], cg_ref[...], cb_ref[...])
        dgc, dcg, dcb = t_vjp(dyb)
        dgc_ref[...] = dgc
        dcg_ref[...] += dcg
        dcb_ref[...] += dcb

    vec = pl.BlockSpec((None, 1, AW), lambda m: (layer, 0, 0))
    spw_spec = pl.BlockSpec((None, A_GROUPS, CHUNK, CHUNK), lambda m: (layer, 0, 0, 0))
    ovec = pl.BlockSpec((1, AW), lambda m: (0, 0))
    return pl.pallas_call(
        body, name=name, grid=(S // tm,),
        in_specs=[pl.BlockSpec((tm, AW), lambda m: (m, 0)), pl.BlockSpec((tm, AW), lambda m: (m, 1)),
                  pl.BlockSpec((tm, AW), lambda m: (m, 0)), pl.BlockSpec((tm, 2 * AW), lambda m: (m, 0)),
                  spw_spec, spw_spec, pl.BlockSpec((None, CHUNK, AW), lambda m: (layer, 0, 0)), vec, vec, vec, vec],
        out_specs=[pl.BlockSpec((2, tm, AW), lambda m: (0, m, 0)), pl.BlockSpec((tm, AW), lambda m: (m, 0)),
                   pl.BlockSpec((A_GROUPS, CHUNK, CHUNK), lambda m: (0, 0, 0)),
                   pl.BlockSpec((CHUNK, AW), lambda m: (0, 0)), ovec, ovec, ovec, ovec],
        out_shape=[jax.ShapeDtypeStruct((4, S, AW), F32), jax.ShapeDtypeStruct((S, AW), F32),
                   jax.ShapeDtypeStruct((A_GROUPS, CHUNK, CHUNK), F32), jax.ShapeDtypeStruct((CHUNK, AW), F32)]
                  + [jax.ShapeDtypeStruct((1, AW), F32)] * 4,
        compiler_params=_cp("arbitrary"),
    )(z, z, gconv, dcat, spw, spw_t, bias_full, vn_g, vn_b, cn_g, cn_b)


def _fold_bias(dbias_full):
    def body(d_ref, o_ref):
        d = d_ref[...]
        hi = d.astype(BF16)
        lo = (d - hi.astype(F32)).astype(BF16)
        r = lax.broadcasted_iota(jnp.int32, (512, LANES), 0)
        c = lax.broadcasted_iota(jnp.int32, (512, LANES), 1)
        fold = jnp.where(lax.shift_right_logical(r, 6) == c, 1.0, 0.0).astype(BF16)
        o_ref[...] = _dot(hi, fold, 1, 0) + _dot(lo, fold, 1, 0)

    return pl.pallas_call(body, name="fold_spatial_bias", out_shape=jax.ShapeDtypeStruct((CHUNK, LANES), F32))(dbias_full)


def _halo_specs(tm, n_halo_blocks, col):
    r = tm // CONV_HALO
    prev = pl.BlockSpec((CONV_HALO, LANES), lambda j, i: (jnp.maximum(i * r - 1, 0), col + j))
    cur = pl.BlockSpec((tm, LANES), lambda j, i: (i, col + j))
    nxt = pl.BlockSpec((CONV_HALO, LANES), lambda j, i: (jnp.minimum((i + 1) * r, n_halo_blocks - 1), col + j))
    return [prev, cur, nxt]


def _fill_halo(scr, prev, cur, nxt, tm, i, n_i):
    scr[0:CONV_HALO, :] = jnp.where(i > 0, prev, 0.0)
    scr[CONV_HALO:CONV_HALO + tm, :] = cur
    scr[CONV_HALO + tm:2 * CONV_HALO + tm, :] = jnp.where(i < n_i - 1, nxt, 0.0)


def _glu_conv_fwd(name, z, cw, cb3, layer):
    S = z.shape[0]
    tm = _tile(S, 512)
    n_i = S // tm
    pad = CONV_WIDTH // 2

    def body(vp, vc, vn, gp, gc, gn, w_ref, b_ref, out_ref, scr):
        i = pl.program_id(1)
        glu = lambda a, b: a[...] * jax.nn.sigmoid(b[...])
        _fill_halo(scr, glu(vp, gp), glu(vc, gc), glu(vn, gn), tm, i, n_i)
        acc = jnp.zeros((tm, LANES), F32)
        for j in range(CONV_WIDTH):
            acc = acc + w_ref[j:j + 1, :] * scr[pl.ds(CONV_HALO - pad + j, tm), :]
        out_ref[...] = acc + b_ref[...]

    return pl.pallas_call(
        body, name=name, grid=(4, n_i),
        in_specs=_halo_specs(tm, S // CONV_HALO, 8) + _halo_specs(tm, S // CONV_HALO, 12)
        + [pl.BlockSpec((None, CONV_WIDTH, LANES), lambda j, i: (j, 0, 0)),
           pl.BlockSpec((None, 1, LANES), lambda j, i: (layer, 0, j))],
        out_specs=pl.BlockSpec((tm, LANES), lambda j, i: (i, j)),
        out_shape=jax.ShapeDtypeStruct((S, 4 * LANES), F32),
        scratch_shapes=[pltpu.VMEM((tm + 2 * CONV_HALO, LANES), F32)],
        compiler_params=_cp("parallel", "parallel"),
    )(z, z, z, z, z, z, cw, cb3)


def _glu_conv_bwd(name, z, dgconv, dz, cw):
    S = z.shape[0]
    tm = _tile(S, 512)
    n_i = S // tm
    pad = CONV_WIDTH // 2

    def body(vp, vc, vn, gp, gc, gn, dp, dc, dn, w_ref, dz_in, dz_ref, gf_ref, gb_ref, db_ref, g_scr, d_scr):
        i = pl.program_id(1)
        sig = jax.nn.sigmoid(gc[...])
        _fill_halo(g_scr, vp[...] * jax.nn.sigmoid(gp[...]), vc[...] * sig, vn[...] * jax.nn.sigmoid(gn[...]), tm, i, n_i)
        _fill_halo(d_scr, dp[...], dc[...], dn[...], tm, i, n_i)

        @pl.when(i == 0)
        def _():
            gf_ref[...] = jnp.zeros_like(gf_ref)
            db_ref[...] = jnp.zeros_like(db_ref)

        d_cur = dc[...]
        dglu = jnp.zeros((tm, LANES), F32)
        for j in range(CONV_WIDTH):
            dglu = dglu + w_ref[j:j + 1, :] * d_scr[pl.ds(CONV_HALO + pad - j, tm), :]
            gf_ref[j:j + 1, :] += jnp.sum(d_cur * g_scr[pl.ds(CONV_HALO - pad + j, tm), :], axis=0, keepdims=True)
        db_ref[...] += jnp.sum(d_cur, axis=0, keepdims=True)
        dz_ref[0] = dglu * sig
        dz_ref[1] = dglu * vc[...] * sig * (1.0 - sig)

        @pl.when(i == n_i - 1)
        def _():
            gb_ref[...] = gf_ref[...].astype(BF16)

    w_spec = pl.BlockSpec((None, CONV_WIDTH, LANES), lambda j, i: (j, 0, 0))
    return pl.pallas_call(
        body, name=name, grid=(4, n_i),
        in_specs=_halo_specs(tm, S // CONV_HALO, 8) + _halo_specs(tm, S // CONV_HALO, 12)
        + _halo_specs(tm, S // CONV_HALO, 0) + [w_spec, ANY],
        out_specs=[pl.BlockSpec((2, tm, LANES), lambda j, i: (1, i, j)),
                   w_spec, w_spec, pl.BlockSpec((1, LANES), lambda j, i: (0, j))],
        out_shape=[jax.ShapeDtypeStruct(dz.shape, F32), jax.ShapeDtypeStruct(cw.shape, F32),
                   jax.ShapeDtypeStruct(cw.shape, BF16), jax.ShapeDtypeStruct((1, 4 * LANES), F32)],
        input_output_aliases={10: 0},
        scratch_shapes=[pltpu.VMEM((tm + 2 * CONV_HALO, LANES), F32)] * 2,
        compiler_params=_cp("parallel", "arbitrary"),
    )(z, z, z, z, z, z, dgconv, dgconv, dgconv, cw, dz)


def _seg_matrix(scale):
    r = lax.broadcasted_iota(jnp.int32, (LANES, LANES), 0)
    c = lax.broadcasted_iota(jnp.int32, (LANES, LANES), 1)
    return jnp.where(lax.shift_right_logical(r, 6) == lax.shift_right_logical(c, 6), scale, 0.0).astype(BF16)


def _seg_sum(x, seg):
    hi = x.astype(BF16)
    lo = (x - hi.astype(F32)).astype(BF16)
    return _dot(hi, seg, 1, 0) + _dot(lo, seg, 1, 0)


def _rope_tables(S):
    pos = jnp.arange(S, dtype=F32)
    inv_freq = ROPE_THETA ** (-jnp.arange(0, ROT_DIM, 2, dtype=F32) / ROT_DIM)
    ang = pos[:, None] * inv_freq[None, :]
    cos, sin = jnp.cos(ang), jnp.sin(ang)
    half = ROT_DIM // 2
    rest = HEAD_DIM - ROT_DIM
    one, zero = jnp.ones((S, rest), F32), jnp.zeros((S, rest), F32)
    zh = jnp.zeros((S, half), F32)
    c = jnp.concatenate([cos, cos, one], axis=1)
    sa = jnp.concatenate([-sin, zh, zero], axis=1)
    sb = jnp.concatenate([zh, sin, zero], axis=1)
    return [jnp.tile(t, (1, 2)) for t in (c, sa, sb)]


def _qk_fwd(name, qkv, gq, gk, tables):
    S = qkv.shape[0]
    W = N_HEADS * HEAD_DIM
    tm = _tile(S, 256)
    half = ROT_DIM // 2

    def body(q_ref, k_ref, gq_ref, gk_ref, c_ref, sa_ref, sb_ref, qn_ref, kn_ref):
        seg = _seg_matrix(1.0 / HEAD_DIM)
        c, sa, sb = c_ref[...], sa_ref[...], sb_ref[...]
        for t_ref, g_ref, o_ref in ((q_ref, gq_ref, qn_ref), (k_ref, gk_ref, kn_ref)):
            for blk in range(W // LANES):
                cols = slice(blk * LANES, (blk + 1) * LANES)
                t = t_ref[:, cols]
                y = t * lax.rsqrt(_seg_sum(t * t, seg) + EPS) * g_ref[...]
                o_ref[:, cols] = y * c + pltpu.roll(y, LANES - half, 1) * sa + pltpu.roll(y, half, 1) * sb

    row = lambda k: pl.BlockSpec((tm, W), lambda m: (m, k))
    gain = pl.BlockSpec((1, LANES), lambda m: (0, 0))
    tab = pl.BlockSpec((tm, LANES), lambda m: (m, 0))
    return pl.pallas_call(
        body, name=name, grid=(S // tm,),
        in_specs=[row(0), row(1), gain, gain, tab, tab, tab], out_specs=[row(0)] * 2,
        out_shape=[jax.ShapeDtypeStruct((S, W), F32)] * 2, compiler_params=_cp("parallel"),
    )(qkv, qkv, gq, gk, *tables)


def _qk_bwd(name, qkv, gq, gk, tables, dqs, dks, dvs):
    S = qkv.shape[0]
    W = N_HEADS * HEAD_DIM
    tm = _tile(S, 256)
    half = ROT_DIM // 2
    n_p = len(dqs)

    def body(q_ref, k_ref, gq_ref, gk_ref, c_ref, sa_ref, sb_ref, *rest):
        dq_refs, dk_refs, dv_refs = rest[:n_p], rest[n_p:2 * n_p], rest[2 * n_p:3 * n_p]
        dqkv_ref, dgq_ref, dgk_ref = rest[3 * n_p:]

        @pl.when(pl.program_id(0) == 0)
        def _():
            dgq_ref[...] = jnp.zeros_like(dgq_ref)
            dgk_ref[...] = jnp.zeros_like(dgk_ref)

        seg = _seg_matrix(1.0 / HEAD_DIM)
        r_i = lax.broadcasted_iota(jnp.int32, (LANES, LANES), 0)
        c_i = lax.broadcasted_iota(jnp.int32, (LANES, LANES), 1)
        same_dim = jnp.where((r_i & (HEAD_DIM - 1)) == (c_i & (HEAD_DIM - 1)), 1.0, 0.0).astype(BF16)
        c, sa, sb = c_ref[...], sa_ref[...], sb_ref[...]
        for idx, (t_ref, g_ref, d_refs, dg_ref) in enumerate(((q_ref, gq_ref, dq_refs, dgq_ref),
                                                              (k_ref, gk_ref, dk_refs, dgk_ref))):
            dg = jnp.zeros((1, LANES), F32)
            for blk in range(W // LANES):
                cols = slice(blk * LANES, (blk + 1) * LANES)
                dout = d_refs[0][:, cols]
                for r in d_refs[1:]:
                    dout = dout + r[:, cols]
                dy = dout * c + pltpu.roll(dout * sa, half, 1) + pltpu.roll(dout * sb, LANES - half, 1)
                t = t_ref[:, cols]
                r_ = lax.rsqrt(_seg_sum(t * t, seg) + EPS)
                xhat = t * r_
                dg = dg + jnp.sum(dy * xhat, axis=0, keepdims=True)
                dxhat = dy * g_ref[...]
                dt = r_ * (dxhat - xhat * _seg_sum(dxhat * xhat, seg))
                dqkv_ref[:, idx * W + blk * LANES: idx * W + (blk + 1) * LANES] = dt.astype(BF16)
            dg_ref[...] += _seg_sum(jnp.broadcast_to(dg, (8, LANES)), same_dim)[0:1]
        dv = dv_refs[0][...]
        for r in dv_refs[1:]:
            dv = dv + r[...]
        dqkv_ref[:, 2 * W:] = dv.astype(BF16)

    row = lambda k: pl.BlockSpec((tm, W), lambda m: (m, k))
    gain = pl.BlockSpec((1, LANES), lambda m: (0, 0))
    tab = pl.BlockSpec((tm, LANES), lambda m: (m, 0))
    return pl.pallas_call(
        body, name=name, grid=(S // tm,),
        in_specs=[row(0), row(1), gain, gain, tab, tab, tab] + [row(0)] * (3 * n_p),
        out_specs=[pl.BlockSpec((tm, 3 * W), lambda m: (m, 0)), gain, gain],
        out_shape=[jax.ShapeDtypeStruct((S, 3 * W), BF16), jax.ShapeDtypeStruct((1, LANES), F32),
                   jax.ShapeDtypeStruct((1, LANES), F32)],
        compiler_params=_cp("arbitrary"),
    )(qkv, qkv, gq, gk, *tables, *dqs, *dks, *dvs)


ATTN_BQ = 2 * BAND
ATTN_ROWS = 16 * ATTN_BQ
V_COL = 2 * N_HEADS * HEAD_DIM // LANES


def _attn_geometry(S, d):
    rows = min(ATTN_ROWS, S)
    halo = BAND * d
    assert rows % (ATTN_BQ * d) == 0 and S % rows == 0, (S, d)
    return rows, halo, rows // (ATTN_BQ * d)


def _attn_specs(S, d, col):
    rows, halo, _ = _attn_geometry(S, d)
    r = rows // halo
    n_h = S // halo
    prev = pl.BlockSpec((halo, LANES), lambda j, i: (jnp.maximum(i * r - 1, 0), col + j))
    cur = pl.BlockSpec((rows, LANES), lambda j, i: (i, col + j))
    nxt = pl.BlockSpec((halo, LANES), lambda j, i: (jnp.minimum((i + 1) * r, n_h - 1), col + j))
    return [prev, cur, nxt]


def _fill_window(scr, prev, cur, nxt, rows, halo):
    scr[0:halo, :] = prev[...]
    scr[halo:halo + rows, :] = cur[...]
    scr[halo + rows:2 * halo + rows, :] = nxt[...]


def _strided(ref, start, size, d):
    return ref[pl.ds(start, size, stride=d) if d > 1 else pl.ds(start, size), :]


def _band_mask(i, S, d, sb):
    rows, _, _ = _attn_geometry(S, d)
    L = S // d
    base = i * (rows // d) + sb * ATTN_BQ
    wk = ATTN_BQ + 2 * BAND
    row = lax.broadcasted_iota(jnp.int32, (ATTN_BQ, wk), 0)
    col = lax.broadcasted_iota(jnp.int32, (ATTN_BQ, wk), 1)
    lj = base - BAND + col
    return (jnp.abs(col - BAND - row) <= BAND) & (lj >= 0) & (lj < L)


def _attn_fwd(name, q, k, v, v_col, d):
    S, W = q.shape
    rows, halo, n_sb = _attn_geometry(S, d)
    wk = ATTN_BQ + 2 * BAND
    scale = HEAD_DIM ** -0.5

    def body(q_ref, kp, kc, kn, vp, vc, vn, o_ref, lse_ref, kw, vw):
        i = pl.program_id(1)
        _fill_window(kw, kp, kc, kn, rows, halo)
        _fill_window(vw, vp, vc, vn, rows, halo)
        first = _first_head((ATTN_BQ, LANES))
        zero = jnp.zeros((), BF16)
        for sb in range(n_sb):
            valid = _band_mask(i, S, d, sb)
            for r in range(d):
                start = r + d * sb * ATTN_BQ
                qv = _strided(q_ref, start, ATTN_BQ, d).astype(BF16)
                kv = _strided(kw, start, wk, d).astype(BF16)
                vv = _strided(vw, start, wk, d).astype(BF16)
                o_h, lse_h = [], []
                for hm in (first, jnp.logical_not(first)):
                    s = jnp.where(valid, _dot(jnp.where(hm, qv, zero), kv, 1, 1) * scale, NEG)
                    mx = jnp.max(s, axis=-1, keepdims=True)
                    p = jnp.exp(s - mx)
                    den = jnp.sum(p, axis=-1, keepdims=True)
                    o_h.append(_dot(p.astype(BF16), vv, 1, 0) / den)
                    lse_h.append(mx + jnp.log(den))
                dst = pl.ds(start, ATTN_BQ, stride=d) if d > 1 else pl.ds(start, ATTN_BQ)
                o_ref[dst, :] = jnp.where(first, o_h[0], o_h[1])
                lse_ref[dst, :] = jnp.where(first, lse_h[0], lse_h[1])

    cur = _attn_specs(S, d, 0)[1]
    return pl.pallas_call(
        body, name=name, grid=(W // LANES, S // rows),
        in_specs=[cur] + _attn_specs(S, d, 0) + _attn_specs(S, d, v_col), out_specs=[cur, cur],
        out_shape=[jax.ShapeDtypeStruct((S, W), F32)] * 2,
        scratch_shapes=[pltpu.VMEM((rows + 2 * halo, LANES), F32)] * 2,
        compiler_params=_cp("parallel", "parallel"),
    )(q, k, k, k, v, v, v)


def _attn_merge(os, lses):
    S, W = os[0].shape
    tm = _tile(S, 256)
    n_p = len(os)

    def body(*refs):
        o_refs, l_refs = refs[:n_p], refs[n_p:2 * n_p]
        o_ref, lt_ref = refs[2 * n_p:]
        ls = [r[...] for r in l_refs]
        mx = functools.reduce(jnp.maximum, ls)
        es = [jnp.exp(l - mx) for l in ls]
        den = functools.reduce(lambda a, b: a + b, es)
        acc = es[0] * o_refs[0][...]
        for e, r in zip(es[1:], o_refs[1:]):
            acc = acc + e * r[...]
        o_ref[...] = (acc / den).astype(BF16)
        lt_ref[...] = mx + jnp.log(den)

    row = pl.BlockSpec((tm, W), lambda m: (m, 0))
    return pl.pallas_call(
        body, name="attn_merge", grid=(S // tm,), in_specs=[row] * (2 * n_p), out_specs=[row, row],
        out_shape=[jax.ShapeDtypeStruct((S, W), BF16), jax.ShapeDtypeStruct((S, W), F32)],
        compiler_params=_cp("parallel"),
    )(*os, *lses)


def _attn_delta(do, o):
    S, W = do.shape
    tm = _tile(S, 256)

    def body(do_ref, o_ref, dl_ref):
        seg = _seg_matrix(1.0)
        for blk in range(W // LANES):
            cols = slice(blk * LANES, (blk + 1) * LANES)
            dl_ref[:, cols] = _seg_sum(do_ref[:, cols] * o_ref[:, cols].astype(F32), seg)

    row = pl.BlockSpec((tm, W), lambda m: (m, 0))
    return pl.pallas_call(
        body, name="attn_delta", grid=(S // tm,), in_specs=[row, row], out_specs=row,
        out_shape=jax.ShapeDtypeStruct((S, W), F32), compiler_params=_cp("parallel"),
    )(do, o)


def _attn_bwd_q(name, q, k, v, v_col, do, lse, delta, d):
    S, W = q.shape
    rows, halo, n_sb = _attn_geometry(S, d)
    wk = ATTN_BQ + 2 * BAND
    scale = HEAD_DIM ** -0.5

    def body(q_ref, do_ref, l_ref, dl_ref, kp, kc, kn, vp, vc, vn, dq_ref, kw, vw):
        i = pl.program_id(1)
        _fill_window(kw, kp, kc, kn, rows, halo)
        _fill_window(vw, vp, vc, vn, rows, halo)
        first = _first_head((ATTN_BQ, LANES))
        zero = jnp.zeros((), BF16)
        for sb in range(n_sb):
            valid = _band_mask(i, S, d, sb)
            for r in range(d):
                start = r + d * sb * ATTN_BQ
                qv = _strided(q_ref, start, ATTN_BQ, d).astype(BF16)
                dov = _strided(do_ref, start, ATTN_BQ, d).astype(BF16)
                lv = _strided(l_ref, start, ATTN_BQ, d)
                dlv = _strided(dl_ref, start, ATTN_BQ, d)
                kv = _strided(kw, start, wk, d).astype(BF16)
                vv = _strided(vw, start, wk, d).astype(BF16)
                dq_h = []
                for hh, hm in enumerate((first, jnp.logical_not(first))):
                    lane0 = hh * HEAD_DIM
                    s = jnp.where(valid, _dot(jnp.where(hm, qv, zero), kv, 1, 1) * scale, NEG)
                    p = jnp.exp(s - lv[:, lane0:lane0 + 1])
                    dp = _dot(jnp.where(hm, dov, zero), vv, 1, 1)
                    ds = p * (dp - dlv[:, lane0:lane0 + 1]) * scale
                    dq_h.append(_dot(ds.astype(BF16), kv, 1, 0))
                dst = pl.ds(start, ATTN_BQ, stride=d) if d > 1 else pl.ds(start, ATTN_BQ)
                dq_ref[dst, :] = jnp.where(first, dq_h[0], dq_h[1])

    cur = _attn_specs(S, d, 0)[1]
    return pl.pallas_call(
        body, name=name, grid=(W // LANES, S // rows),
        in_specs=[cur] * 4 + _attn_specs(S, d, 0) + _attn_specs(S, d, v_col), out_specs=cur,
        out_shape=jax.ShapeDtypeStruct((S, W), F32),
        scratch_shapes=[pltpu.VMEM((rows + 2 * halo, LANES), F32)] * 2,
        compiler_params=_cp("parallel", "parallel"),
    )(q, do, lse, delta, k, k, k, v, v, v)


def _split3(x):
    hi = x.astype(BF16)
    r1 = x - hi.astype(F32)
    mid = r1.astype(BF16)
    return hi, mid, (r1 - mid.astype(F32)).astype(BF16)


def _lane_to_row(parts, lane0):
    sel = jnp.where(lax.broadcasted_iota(jnp.int32, (8, LANES), 1) == lane0, 1.0, 0.0).astype(BF16)
    out = _dot(sel, parts[0], 1, 1) + _dot(sel, parts[1], 1, 1) + _dot(sel, parts[2], 1, 1)
    return out[0:1, :]


def _attn_bwd_kv(name, q, k, v, v_col, do, lse, delta, d):
    S, W = q.shape
    rows, halo, n_sb = _attn_geometry(S, d)
    wk = ATTN_BQ + 2 * BAND
    scale = HEAD_DIM ** -0.5

    def body(k_ref, v_ref, qp, qc, qn, dop, doc, don, lp, lc, ln, dlp, dlc, dln, dk_ref, dv_ref, qw, dow, lw, dlw):
        i = pl.program_id(1)
        _fill_window(qw, qp, qc, qn, rows, halo)
        _fill_window(dow, dop, doc, don, rows, halo)
        _fill_window(lw, lp, lc, ln, rows, halo)
        _fill_window(dlw, dlp, dlc, dln, rows, halo)
        first = _first_head((ATTN_BQ, LANES))
        zero = jnp.zeros((), BF16)
        for sb in range(n_sb):
            valid = _band_mask(i, S, d, sb)
            for r in range(d):
                start = r + d * sb * ATTN_BQ
                kv = _strided(k_ref, start, ATTN_BQ, d).astype(BF16)
                vv = _strided(v_ref, start, ATTN_BQ, d).astype(BF16)
                qv = _strided(qw, start, wk, d).astype(BF16)
                dov = _strided(dow, start, wk, d).astype(BF16)
                l_parts = _split3(_strided(lw, start, wk, d))
                dl_parts = _split3(_strided(dlw, start, wk, d))
                dk_h, dv_h = [], []
                for hh, hm in enumerate((first, jnp.logical_not(first))):
                    lane0 = hh * HEAD_DIM
                    s = jnp.where(valid, _dot(jnp.where(hm, kv, zero), qv, 1, 1) * scale, NEG)
                    p = jnp.exp(s - _lane_to_row(l_parts, lane0))
                    dv_h.append(_dot(p.astype(BF16), dov, 1, 0))
                    dp = _dot(jnp.where(hm, vv, zero), dov, 1, 1)
                    ds = p * (dp - _lane_to_row(dl_parts, lane0)) * scale
                    dk_h.append(_dot(ds.astype(BF16), qv, 1, 0))
                dst = pl.ds(start, ATTN_BQ, stride=d) if d > 1 else pl.ds(start, ATTN_BQ)
                dk_ref[dst, :] = jnp.where(first, dk_h[0], dk_h[1])
                dv_ref[dst, :] = jnp.where(first, dv_h[0], dv_h[1])

    cur = _attn_specs(S, d, 0)[1]
    win = _attn_specs(S, d, 0)
    return pl.pallas_call(
        body, name=name, grid=(W // LANES, S // rows),
        in_specs=[cur, _attn_specs(S, d, v_col)[1]] + win * 4, out_specs=[cur, cur],
        out_shape=[jax.ShapeDtypeStruct((S, W), F32)] * 2,
        scratch_shapes=[pltpu.VMEM((rows + 2 * halo, LANES), F32)] * 4,
        compiler_params=_cp("parallel", "parallel"),
    )(k, v, q, q, q, do, do, do, lse, lse, lse, delta, delta, delta)


def _place():
    x, y, c = lax.axis_index("x"), lax.axis_index("y"), lax.axis_index("c")
    chips = [(1 - x, y), (x, 1 - y), (1 - x, 1 - y)]
    return x, y, c, chips


HBM = pl.BlockSpec(memory_space=pltpu.HBM)
SEM = pl.BlockSpec(memory_space=pltpu.SEMAPHORE)
DATAFLOW = pltpu.SideEffectType.DATAFLOW_SIDE_EFFECTING


def _exchange_copies(kind, srcs, dsts, send_sems, recv_sems):
    x, y, c, chips = _place()
    mine = 2 * x + y
    cps = []
    for t in range(len(srcs)):
        for k, (px, py) in enumerate(chips):
            src = srcs[t] if kind == "gather" else srcs[t].at[2 * px + py]
            dst = dsts[t].at[mine] if kind == "gather" else dsts[t].at[k]
            cps.append(pltpu.make_async_remote_copy(src_ref=src, dst_ref=dst, send_sem=send_sems.at[3 * t + k],
                                                    recv_sem=recv_sems.at[3 * t + k], device_id=(px, py, c), device_id_type=MESH))
    return cps


def _exchange_start(name, kind, groups):
    sizes = [len(g) for g in groups]
    n, n_g = sum(sizes), len(groups)

    def body(*refs):
        srcs, dsts = refs[:n], refs[n:2 * n]
        sems = refs[2 * n:2 * n + 2 * n_g]
        token = refs[4 * n + 2 * n_g]
        off = 0
        for gi, size in enumerate(sizes):
            for cp in _exchange_copies(kind, srcs[off:off + size], dsts[off:off + size], sems[2 * gi], sems[2 * gi + 1]):
                cp.start()
            off += size
        token[...] = jnp.zeros_like(token)

    arrays = [pltpu.with_memory_space_constraint(a, pltpu.HBM) for a in
              [s for g in groups for s, _ in g] + [d for g in groups for _, d in g]]
    sem_shapes = []
    for size in sizes:
        sem_shapes += [pltpu.SemaphoreType.DMA((3 * size,))] * 2
    outs = pl.pallas_call(
        body, name=name,
        in_specs=[HBM] * (2 * n),
        out_specs=[SEM] * (2 * n_g) + [HBM] * (2 * n) + [pl.BlockSpec(memory_space=pltpu.VMEM)],
        out_shape=sem_shapes + [pltpu.HBM(a.shape, a.dtype) for a in arrays] + [jax.ShapeDtypeStruct((8, LANES), F32)],
        input_output_aliases={t: 2 * n_g + t for t in range(2 * n)},
        compiler_params=pltpu.CompilerParams(has_side_effects=DATAFLOW),
    )(*arrays)
    sems, thru, token = outs[:2 * n_g], outs[2 * n_g:-1], outs[-1]
    handles, off = [], 0
    for gi, size in enumerate(sizes):
        handles.append((sems[2 * gi], sems[2 * gi + 1], thru[off:off + size], thru[n + off:n + off + size]))
        off += size
    return handles, token


def _exchange_wait(name, kind, handle, after):
    send_sems, recv_sems, srcs, dsts = handle
    n = len(srcs)

    def body(*refs):
        for cp in _exchange_copies(kind, refs[:n], refs[n:2 * n], refs[2 * n], refs[2 * n + 1]):
            cp.wait_send()
            cp.wait_recv()

    outs = pl.pallas_call(
        body, name=name,
        in_specs=[HBM] * (2 * n) + [SEM, SEM, ANY], out_specs=[HBM] * (2 * n),
        out_shape=[pltpu.HBM(a.shape, a.dtype) for a in (*srcs, *dsts)],
        input_output_aliases={t: t for t in range(2 * n)},
        compiler_params=pltpu.CompilerParams(has_side_effects=DATAFLOW),
    )(*srcs, *dsts, send_sems, recv_sems, after)
    return outs[n:]


def _prepare_shard(name, w, idx, dtype, mine):
    _, R, C = w.shape
    tr = _row_tile(R)

    def body(mine_ref, w_ref, src_ref, land_ref):
        val = w_ref[...].astype(dtype)
        src_ref[...] = val
        land_ref[...] = val

    return pl.pallas_call(
        body, name=name,
        grid_spec=pltpu.PrefetchScalarGridSpec(
            num_scalar_prefetch=1, grid=(R // tr,),
            in_specs=[pl.BlockSpec((None, tr, C), lambda i, s: (idx, i, 0))],
            out_specs=[pl.BlockSpec((tr, C), lambda i, s: (i, 0)), pl.BlockSpec((None, tr, C), lambda i, s: (s[0], i, 0))]),
        out_shape=[jax.ShapeDtypeStruct((R, C), dtype), jax.ShapeDtypeStruct((N_SHARDS, R, C), dtype)],
        compiler_params=_cp("parallel"),
    )(mine, w)


def _swap_with_sibling(parts):
    n = len(parts)

    def body(*refs):
        ins, outs = refs[:n], refs[n:2 * n]
        send_sems, recv_sems = refs[2 * n:]
        x, y, c, _ = _place()
        cps = [pltpu.make_async_remote_copy(src_ref=ins[t], dst_ref=outs[t], send_sem=send_sems.at[t], recv_sem=recv_sems.at[t],
                                            device_id=(x, y, 1 - c), device_id_type=MESH) for t in range(n)]
        for cp in cps:
            cp.start()
        for cp in cps:
            cp.wait_recv()
        for cp in cps:
            cp.wait_send()

    return pl.pallas_call(
        body, name="swap_partial_grads", in_specs=[ANY] * n, out_specs=[ANY] * n,
        out_shape=[jax.ShapeDtypeStruct(p.shape, p.dtype) for p in parts],
        scratch_shapes=[pltpu.SemaphoreType.DMA((n,)), pltpu.SemaphoreType.DMA((n,))],
    )(*parts)


def _allreduce_small(v):
    rows = v.shape[0]

    def body(v_ref, out_ref, buf, send_sems, recv_sems):
        x, y, c, chips = _place()
        me, sibling = (x, y, c), (x, y, 1 - c)

        def slot(px, py, pc):
            return buf.at[4 * px + 2 * py + pc]

        def copy(k, block, to, src=None):
            return pltpu.make_async_remote_copy(
                src_ref=slot(*block) if src is None else src, dst_ref=slot(*block), send_sem=send_sems.at[k],
                recv_sem=recv_sems.at[k], device_id=to, device_id_type=MESH)

        slot(*me)[...] = v_ref[...]
        first = [copy(0, me, sibling, src=v_ref)] + [copy(1 + j, me, (*chip, c), src=v_ref) for j, chip in enumerate(chips)]
        for cp in first:
            cp.start()
        passed = [copy(4 + j, (*chip, c), sibling) for j, chip in enumerate(chips)]
        for j, chip in enumerate(chips):
            copy(1 + j, (*chip, c), me).wait_recv()
            passed[j].start()
        copy(0, sibling, me).wait_recv()
        for j, chip in enumerate(chips):
            copy(4 + j, (*chip, 1 - c), me).wait_recv()
        for cp in first + passed:
            cp.wait_send()
        acc = buf[0]
        for k in range(1, 8):
            acc = acc + buf[k]
        out_ref[...] = acc

    return pl.pallas_call(
        body, name="allreduce_small_grads",
        in_specs=[pl.BlockSpec(memory_space=pltpu.VMEM)], out_specs=pl.BlockSpec(memory_space=pltpu.VMEM),
        out_shape=jax.ShapeDtypeStruct((rows, LANES), F32),
        scratch_shapes=[pltpu.VMEM((8, rows, LANES), F32), pltpu.SemaphoreType.DMA((7,)), pltpu.SemaphoreType.DMA((7,))],
        compiler_params=pltpu.CompilerParams(vmem_limit_bytes=VMEM_LIMIT_BYTES),
    )(v)


MM_TM_K = 512
WGRAD_TM = 2048


def _rows_merged(w):
    return w.reshape(1, w.shape[0] * w.shape[1], w.shape[2])


def _sq_relu_epilogue(acc):
    r = jnp.maximum(acc, 0.0)
    return acc, r * r


def _add_epilogue(acc, x):
    return (acc + x,)


def _add_norm_epilogue(acc, x, g):
    y = acc + x
    r = lax.rsqrt(jnp.mean(y * y, axis=-1, keepdims=True) + EPS)
    return y, y * r * g


def _norm_bwd_epilogue(dh, x, dres, g):
    r = lax.rsqrt(jnp.mean(x * x, axis=-1, keepdims=True) + EPS)
    xhat = x * r
    dxhat = dh * g
    dx = dres + r * (dxhat - xhat * jnp.mean(dxhat * xhat, axis=-1, keepdims=True))
    return dx, jnp.sum(dh * xhat, axis=0, keepdims=True)


def _sq_relu_grad_epilogue(acc, a):
    return (acc * (2.0 * jnp.maximum(a.astype(F32), 0.0)),)


STAGES = ("mixer", "mlp")


def _stage_tensors(layer, stage):
    i = layer // 2
    if stage == "mlp":
        return [("mlp_w1", layer), ("mlp_w2", layer)]
    return [("ab_w_in", i), ("b_conv_w", i), ("ab_w_out", i)] if layer % 2 == 0 else [("c_w_qkv", i), ("c_w_out", i)]


def _local_step(x, target, p, weights_of, grads_done):
    S, D = x.shape
    depth = p["mix_norm_g"].shape[0]
    n_even = (depth + 1) // 2
    mix_g3 = p["mix_norm_g"].reshape(depth, 1, D)
    mlp_g3 = p["mlp_norm_g"].reshape(depth, 1, D)
    vec3 = lambda t: t.reshape(t.shape[0], 1, t.shape[1])
    spw16 = p["a_spatial_w"].astype(BF16)
    spw16_t = jnp.swapaxes(spw16, 2, 3)
    bias_full = jnp.repeat(jnp.swapaxes(p["a_spatial_b"], 1, 2), HEAD_DIM, axis=2)
    vn_g, vn_b, cn_g, cn_b, cb3 = (vec3(p[k]) for k in ("a_vnorm_g", "a_vnorm_b", "b_norm_g", "b_norm_b", "b_conv_b"))
    tables = _rope_tables(S)
    gq = jnp.tile(p["c_q_norm_g"], (1, 2))
    gk = jnp.tile(p["c_k_norm_g"], (1, 2))

    saved = []
    h = _rms_fwd("mix_norm_0", x, mix_g3, 0)
    for layer in range(depth):
        i = layer // 2
        wl = dict(weights_of(layer, "mixer", x))
        rec = {"x_mix": x, "w": wl, "h_mix": h}
        if layer % 2 == 0:
            (z,) = _mm_ngroup(f"ab_in_{layer}", h, wl["ab_w_in"], nt=False, tm=MM_TM_K, out_dtypes=[F32])
            gconv = _glu_conv_fwd(f"glu_conv_{layer}", z, wl["b_conv_w"], cb3, i)
            cat = _ab_tail_fwd(f"ab_tail_{layer}", z, gconv, spw16, bias_full, vn_g, vn_b, cn_g, cn_b, i)
            x, h = _mm_kgroup(f"ab_out_{layer}", cat, _rows_merged(wl["ab_w_out"]), nt=False, tm=MM_TM_K,
                              out_dtypes=[F32, BF16], extras=(x,), vecs=[(mlp_g3, layer)], epilogue=_add_norm_epilogue)
            rec.update(z=z, gconv=gconv, cat=cat)
        else:
            (qkv,) = _mm_ngroup(f"c_qkv_{layer}", h, wl["c_w_qkv"], nt=False, tm=MM_TM_K, out_dtypes=[F32])
            qn, kn = _qk_fwd(f"qk_norm_rope_{layer}", qkv, gq[i:i + 1], gk[i:i + 1], tables)
            os, lses = zip(*[_attn_fwd(f"attn_d{d}_{layer}", qn, kn, qkv, V_COL, d) for d in PATTERN_DILATIONS])
            o, lse = _attn_merge(os, lses)
            x, h = _mm_kgroup(f"c_out_{layer}", o, _rows_merged(wl["c_w_out"]), nt=False, tm=MM_TM_K,
                              out_dtypes=[F32, BF16], extras=(x,), vecs=[(mlp_g3, layer)], epilogue=_add_norm_epilogue)
            rec.update(qkv=qkv, qn=qn, kn=kn, o=o, lse=lse)
        rec["x_mlp"] = x
        wl.update(weights_of(layer, "mlp", x))
        a, hsq = _mm_ngroup(f"mlp_up_{layer}", h, wl["mlp_w1"], nt=False, tm=MM_TM_K, out_dtypes=[BF16, BF16],
                            epilogue=_sq_relu_epilogue)
        rec.update(h_mlp=h, a=a, hsq=hsq)
        if layer + 1 < depth:
            x, h = _mm_kgroup(f"mlp_down_{layer}", hsq, _rows_merged(wl["mlp_w2"]), nt=False, tm=MM_TM_K,
                              out_dtypes=[F32, BF16], extras=(x,), vecs=[(mix_g3, layer + 1)], epilogue=_add_norm_epilogue)
        else:
            (x,) = _mm_kgroup(f"mlp_down_{layer}", hsq, _rows_merged(wl["mlp_w2"]), nt=False, tm=MM_TM_K, out_dtypes=[F32],
                              extras=(x,), epilogue=_add_epilogue)
        saved.append(rec)

    dx, loss_row = _loss_grad(x, target)

    small = {k: [None] * v.shape[0] for k, v in p.items()}
    token = None
    for layer in reversed(range(depth)):
        i = layer // 2
        rec = saved[layer]
        wl = rec["w"]
        g = {}
        (da,) = _mm_ngroup(f"mlp_down_dgrad_{layer}", dx, wl["mlp_w2"], nt=True, tm=MM_TM_K, out_dtypes=[BF16],
                           extras=(rec["a"],), epilogue=_sq_relu_grad_epilogue, anchor=token)
        g["mlp_w2"] = _wgrad(f"mlp_down_wgrad_{layer}", rec["hsq"], dx, wl["mlp_w2"].shape, a_group=True, tm=WGRAD_TM)
        g["mlp_w1"] = _wgrad(f"mlp_up_wgrad_{layer}", rec["h_mlp"], da, wl["mlp_w1"].shape, a_group=False, tm=WGRAD_TM)
        dx, small["mlp_norm_g"][layer] = _mm_kgroup(
            f"mlp_up_dgrad_{layer}", da, wl["mlp_w1"], nt=True, tm=MM_TM_K, out_dtypes=[F32], extras=(rec["x_mlp"], dx),
            vecs=[(mlp_g3, layer)], n_sums=1, epilogue=_norm_bwd_epilogue)
        token = grads_done(layer, "mlp", g)
        g = {}
        if layer % 2 == 0:
            w_out = _rows_merged(wl["ab_w_out"])
            (dcat,) = _mm_ngroup(f"ab_out_dgrad_{layer}", dx, w_out, nt=True, tm=MM_TM_K, out_dtypes=[F32], anchor=token)
            g["ab_w_out"] = [t.reshape(wl["ab_w_out"].shape) for t in
                             _wgrad(f"ab_out_wgrad_{layer}", rec["cat"], dx, w_out.shape, a_group=True, tm=WGRAD_TM)]
            dz, dgconv, dspw, dbias, dvg, dvb, dcg, dcb = _ab_tail_bwd(
                f"ab_tail_bwd_{layer}", rec["z"], rec["gconv"], dcat, spw16, spw16_t, bias_full, vn_g, vn_b, cn_g, cn_b, i)
            dz, gf, gb, dcbias = _glu_conv_bwd(f"glu_conv_bwd_{layer}", rec["z"], dgconv, dz, wl["b_conv_w"])
            g["b_conv_w"] = (gf, gb)
            small["a_spatial_w"][i] = dspw
            small["a_spatial_b"][i] = _fold_bias(dbias)[:, :A_GROUPS].T
            for k, val in (("a_vnorm_g", dvg), ("a_vnorm_b", dvb), ("b_norm_g", dcg), ("b_norm_b", dcb), ("b_conv_b", dcbias)):
                small[k][i] = val
            g["ab_w_in"] = _wgrad(f"ab_in_wgrad_{layer}", rec["h_mix"], dz, wl["ab_w_in"].shape, a_group=False, tm=WGRAD_TM)
            dgrad = (f"ab_in_dgrad_{layer}", dz, wl["ab_w_in"])
        else:
            w_out = _rows_merged(wl["c_w_out"])
            (do,) = _mm_ngroup(f"c_out_dgrad_{layer}", dx, w_out, nt=True, tm=MM_TM_K, out_dtypes=[F32], anchor=token)
            g["c_w_out"] = [t.reshape(wl["c_w_out"].shape) for t in
                            _wgrad(f"c_out_wgrad_{layer}", rec["o"], dx, w_out.shape, a_group=True, tm=WGRAD_TM)]
            delta = _attn_delta(do, rec["o"])
            attn_args = (rec["qn"], rec["kn"], rec["qkv"], V_COL, do, rec["lse"], delta)
            dqs = [_attn_bwd_q(f"attn_bwd_q_d{d}_{layer}", *attn_args, d) for d in PATTERN_DILATIONS]
            dks, dvs = zip(*[_attn_bwd_kv(f"attn_bwd_kv_d{d}_{layer}", *attn_args, d) for d in PATTERN_DILATIONS])
            dqkv, dgq, dgk = _qk_bwd(f"qk_norm_rope_bwd_{layer}", rec["qkv"], gq[i:i + 1], gk[i:i + 1], tables, dqs, dks, dvs)
            small["c_q_norm_g"][i] = dgq[:, :HEAD_DIM]
            small["c_k_norm_g"][i] = dgk[:, :HEAD_DIM]
            g["c_w_qkv"] = _wgrad(f"c_qkv_wgrad_{layer}", rec["h_mix"], dqkv, wl["c_w_qkv"].shape, a_group=False, tm=WGRAD_TM)
            dgrad = (f"c_qkv_dgrad_{layer}", dqkv, wl["c_w_qkv"])
        dx, small["mix_norm_g"][layer] = _mm_kgroup(
            *dgrad, nt=True, tm=MM_TM_K, out_dtypes=[F32], extras=(rec["x_mix"], dx), vecs=[(mix_g3, layer)], n_sums=1,
            epilogue=_norm_bwd_epilogue)
        token = grads_done(layer, "mixer", g)

    small = {k: jnp.stack([t.reshape(p[k].shape[1:]) for t in v]) for k, v in small.items()}
    return loss_row, dx, small


SHARDED = ("mlp_w1", "mlp_w2", "ab_w_in", "b_conv_w", "ab_w_out", "c_w_qkv", "c_w_out")
SMALL = ("mix_norm_g", "mlp_norm_g", "a_spatial_w", "a_spatial_b", "a_vnorm_g", "a_vnorm_b", "b_conv_b", "b_norm_g",
         "b_norm_b", "c_q_norm_g", "c_k_norm_g")
WEIGHTS = ("mix_norm_g", "mlp_norm_g", "mlp_w1", "mlp_w2", "ab_w_in", "a_spatial_w", "a_spatial_b", "a_vnorm_g",
           "a_vnorm_b", "b_conv_w", "b_conv_b", "b_norm_g", "b_norm_b", "ab_w_out", "c_w_qkv", "c_q_norm_g",
           "c_k_norm_g", "c_w_out")


def _pack(parts):
    flat = jnp.concatenate([parts[k].reshape(-1) for k in SMALL])
    rows = -(-flat.shape[0] // (256 * LANES)) * 256
    return jnp.pad(flat, (0, rows * LANES - flat.shape[0])).reshape(rows, LANES)


def _unpack(packed, like):
    flat = packed.reshape(-1)
    out, off = {}, 0
    for k in SMALL:
        n = like[k].size
        out[k] = flat[off:off + n].reshape(like[k].shape)
        off += n
    return out


def kernel(x, mix_norm_g, mlp_norm_g, mlp_w1, mlp_w2, ab_w_in, a_spatial_w, a_spatial_b, a_vnorm_g, a_vnorm_b, b_conv_w, b_conv_b, b_norm_g, b_norm_b, ab_w_out, c_w_qkv, c_q_norm_g, c_k_norm_g, c_w_out, loss_target, m_mix_norm_g, m_mlp_norm_g, m_mlp_w1, m_mlp_w2, m_ab_w_in, m_a_spatial_w, m_a_spatial_b, m_a_vnorm_g, m_a_vnorm_b, m_b_conv_w, m_b_conv_b, m_b_norm_g, m_b_norm_b, m_ab_w_out, m_c_w_qkv, m_c_q_norm_g, m_c_k_norm_g, m_c_w_out, v_mix_norm_g, v_mlp_norm_g, v_mlp_w1, v_mlp_w2, v_ab_w_in, v_a_spatial_w, v_a_spatial_b, v_a_vnorm_g, v_a_vnorm_b, v_b_conv_w, v_b_conv_b, v_b_norm_g, v_b_norm_b, v_ab_w_out, v_c_w_qkv, v_c_q_norm_g, v_c_k_norm_g, v_c_w_out):
    w = dict(mix_norm_g=mix_norm_g, mlp_norm_g=mlp_norm_g, mlp_w1=mlp_w1, mlp_w2=mlp_w2, ab_w_in=ab_w_in,
             a_spatial_w=a_spatial_w, a_spatial_b=a_spatial_b, a_vnorm_g=a_vnorm_g, a_vnorm_b=a_vnorm_b,
             b_conv_w=b_conv_w, b_conv_b=b_conv_b, b_norm_g=b_norm_g, b_norm_b=b_norm_b, ab_w_out=ab_w_out,
             c_w_qkv=c_w_qkv, c_q_norm_g=c_q_norm_g, c_k_norm_g=c_k_norm_g, c_w_out=c_w_out)
    m = dict(mix_norm_g=m_mix_norm_g, mlp_norm_g=m_mlp_norm_g, mlp_w1=m_mlp_w1, mlp_w2=m_mlp_w2, ab_w_in=m_ab_w_in,
             a_spatial_w=m_a_spatial_w, a_spatial_b=m_a_spatial_b, a_vnorm_g=m_a_vnorm_g, a_vnorm_b=m_a_vnorm_b,
             b_conv_w=m_b_conv_w, b_conv_b=m_b_conv_b, b_norm_g=m_b_norm_g, b_norm_b=m_b_norm_b, ab_w_out=m_ab_w_out,
             c_w_qkv=m_c_w_qkv, c_q_norm_g=m_c_q_norm_g, c_k_norm_g=m_c_k_norm_g, c_w_out=m_c_w_out)
    v = dict(mix_norm_g=v_mix_norm_g, mlp_norm_g=v_mlp_norm_g, mlp_w1=v_mlp_w1, mlp_w2=v_mlp_w2, ab_w_in=v_ab_w_in,
             a_spatial_w=v_a_spatial_w, a_spatial_b=v_a_spatial_b, a_vnorm_g=v_a_vnorm_g, a_vnorm_b=v_a_vnorm_b,
             b_conv_w=v_b_conv_w, b_conv_b=v_b_conv_b, b_norm_g=v_b_norm_g, b_norm_b=v_b_norm_b, ab_w_out=v_ab_w_out,
             c_w_qkv=v_c_w_qkv, c_q_norm_g=v_c_q_norm_g, c_k_norm_g=v_c_k_norm_g, c_w_out=v_c_w_out)

    S, D = x.shape[1], x.shape[2]
    depth = mix_norm_g.shape[0]
    mine = (2 * lax.axis_index("x") + lax.axis_index("y")).astype(jnp.int32).reshape(1)

    stages = [(layer, stage) for layer in range(depth) for stage in STAGES]
    groups = [[(k, i) + tuple(_prepare_shard(f"prepare_{k}_{i}", w[k], i, F32 if k == "b_conv_w" else BF16, mine))
               for k, i in _stage_tensors(*st)] for st in stages]
    handles, gather_token = _exchange_start("gather_weights_start", "gather", [[(s, l) for _, _, s, l in g] for g in groups])
    handles = dict(zip(stages, handles))

    def weights_of(layer, stage, after):
        got = _exchange_wait(f"gather_weights_wait_{layer}_{stage}", "gather", handles[layer, stage],
                             gather_token if (layer, stage) == stages[0] else after)
        return {k: a for (k, _), a in zip(_stage_tensors(layer, stage), got)}

    scattered = {}

    def grads_done(layer, stage, g):
        names = [k for k, _ in _stage_tensors(layer, stage)]
        group = [(g[k][1], lax.empty((3,) + g[k][1].shape[1:], BF16)) for k in names]
        (handle,), token = _exchange_start(f"scatter_grads_start_{layer}_{stage}", "scatter", [group])
        scattered[layer, stage] = (handle, [g[k][0] for k in names])
        return token

    small_params = {k: w[k] for k in SMALL}
    loss_row, dx, small_grads = _local_step(x.reshape(S, D), loss_target.reshape(S, D), small_params, weights_of, grads_done)

    loss = lax.psum(loss_row[0, 0], ("x", "y", "c"))

    partial, order = [], []
    for layer, stage in reversed(stages):
        handle, gfs = scattered[layer, stage]
        recvs = _exchange_wait(f"scatter_grads_wait_{layer}_{stage}", "scatter", handle, dx)
        tensors = _stage_tensors(layer, stage)
        partial += [_sum4(f"sum_chips_{k}_{i}", gf, r, mine) for (k, i), gf, r in zip(tensors, gfs, recvs)]
        order += tensors
    other = _swap_with_sibling(partial)
    stacked = {k: [lax.empty(w[k].shape, F32) for _ in range(4)] for k in SHARDED}
    for (k, i), a, b in zip(order, partial, other):
        stacked[k] = _adamw_layer(f"adamw_{k}_{i}", w[k], m[k], v[k], i, a, b, stacked[k])
    grads, deltas, new_m, new_v = ({k: stacked[k][j] for k in SHARDED} for j in range(4))

    g_small = _allreduce_small(_pack(small_grads))
    outs = _adamw("adamw_small", _pack(small_params), _pack({k: m[k] for k in SMALL}), _pack({k: v[k] for k in SMALL}), g_small)
    for d_, packed in zip((grads, deltas, new_m, new_v), outs):
        d_.update(_unpack(packed, small_params))

    return (loss, dx.reshape(1, S, D), *[grads[k] for k in WEIGHTS], *[deltas[k] for k in WEIGHTS],
            *[new_m[k] for k in WEIGHTS], *[new_v[k] for k in WEIGHTS])
```

```python
import functools

import jax
import jax.numpy as jnp
from jax import lax
from jax.experimental import pallas as pl
from jax.experimental.pallas import tpu as pltpu

F32, BF16 = jnp.float32, jnp.bfloat16
MESH = pl.DeviceIdType.MESH
ANY = pl.BlockSpec(memory_space=pl.ANY)

VMEM_LIMIT_BYTES = 56 * 1024 * 1024
LANES = 128
ELEMENTWISE_ROWS = 256

EPS = 1e-6
NEG = -1e30
HEAD_DIM = 64
N_HEADS = 16
CHUNK = 128
A_GROUPS = 8
CONV_WIDTH = 31
CONV_HALO = 16
BAND = 64
PATTERN_DILATIONS = (1, 4, 16)
ROT_DIM = 16
ROPE_THETA = 500000.0
N_SHARDS = 4

ADAM_LR, ADAM_B1, ADAM_B2, ADAM_EPS, ADAM_WD, ADAM_STEP = 0.001, 0.9, 0.999, 1e-08, 0.01, 10


def _cp(*sem):
    return pltpu.CompilerParams(dimension_semantics=sem, vmem_limit_bytes=VMEM_LIMIT_BYTES)


def _tile(n, pref):
    t = min(n, pref)
    assert n % t == 0, (n, pref)
    return t


def _dot(a, b, ca, cb):
    return lax.dot_general(a, b, (((ca,), (cb,)), ((), ())), preferred_element_type=F32)


def _mm_ngroup(name, a, w, *, nt, tm, out_dtypes, extras=(), epilogue=None, anchor=None):
    M, K = a.shape
    G, R, C = w.shape
    nw = R if nt else C
    assert K == (C if nt else R)
    tm = _tile(M, tm)
    n_ex = len(extras)
    anchors = [] if anchor is None else [anchor]

    def body(a_ref, w_ref, *rest):
        rest = rest[len(anchors):]
        av = a_ref[...].astype(BF16)
        for g in range(G):
            cols = slice(g * nw, (g + 1) * nw)
            acc = _dot(av, w_ref[g], 1, 1 if nt else 0)
            res = epilogue(acc, *[e[:, cols] for e in rest[:n_ex]]) if epilogue else (acc,)
            for o_ref, r in zip(rest[n_ex:], res):
                o_ref[:, cols] = r.astype(o_ref.dtype)

    blk = pl.BlockSpec((tm, G * nw), lambda m: (m, 0))
    return pl.pallas_call(
        body, name=name, grid=(M // tm,),
        in_specs=[pl.BlockSpec((tm, K), lambda m: (m, 0)), pl.BlockSpec((G, R, C), lambda m: (0, 0, 0))]
        + [pl.BlockSpec((8, LANES), lambda m: (0, 0))] * len(anchors) + [blk] * n_ex,
        out_specs=[blk] * len(out_dtypes),
        out_shape=[jax.ShapeDtypeStruct((M, G * nw), dt) for dt in out_dtypes],
        compiler_params=_cp("parallel"),
    )(a, w, *anchors, *extras)


def _mm_kgroup(name, a, w, *, nt, tm, out_dtypes, extras=(), vecs=(), n_sums=0, epilogue=None):
    G, R, C = w.shape
    kw, N = (C, R) if nt else (R, C)
    if a.ndim == 3:
        M = a.shape[1]
        assert a.shape[0] == G and a.shape[2] == kw
    else:
        M = a.shape[0]
        assert a.shape[1] == G * kw
    tm = _tile(M, tm)
    n_ex = len(extras)
    a_spec = (pl.BlockSpec((G, tm, kw), lambda m: (0, m, 0)) if a.ndim == 3 else pl.BlockSpec((tm, G * kw), lambda m: (m, 0)))

    def body(a_ref, w_ref, *rest):
        acc = None
        for g in range(G):
            a_g = a_ref[g] if a.ndim == 3 else a_ref[:, g * kw:(g + 1) * kw]
            part = _dot(a_g.astype(BF16), w_ref[g], 1, 1 if nt else 0)
            acc = part if acc is None else acc + part
        n_in = n_ex + len(vecs)
        res = epilogue(acc, *[e[...] for e in rest[:n_in]]) if epilogue else (acc,)
        outs = rest[n_in:]
        n_tiles = len(outs) - n_sums
        for o_ref, r in zip(outs[:n_tiles], res[:n_tiles]):
            o_ref[...] = r.astype(o_ref.dtype)
        if n_sums:
            @pl.when(pl.program_id(0) == 0)
            def _():
                for s_ref in outs[n_tiles:]:
                    s_ref[...] = jnp.zeros_like(s_ref)

            for s_ref, r in zip(outs[n_tiles:], res[n_tiles:]):
                s_ref[...] += r

    blk = pl.BlockSpec((tm, N), lambda m: (m, 0))
    row = pl.BlockSpec((1, N), lambda m: (0, 0))
    return pl.pallas_call(
        body, name=name, grid=(M // tm,),
        in_specs=[a_spec, pl.BlockSpec((G, R, C), lambda m: (0, 0, 0))] + [blk] * n_ex
        + [pl.BlockSpec((None, 1, N), lambda m, i=i: (i, 0, 0)) for _, i in vecs],
        out_specs=[blk] * len(out_dtypes) + [row] * n_sums,
        out_shape=[jax.ShapeDtypeStruct((M, N), dt) for dt in out_dtypes] + [jax.ShapeDtypeStruct((1, N), F32)] * n_sums,
        compiler_params=_cp("arbitrary" if n_sums else "parallel"),
    )(a, w, *extras, *[v for v, _ in vecs])


def _wgrad(name, a, b, shape, *, a_group, tm):
    G, R, C = shape
    M = a.shape[0]
    tm = _tile(M, tm)
    n_m = M // tm

    def body(a_ref, b_ref, gf_ref, gb_ref):
        m = pl.program_id(1)
        part = _dot(a_ref[...].astype(BF16), b_ref[...].astype(BF16), 0, 0)

        @pl.when(m == 0)
        def _():
            gf_ref[...] = part

        @pl.when(m > 0)
        def _():
            gf_ref[...] += part

        @pl.when(m == n_m - 1)
        def _():
            gb_ref[...] = gf_ref[...].astype(BF16)

    a_spec = pl.BlockSpec((tm, R), (lambda g, m: (m, g)) if a_group else (lambda g, m: (m, 0)))
    if b.ndim == 3:
        assert not a_group
        b_spec = pl.BlockSpec((None, tm, C), lambda g, m: (g, m, 0))
    else:
        b_spec = pl.BlockSpec((tm, C), (lambda g, m: (m, 0)) if a_group else (lambda g, m: (m, g)))
    o_spec = pl.BlockSpec((None, R, C), lambda g, m: (g, 0, 0))
    return pl.pallas_call(
        body, name=name, grid=(G, n_m),
        in_specs=[a_spec, b_spec], out_specs=[o_spec, o_spec],
        out_shape=[jax.ShapeDtypeStruct(shape, F32), jax.ShapeDtypeStruct(shape, BF16)],
        compiler_params=_cp("parallel", "arbitrary"),
    )(a, b)


def _rms_fwd(name, x, g3, layer):
    S, D = x.shape
    tm = _tile(S, 512)

    def body(x_ref, g_ref, h_ref):
        xv = x_ref[...]
        r = lax.rsqrt(jnp.mean(xv * xv, axis=-1, keepdims=True) + EPS)
        h_ref[...] = (xv * r * g_ref[...]).astype(BF16)

    row = pl.BlockSpec((tm, D), lambda m: (m, 0))
    return pl.pallas_call(
        body, name=name, grid=(S // tm,),
        in_specs=[row, pl.BlockSpec((None, 1, D), lambda m: (layer, 0, 0))], out_specs=row,
        out_shape=jax.ShapeDtypeStruct((S, D), BF16), compiler_params=_cp("parallel"),
    )(x, g3)


def _loss_grad(y, target):
    S, D = y.shape
    tm = _tile(S, 512)

    def body(y_ref, t_ref, dy_ref, l_ref):
        e = y_ref[...] - t_ref[...]
        dy_ref[...] = e * (1.0 / D)

        @pl.when(pl.program_id(0) == 0)
        def _():
            l_ref[...] = jnp.zeros_like(l_ref)

        l_ref[...] += (0.5 / D) * jnp.sum(jnp.sum(e * e, axis=1, keepdims=True), axis=0, keepdims=True)

    row = pl.BlockSpec((tm, D), lambda m: (m, 0))
    return pl.pallas_call(
        body, name="loss_grad", grid=(S // tm,), in_specs=[row, row],
        out_specs=[row, pl.BlockSpec((1, LANES), lambda m: (0, 0))],
        out_shape=[jax.ShapeDtypeStruct((S, D), F32), jax.ShapeDtypeStruct((1, LANES), F32)],
        compiler_params=_cp("arbitrary"),
    )(y, target)


def _adamw_math(w, m, v, g):
    m2 = ADAM_B1 * m + (1.0 - ADAM_B1) * g
    v2 = ADAM_B2 * v + (1.0 - ADAM_B2) * jnp.square(g)
    m_hat = m2 / (1.0 - ADAM_B1 ** ADAM_STEP)
    v_hat = v2 / (1.0 - ADAM_B2 ** ADAM_STEP)
    return g, -ADAM_LR * (m_hat / (jnp.sqrt(v_hat) + ADAM_EPS) + ADAM_WD * w), m2, v2


def _row_tile(rows):
    return _tile(rows, ELEMENTWISE_ROWS) if rows % ELEMENTWISE_ROWS == 0 else rows


def _adamw(name, w, m, v, g):
    rows, C = w.shape
    tr = _row_tile(rows)

    def body(w_ref, m_ref, v_ref, g_in, g_ref, d_ref, nm_ref, nv_ref):
        for o_ref, val in zip((g_ref, d_ref, nm_ref, nv_ref), _adamw_math(w_ref[...], m_ref[...], v_ref[...], g_in[...])):
            o_ref[...] = val

    blk = pl.BlockSpec((tr, C), lambda i: (i, 0))
    return pl.pallas_call(
        body, name=name, grid=(rows // tr,), in_specs=[blk] * 4, out_specs=[blk] * 4,
        out_shape=[jax.ShapeDtypeStruct((rows, C), F32)] * 4, compiler_params=_cp("parallel"),
    )(w, m, v, g)


def _adamw_layer(name, w, m, v, layer, mine, theirs, outs):
    _, R, C = w.shape
    tr = _row_tile(R)

    def body(w_ref, m_ref, v_ref, a_ref, b_ref, *rest):
        g = a_ref[...] + b_ref[...]
        for o_ref, val in zip(rest[4:], _adamw_math(w_ref[...], m_ref[...], v_ref[...], g)):
            o_ref[...] = val

    st = pl.BlockSpec((None, tr, C), lambda i: (layer, i, 0))
    part = pl.BlockSpec((tr, C), lambda i: (i, 0))
    return pl.pallas_call(
        body, name=name, grid=(R // tr,), in_specs=[st] * 3 + [part] * 2 + [ANY] * 4, out_specs=[st] * 4,
        out_shape=[jax.ShapeDtypeStruct(w.shape, F32)] * 4, input_output_aliases={5 + j: j for j in range(4)},
        compiler_params=_cp("parallel"),
    )(w, m, v, mine, theirs, *outs)


def _sum4(name, gf, recv, mine):
    _, R, C = gf.shape
    tr = _row_tile(R)

    def body(mine_ref, o_ref, r_ref, out_ref):
        acc = o_ref[...]
        for k in range(3):
            acc = acc + r_ref[k].astype(F32)
        out_ref[...] = acc

    return pl.pallas_call(
        body, name=name,
        grid_spec=pltpu.PrefetchScalarGridSpec(
            num_scalar_prefetch=1, grid=(R // tr,),
            in_specs=[pl.BlockSpec((None, tr, C), lambda i, s: (s[0], i, 0)), pl.BlockSpec((3, tr, C), lambda i, s: (0, i, 0))],
            out_specs=pl.BlockSpec((tr, C), lambda i, s: (i, 0))),
        out_shape=jax.ShapeDtypeStruct((R, C), F32), compiler_params=_cp("parallel"),
    )(mine, gf, recv)


def _gelu(x):
    return x * (0.5 * (1.0 + jnp.tanh(0.7978845608028654 * (x + 0.044715 * (x * x * x)))))


def _layernorm(t, g, b):
    mu = jnp.mean(t, axis=-1, keepdims=True)
    var = jnp.mean(jnp.square(t - mu), axis=-1, keepdims=True)
    return (t - mu) * lax.rsqrt(var + EPS) * g + b


def _silu(x):
    return x * jax.nn.sigmoid(x)


def _a_value(zv, g, b):
    return _layernorm(_gelu(zv), g, b)


def _b_tail(gc, g, b):
    return _silu(_layernorm(gc, g, b))


def _first_head(shape):
    return lax.broadcasted_iota(jnp.int32, shape, len(shape) - 1) < HEAD_DIM


def _spatial_mix(spw_ref, vb, tm):
    first = _first_head((CHUNK, LANES))
    rows = []
    for n in range(tm // CHUNK):
        blocks = []
        for j in range(A_GROUPS // 2):
            vblk = vb[n * CHUNK:(n + 1) * CHUNK, j * LANES:(j + 1) * LANES]
            r0 = _dot(spw_ref[2 * j], vblk, 1, 0)
            r1 = _dot(spw_ref[2 * j + 1], vblk, 1, 0)
            blocks.append(jnp.where(first, r0, r1))
        rows.append(jnp.concatenate(blocks, axis=1))
    return jnp.concatenate(rows, axis=0) if len(rows) > 1 else rows[0]


def _ab_tail_fwd(name, z, gconv, spw, bias_full, vn_g, vn_b, cn_g, cn_b, layer):
    S = z.shape[0]
    AW = 512
    tm = _tile(S, 256)

    def body(zu_ref, zv_ref, gc_ref, spw_ref, bias_ref, vg_ref, vb_ref, cg_ref, cb_ref, cat_ref):
        u = _gelu(zu_ref[...])
        v = _a_value(zv_ref[...], vg_ref[...], vb_ref[...])
        sv = _spatial_mix(spw_ref, v.astype(BF16), tm) + jnp.tile(bias_ref[...], (tm // CHUNK, 1))
        cat_ref[:, :AW] = (u * sv).astype(BF16)
        cat_ref[:, AW:] = _b_tail(gc_ref[...], cg_ref[...], cb_ref[...]).astype(BF16)

    vec = pl.BlockSpec((None, 1, AW), lambda m: (layer, 0, 0))
    return pl.pallas_call(
        body, name=name, grid=(S // tm,),
        in_specs=[pl.BlockSpec((tm, AW), lambda m: (m, 0)), pl.BlockSpec((tm, AW), lambda m: (m, 1)),
                  pl.BlockSpec((tm, AW), lambda m: (m, 0)),
                  pl.BlockSpec((None, A_GROUPS, CHUNK, CHUNK), lambda m: (layer, 0, 0, 0)),
                  pl.BlockSpec((None, CHUNK, AW), lambda m: (layer, 0, 0)), vec, vec, vec, vec],
        out_specs=pl.BlockSpec((tm, 2 * AW), lambda m: (m, 0)),
        out_shape=jax.ShapeDtypeStruct((S, 2 * AW), BF16), compiler_params=_cp("parallel"),
    )(z, z, gconv, spw, bias_full, vn_g, vn_b, cn_g, cn_b)


def _ab_tail_bwd(name, z, gconv, dcat, spw, spw_t, bias_full, vn_g, vn_b, cn_g, cn_b, layer):
    S = z.shape[0]
    AW = 512
    tm = _tile(S, 256)
    n_chunks = tm // CHUNK

    def body(zu_ref, zv_ref, gc_ref, dcat_ref, spw_ref, spwt_ref, bias_ref, vg_ref, vb_ref, cg_ref, cb_ref,
             dz_ref, dgc_ref, dspw_ref, dbias_ref, dvg_ref, dvb_ref, dcg_ref, dcb_ref):
        @pl.when(pl.program_id(0) == 0)
        def _():
            for r in (dspw_ref, dbias_ref, dvg_ref, dvb_ref, dcg_ref, dcb_ref):
                r[...] = jnp.zeros_like(r)

        dya = dcat_ref[:, :AW]
        dyb = dcat_ref[:, AW:]
        u, u_vjp = jax.vjp(_gelu, zu_ref[...])
        v, v_vjp = jax.vjp(_a_value, zv_ref[...], vg_ref[...], vb_ref[...])
        vb16 = v.astype(BF16)
        sv = _spatial_mix(spw_ref, vb16, tm) + jnp.tile(bias_ref[...], (n_chunks, 1))
        (dzu,) = u_vjp(dya * sv)
        dsv = dya * u
        dsv16 = dsv.astype(BF16)
        dv = _spatial_mix(spwt_ref, dsv16, tm)
        dzv, dvg, dvb = v_vjp(dv)
        dz_ref[0] = dzu
        dz_ref[1] = dzv
        dvg_ref[...] += dvg
        dvb_ref[...] += dvb

        first = _first_head((CHUNK, LANES))
        zero = jnp.zeros((), BF16)
        dbias = jnp.zeros((CHUNK, AW), F32)
        for n in range(n_chunks):
            rows = slice(n * CHUNK, (n + 1) * CHUNK)
            dbias = dbias + dsv[rows]
            for j in range(A_GROUPS // 2):
                cols = slice(j * LANES, (j + 1) * LANES)
                dblk, vblk = dsv16[rows, cols], vb16[rows, cols]
                dspw_ref[2 * j] += _dot(jnp.where(first, dblk, zero), vblk, 1, 1)
                dspw_ref[2 * j + 1] += _dot(jnp.where(first, zero, dblk), vblk, 1, 1)
        dbias_ref[...] += dbias

        _, t_vjp = jax.vjp(_b_tail, gc_ref[...], cg_ref[...], cb_ref[...])
        dgc, dcg, dcb = t_vjp(dyb)
        dgc_ref[...] = dgc
        dcg_ref[...] += dcg
        dcb_ref[...] += dcb

    vec = pl.BlockSpec((None, 1, AW), lambda m: (layer, 0, 0))
    spw_spec = pl.BlockSpec((None, A_GROUPS, CHUNK, CHUNK), lambda m: (layer, 0, 0, 0))
    ovec = pl.BlockSpec((1, AW), lambda m: (0, 0))
    return pl.pallas_call(
        body, name=name, grid=(S // tm,),
        in_specs=[pl.BlockSpec((tm, AW), lambda m: (m, 0)), pl.BlockSpec((tm, AW), lambda m: (m, 1)),
                  pl.BlockSpec((tm, AW), lambda m: (m, 0)), pl.BlockSpec((tm, 2 * AW), lambda m: (m, 0)),
                  spw_spec, spw_spec, pl.BlockSpec((None, CHUNK, AW), lambda m: (layer, 0, 0)), vec, vec, vec, vec],
        out_specs=[pl.BlockSpec((2, tm, AW), lambda m: (0, m, 0)), pl.BlockSpec((tm, AW), lambda m: (m, 0)),
                   pl.BlockSpec((A_GROUPS, CHUNK, CHUNK), lambda m: (0, 0, 0)),
                   pl.BlockSpec((CHUNK, AW), lambda m: (0, 0)), ovec, ovec, ovec, ovec],
        out_shape=[jax.ShapeDtypeStruct((4, S, AW), F32), jax.ShapeDtypeStruct((S, AW), F32),
                   jax.ShapeDtypeStruct((A_GROUPS, CHUNK, CHUNK), F32), jax.ShapeDtypeStruct((CHUNK, AW), F32)]
                  + [jax.ShapeDtypeStruct((1, AW), F32)] * 4,
        compiler_params=_cp("arbitrary"),
    )(z, z, gconv, dcat, spw, spw_t, bias_full, vn_g, vn_b, cn_g, cn_b)


def _fold_bias(dbias_full):
    def body(d_ref, o_ref):
        d = d_ref[...]
        hi = d.astype(BF16)
        lo = (d - hi.astype(F32)).astype(BF16)
        r = lax.broadcasted_iota(jnp.int32, (512, LANES), 0)
        c = lax.broadcasted_iota(jnp.int32, (512, LANES), 1)
        fold = jnp.where(lax.shift_right_logical(r, 6) == c, 1.0, 0.0).astype(BF16)
        o_ref[...] = _dot(hi, fold, 1, 0) + _dot(lo, fold, 1, 0)

    return pl.pallas_call(body, name="fold_spatial_bias", out_shape=jax.ShapeDtypeStruct((CHUNK, LANES), F32))(dbias_full)


def _halo_specs(tm, n_halo_blocks, col):
    r = tm // CONV_HALO
    prev = pl.BlockSpec((CONV_HALO, LANES), lambda j, i: (jnp.maximum(i * r - 1, 0), col + j))
    cur = pl.BlockSpec((tm, LANES), lambda j, i: (i, col + j))
    nxt = pl.BlockSpec((CONV_HALO, LANES), lambda j, i: (jnp.minimum((i + 1) * r, n_halo_blocks - 1), col + j))
    return [prev, cur, nxt]


def _fill_halo(scr, prev, cur, nxt, tm, i, n_i):
    scr[0:CONV_HALO, :] = jnp.where(i > 0, prev, 0.0)
    scr[CONV_HALO:CONV_HALO + tm, :] = cur
    scr[CONV_HALO + tm:2 * CONV_HALO + tm, :] = jnp.where(i < n_i - 1, nxt, 0.0)


def _glu_conv_fwd(name, z, cw, cb3, layer):
    S = z.shape[0]
    tm = _tile(S, 512)
    n_i = S // tm
    pad = CONV_WIDTH // 2

    def body(vp, vc, vn, gp, gc, gn, w_ref, b_ref, out_ref, scr):
        i = pl.program_id(1)
        glu = lambda a, b: a[...] * jax.nn.sigmoid(b[...])
        _fill_halo(scr, glu(vp, gp), glu(vc, gc), glu(vn, gn), tm, i, n_i)
        acc = jnp.zeros((tm, LANES), F32)
        for j in range(CONV_WIDTH):
            acc = acc + w_ref[j:j + 1, :] * scr[pl.ds(CONV_HALO - pad + j, tm), :]
        out_ref[...] = acc + b_ref[...]

    return pl.pallas_call(
        body, name=name, grid=(4, n_i),
        in_specs=_halo_specs(tm, S // CONV_HALO, 8) + _halo_specs(tm, S // CONV_HALO, 12)
        + [pl.BlockSpec((None, CONV_WIDTH, LANES), lambda j, i: (j, 0, 0)),
           pl.BlockSpec((None, 1, LANES), lambda j, i: (layer, 0, j))],
        out_specs=pl.BlockSpec((tm, LANES), lambda j, i: (i, j)),
        out_shape=jax.ShapeDtypeStruct((S, 4 * LANES), F32),
        scratch_shapes=[pltpu.VMEM((tm + 2 * CONV_HALO, LANES), F32)],
        compiler_params=_cp("parallel", "parallel"),
    )(z, z, z, z, z, z, cw, cb3)


def _glu_conv_bwd(name, z, dgconv, dz, cw):
    S = z.shape[0]
    tm = _tile(S, 512)
    n_i = S // tm
    pad = CONV_WIDTH // 2

    def body(vp, vc, vn, gp, gc, gn, dp, dc, dn, w_ref, dz_in, dz_ref, gf_ref, gb_ref, db_ref, g_scr, d_scr):
        i = pl.program_id(1)
        sig = jax.nn.sigmoid(gc[...])
        _fill_halo(g_scr, vp[...] * jax.nn.sigmoid(gp[...]), vc[...] * sig, vn[...] * jax.nn.sigmoid(gn[...]), tm, i, n_i)
        _fill_halo(d_scr, dp[...], dc[...], dn[...], tm, i, n_i)

        @pl.when(i == 0)
        def _():
            gf_ref[...] = jnp.zeros_like(gf_ref)
            db_ref[...] = jnp.zeros_like(db_ref)

        d_cur = dc[...]
        dglu = jnp.zeros((tm, LANES), F32)
        for j in range(CONV_WIDTH):
            dglu = dglu + w_ref[j:j + 1, :] * d_scr[pl.ds(CONV_HALO + pad - j, tm), :]
            gf_ref[j:j + 1, :] += jnp.sum(d_cur * g_scr[pl.ds(CONV_HALO - pad + j, tm), :], axis=0, keepdims=True)
        db_ref[...] += jnp.sum(d_cur, axis=0, keepdims=True)
        dz_ref[0] = dglu * sig
        dz_ref[1] = dglu * vc[...] * sig * (1.0 - sig)

        @pl.when(i == n_i - 1)
        def _():
            gb_ref[...] = gf_ref[...].astype(BF16)

    w_spec = pl.BlockSpec((None, CONV_WIDTH, LANES), lambda j, i: (j, 0, 0))
    return pl.pallas_call(
        body, name=name, grid=(4, n_i),
        in_specs=_halo_specs(tm, S // CONV_HALO, 8) + _halo_specs(tm, S // CONV_HALO, 12)
        + _halo_specs(tm, S // CONV_HALO, 0) + [w_spec, ANY],
        out_specs=[pl.BlockSpec((2, tm, LANES), lambda j, i: (1, i, j)),
                   w_spec, w_spec, pl.BlockSpec((1, LANES), lambda j, i: (0, j))],
        out_shape=[jax.ShapeDtypeStruct(dz.shape, F32), jax.ShapeDtypeStruct(cw.shape, F32),
                   jax.ShapeDtypeStruct(cw.shape, BF16), jax.ShapeDtypeStruct((1, 4 * LANES), F32)],
        input_output_aliases={10: 0},
        scratch_shapes=[pltpu.VMEM((tm + 2 * CONV_HALO, LANES), F32)] * 2,
        compiler_params=_cp("parallel", "arbitrary"),
    )(z, z, z, z, z, z, dgconv, dgconv, dgconv, cw, dz)


def _seg_matrix(scale):
    r = lax.broadcasted_iota(jnp.int32, (LANES, LANES), 0)
    c = lax.broadcasted_iota(jnp.int32, (LANES, LANES), 1)
    return jnp.where(lax.shift_right_logical(r, 6) == lax.shift_right_logical(c, 6), scale, 0.0).astype(BF16)


def _seg_sum(x, seg):
    hi = x.astype(BF16)
    lo = (x - hi.astype(F32)).astype(BF16)
    return _dot(hi, seg, 1, 0) + _dot(lo, seg, 1, 0)


def _rope_tables(S):
    pos = jnp.arange(S, dtype=F32)
    inv_freq = ROPE_THETA ** (-jnp.arange(0, ROT_DIM, 2, dtype=F32) / ROT_DIM)
    ang = pos[:, None] * inv_freq[None, :]
    cos, sin = jnp.cos(ang), jnp.sin(ang)
    half = ROT_DIM // 2
    rest = HEAD_DIM - ROT_DIM
    one, zero = jnp.ones((S, rest), F32), jnp.zeros((S, rest), F32)
    zh = jnp.zeros((S, half), F32)
    c = jnp.concatenate([cos, cos, one], axis=1)
    sa = jnp.concatenate([-sin, zh, zero], axis=1)
    sb = jnp.concatenate([zh, sin, zero], axis=1)
    return [jnp.tile(t, (1, 2)) for t in (c, sa, sb)]


def _qk_fwd(name, qkv, gq, gk, tables):
    S = qkv.shape[0]
    W = N_HEADS * HEAD_DIM
    tm = _tile(S, 256)
    half = ROT_DIM // 2

    def body(q_ref, k_ref, gq_ref, gk_ref, c_ref, sa_ref, sb_ref, qn_ref, kn_ref):
        seg = _seg_matrix(1.0 / HEAD_DIM)
        c, sa, sb = c_ref[...], sa_ref[...], sb_ref[...]
        for t_ref, g_ref, o_ref in ((q_ref, gq_ref, qn_ref), (k_ref, gk_ref, kn_ref)):
            for blk in range(W // LANES):
                cols = slice(blk * LANES, (blk + 1) * LANES)
                t = t_ref[:, cols]
                y = t * lax.rsqrt(_seg_sum(t * t, seg) + EPS) * g_ref[...]
                o_ref[:, cols] = y * c + pltpu.roll(y, LANES - half, 1) * sa + pltpu.roll(y, half, 1) * sb

    row = lambda k: pl.BlockSpec((tm, W), lambda m: (m, k))
    gain = pl.BlockSpec((1, LANES), lambda m: (0, 0))
    tab = pl.BlockSpec((tm, LANES), lambda m: (m, 0))
    return pl.pallas_call(
        body, name=name, grid=(S // tm,),
        in_specs=[row(0), row(1), gain, gain, tab, tab, tab], out_specs=[row(0)] * 2,
        out_shape=[jax.ShapeDtypeStruct((S, W), F32)] * 2, compiler_params=_cp("parallel"),
    )(qkv, qkv, gq, gk, *tables)


def _qk_bwd(name, qkv, gq, gk, tables, dqs, dks, dvs):
    S = qkv.shape[0]
    W = N_HEADS * HEAD_DIM
    tm = _tile(S, 256)
    half = ROT_DIM // 2
    n_p = len(dqs)

    def body(q_ref, k_ref, gq_ref, gk_ref, c_ref, sa_ref, sb_ref, *rest):
        dq_refs, dk_refs, dv_refs = rest[:n_p], rest[n_p:2 * n_p], rest[2 * n_p:3 * n_p]
        dqkv_ref, dgq_ref, dgk_ref = rest[3 * n_p:]

        @pl.when(pl.program_id(0) == 0)
        def _():
            dgq_ref[...] = jnp.zeros_like(dgq_ref)
            dgk_ref[...] = jnp.zeros_like(dgk_ref)

        seg = _seg_matrix(1.0 / HEAD_DIM)
        r_i = lax.broadcasted_iota(jnp.int32, (LANES, LANES), 0)
        c_i = lax.broadcasted_iota(jnp.int32, (LANES, LANES), 1)
        same_dim = jnp.where((r_i & (HEAD_DIM - 1)) == (c_i & (HEAD_DIM - 1)), 1.0, 0.0).astype(BF16)
        c, sa, sb = c_ref[...], sa_ref[...], sb_ref[...]
        for idx, (t_ref, g_ref, d_refs, dg_ref) in enumerate(((q_ref, gq_ref, dq_refs, dgq_ref),
                                                              (k_ref, gk_ref, dk_refs, dgk_ref))):
            dg = jnp.zeros((1, LANES), F32)
            for blk in range(W // LANES):
                cols = slice(blk * LANES, (blk + 1) * LANES)
                dout = d_refs[0][:, cols]
                for r in d_refs[1:]:
                    dout = dout + r[:, cols]
                dy = dout * c + pltpu.roll(dout * sa, half, 1) + pltpu.roll(dout * sb, LANES - half, 1)
                t = t_ref[:, cols]
                r_ = lax.rsqrt(_seg_sum(t * t, seg) + EPS)
                xhat = t * r_
                dg = dg + jnp.sum(dy * xhat, axis=0, keepdims=True)
                dxhat = dy * g_ref[...]
                dt = r_ * (dxhat - xhat * _seg_sum(dxhat * xhat, seg))
                dqkv_ref[:, idx * W + blk * LANES: idx * W + (blk + 1) * LANES] = dt.astype(BF16)
            dg_ref[...] += _seg_sum(jnp.broadcast_to(dg, (8, LANES)), same_dim)[0:1]
        dv = dv_refs[0][...]
        for r in dv_refs[1:]:
            dv = dv + r[...]
        dqkv_ref[:, 2 * W:] = dv.astype(BF16)

    row = lambda k: pl.BlockSpec((tm, W), lambda m: (m, k))
    gain = pl.BlockSpec((1, LANES), lambda m: (0, 0))
    tab = pl.BlockSpec((tm, LANES), lambda m: (m, 0))
    return pl.pallas_call(
        body, name=name, grid=(S // tm,),
        in_specs=[row(0), row(1), gain, gain, tab, tab, tab] + [row(0)] * (3 * n_p),
        out_specs=[pl.BlockSpec((tm, 3 * W), lambda m: (m, 0)), gain, gain],
        out_shape=[jax.ShapeDtypeStruct((S, 3 * W), BF16), jax.ShapeDtypeStruct((1, LANES), F32),
                   jax.ShapeDtypeStruct((1, LANES), F32)],
        compiler_params=_cp("arbitrary"),
    )(qkv, qkv, gq, gk, *tables, *dqs, *dks, *dvs)


ATTN_BQ = 2 * BAND
ATTN_ROWS = 16 * ATTN_BQ
V_COL = 2 * N_HEADS * HEAD_DIM // LANES


def _attn_geometry(S, d):
    rows = min(ATTN_ROWS, S)
    halo = BAND * d
    assert rows % (ATTN_BQ * d) == 0 and S % rows == 0, (S, d)
    return rows, halo, rows // (ATTN_BQ * d)


def _attn_specs(S, d, col):
    rows, halo, _ = _attn_geometry(S, d)
    r = rows // halo
    n_h = S // halo
    prev = pl.BlockSpec((halo, LANES), lambda j, i: (jnp.maximum(i * r - 1, 0), col + j))
    cur = pl.BlockSpec((rows, LANES), lambda j, i: (i, col + j))
    nxt = pl.BlockSpec((halo, LANES), lambda j, i: (jnp.minimum((i + 1) * r, n_h - 1), col + j))
    return [prev, cur, nxt]


def _fill_window(scr, prev, cur, nxt, rows, halo):
    scr[0:halo, :] = prev[...]
    scr[halo:halo + rows, :] = cur[...]
    scr[halo + rows:2 * halo + rows, :] = nxt[...]


ATTN_CHAINS = 2


def _chain_groups(n_sb, d):
    chains = [(sb, r) for sb in range(n_sb) for r in range(d)]
    return [chains[j:j + ATTN_CHAINS] for j in range(0, len(chains), ATTN_CHAINS)]


def _strided(ref, start, size, d):
    return ref[pl.ds(start, size, stride=d) if d > 1 else pl.ds(start, size), :]


def _band_mask(i, S, d, sb):
    rows, _, _ = _attn_geometry(S, d)
    L = S // d
    base = i * (rows // d) + sb * ATTN_BQ
    wk = ATTN_BQ + 2 * BAND
    row = lax.broadcasted_iota(jnp.int32, (ATTN_BQ, wk), 0)
    col = lax.broadcasted_iota(jnp.int32, (ATTN_BQ, wk), 1)
    lj = base - BAND + col
    return (jnp.abs(col - BAND - row) <= BAND) & (lj >= 0) & (lj < L)


def _attn_fwd(name, q, k, v, v_col, d):
    S, W = q.shape
    rows, halo, n_sb = _attn_geometry(S, d)
    wk = ATTN_BQ + 2 * BAND
    scale = HEAD_DIM ** -0.5

    def body(q_ref, kp, kc, kn, vp, vc, vn, o_ref, lse_ref, kw, vw):
        i = pl.program_id(1)
        _fill_window(kw, kp, kc, kn, rows, halo)
        _fill_window(vw, vp, vc, vn, rows, halo)
        first = _first_head((ATTN_BQ, LANES))
        heads = (first, jnp.logical_not(first))
        zero = jnp.zeros((), BF16)
        for group in _chain_groups(n_sb, d):
            masks = {sb: _band_mask(i, S, d, sb) for sb in sorted({sb for sb, _ in group})}
            starts = [r + d * sb * ATTN_BQ for sb, r in group]
            qs = [_strided(q_ref, st, ATTN_BQ, d).astype(BF16) for st in starts]
            ks = [_strided(kw, st, wk, d).astype(BF16) for st in starts]
            vs = [_strided(vw, st, wk, d).astype(BF16) for st in starts]
            s_all = [[_dot(jnp.where(hm, qv, zero), kv, 1, 1) for hm in heads] for qv, kv in zip(qs, ks)]
            p_all, den_all, lse_all = [], [], []
            for (sb, _), s_h in zip(group, s_all):
                s_h = [jnp.where(masks[sb], s * scale, NEG) for s in s_h]
                mx_h = [jnp.max(s, axis=-1, keepdims=True) for s in s_h]
                p_h = [jnp.exp(s - mx) for s, mx in zip(s_h, mx_h)]
                den_h = [jnp.sum(p, axis=-1, keepdims=True) for p in p_h]
                p_all.append([p.astype(BF16) for p in p_h])
                den_all.append(den_h)
                lse_all.append([mx + jnp.log(den) for mx, den in zip(mx_h, den_h)])
            o_all = [[_dot(p, vv, 1, 0) for p in p_h] for p_h, vv in zip(p_all, vs)]
            for st, o_h, den_h, lse_h in zip(starts, o_all, den_all, lse_all):
                dst = pl.ds(st, ATTN_BQ, stride=d) if d > 1 else pl.ds(st, ATTN_BQ)
                o_ref[dst, :] = jnp.where(first, o_h[0] / den_h[0], o_h[1] / den_h[1])
                lse_ref[dst, :] = jnp.where(first, lse_h[0], lse_h[1])

    cur = _attn_specs(S, d, 0)[1]
    return pl.pallas_call(
        body, name=name, grid=(W // LANES, S // rows),
        in_specs=[cur] + _attn_specs(S, d, 0) + _attn_specs(S, d, v_col), out_specs=[cur, cur],
        out_shape=[jax.ShapeDtypeStruct((S, W), F32)] * 2,
        scratch_shapes=[pltpu.VMEM((rows + 2 * halo, LANES), F32)] * 2,
        compiler_params=_cp("parallel", "parallel"),
    )(q, k, k, k, v, v, v)


def _attn_merge(os, lses):
    S, W = os[0].shape
    tm = _tile(S, 256)
    n_p = len(os)

    def body(*refs):
        o_refs, l_refs = refs[:n_p], refs[n_p:2 * n_p]
        o_ref, lt_ref = refs[2 * n_p:]
        ls = [r[...] for r in l_refs]
        mx = functools.reduce(jnp.maximum, ls)
        es = [jnp.exp(l - mx) for l in ls]
        den = functools.reduce(lambda a, b: a + b, es)
        acc = es[0] * o_refs[0][...]
        for e, r in zip(es[1:], o_refs[1:]):
            acc = acc + e * r[...]
        o_ref[...] = (acc / den).astype(BF16)
        lt_ref[...] = mx + jnp.log(den)

    row = pl.BlockSpec((tm, W), lambda m: (m, 0))
    return pl.pallas_call(
        body, name="attn_merge", grid=(S // tm,), in_specs=[row] * (2 * n_p), out_specs=[row, row],
        out_shape=[jax.ShapeDtypeStruct((S, W), BF16), jax.ShapeDtypeStruct((S, W), F32)],
        compiler_params=_cp("parallel"),
    )(*os, *lses)


def _attn_delta(do, o):
    S, W = do.shape
    tm = _tile(S, 256)

    def body(do_ref, o_ref, dl_ref):
        seg = _seg_matrix(1.0)
        for blk in range(W // LANES):
            cols = slice(blk * LANES, (blk + 1) * LANES)
            dl_ref[:, cols] = _seg_sum(do_ref[:, cols] * o_ref[:, cols].astype(F32), seg)

    row = pl.BlockSpec((tm, W), lambda m: (m, 0))
    return pl.pallas_call(
        body, name="attn_delta", grid=(S // tm,), in_specs=[row, row], out_specs=row,
        out_shape=jax.ShapeDtypeStruct((S, W), F32), compiler_params=_cp("parallel"),
    )(do, o)


def _attn_bwd_q(name, q, k, v, v_col, do, lse, delta, d):
    S, W = q.shape
    rows, halo, n_sb = _attn_geometry(S, d)
    wk = ATTN_BQ + 2 * BAND
    scale = HEAD_DIM ** -0.5

    def body(q_ref, do_ref, l_ref, dl_ref, kp, kc, kn, vp, vc, vn, dq_ref, kw, vw):
        i = pl.program_id(1)
        _fill_window(kw, kp, kc, kn, rows, halo)
        _fill_window(vw, vp, vc, vn, rows, halo)
        first = _first_head((ATTN_BQ, LANES))
        heads = (first, jnp.logical_not(first))
        zero = jnp.zeros((), BF16)
        wide = lambda t: jnp.concatenate([t] * (wk // LANES), axis=1)
        for group in _chain_groups(n_sb, d):
            masks = {sb: _band_mask(i, S, d, sb) for sb in sorted({sb for sb, _ in group})}
            starts = [r + d * sb * ATTN_BQ for sb, r in group]
            qs = [_strided(q_ref, st, ATTN_BQ, d).astype(BF16) for st in starts]
            dos = [_strided(do_ref, st, ATTN_BQ, d).astype(BF16) for st in starts]
            ks = [_strided(kw, st, wk, d).astype(BF16) for st in starts]
            vs = [_strided(vw, st, wk, d).astype(BF16) for st in starts]
            s_all = [[_dot(jnp.where(hm, qv, zero), kv, 1, 1) for hm in heads] for qv, kv in zip(qs, ks)]
            dp_all = [[_dot(jnp.where(hm, dov, zero), vv, 1, 1) for hm in heads] for dov, vv in zip(dos, vs)]
            ds_all = []
            for (sb, _), st, s_h, dp_h in zip(group, starts, s_all, dp_all):
                lv, dlv = _strided(l_ref, st, ATTN_BQ, d), _strided(dl_ref, st, ATTN_BQ, d)
                l_sw, dl_sw = pltpu.roll(lv, HEAD_DIM, 1), pltpu.roll(dlv, HEAD_DIM, 1)
                ds_h = []
                for hm, s, dp in zip(heads, s_h, dp_h):
                    p = jnp.exp(jnp.where(masks[sb], s * scale, NEG) - wide(jnp.where(hm, lv, l_sw)))
                    ds_h.append((p * (dp - wide(jnp.where(hm, dlv, dl_sw))) * scale).astype(BF16))
                ds_all.append(ds_h)
            dq_all = [[_dot(ds, kv, 1, 0) for ds in ds_h] for ds_h, kv in zip(ds_all, ks)]
            for st, dq_h in zip(starts, dq_all):
                dst = pl.ds(st, ATTN_BQ, stride=d) if d > 1 else pl.ds(st, ATTN_BQ)
                dq_ref[dst, :] = jnp.where(first, dq_h[0], dq_h[1])

    cur = _attn_specs(S, d, 0)[1]
    return pl.pallas_call(
        body, name=name, grid=(W // LANES, S // rows),
        in_specs=[cur] * 4 + _attn_specs(S, d, 0) + _attn_specs(S, d, v_col), out_specs=cur,
        out_shape=jax.ShapeDtypeStruct((S, W), F32),
        scratch_shapes=[pltpu.VMEM((rows + 2 * halo, LANES), F32)] * 2,
        compiler_params=_cp("parallel", "parallel"),
    )(q, do, lse, delta, k, k, k, v, v, v)


def _attn_bwd_kv(name, q, k, v, v_col, do, lse, delta, d):
    S, W = q.shape
    rows, halo, n_sb = _attn_geometry(S, d)
    wk = ATTN_BQ + 2 * BAND
    scale = HEAD_DIM ** -0.5

    def body(k_ref, v_ref, qp, qc, qn, dop, doc, don, lp, lc, ln, dlp, dlc, dln, dk_ref, dv_ref, qw, dow, lw, dlw):
        i = pl.program_id(1)
        _fill_window(qw, qp, qc, qn, rows, halo)
        _fill_window(dow, dop, doc, don, rows, halo)
        _fill_window(lw, lp, lc, ln, rows, halo)
        _fill_window(dlw, dlp, dlc, dln, rows, halo)
        first = _first_head((ATTN_BQ, LANES))
        heads = (first, jnp.logical_not(first))
        zero = jnp.zeros((), BF16)
        for group in _chain_groups(n_sb, d):
            masks = {sb: _band_mask(i, S, d, sb) for sb in sorted({sb for sb, _ in group})}
            starts = [r + d * sb * ATTN_BQ for sb, r in group]
            ks = [_strided(k_ref, st, ATTN_BQ, d).astype(BF16) for st in starts]
            vs = [_strided(v_ref, st, ATTN_BQ, d).astype(BF16) for st in starts]
            qs = [_strided(qw, st, wk, d).astype(BF16) for st in starts]
            dos = [_strided(dow, st, wk, d).astype(BF16) for st in starts]
            s_all = [[_dot(jnp.where(hm, kv, zero), qv, 1, 1) for hm in heads] for kv, qv in zip(ks, qs)]
            dp_all = [[_dot(jnp.where(hm, vv, zero), dov, 1, 1) for hm in heads] for vv, dov in zip(vs, dos)]
            p_all, ds_all = [], []
            for (sb, _), st, s_h, dp_h in zip(group, starts, s_all, dp_all):
                l_t, dl_t = _strided(lw, st, wk, d).T, _strided(dlw, st, wk, d).T
                p_h = [jnp.exp(jnp.where(masks[sb], s * scale, NEG) - l_t[hh * HEAD_DIM:hh * HEAD_DIM + 1, :])
                       for hh, s in enumerate(s_h)]
                ds_all.append([(p * (dp - dl_t[hh * HEAD_DIM:hh * HEAD_DIM + 1, :]) * scale).astype(BF16)
                               for hh, (p, dp) in enumerate(zip(p_h, dp_h))])
                p_all.append([p.astype(BF16) for p in p_h])
            dv_all = [[_dot(p, dov, 1, 0) for p in p_h] for p_h, dov in zip(p_all, dos)]
            dk_all = [[_dot(ds, qv, 1, 0) for ds in ds_h] for ds_h, qv in zip(ds_all, qs)]
            for st, dk_h, dv_h in zip(starts, dk_all, dv_all):
                dst = pl.ds(st, ATTN_BQ, stride=d) if d > 1 else pl.ds(st, ATTN_BQ)
                dk_ref[dst, :] = jnp.where(first, dk_h[0], dk_h[1])
                dv_ref[dst, :] = jnp.where(first, dv_h[0], dv_h[1])

    cur = _attn_specs(S, d, 0)[1]
    win = _attn_specs(S, d, 0)
    return pl.pallas_call(
        body, name=name, grid=(W // LANES, S // rows),
        in_specs=[cur, _attn_specs(S, d, v_col)[1]] + win * 4, out_specs=[cur, cur],
        out_shape=[jax.ShapeDtypeStruct((S, W), F32)] * 2,
        scratch_shapes=[pltpu.VMEM((rows + 2 * halo, LANES), F32)] * 4,
        compiler_params=_cp("parallel", "parallel"),
    )(k, v, q, q, q, do, do, do, lse, lse, lse, delta, delta, delta)


def _place():
    x, y, c = lax.axis_index("x"), lax.axis_index("y"), lax.axis_index("c")
    chips = [(1 - x, y), (x, 1 - y), (1 - x, 1 - y)]
    return x, y, c, chips


HBM = pl.BlockSpec(memory_space=pltpu.HBM)
SEM = pl.BlockSpec(memory_space=pltpu.SEMAPHORE)
DATAFLOW = pltpu.SideEffectType.DATAFLOW_SIDE_EFFECTING


def _exchange_copies(kind, srcs, dsts, send_sems, recv_sems):
    x, y, c, chips = _place()
    mine = 2 * x + y
    cps = []
    for t in range(len(srcs)):
        for k, (px, py) in enumerate(chips):
            src = srcs[t] if kind == "gather" else srcs[t].at[2 * px + py]
            dst = dsts[t].at[mine] if kind == "gather" else dsts[t].at[k]
            cps.append(pltpu.make_async_remote_copy(src_ref=src, dst_ref=dst, send_sem=send_sems.at[3 * t + k],
                                                    recv_sem=recv_sems.at[3 * t + k], device_id=(px, py, c), device_id_type=MESH))
    return cps


def _exchange_start(name, kind, groups):
    sizes = [len(g) for g in groups]
    n, n_g = sum(sizes), len(groups)

    def body(*refs):
        srcs, dsts = refs[:n], refs[n:2 * n]
        sems = refs[2 * n:2 * n + 2 * n_g]
        token = refs[4 * n + 2 * n_g]
        off = 0
        for gi, size in enumerate(sizes):
            for cp in _exchange_copies(kind, srcs[off:off + size], dsts[off:off + size], sems[2 * gi], sems[2 * gi + 1]):
                cp.start()
            off += size
        token[...] = jnp.zeros_like(token)

    arrays = [pltpu.with_memory_space_constraint(a, pltpu.HBM) for a in
              [s for g in groups for s, _ in g] + [d for g in groups for _, d in g]]
    sem_shapes = []
    for size in sizes:
        sem_shapes += [pltpu.SemaphoreType.DMA((3 * size,))] * 2
    outs = pl.pallas_call(
        body, name=name,
        in_specs=[HBM] * (2 * n),
        out_specs=[SEM] * (2 * n_g) + [HBM] * (2 * n) + [pl.BlockSpec(memory_space=pltpu.VMEM)],
        out_shape=sem_shapes + [pltpu.HBM(a.shape, a.dtype) for a in arrays] + [jax.ShapeDtypeStruct((8, LANES), F32)],
        input_output_aliases={t: 2 * n_g + t for t in range(2 * n)},
        compiler_params=pltpu.CompilerParams(has_side_effects=DATAFLOW),
    )(*arrays)
    sems, thru, token = outs[:2 * n_g], outs[2 * n_g:-1], outs[-1]
    handles, off = [], 0
    for gi, size in enumerate(sizes):
        handles.append((sems[2 * gi], sems[2 * gi + 1], thru[off:off + size], thru[n + off:n + off + size]))
        off += size
    return handles, token


def _exchange_wait(name, kind, handle, after):
    send_sems, recv_sems, srcs, dsts = handle
    n = len(srcs)

    def body(*refs):
        for cp in _exchange_copies(kind, refs[:n], refs[n:2 * n], refs[2 * n], refs[2 * n + 1]):
            cp.wait_send()
            cp.wait_recv()

    outs = pl.pallas_call(
        body, name=name,
        in_specs=[HBM] * (2 * n) + [SEM, SEM, ANY], out_specs=[HBM] * (2 * n),
        out_shape=[pltpu.HBM(a.shape, a.dtype) for a in (*srcs, *dsts)],
        input_output_aliases={t: t for t in range(2 * n)},
        compiler_params=pltpu.CompilerParams(has_side_effects=DATAFLOW),
    )(*srcs, *dsts, send_sems, recv_sems, after)
    return outs[n:]


def _prepare_shard(name, w, idx, dtype, mine):
    _, R, C = w.shape
    tr = _row_tile(R)

    def body(mine_ref, w_ref, src_ref, land_ref):
        val = w_ref[...].astype(dtype)
        src_ref[...] = val
        land_ref[...] = val

    return pl.pallas_call(
        body, name=name,
        grid_spec=pltpu.PrefetchScalarGridSpec(
            num_scalar_prefetch=1, grid=(R // tr,),
            in_specs=[pl.BlockSpec((None, tr, C), lambda i, s: (idx, i, 0))],
            out_specs=[pl.BlockSpec((tr, C), lambda i, s: (i, 0)), pl.BlockSpec((None, tr, C), lambda i, s: (s[0], i, 0))]),
        out_shape=[jax.ShapeDtypeStruct((R, C), dtype), jax.ShapeDtypeStruct((N_SHARDS, R, C), dtype)],
        compiler_params=_cp("parallel"),
    )(mine, w)


def _swap_with_sibling(parts):
    n = len(parts)

    def body(*refs):
        ins, outs = refs[:n], refs[n:2 * n]
        send_sems, recv_sems = refs[2 * n:]
        x, y, c, _ = _place()
        cps = [pltpu.make_async_remote_copy(src_ref=ins[t], dst_ref=outs[t], send_sem=send_sems.at[t], recv_sem=recv_sems.at[t],
                                            device_id=(x, y, 1 - c), device_id_type=MESH) for t in range(n)]
        for cp in cps:
            cp.start()
        for cp in cps:
            cp.wait_recv()
        for cp in cps:
            cp.wait_send()

    return pl.pallas_call(
        body, name="swap_partial_grads", in_specs=[ANY] * n, out_specs=[ANY] * n,
        out_shape=[jax.ShapeDtypeStruct(p.shape, p.dtype) for p in parts],
        scratch_shapes=[pltpu.SemaphoreType.DMA((n,)), pltpu.SemaphoreType.DMA((n,))],
    )(*parts)


def _allreduce_small(v):
    rows = v.shape[0]

    def body(v_ref, out_ref, buf, send_sems, recv_sems):
        x, y, c, chips = _place()
        me, sibling = (x, y, c), (x, y, 1 - c)

        def slot(px, py, pc):
            return buf.at[4 * px + 2 * py + pc]

        def copy(k, block, to, src=None):
            return pltpu.make_async_remote_copy(
                src_ref=slot(*block) if src is None else src, dst_ref=slot(*block), send_sem=send_sems.at[k],
                recv_sem=recv_sems.at[k], device_id=to, device_id_type=MESH)

        slot(*me)[...] = v_ref[...]
        first = [copy(0, me, sibling, src=v_ref)] + [copy(1 + j, me, (*chip, c), src=v_ref) for j, chip in enumerate(chips)]
        for cp in first:
            cp.start()
        passed = [copy(4 + j, (*chip, c), sibling) for j, chip in enumerate(chips)]
        for j, chip in enumerate(chips):
            copy(1 + j, (*chip, c), me).wait_recv()
            passed[j].start()
        copy(0, sibling, me).wait_recv()
        for j, chip in enumerate(chips):
            copy(4 + j, (*chip, 1 - c), me).wait_recv()
        for cp in first + passed:
            cp.wait_send()
        acc = buf[0]
        for k in range(1, 8):
            acc = acc + buf[k]
        out_ref[...] = acc

    return pl.pallas_call(
        body, name="allreduce_small_grads",
        in_specs=[pl.BlockSpec(memory_space=pltpu.VMEM)], out_specs=pl.BlockSpec(memory_space=pltpu.VMEM),
        out_shape=jax.ShapeDtypeStruct((rows, LANES), F32),
        scratch_shapes=[pltpu.VMEM((8, rows, LANES), F32), pltpu.SemaphoreType.DMA((7,)), pltpu.SemaphoreType.DMA((7,))],
        compiler_params=pltpu.CompilerParams(vmem_limit_bytes=VMEM_LIMIT_BYTES),
    )(v)


MM_TM_K = 512
WGRAD_TM = 2048


def _rows_merged(w):
    return w.reshape(1, w.shape[0] * w.shape[1], w.shape[2])


def _sq_relu_epilogue(acc):
    r = jnp.maximum(acc, 0.0)
    return acc, r * r


def _add_epilogue(acc, x):
    return (acc + x,)


def _add_norm_epilogue(acc, x, g):
    y = acc + x
    r = lax.rsqrt(jnp.mean(y * y, axis=-1, keepdims=True) + EPS)
    return y, y * r * g


def _norm_bwd_epilogue(dh, x, dres, g):
    r = lax.rsqrt(jnp.mean(x * x, axis=-1, keepdims=True) + EPS)
    xhat = x * r
    dxhat = dh * g
    dx = dres + r * (dxhat - xhat * jnp.mean(dxhat * xhat, axis=-1, keepdims=True))
    return dx, jnp.sum(dh * xhat, axis=0, keepdims=True)


def _sq_relu_grad_epilogue(acc, a):
    return (acc * (2.0 * jnp.maximum(a.astype(F32), 0.0)),)


STAGES = ("mixer", "mlp")


def _stage_tensors(layer, stage):
    i = layer // 2
    if stage == "mlp":
        return [("mlp_w1", layer), ("mlp_w2", layer)]
    return [("ab_w_in", i), ("b_conv_w", i), ("ab_w_out", i)] if layer % 2 == 0 else [("c_w_qkv", i), ("c_w_out", i)]


def _local_step(x, target, p, weights_of, grads_done):
    S, D = x.shape
    depth = p["mix_norm_g"].shape[0]
    n_even = (depth + 1) // 2
    mix_g3 = p["mix_norm_g"].reshape(depth, 1, D)
    mlp_g3 = p["mlp_norm_g"].reshape(depth, 1, D)
    vec3 = lambda t: t.reshape(t.shape[0], 1, t.shape[1])
    spw16 = p["a_spatial_w"].astype(BF16)
    spw16_t = jnp.swapaxes(spw16, 2, 3)
    bias_full = jnp.repeat(jnp.swapaxes(p["a_spatial_b"], 1, 2), HEAD_DIM, axis=2)
    vn_g, vn_b, cn_g, cn_b, cb3 = (vec3(p[k]) for k in ("a_vnorm_g", "a_vnorm_b", "b_norm_g", "b_norm_b", "b_conv_b"))
    tables = _rope_tables(S)
    gq = jnp.tile(p["c_q_norm_g"], (1, 2))
    gk = jnp.tile(p["c_k_norm_g"], (1, 2))

    saved = []
    h = _rms_fwd("mix_norm_0", x, mix_g3, 0)
    for layer in range(depth):
        i = layer // 2
        wl = dict(weights_of(layer, "mixer", x))
        rec = {"x_mix": x, "w": wl, "h_mix": h}
        if layer % 2 == 0:
            (z,) = _mm_ngroup(f"ab_in_{layer}", h, wl["ab_w_in"], nt=False, tm=MM_TM_K, out_dtypes=[F32])
            gconv = _glu_conv_fwd(f"glu_conv_{layer}", z, wl["b_conv_w"], cb3, i)
            cat = _ab_tail_fwd(f"ab_tail_{layer}", z, gconv, spw16, bias_full, vn_g, vn_b, cn_g, cn_b, i)
            x, h = _mm_kgroup(f"ab_out_{layer}", cat, _rows_merged(wl["ab_w_out"]), nt=False, tm=MM_TM_K,
                              out_dtypes=[F32, BF16], extras=(x,), vecs=[(mlp_g3, layer)], epilogue=_add_norm_epilogue)
            rec.update(z=z, gconv=gconv, cat=cat)
        else:
            (qkv,) = _mm_ngroup(f"c_qkv_{layer}", h, wl["c_w_qkv"], nt=False, tm=MM_TM_K, out_dtypes=[F32])
            qn, kn = _qk_fwd(f"qk_norm_rope_{layer}", qkv, gq[i:i + 1], gk[i:i + 1], tables)
            os, lses = zip(*[_attn_fwd(f"attn_d{d}_{layer}", qn, kn, qkv, V_COL, d) for d in PATTERN_DILATIONS])
            o, lse = _attn_merge(os, lses)
            x, h = _mm_kgroup(f"c_out_{layer}", o, _rows_merged(wl["c_w_out"]), nt=False, tm=MM_TM_K,
                              out_dtypes=[F32, BF16], extras=(x,), vecs=[(mlp_g3, layer)], epilogue=_add_norm_epilogue)
            rec.update(qkv=qkv, qn=qn, kn=kn, o=o, lse=lse)
        rec["x_mlp"] = x
        wl.update(weights_of(layer, "mlp", x))
        a, hsq = _mm_ngroup(f"mlp_up_{layer}", h, wl["mlp_w1"], nt=False, tm=MM_TM_K, out_dtypes=[BF16, BF16],
                            epilogue=_sq_relu_epilogue)
        rec.update(h_mlp=h, a=a, hsq=hsq)
        if layer + 1 < depth:
            x, h = _mm_kgroup(f"mlp_down_{layer}", hsq, _rows_merged(wl["mlp_w2"]), nt=False, tm=MM_TM_K,
                              out_dtypes=[F32, BF16], extras=(x,), vecs=[(mix_g3, layer + 1)], epilogue=_add_norm_epilogue)
        else:
            (x,) = _mm_kgroup(f"mlp_down_{layer}", hsq, _rows_merged(wl["mlp_w2"]), nt=False, tm=MM_TM_K, out_dtypes=[F32],
                              extras=(x,), epilogue=_add_epilogue)
        saved.append(rec)

    dx, loss_row = _loss_grad(x, target)

    small = {k: [None] * v.shape[0] for k, v in p.items()}
    token = None
    for layer in reversed(range(depth)):
        i = layer // 2
        rec = saved[layer]
        wl = rec["w"]
        g = {}
        (da,) = _mm_ngroup(f"mlp_down_dgrad_{layer}", dx, wl["mlp_w2"], nt=True, tm=MM_TM_K, out_dtypes=[BF16],
                           extras=(rec["a"],), epilogue=_sq_relu_grad_epilogue, anchor=token)
        g["mlp_w2"] = _wgrad(f"mlp_down_wgrad_{layer}", rec["hsq"], dx, wl["mlp_w2"].shape, a_group=True, tm=WGRAD_TM)
        g["mlp_w1"] = _wgrad(f"mlp_up_wgrad_{layer}", rec["h_mlp"], da, wl["mlp_w1"].shape, a_group=False, tm=WGRAD_TM)
        dx, small["mlp_norm_g"][layer] = _mm_kgroup(
            f"mlp_up_dgrad_{layer}", da, wl["mlp_w1"], nt=True, tm=MM_TM_K, out_dtypes=[F32], extras=(rec["x_mlp"], dx),
            vecs=[(mlp_g3, layer)], n_sums=1, epilogue=_norm_bwd_epilogue)
        token = grads_done(layer, "mlp", g)
        g = {}
        if layer % 2 == 0:
            w_out = _rows_merged(wl["ab_w_out"])
            (dcat,) = _mm_ngroup(f"ab_out_dgrad_{layer}", dx, w_out, nt=True, tm=MM_TM_K, out_dtypes=[F32], anchor=token)
            g["ab_w_out"] = [t.reshape(wl["ab_w_out"].shape) for t in
                             _wgrad(f"ab_out_wgrad_{layer}", rec["cat"], dx, w_out.shape, a_group=True, tm=WGRAD_TM)]
            dz, dgconv, dspw, dbias, dvg, dvb, dcg, dcb = _ab_tail_bwd(
                f"ab_tail_bwd_{layer}", rec["z"], rec["gconv"], dcat, spw16, spw16_t, bias_full, vn_g, vn_b, cn_g, cn_b, i)
            dz, gf, gb, dcbias = _glu_conv_bwd(f"glu_conv_bwd_{layer}", rec["z"], dgconv, dz, wl["b_conv_w"])
            g["b_conv_w"] = (gf, gb)
            small["a_spatial_w"][i] = dspw
            small["a_spatial_b"][i] = _fold_bias(dbias)[:, :A_GROUPS].T
            for k, val in (("a_vnorm_g", dvg), ("a_vnorm_b", dvb), ("b_norm_g", dcg), ("b_norm_b", dcb), ("b_conv_b", dcbias)):
                small[k][i] = val
            g["ab_w_in"] = _wgrad(f"ab_in_wgrad_{layer}", rec["h_mix"], dz, wl["ab_w_in"].shape, a_group=False, tm=WGRAD_TM)
            dgrad = (f"ab_in_dgrad_{layer}", dz, wl["ab_w_in"])
        else:
            w_out = _rows_merged(wl["c_w_out"])
            (do,) = _mm_ngroup(f"c_out_dgrad_{layer}", dx, w_out, nt=True, tm=MM_TM_K, out_dtypes=[F32], anchor=token)
            g["c_w_out"] = [t.reshape(wl["c_w_out"].shape) for t in
                            _wgrad(f"c_out_wgrad_{layer}", rec["o"], dx, w_out.shape, a_group=True, tm=WGRAD_TM)]
            delta = _attn_delta(do, rec["o"])
            attn_args = (rec["qn"], rec["kn"], rec["qkv"], V_COL, do, rec["lse"], delta)
            dqs = [_attn_bwd_q(f"attn_bwd_q_d{d}_{layer}", *attn_args, d) for d in PATTERN_DILATIONS]
            dks, dvs = zip(*[_attn_bwd_kv(f"attn_bwd_kv_d{d}_{layer}", *attn_args, d) for d in PATTERN_DILATIONS])
            dqkv, dgq, dgk = _qk_bwd(f"qk_norm_rope_bwd_{layer}", rec["qkv"], gq[i:i + 1], gk[i:i + 1], tables, dqs, dks, dvs)
            small["c_q_norm_g"][i] = dgq[:, :HEAD_DIM]
            small["c_k_norm_g"][i] = dgk[:, :HEAD_DIM]
            g["c_w_qkv"] = _wgrad(f"c_qkv_wgrad_{layer}", rec["h_mix"], dqkv, wl["c_w_qkv"].shape, a_group=False, tm=WGRAD_TM)
            dgrad = (f"c_qkv_dgrad_{layer}", dqkv, wl["c_w_qkv"])
        dx, small["mix_norm_g"][layer] = _mm_kgroup(
            *dgrad, nt=True, tm=MM_TM_K, out_dtypes=[F32], extras=(rec["x_mix"], dx), vecs=[(mix_g3, layer)], n_sums=1,
            epilogue=_norm_bwd_epilogue)
        token = grads_done(layer, "mixer", g)

    small = {k: jnp.stack([t.reshape(p[k].shape[1:]) for t in v]) for k, v in small.items()}
    return loss_row, dx, small


SHARDED = ("mlp_w1", "mlp_w2", "ab_w_in", "b_conv_w", "ab_w_out", "c_w_qkv", "c_w_out")
SMALL = ("mix_norm_g", "mlp_norm_g", "a_spatial_w", "a_spatial_b", "a_vnorm_g", "a_vnorm_b", "b_conv_b", "b_norm_g",
         "b_norm_b", "c_q_norm_g", "c_k_norm_g")
WEIGHTS = ("mix_norm_g", "mlp_norm_g", "mlp_w1", "mlp_w2", "ab_w_in", "a_spatial_w", "a_spatial_b", "a_vnorm_g",
           "a_vnorm_b", "b_conv_w", "b_conv_b", "b_norm_g", "b_norm_b", "ab_w_out", "c_w_qkv", "c_q_norm_g",
           "c_k_norm_g", "c_w_out")


def _pack(parts):
    flat = jnp.concatenate([parts[k].reshape(-1) for k in SMALL])
    rows = -(-flat.shape[0] // (256 * LANES)) * 256
    return jnp.pad(flat, (0, rows * LANES - flat.shape[0])).reshape(rows, LANES)


def _unpack(packed, like):
    flat = packed.reshape(-1)
    out, off = {}, 0
    for k in SMALL:
        n = like[k].size
        out[k] = flat[off:off + n].reshape(like[k].shape)
        off += n
    return out


def kernel(x, mix_norm_g, mlp_norm_g, mlp_w1, mlp_w2, ab_w_in, a_spatial_w, a_spatial_b, a_vnorm_g, a_vnorm_b, b_conv_w, b_conv_b, b_norm_g, b_norm_b, ab_w_out, c_w_qkv, c_q_norm_g, c_k_norm_g, c_w_out, loss_target, m_mix_norm_g, m_mlp_norm_g, m_mlp_w1, m_mlp_w2, m_ab_w_in, m_a_spatial_w, m_a_spatial_b, m_a_vnorm_g, m_a_vnorm_b, m_b_conv_w, m_b_conv_b, m_b_norm_g, m_b_norm_b, m_ab_w_out, m_c_w_qkv, m_c_q_norm_g, m_c_k_norm_g, m_c_w_out, v_mix_norm_g, v_mlp_norm_g, v_mlp_w1, v_mlp_w2, v_ab_w_in, v_a_spatial_w, v_a_spatial_b, v_a_vnorm_g, v_a_vnorm_b, v_b_conv_w, v_b_conv_b, v_b_norm_g, v_b_norm_b, v_ab_w_out, v_c_w_qkv, v_c_q_norm_g, v_c_k_norm_g, v_c_w_out):
    w = dict(mix_norm_g=mix_norm_g, mlp_norm_g=mlp_norm_g, mlp_w1=mlp_w1, mlp_w2=mlp_w2, ab_w_in=ab_w_in,
             a_spatial_w=a_spatial_w, a_spatial_b=a_spatial_b, a_vnorm_g=a_vnorm_g, a_vnorm_b=a_vnorm_b,
             b_conv_w=b_conv_w, b_conv_b=b_conv_b, b_norm_g=b_norm_g, b_norm_b=b_norm_b, ab_w_out=ab_w_out,
             c_w_qkv=c_w_qkv, c_q_norm_g=c_q_norm_g, c_k_norm_g=c_k_norm_g, c_w_out=c_w_out)
    m = dict(mix_norm_g=m_mix_norm_g, mlp_norm_g=m_mlp_norm_g, mlp_w1=m_mlp_w1, mlp_w2=m_mlp_w2, ab_w_in=m_ab_w_in,
             a_spatial_w=m_a_spatial_w, a_spatial_b=m_a_spatial_b, a_vnorm_g=m_a_vnorm_g, a_vnorm_b=m_a_vnorm_b,
             b_conv_w=m_b_conv_w, b_conv_b=m_b_conv_b, b_norm_g=m_b_norm_g, b_norm_b=m_b_norm_b, ab_w_out=m_ab_w_out,
             c_w_qkv=m_c_w_qkv, c_q_norm_g=m_c_q_norm_g, c_k_norm_g=m_c_k_norm_g, c_w_out=m_c_w_out)
    v = dict(mix_norm_g=v_mix_norm_g, mlp_norm_g=v_mlp_norm_g, mlp_w1=v_mlp_w1, mlp_w2=v_mlp_w2, ab_w_in=v_ab_w_in,
             a_spatial_w=v_a_spatial_w, a_spatial_b=v_a_spatial_b, a_vnorm_g=v_a_vnorm_g, a_vnorm_b=v_a_vnorm_b,
             b_conv_w=v_b_conv_w, b_conv_b=v_b_conv_b, b_norm_g=v_b_norm_g, b_norm_b=v_b_norm_b, ab_w_out=v_ab_w_out,
             c_w_qkv=v_c_w_qkv, c_q_norm_g=v_c_q_norm_g, c_k_norm_g=v_c_k_norm_g, c_w_out=v_c_w_out)

    S, D = x.shape[1], x.shape[2]
    depth = mix_norm_g.shape[0]
    mine = (2 * lax.axis_index("x") + lax.axis_index("y")).astype(jnp.int32).reshape(1)

    stages = [(layer, stage) for layer in range(depth) for stage in STAGES]
    groups = [[(k, i) + tuple(_prepare_shard(f"prepare_{k}_{i}", w[k], i, F32 if k == "b_conv_w" else BF16, mine))
               for k, i in _stage_tensors(*st)] for st in stages]
    handles, gather_token = _exchange_start("gather_weights_start", "gather", [[(s, l) for _, _, s, l in g] for g in groups])
    handles = dict(zip(stages, handles))

    def weights_of(layer, stage, after):
        got = _exchange_wait(f"gather_weights_wait_{layer}_{stage}", "gather", handles[layer, stage],
                             gather_token if (layer, stage) == stages[0] else after)
        return {k: a for (k, _), a in zip(_stage_tensors(layer, stage), got)}

    scattered = {}

    def grads_done(layer, stage, g):
        names = [k for k, _ in _stage_tensors(layer, stage)]
        group = [(g[k][1], lax.empty((3,) + g[k][1].shape[1:], BF16)) for k in names]
        (handle,), token = _exchange_start(f"scatter_grads_start_{layer}_{stage}", "scatter", [group])
        scattered[layer, stage] = (handle, [g[k][0] for k in names])
        return token

    small_params = {k: w[k] for k in SMALL}
    loss_row, dx, small_grads = _local_step(x.reshape(S, D), loss_target.reshape(S, D), small_params, weights_of, grads_done)

    loss = lax.psum(loss_row[0, 0], ("x", "y", "c"))

    partial, order = [], []
    for layer, stage in reversed(stages):
        handle, gfs = scattered[layer, stage]
        recvs = _exchange_wait(f"scatter_grads_wait_{layer}_{stage}", "scatter", handle, dx)
        tensors = _stage_tensors(layer, stage)
        partial += [_sum4(f"sum_chips_{k}_{i}", gf, r, mine) for (k, i), gf, r in zip(tensors, gfs, recvs)]
        order += tensors
    other = _swap_with_sibling(partial)
    stacked = {k: [lax.empty(w[k].shape, F32) for _ in range(4)] for k in SHARDED}
    for (k, i), a, b in zip(order, partial, other):
        stacked[k] = _adamw_layer(f"adamw_{k}_{i}", w[k], m[k], v[k], i, a, b, stacked[k])
    grads, deltas, new_m, new_v = ({k: stacked[k][j] for k in SHARDED} for j in range(4))

    g_small = _allreduce_small(_pack(small_grads))
    outs = _adamw("adamw_small", _pack(small_params), _pack({k: m[k] for k in SMALL}), _pack({k: v[k] for k in SMALL}), g_small)
    for d_, packed in zip((grads, deltas, new_m, new_v), outs):
        d_.update(_unpack(packed, small_params))

    return (loss, dx.reshape(1, S, D), *[grads[k] for k in WEIGHTS], *[deltas[k] for k in WEIGHTS],
            *[new_m[k] for k in WEIGHTS], *[new_v[k] for k in WEIGHTS])
```

```python
import functools

import jax
import jax.numpy as jnp
from jax import lax
from jax.experimental import pallas as pl
from jax.experimental.pallas import tpu as pltpu

F32, BF16 = jnp.float32, jnp.bfloat16
MESH = pl.DeviceIdType.MESH
ANY = pl.BlockSpec(memory_space=pl.ANY)

VMEM_LIMIT_BYTES = 56 * 1024 * 1024
LANES = 128
ELEMENTWISE_ROWS = 256

EPS = 1e-6
NEG = -1e30
HEAD_DIM = 64
N_HEADS = 16
CHUNK = 128
A_GROUPS = 8
CONV_WIDTH = 31
CONV_HALO = 16
CONV_CHUNK = 64
BAND = 64
PATTERN_DILATIONS = (1, 4, 16)
ROT_DIM = 16
ROPE_THETA = 500000.0
N_SHARDS = 4

ADAM_LR, ADAM_B1, ADAM_B2, ADAM_EPS, ADAM_WD, ADAM_STEP = 0.001, 0.9, 0.999, 1e-08, 0.01, 10


def _cp(*sem):
    return pltpu.CompilerParams(dimension_semantics=sem, vmem_limit_bytes=VMEM_LIMIT_BYTES)


def _tile(n, pref):
    t = min(n, pref)
    assert n % t == 0, (n, pref)
    return t


def _dot(a, b, ca, cb):
    return lax.dot_general(a, b, (((ca,), (cb,)), ((), ())), preferred_element_type=F32)


def _mm_ngroup(name, a, w, *, nt, tm, out_dtypes, extras=(), epilogue=None, anchor=None):
    M, K = a.shape
    G, R, C = w.shape
    nw = R if nt else C
    assert K == (C if nt else R)
    tm = _tile(M, tm)
    n_ex = len(extras)
    anchors = [] if anchor is None else [anchor]

    def body(a_ref, w_ref, *rest):
        rest = rest[len(anchors):]
        av = a_ref[...].astype(BF16)
        for g in range(G):
            cols = slice(g * nw, (g + 1) * nw)
            acc = _dot(av, w_ref[g], 1, 1 if nt else 0)
            res = epilogue(acc, *[e[:, cols] for e in rest[:n_ex]]) if epilogue else (acc,)
            for o_ref, r in zip(rest[n_ex:], res):
                o_ref[:, cols] = r.astype(o_ref.dtype)

    blk = pl.BlockSpec((tm, G * nw), lambda m: (m, 0))
    return pl.pallas_call(
        body, name=name, grid=(M // tm,),
        in_specs=[pl.BlockSpec((tm, K), lambda m: (m, 0)), pl.BlockSpec((G, R, C), lambda m: (0, 0, 0))]
        + [pl.BlockSpec((8, LANES), lambda m: (0, 0))] * len(anchors) + [blk] * n_ex,
        out_specs=[blk] * len(out_dtypes),
        out_shape=[jax.ShapeDtypeStruct((M, G * nw), dt) for dt in out_dtypes],
        compiler_params=_cp("parallel"),
    )(a, w, *anchors, *extras)


def _mm_kgroup(name, a, w, *, nt, tm, out_dtypes, extras=(), vecs=(), n_sums=0, epilogue=None):
    G, R, C = w.shape
    kw, N = (C, R) if nt else (R, C)
    if a.ndim == 3:
        M = a.shape[1]
        assert a.shape[0] == G and a.shape[2] == kw
    else:
        M = a.shape[0]
        assert a.shape[1] == G * kw
    tm = _tile(M, tm)
    n_ex = len(extras)
    a_spec = (pl.BlockSpec((G, tm, kw), lambda m: (0, m, 0)) if a.ndim == 3 else pl.BlockSpec((tm, G * kw), lambda m: (m, 0)))

    def body(a_ref, w_ref, *rest):
        acc = None
        for g in range(G):
            a_g = a_ref[g] if a.ndim == 3 else a_ref[:, g * kw:(g + 1) * kw]
            part = _dot(a_g.astype(BF16), w_ref[g], 1, 1 if nt else 0)
            acc = part if acc is None else acc + part
        n_in = n_ex + len(vecs)
        res = epilogue(acc, *[e[...] for e in rest[:n_in]]) if epilogue else (acc,)
        outs = rest[n_in:]
        n_tiles = len(outs) - n_sums
        for o_ref, r in zip(outs[:n_tiles], res[:n_tiles]):
            o_ref[...] = r.astype(o_ref.dtype)
        if n_sums:
            @pl.when(pl.program_id(0) == 0)
            def _():
                for s_ref in outs[n_tiles:]:
                    s_ref[...] = jnp.zeros_like(s_ref)

            for s_ref, r in zip(outs[n_tiles:], res[n_tiles:]):
                s_ref[...] += r

    blk = pl.BlockSpec((tm, N), lambda m: (m, 0))
    row = pl.BlockSpec((1, N), lambda m: (0, 0))
    return pl.pallas_call(
        body, name=name, grid=(M // tm,),
        in_specs=[a_spec, pl.BlockSpec((G, R, C), lambda m: (0, 0, 0))] + [blk] * n_ex
        + [pl.BlockSpec((None, 1, N), lambda m, i=i: (i, 0, 0)) for _, i in vecs],
        out_specs=[blk] * len(out_dtypes) + [row] * n_sums,
        out_shape=[jax.ShapeDtypeStruct((M, N), dt) for dt in out_dtypes] + [jax.ShapeDtypeStruct((1, N), F32)] * n_sums,
        compiler_params=_cp("arbitrary" if n_sums else "parallel"),
    )(a, w, *extras, *[v for v, _ in vecs])


def _wgrad(name, a, b, shape, *, a_group, tm):
    G, R, C = shape
    M = a.shape[0]
    tm = _tile(M, tm)
    n_m = M // tm

    def body(a_ref, b_ref, gf_ref, gb_ref):
        m = pl.program_id(1)
        part = _dot(a_ref[...].astype(BF16), b_ref[...].astype(BF16), 0, 0)

        @pl.when(m == 0)
        def _():
            gf_ref[...] = part

        @pl.when(m > 0)
        def _():
            gf_ref[...] += part

        @pl.when(m == n_m - 1)
        def _():
            gb_ref[...] = gf_ref[...].astype(BF16)

    a_spec = pl.BlockSpec((tm, R), (lambda g, m: (m, g)) if a_group else (lambda g, m: (m, 0)))
    if b.ndim == 3:
        assert not a_group
        b_spec = pl.BlockSpec((None, tm, C), lambda g, m: (g, m, 0))
    else:
        b_spec = pl.BlockSpec((tm, C), (lambda g, m: (m, 0)) if a_group else (lambda g, m: (m, g)))
    o_spec = pl.BlockSpec((None, R, C), lambda g, m: (g, 0, 0))
    return pl.pallas_call(
        body, name=name, grid=(G, n_m),
        in_specs=[a_spec, b_spec], out_specs=[o_spec, o_spec],
        out_shape=[jax.ShapeDtypeStruct(shape, F32), jax.ShapeDtypeStruct(shape, BF16)],
        compiler_params=_cp("parallel", "arbitrary"),
    )(a, b)


def _rms_fwd(name, x, g3, layer):
    S, D = x.shape
    tm = _tile(S, 512)

    def body(x_ref, g_ref, h_ref):
        xv = x_ref[...]
        r = lax.rsqrt(jnp.mean(xv * xv, axis=-1, keepdims=True) + EPS)
        h_ref[...] = (xv * r * g_ref[...]).astype(BF16)

    row = pl.BlockSpec((tm, D), lambda m: (m, 0))
    return pl.pallas_call(
        body, name=name, grid=(S // tm,),
        in_specs=[row, pl.BlockSpec((None, 1, D), lambda m: (layer, 0, 0))], out_specs=row,
        out_shape=jax.ShapeDtypeStruct((S, D), BF16), compiler_params=_cp("parallel"),
    )(x, g3)


def _adamw_math(w, m, v, g):
    m2 = ADAM_B1 * m + (1.0 - ADAM_B1) * g
    v2 = ADAM_B2 * v + (1.0 - ADAM_B2) * jnp.square(g)
    m_hat = m2 / (1.0 - ADAM_B1 ** ADAM_STEP)
    v_hat = v2 / (1.0 - ADAM_B2 ** ADAM_STEP)
    return g, -ADAM_LR * (m_hat / (jnp.sqrt(v_hat) + ADAM_EPS) + ADAM_WD * w), m2, v2


def _row_tile(rows):
    return _tile(rows, ELEMENTWISE_ROWS) if rows % ELEMENTWISE_ROWS == 0 else rows


def _adamw(name, w, m, v, g):
    rows, C = w.shape
    tr = _row_tile(rows)

    def body(w_ref, m_ref, v_ref, g_in, g_ref, d_ref, nm_ref, nv_ref):
        for o_ref, val in zip((g_ref, d_ref, nm_ref, nv_ref), _adamw_math(w_ref[...], m_ref[...], v_ref[...], g_in[...])):
            o_ref[...] = val

    blk = pl.BlockSpec((tr, C), lambda i: (i, 0))
    return pl.pallas_call(
        body, name=name, grid=(rows // tr,), in_specs=[blk] * 4, out_specs=[blk] * 4,
        out_shape=[jax.ShapeDtypeStruct((rows, C), F32)] * 4, compiler_params=_cp("parallel"),
    )(w, m, v, g)


def _adamw_layer(name, w, m, v, layer, mine, theirs, outs):
    _, R, C = w.shape
    tr = _row_tile(R)

    def body(w_ref, m_ref, v_ref, a_ref, b_ref, *rest):
        g = a_ref[...] + b_ref[...]
        for o_ref, val in zip(rest[4:], _adamw_math(w_ref[...], m_ref[...], v_ref[...], g)):
            o_ref[...] = val

    st = pl.BlockSpec((None, tr, C), lambda i: (layer, i, 0))
    part = pl.BlockSpec((tr, C), lambda i: (i, 0))
    return pl.pallas_call(
        body, name=name, grid=(R // tr,), in_specs=[st] * 3 + [part] * 2 + [ANY] * 4, out_specs=[st] * 4,
        out_shape=[jax.ShapeDtypeStruct(w.shape, F32)] * 4, input_output_aliases={5 + j: j for j in range(4)},
        compiler_params=_cp("parallel"),
    )(w, m, v, mine, theirs, *outs)


def _sum4(name, gf, recv, mine):
    _, R, C = gf.shape
    tr = _row_tile(R)

    def body(mine_ref, o_ref, r_ref, out_ref):
        acc = o_ref[...]
        for k in range(3):
            acc = acc + r_ref[k].astype(F32)
        out_ref[...] = acc

    return pl.pallas_call(
        body, name=name,
        grid_spec=pltpu.PrefetchScalarGridSpec(
            num_scalar_prefetch=1, grid=(R // tr,),
            in_specs=[pl.BlockSpec((None, tr, C), lambda i, s: (s[0], i, 0)), pl.BlockSpec((3, tr, C), lambda i, s: (0, i, 0))],
            out_specs=pl.BlockSpec((tr, C), lambda i, s: (i, 0))),
        out_shape=jax.ShapeDtypeStruct((R, C), F32), compiler_params=_cp("parallel"),
    )(mine, gf, recv)


def _gelu(x):
    return x * (0.5 * (1.0 + jnp.tanh(0.7978845608028654 * (x + 0.044715 * (x * x * x)))))


def _layernorm(t, g, b):
    mu = jnp.mean(t, axis=-1, keepdims=True)
    var = jnp.mean(jnp.square(t - mu), axis=-1, keepdims=True)
    return (t - mu) * lax.rsqrt(var + EPS) * g + b


def _silu(x):
    return x * jax.nn.sigmoid(x)


def _a_value(zv, g, b):
    return _layernorm(_gelu(zv), g, b)


def _b_tail(gc, g, b):
    return _silu(_layernorm(gc, g, b))


def _first_head(shape):
    return lax.broadcasted_iota(jnp.int32, shape, len(shape) - 1) < HEAD_DIM


def _spatial_mix(spw_ref, vb, tm):
    first = _first_head((CHUNK, LANES))
    rows = []
    for n in range(tm // CHUNK):
        blocks = []
        for j in range(A_GROUPS // 2):
            vblk = vb[n * CHUNK:(n + 1) * CHUNK, j * LANES:(j + 1) * LANES]
            r0 = _dot(spw_ref[2 * j], vblk, 1, 0)
            r1 = _dot(spw_ref[2 * j + 1], vblk, 1, 0)
            blocks.append(jnp.where(first, r0, r1))
        rows.append(jnp.concatenate(blocks, axis=1))
    return jnp.concatenate(rows, axis=0) if len(rows) > 1 else rows[0]


def _ab_tail_fwd(name, z, gconv, spw, bias_full, vn_g, vn_b, cn_g, cn_b, layer):
    S = z.shape[0]
    AW = 512
    tm = _tile(S, 256)

    def body(zu_ref, zv_ref, gc_ref, spw_ref, bias_ref, vg_ref, vb_ref, cg_ref, cb_ref, cat_ref):
        u = _gelu(zu_ref[...])
        v = _a_value(zv_ref[...], vg_ref[...], vb_ref[...])
        sv = _spatial_mix(spw_ref, v.astype(BF16), tm) + jnp.tile(bias_ref[...], (tm // CHUNK, 1))
        cat_ref[:, :AW] = (u * sv).astype(BF16)
        cat_ref[:, AW:] = _b_tail(gc_ref[...], cg_ref[...], cb_ref[...]).astype(BF16)

    vec = pl.BlockSpec((None, 1, AW), lambda m: (layer, 0, 0))
    return pl.pallas_call(
        body, name=name, grid=(S // tm,),
        in_specs=[pl.BlockSpec((tm, AW), lambda m: (m, 0)), pl.BlockSpec((tm, AW), lambda m: (m, 1)),
                  pl.BlockSpec((tm, AW), lambda m: (m, 0)),
                  pl.BlockSpec((None, A_GROUPS, CHUNK, CHUNK), lambda m: (layer, 0, 0, 0)),
                  pl.BlockSpec((None, CHUNK, AW), lambda m: (layer, 0, 0)), vec, vec, vec, vec],
        out_specs=pl.BlockSpec((tm, 2 * AW), lambda m: (m, 0)),
        out_shape=jax.ShapeDtypeStruct((S, 2 * AW), BF16), compiler_params=_cp("parallel"),
    )(z, z, gconv, spw, bias_full, vn_g, vn_b, cn_g, cn_b)


def _ab_tail_bwd(name, z, gconv, dcat, spw, spw_t, bias_full, vn_g, vn_b, cn_g, cn_b, layer):
    S = z.shape[0]
    AW = 512
    tm = _tile(S, 256)
    n_chunks = tm // CHUNK

    def body(zu_ref, zv_ref, gc_ref, dcat_ref, spw_ref, spwt_ref, bias_ref, vg_ref, vb_ref, cg_ref, cb_ref,
             dz_ref, dgc_ref, dspw_ref, dbias_ref, dvg_ref, dvb_ref, dcg_ref, dcb_ref):
        @pl.when(pl.program_id(0) == 0)
        def _():
            for r in (dspw_ref, dbias_ref, dvg_ref, dvb_ref, dcg_ref, dcb_ref):
                r[...] = jnp.zeros_like(r)

        dya = dcat_ref[:, :AW]
        dyb = dcat_ref[:, AW:]
        u, u_vjp = jax.vjp(_gelu, zu_ref[...])
        v, v_vjp = jax.vjp(_a_value, zv_ref[...], vg_ref[...], vb_ref[...])
        vb16 = v.astype(BF16)
        sv = _spatial_mix(spw_ref, vb16, tm) + jnp.tile(bias_ref[...], (n_chunks, 1))
        (dzu,) = u_vjp(dya * sv)
        dsv = dya * u
        dsv16 = dsv.astype(BF16)
        dv = _spatial_mix(spwt_ref, dsv16, tm)
        dzv, dvg, dvb = v_vjp(dv)
        dz_ref[0] = dzu
        dz_ref[1] = dzv
        dvg_ref[...] += dvg
        dvb_ref[...] += dvb

        first = _first_head((CHUNK, LANES))
        zero = jnp.zeros((), BF16)
        dbias = jnp.zeros((CHUNK, AW), F32)
        for n in range(n_chunks):
            rows = slice(n * CHUNK, (n + 1) * CHUNK)
            dbias = dbias + dsv[rows]
            for j in range(A_GROUPS // 2):
                cols = slice(j * LANES, (j + 1) * LANES)
                dblk, vblk = dsv16[rows, cols], vb16[rows, cols]
                dspw_ref[2 * j] += _dot(jnp.where(first, dblk, zero), vblk, 1, 1)
                dspw_ref[2 * j + 1] += _dot(jnp.where(first, zero, dblk), vblk, 1, 1)
        dbias_ref[...] += dbias

        _, t_vjp = jax.vjp(_b_tail, gc_ref[...], cg_ref[...], cb_ref[...])
        dgc, dcg, dcb = t_vjp(dyb)
        dgc_ref[...] = dgc
        dcg_ref[...] += dcg
        dcb_ref[...] += dcb

    vec = pl.BlockSpec((None, 1, AW), lambda m: (layer, 0, 0))
    spw_spec = pl.BlockSpec((None, A_GROUPS, CHUNK, CHUNK), lambda m: (layer, 0, 0, 0))
    ovec = pl.BlockSpec((1, AW), lambda m: (0, 0))
    return pl.pallas_call(
        body, name=name, grid=(S // tm,),
        in_specs=[pl.BlockSpec((tm, AW), lambda m: (m, 0)), pl.BlockSpec((tm, AW), lambda m: (m, 1)),
                  pl.BlockSpec((tm, AW), lambda m: (m, 0)), pl.BlockSpec((tm, 2 * AW), lambda m: (m, 0)),
                  spw_spec, spw_spec, pl.BlockSpec((None, CHUNK, AW), lambda m: (layer, 0, 0)), vec, vec, vec, vec],
        out_specs=[pl.BlockSpec((2, tm, AW), lambda m: (0, m, 0)), pl.BlockSpec((tm, AW), lambda m: (m, 0)),
                   pl.BlockSpec((A_GROUPS, CHUNK, CHUNK), lambda m: (0, 0, 0)),
                   pl.BlockSpec((CHUNK, AW), lambda m: (0, 0)), ovec, ovec, ovec, ovec],
        out_shape=[jax.ShapeDtypeStruct((4, S, AW), F32), jax.ShapeDtypeStruct((S, AW), F32),
                   jax.ShapeDtypeStruct((A_GROUPS, CHUNK, CHUNK), F32), jax.ShapeDtypeStruct((CHUNK, AW), F32)]
                  + [jax.ShapeDtypeStruct((1, AW), F32)] * 4,
        compiler_params=_cp("arbitrary"),
    )(z, z, gconv, dcat, spw, spw_t, bias_full, vn_g, vn_b, cn_g, cn_b)


def _fold_bias(dbias_full):
    def body(d_ref, o_ref):
        d = d_ref[...]
        hi = d.astype(BF16)
        lo = (d - hi.astype(F32)).astype(BF16)
        r = lax.broadcasted_iota(jnp.int32, (512, LANES), 0)
        c = lax.broadcasted_iota(jnp.int32, (512, LANES), 1)
        fold = jnp.where(lax.shift_right_logical(r, 6) == c, 1.0, 0.0).astype(BF16)
        o_ref[...] = _dot(hi, fold, 1, 0) + _dot(lo, fold, 1, 0)

    return pl.pallas_call(body, name="fold_spatial_bias", out_shape=jax.ShapeDtypeStruct((CHUNK, LANES), F32))(dbias_full)


def _halo_specs(tm, n_halo_blocks, col):
    r = tm // CONV_HALO
    prev = pl.BlockSpec((CONV_HALO, LANES), lambda j, i: (jnp.maximum(i * r - 1, 0), col + j))
    cur = pl.BlockSpec((tm, LANES), lambda j, i: (i, col + j))
    nxt = pl.BlockSpec((CONV_HALO, LANES), lambda j, i: (jnp.minimum((i + 1) * r, n_halo_blocks - 1), col + j))
    return [prev, cur, nxt]


def _fill_halo(scr, prev, cur, nxt, tm, i, n_i):
    scr[0:CONV_HALO, :] = jnp.where(i > 0, prev, 0.0)
    scr[CONV_HALO:CONV_HALO + tm, :] = cur
    scr[CONV_HALO + tm:2 * CONV_HALO + tm, :] = jnp.where(i < n_i - 1, nxt, 0.0)


def _glu_conv_fwd(name, z, cw, cb3, layer):
    S = z.shape[0]
    tm = _tile(S, 512)
    n_i = S // tm
    pad = CONV_WIDTH // 2

    def body(vp, vc, vn, gp, gc, gn, w_ref, b_ref, out_ref, scr):
        i = pl.program_id(1)
        glu = lambda a, b: a[...] * jax.nn.sigmoid(b[...])
        _fill_halo(scr, glu(vp, gp), glu(vc, gc), glu(vn, gn), tm, i, n_i)
        taps = [w_ref[j:j + 1, :] for j in range(CONV_WIDTH)]
        for c0 in range(0, tm, CONV_CHUNK):
            acc = jnp.zeros((CONV_CHUNK, LANES), F32) + b_ref[...]
            for j in range(CONV_WIDTH):
                acc = acc + taps[j] * scr[pl.ds(c0 + CONV_HALO - pad + j, CONV_CHUNK), :]
            out_ref[pl.ds(c0, CONV_CHUNK), :] = acc

    return pl.pallas_call(
        body, name=name, grid=(4, n_i),
        in_specs=_halo_specs(tm, S // CONV_HALO, 8) + _halo_specs(tm, S // CONV_HALO, 12)
        + [pl.BlockSpec((None, CONV_WIDTH, LANES), lambda j, i: (j, 0, 0)),
           pl.BlockSpec((None, 1, LANES), lambda j, i: (layer, 0, j))],
        out_specs=pl.BlockSpec((tm, LANES), lambda j, i: (i, j)),
        out_shape=jax.ShapeDtypeStruct((S, 4 * LANES), F32),
        scratch_shapes=[pltpu.VMEM((tm + 2 * CONV_HALO, LANES), F32)],
        compiler_params=_cp("parallel", "parallel"),
    )(z, z, z, z, z, z, cw, cb3)


def _glu_conv_bwd(name, z, dgconv, dz, cw):
    S = z.shape[0]
    tm = _tile(S, 512)
    n_i = S // tm
    pad = CONV_WIDTH // 2

    def body(vp, vc, vn, gp, gc, gn, dp, dc, dn, w_ref, dz_in, dz_ref, gf_ref, gb_ref, db_ref, g_scr, d_scr):
        i = pl.program_id(1)
        sig = jax.nn.sigmoid(gc[...])
        _fill_halo(g_scr, vp[...] * jax.nn.sigmoid(gp[...]), vc[...] * sig, vn[...] * jax.nn.sigmoid(gn[...]), tm, i, n_i)
        _fill_halo(d_scr, dp[...], dc[...], dn[...], tm, i, n_i)

        @pl.when(i == 0)
        def _():
            gf_ref[...] = jnp.zeros_like(gf_ref)
            db_ref[...] = jnp.zeros_like(db_ref)

        taps = [w_ref[j:j + 1, :] for j in range(CONV_WIDTH)]
        dw = [jnp.zeros((8, LANES), F32) for _ in range(CONV_WIDTH)]
        db = jnp.zeros((8, LANES), F32)
        fold8 = lambda t: jnp.sum(t.reshape(CONV_CHUNK // 8, 8, LANES), axis=0)
        for c0 in range(0, tm, CONV_CHUNK):
            rows = pl.ds(c0, CONV_CHUNK)
            d_cur = dc[rows, :]
            dglu = jnp.zeros((CONV_CHUNK, LANES), F32)
            for j in range(CONV_WIDTH):
                dglu = dglu + taps[j] * d_scr[pl.ds(c0 + CONV_HALO + pad - j, CONV_CHUNK), :]
                dw[j] = dw[j] + fold8(d_cur * g_scr[pl.ds(c0 + CONV_HALO - pad + j, CONV_CHUNK), :])
            db = db + fold8(d_cur)
            sig_c = jax.nn.sigmoid(gc[rows, :])
            dz_ref[0, rows, :] = dglu * sig_c
            dz_ref[1, rows, :] = dglu * vc[rows, :] * sig_c * (1.0 - sig_c)
        for j in range(CONV_WIDTH):
            gf_ref[j:j + 1, :] += jnp.sum(dw[j], axis=0, keepdims=True)
        db_ref[...] += jnp.sum(db, axis=0, keepdims=True)

        @pl.when(i == n_i - 1)
        def _():
            gb_ref[...] = gf_ref[...].astype(BF16)

    w_spec = pl.BlockSpec((None, CONV_WIDTH, LANES), lambda j, i: (j, 0, 0))
    return pl.pallas_call(
        body, name=name, grid=(4, n_i),
        in_specs=_halo_specs(tm, S // CONV_HALO, 8) + _halo_specs(tm, S // CONV_HALO, 12)
        + _halo_specs(tm, S // CONV_HALO, 0) + [w_spec, ANY],
        out_specs=[pl.BlockSpec((2, tm, LANES), lambda j, i: (1, i, j)),
                   w_spec, w_spec, pl.BlockSpec((1, LANES), lambda j, i: (0, j))],
        out_shape=[jax.ShapeDtypeStruct(dz.shape, F32), jax.ShapeDtypeStruct(cw.shape, F32),
                   jax.ShapeDtypeStruct(cw.shape, BF16), jax.ShapeDtypeStruct((1, 4 * LANES), F32)],
        input_output_aliases={10: 0},
        scratch_shapes=[pltpu.VMEM((tm + 2 * CONV_HALO, LANES), F32)] * 2,
        compiler_params=_cp("parallel", "arbitrary"),
    )(z, z, z, z, z, z, dgconv, dgconv, dgconv, cw, dz)


def _seg_matrix(scale):
    r = lax.broadcasted_iota(jnp.int32, (LANES, LANES), 0)
    c = lax.broadcasted_iota(jnp.int32, (LANES, LANES), 1)
    return jnp.where(lax.shift_right_logical(r, 6) == lax.shift_right_logical(c, 6), scale, 0.0).astype(BF16)


def _seg_sum(x, seg):
    hi = x.astype(BF16)
    lo = (x - hi.astype(F32)).astype(BF16)
    return _dot(hi, seg, 1, 0) + _dot(lo, seg, 1, 0)


def _rope_tables(S):
    pos = jnp.arange(S, dtype=F32)
    inv_freq = ROPE_THETA ** (-jnp.arange(0, ROT_DIM, 2, dtype=F32) / ROT_DIM)
    ang = pos[:, None] * inv_freq[None, :]
    cos, sin = jnp.cos(ang), jnp.sin(ang)
    half = ROT_DIM // 2
    rest = HEAD_DIM - ROT_DIM
    one, zero = jnp.ones((S, rest), F32), jnp.zeros((S, rest), F32)
    zh = jnp.zeros((S, half), F32)
    c = jnp.concatenate([cos, cos, one], axis=1)
    sa = jnp.concatenate([-sin, zh, zero], axis=1)
    sb = jnp.concatenate([zh, sin, zero], axis=1)
    return [jnp.tile(t, (1, 2)) for t in (c, sa, sb)]


def _qk_fwd(name, qkv, gq, gk, tables):
    S = qkv.shape[0]
    W = N_HEADS * HEAD_DIM
    tm = _tile(S, 256)
    half = ROT_DIM // 2

    def body(q_ref, k_ref, gq_ref, gk_ref, c_ref, sa_ref, sb_ref, qn_ref, kn_ref):
        seg = _seg_matrix(1.0 / HEAD_DIM)
        c, sa, sb = c_ref[...], sa_ref[...], sb_ref[...]
        for t_ref, g_ref, o_ref in ((q_ref, gq_ref, qn_ref), (k_ref, gk_ref, kn_ref)):
            for blk in range(W // LANES):
                cols = slice(blk * LANES, (blk + 1) * LANES)
                t = t_ref[:, cols]
                y = t * lax.rsqrt(_seg_sum(t * t, seg) + EPS) * g_ref[...]
                o_ref[:, cols] = y * c + pltpu.roll(y, LANES - half, 1) * sa + pltpu.roll(y, half, 1) * sb

    row = lambda k: pl.BlockSpec((tm, W), lambda m: (m, k))
    gain = pl.BlockSpec((1, LANES), lambda m: (0, 0))
    tab = pl.BlockSpec((tm, LANES), lambda m: (m, 0))
    return pl.pallas_call(
        body, name=name, grid=(S // tm,),
        in_specs=[row(0), row(1), gain, gain, tab, tab, tab], out_specs=[row(0)] * 2,
        out_shape=[jax.ShapeDtypeStruct((S, W), F32)] * 2, compiler_params=_cp("parallel"),
    )(qkv, qkv, gq, gk, *tables)


def _qk_bwd(name, qkv, gq, gk, tables, dqs, dks, dvs):
    S = qkv.shape[0]
    W = N_HEADS * HEAD_DIM
    tm = _tile(S, 256)
    half = ROT_DIM // 2
    n_p = len(dqs)

    def body(q_ref, k_ref, gq_ref, gk_ref, c_ref, sa_ref, sb_ref, *rest):
        dq_refs, dk_refs, dv_refs = rest[:n_p], rest[n_p:2 * n_p], rest[2 * n_p:3 * n_p]
        dqkv_ref, dgq_ref, dgk_ref = rest[3 * n_p:]

        @pl.when(pl.program_id(0) == 0)
        def _():
            dgq_ref[...] = jnp.zeros_like(dgq_ref)
            dgk_ref[...] = jnp.zeros_like(dgk_ref)

        seg = _seg_matrix(1.0 / HEAD_DIM)
        r_i = lax.broadcasted_iota(jnp.int32, (LANES, LANES), 0)
        c_i = lax.broadcasted_iota(jnp.int32, (LANES, LANES), 1)
        same_dim = jnp.where((r_i & (HEAD_DIM - 1)) == (c_i & (HEAD_DIM - 1)), 1.0, 0.0).astype(BF16)
        c, sa, sb = c_ref[...], sa_ref[...], sb_ref[...]
        for idx, (t_ref, g_ref, d_refs, dg_ref) in enumerate(((q_ref, gq_ref, dq_refs, dgq_ref),
                                                              (k_ref, gk_ref, dk_refs, dgk_ref))):
            dg = jnp.zeros((1, LANES), F32)
            for blk in range(W // LANES):
                cols = slice(blk * LANES, (blk + 1) * LANES)
                dout = d_refs[0][:, cols]
                for r in d_refs[1:]:
                    dout = dout + r[:, cols]
                dy = dout * c + pltpu.roll(dout * sa, half, 1) + pltpu.roll(dout * sb, LANES - half, 1)
                t = t_ref[:, cols]
                r_ = lax.rsqrt(_seg_sum(t * t, seg) + EPS)
                xhat = t * r_
                dg = dg + jnp.sum(dy * xhat, axis=0, keepdims=True)
                dxhat = dy * g_ref[...]
                dt = r_ * (dxhat - xhat * _seg_sum(dxhat * xhat, seg))
                dqkv_ref[:, idx * W + blk * LANES: idx * W + (blk + 1) * LANES] = dt.astype(BF16)
            dg_ref[...] += _seg_sum(jnp.broadcast_to(dg, (8, LANES)), same_dim)[0:1]
        dv = dv_refs[0][...]
        for r in dv_refs[1:]:
            dv = dv + r[...]
        dqkv_ref[:, 2 * W:] = dv.astype(BF16)

    row = lambda k: pl.BlockSpec((tm, W), lambda m: (m, k))
    gain = pl.BlockSpec((1, LANES), lambda m: (0, 0))
    tab = pl.BlockSpec((tm, LANES), lambda m: (m, 0))
    return pl.pallas_call(
        body, name=name, grid=(S // tm,),
        in_specs=[row(0), row(1), gain, gain, tab, tab, tab] + [row(0)] * (3 * n_p),
        out_specs=[pl.BlockSpec((tm, 3 * W), lambda m: (m, 0)), gain, gain],
        out_shape=[jax.ShapeDtypeStruct((S, 3 * W), BF16), jax.ShapeDtypeStruct((1, LANES), F32),
                   jax.ShapeDtypeStruct((1, LANES), F32)],
        compiler_params=_cp("arbitrary"),
    )(qkv, qkv, gq, gk, *tables, *dqs, *dks, *dvs)


ATTN_BQ = 2 * BAND
ATTN_ROWS = 16 * ATTN_BQ
V_COL = 2 * N_HEADS * HEAD_DIM // LANES


def _attn_geometry(S, d):
    rows = min(ATTN_ROWS, S)
    halo = BAND * d
    assert rows % (ATTN_BQ * d) == 0 and S % rows == 0, (S, d)
    return rows, halo, rows // (ATTN_BQ * d)


def _attn_specs(S, d, col):
    rows, halo, _ = _attn_geometry(S, d)
    r = rows // halo
    n_h = S // halo
    prev = pl.BlockSpec((halo, LANES), lambda j, i: (jnp.maximum(i * r - 1, 0), col + j))
    cur = pl.BlockSpec((rows, LANES), lambda j, i: (i, col + j))
    nxt = pl.BlockSpec((halo, LANES), lambda j, i: (jnp.minimum((i + 1) * r, n_h - 1), col + j))
    return [prev, cur, nxt]


def _fill_window(scr, prev, cur, nxt, rows, halo):
    scr[0:halo, :] = prev[...]
    scr[halo:halo + rows, :] = cur[...]
    scr[halo + rows:2 * halo + rows, :] = nxt[...]


ATTN_CHAINS = 2


def _chain_groups(n_sb, d):
    chains = [(sb, r) for sb in range(n_sb) for r in range(d)]
    return [chains[j:j + ATTN_CHAINS] for j in range(0, len(chains), ATTN_CHAINS)]


def _strided(ref, start, size, d):
    return ref[pl.ds(start, size, stride=d) if d > 1 else pl.ds(start, size), :]


def _band_mask(i, S, d, sb):
    rows, _, _ = _attn_geometry(S, d)
    L = S // d
    base = i * (rows // d) + sb * ATTN_BQ
    wk = ATTN_BQ + 2 * BAND
    row = lax.broadcasted_iota(jnp.int32, (ATTN_BQ, wk), 0)
    col = lax.broadcasted_iota(jnp.int32, (ATTN_BQ, wk), 1)
    lj = base - BAND + col
    return (jnp.abs(col - BAND - row) <= BAND) & (lj >= 0) & (lj < L)


def _attn_fwd(name, q, k, v, v_col, d):
    S, W = q.shape
    rows, halo, n_sb = _attn_geometry(S, d)
    wk = ATTN_BQ + 2 * BAND
    scale = HEAD_DIM ** -0.5

    def body(q_ref, kp, kc, kn, vp, vc, vn, o_ref, lse_ref, kw, vw):
        i = pl.program_id(1)
        _fill_window(kw, kp, kc, kn, rows, halo)
        _fill_window(vw, vp, vc, vn, rows, halo)
        first = _first_head((ATTN_BQ, LANES))
        heads = (first, jnp.logical_not(first))
        zero = jnp.zeros((), BF16)
        for group in _chain_groups(n_sb, d):
            masks = {sb: _band_mask(i, S, d, sb) for sb in sorted({sb for sb, _ in group})}
            starts = [r + d * sb * ATTN_BQ for sb, r in group]
            qs = [_strided(q_ref, st, ATTN_BQ, d).astype(BF16) for st in starts]
            ks = [_strided(kw, st, wk, d).astype(BF16) for st in starts]
            vs = [_strided(vw, st, wk, d).astype(BF16) for st in starts]
            s_all = [[_dot(jnp.where(hm, qv, zero), kv, 1, 1) for hm in heads] for qv, kv in zip(qs, ks)]
            p_all, den_all, lse_all = [], [], []
            for (sb, _), s_h in zip(group, s_all):
                s_h = [jnp.where(masks[sb], s * scale, NEG) for s in s_h]
                mx_h = [jnp.max(s, axis=-1, keepdims=True) for s in s_h]
                p_h = [jnp.exp(s - mx) for s, mx in zip(s_h, mx_h)]
                den_h = [jnp.sum(p, axis=-1, keepdims=True) for p in p_h]
                p_all.append([p.astype(BF16) for p in p_h])
                den_all.append(den_h)
                lse_all.append([mx + jnp.log(den) for mx, den in zip(mx_h, den_h)])
            o_all = [[_dot(p, vv, 1, 0) for p in p_h] for p_h, vv in zip(p_all, vs)]
            for st, o_h, den_h, lse_h in zip(starts, o_all, den_all, lse_all):
                dst = pl.ds(st, ATTN_BQ, stride=d) if d > 1 else pl.ds(st, ATTN_BQ)
                o_ref[dst, :] = jnp.where(first, o_h[0] / den_h[0], o_h[1] / den_h[1])
                lse_ref[dst, :] = jnp.where(first, lse_h[0], lse_h[1])

    cur = _attn_specs(S, d, 0)[1]
    return pl.pallas_call(
        body, name=name, grid=(W // LANES, S // rows),
        in_specs=[cur] + _attn_specs(S, d, 0) + _attn_specs(S, d, v_col), out_specs=[cur, cur],
        out_shape=[jax.ShapeDtypeStruct((S, W), F32)] * 2,
        scratch_shapes=[pltpu.VMEM((rows + 2 * halo, LANES), F32)] * 2,
        compiler_params=_cp("parallel", "parallel"),
    )(q, k, k, k, v, v, v)


def _attn_merge(os, lses):
    S, W = os[0].shape
    tm = _tile(S, 256)
    n_p = len(os)

    def body(*refs):
        o_refs, l_refs = refs[:n_p], refs[n_p:2 * n_p]
        o_ref, lt_ref = refs[2 * n_p:]
        ls = [r[...] for r in l_refs]
        mx = functools.reduce(jnp.maximum, ls)
        es = [jnp.exp(l - mx) for l in ls]
        den = functools.reduce(lambda a, b: a + b, es)
        acc = es[0] * o_refs[0][...]
        for e, r in zip(es[1:], o_refs[1:]):
            acc = acc + e * r[...]
        o_ref[...] = (acc / den).astype(BF16)
        lt_ref[...] = mx + jnp.log(den)

    row = pl.BlockSpec((tm, W), lambda m: (m, 0))
    return pl.pallas_call(
        body, name="attn_merge", grid=(S // tm,), in_specs=[row] * (2 * n_p), out_specs=[row, row],
        out_shape=[jax.ShapeDtypeStruct((S, W), BF16), jax.ShapeDtypeStruct((S, W), F32)],
        compiler_params=_cp("parallel"),
    )(*os, *lses)


def _delta_epilogue(do, o):
    seg = _seg_matrix(1.0)
    prod = do * o.astype(F32)
    delta = [_seg_sum(prod[:, blk * LANES:(blk + 1) * LANES], seg) for blk in range(do.shape[1] // LANES)]
    return do, jnp.concatenate(delta, axis=1)


def _attn_bwd_q(name, q, k, v, v_col, do, lse, delta, d):
    S, W = q.shape
    rows, halo, n_sb = _attn_geometry(S, d)
    wk = ATTN_BQ + 2 * BAND
    scale = HEAD_DIM ** -0.5

    def body(q_ref, do_ref, l_ref, dl_ref, kp, kc, kn, vp, vc, vn, dq_ref, kw, vw):
        i = pl.program_id(1)
        _fill_window(kw, kp, kc, kn, rows, halo)
        _fill_window(vw, vp, vc, vn, rows, halo)
        first = _first_head((ATTN_BQ, LANES))
        heads = (first, jnp.logical_not(first))
        zero = jnp.zeros((), BF16)
        wide = lambda t: jnp.concatenate([t] * (wk // LANES), axis=1)
        for group in _chain_groups(n_sb, d):
            masks = {sb: _band_mask(i, S, d, sb) for sb in sorted({sb for sb, _ in group})}
            starts = [r + d * sb * ATTN_BQ for sb, r in group]
            qs = [_strided(q_ref, st, ATTN_BQ, d).astype(BF16) for st in starts]
            dos = [_strided(do_ref, st, ATTN_BQ, d).astype(BF16) for st in starts]
            ks = [_strided(kw, st, wk, d).astype(BF16) for st in starts]
            vs = [_strided(vw, st, wk, d).astype(BF16) for st in starts]
            s_all = [[_dot(jnp.where(hm, qv, zero), kv, 1, 1) for hm in heads] for qv, kv in zip(qs, ks)]
            dp_all = [[_dot(jnp.where(hm, dov, zero), vv, 1, 1) for hm in heads] for dov, vv in zip(dos, vs)]
            ds_all = []
            for (sb, _), st, s_h, dp_h in zip(group, starts, s_all, dp_all):
                lv, dlv = _strided(l_ref, st, ATTN_BQ, d), _strided(dl_ref, st, ATTN_BQ, d)
                l_sw, dl_sw = pltpu.roll(lv, HEAD_DIM, 1), pltpu.roll(dlv, HEAD_DIM, 1)
                ds_h = []
                for hm, s, dp in zip(heads, s_h, dp_h):
                    p = jnp.exp(jnp.where(masks[sb], s * scale, NEG) - wide(jnp.where(hm, lv, l_sw)))
                    ds_h.append((p * (dp - wide(jnp.where(hm, dlv, dl_sw))) * scale).astype(BF16))
                ds_all.append(ds_h)
            dq_all = [[_dot(ds, kv, 1, 0) for ds in ds_h] for ds_h, kv in zip(ds_all, ks)]
            for st, dq_h in zip(starts, dq_all):
                dst = pl.ds(st, ATTN_BQ, stride=d) if d > 1 else pl.ds(st, ATTN_BQ)
                dq_ref[dst, :] = jnp.where(first, dq_h[0], dq_h[1])

    cur = _attn_specs(S, d, 0)[1]
    return pl.pallas_call(
        body, name=name, grid=(W // LANES, S // rows),
        in_specs=[cur] * 4 + _attn_specs(S, d, 0) + _attn_specs(S, d, v_col), out_specs=cur,
        out_shape=jax.ShapeDtypeStruct((S, W), F32),
        scratch_shapes=[pltpu.VMEM((rows + 2 * halo, LANES), F32)] * 2,
        compiler_params=_cp("parallel", "parallel"),
    )(q, do, lse, delta, k, k, k, v, v, v)


def _attn_bwd_kv(name, q, k, v, v_col, do, lse, delta, d):
    S, W = q.shape
    rows, halo, n_sb = _attn_geometry(S, d)
    wk = ATTN_BQ + 2 * BAND
    scale = HEAD_DIM ** -0.5

    def body(k_ref, v_ref, qp, qc, qn, dop, doc, don, lp, lc, ln, dlp, dlc, dln, dk_ref, dv_ref, qw, dow, lw, dlw):
        i = pl.program_id(1)
        _fill_window(qw, qp, qc, qn, rows, halo)
        _fill_window(dow, dop, doc, don, rows, halo)
        _fill_window(lw, lp, lc, ln, rows, halo)
        _fill_window(dlw, dlp, dlc, dln, rows, halo)
        first = _first_head((ATTN_BQ, LANES))
        heads = (first, jnp.logical_not(first))
        zero = jnp.zeros((), BF16)
        for group in _chain_groups(n_sb, d):
            masks = {sb: _band_mask(i, S, d, sb) for sb in sorted({sb for sb, _ in group})}
            starts = [r + d * sb * ATTN_BQ for sb, r in group]
            ks = [_strided(k_ref, st, ATTN_BQ, d).astype(BF16) for st in starts]
            vs = [_strided(v_ref, st, ATTN_BQ, d).astype(BF16) for st in starts]
            qs = [_strided(qw, st, wk, d).astype(BF16) for st in starts]
            dos = [_strided(dow, st, wk, d).astype(BF16) for st in starts]
            s_all = [[_dot(jnp.where(hm, kv, zero), qv, 1, 1) for hm in heads] for kv, qv in zip(ks, qs)]
            dp_all = [[_dot(jnp.where(hm, vv, zero), dov, 1, 1) for hm in heads] for vv, dov in zip(vs, dos)]
            p_all, ds_all = [], []
            for (sb, _), st, s_h, dp_h in zip(group, starts, s_all, dp_all):
                l_t, dl_t = _strided(lw, st, wk, d).T, _strided(dlw, st, wk, d).T
                p_h = [jnp.exp(jnp.where(masks[sb], s * scale, NEG) - l_t[hh * HEAD_DIM:hh * HEAD_DIM + 1, :])
                       for hh, s in enumerate(s_h)]
                ds_all.append([(p * (dp - dl_t[hh * HEAD_DIM:hh * HEAD_DIM + 1, :]) * scale).astype(BF16)
                               for hh, (p, dp) in enumerate(zip(p_h, dp_h))])
                p_all.append([p.astype(BF16) for p in p_h])
            dv_all = [[_dot(p, dov, 1, 0) for p in p_h] for p_h, dov in zip(p_all, dos)]
            dk_all = [[_dot(ds, qv, 1, 0) for ds in ds_h] for ds_h, qv in zip(ds_all, qs)]
            for st, dk_h, dv_h in zip(starts, dk_all, dv_all):
                dst = pl.ds(st, ATTN_BQ, stride=d) if d > 1 else pl.ds(st, ATTN_BQ)
                dk_ref[dst, :] = jnp.where(first, dk_h[0], dk_h[1])
                dv_ref[dst, :] = jnp.where(first, dv_h[0], dv_h[1])

    cur = _attn_specs(S, d, 0)[1]
    win = _attn_specs(S, d, 0)
    return pl.pallas_call(
        body, name=name, grid=(W // LANES, S // rows),
        in_specs=[cur, _attn_specs(S, d, v_col)[1]] + win * 4, out_specs=[cur, cur],
        out_shape=[jax.ShapeDtypeStruct((S, W), F32)] * 2,
        scratch_shapes=[pltpu.VMEM((rows + 2 * halo, LANES), F32)] * 4,
        compiler_params=_cp("parallel", "parallel"),
    )(k, v, q, q, q, do, do, do, lse, lse, lse, delta, delta, delta)


def _place():
    x, y, c = lax.axis_index("x"), lax.axis_index("y"), lax.axis_index("c")
    chips = [(1 - x, y), (x, 1 - y), (1 - x, 1 - y)]
    return x, y, c, chips


HBM = pl.BlockSpec(memory_space=pltpu.HBM)
SEM = pl.BlockSpec(memory_space=pltpu.SEMAPHORE)
DATAFLOW = pltpu.SideEffectType.DATAFLOW_SIDE_EFFECTING


def _exchange_copies(kind, srcs, dsts, send_sems, recv_sems):
    x, y, c, chips = _place()
    mine = 2 * x + y
    cps = []
    for t in range(len(srcs)):
        for k, (px, py) in enumerate(chips):
            src = srcs[t] if kind == "gather" else srcs[t].at[2 * px + py]
            dst = dsts[t].at[mine] if kind == "gather" else dsts[t].at[k]
            cps.append(pltpu.make_async_remote_copy(src_ref=src, dst_ref=dst, send_sem=send_sems.at[3 * t + k],
                                                    recv_sem=recv_sems.at[3 * t + k], device_id=(px, py, c), device_id_type=MESH))
    return cps


def _exchange_start(name, kind, groups):
    sizes = [len(g) for g in groups]
    n, n_g = sum(sizes), len(groups)

    def body(*refs):
        srcs, dsts = refs[:n], refs[n:2 * n]
        sems = refs[2 * n:2 * n + 2 * n_g]
        token = refs[4 * n + 2 * n_g]
        off = 0
        for gi, size in enumerate(sizes):
            for cp in _exchange_copies(kind, srcs[off:off + size], dsts[off:off + size], sems[2 * gi], sems[2 * gi + 1]):
                cp.start()
            off += size
        token[...] = jnp.zeros_like(token)

    arrays = [pltpu.with_memory_space_constraint(a, pltpu.HBM) for a in
              [s for g in groups for s, _ in g] + [d for g in groups for _, d in g]]
    sem_shapes = []
    for size in sizes:
        sem_shapes += [pltpu.SemaphoreType.DMA((3 * size,))] * 2
    outs = pl.pallas_call(
        body, name=name,
        in_specs=[HBM] * (2 * n),
        out_specs=[SEM] * (2 * n_g) + [HBM] * (2 * n) + [pl.BlockSpec(memory_space=pltpu.VMEM)],
        out_shape=sem_shapes + [pltpu.HBM(a.shape, a.dtype) for a in arrays] + [jax.ShapeDtypeStruct((8, LANES), F32)],
        input_output_aliases={t: 2 * n_g + t for t in range(2 * n)},
        compiler_params=pltpu.CompilerParams(has_side_effects=DATAFLOW),
    )(*arrays)
    sems, thru, token = outs[:2 * n_g], outs[2 * n_g:-1], outs[-1]
    handles, off = [], 0
    for gi, size in enumerate(sizes):
        handles.append((sems[2 * gi], sems[2 * gi + 1], thru[off:off + size], thru[n + off:n + off + size]))
        off += size
    return handles, token


def _exchange_wait(name, kind, handle, after):
    send_sems, recv_sems, srcs, dsts = handle
    n = len(srcs)

    def body(*refs):
        for cp in _exchange_copies(kind, refs[:n], refs[n:2 * n], refs[2 * n], refs[2 * n + 1]):
            cp.wait_send()
            cp.wait_recv()

    outs = pl.pallas_call(
        body, name=name,
        in_specs=[HBM] * (2 * n) + [SEM, SEM, ANY], out_specs=[HBM] * (2 * n),
        out_shape=[pltpu.HBM(a.shape, a.dtype) for a in (*srcs, *dsts)],
        input_output_aliases={t: t for t in range(2 * n)},
        compiler_params=pltpu.CompilerParams(has_side_effects=DATAFLOW),
    )(*srcs, *dsts, send_sems, recv_sems, after)
    return outs[n:]


def _prepare_shard(name, w, idx, dtype, mine):
    _, R, C = w.shape
    tr = _row_tile(R)

    def body(mine_ref, w_ref, src_ref, land_ref):
        val = w_ref[...].astype(dtype)
        src_ref[...] = val
        land_ref[...] = val

    return pl.pallas_call(
        body, name=name,
        grid_spec=pltpu.PrefetchScalarGridSpec(
            num_scalar_prefetch=1, grid=(R // tr,),
            in_specs=[pl.BlockSpec((None, tr, C), lambda i, s: (idx, i, 0))],
            out_specs=[pl.BlockSpec((tr, C), lambda i, s: (i, 0)), pl.BlockSpec((None, tr, C), lambda i, s: (s[0], i, 0))]),
        out_shape=[jax.ShapeDtypeStruct((R, C), dtype), jax.ShapeDtypeStruct((N_SHARDS, R, C), dtype)],
        compiler_params=_cp("parallel"),
    )(mine, w)


def _swap_with_sibling(parts):
    n = len(parts)

    def body(*refs):
        ins, outs = refs[:n], refs[n:2 * n]
        send_sems, recv_sems = refs[2 * n:]
        x, y, c, _ = _place()
        cps = [pltpu.make_async_remote_copy(src_ref=ins[t], dst_ref=outs[t], send_sem=send_sems.at[t], recv_sem=recv_sems.at[t],
                                            device_id=(x, y, 1 - c), device_id_type=MESH) for t in range(n)]
        for cp in cps:
            cp.start()
        for cp in cps:
            cp.wait_recv()
        for cp in cps:
            cp.wait_send()

    return pl.pallas_call(
        body, name="swap_partial_grads", in_specs=[ANY] * n, out_specs=[ANY] * n,
        out_shape=[jax.ShapeDtypeStruct(p.shape, p.dtype) for p in parts],
        scratch_shapes=[pltpu.SemaphoreType.DMA((n,)), pltpu.SemaphoreType.DMA((n,))],
    )(*parts)


def _allreduce_small(v):
    rows = v.shape[0]

    def body(v_ref, out_ref, buf, send_sems, recv_sems):
        x, y, c, chips = _place()
        me, sibling = (x, y, c), (x, y, 1 - c)

        def slot(px, py, pc):
            return buf.at[4 * px + 2 * py + pc]

        def copy(k, block, to, src=None):
            return pltpu.make_async_remote_copy(
                src_ref=slot(*block) if src is None else src, dst_ref=slot(*block), send_sem=send_sems.at[k],
                recv_sem=recv_sems.at[k], device_id=to, device_id_type=MESH)

        slot(*me)[...] = v_ref[...]
        first = [copy(0, me, sibling, src=v_ref)] + [copy(1 + j, me, (*chip, c), src=v_ref) for j, chip in enumerate(chips)]
        for cp in first:
            cp.start()
        passed = [copy(4 + j, (*chip, c), sibling) for j, chip in enumerate(chips)]
        for j, chip in enumerate(chips):
            copy(1 + j, (*chip, c), me).wait_recv()
            passed[j].start()
        copy(0, sibling, me).wait_recv()
        for j, chip in enumerate(chips):
            copy(4 + j, (*chip, 1 - c), me).wait_recv()
        for cp in first + passed:
            cp.wait_send()
        acc = buf[0]
        for k in range(1, 8):
            acc = acc + buf[k]
        out_ref[...] = acc

    return pl.pallas_call(
        body, name="allreduce_small_grads",
        in_specs=[pl.BlockSpec(memory_space=pltpu.VMEM)], out_specs=pl.BlockSpec(memory_space=pltpu.VMEM),
        out_shape=jax.ShapeDtypeStruct((rows, LANES), F32),
        scratch_shapes=[pltpu.VMEM((8, rows, LANES), F32), pltpu.SemaphoreType.DMA((7,)), pltpu.SemaphoreType.DMA((7,))],
        compiler_params=pltpu.CompilerParams(vmem_limit_bytes=VMEM_LIMIT_BYTES),
    )(v)


MM_TM_K = 512
WGRAD_TM = 2048


def _rows_merged(w):
    return w.reshape(1, w.shape[0] * w.shape[1], w.shape[2])


def _sq_relu_epilogue(acc):
    r = jnp.maximum(acc, 0.0)
    return acc, r * r


def _add_epilogue(acc, x):
    return (acc + x,)


def _add_loss_epilogue(acc, x, target):
    e = acc + x - target
    D = e.shape[1]
    share = (0.5 / D) * jnp.sum(jnp.sum(e * e, axis=1, keepdims=True), axis=0, keepdims=True)
    return e * (1.0 / D), jnp.broadcast_to(share, (1, D))


def _add_norm_epilogue(acc, x, g):
    y = acc + x
    r = lax.rsqrt(jnp.mean(y * y, axis=-1, keepdims=True) + EPS)
    return y, y * r * g


def _norm_bwd_epilogue(dh, x, dres, g):
    r = lax.rsqrt(jnp.mean(x * x, axis=-1, keepdims=True) + EPS)
    xhat = x * r
    dxhat = dh * g
    dx = dres + r * (dxhat - xhat * jnp.mean(dxhat * xhat, axis=-1, keepdims=True))
    return dx, jnp.sum(dh * xhat, axis=0, keepdims=True)


def _sq_relu_grad_epilogue(acc, a):
    return (acc * (2.0 * jnp.maximum(a.astype(F32), 0.0)),)


STAGES = ("mixer", "mlp")


def _stage_tensors(layer, stage):
    i = layer // 2
    if stage == "mlp":
        return [("mlp_w1", layer), ("mlp_w2", layer)]
    return [("ab_w_in", i), ("b_conv_w", i), ("ab_w_out", i)] if layer % 2 == 0 else [("c_w_qkv", i), ("c_w_out", i)]


def _local_step(x, target, p, weights_of, grads_done):
    S, D = x.shape
    depth = p["mix_norm_g"].shape[0]
    n_even = (depth + 1) // 2
    mix_g3 = p["mix_norm_g"].reshape(depth, 1, D)
    mlp_g3 = p["mlp_norm_g"].reshape(depth, 1, D)
    vec3 = lambda t: t.reshape(t.shape[0], 1, t.shape[1])
    spw16 = p["a_spatial_w"].astype(BF16)
    spw16_t = jnp.swapaxes(spw16, 2, 3)
    bias_full = jnp.repeat(jnp.swapaxes(p["a_spatial_b"], 1, 2), HEAD_DIM, axis=2)
    vn_g, vn_b, cn_g, cn_b, cb3 = (vec3(p[k]) for k in ("a_vnorm_g", "a_vnorm_b", "b_norm_g", "b_norm_b", "b_conv_b"))
    tables = _rope_tables(S)
    gq = jnp.tile(p["c_q_norm_g"], (1, 2))
    gk = jnp.tile(p["c_k_norm_g"], (1, 2))

    saved = []
    h = _rms_fwd("mix_norm_0", x, mix_g3, 0)
    for layer in range(depth):
        i = layer // 2
        wl = dict(weights_of(layer, "mixer", x))
        rec = {"x_mix": x, "w": wl, "h_mix": h}
        if layer % 2 == 0:
            (z,) = _mm_ngroup(f"ab_in_{layer}", h, wl["ab_w_in"], nt=False, tm=MM_TM_K, out_dtypes=[F32])
            gconv = _glu_conv_fwd(f"glu_conv_{layer}", z, wl["b_conv_w"], cb3, i)
            cat = _ab_tail_fwd(f"ab_tail_{layer}", z, gconv, spw16, bias_full, vn_g, vn_b, cn_g, cn_b, i)
            x, h = _mm_kgroup(f"ab_out_{layer}", cat, _rows_merged(wl["ab_w_out"]), nt=False, tm=MM_TM_K,
                              out_dtypes=[F32, BF16], extras=(x,), vecs=[(mlp_g3, layer)], epilogue=_add_norm_epilogue)
            rec.update(z=z, gconv=gconv, cat=cat)
        else:
            (qkv,) = _mm_ngroup(f"c_qkv_{layer}", h, wl["c_w_qkv"], nt=False, tm=MM_TM_K, out_dtypes=[F32])
            qn, kn = _qk_fwd(f"qk_norm_rope_{layer}", qkv, gq[i:i + 1], gk[i:i + 1], tables)
            os, lses = zip(*[_attn_fwd(f"attn_d{d}_{layer}", qn, kn, qkv, V_COL, d) for d in PATTERN_DILATIONS])
            o, lse = _attn_merge(os, lses)
            x, h = _mm_kgroup(f"c_out_{layer}", o, _rows_merged(wl["c_w_out"]), nt=False, tm=MM_TM_K,
                              out_dtypes=[F32, BF16], extras=(x,), vecs=[(mlp_g3, layer)], epilogue=_add_norm_epilogue)
            rec.update(qkv=qkv, qn=qn, kn=kn, o=o, lse=lse)
        rec["x_mlp"] = x
        wl.update(weights_of(layer, "mlp", x))
        a, hsq = _mm_ngroup(f"mlp_up_{layer}", h, wl["mlp_w1"], nt=False, tm=MM_TM_K, out_dtypes=[BF16, BF16],
                            epilogue=_sq_relu_epilogue)
        rec.update(h_mlp=h, a=a, hsq=hsq)
        if layer + 1 < depth:
            x, h = _mm_kgroup(f"mlp_down_{layer}", hsq, _rows_merged(wl["mlp_w2"]), nt=False, tm=MM_TM_K,
                              out_dtypes=[F32, BF16], extras=(x,), vecs=[(mix_g3, layer + 1)], epilogue=_add_norm_epilogue)
        else:
            dx, loss_row = _mm_kgroup(f"mlp_down_{layer}", hsq, _rows_merged(wl["mlp_w2"]), nt=False, tm=MM_TM_K,
                                      out_dtypes=[F32], extras=(x, target), n_sums=1, epilogue=_add_loss_epilogue)
        saved.append(rec)

    small = {k: [None] * v.shape[0] for k, v in p.items()}
    token = None
    for layer in reversed(range(depth)):
        i = layer // 2
        rec = saved[layer]
        wl = rec["w"]
        g = {}
        (da,) = _mm_ngroup(f"mlp_down_dgrad_{layer}", dx, wl["mlp_w2"], nt=True, tm=MM_TM_K, out_dtypes=[BF16],
                           extras=(rec["a"],), epilogue=_sq_relu_grad_epilogue, anchor=token)
        g["mlp_w2"] = _wgrad(f"mlp_down_wgrad_{layer}", rec["hsq"], dx, wl["mlp_w2"].shape, a_group=True, tm=WGRAD_TM)
        g["mlp_w1"] = _wgrad(f"mlp_up_wgrad_{layer}", rec["h_mlp"], da, wl["mlp_w1"].shape, a_group=False, tm=WGRAD_TM)
        dx, small["mlp_norm_g"][layer] = _mm_kgroup(
            f"mlp_up_dgrad_{layer}", da, wl["mlp_w1"], nt=True, tm=MM_TM_K, out_dtypes=[F32], extras=(rec["x_mlp"], dx),
            vecs=[(mlp_g3, layer)], n_sums=1, epilogue=_norm_bwd_epilogue)
        token = grads_done(layer, "mlp", g)
        g = {}
        if layer % 2 == 0:
            w_out = _rows_merged(wl["ab_w_out"])
            (dcat,) = _mm_ngroup(f"ab_out_dgrad_{layer}", dx, w_out, nt=True, tm=MM_TM_K, out_dtypes=[F32], anchor=token)
            g["ab_w_out"] = [t.reshape(wl["ab_w_out"].shape) for t in
                             _wgrad(f"ab_out_wgrad_{layer}", rec["cat"], dx, w_out.shape, a_group=True, tm=WGRAD_TM)]
            dz, dgconv, dspw, dbias, dvg, dvb, dcg, dcb = _ab_tail_bwd(
                f"ab_tail_bwd_{layer}", rec["z"], rec["gconv"], dcat, spw16, spw16_t, bias_full, vn_g, vn_b, cn_g, cn_b, i)
            dz, gf, gb, dcbias = _glu_conv_bwd(f"glu_conv_bwd_{layer}", rec["z"], dgconv, dz, wl["b_conv_w"])
            g["b_conv_w"] = (gf, gb)
            small["a_spatial_w"][i] = dspw
            small["a_spatial_b"][i] = _fold_bias(dbias)[:, :A_GROUPS].T
            for k, val in (("a_vnorm_g", dvg), ("a_vnorm_b", dvb), ("b_norm_g", dcg), ("b_norm_b", dcb), ("b_conv_b", dcbias)):
                small[k][i] = val
            g["ab_w_in"] = _wgrad(f"ab_in_wgrad_{layer}", rec["h_mix"], dz, wl["ab_w_in"].shape, a_group=False, tm=WGRAD_TM)
            dgrad = (f"ab_in_dgrad_{layer}", dz, wl["ab_w_in"])
        else:
            w_out = _rows_merged(wl["c_w_out"])
            do, delta = _mm_ngroup(f"c_out_dgrad_{layer}", dx, w_out, nt=True, tm=MM_TM_K, out_dtypes=[F32, F32],
                                   extras=(rec["o"],), epilogue=_delta_epilogue, anchor=token)
            g["c_w_out"] = [t.reshape(wl["c_w_out"].shape) for t in
                            _wgrad(f"c_out_wgrad_{layer}", rec["o"], dx, w_out.shape, a_group=True, tm=WGRAD_TM)]
            attn_args = (rec["qn"], rec["kn"], rec["qkv"], V_COL, do, rec["lse"], delta)
            dqs = [_attn_bwd_q(f"attn_bwd_q_d{d}_{layer}", *attn_args, d) for d in PATTERN_DILATIONS]
            dks, dvs = zip(*[_attn_bwd_kv(f"attn_bwd_kv_d{d}_{layer}", *attn_args, d) for d in PATTERN_DILATIONS])
            dqkv, dgq, dgk = _qk_bwd(f"qk_norm_rope_bwd_{layer}", rec["qkv"], gq[i:i + 1], gk[i:i + 1], tables, dqs, dks, dvs)
            small["c_q_norm_g"][i] = dgq[:, :HEAD_DIM]
            small["c_k_norm_g"][i] = dgk[:, :HEAD_DIM]
            g["c_w_qkv"] = _wgrad(f"c_qkv_wgrad_{layer}", rec["h_mix"], dqkv, wl["c_w_qkv"].shape, a_group=False, tm=WGRAD_TM)
            dgrad = (f"c_qkv_dgrad_{layer}", dqkv, wl["c_w_qkv"])
        dx, small["mix_norm_g"][layer] = _mm_kgroup(
            *dgrad, nt=True, tm=MM_TM_K, out_dtypes=[F32], extras=(rec["x_mix"], dx), vecs=[(mix_g3, layer)], n_sums=1,
            epilogue=_norm_bwd_epilogue)
        token = grads_done(layer, "mixer", g)

    small = {k: jnp.stack([t.reshape(p[k].shape[1:]) for t in v]) for k, v in small.items()}
    return loss_row, dx, small


SHARDED = ("mlp_w1", "mlp_w2", "ab_w_in", "b_conv_w", "ab_w_out", "c_w_qkv", "c_w_out")
SMALL = ("mix_norm_g", "mlp_norm_g", "a_spatial_w", "a_spatial_b", "a_vnorm_g", "a_vnorm_b", "b_conv_b", "b_norm_g",
         "b_norm_b", "c_q_norm_g", "c_k_norm_g")
WEIGHTS = ("mix_norm_g", "mlp_norm_g", "mlp_w1", "mlp_w2", "ab_w_in", "a_spatial_w", "a_spatial_b", "a_vnorm_g",
           "a_vnorm_b", "b_conv_w", "b_conv_b", "b_norm_g", "b_norm_b", "ab_w_out", "c_w_qkv", "c_q_norm_g",
           "c_k_norm_g", "c_w_out")


def _pack(parts):
    flat = jnp.concatenate([parts[k].reshape(-1) for k in SMALL])
    rows = -(-flat.shape[0] // (256 * LANES)) * 256
    return jnp.pad(flat, (0, rows * LANES - flat.shape[0])).reshape(rows, LANES)


def _unpack(packed, like):
    flat = packed.reshape(-1)
    out, off = {}, 0
    for k in SMALL:
        n = like[k].size
        out[k] = flat[off:off + n].reshape(like[k].shape)
        off += n
    return out


def kernel(x, mix_norm_g, mlp_norm_g, mlp_w1, mlp_w2, ab_w_in, a_spatial_w, a_spatial_b, a_vnorm_g, a_vnorm_b, b_conv_w, b_conv_b, b_norm_g, b_norm_b, ab_w_out, c_w_qkv, c_q_norm_g, c_k_norm_g, c_w_out, loss_target, m_mix_norm_g, m_mlp_norm_g, m_mlp_w1, m_mlp_w2, m_ab_w_in, m_a_spatial_w, m_a_spatial_b, m_a_vnorm_g, m_a_vnorm_b, m_b_conv_w, m_b_conv_b, m_b_norm_g, m_b_norm_b, m_ab_w_out, m_c_w_qkv, m_c_q_norm_g, m_c_k_norm_g, m_c_w_out, v_mix_norm_g, v_mlp_norm_g, v_mlp_w1, v_mlp_w2, v_ab_w_in, v_a_spatial_w, v_a_spatial_b, v_a_vnorm_g, v_a_vnorm_b, v_b_conv_w, v_b_conv_b, v_b_norm_g, v_b_norm_b, v_ab_w_out, v_c_w_qkv, v_c_q_norm_g, v_c_k_norm_g, v_c_w_out):
    w = dict(mix_norm_g=mix_norm_g, mlp_norm_g=mlp_norm_g, mlp_w1=mlp_w1, mlp_w2=mlp_w2, ab_w_in=ab_w_in,
             a_spatial_w=a_spatial_w, a_spatial_b=a_spatial_b, a_vnorm_g=a_vnorm_g, a_vnorm_b=a_vnorm_b,
             b_conv_w=b_conv_w, b_conv_b=b_conv_b, b_norm_g=b_norm_g, b_norm_b=b_norm_b, ab_w_out=ab_w_out,
             c_w_qkv=c_w_qkv, c_q_norm_g=c_q_norm_g, c_k_norm_g=c_k_norm_g, c_w_out=c_w_out)
    m = dict(mix_norm_g=m_mix_norm_g, mlp_norm_g=m_mlp_norm_g, mlp_w1=m_mlp_w1, mlp_w2=m_mlp_w2, ab_w_in=m_ab_w_in,
             a_spatial_w=m_a_spatial_w, a_spatial_b=m_a_spatial_b, a_vnorm_g=m_a_vnorm_g, a_vnorm_b=m_a_vnorm_b,
             b_conv_w=m_b_conv_w, b_conv_b=m_b_conv_b, b_norm_g=m_b_norm_g, b_norm_b=m_b_norm_b, ab_w_out=m_ab_w_out,
             c_w_qkv=m_c_w_qkv, c_q_norm_g=m_c_q_norm_g, c_k_norm_g=m_c_k_norm_g, c_w_out=m_c_w_out)
    v = dict(mix_norm_g=v_mix_norm_g, mlp_norm_g=v_mlp_norm_g, mlp_w1=v_mlp_w1, mlp_w2=v_mlp_w2, ab_w_in=v_ab_w_in,
             a_spatial_w=v_a_spatial_w, a_spatial_b=v_a_spatial_b, a_vnorm_g=v_a_vnorm_g, a_vnorm_b=v_a_vnorm_b,
             b_conv_w=v_b_conv_w, b_conv_b=v_b_conv_b, b_norm_g=v_b_norm_g, b_norm_b=v_b_norm_b, ab_w_out=v_ab_w_out,
             c_w_qkv=v_c_w_qkv, c_q_norm_g=v_c_q_norm_g, c_k_norm_g=v_c_k_norm_g, c_w_out=v_c_w_out)

    S, D = x.shape[1], x.shape[2]
    depth = mix_norm_g.shape[0]
    mine = (2 * lax.axis_index("x") + lax.axis_index("y")).astype(jnp.int32).reshape(1)

    stages = [(layer, stage) for layer in range(depth) for stage in STAGES]
    groups = [[(k, i) + tuple(_prepare_shard(f"prepare_{k}_{i}", w[k], i, F32 if k == "b_conv_w" else BF16, mine))
               for k, i in _stage_tensors(*st)] for st in stages]
    handles, gather_token = _exchange_start("gather_weights_start", "gather", [[(s, l) for _, _, s, l in g] for g in groups])
    handles = dict(zip(stages, handles))

    def weights_of(layer, stage, after):
        got = _exchange_wait(f"gather_weights_wait_{layer}_{stage}", "gather", handles[layer, stage],
                             gather_token if (layer, stage) == stages[0] else after)
        return {k: a for (k, _), a in zip(_stage_tensors(layer, stage), got)}

    scattered = {}

    def grads_done(layer, stage, g):
        names = [k for k, _ in _stage_tensors(layer, stage)]
        group = [(g[k][1], lax.empty((3,) + g[k][1].shape[1:], BF16)) for k in names]
        (handle,), token = _exchange_start(f"scatter_grads_start_{layer}_{stage}", "scatter", [group])
        scattered[layer, stage] = (handle, [g[k][0] for k in names])
        return token

    small_params = {k: w[k] for k in SMALL}
    loss_row, dx, small_grads = _local_step(x.reshape(S, D), loss_target.reshape(S, D), small_params, weights_of, grads_done)

    loss = lax.psum(loss_row[0, 0], ("x", "y", "c"))

    partial, order = [], []
    for layer, stage in reversed(stages):
        handle, gfs = scattered[layer, stage]
        recvs = _exchange_wait(f"scatter_grads_wait_{layer}_{stage}", "scatter", handle, dx)
        tensors = _stage_tensors(layer, stage)
        partial += [_sum4(f"sum_chips_{k}_{i}", gf, r, mine) for (k, i), gf, r in zip(tensors, gfs, recvs)]
        order += tensors
    other = _swap_with_sibling(partial)
    stacked = {k: [lax.empty(w[k].shape, F32) for _ in range(4)] for k in SHARDED}
    for (k, i), a, b in zip(order, partial, other):
        stacked[k] = _adamw_layer(f"adamw_{k}_{i}", w[k], m[k], v[k], i, a, b, stacked[k])
    grads, deltas, new_m, new_v = ({k: stacked[k][j] for k in SHARDED} for j in range(4))

    g_small = _allreduce_small(_pack(small_grads))
    outs = _adamw("adamw_small", _pack(small_params), _pack({k: m[k] for k in SMALL}), _pack({k: v[k] for k in SMALL}), g_small)
    for d_, packed in zip((grads, deltas, new_m, new_v), outs):
        d_.update(_unpack(packed, small_params))

    return (loss, dx.reshape(1, S, D), *[grads[k] for k in WEIGHTS], *[deltas[k] for k in WEIGHTS],
            *[new_m[k] for k in WEIGHTS], *[new_v[k] for k in WEIGHTS])
```

```python
import functools

import jax
import jax.numpy as jnp
from jax import lax
from jax.experimental import pallas as pl
from jax.experimental.pallas import tpu as pltpu

F32, BF16 = jnp.float32, jnp.bfloat16
MESH = pl.DeviceIdType.MESH
ANY = pl.BlockSpec(memory_space=pl.ANY)

VMEM_LIMIT_BYTES = 56 * 1024 * 1024
LANES = 128
ELEMENTWISE_ROWS = 256

EPS = 1e-6
NEG = -1e30
HEAD_DIM = 64
N_HEADS = 16
CHUNK = 128
A_GROUPS = 8
CONV_WIDTH = 31
CONV_HALO = 16
CONV_CHUNK = 64
BAND = 64
PATTERN_DILATIONS = (1, 4, 16)
ROT_DIM = 16
ROPE_THETA = 500000.0
N_SHARDS = 4

ADAM_LR, ADAM_B1, ADAM_B2, ADAM_EPS, ADAM_WD, ADAM_STEP = 0.001, 0.9, 0.999, 1e-08, 0.01, 10


def _cp(*sem):
    return pltpu.CompilerParams(dimension_semantics=sem, vmem_limit_bytes=VMEM_LIMIT_BYTES)


def _tile(n, pref):
    t = min(n, pref)
    assert n % t == 0, (n, pref)
    return t


def _dot(a, b, ca, cb):
    return lax.dot_general(a, b, (((ca,), (cb,)), ((), ())), preferred_element_type=F32)


def _mm_ngroup(name, a, w, *, nt, tm, out_dtypes, extras=(), epilogue=None, anchor=None):
    M, K = a.shape
    G, R, C = w.shape
    nw = R if nt else C
    assert K == (C if nt else R)
    tm = _tile(M, tm)
    n_ex = len(extras)
    anchors = [] if anchor is None else [anchor]

    def body(a_ref, w_ref, *rest):
        rest = rest[len(anchors):]
        av = a_ref[...].astype(BF16)
        for g in range(G):
            cols = slice(g * nw, (g + 1) * nw)
            acc = _dot(av, w_ref[g], 1, 1 if nt else 0)
            res = epilogue(acc, *[e[:, cols] for e in rest[:n_ex]]) if epilogue else (acc,)
            for o_ref, r in zip(rest[n_ex:], res):
                o_ref[:, cols] = r.astype(o_ref.dtype)

    blk = pl.BlockSpec((tm, G * nw), lambda m: (m, 0))
    return pl.pallas_call(
        body, name=name, grid=(M // tm,),
        in_specs=[pl.BlockSpec((tm, K), lambda m: (m, 0)), pl.BlockSpec((G, R, C), lambda m: (0, 0, 0))]
        + [pl.BlockSpec((8, LANES), lambda m: (0, 0))] * len(anchors) + [blk] * n_ex,
        out_specs=[blk] * len(out_dtypes),
        out_shape=[jax.ShapeDtypeStruct((M, G * nw), dt) for dt in out_dtypes],
        compiler_params=_cp("parallel"),
    )(a, w, *anchors, *extras)


def _mm_kgroup(name, a, w, *, nt, tm, out_dtypes, extras=(), vecs=(), n_sums=0, epilogue=None, anchor=None):
    G, R, C = w.shape
    kw, N = (C, R) if nt else (R, C)
    if a.ndim == 3:
        M = a.shape[1]
        assert a.shape[0] == G and a.shape[2] == kw
    else:
        M = a.shape[0]
        assert a.shape[1] == G * kw
    tm = _tile(M, tm)
    n_ex = len(extras)
    a_spec = (pl.BlockSpec((G, tm, kw), lambda m: (0, m, 0)) if a.ndim == 3 else pl.BlockSpec((tm, G * kw), lambda m: (m, 0)))
    anchors = [] if anchor is None else [anchor]

    def body(a_ref, w_ref, *rest):
        rest = rest[len(anchors):]
        acc = None
        for g in range(G):
            a_g = a_ref[g] if a.ndim == 3 else a_ref[:, g * kw:(g + 1) * kw]
            part = _dot(a_g.astype(BF16), w_ref[g], 1, 1 if nt else 0)
            acc = part if acc is None else acc + part
        n_in = n_ex + len(vecs)
        res = epilogue(acc, *[e[...] for e in rest[:n_in]]) if epilogue else (acc,)
        outs = rest[n_in:]
        n_tiles = len(outs) - n_sums
        for o_ref, r in zip(outs[:n_tiles], res[:n_tiles]):
            o_ref[...] = r.astype(o_ref.dtype)
        if n_sums:
            @pl.when(pl.program_id(0) == 0)
            def _():
                for s_ref in outs[n_tiles:]:
                    s_ref[...] = jnp.zeros_like(s_ref)

            for s_ref, r in zip(outs[n_tiles:], res[n_tiles:]):
                s_ref[...] += r

    blk = pl.BlockSpec((tm, N), lambda m: (m, 0))
    row = pl.BlockSpec((1, N), lambda m: (0, 0))
    return pl.pallas_call(
        body, name=name, grid=(M // tm,),
        in_specs=[a_spec, pl.BlockSpec((G, R, C), lambda m: (0, 0, 0))]
        + [pl.BlockSpec((8, LANES), lambda m: (0, 0))] * len(anchors) + [blk] * n_ex
        + [pl.BlockSpec((None, 1, N), lambda m, i=i: (i, 0, 0)) for _, i in vecs],
        out_specs=[blk] * len(out_dtypes) + [row] * n_sums,
        out_shape=[jax.ShapeDtypeStruct((M, N), dt) for dt in out_dtypes] + [jax.ShapeDtypeStruct((1, N), F32)] * n_sums,
        compiler_params=_cp("arbitrary" if n_sums else "parallel"),
    )(a, w, *anchors, *extras, *[v for v, _ in vecs])


def _wgrad(name, a, b, shape, *, a_group, tm):
    G, R, C = shape
    M = a.shape[0]
    tm = _tile(M, tm)
    n_m = M // tm

    def body(a_ref, b_ref, gf_ref, gb_ref):
        m = pl.program_id(1)
        part = _dot(a_ref[...].astype(BF16), b_ref[...].astype(BF16), 0, 0)

        @pl.when(m == 0)
        def _():
            gf_ref[...] = part

        @pl.when(m > 0)
        def _():
            gf_ref[...] += part

        @pl.when(m == n_m - 1)
        def _():
            gb_ref[...] = gf_ref[...].astype(BF16)

    a_spec = pl.BlockSpec((tm, R), (lambda g, m: (m, g)) if a_group else (lambda g, m: (m, 0)))
    if b.ndim == 3:
        assert not a_group
        b_spec = pl.BlockSpec((None, tm, C), lambda g, m: (g, m, 0))
    else:
        b_spec = pl.BlockSpec((tm, C), (lambda g, m: (m, 0)) if a_group else (lambda g, m: (m, g)))
    o_spec = pl.BlockSpec((None, R, C), lambda g, m: (g, 0, 0))
    return pl.pallas_call(
        body, name=name, grid=(G, n_m),
        in_specs=[a_spec, b_spec], out_specs=[o_spec, o_spec],
        out_shape=[jax.ShapeDtypeStruct(shape, F32), jax.ShapeDtypeStruct(shape, BF16)],
        compiler_params=_cp("parallel", "arbitrary"),
    )(a, b)


def _rms_fwd(name, x, g3, layer):
    S, D = x.shape
    tm = _tile(S, 512)

    def body(x_ref, g_ref, h_ref):
        xv = x_ref[...]
        r = lax.rsqrt(jnp.mean(xv * xv, axis=-1, keepdims=True) + EPS)
        h_ref[...] = (xv * r * g_ref[...]).astype(BF16)

    row = pl.BlockSpec((tm, D), lambda m: (m, 0))
    return pl.pallas_call(
        body, name=name, grid=(S // tm,),
        in_specs=[row, pl.BlockSpec((None, 1, D), lambda m: (layer, 0, 0))], out_specs=row,
        out_shape=jax.ShapeDtypeStruct((S, D), BF16), compiler_params=_cp("parallel"),
    )(x, g3)


def _adamw_math(w, m, v, g):
    m2 = ADAM_B1 * m + (1.0 - ADAM_B1) * g
    v2 = ADAM_B2 * v + (1.0 - ADAM_B2) * jnp.square(g)
    m_hat = m2 / (1.0 - ADAM_B1 ** ADAM_STEP)
    v_hat = v2 / (1.0 - ADAM_B2 ** ADAM_STEP)
    return g, -ADAM_LR * (m_hat / (jnp.sqrt(v_hat) + ADAM_EPS) + ADAM_WD * w), m2, v2


def _row_tile(rows):
    return _tile(rows, ELEMENTWISE_ROWS) if rows % ELEMENTWISE_ROWS == 0 else rows


def _adamw(name, w, m, v, g):
    rows, C = w.shape
    tr = _row_tile(rows)

    def body(w_ref, m_ref, v_ref, g_in, g_ref, d_ref, nm_ref, nv_ref):
        for o_ref, val in zip((g_ref, d_ref, nm_ref, nv_ref), _adamw_math(w_ref[...], m_ref[...], v_ref[...], g_in[...])):
            o_ref[...] = val

    blk = pl.BlockSpec((tr, C), lambda i: (i, 0))
    return pl.pallas_call(
        body, name=name, grid=(rows // tr,), in_specs=[blk] * 4, out_specs=[blk] * 4,
        out_shape=[jax.ShapeDtypeStruct((rows, C), F32)] * 4, compiler_params=_cp("parallel"),
    )(w, m, v, g)


def _adamw_layer(name, w, m, v, layer, mine, theirs, outs):
    _, R, C = w.shape
    tr = _row_tile(R)

    def body(w_ref, m_ref, v_ref, a_ref, b_ref, *rest):
        g = a_ref[...] + b_ref[...]
        for o_ref, val in zip(rest[4:], _adamw_math(w_ref[...], m_ref[...], v_ref[...], g)):
            o_ref[...] = val

    st = pl.BlockSpec((None, tr, C), lambda i: (layer, i, 0))
    part = pl.BlockSpec((tr, C), lambda i: (i, 0))
    return pl.pallas_call(
        body, name=name, grid=(R // tr,), in_specs=[st] * 3 + [part] * 2 + [ANY] * 4, out_specs=[st] * 4,
        out_shape=[jax.ShapeDtypeStruct(w.shape, F32)] * 4, input_output_aliases={5 + j: j for j in range(4)},
        compiler_params=_cp("parallel"),
    )(w, m, v, mine, theirs, *outs)


def _sum4(name, gf, recv, mine):
    _, R, C = gf.shape
    tr = _row_tile(R)

    def body(mine_ref, o_ref, r_ref, out_ref):
        acc = o_ref[...]
        for k in range(3):
            acc = acc + r_ref[k].astype(F32)
        out_ref[...] = acc

    return pl.pallas_call(
        body, name=name,
        grid_spec=pltpu.PrefetchScalarGridSpec(
            num_scalar_prefetch=1, grid=(R // tr,),
            in_specs=[pl.BlockSpec((None, tr, C), lambda i, s: (s[0], i, 0)), pl.BlockSpec((3, tr, C), lambda i, s: (0, i, 0))],
            out_specs=pl.BlockSpec((tr, C), lambda i, s: (i, 0))),
        out_shape=jax.ShapeDtypeStruct((R, C), F32), compiler_params=_cp("parallel"),
    )(mine, gf, recv)


def _gelu(x):
    return x * (0.5 * (1.0 + jnp.tanh(0.7978845608028654 * (x + 0.044715 * (x * x * x)))))


def _layernorm(t, g, b):
    mu = jnp.mean(t, axis=-1, keepdims=True)
    var = jnp.mean(jnp.square(t - mu), axis=-1, keepdims=True)
    return (t - mu) * lax.rsqrt(var + EPS) * g + b


def _silu(x):
    return x * jax.nn.sigmoid(x)


def _a_value(zv, g, b):
    return _layernorm(_gelu(zv), g, b)


def _b_tail(gc, g, b):
    return _silu(_layernorm(gc, g, b))


def _first_head(shape):
    return lax.broadcasted_iota(jnp.int32, shape, len(shape) - 1) < HEAD_DIM


def _spatial_mix(spw_ref, vb, tm):
    first = _first_head((CHUNK, LANES))
    rows = []
    for n in range(tm // CHUNK):
        blocks = []
        for j in range(A_GROUPS // 2):
            vblk = vb[n * CHUNK:(n + 1) * CHUNK, j * LANES:(j + 1) * LANES]
            r0 = _dot(spw_ref[2 * j], vblk, 1, 0)
            r1 = _dot(spw_ref[2 * j + 1], vblk, 1, 0)
            blocks.append(jnp.where(first, r0, r1))
        rows.append(jnp.concatenate(blocks, axis=1))
    return jnp.concatenate(rows, axis=0) if len(rows) > 1 else rows[0]


def _ab_tail_fwd(name, z, gconv, spw, bias_full, vn_g, vn_b, cn_g, cn_b, layer):
    S = z.shape[0]
    AW = 512
    tm = _tile(S, 256)

    def body(zu_ref, zv_ref, gc_ref, spw_ref, bias_ref, vg_ref, vb_ref, cg_ref, cb_ref, cat_ref):
        u = _gelu(zu_ref[...])
        v = _a_value(zv_ref[...], vg_ref[...], vb_ref[...])
        sv = _spatial_mix(spw_ref, v.astype(BF16), tm) + jnp.tile(bias_ref[...], (tm // CHUNK, 1))
        cat_ref[:, :AW] = (u * sv).astype(BF16)
        cat_ref[:, AW:] = _b_tail(gc_ref[...], cg_ref[...], cb_ref[...]).astype(BF16)

    vec = pl.BlockSpec((None, 1, AW), lambda m: (layer, 0, 0))
    return pl.pallas_call(
        body, name=name, grid=(S // tm,),
        in_specs=[pl.BlockSpec((tm, AW), lambda m: (m, 0)), pl.BlockSpec((tm, AW), lambda m: (m, 1)),
                  pl.BlockSpec((tm, AW), lambda m: (m, 0)),
                  pl.BlockSpec((None, A_GROUPS, CHUNK, CHUNK), lambda m: (layer, 0, 0, 0)),
                  pl.BlockSpec((None, CHUNK, AW), lambda m: (layer, 0, 0)), vec, vec, vec, vec],
        out_specs=pl.BlockSpec((tm, 2 * AW), lambda m: (m, 0)),
        out_shape=jax.ShapeDtypeStruct((S, 2 * AW), BF16), compiler_params=_cp("parallel"),
    )(z, z, gconv, spw, bias_full, vn_g, vn_b, cn_g, cn_b)


def _ab_tail_bwd(name, z, gconv, dcat, spw, spw_t, bias_full, vn_g, vn_b, cn_g, cn_b, layer):
    S = z.shape[0]
    AW = 512
    tm = _tile(S, 256)
    n_chunks = tm // CHUNK

    def body(zu_ref, zv_ref, gc_ref, dcat_ref, spw_ref, spwt_ref, bias_ref, vg_ref, vb_ref, cg_ref, cb_ref,
             dz_ref, dgc_ref, dspw_ref, dbias_ref, dvg_ref, dvb_ref, dcg_ref, dcb_ref):
        @pl.when(pl.program_id(0) == 0)
        def _():
            for r in (dspw_ref, dbias_ref, dvg_ref, dvb_ref, dcg_ref, dcb_ref):
                r[...] = jnp.zeros_like(r)

        dya = dcat_ref[:, :AW]
        dyb = dcat_ref[:, AW:]
        u, u_vjp = jax.vjp(_gelu, zu_ref[...])
        v, v_vjp = jax.vjp(_a_value, zv_ref[...], vg_ref[...], vb_ref[...])
        vb16 = v.astype(BF16)
        sv = _spatial_mix(spw_ref, vb16, tm) + jnp.tile(bias_ref[...], (n_chunks, 1))
        (dzu,) = u_vjp(dya * sv)
        dsv = dya * u
        dsv16 = dsv.astype(BF16)
        dv = _spatial_mix(spwt_ref, dsv16, tm)
        dzv, dvg, dvb = v_vjp(dv)
        dz_ref[0] = dzu
        dz_ref[1] = dzv
        dvg_ref[...] += dvg
        dvb_ref[...] += dvb

        first = _first_head((CHUNK, LANES))
        zero = jnp.zeros((), BF16)
        dbias = jnp.zeros((CHUNK, AW), F32)
        for n in range(n_chunks):
            rows = slice(n * CHUNK, (n + 1) * CHUNK)
            dbias = dbias + dsv[rows]
            for j in range(A_GROUPS // 2):
                cols = slice(j * LANES, (j + 1) * LANES)
                dblk, vblk = dsv16[rows, cols], vb16[rows, cols]
                dspw_ref[2 * j] += _dot(jnp.where(first, dblk, zero), vblk, 1, 1)
                dspw_ref[2 * j + 1] += _dot(jnp.where(first, zero, dblk), vblk, 1, 1)
        dbias_ref[...] += dbias

        _, t_vjp = jax.vjp(_b_tail, gc_ref[...], cg_ref[...], cb_ref[...])
        dgc, dcg, dcb = t_vjp(dyb)
        dgc_ref[...] = dgc
        dcg_ref[...] += dcg
        dcb_ref[...] += dcb

    vec = pl.BlockSpec((None, 1, AW), lambda m: (layer, 0, 0))
    spw_spec = pl.BlockSpec((None, A_GROUPS, CHUNK, CHUNK), lambda m: (layer, 0, 0, 0))
    ovec = pl.BlockSpec((1, AW), lambda m: (0, 0))
    return pl.pallas_call(
        body, name=name, grid=(S // tm,),
        in_specs=[pl.BlockSpec((tm, AW), lambda m: (m, 0)), pl.BlockSpec((tm, AW), lambda m: (m, 1)),
                  pl.BlockSpec((tm, AW), lambda m: (m, 0)), pl.BlockSpec((tm, 2 * AW), lambda m: (m, 0)),
                  spw_spec, spw_spec, pl.BlockSpec((None, CHUNK, AW), lambda m: (layer, 0, 0)), vec, vec, vec, vec],
        out_specs=[pl.BlockSpec((2, tm, AW), lambda m: (0, m, 0)), pl.BlockSpec((tm, AW), lambda m: (m, 0)),
                   pl.BlockSpec((A_GROUPS, CHUNK, CHUNK), lambda m: (0, 0, 0)),
                   pl.BlockSpec((CHUNK, AW), lambda m: (0, 0)), ovec, ovec, ovec, ovec],
        out_shape=[jax.ShapeDtypeStruct((4, S, AW), F32), jax.ShapeDtypeStruct((S, AW), F32),
                   jax.ShapeDtypeStruct((A_GROUPS, CHUNK, CHUNK), F32), jax.ShapeDtypeStruct((CHUNK, AW), F32)]
                  + [jax.ShapeDtypeStruct((1, AW), F32)] * 4,
        compiler_params=_cp("arbitrary"),
    )(z, z, gconv, dcat, spw, spw_t, bias_full, vn_g, vn_b, cn_g, cn_b)


def _fold_bias(dbias_full):
    def body(d_ref, o_ref):
        d = d_ref[...]
        hi = d.astype(BF16)
        lo = (d - hi.astype(F32)).astype(BF16)
        r = lax.broadcasted_iota(jnp.int32, (512, LANES), 0)
        c = lax.broadcasted_iota(jnp.int32, (512, LANES), 1)
        fold = jnp.where(lax.shift_right_logical(r, 6) == c, 1.0, 0.0).astype(BF16)
        o_ref[...] = _dot(hi, fold, 1, 0) + _dot(lo, fold, 1, 0)

    return pl.pallas_call(body, name="fold_spatial_bias", out_shape=jax.ShapeDtypeStruct((CHUNK, LANES), F32))(dbias_full)


def _halo_specs(tm, n_halo_blocks, col):
    r = tm // CONV_HALO
    prev = pl.BlockSpec((CONV_HALO, LANES), lambda j, i: (jnp.maximum(i * r - 1, 0), col + j))
    cur = pl.BlockSpec((tm, LANES), lambda j, i: (i, col + j))
    nxt = pl.BlockSpec((CONV_HALO, LANES), lambda j, i: (jnp.minimum((i + 1) * r, n_halo_blocks - 1), col + j))
    return [prev, cur, nxt]


def _fill_halo(scr, prev, cur, nxt, tm, i, n_i):
    scr[0:CONV_HALO, :] = jnp.where(i > 0, prev, 0.0)
    scr[CONV_HALO:CONV_HALO + tm, :] = cur
    scr[CONV_HALO + tm:2 * CONV_HALO + tm, :] = jnp.where(i < n_i - 1, nxt, 0.0)


def _glu_conv_fwd(name, z, cw, cb3, layer):
    S = z.shape[0]
    tm = _tile(S, 512)
    n_i = S // tm
    pad = CONV_WIDTH // 2

    def body(vp, vc, vn, gp, gc, gn, w_ref, b_ref, out_ref, scr):
        i = pl.program_id(1)
        glu = lambda a, b: a[...] * jax.nn.sigmoid(b[...])
        _fill_halo(scr, glu(vp, gp), glu(vc, gc), glu(vn, gn), tm, i, n_i)
        taps = [w_ref[j:j + 1, :] for j in range(CONV_WIDTH)]
        for c0 in range(0, tm, CONV_CHUNK):
            acc = jnp.zeros((CONV_CHUNK, LANES), F32) + b_ref[...]
            for j in range(CONV_WIDTH):
                acc = acc + taps[j] * scr[pl.ds(c0 + CONV_HALO - pad + j, CONV_CHUNK), :]
            out_ref[pl.ds(c0, CONV_CHUNK), :] = acc

    return pl.pallas_call(
        body, name=name, grid=(4, n_i),
        in_specs=_halo_specs(tm, S // CONV_HALO, 8) + _halo_specs(tm, S // CONV_HALO, 12)
        + [pl.BlockSpec((None, CONV_WIDTH, LANES), lambda j, i: (j, 0, 0)),
           pl.BlockSpec((None, 1, LANES), lambda j, i: (layer, 0, j))],
        out_specs=pl.BlockSpec((tm, LANES), lambda j, i: (i, j)),
        out_shape=jax.ShapeDtypeStruct((S, 4 * LANES), F32),
        scratch_shapes=[pltpu.VMEM((tm + 2 * CONV_HALO, LANES), F32)],
        compiler_params=_cp("parallel", "parallel"),
    )(z, z, z, z, z, z, cw, cb3)


def _glu_conv_bwd(name, z, dgconv, dz, cw):
    S = z.shape[0]
    tm = _tile(S, 512)
    n_i = S // tm
    pad = CONV_WIDTH // 2

    def body(vp, vc, vn, gp, gc, gn, dp, dc, dn, w_ref, dz_in, dz_ref, gf_ref, gb_ref, db_ref, g_scr, d_scr):
        i = pl.program_id(1)
        sig = jax.nn.sigmoid(gc[...])
        _fill_halo(g_scr, vp[...] * jax.nn.sigmoid(gp[...]), vc[...] * sig, vn[...] * jax.nn.sigmoid(gn[...]), tm, i, n_i)
        _fill_halo(d_scr, dp[...], dc[...], dn[...], tm, i, n_i)

        @pl.when(i == 0)
        def _():
            gf_ref[...] = jnp.zeros_like(gf_ref)
            db_ref[...] = jnp.zeros_like(db_ref)

        taps = [w_ref[j:j + 1, :] for j in range(CONV_WIDTH)]
        dw = [jnp.zeros((8, LANES), F32) for _ in range(CONV_WIDTH)]
        db = jnp.zeros((8, LANES), F32)
        fold8 = lambda t: jnp.sum(t.reshape(CONV_CHUNK // 8, 8, LANES), axis=0)
        for c0 in range(0, tm, CONV_CHUNK):
            rows = pl.ds(c0, CONV_CHUNK)
            d_cur = dc[rows, :]
            dglu = jnp.zeros((CONV_CHUNK, LANES), F32)
            for j in range(CONV_WIDTH):
                dglu = dglu + taps[j] * d_scr[pl.ds(c0 + CONV_HALO + pad - j, CONV_CHUNK), :]
                dw[j] = dw[j] + fold8(d_cur * g_scr[pl.ds(c0 + CONV_HALO - pad + j, CONV_CHUNK), :])
            db = db + fold8(d_cur)
            sig_c = jax.nn.sigmoid(gc[rows, :])
            dz_ref[0, rows, :] = dglu * sig_c
            dz_ref[1, rows, :] = dglu * vc[rows, :] * sig_c * (1.0 - sig_c)
        for j in range(CONV_WIDTH):
            gf_ref[j:j + 1, :] += jnp.sum(dw[j], axis=0, keepdims=True)
        db_ref[...] += jnp.sum(db, axis=0, keepdims=True)

        @pl.when(i == n_i - 1)
        def _():
            gb_ref[...] = gf_ref[...].astype(BF16)

    w_spec = pl.BlockSpec((None, CONV_WIDTH, LANES), lambda j, i: (j, 0, 0))
    return pl.pallas_call(
        body, name=name, grid=(4, n_i),
        in_specs=_halo_specs(tm, S // CONV_HALO, 8) + _halo_specs(tm, S // CONV_HALO, 12)
        + _halo_specs(tm, S // CONV_HALO, 0) + [w_spec, ANY],
        out_specs=[pl.BlockSpec((2, tm, LANES), lambda j, i: (1, i, j)),
                   w_spec, w_spec, pl.BlockSpec((1, LANES), lambda j, i: (0, j))],
        out_shape=[jax.ShapeDtypeStruct(dz.shape, F32), jax.ShapeDtypeStruct(cw.shape, F32),
                   jax.ShapeDtypeStruct(cw.shape, BF16), jax.ShapeDtypeStruct((1, 4 * LANES), F32)],
        input_output_aliases={10: 0},
        scratch_shapes=[pltpu.VMEM((tm + 2 * CONV_HALO, LANES), F32)] * 2,
        compiler_params=_cp("parallel", "arbitrary"),
    )(z, z, z, z, z, z, dgconv, dgconv, dgconv, cw, dz)


def _seg_matrix(scale):
    r = lax.broadcasted_iota(jnp.int32, (LANES, LANES), 0)
    c = lax.broadcasted_iota(jnp.int32, (LANES, LANES), 1)
    return jnp.where(lax.shift_right_logical(r, 6) == lax.shift_right_logical(c, 6), scale, 0.0).astype(BF16)


def _seg_sum(x, seg):
    hi = x.astype(BF16)
    lo = (x - hi.astype(F32)).astype(BF16)
    return _dot(hi, seg, 1, 0) + _dot(lo, seg, 1, 0)


def _rope_tables(S):
    pos = jnp.arange(S, dtype=F32)
    inv_freq = ROPE_THETA ** (-jnp.arange(0, ROT_DIM, 2, dtype=F32) / ROT_DIM)
    ang = pos[:, None] * inv_freq[None, :]
    cos, sin = jnp.cos(ang), jnp.sin(ang)
    half = ROT_DIM // 2
    rest = HEAD_DIM - ROT_DIM
    one, zero = jnp.ones((S, rest), F32), jnp.zeros((S, rest), F32)
    zh = jnp.zeros((S, half), F32)
    c = jnp.concatenate([cos, cos, one], axis=1)
    sa = jnp.concatenate([-sin, zh, zero], axis=1)
    sb = jnp.concatenate([zh, sin, zero], axis=1)
    return [jnp.tile(t, (1, 2)) for t in (c, sa, sb)]


QK_CHUNK = 64


def _qk_fwd(name, qkv, gq, gk, tables):
    S = qkv.shape[0]
    W = N_HEADS * HEAD_DIM
    tm = _tile(S, 256)
    half = ROT_DIM // 2

    def body(q_ref, k_ref, gq_ref, gk_ref, c_ref, sa_ref, sb_ref, qn_ref, kn_ref):
        seg = _seg_matrix(1.0 / HEAD_DIM)
        for r0 in range(0, tm, QK_CHUNK):
            rows = pl.ds(r0, QK_CHUNK)
            c, sa, sb = c_ref[rows, :], sa_ref[rows, :], sb_ref[rows, :]
            for t_ref, g_ref, o_ref in ((q_ref, gq_ref, qn_ref), (k_ref, gk_ref, kn_ref)):
                for blk in range(W // LANES):
                    cols = slice(blk * LANES, (blk + 1) * LANES)
                    t = t_ref[rows, cols]
                    y = t * lax.rsqrt(_seg_sum(t * t, seg) + EPS) * g_ref[...]
                    o_ref[rows, cols] = y * c + pltpu.roll(y, LANES - half, 1) * sa + pltpu.roll(y, half, 1) * sb

    row = lambda k: pl.BlockSpec((tm, W), lambda m: (m, k))
    gain = pl.BlockSpec((1, LANES), lambda m: (0, 0))
    tab = pl.BlockSpec((tm, LANES), lambda m: (m, 0))
    return pl.pallas_call(
        body, name=name, grid=(S // tm,),
        in_specs=[row(0), row(1), gain, gain, tab, tab, tab], out_specs=[row(0)] * 2,
        out_shape=[jax.ShapeDtypeStruct((S, W), F32)] * 2, compiler_params=_cp("parallel"),
    )(qkv, qkv, gq, gk, *tables)


def _qk_bwd(name, qkv, gq, gk, tables, dqs, dks, dvs):
    S = qkv.shape[0]
    W = N_HEADS * HEAD_DIM
    tm = _tile(S, 256)
    half = ROT_DIM // 2
    n_p = len(dqs)

    def body(q_ref, k_ref, gq_ref, gk_ref, c_ref, sa_ref, sb_ref, *rest):
        dq_refs, dk_refs, dv_refs = rest[:n_p], rest[n_p:2 * n_p], rest[2 * n_p:3 * n_p]
        dqkv_ref, dgq_ref, dgk_ref = rest[3 * n_p:]

        @pl.when(pl.program_id(0) == 0)
        def _():
            dgq_ref[...] = jnp.zeros_like(dgq_ref)
            dgk_ref[...] = jnp.zeros_like(dgk_ref)

        seg = _seg_matrix(1.0 / HEAD_DIM)
        r_i = lax.broadcasted_iota(jnp.int32, (LANES, LANES), 0)
        c_i = lax.broadcasted_iota(jnp.int32, (LANES, LANES), 1)
        same_dim = jnp.where((r_i & (HEAD_DIM - 1)) == (c_i & (HEAD_DIM - 1)), 1.0, 0.0).astype(BF16)
        dgs = [jnp.zeros((8, LANES), F32), jnp.zeros((8, LANES), F32)]
        fold8 = lambda t: jnp.sum(t.reshape(QK_CHUNK // 8, 8, LANES), axis=0)
        for r0 in range(0, tm, QK_CHUNK):
            rows = pl.ds(r0, QK_CHUNK)
            c, sa, sb = c_ref[rows, :], sa_ref[rows, :], sb_ref[rows, :]
            for idx, (t_ref, g_ref, d_refs) in enumerate(((q_ref, gq_ref, dq_refs), (k_ref, gk_ref, dk_refs))):
                for blk in range(W // LANES):
                    cols = slice(blk * LANES, (blk + 1) * LANES)
                    dout = d_refs[0][rows, cols]
                    for r in d_refs[1:]:
                        dout = dout + r[rows, cols]
                    dy = dout * c + pltpu.roll(dout * sa, half, 1) + pltpu.roll(dout * sb, LANES - half, 1)
                    t = t_ref[rows, cols]
                    r_ = lax.rsqrt(_seg_sum(t * t, seg) + EPS)
                    xhat = t * r_
                    dgs[idx] = dgs[idx] + fold8(dy * xhat)
                    dxhat = dy * g_ref[...]
                    dt = r_ * (dxhat - xhat * _seg_sum(dxhat * xhat, seg))
                    dqkv_ref[rows, idx * W + blk * LANES: idx * W + (blk + 1) * LANES] = dt.astype(BF16)
            dv = dv_refs[0][rows, :]
            for r in dv_refs[1:]:
                dv = dv + r[rows, :]
            dqkv_ref[rows, 2 * W:] = dv.astype(BF16)
        for dg, dg_ref in zip(dgs, (dgq_ref, dgk_ref)):
            dg_ref[...] += jnp.sum(_seg_sum(dg, same_dim), axis=0, keepdims=True)

    row = lambda k: pl.BlockSpec((tm, W), lambda m: (m, k))
    gain = pl.BlockSpec((1, LANES), lambda m: (0, 0))
    tab = pl.BlockSpec((tm, LANES), lambda m: (m, 0))
    return pl.pallas_call(
        body, name=name, grid=(S // tm,),
        in_specs=[row(0), row(1), gain, gain, tab, tab, tab] + [row(0)] * (3 * n_p),
        out_specs=[pl.BlockSpec((tm, 3 * W), lambda m: (m, 0)), gain, gain],
        out_shape=[jax.ShapeDtypeStruct((S, 3 * W), BF16), jax.ShapeDtypeStruct((1, LANES), F32),
                   jax.ShapeDtypeStruct((1, LANES), F32)],
        compiler_params=_cp("arbitrary"),
    )(qkv, qkv, gq, gk, *tables, *dqs, *dks, *dvs)


ATTN_BQ = 2 * BAND
ATTN_ROWS = 16 * ATTN_BQ
V_COL = 2 * N_HEADS * HEAD_DIM // LANES


def _attn_geometry(S, d):
    rows = min(ATTN_ROWS, S)
    halo = BAND * d
    assert rows % (ATTN_BQ * d) == 0 and S % rows == 0, (S, d)
    return rows, halo, rows // (ATTN_BQ * d)


def _attn_specs(S, d, col):
    rows, halo, _ = _attn_geometry(S, d)
    r = rows // halo
    n_h = S // halo
    prev = pl.BlockSpec((halo, LANES), lambda j, i: (jnp.maximum(i * r - 1, 0), col + j))
    cur = pl.BlockSpec((rows, LANES), lambda j, i: (i, col + j))
    nxt = pl.BlockSpec((halo, LANES), lambda j, i: (jnp.minimum((i + 1) * r, n_h - 1), col + j))
    return [prev, cur, nxt]


def _fill_window(scr, prev, cur, nxt, rows, halo):
    scr[0:halo, :] = prev[...]
    scr[halo:halo + rows, :] = cur[...]
    scr[halo + rows:2 * halo + rows, :] = nxt[...]


def _chain_groups(n_sb, d, size):
    chains = [(sb, r) for sb in range(n_sb) for r in range(d)]
    return [chains[j:j + size] for j in range(0, len(chains), size)]


def _strided(ref, start, size, d):
    return ref[pl.ds(start, size, stride=d) if d > 1 else pl.ds(start, size), :]


def _band_mask(i, S, d, sb):
    rows, _, _ = _attn_geometry(S, d)
    L = S // d
    base = i * (rows // d) + sb * ATTN_BQ
    wk = ATTN_BQ + 2 * BAND
    row = lax.broadcasted_iota(jnp.int32, (ATTN_BQ, wk), 0)
    col = lax.broadcasted_iota(jnp.int32, (ATTN_BQ, wk), 1)
    lj = base - BAND + col
    return (jnp.abs(col - BAND - row) <= BAND) & (lj >= 0) & (lj < L)


def _attn_fwd(name, q, k, v, v_col, d):
    S, W = q.shape
    rows, halo, n_sb = _attn_geometry(S, d)
    wk = ATTN_BQ + 2 * BAND
    scale = HEAD_DIM ** -0.5

    def body(q_ref, kp, kc, kn, vp, vc, vn, o_ref, lse_ref, kw, vw):
        i = pl.program_id(1)
        _fill_window(kw, kp, kc, kn, rows, halo)
        _fill_window(vw, vp, vc, vn, rows, halo)
        first = _first_head((ATTN_BQ, LANES))
        heads = (first, jnp.logical_not(first))
        zero = jnp.zeros((), BF16)
        for group in _chain_groups(n_sb, d, 4):
            masks = {sb: _band_mask(i, S, d, sb) for sb in sorted({sb for sb, _ in group})}
            starts = [r + d * sb * ATTN_BQ for sb, r in group]
            qs = [_strided(q_ref, st, ATTN_BQ, d).astype(BF16) for st in starts]
            ks = [_strided(kw, st, wk, d).astype(BF16) for st in starts]
            vs = [_strided(vw, st, wk, d).astype(BF16) for st in starts]
            s_all = [[_dot(jnp.where(hm, qv, zero), kv, 1, 1) for hm in heads] for qv, kv in zip(qs, ks)]
            p_all, den_all, lse_all = [], [], []
            for (sb, _), s_h in zip(group, s_all):
                s_h = [jnp.where(masks[sb], s * scale, NEG) for s in s_h]
                mx_h = [jnp.max(s, axis=-1, keepdims=True) for s in s_h]
                p_h = [jnp.exp(s - mx) for s, mx in zip(s_h, mx_h)]
                den_h = [jnp.sum(p, axis=-1, keepdims=True) for p in p_h]
                p_all.append([p.astype(BF16) for p in p_h])
                den_all.append(den_h)
                lse_all.append([mx + jnp.log(den) for mx, den in zip(mx_h, den_h)])
            o_all = [[_dot(p, vv, 1, 0) for p in p_h] for p_h, vv in zip(p_all, vs)]
            for st, o_h, den_h, lse_h in zip(starts, o_all, den_all, lse_all):
                dst = pl.ds(st, ATTN_BQ, stride=d) if d > 1 else pl.ds(st, ATTN_BQ)
                o_ref[dst, :] = jnp.where(first, o_h[0] / den_h[0], o_h[1] / den_h[1])
                lse_ref[dst, :] = jnp.where(first, lse_h[0], lse_h[1])

    cur = _attn_specs(S, d, 0)[1]
    return pl.pallas_call(
        body, name=name, grid=(W // LANES, S // rows),
        in_specs=[cur] + _attn_specs(S, d, 0) + _attn_specs(S, d, v_col), out_specs=[cur, cur],
        out_shape=[jax.ShapeDtypeStruct((S, W), F32)] * 2,
        scratch_shapes=[pltpu.VMEM((rows + 2 * halo, LANES), F32)] * 2,
        compiler_params=_cp("parallel", "parallel"),
    )(q, k, k, k, v, v, v)


def _attn_merge(os, lses):
    S, W = os[0].shape
    tm = _tile(S, 256)
    n_p = len(os)

    def body(*refs):
        o_refs, l_refs = refs[:n_p], refs[n_p:2 * n_p]
        o_ref, lt_ref = refs[2 * n_p:]
        ls = [r[...] for r in l_refs]
        mx = functools.reduce(jnp.maximum, ls)
        es = [jnp.exp(l - mx) for l in ls]
        den = functools.reduce(lambda a, b: a + b, es)
        acc = es[0] * o_refs[0][...]
        for e, r in zip(es[1:], o_refs[1:]):
            acc = acc + e * r[...]
        o_ref[...] = (acc / den).astype(BF16)
        lt_ref[...] = mx + jnp.log(den)

    row = pl.BlockSpec((tm, W), lambda m: (m, 0))
    return pl.pallas_call(
        body, name="attn_merge", grid=(S // tm,), in_specs=[row] * (2 * n_p), out_specs=[row, row],
        out_shape=[jax.ShapeDtypeStruct((S, W), BF16), jax.ShapeDtypeStruct((S, W), F32)],
        compiler_params=_cp("parallel"),
    )(*os, *lses)


def _delta_epilogue(do, o):
    seg = _seg_matrix(1.0)
    prod = do * o.astype(F32)
    delta = [_seg_sum(prod[:, blk * LANES:(blk + 1) * LANES], seg) for blk in range(do.shape[1] // LANES)]
    return do, jnp.concatenate(delta, axis=1)


def _attn_bwd_q(name, q, k, v, v_col, do, lse, delta, d):
    S, W = q.shape
    rows, halo, n_sb = _attn_geometry(S, d)
    wk = ATTN_BQ + 2 * BAND
    scale = HEAD_DIM ** -0.5

    def body(q_ref, do_ref, l_ref, dl_ref, kp, kc, kn, vp, vc, vn, dq_ref, kw, vw):
        i = pl.program_id(1)
        _fill_window(kw, kp, kc, kn, rows, halo)
        _fill_window(vw, vp, vc, vn, rows, halo)
        first = _first_head((ATTN_BQ, LANES))
        heads = (first, jnp.logical_not(first))
        zero = jnp.zeros((), BF16)
        wide = lambda t: jnp.concatenate([t] * (wk // LANES), axis=1)
        for group in _chain_groups(n_sb, d, 4):
            masks = {sb: _band_mask(i, S, d, sb) for sb in sorted({sb for sb, _ in group})}
            starts = [r + d * sb * ATTN_BQ for sb, r in group]
            qs = [_strided(q_ref, st, ATTN_BQ, d).astype(BF16) for st in starts]
            dos = [_strided(do_ref, st, ATTN_BQ, d).astype(BF16) for st in starts]
            ks = [_strided(kw, st, wk, d).astype(BF16) for st in starts]
            vs = [_strided(vw, st, wk, d).astype(BF16) for st in starts]
            s_all = [[_dot(jnp.where(hm, qv, zero), kv, 1, 1) for hm in heads] for qv, kv in zip(qs, ks)]
            dp_all = [[_dot(jnp.where(hm, dov, zero), vv, 1, 1) for hm in heads] for dov, vv in zip(dos, vs)]
            ds_all = []
            for (sb, _), st, s_h, dp_h in zip(group, starts, s_all, dp_all):
                lv, dlv = _strided(l_ref, st, ATTN_BQ, d), _strided(dl_ref, st, ATTN_BQ, d)
                l_sw, dl_sw = pltpu.roll(lv, HEAD_DIM, 1), pltpu.roll(dlv, HEAD_DIM, 1)
                ds_h = []
                for hm, s, dp in zip(heads, s_h, dp_h):
                    p = jnp.exp(jnp.where(masks[sb], s * scale, NEG) - wide(jnp.where(hm, lv, l_sw)))
                    ds_h.append((p * (dp - wide(jnp.where(hm, dlv, dl_sw))) * scale).astype(BF16))
                ds_all.append(ds_h)
            dq_all = [[_dot(ds, kv, 1, 0) for ds in ds_h] for ds_h, kv in zip(ds_all, ks)]
            for st, dq_h in zip(starts, dq_all):
                dst = pl.ds(st, ATTN_BQ, stride=d) if d > 1 else pl.ds(st, ATTN_BQ)
                dq_ref[dst, :] = jnp.where(first, dq_h[0], dq_h[1])

    cur = _attn_specs(S, d, 0)[1]
    return pl.pallas_call(
        body, name=name, grid=(W // LANES, S // rows),
        in_specs=[cur] * 4 + _attn_specs(S, d, 0) + _attn_specs(S, d, v_col), out_specs=cur,
        out_shape=jax.ShapeDtypeStruct((S, W), F32),
        scratch_shapes=[pltpu.VMEM((rows + 2 * halo, LANES), F32)] * 2,
        compiler_params=_cp("parallel", "parallel"),
    )(q, do, lse, delta, k, k, k, v, v, v)


def _attn_bwd_kv(name, q, k, v, v_col, do, lse, delta, d):
    S, W = q.shape
    rows, halo, n_sb = _attn_geometry(S, d)
    wk = ATTN_BQ + 2 * BAND
    scale = HEAD_DIM ** -0.5

    def body(k_ref, v_ref, qp, qc, qn, dop, doc, don, lp, lc, ln, dlp, dlc, dln, dk_ref, dv_ref, qw, dow, lw, dlw):
        i = pl.program_id(1)
        _fill_window(qw, qp, qc, qn, rows, halo)
        _fill_window(dow, dop, doc, don, rows, halo)
        _fill_window(lw, lp, lc, ln, rows, halo)
        _fill_window(dlw, dlp, dlc, dln, rows, halo)
        first = _first_head((ATTN_BQ, LANES))
        heads = (first, jnp.logical_not(first))
        zero = jnp.zeros((), BF16)
        for group in _chain_groups(n_sb, d, 2):
            masks = {sb: _band_mask(i, S, d, sb) for sb in sorted({sb for sb, _ in group})}
            starts = [r + d * sb * ATTN_BQ for sb, r in group]
            ks = [_strided(k_ref, st, ATTN_BQ, d).astype(BF16) for st in starts]
            vs = [_strided(v_ref, st, ATTN_BQ, d).astype(BF16) for st in starts]
            qs = [_strided(qw, st, wk, d).astype(BF16) for st in starts]
            dos = [_strided(dow, st, wk, d).astype(BF16) for st in starts]
            s_all = [[_dot(jnp.where(hm, kv, zero), qv, 1, 1) for hm in heads] for kv, qv in zip(ks, qs)]
            dp_all = [[_dot(jnp.where(hm, vv, zero), dov, 1, 1) for hm in heads] for vv, dov in zip(vs, dos)]
            p_all, ds_all = [], []
            for (sb, _), st, s_h, dp_h in zip(group, starts, s_all, dp_all):
                l_t, dl_t = _strided(lw, st, wk, d).T, _strided(dlw, st, wk, d).T
                p_h = [jnp.exp(jnp.where(masks[sb], s * scale, NEG) - l_t[hh * HEAD_DIM:hh * HEAD_DIM + 1, :])
                       for hh, s in enumerate(s_h)]
                ds_all.append([(p * (dp - dl_t[hh * HEAD_DIM:hh * HEAD_DIM + 1, :]) * scale).astype(BF16)
                               for hh, (p, dp) in enumerate(zip(p_h, dp_h))])
                p_all.append([p.astype(BF16) for p in p_h])
            dv_all = [[_dot(p, dov, 1, 0) for p in p_h] for p_h, dov in zip(p_all, dos)]
            dk_all = [[_dot(ds, qv, 1, 0) for ds in ds_h] for ds_h, qv in zip(ds_all, qs)]
            for st, dk_h, dv_h in zip(starts, dk_all, dv_all):
                dst = pl.ds(st, ATTN_BQ, stride=d) if d > 1 else pl.ds(st, ATTN_BQ)
                dk_ref[dst, :] = jnp.where(first, dk_h[0], dk_h[1])
                dv_ref[dst, :] = jnp.where(first, dv_h[0], dv_h[1])

    cur = _attn_specs(S, d, 0)[1]
    win = _attn_specs(S, d, 0)
    return pl.pallas_call(
        body, name=name, grid=(W // LANES, S // rows),
        in_specs=[cur, _attn_specs(S, d, v_col)[1]] + win * 4, out_specs=[cur, cur],
        out_shape=[jax.ShapeDtypeStruct((S, W), F32)] * 2,
        scratch_shapes=[pltpu.VMEM((rows + 2 * halo, LANES), F32)] * 4,
        compiler_params=_cp("parallel", "parallel"),
    )(k, v, q, q, q, do, do, do, lse, lse, lse, delta, delta, delta)


def _place():
    x, y, c = lax.axis_index("x"), lax.axis_index("y"), lax.axis_index("c")
    chips = [(1 - x, y), (x, 1 - y), (1 - x, 1 - y)]
    return x, y, c, chips


HBM = pl.BlockSpec(memory_space=pltpu.HBM)
SEM = pl.BlockSpec(memory_space=pltpu.SEMAPHORE)
DATAFLOW = pltpu.SideEffectType.DATAFLOW_SIDE_EFFECTING


def _exchange_copies(kind, srcs, dsts, send_sems, recv_sems):
    x, y, c, chips = _place()
    mine = 2 * x + y
    cps = []
    for t in range(len(srcs)):
        for k, (px, py) in enumerate(chips):
            src = srcs[t] if kind == "gather" else srcs[t].at[2 * px + py]
            dst = dsts[t].at[mine] if kind == "gather" else dsts[t].at[k]
            cps.append(pltpu.make_async_remote_copy(src_ref=src, dst_ref=dst, send_sem=send_sems.at[3 * t + k],
                                                    recv_sem=recv_sems.at[3 * t + k], device_id=(px, py, c), device_id_type=MESH))
    return cps


def _exchange_start(name, kind, groups):
    sizes = [len(g) for g in groups]
    n, n_g = sum(sizes), len(groups)

    def body(*refs):
        srcs, dsts = refs[:n], refs[n:2 * n]
        sems = refs[2 * n:2 * n + 2 * n_g]
        token = refs[4 * n + 2 * n_g]
        off = 0
        for gi, size in enumerate(sizes):
            for cp in _exchange_copies(kind, srcs[off:off + size], dsts[off:off + size], sems[2 * gi], sems[2 * gi + 1]):
                cp.start()
            off += size
        token[...] = jnp.zeros_like(token)

    arrays = [pltpu.with_memory_space_constraint(a, pltpu.HBM) for a in
              [s for g in groups for s, _ in g] + [d for g in groups for _, d in g]]
    sem_shapes = []
    for size in sizes:
        sem_shapes += [pltpu.SemaphoreType.DMA((3 * size,))] * 2
    outs = pl.pallas_call(
        body, name=name,
        in_specs=[HBM] * (2 * n),
        out_specs=[SEM] * (2 * n_g) + [HBM] * (2 * n) + [pl.BlockSpec(memory_space=pltpu.VMEM)],
        out_shape=sem_shapes + [pltpu.HBM(a.shape, a.dtype) for a in arrays] + [jax.ShapeDtypeStruct((8, LANES), F32)],
        input_output_aliases={t: 2 * n_g + t for t in range(2 * n)},
        compiler_params=pltpu.CompilerParams(has_side_effects=DATAFLOW),
    )(*arrays)
    sems, thru, token = outs[:2 * n_g], outs[2 * n_g:-1], outs[-1]
    handles, off = [], 0
    for gi, size in enumerate(sizes):
        handles.append((sems[2 * gi], sems[2 * gi + 1], thru[off:off + size], thru[n + off:n + off + size]))
        off += size
    return handles, token


def _exchange_wait(name, kind, handle, after):
    send_sems, recv_sems, srcs, dsts = handle
    n = len(srcs)

    def body(*refs):
        for cp in _exchange_copies(kind, refs[:n], refs[n:2 * n], refs[2 * n], refs[2 * n + 1]):
            cp.wait_send()
            cp.wait_recv()

    outs = pl.pallas_call(
        body, name=name,
        in_specs=[HBM] * (2 * n) + [SEM, SEM, ANY], out_specs=[HBM] * (2 * n),
        out_shape=[pltpu.HBM(a.shape, a.dtype) for a in (*srcs, *dsts)],
        input_output_aliases={t: t for t in range(2 * n)},
        compiler_params=pltpu.CompilerParams(has_side_effects=DATAFLOW),
    )(*srcs, *dsts, send_sems, recv_sems, after)
    return outs[n:]


def _prepare_shard(name, w, idx, dtype, mine):
    _, R, C = w.shape
    tr = _row_tile(R)

    def body(mine_ref, w_ref, src_ref, land_ref):
        val = w_ref[...].astype(dtype)
        src_ref[...] = val
        land_ref[...] = val

    return pl.pallas_call(
        body, name=name,
        grid_spec=pltpu.PrefetchScalarGridSpec(
            num_scalar_prefetch=1, grid=(R // tr,),
            in_specs=[pl.BlockSpec((None, tr, C), lambda i, s: (idx, i, 0))],
            out_specs=[pl.BlockSpec((tr, C), lambda i, s: (i, 0)), pl.BlockSpec((None, tr, C), lambda i, s: (s[0], i, 0))]),
        out_shape=[jax.ShapeDtypeStruct((R, C), dtype), jax.ShapeDtypeStruct((N_SHARDS, R, C), dtype)],
        compiler_params=_cp("parallel"),
    )(mine, w)


def _swap_with_sibling(parts):
    n = len(parts)

    def body(*refs):
        ins, outs = refs[:n], refs[n:2 * n]
        send_sems, recv_sems = refs[2 * n:]
        x, y, c, _ = _place()
        cps = [pltpu.make_async_remote_copy(src_ref=ins[t], dst_ref=outs[t], send_sem=send_sems.at[t], recv_sem=recv_sems.at[t],
                                            device_id=(x, y, 1 - c), device_id_type=MESH) for t in range(n)]
        for cp in cps:
            cp.start()
        for cp in cps:
            cp.wait_recv()
        for cp in cps:
            cp.wait_send()

    return pl.pallas_call(
        body, name="swap_partial_grads", in_specs=[ANY] * n, out_specs=[ANY] * n,
        out_shape=[jax.ShapeDtypeStruct(p.shape, p.dtype) for p in parts],
        scratch_shapes=[pltpu.SemaphoreType.DMA((n,)), pltpu.SemaphoreType.DMA((n,))],
    )(*parts)


def _allreduce_small(v):
    rows = v.shape[0]

    def body(v_ref, out_ref, buf, send_sems, recv_sems):
        x, y, c, chips = _place()
        me, sibling = (x, y, c), (x, y, 1 - c)

        def slot(px, py, pc):
            return buf.at[4 * px + 2 * py + pc]

        def copy(k, block, to, src=None):
            return pltpu.make_async_remote_copy(
                src_ref=slot(*block) if src is None else src, dst_ref=slot(*block), send_sem=send_sems.at[k],
                recv_sem=recv_sems.at[k], device_id=to, device_id_type=MESH)

        slot(*me)[...] = v_ref[...]
        first = [copy(0, me, sibling, src=v_ref)] + [copy(1 + j, me, (*chip, c), src=v_ref) for j, chip in enumerate(chips)]
        for cp in first:
            cp.start()
        passed = [copy(4 + j, (*chip, c), sibling) for j, chip in enumerate(chips)]
        for j, chip in enumerate(chips):
            copy(1 + j, (*chip, c), me).wait_recv()
            passed[j].start()
        copy(0, sibling, me).wait_recv()
        for j, chip in enumerate(chips):
            copy(4 + j, (*chip, 1 - c), me).wait_recv()
        for cp in first + passed:
            cp.wait_send()
        acc = buf[0]
        for k in range(1, 8):
            acc = acc + buf[k]
        out_ref[...] = acc

    return pl.pallas_call(
        body, name="allreduce_small_grads",
        in_specs=[pl.BlockSpec(memory_space=pltpu.VMEM)], out_specs=pl.BlockSpec(memory_space=pltpu.VMEM),
        out_shape=jax.ShapeDtypeStruct((rows, LANES), F32),
        scratch_shapes=[pltpu.VMEM((8, rows, LANES), F32), pltpu.SemaphoreType.DMA((7,)), pltpu.SemaphoreType.DMA((7,))],
        compiler_params=pltpu.CompilerParams(vmem_limit_bytes=VMEM_LIMIT_BYTES),
    )(v)


MM_TM_K = 512
WGRAD_TM = 2048


def _rows_merged(w):
    return w.reshape(1, w.shape[0] * w.shape[1], w.shape[2])


def _sq_relu_epilogue(acc):
    r = jnp.maximum(acc, 0.0)
    return acc, r * r


def _add_epilogue(acc, x):
    return (acc + x,)


def _add_loss_epilogue(acc, x, target):
    e = acc + x - target
    D = e.shape[1]
    share = (0.5 / D) * jnp.sum(jnp.sum(e * e, axis=1, keepdims=True), axis=0, keepdims=True)
    return e * (1.0 / D), jnp.broadcast_to(share, (1, D))


def _add_norm_epilogue(acc, x, g):
    y = acc + x
    r = lax.rsqrt(jnp.mean(y * y, axis=-1, keepdims=True) + EPS)
    return y, y * r * g


def _norm_bwd_epilogue(dh, x, dres, g):
    r = lax.rsqrt(jnp.mean(x * x, axis=-1, keepdims=True) + EPS)
    xhat = x * r
    dxhat = dh * g
    dx = dres + r * (dxhat - xhat * jnp.mean(dxhat * xhat, axis=-1, keepdims=True))
    return dx, jnp.sum(dh * xhat, axis=0, keepdims=True)


def _sq_relu_grad_epilogue(acc, a):
    return (acc * (2.0 * jnp.maximum(a.astype(F32), 0.0)),)


STAGES = ("mixer_in", "mixer_out", "mlp")


def _stage_tensors(layer, stage):
    i = layer // 2
    if stage == "mlp":
        return [("mlp_w1", layer), ("mlp_w2", layer)]
    if stage == "mixer_in":
        return [("ab_w_in", i)] if layer % 2 == 0 else [("c_w_qkv", i)]
    return [("b_conv_w", i), ("ab_w_out", i)] if layer % 2 == 0 else [("c_w_out", i)]


def _local_step(x, target, p, weights_of, grads_done):
    S, D = x.shape
    depth = p["mix_norm_g"].shape[0]
    n_even = (depth + 1) // 2
    mix_g3 = p["mix_norm_g"].reshape(depth, 1, D)
    mlp_g3 = p["mlp_norm_g"].reshape(depth, 1, D)
    vec3 = lambda t: t.reshape(t.shape[0], 1, t.shape[1])
    spw16 = p["a_spatial_w"].astype(BF16)
    spw16_t = jnp.swapaxes(spw16, 2, 3)
    bias_full = jnp.repeat(jnp.swapaxes(p["a_spatial_b"], 1, 2), HEAD_DIM, axis=2)
    vn_g, vn_b, cn_g, cn_b, cb3 = (vec3(p[k]) for k in ("a_vnorm_g", "a_vnorm_b", "b_norm_g", "b_norm_b", "b_conv_b"))
    tables = _rope_tables(S)
    gq = jnp.tile(p["c_q_norm_g"], (1, 2))
    gk = jnp.tile(p["c_k_norm_g"], (1, 2))

    saved = []
    h = _rms_fwd("mix_norm_0", x, mix_g3, 0)
    for layer in range(depth):
        i = layer // 2
        wl = dict(weights_of(layer, "mixer_in", x))
        rec = {"x_mix": x, "w": wl, "h_mix": h}
        if layer % 2 == 0:
            (z,) = _mm_ngroup(f"ab_in_{layer}", h, wl["ab_w_in"], nt=False, tm=MM_TM_K, out_dtypes=[F32])
            wl.update(weights_of(layer, "mixer_out", z))
            gconv = _glu_conv_fwd(f"glu_conv_{layer}", z, wl["b_conv_w"], cb3, i)
            cat = _ab_tail_fwd(f"ab_tail_{layer}", z, gconv, spw16, bias_full, vn_g, vn_b, cn_g, cn_b, i)
            x, h = _mm_kgroup(f"ab_out_{layer}", cat, _rows_merged(wl["ab_w_out"]), nt=False, tm=MM_TM_K,
                              out_dtypes=[F32, BF16], extras=(x,), vecs=[(mlp_g3, layer)], epilogue=_add_norm_epilogue)
            rec.update(z=z, gconv=gconv, cat=cat)
        else:
            (qkv,) = _mm_ngroup(f"c_qkv_{layer}", h, wl["c_w_qkv"], nt=False, tm=MM_TM_K, out_dtypes=[F32])
            wl.update(weights_of(layer, "mixer_out", qkv))
            qn, kn = _qk_fwd(f"qk_norm_rope_{layer}", qkv, gq[i:i + 1], gk[i:i + 1], tables)
            os, lses = zip(*[_attn_fwd(f"attn_d{d}_{layer}", qn, kn, qkv, V_COL, d) for d in PATTERN_DILATIONS])
            o, lse = _attn_merge(os, lses)
            x, h = _mm_kgroup(f"c_out_{layer}", o, _rows_merged(wl["c_w_out"]), nt=False, tm=MM_TM_K,
                              out_dtypes=[F32, BF16], extras=(x,), vecs=[(mlp_g3, layer)], epilogue=_add_norm_epilogue)
            rec.update(qkv=qkv, qn=qn, kn=kn, o=o, lse=lse)
        rec["x_mlp"] = x
        wl.update(weights_of(layer, "mlp", x))
        a, hsq = _mm_ngroup(f"mlp_up_{layer}", h, wl["mlp_w1"], nt=False, tm=MM_TM_K, out_dtypes=[BF16, BF16],
                            epilogue=_sq_relu_epilogue)
        rec.update(h_mlp=h, a=a, hsq=hsq)
        if layer + 1 < depth:
            x, h = _mm_kgroup(f"mlp_down_{layer}", hsq, _rows_merged(wl["mlp_w2"]), nt=False, tm=MM_TM_K,
                              out_dtypes=[F32, BF16], extras=(x,), vecs=[(mix_g3, layer + 1)], epilogue=_add_norm_epilogue)
        else:
            dx, loss_row = _mm_kgroup(f"mlp_down_{layer}", hsq, _rows_merged(wl["mlp_w2"]), nt=False, tm=MM_TM_K,
                                      out_dtypes=[F32], extras=(x, target), n_sums=1, epilogue=_add_loss_epilogue)
        saved.append(rec)

    small = {k: [None] * v.shape[0] for k, v in p.items()}
    token = None
    for layer in reversed(range(depth)):
        i = layer // 2
        rec = saved[layer]
        wl = rec["w"]
        g = {}
        (da,) = _mm_ngroup(f"mlp_down_dgrad_{layer}", dx, wl["mlp_w2"], nt=True, tm=MM_TM_K, out_dtypes=[BF16],
                           extras=(rec["a"],), epilogue=_sq_relu_grad_epilogue, anchor=token)
        g["mlp_w2"] = _wgrad(f"mlp_down_wgrad_{layer}", rec["hsq"], dx, wl["mlp_w2"].shape, a_group=True, tm=WGRAD_TM)
        g["mlp_w1"] = _wgrad(f"mlp_up_wgrad_{layer}", rec["h_mlp"], da, wl["mlp_w1"].shape, a_group=False, tm=WGRAD_TM)
        dx, small["mlp_norm_g"][layer] = _mm_kgroup(
            f"mlp_up_dgrad_{layer}", da, wl["mlp_w1"], nt=True, tm=MM_TM_K, out_dtypes=[F32], extras=(rec["x_mlp"], dx),
            vecs=[(mlp_g3, layer)], n_sums=1, epilogue=_norm_bwd_epilogue)
        token = grads_done(layer, "mlp", g)
        g = {}
        if layer % 2 == 0:
            w_out = _rows_merged(wl["ab_w_out"])
            (dcat,) = _mm_ngroup(f"ab_out_dgrad_{layer}", dx, w_out, nt=True, tm=MM_TM_K, out_dtypes=[F32], anchor=token)
            g["ab_w_out"] = [t.reshape(wl["ab_w_out"].shape) for t in
                             _wgrad(f"ab_out_wgrad_{layer}", rec["cat"], dx, w_out.shape, a_group=True, tm=WGRAD_TM)]
            dz, dgconv, dspw, dbias, dvg, dvb, dcg, dcb = _ab_tail_bwd(
                f"ab_tail_bwd_{layer}", rec["z"], rec["gconv"], dcat, spw16, spw16_t, bias_full, vn_g, vn_b, cn_g, cn_b, i)
            dz, gf, gb, dcbias = _glu_conv_bwd(f"glu_conv_bwd_{layer}", rec["z"], dgconv, dz, wl["b_conv_w"])
            g["b_conv_w"] = (gf, gb)
            token = grads_done(layer, "mixer_out", g)
            g = {}
            small["a_spatial_w"][i] = dspw
            small["a_spatial_b"][i] = _fold_bias(dbias)[:, :A_GROUPS].T
            for k, val in (("a_vnorm_g", dvg), ("a_vnorm_b", dvb), ("b_norm_g", dcg), ("b_norm_b", dcb), ("b_conv_b", dcbias)):
                small[k][i] = val
            g["ab_w_in"] = _wgrad(f"ab_in_wgrad_{layer}", rec["h_mix"], dz, wl["ab_w_in"].shape, a_group=False, tm=WGRAD_TM)
            dgrad = (f"ab_in_dgrad_{layer}", dz, wl["ab_w_in"])
        else:
            w_out = _rows_merged(wl["c_w_out"])
            do, delta = _mm_ngroup(f"c_out_dgrad_{layer}", dx, w_out, nt=True, tm=MM_TM_K, out_dtypes=[F32, F32],
                                   extras=(rec["o"],), epilogue=_delta_epilogue, anchor=token)
            g["c_w_out"] = [t.reshape(wl["c_w_out"].shape) for t in
                            _wgrad(f"c_out_wgrad_{layer}", rec["o"], dx, w_out.shape, a_group=True, tm=WGRAD_TM)]
            token = grads_done(layer, "mixer_out", g)
            g = {}
            attn_args = (rec["qn"], rec["kn"], rec["qkv"], V_COL, do, rec["lse"], delta)
            dqs = [_attn_bwd_q(f"attn_bwd_q_d{d}_{layer}", *attn_args, d) for d in PATTERN_DILATIONS]
            dks, dvs = zip(*[_attn_bwd_kv(f"attn_bwd_kv_d{d}_{layer}", *attn_args, d) for d in PATTERN_DILATIONS])
            dqkv, dgq, dgk = _qk_bwd(f"qk_norm_rope_bwd_{layer}", rec["qkv"], gq[i:i + 1], gk[i:i + 1], tables, dqs, dks, dvs)
            small["c_q_norm_g"][i] = dgq[:, :HEAD_DIM]
            small["c_k_norm_g"][i] = dgk[:, :HEAD_DIM]
            g["c_w_qkv"] = _wgrad(f"c_qkv_wgrad_{layer}", rec["h_mix"], dqkv, wl["c_w_qkv"].shape, a_group=False, tm=WGRAD_TM)
            dgrad = (f"c_qkv_dgrad_{layer}", dqkv, wl["c_w_qkv"])
        dx, small["mix_norm_g"][layer] = _mm_kgroup(
            *dgrad, nt=True, tm=MM_TM_K, out_dtypes=[F32], extras=(rec["x_mix"], dx), vecs=[(mix_g3, layer)], n_sums=1,
            epilogue=_norm_bwd_epilogue, anchor=token)
        token = grads_done(layer, "mixer_in", g)

    small = {k: jnp.stack([t.reshape(p[k].shape[1:]) for t in v]) for k, v in small.items()}
    return loss_row, dx, small


SHARDED = ("mlp_w1", "mlp_w2", "ab_w_in", "b_conv_w", "ab_w_out", "c_w_qkv", "c_w_out")
SMALL = ("mix_norm_g", "mlp_norm_g", "a_spatial_w", "a_spatial_b", "a_vnorm_g", "a_vnorm_b", "b_conv_b", "b_norm_g",
         "b_norm_b", "c_q_norm_g", "c_k_norm_g")
WEIGHTS = ("mix_norm_g", "mlp_norm_g", "mlp_w1", "mlp_w2", "ab_w_in", "a_spatial_w", "a_spatial_b", "a_vnorm_g",
           "a_vnorm_b", "b_conv_w", "b_conv_b", "b_norm_g", "b_norm_b", "ab_w_out", "c_w_qkv", "c_q_norm_g",
           "c_k_norm_g", "c_w_out")


def _pack(parts):
    flat = jnp.concatenate([parts[k].reshape(-1) for k in SMALL])
    rows = -(-flat.shape[0] // (256 * LANES)) * 256
    return jnp.pad(flat, (0, rows * LANES - flat.shape[0])).reshape(rows, LANES)


def _unpack(packed, like):
    flat = packed.reshape(-1)
    out, off = {}, 0
    for k in SMALL:
        n = like[k].size
        out[k] = flat[off:off + n].reshape(like[k].shape)
        off += n
    return out


def kernel(x, mix_norm_g, mlp_norm_g, mlp_w1, mlp_w2, ab_w_in, a_spatial_w, a_spatial_b, a_vnorm_g, a_vnorm_b, b_conv_w, b_conv_b, b_norm_g, b_norm_b, ab_w_out, c_w_qkv, c_q_norm_g, c_k_norm_g, c_w_out, loss_target, m_mix_norm_g, m_mlp_norm_g, m_mlp_w1, m_mlp_w2, m_ab_w_in, m_a_spatial_w, m_a_spatial_b, m_a_vnorm_g, m_a_vnorm_b, m_b_conv_w, m_b_conv_b, m_b_norm_g, m_b_norm_b, m_ab_w_out, m_c_w_qkv, m_c_q_norm_g, m_c_k_norm_g, m_c_w_out, v_mix_norm_g, v_mlp_norm_g, v_mlp_w1, v_mlp_w2, v_ab_w_in, v_a_spatial_w, v_a_spatial_b, v_a_vnorm_g, v_a_vnorm_b, v_b_conv_w, v_b_conv_b, v_b_norm_g, v_b_norm_b, v_ab_w_out, v_c_w_qkv, v_c_q_norm_g, v_c_k_norm_g, v_c_w_out):
    w = dict(mix_norm_g=mix_norm_g, mlp_norm_g=mlp_norm_g, mlp_w1=mlp_w1, mlp_w2=mlp_w2, ab_w_in=ab_w_in,
             a_spatial_w=a_spatial_w, a_spatial_b=a_spatial_b, a_vnorm_g=a_vnorm_g, a_vnorm_b=a_vnorm_b,
             b_conv_w=b_conv_w, b_conv_b=b_conv_b, b_norm_g=b_norm_g, b_norm_b=b_norm_b, ab_w_out=ab_w_out,
             c_w_qkv=c_w_qkv, c_q_norm_g=c_q_norm_g, c_k_norm_g=c_k_norm_g, c_w_out=c_w_out)
    m = dict(mix_norm_g=m_mix_norm_g, mlp_norm_g=m_mlp_norm_g, mlp_w1=m_mlp_w1, mlp_w2=m_mlp_w2, ab_w_in=m_ab_w_in,
             a_spatial_w=m_a_spatial_w, a_spatial_b=m_a_spatial_b, a_vnorm_g=m_a_vnorm_g, a_vnorm_b=m_a_vnorm_b,
             b_conv_w=m_b_conv_w, b_conv_b=m_b_conv_b, b_norm_g=m_b_norm_g, b_norm_b=m_b_norm_b, ab_w_out=m_ab_w_out,
             c_w_qkv=m_c_w_qkv, c_q_norm_g=m_c_q_norm_g, c_k_norm_g=m_c_k_norm_g, c_w_out=m_c_w_out)
    v = dict(mix_norm_g=v_mix_norm_g, mlp_norm_g=v_mlp_norm_g, mlp_w1=v_mlp_w1, mlp_w2=v_mlp_w2, ab_w_in=v_ab_w_in,
             a_spatial_w=v_a_spatial_w, a_spatial_b=v_a_spatial_b, a_vnorm_g=v_a_vnorm_g, a_vnorm_b=v_a_vnorm_b,
             b_conv_w=v_b_conv_w, b_conv_b=v_b_conv_b, b_norm_g=v_b_norm_g, b_norm_b=v_b_norm_b, ab_w_out=v_ab_w_out,
             c_w_qkv=v_c_w_qkv, c_q_norm_g=v_c_q_norm_g, c_k_norm_g=v_c_k_norm_g, c_w_out=v_c_w_out)

    S, D = x.shape[1], x.shape[2]
    depth = mix_norm_g.shape[0]
    mine = (2 * lax.axis_index("x") + lax.axis_index("y")).astype(jnp.int32).reshape(1)

    stages = [(layer, stage) for layer in range(depth) for stage in STAGES]
    groups = [[(k, i) + tuple(_prepare_shard(f"prepare_{k}_{i}", w[k], i, F32 if k == "b_conv_w" else BF16, mine))
               for k, i in _stage_tensors(*st)] for st in stages]
    handles, gather_token = _exchange_start("gather_weights_start", "gather", [[(s, l) for _, _, s, l in g] for g in groups])
    handles = dict(zip(stages, handles))

    def weights_of(layer, stage, after):
        got = _exchange_wait(f"gather_weights_wait_{layer}_{stage}", "gather", handles[layer, stage],
                             gather_token if (layer, stage) == stages[0] else after)
        return {k: a for (k, _), a in zip(_stage_tensors(layer, stage), got)}

    scattered = {}

    def grads_done(layer, stage, g):
        names = [k for k, _ in _stage_tensors(layer, stage)]
        group = [(g[k][1], lax.empty((3,) + g[k][1].shape[1:], BF16)) for k in names]
        (handle,), token = _exchange_start(f"scatter_grads_start_{layer}_{stage}", "scatter", [group])
        scattered[layer, stage] = (handle, [g[k][0] for k in names])
        return token

    small_params = {k: w[k] for k in SMALL}
    loss_row, dx, small_grads = _local_step(x.reshape(S, D), loss_target.reshape(S, D), small_params, weights_of, grads_done)

    loss = lax.psum(loss_row[0, 0], ("x", "y", "c"))

    partial, order = [], []
    for layer, stage in reversed(stages):
        handle, gfs = scattered[layer, stage]
        recvs = _exchange_wait(f"scatter_grads_wait_{layer}_{stage}", "scatter", handle, dx)
        tensors = _stage_tensors(layer, stage)
        partial += [_sum4(f"sum_chips_{k}_{i}", gf, r, mine) for (k, i), gf, r in zip(tensors, gfs, recvs)]
        order += tensors
    other = _swap_with_sibling(partial)
    stacked = {k: [lax.empty(w[k].shape, F32) for _ in range(4)] for k in SHARDED}
    for (k, i), a, b in zip(order, partial, other):
        stacked[k] = _adamw_layer(f"adamw_{k}_{i}", w[k], m[k], v[k], i, a, b, stacked[k])
    grads, deltas, new_m, new_v = ({k: stacked[k][j] for k in SHARDED} for j in range(4))

    g_small = _allreduce_small(_pack(small_grads))
    outs = _adamw("adamw_small", _pack(small_params), _pack({k: m[k] for k in SMALL}), _pack({k: v[k] for k in SMALL}), g_small)
    for d_, packed in zip((grads, deltas, new_m, new_v), outs):
        d_.update(_unpack(packed, small_params))

    return (loss, dx.reshape(1, S, D), *[grads[k] for k in WEIGHTS], *[deltas[k] for k in WEIGHTS],
            *[new_m[k] for k in WEIGHTS], *[new_v[k] for k in WEIGHTS])
```

```python
import functools

import jax
import jax.numpy as jnp
from jax import lax
from jax.experimental import pallas as pl
from jax.experimental.pallas import tpu as pltpu

F32, BF16 = jnp.float32, jnp.bfloat16
MESH = pl.DeviceIdType.MESH
ANY = pl.BlockSpec(memory_space=pl.ANY)

VMEM_LIMIT_BYTES = 56 * 1024 * 1024
LANES = 128
ELEMENTWISE_ROWS = 256

EPS = 1e-6
NEG = -1e30
HEAD_DIM = 64
N_HEADS = 16
CHUNK = 128
A_GROUPS = 8
CONV_WIDTH = 31
CONV_HALO = 16
CONV_CHUNK = 64
BAND = 64
PATTERN_DILATIONS = (1, 4, 16)
ROT_DIM = 16
ROPE_THETA = 500000.0
N_SHARDS = 4

ADAM_LR, ADAM_B1, ADAM_B2, ADAM_EPS, ADAM_WD, ADAM_STEP = 0.001, 0.9, 0.999, 1e-08, 0.01, 10


def _cp(*sem):
    return pltpu.CompilerParams(dimension_semantics=sem, vmem_limit_bytes=VMEM_LIMIT_BYTES)


def _tile(n, pref):
    t = min(n, pref)
    assert n % t == 0, (n, pref)
    return t


def _dot(a, b, ca, cb):
    return lax.dot_general(a, b, (((ca,), (cb,)), ((), ())), preferred_element_type=F32)


def _mm_ngroup(name, a, w, *, nt, tm, out_dtypes, extras=(), epilogue=None, anchor=None):
    M, K = a.shape
    G, R, C = w.shape
    nw = R if nt else C
    assert K == (C if nt else R)
    tm = _tile(M, tm)
    n_ex = len(extras)
    anchors = [] if anchor is None else [anchor]

    def body(a_ref, w_ref, *rest):
        rest = rest[len(anchors):]
        av = a_ref[...].astype(BF16)
        for g in range(G):
            cols = slice(g * nw, (g + 1) * nw)
            acc = _dot(av, w_ref[g], 1, 1 if nt else 0)
            res = epilogue(acc, *[e[:, cols] for e in rest[:n_ex]]) if epilogue else (acc,)
            for o_ref, r in zip(rest[n_ex:], res):
                o_ref[:, cols] = r.astype(o_ref.dtype)

    blk = pl.BlockSpec((tm, G * nw), lambda m: (m, 0))
    return pl.pallas_call(
        body, name=name, grid=(M // tm,),
        in_specs=[pl.BlockSpec((tm, K), lambda m: (m, 0)), pl.BlockSpec((G, R, C), lambda m: (0, 0, 0))]
        + [pl.BlockSpec((8, LANES), lambda m: (0, 0))] * len(anchors) + [blk] * n_ex,
        out_specs=[blk] * len(out_dtypes),
        out_shape=[jax.ShapeDtypeStruct((M, G * nw), dt) for dt in out_dtypes],
        compiler_params=_cp("parallel"),
    )(a, w, *anchors, *extras)


def _mm_kgroup(name, a, w, *, nt, tm, out_dtypes, extras=(), vecs=(), n_sums=0, epilogue=None, anchor=None):
    G, R, C = w.shape
    kw, N = (C, R) if nt else (R, C)
    if a.ndim == 3:
        M = a.shape[1]
        assert a.shape[0] == G and a.shape[2] == kw
    else:
        M = a.shape[0]
        assert a.shape[1] == G * kw
    tm = _tile(M, tm)
    n_ex = len(extras)
    a_spec = (pl.BlockSpec((G, tm, kw), lambda m: (0, m, 0)) if a.ndim == 3 else pl.BlockSpec((tm, G * kw), lambda m: (m, 0)))
    anchors = [] if anchor is None else [anchor]

    def body(a_ref, w_ref, *rest):
        rest = rest[len(anchors):]
        acc = None
        for g in range(G):
            a_g = a_ref[g] if a.ndim == 3 else a_ref[:, g * kw:(g + 1) * kw]
            part = _dot(a_g.astype(BF16), w_ref[g], 1, 1 if nt else 0)
            acc = part if acc is None else acc + part
        n_in = n_ex + len(vecs)
        res = epilogue(acc, *[e[...] for e in rest[:n_in]]) if epilogue else (acc,)
        outs = rest[n_in:]
        n_tiles = len(outs) - n_sums
        for o_ref, r in zip(outs[:n_tiles], res[:n_tiles]):
            o_ref[...] = r.astype(o_ref.dtype)
        if n_sums:
            @pl.when(pl.program_id(0) == 0)
            def _():
                for s_ref in outs[n_tiles:]:
                    s_ref[...] = jnp.zeros_like(s_ref)

            for s_ref, r in zip(outs[n_tiles:], res[n_tiles:]):
                s_ref[...] += r

    blk = pl.BlockSpec((tm, N), lambda m: (m, 0))
    row = pl.BlockSpec((1, N), lambda m: (0, 0))
    return pl.pallas_call(
        body, name=name, grid=(M // tm,),
        in_specs=[a_spec, pl.BlockSpec((G, R, C), lambda m: (0, 0, 0))]
        + [pl.BlockSpec((8, LANES), lambda m: (0, 0))] * len(anchors) + [blk] * n_ex
        + [pl.BlockSpec((None, 1, N), lambda m, i=i: (i, 0, 0)) for _, i in vecs],
        out_specs=[blk] * len(out_dtypes) + [row] * n_sums,
        out_shape=[jax.ShapeDtypeStruct((M, N), dt) for dt in out_dtypes] + [jax.ShapeDtypeStruct((1, N), F32)] * n_sums,
        compiler_params=_cp("arbitrary" if n_sums else "parallel"),
    )(a, w, *anchors, *extras, *[v for v, _ in vecs])


def _wgrad(name, a, b, shape, *, a_group, tm, anchor=None):
    G, R, C = shape
    M = a.shape[0]
    tm = _tile(M, tm)
    n_m = M // tm
    anchors = [] if anchor is None else [anchor]

    def body(a_ref, b_ref, *rest):
        gf_ref, gb_ref = rest[len(anchors):]
        m = pl.program_id(1)
        part = _dot(a_ref[...].astype(BF16), b_ref[...].astype(BF16), 0, 0)

        @pl.when(m == 0)
        def _():
            gf_ref[...] = part

        @pl.when(m > 0)
        def _():
            gf_ref[...] += part

        @pl.when(m == n_m - 1)
        def _():
            gb_ref[...] = gf_ref[...].astype(BF16)

    a_spec = pl.BlockSpec((tm, R), (lambda g, m: (m, g)) if a_group else (lambda g, m: (m, 0)))
    if b.ndim == 3:
        assert not a_group
        b_spec = pl.BlockSpec((None, tm, C), lambda g, m: (g, m, 0))
    else:
        b_spec = pl.BlockSpec((tm, C), (lambda g, m: (m, 0)) if a_group else (lambda g, m: (m, g)))
    o_spec = pl.BlockSpec((None, R, C), lambda g, m: (g, 0, 0))
    return pl.pallas_call(
        body, name=name, grid=(G, n_m),
        in_specs=[a_spec, b_spec] + [pl.BlockSpec((8, LANES), lambda g, m: (0, 0))] * len(anchors), out_specs=[o_spec, o_spec],
        out_shape=[jax.ShapeDtypeStruct(shape, F32), jax.ShapeDtypeStruct(shape, BF16)],
        compiler_params=_cp("parallel", "arbitrary"),
    )(a, b, *anchors)


def _rms_fwd(name, x, g3, layer):
    S, D = x.shape
    tm = _tile(S, 512)

    def body(x_ref, g_ref, h_ref):
        xv = x_ref[...]
        r = lax.rsqrt(jnp.mean(xv * xv, axis=-1, keepdims=True) + EPS)
        h_ref[...] = (xv * r * g_ref[...]).astype(BF16)

    row = pl.BlockSpec((tm, D), lambda m: (m, 0))
    return pl.pallas_call(
        body, name=name, grid=(S // tm,),
        in_specs=[row, pl.BlockSpec((None, 1, D), lambda m: (layer, 0, 0))], out_specs=row,
        out_shape=jax.ShapeDtypeStruct((S, D), BF16), compiler_params=_cp("parallel"),
    )(x, g3)


def _adamw_math(w, m, v, g):
    m2 = ADAM_B1 * m + (1.0 - ADAM_B1) * g
    v2 = ADAM_B2 * v + (1.0 - ADAM_B2) * jnp.square(g)
    m_hat = m2 / (1.0 - ADAM_B1 ** ADAM_STEP)
    v_hat = v2 / (1.0 - ADAM_B2 ** ADAM_STEP)
    return g, -ADAM_LR * (m_hat / (jnp.sqrt(v_hat) + ADAM_EPS) + ADAM_WD * w), m2, v2


def _row_tile(rows):
    return _tile(rows, ELEMENTWISE_ROWS) if rows % ELEMENTWISE_ROWS == 0 else rows


def _adamw(name, w, m, v, g):
    rows, C = w.shape
    tr = _row_tile(rows)

    def body(w_ref, m_ref, v_ref, g_in, g_ref, d_ref, nm_ref, nv_ref):
        for o_ref, val in zip((g_ref, d_ref, nm_ref, nv_ref), _adamw_math(w_ref[...], m_ref[...], v_ref[...], g_in[...])):
            o_ref[...] = val

    blk = pl.BlockSpec((tr, C), lambda i: (i, 0))
    return pl.pallas_call(
        body, name=name, grid=(rows // tr,), in_specs=[blk] * 4, out_specs=[blk] * 4,
        out_shape=[jax.ShapeDtypeStruct((rows, C), F32)] * 4, compiler_params=_cp("parallel"),
    )(w, m, v, g)


def _adamw_layer(name, w, m, v, layer, mine, theirs, outs):
    _, R, C = w.shape
    tr = _row_tile(R)

    def body(w_ref, m_ref, v_ref, a_ref, b_ref, *rest):
        g = a_ref[...] + b_ref[...]
        for o_ref, val in zip(rest[4:], _adamw_math(w_ref[...], m_ref[...], v_ref[...], g)):
            o_ref[...] = val

    st = pl.BlockSpec((None, tr, C), lambda i: (layer, i, 0))
    part = pl.BlockSpec((tr, C), lambda i: (i, 0))
    return pl.pallas_call(
        body, name=name, grid=(R // tr,), in_specs=[st] * 3 + [part] * 2 + [ANY] * 4, out_specs=[st] * 4,
        out_shape=[jax.ShapeDtypeStruct(w.shape, F32)] * 4, input_output_aliases={5 + j: j for j in range(4)},
        compiler_params=_cp("parallel"),
    )(w, m, v, mine, theirs, *outs)


def _sum4(name, gf, recv, mine):
    _, R, C = gf.shape
    tr = _row_tile(R)

    def body(mine_ref, o_ref, r_ref, out_ref):
        acc = o_ref[...]
        for k in range(3):
            acc = acc + r_ref[k].astype(F32)
        out_ref[...] = acc

    return pl.pallas_call(
        body, name=name,
        grid_spec=pltpu.PrefetchScalarGridSpec(
            num_scalar_prefetch=1, grid=(R // tr,),
            in_specs=[pl.BlockSpec((None, tr, C), lambda i, s: (s[0], i, 0)), pl.BlockSpec((3, tr, C), lambda i, s: (0, i, 0))],
            out_specs=pl.BlockSpec((tr, C), lambda i, s: (i, 0))),
        out_shape=jax.ShapeDtypeStruct((R, C), F32), compiler_params=_cp("parallel"),
    )(mine, gf, recv)


def _gelu(x):
    return x * (0.5 * (1.0 + jnp.tanh(0.7978845608028654 * (x + 0.044715 * (x * x * x)))))


def _layernorm(t, g, b):
    mu = jnp.mean(t, axis=-1, keepdims=True)
    var = jnp.mean(jnp.square(t - mu), axis=-1, keepdims=True)
    return (t - mu) * lax.rsqrt(var + EPS) * g + b


def _silu(x):
    return x * jax.nn.sigmoid(x)


def _a_value(zv, g, b):
    return _layernorm(_gelu(zv), g, b)


def _b_tail(gc, g, b):
    return _silu(_layernorm(gc, g, b))


def _first_head(shape):
    return lax.broadcasted_iota(jnp.int32, shape, len(shape) - 1) < HEAD_DIM


def _spatial_mix(spw_ref, vb, tm):
    first = _first_head((CHUNK, LANES))
    rows = []
    for n in range(tm // CHUNK):
        blocks = []
        for j in range(A_GROUPS // 2):
            vblk = vb[n * CHUNK:(n + 1) * CHUNK, j * LANES:(j + 1) * LANES]
            r0 = _dot(spw_ref[2 * j], vblk, 1, 0)
            r1 = _dot(spw_ref[2 * j + 1], vblk, 1, 0)
            blocks.append(jnp.where(first, r0, r1))
        rows.append(jnp.concatenate(blocks, axis=1))
    return jnp.concatenate(rows, axis=0) if len(rows) > 1 else rows[0]


def _ab_tail_out_proj(name, z, gconv, spw, bias_full, vn_g, vn_b, cn_g, cn_b, layer, w, x, g3, g_layer):
    S = z.shape[0]
    AW = 512
    tm = _tile(S, 256)

    def body(zu_ref, zv_ref, gc_ref, spw_ref, bias_ref, vg_ref, vb_ref, cg_ref, cb_ref, w_ref, x_ref, g_ref,
             xo_ref, h_ref, cat_ref):
        u = _gelu(zu_ref[...])
        v = _a_value(zv_ref[...], vg_ref[...], vb_ref[...])
        sv = _spatial_mix(spw_ref, v.astype(BF16), tm) + jnp.tile(bias_ref[...], (tm // CHUNK, 1))
        cat = jnp.concatenate([(u * sv).astype(BF16), _b_tail(gc_ref[...], cg_ref[...], cb_ref[...]).astype(BF16)], axis=1)
        cat_ref[...] = cat
        y, h = _add_norm_epilogue(_dot(cat, w_ref[0], 1, 0), x_ref[...], g_ref[...])
        xo_ref[...] = y
        h_ref[...] = h.astype(BF16)

    vec = pl.BlockSpec((None, 1, AW), lambda m: (layer, 0, 0))
    row = pl.BlockSpec((tm, 2 * AW), lambda m: (m, 0))
    return pl.pallas_call(
        body, name=name, grid=(S // tm,),
        in_specs=[pl.BlockSpec((tm, AW), lambda m: (m, 0)), pl.BlockSpec((tm, AW), lambda m: (m, 1)),
                  pl.BlockSpec((tm, AW), lambda m: (m, 0)),
                  pl.BlockSpec((None, A_GROUPS, CHUNK, CHUNK), lambda m: (layer, 0, 0, 0)),
                  pl.BlockSpec((None, CHUNK, AW), lambda m: (layer, 0, 0)), vec, vec, vec, vec,
                  pl.BlockSpec(w.shape, lambda m: (0, 0, 0)), row, pl.BlockSpec((None, 1, 2 * AW), lambda m: (g_layer, 0, 0))],
        out_specs=[row] * 3,
        out_shape=[jax.ShapeDtypeStruct((S, 2 * AW), dt) for dt in (F32, BF16, BF16)], compiler_params=_cp("parallel"),
    )(z, z, gconv, spw, bias_full, vn_g, vn_b, cn_g, cn_b, w, x, g3)


def _ab_tail_bwd(name, z, gconv, dcat, spw, spw_t, bias_full, vn_g, vn_b, cn_g, cn_b, layer):
    S = z.shape[0]
    AW = 512
    tm = _tile(S, 256)
    n_chunks = tm // CHUNK

    def body(zu_ref, zv_ref, gc_ref, dcat_ref, spw_ref, spwt_ref, bias_ref, vg_ref, vb_ref, cg_ref, cb_ref,
             dz_ref, dgc_ref, dspw_ref, dbias_ref, dvg_ref, dvb_ref, dcg_ref, dcb_ref):
        @pl.when(pl.program_id(0) == 0)
        def _():
            for r in (dspw_ref, dbias_ref, dvg_ref, dvb_ref, dcg_ref, dcb_ref):
                r[...] = jnp.zeros_like(r)

        dya = dcat_ref[:, :AW]
        dyb = dcat_ref[:, AW:]
        u, u_vjp = jax.vjp(_gelu, zu_ref[...])
        v, v_vjp = jax.vjp(_a_value, zv_ref[...], vg_ref[...], vb_ref[...])
        vb16 = v.astype(BF16)
        sv = _spatial_mix(spw_ref, vb16, tm) + jnp.tile(bias_ref[...], (n_chunks, 1))
        (dzu,) = u_vjp(dya * sv)
        dsv = dya * u
        dsv16 = dsv.astype(BF16)
        dv = _spatial_mix(spwt_ref, dsv16, tm)
        dzv, dvg, dvb = v_vjp(dv)
        dz_ref[0] = dzu
        dz_ref[1] = dzv
        dvg_ref[...] += dvg
        dvb_ref[...] += dvb

        first = _first_head((CHUNK, LANES))
        zero = jnp.zeros((), BF16)
        dbias = jnp.zeros((CHUNK, AW), F32)
        for n in range(n_chunks):
            rows = slice(n * CHUNK, (n + 1) * CHUNK)
            dbias = dbias + dsv[rows]
            for j in range(A_GROUPS // 2):
                cols = slice(j * LANES, (j + 1) * LANES)
                dblk, vblk = dsv16[rows, cols], vb16[rows, cols]
                dspw_ref[2 * j] += _dot(jnp.where(first, dblk, zero), vblk, 1, 1)
                dspw_ref[2 * j + 1] += _dot(jnp.where(first, zero, dblk), vblk, 1, 1)
        dbias_ref[...] += dbias

        _, t_vjp = jax.vjp(_b_tail, gc_ref[...], cg_ref[...], cb_ref[...])
        dgc, dcg, dcb = t_vjp(dyb)
        dgc_ref[...] = dgc
        dcg_ref[...] += dcg
        dcb_ref[...] += dcb

    vec = pl.BlockSpec((None, 1, AW), lambda m: (layer, 0, 0))
    spw_spec = pl.BlockSpec((None, A_GROUPS, CHUNK, CHUNK), lambda m: (layer, 0, 0, 0))
    ovec = pl.BlockSpec((1, AW), lambda m: (0, 0))
    return pl.pallas_call(
        body, name=name, grid=(S // tm,),
        in_specs=[pl.BlockSpec((tm, AW), lambda m: (m, 0)), pl.BlockSpec((tm, AW), lambda m: (m, 1)),
                  pl.BlockSpec((tm, AW), lambda m: (m, 0)), pl.BlockSpec((tm, 2 * AW), lambda m: (m, 0)),
                  spw_spec, spw_spec, pl.BlockSpec((None, CHUNK, AW), lambda m: (layer, 0, 0)), vec, vec, vec, vec],
        out_specs=[pl.BlockSpec((2, tm, AW), lambda m: (0, m, 0)), pl.BlockSpec((tm, AW), lambda m: (m, 0)),
                   pl.BlockSpec((A_GROUPS, CHUNK, CHUNK), lambda m: (0, 0, 0)),
                   pl.BlockSpec((CHUNK, AW), lambda m: (0, 0)), ovec, ovec, ovec, ovec],
        out_shape=[jax.ShapeDtypeStruct((4, S, AW), F32), jax.ShapeDtypeStruct((S, AW), F32),
                   jax.ShapeDtypeStruct((A_GROUPS, CHUNK, CHUNK), F32), jax.ShapeDtypeStruct((CHUNK, AW), F32)]
                  + [jax.ShapeDtypeStruct((1, AW), F32)] * 4,
        compiler_params=_cp("arbitrary"),
    )(z, z, gconv, dcat, spw, spw_t, bias_full, vn_g, vn_b, cn_g, cn_b)


def _fold_bias(dbias_full):
    def body(d_ref, o_ref):
        d = d_ref[...]
        hi = d.astype(BF16)
        lo = (d - hi.astype(F32)).astype(BF16)
        r = lax.broadcasted_iota(jnp.int32, (512, LANES), 0)
        c = lax.broadcasted_iota(jnp.int32, (512, LANES), 1)
        fold = jnp.where(lax.shift_right_logical(r, 6) == c, 1.0, 0.0).astype(BF16)
        o_ref[...] = _dot(hi, fold, 1, 0) + _dot(lo, fold, 1, 0)

    return pl.pallas_call(body, name="fold_spatial_bias", out_shape=jax.ShapeDtypeStruct((CHUNK, LANES), F32))(dbias_full)


def _halo_specs(tm, n_halo_blocks, col):
    r = tm // CONV_HALO
    prev = pl.BlockSpec((CONV_HALO, LANES), lambda j, i: (jnp.maximum(i * r - 1, 0), col + j))
    cur = pl.BlockSpec((tm, LANES), lambda j, i: (i, col + j))
    nxt = pl.BlockSpec((CONV_HALO, LANES), lambda j, i: (jnp.minimum((i + 1) * r, n_halo_blocks - 1), col + j))
    return [prev, cur, nxt]


def _fill_halo(scr, prev, cur, nxt, tm, i, n_i):
    scr[0:CONV_HALO, :] = jnp.where(i > 0, prev, 0.0)
    scr[CONV_HALO:CONV_HALO + tm, :] = cur
    scr[CONV_HALO + tm:2 * CONV_HALO + tm, :] = jnp.where(i < n_i - 1, nxt, 0.0)


def _glu_conv_fwd(name, z, cw, cb3, layer):
    S = z.shape[0]
    tm = _tile(S, 512)
    n_i = S // tm
    pad = CONV_WIDTH // 2

    def body(vp, vc, vn, gp, gc, gn, w_ref, b_ref, out_ref, scr):
        i = pl.program_id(1)
        glu = lambda a, b: a[...] * jax.nn.sigmoid(b[...])
        _fill_halo(scr, glu(vp, gp), glu(vc, gc), glu(vn, gn), tm, i, n_i)
        taps = [w_ref[j:j + 1, :] for j in range(CONV_WIDTH)]
        for c0 in range(0, tm, CONV_CHUNK):
            acc = jnp.zeros((CONV_CHUNK, LANES), F32) + b_ref[...]
            for j in range(CONV_WIDTH):
                acc = acc + taps[j] * scr[pl.ds(c0 + CONV_HALO - pad + j, CONV_CHUNK), :]
            out_ref[pl.ds(c0, CONV_CHUNK), :] = acc

    return pl.pallas_call(
        body, name=name, grid=(4, n_i),
        in_specs=_halo_specs(tm, S // CONV_HALO, 8) + _halo_specs(tm, S // CONV_HALO, 12)
        + [pl.BlockSpec((None, CONV_WIDTH, LANES), lambda j, i: (j, 0, 0)),
           pl.BlockSpec((None, 1, LANES), lambda j, i: (layer, 0, j))],
        out_specs=pl.BlockSpec((tm, LANES), lambda j, i: (i, j)),
        out_shape=jax.ShapeDtypeStruct((S, 4 * LANES), F32),
        scratch_shapes=[pltpu.VMEM((tm + 2 * CONV_HALO, LANES), F32)],
        compiler_params=_cp("parallel", "parallel"),
    )(z, z, z, z, z, z, cw, cb3)


def _glu_conv_bwd(name, z, dgconv, dz, cw):
    S = z.shape[0]
    tm = _tile(S, 512)
    n_i = S // tm
    pad = CONV_WIDTH // 2

    def body(vp, vc, vn, gp, gc, gn, dp, dc, dn, w_ref, dz_in, dz_ref, gf_ref, gb_ref, db_ref, g_scr, d_scr):
        i = pl.program_id(1)
        sig = jax.nn.sigmoid(gc[...])
        _fill_halo(g_scr, vp[...] * jax.nn.sigmoid(gp[...]), vc[...] * sig, vn[...] * jax.nn.sigmoid(gn[...]), tm, i, n_i)
        _fill_halo(d_scr, dp[...], dc[...], dn[...], tm, i, n_i)

        @pl.when(i == 0)
        def _():
            gf_ref[...] = jnp.zeros_like(gf_ref)
            db_ref[...] = jnp.zeros_like(db_ref)

        taps = [w_ref[j:j + 1, :] for j in range(CONV_WIDTH)]
        dw = [jnp.zeros((8, LANES), F32) for _ in range(CONV_WIDTH)]
        db = jnp.zeros((8, LANES), F32)
        fold8 = lambda t: jnp.sum(t.reshape(CONV_CHUNK // 8, 8, LANES), axis=0)
        for c0 in range(0, tm, CONV_CHUNK):
            rows = pl.ds(c0, CONV_CHUNK)
            d_cur = dc[rows, :]
            dglu = jnp.zeros((CONV_CHUNK, LANES), F32)
            for j in range(CONV_WIDTH):
                dglu = dglu + taps[j] * d_scr[pl.ds(c0 + CONV_HALO + pad - j, CONV_CHUNK), :]
                dw[j] = dw[j] + fold8(d_cur * g_scr[pl.ds(c0 + CONV_HALO - pad + j, CONV_CHUNK), :])
            db = db + fold8(d_cur)
            sig_c = jax.nn.sigmoid(gc[rows, :])
            dz_ref[0, rows, :] = dglu * sig_c
            dz_ref[1, rows, :] = dglu * vc[rows, :] * sig_c * (1.0 - sig_c)
        for j in range(CONV_WIDTH):
            gf_ref[j:j + 1, :] += jnp.sum(dw[j], axis=0, keepdims=True)
        db_ref[...] += jnp.sum(db, axis=0, keepdims=True)

        @pl.when(i == n_i - 1)
        def _():
            gb_ref[...] = gf_ref[...].astype(BF16)

    w_spec = pl.BlockSpec((None, CONV_WIDTH, LANES), lambda j, i: (j, 0, 0))
    return pl.pallas_call(
        body, name=name, grid=(4, n_i),
        in_specs=_halo_specs(tm, S // CONV_HALO, 8) + _halo_specs(tm, S // CONV_HALO, 12)
        + _halo_specs(tm, S // CONV_HALO, 0) + [w_spec, ANY],
        out_specs=[pl.BlockSpec((2, tm, LANES), lambda j, i: (1, i, j)),
                   w_spec, w_spec, pl.BlockSpec((1, LANES), lambda j, i: (0, j))],
        out_shape=[jax.ShapeDtypeStruct(dz.shape, F32), jax.ShapeDtypeStruct(cw.shape, F32),
                   jax.ShapeDtypeStruct(cw.shape, BF16), jax.ShapeDtypeStruct((1, 4 * LANES), F32)],
        input_output_aliases={10: 0},
        scratch_shapes=[pltpu.VMEM((tm + 2 * CONV_HALO, LANES), F32)] * 2,
        compiler_params=_cp("parallel", "arbitrary"),
    )(z, z, z, z, z, z, dgconv, dgconv, dgconv, cw, dz)


def _seg_matrix(scale):
    r = lax.broadcasted_iota(jnp.int32, (LANES, LANES), 0)
    c = lax.broadcasted_iota(jnp.int32, (LANES, LANES), 1)
    return jnp.where(lax.shift_right_logical(r, 6) == lax.shift_right_logical(c, 6), scale, 0.0).astype(BF16)


def _seg_sum(x, seg):
    hi = x.astype(BF16)
    lo = (x - hi.astype(F32)).astype(BF16)
    return _dot(hi, seg, 1, 0) + _dot(lo, seg, 1, 0)


def _rope_tables(S):
    pos = jnp.arange(S, dtype=F32)
    inv_freq = ROPE_THETA ** (-jnp.arange(0, ROT_DIM, 2, dtype=F32) / ROT_DIM)
    ang = pos[:, None] * inv_freq[None, :]
    cos, sin = jnp.cos(ang), jnp.sin(ang)
    half = ROT_DIM // 2
    rest = HEAD_DIM - ROT_DIM
    one, zero = jnp.ones((S, rest), F32), jnp.zeros((S, rest), F32)
    zh = jnp.zeros((S, half), F32)
    c = jnp.concatenate([cos, cos, one], axis=1)
    sa = jnp.concatenate([-sin, zh, zero], axis=1)
    sb = jnp.concatenate([zh, sin, zero], axis=1)
    return [jnp.tile(t, (1, 2)) for t in (c, sa, sb)]


QK_CHUNK = 64


def _qk_fwd(name, qkv, gq, gk, tables):
    S = qkv.shape[0]
    W = N_HEADS * HEAD_DIM
    tm = _tile(S, 256)
    half = ROT_DIM // 2

    def body(q_ref, k_ref, gq_ref, gk_ref, c_ref, sa_ref, sb_ref, qn_ref, kn_ref):
        seg = _seg_matrix(1.0 / HEAD_DIM)
        for r0 in range(0, tm, QK_CHUNK):
            rows = pl.ds(r0, QK_CHUNK)
            c, sa, sb = c_ref[rows, :], sa_ref[rows, :], sb_ref[rows, :]
            for t_ref, g_ref, o_ref in ((q_ref, gq_ref, qn_ref), (k_ref, gk_ref, kn_ref)):
                for blk in range(W // LANES):
                    cols = slice(blk * LANES, (blk + 1) * LANES)
                    t = t_ref[rows, cols]
                    y = t * lax.rsqrt(_seg_sum(t * t, seg) + EPS) * g_ref[...]
                    o_ref[rows, cols] = y * c + pltpu.roll(y, LANES - half, 1) * sa + pltpu.roll(y, half, 1) * sb

    row = lambda k: pl.BlockSpec((tm, W), lambda m: (m, k))
    gain = pl.BlockSpec((1, LANES), lambda m: (0, 0))
    tab = pl.BlockSpec((tm, LANES), lambda m: (m, 0))
    return pl.pallas_call(
        body, name=name, grid=(S // tm,),
        in_specs=[row(0), row(1), gain, gain, tab, tab, tab], out_specs=[row(0)] * 2,
        out_shape=[jax.ShapeDtypeStruct((S, W), F32)] * 2, compiler_params=_cp("parallel"),
    )(qkv, qkv, gq, gk, *tables)


def _qk_bwd(name, qkv, gq, gk, tables, dqs, dks, dvs):
    S = qkv.shape[0]
    W = N_HEADS * HEAD_DIM
    tm = _tile(S, 256)
    half = ROT_DIM // 2
    n_p = len(dqs)

    def body(q_ref, k_ref, gq_ref, gk_ref, c_ref, sa_ref, sb_ref, *rest):
        dq_refs, dk_refs, dv_refs = rest[:n_p], rest[n_p:2 * n_p], rest[2 * n_p:3 * n_p]
        dqkv_ref, dgq_ref, dgk_ref = rest[3 * n_p:]

        @pl.when(pl.program_id(0) == 0)
        def _():
            dgq_ref[...] = jnp.zeros_like(dgq_ref)
            dgk_ref[...] = jnp.zeros_like(dgk_ref)

        seg = _seg_matrix(1.0 / HEAD_DIM)
        r_i = lax.broadcasted_iota(jnp.int32, (LANES, LANES), 0)
        c_i = lax.broadcasted_iota(jnp.int32, (LANES, LANES), 1)
        same_dim = jnp.where((r_i & (HEAD_DIM - 1)) == (c_i & (HEAD_DIM - 1)), 1.0, 0.0).astype(BF16)
        dgs = [jnp.zeros((8, LANES), F32), jnp.zeros((8, LANES), F32)]
        fold8 = lambda t: jnp.sum(t.reshape(QK_CHUNK // 8, 8, LANES), axis=0)
        for r0 in range(0, tm, QK_CHUNK):
            rows = pl.ds(r0, QK_CHUNK)
            c, sa, sb = c_ref[rows, :], sa_ref[rows, :], sb_ref[rows, :]
            for idx, (t_ref, g_ref, d_refs) in enumerate(((q_ref, gq_ref, dq_refs), (k_ref, gk_ref, dk_refs))):
                for blk in range(W // LANES):
                    cols = slice(blk * LANES, (blk + 1) * LANES)
                    dout = d_refs[0][rows, cols]
                    for r in d_refs[1:]:
                        dout = dout + r[rows, cols]
                    dy = dout * c + pltpu.roll(dout * sa, half, 1) + pltpu.roll(dout * sb, LANES - half, 1)
                    t = t_ref[rows, cols]
                    r_ = lax.rsqrt(_seg_sum(t * t, seg) + EPS)
                    xhat = t * r_
                    dgs[idx] = dgs[idx] + fold8(dy * xhat)
                    dxhat = dy * g_ref[...]
                    dt = r_ * (dxhat - xhat * _seg_sum(dxhat * xhat, seg))
                    dqkv_ref[rows, idx * W + blk * LANES: idx * W + (blk + 1) * LANES] = dt.astype(BF16)
            dv = dv_refs[0][rows, :]
            for r in dv_refs[1:]:
                dv = dv + r[rows, :]
            dqkv_ref[rows, 2 * W:] = dv.astype(BF16)
        for dg, dg_ref in zip(dgs, (dgq_ref, dgk_ref)):
            dg_ref[...] += jnp.sum(_seg_sum(dg, same_dim), axis=0, keepdims=True)

    row = lambda k: pl.BlockSpec((tm, W), lambda m: (m, k))
    gain = pl.BlockSpec((1, LANES), lambda m: (0, 0))
    tab = pl.BlockSpec((tm, LANES), lambda m: (m, 0))
    return pl.pallas_call(
        body, name=name, grid=(S // tm,),
        in_specs=[row(0), row(1), gain, gain, tab, tab, tab] + [row(0)] * (3 * n_p),
        out_specs=[pl.BlockSpec((tm, 3 * W), lambda m: (m, 0)), gain, gain],
        out_shape=[jax.ShapeDtypeStruct((S, 3 * W), BF16), jax.ShapeDtypeStruct((1, LANES), F32),
                   jax.ShapeDtypeStruct((1, LANES), F32)],
        compiler_params=_cp("arbitrary"),
    )(qkv, qkv, gq, gk, *tables, *dqs, *dks, *dvs)


ATTN_BQ = 2 * BAND
ATTN_ROWS = 16 * ATTN_BQ
V_COL = 2 * N_HEADS * HEAD_DIM // LANES


def _attn_geometry(S, d):
    rows = min(ATTN_ROWS, S)
    halo = BAND * d
    assert rows % (ATTN_BQ * d) == 0 and S % rows == 0, (S, d)
    return rows, halo, rows // (ATTN_BQ * d)


def _attn_specs(S, d, col):
    rows, halo, _ = _attn_geometry(S, d)
    r = rows // halo
    n_h = S // halo
    prev = pl.BlockSpec((halo, LANES), lambda j, i: (jnp.maximum(i * r - 1, 0), col + j))
    cur = pl.BlockSpec((rows, LANES), lambda j, i: (i, col + j))
    nxt = pl.BlockSpec((halo, LANES), lambda j, i: (jnp.minimum((i + 1) * r, n_h - 1), col + j))
    return [prev, cur, nxt]


def _fill_window(scr, prev, cur, nxt, rows, halo):
    scr[0:halo, :] = prev[...]
    scr[halo:halo + rows, :] = cur[...]
    scr[halo + rows:2 * halo + rows, :] = nxt[...]


def _chain_groups(n_sb, d, size):
    chains = [(sb, r) for sb in range(n_sb) for r in range(d)]
    return [chains[j:j + size] for j in range(0, len(chains), size)]


def _strided(ref, start, size, d):
    return ref[pl.ds(start, size, stride=d) if d > 1 else pl.ds(start, size), :]


def _band_mask(i, S, d, sb):
    rows, _, _ = _attn_geometry(S, d)
    L = S // d
    base = i * (rows // d) + sb * ATTN_BQ
    wk = ATTN_BQ + 2 * BAND
    row = lax.broadcasted_iota(jnp.int32, (ATTN_BQ, wk), 0)
    col = lax.broadcasted_iota(jnp.int32, (ATTN_BQ, wk), 1)
    lj = base - BAND + col
    return (jnp.abs(col - BAND - row) <= BAND) & (lj >= 0) & (lj < L)


def _attn_fwd(name, q, k, v, v_col, d):
    S, W = q.shape
    rows, halo, n_sb = _attn_geometry(S, d)
    wk = ATTN_BQ + 2 * BAND
    scale = HEAD_DIM ** -0.5

    def body(q_ref, kp, kc, kn, vp, vc, vn, o_ref, lse_ref, kw, vw):
        i = pl.program_id(1)
        _fill_window(kw, kp, kc, kn, rows, halo)
        _fill_window(vw, vp, vc, vn, rows, halo)
        first = _first_head((ATTN_BQ, LANES))
        heads = (first, jnp.logical_not(first))
        zero = jnp.zeros((), BF16)
        for group in _chain_groups(n_sb, d, 4):
            masks = {sb: _band_mask(i, S, d, sb) for sb in sorted({sb for sb, _ in group})}
            starts = [r + d * sb * ATTN_BQ for sb, r in group]
            qs = [_strided(q_ref, st, ATTN_BQ, d).astype(BF16) for st in starts]
            ks = [_strided(kw, st, wk, d).astype(BF16) for st in starts]
            vs = [_strided(vw, st, wk, d).astype(BF16) for st in starts]
            s_all = [[_dot(jnp.where(hm, qv, zero), kv, 1, 1) for hm in heads] for qv, kv in zip(qs, ks)]
            p_all, den_all, lse_all = [], [], []
            for (sb, _), s_h in zip(group, s_all):
                s_h = [jnp.where(masks[sb], s * scale, NEG) for s in s_h]
                mx_h = [jnp.max(s, axis=-1, keepdims=True) for s in s_h]
                p_h = [jnp.exp(s - mx) for s, mx in zip(s_h, mx_h)]
                den_h = [jnp.sum(p, axis=-1, keepdims=True) for p in p_h]
                p_all.append([p.astype(BF16) for p in p_h])
                den_all.append(den_h)
                lse_all.append([mx + jnp.log(den) for mx, den in zip(mx_h, den_h)])
            o_all = [[_dot(p, vv, 1, 0) for p in p_h] for p_h, vv in zip(p_all, vs)]
            for st, o_h, den_h, lse_h in zip(starts, o_all, den_all, lse_all):
                dst = pl.ds(st, ATTN_BQ, stride=d) if d > 1 else pl.ds(st, ATTN_BQ)
                o_ref[dst, :] = jnp.where(first, o_h[0] / den_h[0], o_h[1] / den_h[1])
                lse_ref[dst, :] = jnp.where(first, lse_h[0], lse_h[1])

    cur = _attn_specs(S, d, 0)[1]
    return pl.pallas_call(
        body, name=name, grid=(W // LANES, S // rows),
        in_specs=[cur] + _attn_specs(S, d, 0) + _attn_specs(S, d, v_col), out_specs=[cur, cur],
        out_shape=[jax.ShapeDtypeStruct((S, W), F32)] * 2,
        scratch_shapes=[pltpu.VMEM((rows + 2 * halo, LANES), F32)] * 2,
        compiler_params=_cp("parallel", "parallel"),
    )(q, k, k, k, v, v, v)


def _merge_out_proj(name, os, lses, w, x, g3, layer):
    S, W = os[0].shape
    tm = _tile(S, 256)
    n_p = len(os)

    def body(*refs):
        o_refs, l_refs = refs[:n_p], refs[n_p:2 * n_p]
        w_ref, x_ref, g_ref, xo_ref, h_ref, o_ref, lt_ref = refs[2 * n_p:]
        ls = [r[...] for r in l_refs]
        mx = functools.reduce(jnp.maximum, ls)
        es = [jnp.exp(l - mx) for l in ls]
        den = functools.reduce(lambda a, b: a + b, es)
        acc = es[0] * o_refs[0][...]
        for e, r in zip(es[1:], o_refs[1:]):
            acc = acc + e * r[...]
        o = (acc / den).astype(BF16)
        o_ref[...] = o
        lt_ref[...] = mx + jnp.log(den)
        y, h = _add_norm_epilogue(_dot(o, w_ref[0], 1, 0), x_ref[...], g_ref[...])
        xo_ref[...] = y
        h_ref[...] = h.astype(BF16)

    row = pl.BlockSpec((tm, W), lambda m: (m, 0))
    return pl.pallas_call(
        body, name=name, grid=(S // tm,),
        in_specs=[row] * (2 * n_p) + [pl.BlockSpec(w.shape, lambda m: (0, 0, 0)), row,
                                      pl.BlockSpec((None, 1, W), lambda m: (layer, 0, 0))],
        out_specs=[row] * 4,
        out_shape=[jax.ShapeDtypeStruct((S, W), dt) for dt in (F32, BF16, BF16, F32)],
        compiler_params=_cp("parallel"),
    )(*os, *lses, w, x, g3)


def _delta_epilogue(do, o):
    seg = _seg_matrix(1.0)
    prod = do * o.astype(F32)
    delta = [_seg_sum(prod[:, blk * LANES:(blk + 1) * LANES], seg) for blk in range(do.shape[1] // LANES)]
    return do, jnp.concatenate(delta, axis=1)


def _attn_bwd_q(name, q, k, v, v_col, do, lse, delta, d):
    S, W = q.shape
    rows, halo, n_sb = _attn_geometry(S, d)
    wk = ATTN_BQ + 2 * BAND
    scale = HEAD_DIM ** -0.5

    def body(q_ref, do_ref, l_ref, dl_ref, kp, kc, kn, vp, vc, vn, dq_ref, kw, vw):
        i = pl.program_id(1)
        _fill_window(kw, kp, kc, kn, rows, halo)
        _fill_window(vw, vp, vc, vn, rows, halo)
        first = _first_head((ATTN_BQ, LANES))
        heads = (first, jnp.logical_not(first))
        zero = jnp.zeros((), BF16)
        wide = lambda t: jnp.concatenate([t] * (wk // LANES), axis=1)
        for group in _chain_groups(n_sb, d, 4):
            masks = {sb: _band_mask(i, S, d, sb) for sb in sorted({sb for sb, _ in group})}
            starts = [r + d * sb * ATTN_BQ for sb, r in group]
            qs = [_strided(q_ref, st, ATTN_BQ, d).astype(BF16) for st in starts]
            dos = [_strided(do_ref, st, ATTN_BQ, d).astype(BF16) for st in starts]
            ks = [_strided(kw, st, wk, d).astype(BF16) for st in starts]
            vs = [_strided(vw, st, wk, d).astype(BF16) for st in starts]
            s_all = [[_dot(jnp.where(hm, qv, zero), kv, 1, 1) for hm in heads] for qv, kv in zip(qs, ks)]
            dp_all = [[_dot(jnp.where(hm, dov, zero), vv, 1, 1) for hm in heads] for dov, vv in zip(dos, vs)]
            ds_all = []
            for (sb, _), st, s_h, dp_h in zip(group, starts, s_all, dp_all):
                lv, dlv = _strided(l_ref, st, ATTN_BQ, d), _strided(dl_ref, st, ATTN_BQ, d)
                l_sw, dl_sw = pltpu.roll(lv, HEAD_DIM, 1), pltpu.roll(dlv, HEAD_DIM, 1)
                ds_h = []
                for hm, s, dp in zip(heads, s_h, dp_h):
                    p = jnp.exp(jnp.where(masks[sb], s * scale, NEG) - wide(jnp.where(hm, lv, l_sw)))
                    ds_h.append((p * (dp - wide(jnp.where(hm, dlv, dl_sw))) * scale).astype(BF16))
                ds_all.append(ds_h)
            dq_all = [[_dot(ds, kv, 1, 0) for ds in ds_h] for ds_h, kv in zip(ds_all, ks)]
            for st, dq_h in zip(starts, dq_all):
                dst = pl.ds(st, ATTN_BQ, stride=d) if d > 1 else pl.ds(st, ATTN_BQ)
                dq_ref[dst, :] = jnp.where(first, dq_h[0], dq_h[1])

    cur = _attn_specs(S, d, 0)[1]
    return pl.pallas_call(
        body, name=name, grid=(W // LANES, S // rows),
        in_specs=[cur] * 4 + _attn_specs(S, d, 0) + _attn_specs(S, d, v_col), out_specs=cur,
        out_shape=jax.ShapeDtypeStruct((S, W), F32),
        scratch_shapes=[pltpu.VMEM((rows + 2 * halo, LANES), F32)] * 2,
        compiler_params=_cp("parallel", "parallel"),
    )(q, do, lse, delta, k, k, k, v, v, v)


def _attn_bwd_kv(name, q, k, v, v_col, do, lse, delta, d):
    S, W = q.shape
    rows, halo, n_sb = _attn_geometry(S, d)
    wk = ATTN_BQ + 2 * BAND
    scale = HEAD_DIM ** -0.5

    def body(k_ref, v_ref, qp, qc, qn, dop, doc, don, lp, lc, ln, dlp, dlc, dln, dk_ref, dv_ref, qw, dow, lw, dlw):
        i = pl.program_id(1)
        _fill_window(qw, qp, qc, qn, rows, halo)
        _fill_window(dow, dop, doc, don, rows, halo)
        _fill_window(lw, lp, lc, ln, rows, halo)
        _fill_window(dlw, dlp, dlc, dln, rows, halo)
        first = _first_head((ATTN_BQ, LANES))
        heads = (first, jnp.logical_not(first))
        zero = jnp.zeros((), BF16)
        for group in _chain_groups(n_sb, d, 2):
            masks = {sb: _band_mask(i, S, d, sb) for sb in sorted({sb for sb, _ in group})}
            starts = [r + d * sb * ATTN_BQ for sb, r in group]
            ks = [_strided(k_ref, st, ATTN_BQ, d).astype(BF16) for st in starts]
            vs = [_strided(v_ref, st, ATTN_BQ, d).astype(BF16) for st in starts]
            qs = [_strided(qw, st, wk, d).astype(BF16) for st in starts]
            dos = [_strided(dow, st, wk, d).astype(BF16) for st in starts]
            s_all = [[_dot(jnp.where(hm, kv, zero), qv, 1, 1) for hm in heads] for kv, qv in zip(ks, qs)]
            dp_all = [[_dot(jnp.where(hm, vv, zero), dov, 1, 1) for hm in heads] for vv, dov in zip(vs, dos)]
            p_all, ds_all = [], []
            for (sb, _), st, s_h, dp_h in zip(group, starts, s_all, dp_all):
                l_t, dl_t = _strided(lw, st, wk, d).T, _strided(dlw, st, wk, d).T
                p_h = [jnp.exp(jnp.where(masks[sb], s * scale, NEG) - l_t[hh * HEAD_DIM:hh * HEAD_DIM + 1, :])
                       for hh, s in enumerate(s_h)]
                ds_all.append([(p * (dp - dl_t[hh * HEAD_DIM:hh * HEAD_DIM + 1, :]) * scale).astype(BF16)
                               for hh, (p, dp) in enumerate(zip(p_h, dp_h))])
                p_all.append([p.astype(BF16) for p in p_h])
            dv_all = [[_dot(p, dov, 1, 0) for p in p_h] for p_h, dov in zip(p_all, dos)]
            dk_all = [[_dot(ds, qv, 1, 0) for ds in ds_h] for ds_h, qv in zip(ds_all, qs)]
            for st, dk_h, dv_h in zip(starts, dk_all, dv_all):
                dst = pl.ds(st, ATTN_BQ, stride=d) if d > 1 else pl.ds(st, ATTN_BQ)
                dk_ref[dst, :] = jnp.where(first, dk_h[0], dk_h[1])
                dv_ref[dst, :] = jnp.where(first, dv_h[0], dv_h[1])

    cur = _attn_specs(S, d, 0)[1]
    win = _attn_specs(S, d, 0)
    return pl.pallas_call(
        body, name=name, grid=(W // LANES, S // rows),
        in_specs=[cur, _attn_specs(S, d, v_col)[1]] + win * 4, out_specs=[cur, cur],
        out_shape=[jax.ShapeDtypeStruct((S, W), F32)] * 2,
        scratch_shapes=[pltpu.VMEM((rows + 2 * halo, LANES), F32)] * 4,
        compiler_params=_cp("parallel", "parallel"),
    )(k, v, q, q, q, do, do, do, lse, lse, lse, delta, delta, delta)


def _place():
    x, y, c = lax.axis_index("x"), lax.axis_index("y"), lax.axis_index("c")
    chips = [(1 - x, y), (x, 1 - y), (1 - x, 1 - y)]
    return x, y, c, chips


HBM = pl.BlockSpec(memory_space=pltpu.HBM)
SEM = pl.BlockSpec(memory_space=pltpu.SEMAPHORE)
DATAFLOW = pltpu.SideEffectType.DATAFLOW_SIDE_EFFECTING


def _exchange_copies(kind, srcs, dsts, send_sems, recv_sems):
    x, y, c, chips = _place()
    mine = 2 * x + y
    cps = []
    for t in range(len(srcs)):
        for k, (px, py) in enumerate(chips):
            src = srcs[t] if kind == "gather" else srcs[t].at[2 * px + py]
            dst = dsts[t].at[mine] if kind == "gather" else dsts[t].at[k]
            cps.append(pltpu.make_async_remote_copy(src_ref=src, dst_ref=dst, send_sem=send_sems.at[3 * t + k],
                                                    recv_sem=recv_sems.at[3 * t + k], device_id=(px, py, c), device_id_type=MESH))
    return cps


def _exchange_start(name, kind, groups):
    sizes = [len(g) for g in groups]
    n, n_g = sum(sizes), len(groups)

    def body(*refs):
        srcs, dsts = refs[:n], refs[n:2 * n]
        sems = refs[2 * n:2 * n + 2 * n_g]
        token = refs[4 * n + 2 * n_g]
        off = 0
        for gi, size in enumerate(sizes):
            for cp in _exchange_copies(kind, srcs[off:off + size], dsts[off:off + size], sems[2 * gi], sems[2 * gi + 1]):
                cp.start()
            off += size
        token[...] = jnp.zeros_like(token)

    arrays = [pltpu.with_memory_space_constraint(a, pltpu.HBM) for a in
              [s for g in groups for s, _ in g] + [d for g in groups for _, d in g]]
    sem_shapes = []
    for size in sizes:
        sem_shapes += [pltpu.SemaphoreType.DMA((3 * size,))] * 2
    outs = pl.pallas_call(
        body, name=name,
        in_specs=[HBM] * (2 * n),
        out_specs=[SEM] * (2 * n_g) + [HBM] * (2 * n) + [pl.BlockSpec(memory_space=pltpu.VMEM)],
        out_shape=sem_shapes + [pltpu.HBM(a.shape, a.dtype) for a in arrays] + [jax.ShapeDtypeStruct((8, LANES), F32)],
        input_output_aliases={t: 2 * n_g + t for t in range(2 * n)},
        compiler_params=pltpu.CompilerParams(has_side_effects=DATAFLOW),
    )(*arrays)
    sems, thru, token = outs[:2 * n_g], outs[2 * n_g:-1], outs[-1]
    handles, off = [], 0
    for gi, size in enumerate(sizes):
        handles.append((sems[2 * gi], sems[2 * gi + 1], thru[off:off + size], thru[n + off:n + off + size]))
        off += size
    return handles, token


def _exchange_wait(name, kind, handle, after):
    send_sems, recv_sems, srcs, dsts = handle
    n = len(srcs)

    def body(*refs):
        for cp in _exchange_copies(kind, refs[:n], refs[n:2 * n], refs[2 * n], refs[2 * n + 1]):
            cp.wait_send()
            cp.wait_recv()

    outs = pl.pallas_call(
        body, name=name,
        in_specs=[HBM] * (2 * n) + [SEM, SEM, ANY], out_specs=[HBM] * (2 * n),
        out_shape=[pltpu.HBM(a.shape, a.dtype) for a in (*srcs, *dsts)],
        input_output_aliases={t: t for t in range(2 * n)},
        compiler_params=pltpu.CompilerParams(has_side_effects=DATAFLOW),
    )(*srcs, *dsts, send_sems, recv_sems, after)
    return outs[n:]


def _prepare_shard(name, w, idx, dtype, mine, anchor=None):
    _, R, C = w.shape
    tr = _row_tile(R)
    anchors = [] if anchor is None else [anchor]

    def body(mine_ref, w_ref, *rest):
        src_ref, land_ref = rest[len(anchors):]
        val = w_ref[...].astype(dtype)
        src_ref[...] = val
        land_ref[...] = val

    return pl.pallas_call(
        body, name=name,
        grid_spec=pltpu.PrefetchScalarGridSpec(
            num_scalar_prefetch=1, grid=(R // tr,),
            in_specs=[pl.BlockSpec((None, tr, C), lambda i, s: (idx, i, 0))]
            + [pl.BlockSpec((8, LANES), lambda i, s: (0, 0))] * len(anchors),
            out_specs=[pl.BlockSpec((tr, C), lambda i, s: (i, 0)), pl.BlockSpec((None, tr, C), lambda i, s: (s[0], i, 0))]),
        out_shape=[jax.ShapeDtypeStruct((R, C), dtype), jax.ShapeDtypeStruct((N_SHARDS, R, C), dtype)],
        compiler_params=_cp("parallel"),
    )(mine, w, *anchors)


def _swap_with_sibling(parts):
    n = len(parts)

    def body(*refs):
        ins, outs = refs[:n], refs[n:2 * n]
        send_sems, recv_sems = refs[2 * n:]
        x, y, c, _ = _place()
        cps = [pltpu.make_async_remote_copy(src_ref=ins[t], dst_ref=outs[t], send_sem=send_sems.at[t], recv_sem=recv_sems.at[t],
                                            device_id=(x, y, 1 - c), device_id_type=MESH) for t in range(n)]
        for cp in cps:
            cp.start()
        for cp in cps:
            cp.wait_recv()
        for cp in cps:
            cp.wait_send()

    return pl.pallas_call(
        body, name="swap_partial_grads", in_specs=[ANY] * n, out_specs=[ANY] * n,
        out_shape=[jax.ShapeDtypeStruct(p.shape, p.dtype) for p in parts],
        scratch_shapes=[pltpu.SemaphoreType.DMA((n,)), pltpu.SemaphoreType.DMA((n,))],
    )(*parts)


def _allreduce_small(v):
    rows = v.shape[0]

    def body(v_ref, out_ref, buf, send_sems, recv_sems):
        x, y, c, chips = _place()
        me, sibling = (x, y, c), (x, y, 1 - c)

        def slot(px, py, pc):
            return buf.at[4 * px + 2 * py + pc]

        def copy(k, block, to, src=None):
            return pltpu.make_async_remote_copy(
                src_ref=slot(*block) if src is None else src, dst_ref=slot(*block), send_sem=send_sems.at[k],
                recv_sem=recv_sems.at[k], device_id=to, device_id_type=MESH)

        slot(*me)[...] = v_ref[...]
        first = [copy(0, me, sibling, src=v_ref)] + [copy(1 + j, me, (*chip, c), src=v_ref) for j, chip in enumerate(chips)]
        for cp in first:
            cp.start()
        passed = [copy(4 + j, (*chip, c), sibling) for j, chip in enumerate(chips)]
        for j, chip in enumerate(chips):
            copy(1 + j, (*chip, c), me).wait_recv()
            passed[j].start()
        copy(0, sibling, me).wait_recv()
        for j, chip in enumerate(chips):
            copy(4 + j, (*chip, 1 - c), me).wait_recv()
        for cp in first + passed:
            cp.wait_send()
        acc = buf[0]
        for k in range(1, 8):
            acc = acc + buf[k]
        out_ref[...] = acc

    return pl.pallas_call(
        body, name="allreduce_small_grads",
        in_specs=[pl.BlockSpec(memory_space=pltpu.VMEM)], out_specs=pl.BlockSpec(memory_space=pltpu.VMEM),
        out_shape=jax.ShapeDtypeStruct((rows, LANES), F32),
        scratch_shapes=[pltpu.VMEM((8, rows, LANES), F32), pltpu.SemaphoreType.DMA((7,)), pltpu.SemaphoreType.DMA((7,))],
        compiler_params=pltpu.CompilerParams(vmem_limit_bytes=VMEM_LIMIT_BYTES),
    )(v)


MM_TM_K = 512
WGRAD_TM = 2048


def _rows_merged(w):
    return w.reshape(1, w.shape[0] * w.shape[1], w.shape[2])


def _sq_relu_epilogue(acc):
    r = jnp.maximum(acc, 0.0)
    return acc, r * r


def _add_epilogue(acc, x):
    return (acc + x,)


def _add_loss_epilogue(acc, x, target):
    e = acc + x - target
    D = e.shape[1]
    share = (0.5 / D) * jnp.sum(jnp.sum(e * e, axis=1, keepdims=True), axis=0, keepdims=True)
    return e * (1.0 / D), jnp.broadcast_to(share, (1, D))


def _add_norm_epilogue(acc, x, g):
    y = acc + x
    r = lax.rsqrt(jnp.mean(y * y, axis=-1, keepdims=True) + EPS)
    return y, y * r * g


def _norm_bwd_epilogue(dh, x, dres, g):
    r = lax.rsqrt(jnp.mean(x * x, axis=-1, keepdims=True) + EPS)
    xhat = x * r
    dxhat = dh * g
    dx = dres + r * (dxhat - xhat * jnp.mean(dxhat * xhat, axis=-1, keepdims=True))
    return dx, jnp.sum(dh * xhat, axis=0, keepdims=True)


def _sq_relu_grad_epilogue(acc, a):
    return (acc * (2.0 * jnp.maximum(a.astype(F32), 0.0)),)


STAGES = ("mixer_in", "mixer_out", "mlp")


def _stage_tensors(layer, stage):
    i = layer // 2
    if stage == "mlp":
        return [("mlp_w1", layer), ("mlp_w2", layer)]
    if stage == "mixer_in":
        return [("ab_w_in", i)] if layer % 2 == 0 else [("c_w_qkv", i)]
    return [("b_conv_w", i), ("ab_w_out", i)] if layer % 2 == 0 else [("c_w_out", i)]


def _local_step(x, target, p, weights_of, grads_done):
    S, D = x.shape
    depth = p["mix_norm_g"].shape[0]
    n_even = (depth + 1) // 2
    mix_g3 = p["mix_norm_g"].reshape(depth, 1, D)
    mlp_g3 = p["mlp_norm_g"].reshape(depth, 1, D)
    vec3 = lambda t: t.reshape(t.shape[0], 1, t.shape[1])
    spw16 = p["a_spatial_w"].astype(BF16)
    spw16_t = jnp.swapaxes(spw16, 2, 3)
    bias_full = jnp.repeat(jnp.swapaxes(p["a_spatial_b"], 1, 2), HEAD_DIM, axis=2)
    vn_g, vn_b, cn_g, cn_b, cb3 = (vec3(p[k]) for k in ("a_vnorm_g", "a_vnorm_b", "b_norm_g", "b_norm_b", "b_conv_b"))
    tables = _rope_tables(S)
    gq = jnp.tile(p["c_q_norm_g"], (1, 2))
    gk = jnp.tile(p["c_k_norm_g"], (1, 2))

    saved = []
    h = _rms_fwd("mix_norm_0", x, mix_g3, 0)
    for layer in range(depth):
        i = layer // 2
        wl = dict(weights_of(layer, "mixer_in", x))
        rec = {"x_mix": x, "w": wl, "h_mix": h}
        if layer % 2 == 0:
            (z,) = _mm_ngroup(f"ab_in_{layer}", h, wl["ab_w_in"], nt=False, tm=MM_TM_K, out_dtypes=[F32])
            wl.update(weights_of(layer, "mixer_out", z))
            gconv = _glu_conv_fwd(f"glu_conv_{layer}", z, wl["b_conv_w"], cb3, i)
            x, h, cat = _ab_tail_out_proj(f"ab_out_{layer}", z, gconv, spw16, bias_full, vn_g, vn_b, cn_g, cn_b, i,
                                          _rows_merged(wl["ab_w_out"]), x, mlp_g3, layer)
            rec.update(z=z, gconv=gconv, cat=cat)
        else:
            (qkv,) = _mm_ngroup(f"c_qkv_{layer}", h, wl["c_w_qkv"], nt=False, tm=MM_TM_K, out_dtypes=[F32])
            wl.update(weights_of(layer, "mixer_out", qkv))
            qn, kn = _qk_fwd(f"qk_norm_rope_{layer}", qkv, gq[i:i + 1], gk[i:i + 1], tables)
            os, lses = zip(*[_attn_fwd(f"attn_d{d}_{layer}", qn, kn, qkv, V_COL, d) for d in PATTERN_DILATIONS])
            x, h, o, lse = _merge_out_proj(f"c_out_{layer}", os, lses, _rows_merged(wl["c_w_out"]), x, mlp_g3, layer)
            rec.update(qkv=qkv, qn=qn, kn=kn, o=o, lse=lse)
        rec["x_mlp"] = x
        wl.update(weights_of(layer, "mlp", x))
        a, hsq = _mm_ngroup(f"mlp_up_{layer}", h, wl["mlp_w1"], nt=False, tm=MM_TM_K, out_dtypes=[BF16, BF16],
                            epilogue=_sq_relu_epilogue)
        rec.update(h_mlp=h, a=a, hsq=hsq)
        if layer + 1 < depth:
            x, h = _mm_kgroup(f"mlp_down_{layer}", hsq, _rows_merged(wl["mlp_w2"]), nt=False, tm=MM_TM_K,
                              out_dtypes=[F32, BF16], extras=(x,), vecs=[(mix_g3, layer + 1)], epilogue=_add_norm_epilogue)
        else:
            dx, loss_row = _mm_kgroup(f"mlp_down_{layer}", hsq, _rows_merged(wl["mlp_w2"]), nt=False, tm=MM_TM_K,
                                      out_dtypes=[F32], extras=(x, target), n_sums=1, epilogue=_add_loss_epilogue)
        saved.append(rec)

    small = {k: [None] * v.shape[0] for k, v in p.items()}
    token = None
    for layer in reversed(range(depth)):
        i = layer // 2
        rec = saved[layer]
        wl = rec["w"]
        g = {}
        (da,) = _mm_ngroup(f"mlp_down_dgrad_{layer}", dx, wl["mlp_w2"], nt=True, tm=MM_TM_K, out_dtypes=[BF16],
                           extras=(rec["a"],), epilogue=_sq_relu_grad_epilogue, anchor=token)
        g["mlp_w2"] = _wgrad(f"mlp_down_wgrad_{layer}", rec["hsq"], dx, wl["mlp_w2"].shape, a_group=True, tm=WGRAD_TM)
        g["mlp_w1"] = _wgrad(f"mlp_up_wgrad_{layer}", rec["h_mlp"], da, wl["mlp_w1"].shape, a_group=False, tm=WGRAD_TM)
        dx, small["mlp_norm_g"][layer] = _mm_kgroup(
            f"mlp_up_dgrad_{layer}", da, wl["mlp_w1"], nt=True, tm=MM_TM_K, out_dtypes=[F32], extras=(rec["x_mlp"], dx),
            vecs=[(mlp_g3, layer)], n_sums=1, epilogue=_norm_bwd_epilogue)
        token = grads_done(layer, "mlp", g)
        g = {}
        if layer % 2 == 0:
            w_out = _rows_merged(wl["ab_w_out"])
            (dcat,) = _mm_ngroup(f"ab_out_dgrad_{layer}", dx, w_out, nt=True, tm=MM_TM_K, out_dtypes=[F32], anchor=token)
            g["ab_w_out"] = [t.reshape(wl["ab_w_out"].shape) for t in
                             _wgrad(f"ab_out_wgrad_{layer}", rec["cat"], dx, w_out.shape, a_group=True, tm=WGRAD_TM)]
            dz, dgconv, dspw, dbias, dvg, dvb, dcg, dcb = _ab_tail_bwd(
                f"ab_tail_bwd_{layer}", rec["z"], rec["gconv"], dcat, spw16, spw16_t, bias_full, vn_g, vn_b, cn_g, cn_b, i)
            dz, gf, gb, dcbias = _glu_conv_bwd(f"glu_conv_bwd_{layer}", rec["z"], dgconv, dz, wl["b_conv_w"])
            g["b_conv_w"] = (gf, gb)
            token = grads_done(layer, "mixer_out", g)
            g = {}
            small["a_spatial_w"][i] = dspw
            small["a_spatial_b"][i] = _fold_bias(dbias)[:, :A_GROUPS].T
            for k, val in (("a_vnorm_g", dvg), ("a_vnorm_b", dvb), ("b_norm_g", dcg), ("b_norm_b", dcb), ("b_conv_b", dcbias)):
                small[k][i] = val
            g["ab_w_in"] = _wgrad(f"ab_in_wgrad_{layer}", rec["h_mix"], dz, wl["ab_w_in"].shape, a_group=False, tm=WGRAD_TM,
                                  anchor=token)
            dgrad = (f"ab_in_dgrad_{layer}", dz, wl["ab_w_in"])
        else:
            w_out = _rows_merged(wl["c_w_out"])
            do, delta = _mm_ngroup(f"c_out_dgrad_{layer}", dx, w_out, nt=True, tm=MM_TM_K, out_dtypes=[F32, F32],
                                   extras=(rec["o"],), epilogue=_delta_epilogue, anchor=token)
            g["c_w_out"] = [t.reshape(wl["c_w_out"].shape) for t in
                            _wgrad(f"c_out_wgrad_{layer}", rec["o"], dx, w_out.shape, a_group=True, tm=WGRAD_TM)]
            token = grads_done(layer, "mixer_out", g)
            g = {}
            attn_args = (rec["qn"], rec["kn"], rec["qkv"], V_COL, do, rec["lse"], delta)
            dqs = [_attn_bwd_q(f"attn_bwd_q_d{d}_{layer}", *attn_args, d) for d in PATTERN_DILATIONS]
            dks, dvs = zip(*[_attn_bwd_kv(f"attn_bwd_kv_d{d}_{layer}", *attn_args, d) for d in PATTERN_DILATIONS])
            dqkv, dgq, dgk = _qk_bwd(f"qk_norm_rope_bwd_{layer}", rec["qkv"], gq[i:i + 1], gk[i:i + 1], tables, dqs, dks, dvs)
            small["c_q_norm_g"][i] = dgq[:, :HEAD_DIM]
            small["c_k_norm_g"][i] = dgk[:, :HEAD_DIM]
            g["c_w_qkv"] = _wgrad(f"c_qkv_wgrad_{layer}", rec["h_mix"], dqkv, wl["c_w_qkv"].shape, a_group=False, tm=WGRAD_TM,
                                  anchor=token)
            dgrad = (f"c_qkv_dgrad_{layer}", dqkv, wl["c_w_qkv"])
        token = grads_done(layer, "mixer_in", g)
        dx, small["mix_norm_g"][layer] = _mm_kgroup(
            *dgrad, nt=True, tm=MM_TM_K, out_dtypes=[F32], extras=(rec["x_mix"], dx), vecs=[(mix_g3, layer)], n_sums=1,
            epilogue=_norm_bwd_epilogue, anchor=token)

    small = {k: jnp.stack([t.reshape(p[k].shape[1:]) for t in v]) for k, v in small.items()}
    return loss_row, dx, small


SHARDED = ("mlp_w1", "mlp_w2", "ab_w_in", "b_conv_w", "ab_w_out", "c_w_qkv", "c_w_out")
SMALL = ("mix_norm_g", "mlp_norm_g", "a_spatial_w", "a_spatial_b", "a_vnorm_g", "a_vnorm_b", "b_conv_b", "b_norm_g",
         "b_norm_b", "c_q_norm_g", "c_k_norm_g")
WEIGHTS = ("mix_norm_g", "mlp_norm_g", "mlp_w1", "mlp_w2", "ab_w_in", "a_spatial_w", "a_spatial_b", "a_vnorm_g",
           "a_vnorm_b", "b_conv_w", "b_conv_b", "b_norm_g", "b_norm_b", "ab_w_out", "c_w_qkv", "c_q_norm_g",
           "c_k_norm_g", "c_w_out")


def _pack(parts):
    flat = jnp.concatenate([parts[k].reshape(-1) for k in SMALL])
    rows = -(-flat.shape[0] // (256 * LANES)) * 256
    return jnp.pad(flat, (0, rows * LANES - flat.shape[0])).reshape(rows, LANES)


def _unpack(packed, like):
    flat = packed.reshape(-1)
    out, off = {}, 0
    for k in SMALL:
        n = like[k].size
        out[k] = flat[off:off + n].reshape(like[k].shape)
        off += n
    return out


def kernel(x, mix_norm_g, mlp_norm_g, mlp_w1, mlp_w2, ab_w_in, a_spatial_w, a_spatial_b, a_vnorm_g, a_vnorm_b, b_conv_w, b_conv_b, b_norm_g, b_norm_b, ab_w_out, c_w_qkv, c_q_norm_g, c_k_norm_g, c_w_out, loss_target, m_mix_norm_g, m_mlp_norm_g, m_mlp_w1, m_mlp_w2, m_ab_w_in, m_a_spatial_w, m_a_spatial_b, m_a_vnorm_g, m_a_vnorm_b, m_b_conv_w, m_b_conv_b, m_b_norm_g, m_b_norm_b, m_ab_w_out, m_c_w_qkv, m_c_q_norm_g, m_c_k_norm_g, m_c_w_out, v_mix_norm_g, v_mlp_norm_g, v_mlp_w1, v_mlp_w2, v_ab_w_in, v_a_spatial_w, v_a_spatial_b, v_a_vnorm_g, v_a_vnorm_b, v_b_conv_w, v_b_conv_b, v_b_norm_g, v_b_norm_b, v_ab_w_out, v_c_w_qkv, v_c_q_norm_g, v_c_k_norm_g, v_c_w_out):
    w = dict(mix_norm_g=mix_norm_g, mlp_norm_g=mlp_norm_g, mlp_w1=mlp_w1, mlp_w2=mlp_w2, ab_w_in=ab_w_in,
             a_spatial_w=a_spatial_w, a_spatial_b=a_spatial_b, a_vnorm_g=a_vnorm_g, a_vnorm_b=a_vnorm_b,
             b_conv_w=b_conv_w, b_conv_b=b_conv_b, b_norm_g=b_norm_g, b_norm_b=b_norm_b, ab_w_out=ab_w_out,
             c_w_qkv=c_w_qkv, c_q_norm_g=c_q_norm_g, c_k_norm_g=c_k_norm_g, c_w_out=c_w_out)
    m = dict(mix_norm_g=m_mix_norm_g, mlp_norm_g=m_mlp_norm_g, mlp_w1=m_mlp_w1, mlp_w2=m_mlp_w2, ab_w_in=m_ab_w_in,
             a_spatial_w=m_a_spatial_w, a_spatial_b=m_a_spatial_b, a_vnorm_g=m_a_vnorm_g, a_vnorm_b=m_a_vnorm_b,
             b_conv_w=m_b_conv_w, b_conv_b=m_b_conv_b, b_norm_g=m_b_norm_g, b_norm_b=m_b_norm_b, ab_w_out=m_ab_w_out,
             c_w_qkv=m_c_w_qkv, c_q_norm_g=m_c_q_norm_g, c_k_norm_g=m_c_k_norm_g, c_w_out=m_c_w_out)
    v = dict(mix_norm_g=v_mix_norm_g, mlp_norm_g=v_mlp_norm_g, mlp_w1=v_mlp_w1, mlp_w2=v_mlp_w2, ab_w_in=v_ab_w_in,
             a_spatial_w=v_a_spatial_w, a_spatial_b=v_a_spatial_b, a_vnorm_g=v_a_vnorm_g, a_vnorm_b=v_a_vnorm_b,
             b_conv_w=v_b_conv_w, b_conv_b=v_b_conv_b, b_norm_g=v_b_norm_g, b_norm_b=v_b_norm_b, ab_w_out=v_ab_w_out,
             c_w_qkv=v_c_w_qkv, c_q_norm_g=v_c_q_norm_g, c_k_norm_g=v_c_k_norm_g, c_w_out=v_c_w_out)

    S, D = x.shape[1], x.shape[2]
    depth = mix_norm_g.shape[0]
    mine = (2 * lax.axis_index("x") + lax.axis_index("y")).astype(jnp.int32).reshape(1)

    stages = [(layer, stage) for layer in range(depth) for stage in STAGES]

    def start_gather(name, some_stages, anchor):
        groups = [[_prepare_shard(f"prepare_{k}_{i}", w[k], i, F32 if k == "b_conv_w" else BF16, mine, anchor)
                   for k, i in _stage_tensors(*st)] for st in some_stages]
        return _exchange_start(name, "gather", groups)

    first, rest = stages[:len(STAGES)], stages[len(STAGES):]
    handles_first, token_first = start_gather("gather_weights_start_first", first, None)
    handles_rest, gather_token = start_gather("gather_weights_start_rest", rest, token_first)
    handles = dict(zip(first + rest, handles_first + handles_rest))

    def weights_of(layer, stage, after):
        got = _exchange_wait(f"gather_weights_wait_{layer}_{stage}", "gather", handles[layer, stage],
                             gather_token if (layer, stage) == stages[0] else after)
        return {k: a for (k, _), a in zip(_stage_tensors(layer, stage), got)}

    scattered = {}

    def grads_done(layer, stage, g):
        names = [k for k, _ in _stage_tensors(layer, stage)]
        group = [(g[k][1], lax.empty((3,) + g[k][1].shape[1:], BF16)) for k in names]
        (handle,), token = _exchange_start(f"scatter_grads_start_{layer}_{stage}", "scatter", [group])
        scattered[layer, stage] = (handle, [g[k][0] for k in names])
        return token

    small_params = {k: w[k] for k in SMALL}
    loss_row, dx, small_grads = _local_step(x.reshape(S, D), loss_target.reshape(S, D), small_params, weights_of, grads_done)

    loss = lax.psum(loss_row[0, 0], ("x", "y", "c"))

    partial, order = [], []
    for layer, stage in reversed(stages):
        handle, gfs = scattered[layer, stage]
        recvs = _exchange_wait(f"scatter_grads_wait_{layer}_{stage}", "scatter", handle, dx)
        tensors = _stage_tensors(layer, stage)
        partial += [_sum4(f"sum_chips_{k}_{i}", gf, r, mine) for (k, i), gf, r in zip(tensors, gfs, recvs)]
        order += tensors
    other = _swap_with_sibling(partial)
    stacked = {k: [lax.empty(w[k].shape, F32) for _ in range(4)] for k in SHARDED}
    for (k, i), a, b in zip(order, partial, other):
        stacked[k] = _adamw_layer(f"adamw_{k}_{i}", w[k], m[k], v[k], i, a, b, stacked[k])
    grads, deltas, new_m, new_v = ({k: stacked[k][j] for k in SHARDED} for j in range(4))

    g_small = _allreduce_small(_pack(small_grads))
    outs = _adamw("adamw_small", _pack(small_params), _pack({k: m[k] for k in SMALL}), _pack({k: v[k] for k in SMALL}), g_small)
    for d_, packed in zip((grads, deltas, new_m, new_v), outs):
        d_.update(_unpack(packed, small_params))

    return (loss, dx.reshape(1, S, D), *[grads[k] for k in WEIGHTS], *[deltas[k] for k in WEIGHTS],
            *[new_m[k] for k in WEIGHTS], *[new_v[k] for k in WEIGHTS])
```

```python
import functools

import jax
import jax.numpy as jnp
from jax import lax
from jax.experimental import pallas as pl
from jax.experimental.pallas import tpu as pltpu

F32, BF16 = jnp.float32, jnp.bfloat16
MESH = pl.DeviceIdType.MESH
ANY = pl.BlockSpec(memory_space=pl.ANY)

VMEM_LIMIT_BYTES = 56 * 1024 * 1024
LANES = 128
ELEMENTWISE_ROWS = 256

EPS = 1e-6
NEG = -1e30
HEAD_DIM = 64
N_HEADS = 16
CHUNK = 128
A_GROUPS = 8
CONV_WIDTH = 31
CONV_HALO = 16
CONV_CHUNK = 64
BAND = 64
PATTERN_DILATIONS = (1, 4, 16)
ROT_DIM = 16
ROPE_THETA = 500000.0
N_SHARDS = 4

ADAM_LR, ADAM_B1, ADAM_B2, ADAM_EPS, ADAM_WD, ADAM_STEP = 0.001, 0.9, 0.999, 1e-08, 0.01, 10


def _cp(*sem):
    return pltpu.CompilerParams(dimension_semantics=sem, vmem_limit_bytes=VMEM_LIMIT_BYTES)


def _tile(n, pref):
    t = min(n, pref)
    assert n % t == 0, (n, pref)
    return t


def _dot(a, b, ca, cb):
    return lax.dot_general(a, b, (((ca,), (cb,)), ((), ())), preferred_element_type=F32)


def _mm_ngroup(name, a, w, *, nt, tm, out_dtypes, extras=(), epilogue=None, anchor=None):
    M, K = a.shape
    G, R, C = w.shape
    nw = R if nt else C
    assert K == (C if nt else R)
    tm = _tile(M, tm)
    n_ex = len(extras)
    anchors = [] if anchor is None else [anchor]

    def body(a_ref, w_ref, *rest):
        rest = rest[len(anchors):]
        av = a_ref[...].astype(BF16)
        for g in range(G):
            cols = slice(g * nw, (g + 1) * nw)
            acc = _dot(av, w_ref[g], 1, 1 if nt else 0)
            res = epilogue(acc, *[e[:, cols] for e in rest[:n_ex]]) if epilogue else (acc,)
            for o_ref, r in zip(rest[n_ex:], res):
                o_ref[:, cols] = r.astype(o_ref.dtype)

    blk = pl.BlockSpec((tm, G * nw), lambda m: (m, 0))
    return pl.pallas_call(
        body, name=name, grid=(M // tm,),
        in_specs=[pl.BlockSpec((tm, K), lambda m: (m, 0)), pl.BlockSpec((G, R, C), lambda m: (0, 0, 0))]
        + [pl.BlockSpec((8, LANES), lambda m: (0, 0))] * len(anchors) + [blk] * n_ex,
        out_specs=[blk] * len(out_dtypes),
        out_shape=[jax.ShapeDtypeStruct((M, G * nw), dt) for dt in out_dtypes],
        compiler_params=_cp("parallel"),
    )(a, w, *anchors, *extras)


def _mm_kgroup(name, a, w, *, nt, tm, out_dtypes, extras=(), vecs=(), n_sums=0, epilogue=None, anchor=None):
    G, R, C = w.shape
    kw, N = (C, R) if nt else (R, C)
    if a.ndim == 3:
        M = a.shape[1]
        assert a.shape[0] == G and a.shape[2] == kw
    else:
        M = a.shape[0]
        assert a.shape[1] == G * kw
    tm = _tile(M, tm)
    n_ex = len(extras)
    a_spec = (pl.BlockSpec((G, tm, kw), lambda m: (0, m, 0)) if a.ndim == 3 else pl.BlockSpec((tm, G * kw), lambda m: (m, 0)))
    anchors = [] if anchor is None else [anchor]

    def body(a_ref, w_ref, *rest):
        rest = rest[len(anchors):]
        acc = None
        for g in range(G):
            a_g = a_ref[g] if a.ndim == 3 else a_ref[:, g * kw:(g + 1) * kw]
            part = _dot(a_g.astype(BF16), w_ref[g], 1, 1 if nt else 0)
            acc = part if acc is None else acc + part
        n_in = n_ex + len(vecs)
        res = epilogue(acc, *[e[...] for e in rest[:n_in]]) if epilogue else (acc,)
        outs = rest[n_in:]
        n_tiles = len(outs) - n_sums
        for o_ref, r in zip(outs[:n_tiles], res[:n_tiles]):
            o_ref[...] = r.astype(o_ref.dtype)
        if n_sums:
            @pl.when(pl.program_id(0) == 0)
            def _():
                for s_ref in outs[n_tiles:]:
                    s_ref[...] = jnp.zeros_like(s_ref)

            for s_ref, r in zip(outs[n_tiles:], res[n_tiles:]):
                s_ref[...] += r

    blk = pl.BlockSpec((tm, N), lambda m: (m, 0))
    row = pl.BlockSpec((1, N), lambda m: (0, 0))
    return pl.pallas_call(
        body, name=name, grid=(M // tm,),
        in_specs=[a_spec, pl.BlockSpec((G, R, C), lambda m: (0, 0, 0))]
        + [pl.BlockSpec((8, LANES), lambda m: (0, 0))] * len(anchors) + [blk] * n_ex
        + [pl.BlockSpec((None, 1, N), lambda m, i=i: (i, 0, 0)) for _, i in vecs],
        out_specs=[blk] * len(out_dtypes) + [row] * n_sums,
        out_shape=[jax.ShapeDtypeStruct((M, N), dt) for dt in out_dtypes] + [jax.ShapeDtypeStruct((1, N), F32)] * n_sums,
        compiler_params=_cp("arbitrary" if n_sums else "parallel"),
    )(a, w, *anchors, *extras, *[v for v, _ in vecs])


def _wgrad(name, a, b, shape, *, a_group, tm, anchor=None):
    G, R, C = shape
    M = a.shape[0]
    tm = _tile(M, tm)
    n_m = M // tm
    anchors = [] if anchor is None else [anchor]

    def body(a_ref, b_ref, *rest):
        gb_ref, gf_ref = rest[len(anchors):]
        m = pl.program_id(1)
        part = _dot(a_ref[...].astype(BF16), b_ref[...].astype(BF16), 0, 0)

        @pl.when(m == 0)
        def _():
            gf_ref[...] = part

        @pl.when(m > 0)
        def _():
            gf_ref[...] += part

        @pl.when(m == n_m - 1)
        def _():
            gb_ref[...] = gf_ref[...].astype(BF16)

    a_spec = pl.BlockSpec((tm, R), (lambda g, m: (m, g)) if a_group else (lambda g, m: (m, 0)))
    if b.ndim == 3:
        assert not a_group
        b_spec = pl.BlockSpec((None, tm, C), lambda g, m: (g, m, 0))
    else:
        b_spec = pl.BlockSpec((tm, C), (lambda g, m: (m, 0)) if a_group else (lambda g, m: (m, g)))
    o_spec = pl.BlockSpec((None, R, C), lambda g, m: (g, 0, 0))
    return pl.pallas_call(
        body, name=name, grid=(G, n_m),
        in_specs=[a_spec, b_spec] + [pl.BlockSpec((8, LANES), lambda g, m: (0, 0))] * len(anchors), out_specs=o_spec,
        out_shape=jax.ShapeDtypeStruct(shape, BF16), scratch_shapes=[pltpu.VMEM((R, C), F32)],
        compiler_params=_cp("parallel", "arbitrary"),
    )(a, b, *anchors)


def _rms_fwd(name, x, g3, layer):
    S, D = x.shape
    tm = _tile(S, 512)

    def body(x_ref, g_ref, h_ref):
        xv = x_ref[...]
        r = lax.rsqrt(jnp.mean(xv * xv, axis=-1, keepdims=True) + EPS)
        h_ref[...] = (xv * r * g_ref[...]).astype(BF16)

    row = pl.BlockSpec((tm, D), lambda m: (m, 0))
    return pl.pallas_call(
        body, name=name, grid=(S // tm,),
        in_specs=[row, pl.BlockSpec((None, 1, D), lambda m: (layer, 0, 0))], out_specs=row,
        out_shape=jax.ShapeDtypeStruct((S, D), BF16), compiler_params=_cp("parallel"),
    )(x, g3)


def _adamw_math(w, m, v, g):
    m2 = ADAM_B1 * m + (1.0 - ADAM_B1) * g
    v2 = ADAM_B2 * v + (1.0 - ADAM_B2) * jnp.square(g)
    m_hat = m2 / (1.0 - ADAM_B1 ** ADAM_STEP)
    v_hat = v2 / (1.0 - ADAM_B2 ** ADAM_STEP)
    return g, -ADAM_LR * (m_hat / (jnp.sqrt(v_hat) + ADAM_EPS) + ADAM_WD * w), m2, v2


def _row_tile(rows):
    return _tile(rows, ELEMENTWISE_ROWS) if rows % ELEMENTWISE_ROWS == 0 else rows


def _adamw(name, w, m, v, g):
    rows, C = w.shape
    tr = _row_tile(rows)

    def body(w_ref, m_ref, v_ref, g_in, g_ref, d_ref, nm_ref, nv_ref):
        for o_ref, val in zip((g_ref, d_ref, nm_ref, nv_ref), _adamw_math(w_ref[...], m_ref[...], v_ref[...], g_in[...])):
            o_ref[...] = val

    blk = pl.BlockSpec((tr, C), lambda i: (i, 0))
    return pl.pallas_call(
        body, name=name, grid=(rows // tr,), in_specs=[blk] * 4, out_specs=[blk] * 4,
        out_shape=[jax.ShapeDtypeStruct((rows, C), F32)] * 4, compiler_params=_cp("parallel"),
    )(w, m, v, g)


def _sum_order():
    x, y, c = lax.axis_index("x"), lax.axis_index("y"), lax.axis_index("c")
    differs = lambda bit, coord: bit + coord - 2 * bit * coord
    slots = [4 * differs(p >> 2 & 1, x) + 2 * differs(p >> 1 & 1, y) + differs(p & 1, c) - 1 for p in range(8)]
    return [jnp.asarray(s, jnp.int32).reshape(1) for s in [2 * x + y, 4 * x + 2 * y + c] + slots]


def _adamw_layer(name, w, m, v, layer, gb, recv, order, outs):
    _, R, C = w.shape
    tr = _row_tile(R)
    n_s = len(order)

    def body(*refs):
        me = refs[1][0]
        w_ref, m_ref, v_ref, own_ref = refs[n_s:n_s + 4]
        theirs, outs_ = refs[n_s + 4:n_s + 12], refs[n_s + 16:]
        g = None
        for p in range(8):
            term = jnp.where(me == p, own_ref[...], theirs[p][...]).astype(F32)
            g = term if g is None else g + term
        for o_ref, val in zip(outs_, _adamw_math(w_ref[...], m_ref[...], v_ref[...], g)):
            o_ref[...] = val

    st = pl.BlockSpec((None, tr, C), lambda i, *s: (layer, i, 0))
    slot = lambda p: pl.BlockSpec((None, tr, C), lambda i, *s: (jnp.maximum(s[2 + p][0], 0), i, 0))
    return pl.pallas_call(
        body, name=name,
        grid_spec=pltpu.PrefetchScalarGridSpec(
            num_scalar_prefetch=n_s, grid=(R // tr,),
            in_specs=[st] * 3 + [pl.BlockSpec((None, tr, C), lambda i, *s: (s[0][0], i, 0))] + [slot(p) for p in range(8)]
            + [ANY] * 4,
            out_specs=[st] * 4),
        out_shape=[jax.ShapeDtypeStruct(w.shape, F32)] * 4, input_output_aliases={n_s + 12 + j: j for j in range(4)},
        compiler_params=_cp("parallel"),
    )(*order, w, m, v, gb, *([recv] * 8), *outs)


def _gelu(x):
    return x * (0.5 * (1.0 + jnp.tanh(0.7978845608028654 * (x + 0.044715 * (x * x * x)))))


def _layernorm(t, g, b):
    mu = jnp.mean(t, axis=-1, keepdims=True)
    var = jnp.mean(jnp.square(t - mu), axis=-1, keepdims=True)
    return (t - mu) * lax.rsqrt(var + EPS) * g + b


def _silu(x):
    return x * jax.nn.sigmoid(x)


def _a_value(zv, g, b):
    return _layernorm(_gelu(zv), g, b)


def _b_tail(gc, g, b):
    return _silu(_layernorm(gc, g, b))


def _first_head(shape):
    return lax.broadcasted_iota(jnp.int32, shape, len(shape) - 1) < HEAD_DIM


def _spatial_mix(spw_ref, vb, tm):
    first = _first_head((CHUNK, LANES))
    rows = []
    for n in range(tm // CHUNK):
        blocks = []
        for j in range(A_GROUPS // 2):
            vblk = vb[n * CHUNK:(n + 1) * CHUNK, j * LANES:(j + 1) * LANES]
            r0 = _dot(spw_ref[2 * j], vblk, 1, 0)
            r1 = _dot(spw_ref[2 * j + 1], vblk, 1, 0)
            blocks.append(jnp.where(first, r0, r1))
        rows.append(jnp.concatenate(blocks, axis=1))
    return jnp.concatenate(rows, axis=0) if len(rows) > 1 else rows[0]


def _ab_tail_out_proj(name, z, gconv, spw, bias_full, vn_g, vn_b, cn_g, cn_b, layer, w, x, g3, g_layer):
    S = z.shape[0]
    AW = 512
    tm = _tile(S, 256)

    def body(zu_ref, zv_ref, gc_ref, spw_ref, bias_ref, vg_ref, vb_ref, cg_ref, cb_ref, w_ref, x_ref, g_ref,
             xo_ref, h_ref, cat_ref):
        u = _gelu(zu_ref[...])
        v = _a_value(zv_ref[...], vg_ref[...], vb_ref[...])
        sv = _spatial_mix(spw_ref, v.astype(BF16), tm) + jnp.tile(bias_ref[...], (tm // CHUNK, 1))
        cat = jnp.concatenate([(u * sv).astype(BF16), _b_tail(gc_ref[...], cg_ref[...], cb_ref[...]).astype(BF16)], axis=1)
        cat_ref[...] = cat
        y, h = _add_norm_epilogue(_dot(cat, w_ref[0], 1, 0), x_ref[...], g_ref[...])
        xo_ref[...] = y
        h_ref[...] = h.astype(BF16)

    vec = pl.BlockSpec((None, 1, AW), lambda m: (layer, 0, 0))
    row = pl.BlockSpec((tm, 2 * AW), lambda m: (m, 0))
    return pl.pallas_call(
        body, name=name, grid=(S // tm,),
        in_specs=[pl.BlockSpec((tm, AW), lambda m: (m, 0)), pl.BlockSpec((tm, AW), lambda m: (m, 1)),
                  pl.BlockSpec((tm, AW), lambda m: (m, 0)),
                  pl.BlockSpec((None, A_GROUPS, CHUNK, CHUNK), lambda m: (layer, 0, 0, 0)),
                  pl.BlockSpec((None, CHUNK, AW), lambda m: (layer, 0, 0)), vec, vec, vec, vec,
                  pl.BlockSpec(w.shape, lambda m: (0, 0, 0)), row, pl.BlockSpec((None, 1, 2 * AW), lambda m: (g_layer, 0, 0))],
        out_specs=[row] * 3,
        out_shape=[jax.ShapeDtypeStruct((S, 2 * AW), dt) for dt in (F32, BF16, BF16)], compiler_params=_cp("parallel"),
    )(z, z, gconv, spw, bias_full, vn_g, vn_b, cn_g, cn_b, w, x, g3)


def _ab_tail_bwd(name, z, gconv, dcat, spw, spw_t, bias_full, vn_g, vn_b, cn_g, cn_b, layer):
    S = z.shape[0]
    AW = 512
    tm = _tile(S, 256)
    n_chunks = tm // CHUNK

    def body(zu_ref, zv_ref, gc_ref, dcat_ref, spw_ref, spwt_ref, bias_ref, vg_ref, vb_ref, cg_ref, cb_ref,
             dz_ref, dgc_ref, dspw_ref, dbias_ref, dvg_ref, dvb_ref, dcg_ref, dcb_ref):
        @pl.when(pl.program_id(0) == 0)
        def _():
            for r in (dspw_ref, dbias_ref, dvg_ref, dvb_ref, dcg_ref, dcb_ref):
                r[...] = jnp.zeros_like(r)

        dya = dcat_ref[:, :AW]
        dyb = dcat_ref[:, AW:]
        u, u_vjp = jax.vjp(_gelu, zu_ref[...])
        v, v_vjp = jax.vjp(_a_value, zv_ref[...], vg_ref[...], vb_ref[...])
        vb16 = v.astype(BF16)
        sv = _spatial_mix(spw_ref, vb16, tm) + jnp.tile(bias_ref[...], (n_chunks, 1))
        (dzu,) = u_vjp(dya * sv)
        dsv = dya * u
        dsv16 = dsv.astype(BF16)
        dv = _spatial_mix(spwt_ref, dsv16, tm)
        dzv, dvg, dvb = v_vjp(dv)
        dz_ref[0] = dzu
        dz_ref[1] = dzv
        dvg_ref[...] += dvg
        dvb_ref[...] += dvb

        first = _first_head((CHUNK, LANES))
        zero = jnp.zeros((), BF16)
        dbias = jnp.zeros((CHUNK, AW), F32)
        for n in range(n_chunks):
            rows = slice(n * CHUNK, (n + 1) * CHUNK)
            dbias = dbias + dsv[rows]
            for j in range(A_GROUPS // 2):
                cols = slice(j * LANES, (j + 1) * LANES)
                dblk, vblk = dsv16[rows, cols], vb16[rows, cols]
                dspw_ref[2 * j] += _dot(jnp.where(first, dblk, zero), vblk, 1, 1)
                dspw_ref[2 * j + 1] += _dot(jnp.where(first, zero, dblk), vblk, 1, 1)
        dbias_ref[...] += dbias

        _, t_vjp = jax.vjp(_b_tail, gc_ref[...], cg_ref[...], cb_ref[...])
        dgc, dcg, dcb = t_vjp(dyb)
        dgc_ref[...] = dgc
        dcg_ref[...] += dcg
        dcb_ref[...] += dcb

    vec = pl.BlockSpec((None, 1, AW), lambda m: (layer, 0, 0))
    spw_spec = pl.BlockSpec((None, A_GROUPS, CHUNK, CHUNK), lambda m: (layer, 0, 0, 0))
    ovec = pl.BlockSpec((1, AW), lambda m: (0, 0))
    return pl.pallas_call(
        body, name=name, grid=(S // tm,),
        in_specs=[pl.BlockSpec((tm, AW), lambda m: (m, 0)), pl.BlockSpec((tm, AW), lambda m: (m, 1)),
                  pl.BlockSpec((tm, AW), lambda m: (m, 0)), pl.BlockSpec((tm, 2 * AW), lambda m: (m, 0)),
                  spw_spec, spw_spec, pl.BlockSpec((None, CHUNK, AW), lambda m: (layer, 0, 0)), vec, vec, vec, vec],
        out_specs=[pl.BlockSpec((2, tm, AW), lambda m: (0, m, 0)), pl.BlockSpec((tm, AW), lambda m: (m, 0)),
                   pl.BlockSpec((A_GROUPS, CHUNK, CHUNK), lambda m: (0, 0, 0)),
                   pl.BlockSpec((CHUNK, AW), lambda m: (0, 0)), ovec, ovec, ovec, ovec],
        out_shape=[jax.ShapeDtypeStruct((4, S, AW), F32), jax.ShapeDtypeStruct((S, AW), F32),
                   jax.ShapeDtypeStruct((A_GROUPS, CHUNK, CHUNK), F32), jax.ShapeDtypeStruct((CHUNK, AW), F32)]
                  + [jax.ShapeDtypeStruct((1, AW), F32)] * 4,
        compiler_params=_cp("arbitrary"),
    )(z, z, gconv, dcat, spw, spw_t, bias_full, vn_g, vn_b, cn_g, cn_b)


def _fold_bias(dbias_full):
    def body(d_ref, o_ref):
        d = d_ref[...]
        hi = d.astype(BF16)
        lo = (d - hi.astype(F32)).astype(BF16)
        r = lax.broadcasted_iota(jnp.int32, (512, LANES), 0)
        c = lax.broadcasted_iota(jnp.int32, (512, LANES), 1)
        fold = jnp.where(lax.shift_right_logical(r, 6) == c, 1.0, 0.0).astype(BF16)
        o_ref[...] = _dot(hi, fold, 1, 0) + _dot(lo, fold, 1, 0)

    return pl.pallas_call(body, name="fold_spatial_bias", out_shape=jax.ShapeDtypeStruct((CHUNK, LANES), F32))(dbias_full)


def _halo_specs(tm, n_halo_blocks, col):
    r = tm // CONV_HALO
    prev = pl.BlockSpec((CONV_HALO, LANES), lambda j, i: (jnp.maximum(i * r - 1, 0), col + j))
    cur = pl.BlockSpec((tm, LANES), lambda j, i: (i, col + j))
    nxt = pl.BlockSpec((CONV_HALO, LANES), lambda j, i: (jnp.minimum((i + 1) * r, n_halo_blocks - 1), col + j))
    return [prev, cur, nxt]


def _fill_halo(scr, prev, cur, nxt, tm, i, n_i):
    scr[0:CONV_HALO, :] = jnp.where(i > 0, prev, 0.0)
    scr[CONV_HALO:CONV_HALO + tm, :] = cur
    scr[CONV_HALO + tm:2 * CONV_HALO + tm, :] = jnp.where(i < n_i - 1, nxt, 0.0)


def _glu_conv_fwd(name, z, cw, cb3, layer):
    S = z.shape[0]
    tm = _tile(S, 512)
    n_i = S // tm
    pad = CONV_WIDTH // 2

    def body(vp, vc, vn, gp, gc, gn, w_ref, b_ref, out_ref, scr):
        i = pl.program_id(1)
        glu = lambda a, b: a[...] * jax.nn.sigmoid(b[...])
        _fill_halo(scr, glu(vp, gp), glu(vc, gc), glu(vn, gn), tm, i, n_i)
        taps = [w_ref[j:j + 1, :] for j in range(CONV_WIDTH)]
        for c0 in range(0, tm, CONV_CHUNK):
            acc = jnp.zeros((CONV_CHUNK, LANES), F32) + b_ref[...]
            for j in range(CONV_WIDTH):
                acc = acc + taps[j] * scr[pl.ds(c0 + CONV_HALO - pad + j, CONV_CHUNK), :]
            out_ref[pl.ds(c0, CONV_CHUNK), :] = acc

    return pl.pallas_call(
        body, name=name, grid=(4, n_i),
        in_specs=_halo_specs(tm, S // CONV_HALO, 8) + _halo_specs(tm, S // CONV_HALO, 12)
        + [pl.BlockSpec((None, CONV_WIDTH, LANES), lambda j, i: (j, 0, 0)),
           pl.BlockSpec((None, 1, LANES), lambda j, i: (layer, 0, j))],
        out_specs=pl.BlockSpec((tm, LANES), lambda j, i: (i, j)),
        out_shape=jax.ShapeDtypeStruct((S, 4 * LANES), F32),
        scratch_shapes=[pltpu.VMEM((tm + 2 * CONV_HALO, LANES), F32)],
        compiler_params=_cp("parallel", "parallel"),
    )(z, z, z, z, z, z, cw, cb3)


def _glu_conv_bwd(name, z, dgconv, dz, cw):
    S = z.shape[0]
    tm = _tile(S, 512)
    n_i = S // tm
    pad = CONV_WIDTH // 2

    def body(vp, vc, vn, gp, gc, gn, dp, dc, dn, w_ref, dz_in, dz_ref, gb_ref, db_ref, g_scr, d_scr, gf_ref):
        i = pl.program_id(1)
        sig = jax.nn.sigmoid(gc[...])
        _fill_halo(g_scr, vp[...] * jax.nn.sigmoid(gp[...]), vc[...] * sig, vn[...] * jax.nn.sigmoid(gn[...]), tm, i, n_i)
        _fill_halo(d_scr, dp[...], dc[...], dn[...], tm, i, n_i)

        @pl.when(i == 0)
        def _():
            gf_ref[...] = jnp.zeros_like(gf_ref)
            db_ref[...] = jnp.zeros_like(db_ref)

        taps = [w_ref[j:j + 1, :] for j in range(CONV_WIDTH)]
        dw = [jnp.zeros((8, LANES), F32) for _ in range(CONV_WIDTH)]
        db = jnp.zeros((8, LANES), F32)
        fold8 = lambda t: jnp.sum(t.reshape(CONV_CHUNK // 8, 8, LANES), axis=0)
        for c0 in range(0, tm, CONV_CHUNK):
            rows = pl.ds(c0, CONV_CHUNK)
            d_cur = dc[rows, :]
            dglu = jnp.zeros((CONV_CHUNK, LANES), F32)
            for j in range(CONV_WIDTH):
                dglu = dglu + taps[j] * d_scr[pl.ds(c0 + CONV_HALO + pad - j, CONV_CHUNK), :]
                dw[j] = dw[j] + fold8(d_cur * g_scr[pl.ds(c0 + CONV_HALO - pad + j, CONV_CHUNK), :])
            db = db + fold8(d_cur)
            sig_c = jax.nn.sigmoid(gc[rows, :])
            dz_ref[0, rows, :] = dglu * sig_c
            dz_ref[1, rows, :] = dglu * vc[rows, :] * sig_c * (1.0 - sig_c)
        for j in range(CONV_WIDTH):
            gf_ref[j:j + 1, :] += jnp.sum(dw[j], axis=0, keepdims=True)
        db_ref[...] += jnp.sum(db, axis=0, keepdims=True)

        @pl.when(i == n_i - 1)
        def _():
            gb_ref[...] = gf_ref[...].astype(BF16)

    w_spec = pl.BlockSpec((None, CONV_WIDTH, LANES), lambda j, i: (j, 0, 0))
    return pl.pallas_call(
        body, name=name, grid=(4, n_i),
        in_specs=_halo_specs(tm, S // CONV_HALO, 8) + _halo_specs(tm, S // CONV_HALO, 12)
        + _halo_specs(tm, S // CONV_HALO, 0) + [w_spec, ANY],
        out_specs=[pl.BlockSpec((2, tm, LANES), lambda j, i: (1, i, j)),
                   w_spec, pl.BlockSpec((1, LANES), lambda j, i: (0, j))],
        out_shape=[jax.ShapeDtypeStruct(dz.shape, F32), jax.ShapeDtypeStruct(cw.shape, BF16),
                   jax.ShapeDtypeStruct((1, 4 * LANES), F32)],
        input_output_aliases={10: 0},
        scratch_shapes=[pltpu.VMEM((tm + 2 * CONV_HALO, LANES), F32)] * 2 + [pltpu.VMEM((CONV_WIDTH, LANES), F32)],
        compiler_params=_cp("parallel", "arbitrary"),
    )(z, z, z, z, z, z, dgconv, dgconv, dgconv, cw, dz)


def _seg_matrix(scale):
    r = lax.broadcasted_iota(jnp.int32, (LANES, LANES), 0)
    c = lax.broadcasted_iota(jnp.int32, (LANES, LANES), 1)
    return jnp.where(lax.shift_right_logical(r, 6) == lax.shift_right_logical(c, 6), scale, 0.0).astype(BF16)


def _seg_sum(x, seg):
    hi = x.astype(BF16)
    lo = (x - hi.astype(F32)).astype(BF16)
    return _dot(hi, seg, 1, 0) + _dot(lo, seg, 1, 0)


def _rope_tables(S):
    pos = jnp.arange(S, dtype=F32)
    inv_freq = ROPE_THETA ** (-jnp.arange(0, ROT_DIM, 2, dtype=F32) / ROT_DIM)
    ang = pos[:, None] * inv_freq[None, :]
    cos, sin = jnp.cos(ang), jnp.sin(ang)
    half = ROT_DIM // 2
    rest = HEAD_DIM - ROT_DIM
    one, zero = jnp.ones((S, rest), F32), jnp.zeros((S, rest), F32)
    zh = jnp.zeros((S, half), F32)
    c = jnp.concatenate([cos, cos, one], axis=1)
    sa = jnp.concatenate([-sin, zh, zero], axis=1)
    sb = jnp.concatenate([zh, sin, zero], axis=1)
    return [jnp.tile(t, (1, 2)) for t in (c, sa, sb)]


QK_CHUNK = 64


def _qk_fwd(name, qkv, gq, gk, tables):
    S = qkv.shape[0]
    W = N_HEADS * HEAD_DIM
    tm = _tile(S, 256)
    half = ROT_DIM // 2

    def body(q_ref, k_ref, gq_ref, gk_ref, c_ref, sa_ref, sb_ref, qn_ref, kn_ref):
        seg = _seg_matrix(1.0 / HEAD_DIM)
        for r0 in range(0, tm, QK_CHUNK):
            rows = pl.ds(r0, QK_CHUNK)
            c, sa, sb = c_ref[rows, :], sa_ref[rows, :], sb_ref[rows, :]
            for t_ref, g_ref, o_ref in ((q_ref, gq_ref, qn_ref), (k_ref, gk_ref, kn_ref)):
                for blk in range(W // LANES):
                    cols = slice(blk * LANES, (blk + 1) * LANES)
                    t = t_ref[rows, cols]
                    y = t * lax.rsqrt(_seg_sum(t * t, seg) + EPS) * g_ref[...]
                    o_ref[rows, cols] = y * c + pltpu.roll(y, LANES - half, 1) * sa + pltpu.roll(y, half, 1) * sb

    row = lambda k: pl.BlockSpec((tm, W), lambda m: (m, k))
    gain = pl.BlockSpec((1, LANES), lambda m: (0, 0))
    tab = pl.BlockSpec((tm, LANES), lambda m: (m, 0))
    return pl.pallas_call(
        body, name=name, grid=(S // tm,),
        in_specs=[row(0), row(1), gain, gain, tab, tab, tab], out_specs=[row(0)] * 2,
        out_shape=[jax.ShapeDtypeStruct((S, W), F32)] * 2, compiler_params=_cp("parallel"),
    )(qkv, qkv, gq, gk, *tables)


def _qk_bwd(name, qkv, gq, gk, tables, dqs, dks, dvs):
    S = qkv.shape[0]
    W = N_HEADS * HEAD_DIM
    tm = _tile(S, 256)
    half = ROT_DIM // 2
    n_p = len(dqs)

    def body(q_ref, k_ref, gq_ref, gk_ref, c_ref, sa_ref, sb_ref, *rest):
        dq_refs, dk_refs, dv_refs = rest[:n_p], rest[n_p:2 * n_p], rest[2 * n_p:3 * n_p]
        dqkv_ref, dgq_ref, dgk_ref = rest[3 * n_p:]

        @pl.when(pl.program_id(0) == 0)
        def _():
            dgq_ref[...] = jnp.zeros_like(dgq_ref)
            dgk_ref[...] = jnp.zeros_like(dgk_ref)

        seg = _seg_matrix(1.0 / HEAD_DIM)
        r_i = lax.broadcasted_iota(jnp.int32, (LANES, LANES), 0)
        c_i = lax.broadcasted_iota(jnp.int32, (LANES, LANES), 1)
        same_dim = jnp.where((r_i & (HEAD_DIM - 1)) == (c_i & (HEAD_DIM - 1)), 1.0, 0.0).astype(BF16)
        dgs = [jnp.zeros((8, LANES), F32), jnp.zeros((8, LANES), F32)]
        fold8 = lambda t: jnp.sum(t.reshape(QK_CHUNK // 8, 8, LANES), axis=0)
        for r0 in range(0, tm, QK_CHUNK):
            rows = pl.ds(r0, QK_CHUNK)
            c, sa, sb = c_ref[rows, :], sa_ref[rows, :], sb_ref[rows, :]
            for idx, (t_ref, g_ref, d_refs) in enumerate(((q_ref, gq_ref, dq_refs), (k_ref, gk_ref, dk_refs))):
                for blk in range(W // LANES):
                    cols = slice(blk * LANES, (blk + 1) * LANES)
                    dout = d_refs[0][rows, cols]
                    for r in d_refs[1:]:
                        dout = dout + r[rows, cols]
                    dy = dout * c + pltpu.roll(dout * sa, half, 1) + pltpu.roll(dout * sb, LANES - half, 1)
                    t = t_ref[rows, cols]
                    r_ = lax.rsqrt(_seg_sum(t * t, seg) + EPS)
                    xhat = t * r_
                    dgs[idx] = dgs[idx] + fold8(dy * xhat)
                    dxhat = dy * g_ref[...]
                    dt = r_ * (dxhat - xhat * _seg_sum(dxhat * xhat, seg))
                    dqkv_ref[rows, idx * W + blk * LANES: idx * W + (blk + 1) * LANES] = dt.astype(BF16)
            dv = dv_refs[0][rows, :]
            for r in dv_refs[1:]:
                dv = dv + r[rows, :]
            dqkv_ref[rows, 2 * W:] = dv.astype(BF16)
        for dg, dg_ref in zip(dgs, (dgq_ref, dgk_ref)):
            dg_ref[...] += jnp.sum(_seg_sum(dg, same_dim), axis=0, keepdims=True)

    row = lambda k: pl.BlockSpec((tm, W), lambda m: (m, k))
    gain = pl.BlockSpec((1, LANES), lambda m: (0, 0))
    tab = pl.BlockSpec((tm, LANES), lambda m: (m, 0))
    return pl.pallas_call(
        body, name=name, grid=(S // tm,),
        in_specs=[row(0), row(1), gain, gain, tab, tab, tab] + [row(0)] * (3 * n_p),
        out_specs=[pl.BlockSpec((tm, 3 * W), lambda m: (m, 0)), gain, gain],
        out_shape=[jax.ShapeDtypeStruct((S, 3 * W), BF16), jax.ShapeDtypeStruct((1, LANES), F32),
                   jax.ShapeDtypeStruct((1, LANES), F32)],
        compiler_params=_cp("arbitrary"),
    )(qkv, qkv, gq, gk, *tables, *dqs, *dks, *dvs)


ATTN_BQ = 2 * BAND
ATTN_ROWS = 16 * ATTN_BQ
V_COL = 2 * N_HEADS * HEAD_DIM // LANES


def _attn_geometry(S, d):
    rows = min(ATTN_ROWS, S)
    halo = BAND * d
    assert rows % (ATTN_BQ * d) == 0 and S % rows == 0, (S, d)
    return rows, halo, rows // (ATTN_BQ * d)


def _attn_specs(S, d, col):
    rows, halo, _ = _attn_geometry(S, d)
    r = rows // halo
    n_h = S // halo
    prev = pl.BlockSpec((halo, LANES), lambda j, i: (jnp.maximum(i * r - 1, 0), col + j))
    cur = pl.BlockSpec((rows, LANES), lambda j, i: (i, col + j))
    nxt = pl.BlockSpec((halo, LANES), lambda j, i: (jnp.minimum((i + 1) * r, n_h - 1), col + j))
    return [prev, cur, nxt]


def _fill_window(scr, prev, cur, nxt, rows, halo):
    scr[0:halo, :] = prev[...]
    scr[halo:halo + rows, :] = cur[...]
    scr[halo + rows:2 * halo + rows, :] = nxt[...]


def _chain_groups(n_sb, d, size):
    chains = [(sb, r) for sb in range(n_sb) for r in range(d)]
    return [chains[j:j + size] for j in range(0, len(chains), size)]


def _strided(ref, start, size, d):
    return ref[pl.ds(start, size, stride=d) if d > 1 else pl.ds(start, size), :]


def _band_mask(i, S, d, sb):
    rows, _, _ = _attn_geometry(S, d)
    L = S // d
    base = i * (rows // d) + sb * ATTN_BQ
    wk = ATTN_BQ + 2 * BAND
    row = lax.broadcasted_iota(jnp.int32, (ATTN_BQ, wk), 0)
    col = lax.broadcasted_iota(jnp.int32, (ATTN_BQ, wk), 1)
    lj = base - BAND + col
    return (jnp.abs(col - BAND - row) <= BAND) & (lj >= 0) & (lj < L)


def _attn_fwd(name, q, k, v, v_col, d):
    S, W = q.shape
    rows, halo, n_sb = _attn_geometry(S, d)
    wk = ATTN_BQ + 2 * BAND
    scale = HEAD_DIM ** -0.5

    def body(q_ref, kp, kc, kn, vp, vc, vn, o_ref, lse_ref, kw, vw):
        i = pl.program_id(1)
        _fill_window(kw, kp, kc, kn, rows, halo)
        _fill_window(vw, vp, vc, vn, rows, halo)
        first = _first_head((ATTN_BQ, LANES))
        heads = (first, jnp.logical_not(first))
        zero = jnp.zeros((), BF16)
        for group in _chain_groups(n_sb, d, 4):
            masks = {sb: _band_mask(i, S, d, sb) for sb in sorted({sb for sb, _ in group})}
            starts = [r + d * sb * ATTN_BQ for sb, r in group]
            qs = [_strided(q_ref, st, ATTN_BQ, d).astype(BF16) for st in starts]
            ks = [_strided(kw, st, wk, d).astype(BF16) for st in starts]
            vs = [_strided(vw, st, wk, d).astype(BF16) for st in starts]
            s_all = [[_dot(jnp.where(hm, qv, zero), kv, 1, 1) for hm in heads] for qv, kv in zip(qs, ks)]
            p_all, den_all, lse_all = [], [], []
            for (sb, _), s_h in zip(group, s_all):
                s_h = [jnp.where(masks[sb], s * scale, NEG) for s in s_h]
                mx_h = [jnp.max(s, axis=-1, keepdims=True) for s in s_h]
                p_h = [jnp.exp(s - mx) for s, mx in zip(s_h, mx_h)]
                den_h = [jnp.sum(p, axis=-1, keepdims=True) for p in p_h]
                p_all.append([p.astype(BF16) for p in p_h])
                den_all.append(den_h)
                lse_all.append([mx + jnp.log(den) for mx, den in zip(mx_h, den_h)])
            o_all = [[_dot(p, vv, 1, 0) for p in p_h] for p_h, vv in zip(p_all, vs)]
            for st, o_h, den_h, lse_h in zip(starts, o_all, den_all, lse_all):
                dst = pl.ds(st, ATTN_BQ, stride=d) if d > 1 else pl.ds(st, ATTN_BQ)
                o_ref[dst, :] = jnp.where(first, o_h[0] / den_h[0], o_h[1] / den_h[1])
                lse_ref[dst, :] = jnp.where(first, lse_h[0], lse_h[1])

    cur = _attn_specs(S, d, 0)[1]
    return pl.pallas_call(
        body, name=name, grid=(W // LANES, S // rows),
        in_specs=[cur] + _attn_specs(S, d, 0) + _attn_specs(S, d, v_col), out_specs=[cur, cur],
        out_shape=[jax.ShapeDtypeStruct((S, W), F32)] * 2,
        scratch_shapes=[pltpu.VMEM((rows + 2 * halo, LANES), F32)] * 2,
        compiler_params=_cp("parallel", "parallel"),
    )(q, k, k, k, v, v, v)


def _merge_out_proj(name, os, lses, w, x, g3, layer):
    S, W = os[0].shape
    tm = _tile(S, 256)
    n_p = len(os)

    def body(*refs):
        o_refs, l_refs = refs[:n_p], refs[n_p:2 * n_p]
        w_ref, x_ref, g_ref, xo_ref, h_ref, o_ref, lt_ref = refs[2 * n_p:]
        ls = [r[...] for r in l_refs]
        mx = functools.reduce(jnp.maximum, ls)
        es = [jnp.exp(l - mx) for l in ls]
        den = functools.reduce(lambda a, b: a + b, es)
        acc = es[0] * o_refs[0][...]
        for e, r in zip(es[1:], o_refs[1:]):
            acc = acc + e * r[...]
        o = (acc / den).astype(BF16)
        o_ref[...] = o
        lt_ref[...] = mx + jnp.log(den)
        y, h = _add_norm_epilogue(_dot(o, w_ref[0], 1, 0), x_ref[...], g_ref[...])
        xo_ref[...] = y
        h_ref[...] = h.astype(BF16)

    row = pl.BlockSpec((tm, W), lambda m: (m, 0))
    return pl.pallas_call(
        body, name=name, grid=(S // tm,),
        in_specs=[row] * (2 * n_p) + [pl.BlockSpec(w.shape, lambda m: (0, 0, 0)), row,
                                      pl.BlockSpec((None, 1, W), lambda m: (layer, 0, 0))],
        out_specs=[row] * 4,
        out_shape=[jax.ShapeDtypeStruct((S, W), dt) for dt in (F32, BF16, BF16, F32)],
        compiler_params=_cp("parallel"),
    )(*os, *lses, w, x, g3)


def _delta_epilogue(do, o):
    seg = _seg_matrix(1.0)
    prod = do * o.astype(F32)
    delta = [_seg_sum(prod[:, blk * LANES:(blk + 1) * LANES], seg) for blk in range(do.shape[1] // LANES)]
    return do, jnp.concatenate(delta, axis=1)


def _attn_bwd_q(name, q, k, v, v_col, do, lse, delta, d):
    S, W = q.shape
    rows, halo, n_sb = _attn_geometry(S, d)
    wk = ATTN_BQ + 2 * BAND
    scale = HEAD_DIM ** -0.5

    def body(q_ref, do_ref, l_ref, dl_ref, kp, kc, kn, vp, vc, vn, dq_ref, kw, vw):
        i = pl.program_id(1)
        _fill_window(kw, kp, kc, kn, rows, halo)
        _fill_window(vw, vp, vc, vn, rows, halo)
        first = _first_head((ATTN_BQ, LANES))
        heads = (first, jnp.logical_not(first))
        zero = jnp.zeros((), BF16)
        wide = lambda t: jnp.concatenate([t] * (wk // LANES), axis=1)
        for group in _chain_groups(n_sb, d, 4):
            masks = {sb: _band_mask(i, S, d, sb) for sb in sorted({sb for sb, _ in group})}
            starts = [r + d * sb * ATTN_BQ for sb, r in group]
            qs = [_strided(q_ref, st, ATTN_BQ, d).astype(BF16) for st in starts]
            dos = [_strided(do_ref, st, ATTN_BQ, d).astype(BF16) for st in starts]
            ks = [_strided(kw, st, wk, d).astype(BF16) for st in starts]
            vs = [_strided(vw, st, wk, d).astype(BF16) for st in starts]
            s_all = [[_dot(jnp.where(hm, qv, zero), kv, 1, 1) for hm in heads] for qv, kv in zip(qs, ks)]
            dp_all = [[_dot(jnp.where(hm, dov, zero), vv, 1, 1) for hm in heads] for dov, vv in zip(dos, vs)]
            ds_all = []
            for (sb, _), st, s_h, dp_h in zip(group, starts, s_all, dp_all):
                lv, dlv = _strided(l_ref, st, ATTN_BQ, d), _strided(dl_ref, st, ATTN_BQ, d)
                l_sw, dl_sw = pltpu.roll(lv, HEAD_DIM, 1), pltpu.roll(dlv, HEAD_DIM, 1)
                ds_h = []
                for hm, s, dp in zip(heads, s_h, dp_h):
                    p = jnp.exp(jnp.where(masks[sb], s * scale, NEG) - wide(jnp.where(hm, lv, l_sw)))
                    ds_h.append((p * (dp - wide(jnp.where(hm, dlv, dl_sw))) * scale).astype(BF16))
                ds_all.append(ds_h)
            dq_all = [[_dot(ds, kv, 1, 0) for ds in ds_h] for ds_h, kv in zip(ds_all, ks)]
            for st, dq_h in zip(starts, dq_all):
                dst = pl.ds(st, ATTN_BQ, stride=d) if d > 1 else pl.ds(st, ATTN_BQ)
                dq_ref[dst, :] = jnp.where(first, dq_h[0], dq_h[1])

    cur = _attn_specs(S, d, 0)[1]
    return pl.pallas_call(
        body, name=name, grid=(W // LANES, S // rows),
        in_specs=[cur] * 4 + _attn_specs(S, d, 0) + _attn_specs(S, d, v_col), out_specs=cur,
        out_shape=jax.ShapeDtypeStruct((S, W), F32),
        scratch_shapes=[pltpu.VMEM((rows + 2 * halo, LANES), F32)] * 2,
        compiler_params=_cp("parallel", "parallel"),
    )(q, do, lse, delta, k, k, k, v, v, v)


def _attn_bwd_kv(name, q, k, v, v_col, do, lse, delta, d):
    S, W = q.shape
    rows, halo, n_sb = _attn_geometry(S, d)
    wk = ATTN_BQ + 2 * BAND
    scale = HEAD_DIM ** -0.5

    def body(k_ref, v_ref, qp, qc, qn, dop, doc, don, lp, lc, ln, dlp, dlc, dln, dk_ref, dv_ref, qw, dow, lw, dlw):
        i = pl.program_id(1)
        _fill_window(qw, qp, qc, qn, rows, halo)
        _fill_window(dow, dop, doc, don, rows, halo)
        _fill_window(lw, lp, lc, ln, rows, halo)
        _fill_window(dlw, dlp, dlc, dln, rows, halo)
        first = _first_head((ATTN_BQ, LANES))
        heads = (first, jnp.logical_not(first))
        zero = jnp.zeros((), BF16)
        for group in _chain_groups(n_sb, d, 2):
            masks = {sb: _band_mask(i, S, d, sb) for sb in sorted({sb for sb, _ in group})}
            starts = [r + d * sb * ATTN_BQ for sb, r in group]
            ks = [_strided(k_ref, st, ATTN_BQ, d).astype(BF16) for st in starts]
            vs = [_strided(v_ref, st, ATTN_BQ, d).astype(BF16) for st in starts]
            qs = [_strided(qw, st, wk, d).astype(BF16) for st in starts]
            dos = [_strided(dow, st, wk, d).astype(BF16) for st in starts]
            s_all = [[_dot(jnp.where(hm, kv, zero), qv, 1, 1) for hm in heads] for kv, qv in zip(ks, qs)]
            dp_all = [[_dot(jnp.where(hm, vv, zero), dov, 1, 1) for hm in heads] for vv, dov in zip(vs, dos)]
            p_all, ds_all = [], []
            for (sb, _), st, s_h, dp_h in zip(group, starts, s_all, dp_all):
                l_t, dl_t = _strided(lw, st, wk, d).T, _strided(dlw, st, wk, d).T
                p_h = [jnp.exp(jnp.where(masks[sb], s * scale, NEG) - l_t[hh * HEAD_DIM:hh * HEAD_DIM + 1, :])
                       for hh, s in enumerate(s_h)]
                ds_all.append([(p * (dp - dl_t[hh * HEAD_DIM:hh * HEAD_DIM + 1, :]) * scale).astype(BF16)
                               for hh, (p, dp) in enumerate(zip(p_h, dp_h))])
                p_all.append([p.astype(BF16) for p in p_h])
            dv_all = [[_dot(p, dov, 1, 0) for p in p_h] for p_h, dov in zip(p_all, dos)]
            dk_all = [[_dot(ds, qv, 1, 0) for ds in ds_h] for ds_h, qv in zip(ds_all, qs)]
            for st, dk_h, dv_h in zip(starts, dk_all, dv_all):
                dst = pl.ds(st, ATTN_BQ, stride=d) if d > 1 else pl.ds(st, ATTN_BQ)
                dk_ref[dst, :] = jnp.where(first, dk_h[0], dk_h[1])
                dv_ref[dst, :] = jnp.where(first, dv_h[0], dv_h[1])

    cur = _attn_specs(S, d, 0)[1]
    win = _attn_specs(S, d, 0)
    return pl.pallas_call(
        body, name=name, grid=(W // LANES, S // rows),
        in_specs=[cur, _attn_specs(S, d, v_col)[1]] + win * 4, out_specs=[cur, cur],
        out_shape=[jax.ShapeDtypeStruct((S, W), F32)] * 2,
        scratch_shapes=[pltpu.VMEM((rows + 2 * halo, LANES), F32)] * 4,
        compiler_params=_cp("parallel", "parallel"),
    )(k, v, q, q, q, do, do, do, lse, lse, lse, delta, delta, delta)


def _place():
    x, y, c = lax.axis_index("x"), lax.axis_index("y"), lax.axis_index("c")
    chips = [(1 - x, y), (x, 1 - y), (1 - x, 1 - y)]
    return x, y, c, chips


HBM = pl.BlockSpec(memory_space=pltpu.HBM)
SEM = pl.BlockSpec(memory_space=pltpu.SEMAPHORE)
DATAFLOW = pltpu.SideEffectType.DATAFLOW_SIDE_EFFECTING


N_PEERS = {"gather": 3, "scatter": 7}


def _exchange_copies(kind, srcs, dsts, send_sems, recv_sems):
    x, y, c, chips = _place()
    mine = 2 * x + y
    n_peers = N_PEERS[kind]
    cps = []
    for t in range(len(srcs)):
        for k in range(n_peers):
            if kind == "gather":
                (px, py), pc = chips[k], c
                src, dst = srcs[t], dsts[t].at[mine]
            else:
                bits = k + 1
                px, py, pc = (1 - x if bits & 4 else x), (1 - y if bits & 2 else y), (1 - c if bits & 1 else c)
                src, dst = srcs[t].at[2 * px + py], dsts[t].at[k]
            cps.append(pltpu.make_async_remote_copy(
                src_ref=src, dst_ref=dst, send_sem=send_sems.at[n_peers * t + k], recv_sem=recv_sems.at[n_peers * t + k],
                device_id=(px, py, pc), device_id_type=MESH))
    return cps


def _exchange_start(name, kind, groups):
    sizes = [len(g) for g in groups]
    n, n_g = sum(sizes), len(groups)

    def body(*refs):
        srcs, dsts = refs[:n], refs[n:2 * n]
        sems = refs[2 * n:2 * n + 2 * n_g]
        token = refs[4 * n + 2 * n_g]
        off = 0
        for gi, size in enumerate(sizes):
            for cp in _exchange_copies(kind, srcs[off:off + size], dsts[off:off + size], sems[2 * gi], sems[2 * gi + 1]):
                cp.start()
            off += size
        token[...] = jnp.zeros_like(token)

    arrays = [pltpu.with_memory_space_constraint(a, pltpu.HBM) for a in
              [s for g in groups for s, _ in g] + [d for g in groups for _, d in g]]
    sem_shapes = []
    for size in sizes:
        sem_shapes += [pltpu.SemaphoreType.DMA((N_PEERS[kind] * size,))] * 2
    outs = pl.pallas_call(
        body, name=name,
        in_specs=[HBM] * (2 * n),
        out_specs=[SEM] * (2 * n_g) + [HBM] * (2 * n) + [pl.BlockSpec(memory_space=pltpu.VMEM)],
        out_shape=sem_shapes + [pltpu.HBM(a.shape, a.dtype) for a in arrays] + [jax.ShapeDtypeStruct((8, LANES), F32)],
        input_output_aliases={t: 2 * n_g + t for t in range(2 * n)},
        compiler_params=pltpu.CompilerParams(has_side_effects=DATAFLOW),
    )(*arrays)
    sems, thru, token = outs[:2 * n_g], outs[2 * n_g:-1], outs[-1]
    handles, off = [], 0
    for gi, size in enumerate(sizes):
        handles.append((sems[2 * gi], sems[2 * gi + 1], thru[off:off + size], thru[n + off:n + off + size]))
        off += size
    return handles, token


def _exchange_wait(name, kind, handle, after):
    send_sems, recv_sems, srcs, dsts = handle
    n = len(srcs)

    def body(*refs):
        for cp in _exchange_copies(kind, refs[:n], refs[n:2 * n], refs[2 * n], refs[2 * n + 1]):
            cp.wait_send()
            cp.wait_recv()

    outs = pl.pallas_call(
        body, name=name,
        in_specs=[HBM] * (2 * n) + [SEM, SEM, ANY], out_specs=[HBM] * (2 * n),
        out_shape=[pltpu.HBM(a.shape, a.dtype) for a in (*srcs, *dsts)],
        input_output_aliases={t: t for t in range(2 * n)},
        compiler_params=pltpu.CompilerParams(has_side_effects=DATAFLOW),
    )(*srcs, *dsts, send_sems, recv_sems, after)
    return outs[:n], outs[n:]


def _prepare_shard(name, w, idx, dtype, mine, anchor=None):
    _, R, C = w.shape
    tr = _row_tile(R)
    anchors = [] if anchor is None else [anchor]

    def body(mine_ref, w_ref, *rest):
        src_ref, land_ref = rest[len(anchors):]
        val = w_ref[...].astype(dtype)
        src_ref[...] = val
        land_ref[...] = val

    return pl.pallas_call(
        body, name=name,
        grid_spec=pltpu.PrefetchScalarGridSpec(
            num_scalar_prefetch=1, grid=(R // tr,),
            in_specs=[pl.BlockSpec((None, tr, C), lambda i, s: (idx, i, 0))]
            + [pl.BlockSpec((8, LANES), lambda i, s: (0, 0))] * len(anchors),
            out_specs=[pl.BlockSpec((tr, C), lambda i, s: (i, 0)), pl.BlockSpec((None, tr, C), lambda i, s: (s[0], i, 0))]),
        out_shape=[jax.ShapeDtypeStruct((R, C), dtype), jax.ShapeDtypeStruct((N_SHARDS, R, C), dtype)],
        compiler_params=_cp("parallel"),
    )(mine, w, *anchors)


def _allreduce_small(v):
    rows = v.shape[0]

    def body(v_ref, out_ref, buf, send_sems, recv_sems):
        x, y, c, chips = _place()
        me, sibling = (x, y, c), (x, y, 1 - c)

        def slot(px, py, pc):
            return buf.at[4 * px + 2 * py + pc]

        def copy(k, block, to, src=None):
            return pltpu.make_async_remote_copy(
                src_ref=slot(*block) if src is None else src, dst_ref=slot(*block), send_sem=send_sems.at[k],
                recv_sem=recv_sems.at[k], device_id=to, device_id_type=MESH)

        slot(*me)[...] = v_ref[...]
        first = [copy(0, me, sibling, src=v_ref)] + [copy(1 + j, me, (*chip, c), src=v_ref) for j, chip in enumerate(chips)]
        for cp in first:
            cp.start()
        passed = [copy(4 + j, (*chip, c), sibling) for j, chip in enumerate(chips)]
        for j, chip in enumerate(chips):
            copy(1 + j, (*chip, c), me).wait_recv()
            passed[j].start()
        copy(0, sibling, me).wait_recv()
        for j, chip in enumerate(chips):
            copy(4 + j, (*chip, 1 - c), me).wait_recv()
        for cp in first + passed:
            cp.wait_send()
        acc = buf[0]
        for k in range(1, 8):
            acc = acc + buf[k]
        out_ref[...] = acc

    return pl.pallas_call(
        body, name="allreduce_small_grads",
        in_specs=[pl.BlockSpec(memory_space=pltpu.VMEM)], out_specs=pl.BlockSpec(memory_space=pltpu.VMEM),
        out_shape=jax.ShapeDtypeStruct((rows, LANES), F32),
        scratch_shapes=[pltpu.VMEM((8, rows, LANES), F32), pltpu.SemaphoreType.DMA((7,)), pltpu.SemaphoreType.DMA((7,))],
        compiler_params=pltpu.CompilerParams(vmem_limit_bytes=VMEM_LIMIT_BYTES),
    )(v)


MM_TM_K = 512
WGRAD_TM = 2048


def _rows_merged(w):
    return w.reshape(1, w.shape[0] * w.shape[1], w.shape[2])


def _sq_relu_epilogue(acc):
    r = jnp.maximum(acc, 0.0)
    return acc, r * r


def _add_epilogue(acc, x):
    return (acc + x,)


def _add_loss_epilogue(acc, x, target):
    e = acc + x - target
    D = e.shape[1]
    share = (0.5 / D) * jnp.sum(jnp.sum(e * e, axis=1, keepdims=True), axis=0, keepdims=True)
    return e * (1.0 / D), jnp.broadcast_to(share, (1, D))


def _add_norm_epilogue(acc, x, g):
    y = acc + x
    r = lax.rsqrt(jnp.mean(y * y, axis=-1, keepdims=True) + EPS)
    return y, y * r * g


def _norm_bwd_epilogue(dh, x, dres, g):
    r = lax.rsqrt(jnp.mean(x * x, axis=-1, keepdims=True) + EPS)
    xhat = x * r
    dxhat = dh * g
    dx = dres + r * (dxhat - xhat * jnp.mean(dxhat * xhat, axis=-1, keepdims=True))
    return dx, jnp.sum(dh * xhat, axis=0, keepdims=True)


def _sq_relu_grad_epilogue(acc, a):
    return (acc * (2.0 * jnp.maximum(a.astype(F32), 0.0)),)


STAGES = ("mixer_in", "mixer_out", "mlp")


def _stage_tensors(layer, stage):
    i = layer // 2
    if stage == "mlp":
        return [("mlp_w1", layer), ("mlp_w2", layer)]
    if stage == "mixer_in":
        return [("ab_w_in", i)] if layer % 2 == 0 else [("c_w_qkv", i)]
    return [("b_conv_w", i), ("ab_w_out", i)] if layer % 2 == 0 else [("c_w_out", i)]


def _local_step(x, target, p, weights_of, grads_done):
    S, D = x.shape
    depth = p["mix_norm_g"].shape[0]
    n_even = (depth + 1) // 2
    mix_g3 = p["mix_norm_g"].reshape(depth, 1, D)
    mlp_g3 = p["mlp_norm_g"].reshape(depth, 1, D)
    vec3 = lambda t: t.reshape(t.shape[0], 1, t.shape[1])
    spw16 = p["a_spatial_w"].astype(BF16)
    spw16_t = jnp.swapaxes(spw16, 2, 3)
    bias_full = jnp.repeat(jnp.swapaxes(p["a_spatial_b"], 1, 2), HEAD_DIM, axis=2)
    vn_g, vn_b, cn_g, cn_b, cb3 = (vec3(p[k]) for k in ("a_vnorm_g", "a_vnorm_b", "b_norm_g", "b_norm_b", "b_conv_b"))
    tables = _rope_tables(S)
    gq = jnp.tile(p["c_q_norm_g"], (1, 2))
    gk = jnp.tile(p["c_k_norm_g"], (1, 2))

    saved = []
    h = _rms_fwd("mix_norm_0", x, mix_g3, 0)
    for layer in range(depth):
        i = layer // 2
        wl = dict(weights_of(layer, "mixer_in", x))
        rec = {"x_mix": x, "w": wl, "h_mix": h}
        if layer % 2 == 0:
            (z,) = _mm_ngroup(f"ab_in_{layer}", h, wl["ab_w_in"], nt=False, tm=MM_TM_K, out_dtypes=[F32])
            wl.update(weights_of(layer, "mixer_out", z))
            gconv = _glu_conv_fwd(f"glu_conv_{layer}", z, wl["b_conv_w"], cb3, i)
            x, h, cat = _ab_tail_out_proj(f"ab_out_{layer}", z, gconv, spw16, bias_full, vn_g, vn_b, cn_g, cn_b, i,
                                          _rows_merged(wl["ab_w_out"]), x, mlp_g3, layer)
            rec.update(z=z, gconv=gconv, cat=cat)
        else:
            (qkv,) = _mm_ngroup(f"c_qkv_{layer}", h, wl["c_w_qkv"], nt=False, tm=MM_TM_K, out_dtypes=[F32])
            wl.update(weights_of(layer, "mixer_out", qkv))
            qn, kn = _qk_fwd(f"qk_norm_rope_{layer}", qkv, gq[i:i + 1], gk[i:i + 1], tables)
            os, lses = zip(*[_attn_fwd(f"attn_d{d}_{layer}", qn, kn, qkv, V_COL, d) for d in PATTERN_DILATIONS])
            x, h, o, lse = _merge_out_proj(f"c_out_{layer}", os, lses, _rows_merged(wl["c_w_out"]), x, mlp_g3, layer)
            rec.update(qkv=qkv, qn=qn, kn=kn, o=o, lse=lse)
        rec["x_mlp"] = x
        wl.update(weights_of(layer, "mlp", x))
        a, hsq = _mm_ngroup(f"mlp_up_{layer}", h, wl["mlp_w1"], nt=False, tm=MM_TM_K, out_dtypes=[BF16, BF16],
                            epilogue=_sq_relu_epilogue)
        rec.update(h_mlp=h, a=a, hsq=hsq)
        if layer + 1 < depth:
            x, h = _mm_kgroup(f"mlp_down_{layer}", hsq, _rows_merged(wl["mlp_w2"]), nt=False, tm=MM_TM_K,
                              out_dtypes=[F32, BF16], extras=(x,), vecs=[(mix_g3, layer + 1)], epilogue=_add_norm_epilogue)
        else:
            dx, loss_row = _mm_kgroup(f"mlp_down_{layer}", hsq, _rows_merged(wl["mlp_w2"]), nt=False, tm=MM_TM_K,
                                      out_dtypes=[F32], extras=(x, target), n_sums=1, epilogue=_add_loss_epilogue)
        saved.append(rec)

    small = {k: [None] * v.shape[0] for k, v in p.items()}
    token = None
    for layer in reversed(range(depth)):
        i = layer // 2
        rec = saved[layer]
        wl = rec["w"]
        g = {}
        (da,) = _mm_ngroup(f"mlp_down_dgrad_{layer}", dx, wl["mlp_w2"], nt=True, tm=MM_TM_K, out_dtypes=[BF16],
                           extras=(rec["a"],), epilogue=_sq_relu_grad_epilogue, anchor=token)
        g["mlp_w2"] = _wgrad(f"mlp_down_wgrad_{layer}", rec["hsq"], dx, wl["mlp_w2"].shape, a_group=True, tm=WGRAD_TM)
        g["mlp_w1"] = _wgrad(f"mlp_up_wgrad_{layer}", rec["h_mlp"], da, wl["mlp_w1"].shape, a_group=False, tm=WGRAD_TM)
        dx, small["mlp_norm_g"][layer] = _mm_kgroup(
            f"mlp_up_dgrad_{layer}", da, wl["mlp_w1"], nt=True, tm=MM_TM_K, out_dtypes=[F32], extras=(rec["x_mlp"], dx),
            vecs=[(mlp_g3, layer)], n_sums=1, epilogue=_norm_bwd_epilogue)
        token = grads_done(layer, "mlp", g)
        g = {}
        if layer % 2 == 0:
            w_out = _rows_merged(wl["ab_w_out"])
            (dcat,) = _mm_ngroup(f"ab_out_dgrad_{layer}", dx, w_out, nt=True, tm=MM_TM_K, out_dtypes=[F32], anchor=token)
            g["ab_w_out"] = _wgrad(f"ab_out_wgrad_{layer}", rec["cat"], dx, w_out.shape, a_group=True,
                                   tm=WGRAD_TM).reshape(wl["ab_w_out"].shape)
            dz, dgconv, dspw, dbias, dvg, dvb, dcg, dcb = _ab_tail_bwd(
                f"ab_tail_bwd_{layer}", rec["z"], rec["gconv"], dcat, spw16, spw16_t, bias_full, vn_g, vn_b, cn_g, cn_b, i)
            dz, g["b_conv_w"], dcbias = _glu_conv_bwd(f"glu_conv_bwd_{layer}", rec["z"], dgconv, dz, wl["b_conv_w"])
            token = grads_done(layer, "mixer_out", g)
            g = {}
            small["a_spatial_w"][i] = dspw
            small["a_spatial_b"][i] = _fold_bias(dbias)[:, :A_GROUPS].T
            for k, val in (("a_vnorm_g", dvg), ("a_vnorm_b", dvb), ("b_norm_g", dcg), ("b_norm_b", dcb), ("b_conv_b", dcbias)):
                small[k][i] = val
            g["ab_w_in"] = _wgrad(f"ab_in_wgrad_{layer}", rec["h_mix"], dz, wl["ab_w_in"].shape, a_group=False, tm=WGRAD_TM,
                                  anchor=token)
            dgrad = (f"ab_in_dgrad_{layer}", dz, wl["ab_w_in"])
        else:
            w_out = _rows_merged(wl["c_w_out"])
            do, delta = _mm_ngroup(f"c_out_dgrad_{layer}", dx, w_out, nt=True, tm=MM_TM_K, out_dtypes=[F32, F32],
                                   extras=(rec["o"],), epilogue=_delta_epilogue, anchor=token)
            g["c_w_out"] = _wgrad(f"c_out_wgrad_{layer}", rec["o"], dx, w_out.shape, a_group=True,
                                  tm=WGRAD_TM).reshape(wl["c_w_out"].shape)
            token = grads_done(layer, "mixer_out", g)
            g = {}
            attn_args = (rec["qn"], rec["kn"], rec["qkv"], V_COL, do, rec["lse"], delta)
            dqs = [_attn_bwd_q(f"attn_bwd_q_d{d}_{layer}", *attn_args, d) for d in PATTERN_DILATIONS]
            dks, dvs = zip(*[_attn_bwd_kv(f"attn_bwd_kv_d{d}_{layer}", *attn_args, d) for d in PATTERN_DILATIONS])
            dqkv, dgq, dgk = _qk_bwd(f"qk_norm_rope_bwd_{layer}", rec["qkv"], gq[i:i + 1], gk[i:i + 1], tables, dqs, dks, dvs)
            small["c_q_norm_g"][i] = dgq[:, :HEAD_DIM]
            small["c_k_norm_g"][i] = dgk[:, :HEAD_DIM]
            g["c_w_qkv"] = _wgrad(f"c_qkv_wgrad_{layer}", rec["h_mix"], dqkv, wl["c_w_qkv"].shape, a_group=False, tm=WGRAD_TM,
                                  anchor=token)
            dgrad = (f"c_qkv_dgrad_{layer}", dqkv, wl["c_w_qkv"])
        token = grads_done(layer, "mixer_in", g)
        dx, small["mix_norm_g"][layer] = _mm_kgroup(
            *dgrad, nt=True, tm=MM_TM_K, out_dtypes=[F32], extras=(rec["x_mix"], dx), vecs=[(mix_g3, layer)], n_sums=1,
            epilogue=_norm_bwd_epilogue, anchor=token)

    small = {k: jnp.stack([t.reshape(p[k].shape[1:]) for t in v]) for k, v in small.items()}
    return loss_row, dx, small


SHARDED = ("mlp_w1", "mlp_w2", "ab_w_in", "b_conv_w", "ab_w_out", "c_w_qkv", "c_w_out")
SMALL = ("mix_norm_g", "mlp_norm_g", "a_spatial_w", "a_spatial_b", "a_vnorm_g", "a_vnorm_b", "b_conv_b", "b_norm_g",
         "b_norm_b", "c_q_norm_g", "c_k_norm_g")
WEIGHTS = ("mix_norm_g", "mlp_norm_g", "mlp_w1", "mlp_w2", "ab_w_in", "a_spatial_w", "a_spatial_b", "a_vnorm_g",
           "a_vnorm_b", "b_conv_w", "b_conv_b", "b_norm_g", "b_norm_b", "ab_w_out", "c_w_qkv", "c_q_norm_g",
           "c_k_norm_g", "c_w_out")


def _pack(parts):
    flat = jnp.concatenate([parts[k].reshape(-1) for k in SMALL])
    rows = -(-flat.shape[0] // (256 * LANES)) * 256
    return jnp.pad(flat, (0, rows * LANES - flat.shape[0])).reshape(rows, LANES)


def _unpack(packed, like):
    flat = packed.reshape(-1)
    out, off = {}, 0
    for k in SMALL:
        n = like[k].size
        out[k] = flat[off:off + n].reshape(like[k].shape)
        off += n
    return out


def kernel(x, mix_norm_g, mlp_norm_g, mlp_w1, mlp_w2, ab_w_in, a_spatial_w, a_spatial_b, a_vnorm_g, a_vnorm_b, b_conv_w, b_conv_b, b_norm_g, b_norm_b, ab_w_out, c_w_qkv, c_q_norm_g, c_k_norm_g, c_w_out, loss_target, m_mix_norm_g, m_mlp_norm_g, m_mlp_w1, m_mlp_w2, m_ab_w_in, m_a_spatial_w, m_a_spatial_b, m_a_vnorm_g, m_a_vnorm_b, m_b_conv_w, m_b_conv_b, m_b_norm_g, m_b_norm_b, m_ab_w_out, m_c_w_qkv, m_c_q_norm_g, m_c_k_norm_g, m_c_w_out, v_mix_norm_g, v_mlp_norm_g, v_mlp_w1, v_mlp_w2, v_ab_w_in, v_a_spatial_w, v_a_spatial_b, v_a_vnorm_g, v_a_vnorm_b, v_b_conv_w, v_b_conv_b, v_b_norm_g, v_b_norm_b, v_ab_w_out, v_c_w_qkv, v_c_q_norm_g, v_c_k_norm_g, v_c_w_out):
    w = dict(mix_norm_g=mix_norm_g, mlp_norm_g=mlp_norm_g, mlp_w1=mlp_w1, mlp_w2=mlp_w2, ab_w_in=ab_w_in,
             a_spatial_w=a_spatial_w, a_spatial_b=a_spatial_b, a_vnorm_g=a_vnorm_g, a_vnorm_b=a_vnorm_b,
             b_conv_w=b_conv_w, b_conv_b=b_conv_b, b_norm_g=b_norm_g, b_norm_b=b_norm_b, ab_w_out=ab_w_out,
             c_w_qkv=c_w_qkv, c_q_norm_g=c_q_norm_g, c_k_norm_g=c_k_norm_g, c_w_out=c_w_out)
    m = dict(mix_norm_g=m_mix_norm_g, mlp_norm_g=m_mlp_norm_g, mlp_w1=m_mlp_w1, mlp_w2=m_mlp_w2, ab_w_in=m_ab_w_in,
             a_spatial_w=m_a_spatial_w, a_spatial_b=m_a_spatial_b, a_vnorm_g=m_a_vnorm_g, a_vnorm_b=m_a_vnorm_b,
             b_conv_w=m_b_conv_w, b_conv_b=m_b_conv_b, b_norm_g=m_b_norm_g, b_norm_b=m_b_norm_b, ab_w_out=m_ab_w_out,
             c_w_qkv=m_c_w_qkv, c_q_norm_g=m_c_q_norm_g, c_k_norm_g=m_c_k_norm_g, c_w_out=m_c_w_out)
    v = dict(mix_norm_g=v_mix_norm_g, mlp_norm_g=v_mlp_norm_g, mlp_w1=v_mlp_w1, mlp_w2=v_mlp_w2, ab_w_in=v_ab_w_in,
             a_spatial_w=v_a_spatial_w, a_spatial_b=v_a_spatial_b, a_vnorm_g=v_a_vnorm_g, a_vnorm_b=v_a_vnorm_b,
             b_conv_w=v_b_conv_w, b_conv_b=v_b_conv_b, b_norm_g=v_b_norm_g, b_norm_b=v_b_norm_b, ab_w_out=v_ab_w_out,
             c_w_qkv=v_c_w_qkv, c_q_norm_g=v_c_q_norm_g, c_k_norm_g=v_c_k_norm_g, c_w_out=v_c_w_out)

    S, D = x.shape[1], x.shape[2]
    depth = mix_norm_g.shape[0]
    mine = (2 * lax.axis_index("x") + lax.axis_index("y")).astype(jnp.int32).reshape(1)

    stages = [(layer, stage) for layer in range(depth) for stage in STAGES]

    def start_gather(name, some_stages, anchor):
        groups = [[_prepare_shard(f"prepare_{k}_{i}", w[k], i, F32 if k == "b_conv_w" else BF16, mine, anchor)
                   for k, i in _stage_tensors(*st)] for st in some_stages]
        return _exchange_start(name, "gather", groups)

    first, rest = stages[:len(STAGES)], stages[len(STAGES):]
    handles_first, token_first = start_gather("gather_weights_start_first", first, None)
    handles_rest, gather_token = start_gather("gather_weights_start_rest", rest, token_first)
    handles = dict(zip(first + rest, handles_first + handles_rest))

    def weights_of(layer, stage, after):
        _, got = _exchange_wait(f"gather_weights_wait_{layer}_{stage}", "gather", handles[layer, stage],
                                gather_token if (layer, stage) == stages[0] else after)
        return {k: a for (k, _), a in zip(_stage_tensors(layer, stage), got)}

    scattered = {}

    def grads_done(layer, stage, g):
        names = [k for k, _ in _stage_tensors(layer, stage)]
        group = [(g[k], lax.empty((N_PEERS["scatter"],) + g[k].shape[1:], BF16)) for k in names]
        (scattered[layer, stage],), token = _exchange_start(f"scatter_grads_start_{layer}_{stage}", "scatter", [group])
        return token

    small_params = {k: w[k] for k in SMALL}
    loss_row, dx, small_grads = _local_step(x.reshape(S, D), loss_target.reshape(S, D), small_params, weights_of, grads_done)

    loss = lax.psum(loss_row[0, 0], ("x", "y", "c"))

    order = _sum_order()
    stacked = {k: [lax.empty(w[k].shape, F32) for _ in range(4)] for k in SHARDED}
    for layer, stage in reversed(stages):
        gbs, recvs = _exchange_wait(f"scatter_grads_wait_{layer}_{stage}", "scatter", scattered[layer, stage], dx)
        for (k, i), gb, recv in zip(_stage_tensors(layer, stage), gbs, recvs):
            stacked[k] = _adamw_layer(f"adamw_{k}_{i}", w[k], m[k], v[k], i, gb, recv, order, stacked[k])
    grads, deltas, new_m, new_v = ({k: stacked[k][j] for k in SHARDED} for j in range(4))

    g_small = _allreduce_small(_pack(small_grads))
    outs = _adamw("adamw_small", _pack(small_params), _pack({k: m[k] for k in SMALL}), _pack({k: v[k] for k in SMALL}), g_small)
    for d_, packed in zip((grads, deltas, new_m, new_v), outs):
        d_.update(_unpack(packed, small_params))

    return (loss, dx.reshape(1, S, D), *[grads[k] for k in WEIGHTS], *[deltas[k] for k in WEIGHTS],
            *[new_m[k] for k in WEIGHTS], *[new_v[k] for k in WEIGHTS])
```

```python
import functools

import jax
import jax.numpy as jnp
from jax import lax
from jax.experimental import pallas as pl
from jax.experimental.pallas import tpu as pltpu

F32, BF16 = jnp.float32, jnp.bfloat16
MESH = pl.DeviceIdType.MESH
ANY = pl.BlockSpec(memory_space=pl.ANY)

VMEM_LIMIT_BYTES = 56 * 1024 * 1024
LANES = 128
ELEMENTWISE_ROWS = 256

EPS = 1e-6
NEG = -1e30
HEAD_DIM = 64
N_HEADS = 16
CHUNK = 128
A_GROUPS = 8
CONV_WIDTH = 31
CONV_HALO = 16
CONV_CHUNK = 64
BAND = 64
PATTERN_DILATIONS = (1, 4, 16)
ROT_DIM = 16
ROPE_THETA = 500000.0
N_SHARDS = 4

ADAM_LR, ADAM_B1, ADAM_B2, ADAM_EPS, ADAM_WD, ADAM_STEP = 0.001, 0.9, 0.999, 1e-08, 0.01, 10


def _cp(*sem):
    return pltpu.CompilerParams(dimension_semantics=sem, vmem_limit_bytes=VMEM_LIMIT_BYTES)


def _tile(n, pref):
    t = min(n, pref)
    assert n % t == 0, (n, pref)
    return t


def _dot(a, b, ca, cb):
    return lax.dot_general(a, b, (((ca,), (cb,)), ((), ())), preferred_element_type=F32)


def _mm_ngroup(name, a, w, *, nt, tm, out_dtypes, extras=(), epilogue=None, anchor=None):
    M, K = a.shape
    G, R, C = w.shape
    nw = R if nt else C
    assert K == (C if nt else R)
    tm = _tile(M, tm)
    n_ex = len(extras)
    anchors = [] if anchor is None else [anchor]

    def body(a_ref, w_ref, *rest):
        rest = rest[len(anchors):]
        av = a_ref[...].astype(BF16)
        for g in range(G):
            cols = slice(g * nw, (g + 1) * nw)
            acc = _dot(av, w_ref[g], 1, 1 if nt else 0)
            res = epilogue(acc, *[e[:, cols] for e in rest[:n_ex]]) if epilogue else (acc,)
            for o_ref, r in zip(rest[n_ex:], res):
                o_ref[:, cols] = r.astype(o_ref.dtype)

    blk = pl.BlockSpec((tm, G * nw), lambda m: (m, 0))
    return pl.pallas_call(
        body, name=name, grid=(M // tm,),
        in_specs=[pl.BlockSpec((tm, K), lambda m: (m, 0)), pl.BlockSpec((G, R, C), lambda m: (0, 0, 0))]
        + [pl.BlockSpec((8, LANES), lambda m: (0, 0))] * len(anchors) + [blk] * n_ex,
        out_specs=[blk] * len(out_dtypes),
        out_shape=[jax.ShapeDtypeStruct((M, G * nw), dt) for dt in out_dtypes],
        compiler_params=_cp("parallel"),
    )(a, w, *anchors, *extras)


def _mm_kgroup(name, a, w, *, nt, tm, out_dtypes, extras=(), vecs=(), n_sums=0, epilogue=None, anchor=None):
    G, R, C = w.shape
    kw, N = (C, R) if nt else (R, C)
    if a.ndim == 3:
        M = a.shape[1]
        assert a.shape[0] == G and a.shape[2] == kw
    else:
        M = a.shape[0]
        assert a.shape[1] == G * kw
    tm = _tile(M, tm)
    n_ex = len(extras)
    a_spec = (pl.BlockSpec((G, tm, kw), lambda m: (0, m, 0)) if a.ndim == 3 else pl.BlockSpec((tm, G * kw), lambda m: (m, 0)))
    anchors = [] if anchor is None else [anchor]

    def body(a_ref, w_ref, *rest):
        rest = rest[len(anchors):]
        acc = None
        for g in range(G):
            a_g = a_ref[g] if a.ndim == 3 else a_ref[:, g * kw:(g + 1) * kw]
            part = _dot(a_g.astype(BF16), w_ref[g], 1, 1 if nt else 0)
            acc = part if acc is None else acc + part
        n_in = n_ex + len(vecs)
        res = epilogue(acc, *[e[...] for e in rest[:n_in]]) if epilogue else (acc,)
        outs = rest[n_in:]
        n_tiles = len(outs) - n_sums
        for o_ref, r in zip(outs[:n_tiles], res[:n_tiles]):
            o_ref[...] = r.astype(o_ref.dtype)
        if n_sums:
            @pl.when(pl.program_id(0) == 0)
            def _():
                for s_ref in outs[n_tiles:]:
                    s_ref[...] = jnp.zeros_like(s_ref)

            for s_ref, r in zip(outs[n_tiles:], res[n_tiles:]):
                s_ref[...] += r

    blk = pl.BlockSpec((tm, N), lambda m: (m, 0))
    row = pl.BlockSpec((1, N), lambda m: (0, 0))
    return pl.pallas_call(
        body, name=name, grid=(M // tm,),
        in_specs=[a_spec, pl.BlockSpec((G, R, C), lambda m: (0, 0, 0))]
        + [pl.BlockSpec((8, LANES), lambda m: (0, 0))] * len(anchors) + [blk] * n_ex
        + [pl.BlockSpec((None, 1, N), lambda m, i=i: (i, 0, 0)) for _, i in vecs],
        out_specs=[blk] * len(out_dtypes) + [row] * n_sums,
        out_shape=[jax.ShapeDtypeStruct((M, N), dt) for dt in out_dtypes] + [jax.ShapeDtypeStruct((1, N), F32)] * n_sums,
        compiler_params=_cp("arbitrary" if n_sums else "parallel"),
    )(a, w, *anchors, *extras, *[v for v, _ in vecs])


def _wgrad(name, a, b, shape, *, a_group, tm, anchor=None):
    G, R, C = shape
    M = a.shape[0]
    tm = _tile(M, tm)
    n_m = M // tm
    anchors = [] if anchor is None else [anchor]

    def body(a_ref, b_ref, *rest):
        gb_ref, gf_ref = rest[len(anchors):]
        m = pl.program_id(1)
        part = _dot(a_ref[...].astype(BF16), b_ref[...].astype(BF16), 0, 0)

        @pl.when(m == 0)
        def _():
            gf_ref[...] = part

        @pl.when(m > 0)
        def _():
            gf_ref[...] += part

        @pl.when(m == n_m - 1)
        def _():
            gb_ref[...] = gf_ref[...].astype(BF16)

    a_spec = pl.BlockSpec((tm, R), (lambda g, m: (m, g)) if a_group else (lambda g, m: (m, 0)))
    if b.ndim == 3:
        assert not a_group
        b_spec = pl.BlockSpec((None, tm, C), lambda g, m: (g, m, 0))
    else:
        b_spec = pl.BlockSpec((tm, C), (lambda g, m: (m, 0)) if a_group else (lambda g, m: (m, g)))
    o_spec = pl.BlockSpec((None, R, C), lambda g, m: (g, 0, 0))
    return pl.pallas_call(
        body, name=name, grid=(G, n_m),
        in_specs=[a_spec, b_spec] + [pl.BlockSpec((8, LANES), lambda g, m: (0, 0))] * len(anchors), out_specs=o_spec,
        out_shape=jax.ShapeDtypeStruct(shape, BF16), scratch_shapes=[pltpu.VMEM((R, C), F32)],
        compiler_params=_cp("parallel", "arbitrary"),
    )(a, b, *anchors)


def _rms_fwd(name, x, g3, layer):
    S, D = x.shape
    tm = _tile(S, 512)

    def body(x_ref, g_ref, h_ref):
        xv = x_ref[...]
        r = lax.rsqrt(jnp.mean(xv * xv, axis=-1, keepdims=True) + EPS)
        h_ref[...] = (xv * r * g_ref[...]).astype(BF16)

    row = pl.BlockSpec((tm, D), lambda m: (m, 0))
    return pl.pallas_call(
        body, name=name, grid=(S // tm,),
        in_specs=[row, pl.BlockSpec((None, 1, D), lambda m: (layer, 0, 0))], out_specs=row,
        out_shape=jax.ShapeDtypeStruct((S, D), BF16), compiler_params=_cp("parallel"),
    )(x, g3)


def _adamw_math(w, m, v, g):
    m2 = ADAM_B1 * m + (1.0 - ADAM_B1) * g
    v2 = ADAM_B2 * v + (1.0 - ADAM_B2) * jnp.square(g)
    m_hat = m2 / (1.0 - ADAM_B1 ** ADAM_STEP)
    v_hat = v2 / (1.0 - ADAM_B2 ** ADAM_STEP)
    return g, -ADAM_LR * (m_hat / (jnp.sqrt(v_hat) + ADAM_EPS) + ADAM_WD * w), m2, v2


def _row_tile(rows):
    return _tile(rows, ELEMENTWISE_ROWS) if rows % ELEMENTWISE_ROWS == 0 else rows


def _adamw(name, w, m, v, g):
    rows, C = w.shape
    tr = _row_tile(rows)

    def body(w_ref, m_ref, v_ref, g_in, g_ref, d_ref, nm_ref, nv_ref):
        for o_ref, val in zip((g_ref, d_ref, nm_ref, nv_ref), _adamw_math(w_ref[...], m_ref[...], v_ref[...], g_in[...])):
            o_ref[...] = val

    blk = pl.BlockSpec((tr, C), lambda i: (i, 0))
    return pl.pallas_call(
        body, name=name, grid=(rows // tr,), in_specs=[blk] * 4, out_specs=[blk] * 4,
        out_shape=[jax.ShapeDtypeStruct((rows, C), F32)] * 4, compiler_params=_cp("parallel"),
    )(w, m, v, g)


def _sum_order():
    x, y, c = lax.axis_index("x"), lax.axis_index("y"), lax.axis_index("c")
    differs = lambda bit, coord: bit + coord - 2 * bit * coord
    slots = [4 * differs(p >> 2 & 1, x) + 2 * differs(p >> 1 & 1, y) + differs(p & 1, c) - 1 for p in range(8)]
    return [jnp.asarray(s, jnp.int32).reshape(1) for s in [2 * x + y, 4 * x + 2 * y + c] + slots]


def _adamw_layer(name, w, m, v, layer, gb, recv, order, outs):
    _, R, C = w.shape
    tr = _row_tile(R)
    n_s = len(order)

    def body(*refs):
        me = refs[1][0]
        w_ref, m_ref, v_ref, own_ref = refs[n_s:n_s + 4]
        theirs, outs_ = refs[n_s + 4:n_s + 12], refs[n_s + 16:]
        g = None
        for p in range(8):
            term = jnp.where(me == p, own_ref[...], theirs[p][...]).astype(F32)
            g = term if g is None else g + term
        for o_ref, val in zip(outs_, _adamw_math(w_ref[...], m_ref[...], v_ref[...], g)):
            o_ref[...] = val

    st = pl.BlockSpec((None, tr, C), lambda i, *s: (layer, i, 0))
    slot = lambda p: pl.BlockSpec((None, tr, C), lambda i, *s: (jnp.maximum(s[2 + p][0], 0), i, 0))
    return pl.pallas_call(
        body, name=name,
        grid_spec=pltpu.PrefetchScalarGridSpec(
            num_scalar_prefetch=n_s, grid=(R // tr,),
            in_specs=[st] * 3 + [pl.BlockSpec((None, tr, C), lambda i, *s: (s[0][0], i, 0))] + [slot(p) for p in range(8)]
            + [ANY] * 4,
            out_specs=[st] * 4),
        out_shape=[jax.ShapeDtypeStruct(w.shape, F32)] * 4, input_output_aliases={n_s + 12 + j: j for j in range(4)},
        compiler_params=_cp("parallel"),
    )(*order, w, m, v, gb, *([recv] * 8), *outs)


def _gelu(x):
    return x * (0.5 * (1.0 + jnp.tanh(0.7978845608028654 * (x + 0.044715 * (x * x * x)))))


def _layernorm(t, g, b):
    mu = jnp.mean(t, axis=-1, keepdims=True)
    var = jnp.mean(jnp.square(t - mu), axis=-1, keepdims=True)
    return (t - mu) * lax.rsqrt(var + EPS) * g + b


def _silu(x):
    return x * jax.nn.sigmoid(x)


def _a_value(zv, g, b):
    return _layernorm(_gelu(zv), g, b)


def _b_tail(gc, g, b):
    return _silu(_layernorm(gc, g, b))


def _first_head(shape):
    return lax.broadcasted_iota(jnp.int32, shape, len(shape) - 1) < HEAD_DIM


def _spatial_mix(spw_ref, vb, tm):
    first = _first_head((CHUNK, LANES))
    rows = []
    for n in range(tm // CHUNK):
        blocks = []
        for j in range(A_GROUPS // 2):
            vblk = vb[n * CHUNK:(n + 1) * CHUNK, j * LANES:(j + 1) * LANES]
            r0 = _dot(spw_ref[2 * j], vblk, 1, 0)
            r1 = _dot(spw_ref[2 * j + 1], vblk, 1, 0)
            blocks.append(jnp.where(first, r0, r1))
        rows.append(jnp.concatenate(blocks, axis=1))
    return jnp.concatenate(rows, axis=0) if len(rows) > 1 else rows[0]


def _ab_tail_out_proj(name, z, gconv, spw, bias_full, vn_g, vn_b, cn_g, cn_b, layer, w, x, g3, g_layer):
    S = z.shape[0]
    AW = 512
    tm = _tile(S, 256)

    def body(zu_ref, zv_ref, gc_ref, spw_ref, bias_ref, vg_ref, vb_ref, cg_ref, cb_ref, w_ref, x_ref, g_ref,
             xo_ref, h_ref, cat_ref):
        u = _gelu(zu_ref[...])
        v = _a_value(zv_ref[...], vg_ref[...], vb_ref[...])
        sv = _spatial_mix(spw_ref, v.astype(BF16), tm) + jnp.tile(bias_ref[...], (tm // CHUNK, 1))
        cat = jnp.concatenate([(u * sv).astype(BF16), _b_tail(gc_ref[...], cg_ref[...], cb_ref[...]).astype(BF16)], axis=1)
        cat_ref[...] = cat
        y, h = _add_norm_epilogue(_dot(cat, w_ref[0], 1, 0), x_ref[...], g_ref[...])
        xo_ref[...] = y
        h_ref[...] = h.astype(BF16)

    vec = pl.BlockSpec((None, 1, AW), lambda m: (layer, 0, 0))
    row = pl.BlockSpec((tm, 2 * AW), lambda m: (m, 0))
    return pl.pallas_call(
        body, name=name, grid=(S // tm,),
        in_specs=[pl.BlockSpec((tm, AW), lambda m: (m, 0)), pl.BlockSpec((tm, AW), lambda m: (m, 1)),
                  pl.BlockSpec((tm, AW), lambda m: (m, 0)),
                  pl.BlockSpec((None, A_GROUPS, CHUNK, CHUNK), lambda m: (layer, 0, 0, 0)),
                  pl.BlockSpec((None, CHUNK, AW), lambda m: (layer, 0, 0)), vec, vec, vec, vec,
                  pl.BlockSpec(w.shape, lambda m: (0, 0, 0)), row, pl.BlockSpec((None, 1, 2 * AW), lambda m: (g_layer, 0, 0))],
        out_specs=[row] * 3,
        out_shape=[jax.ShapeDtypeStruct((S, 2 * AW), dt) for dt in (F32, BF16, BF16)], compiler_params=_cp("parallel"),
    )(z, z, gconv, spw, bias_full, vn_g, vn_b, cn_g, cn_b, w, x, g3)


def _ab_tail_bwd(name, z, gconv, dcat, spw, spw_t, bias_full, vn_g, vn_b, cn_g, cn_b, layer):
    S = z.shape[0]
    AW = 512
    tm = _tile(S, 256)
    n_chunks = tm // CHUNK

    def body(zu_ref, zv_ref, gc_ref, dcat_ref, spw_ref, spwt_ref, bias_ref, vg_ref, vb_ref, cg_ref, cb_ref,
             dz_ref, dgc_ref, dspw_ref, dbias_ref, dvg_ref, dvb_ref, dcg_ref, dcb_ref):
        @pl.when(pl.program_id(0) == 0)
        def _():
            for r in (dspw_ref, dbias_ref, dvg_ref, dvb_ref, dcg_ref, dcb_ref):
                r[...] = jnp.zeros_like(r)

        dya = dcat_ref[:, :AW]
        dyb = dcat_ref[:, AW:]
        u, u_vjp = jax.vjp(_gelu, zu_ref[...])
        v, v_vjp = jax.vjp(_a_value, zv_ref[...], vg_ref[...], vb_ref[...])
        vb16 = v.astype(BF16)
        sv = _spatial_mix(spw_ref, vb16, tm) + jnp.tile(bias_ref[...], (n_chunks, 1))
        (dzu,) = u_vjp(dya * sv)
        dsv = dya * u
        dsv16 = dsv.astype(BF16)
        dv = _spatial_mix(spwt_ref, dsv16, tm)
        dzv, dvg, dvb = v_vjp(dv)
        dz_ref[0] = dzu
        dz_ref[1] = dzv
        dvg_ref[...] += dvg
        dvb_ref[...] += dvb

        first = _first_head((CHUNK, LANES))
        zero = jnp.zeros((), BF16)
        dbias = jnp.zeros((CHUNK, AW), F32)
        for n in range(n_chunks):
            rows = slice(n * CHUNK, (n + 1) * CHUNK)
            dbias = dbias + dsv[rows]
            for j in range(A_GROUPS // 2):
                cols = slice(j * LANES, (j + 1) * LANES)
                dblk, vblk = dsv16[rows, cols], vb16[rows, cols]
                dspw_ref[2 * j] += _dot(jnp.where(first, dblk, zero), vblk, 1, 1)
                dspw_ref[2 * j + 1] += _dot(jnp.where(first, zero, dblk), vblk, 1, 1)
        dbias_ref[...] += dbias

        _, t_vjp = jax.vjp(_b_tail, gc_ref[...], cg_ref[...], cb_ref[...])
        dgc, dcg, dcb = t_vjp(dyb)
        dgc_ref[...] = dgc
        dcg_ref[...] += dcg
        dcb_ref[...] += dcb

    vec = pl.BlockSpec((None, 1, AW), lambda m: (layer, 0, 0))
    spw_spec = pl.BlockSpec((None, A_GROUPS, CHUNK, CHUNK), lambda m: (layer, 0, 0, 0))
    ovec = pl.BlockSpec((1, AW), lambda m: (0, 0))
    return pl.pallas_call(
        body, name=name, grid=(S // tm,),
        in_specs=[pl.BlockSpec((tm, AW), lambda m: (m, 0)), pl.BlockSpec((tm, AW), lambda m: (m, 1)),
                  pl.BlockSpec((tm, AW), lambda m: (m, 0)), pl.BlockSpec((tm, 2 * AW), lambda m: (m, 0)),
                  spw_spec, spw_spec, pl.BlockSpec((None, CHUNK, AW), lambda m: (layer, 0, 0)), vec, vec, vec, vec],
        out_specs=[pl.BlockSpec((2, tm, AW), lambda m: (0, m, 0)), pl.BlockSpec((tm, AW), lambda m: (m, 0)),
                   pl.BlockSpec((A_GROUPS, CHUNK, CHUNK), lambda m: (0, 0, 0)),
                   pl.BlockSpec((CHUNK, AW), lambda m: (0, 0)), ovec, ovec, ovec, ovec],
        out_shape=[jax.ShapeDtypeStruct((4, S, AW), F32), jax.ShapeDtypeStruct((S, AW), F32),
                   jax.ShapeDtypeStruct((A_GROUPS, CHUNK, CHUNK), F32), jax.ShapeDtypeStruct((CHUNK, AW), F32)]
                  + [jax.ShapeDtypeStruct((1, AW), F32)] * 4,
        compiler_params=_cp("arbitrary"),
    )(z, z, gconv, dcat, spw, spw_t, bias_full, vn_g, vn_b, cn_g, cn_b)


def _fold_bias(dbias_full):
    def body(d_ref, o_ref):
        d = d_ref[...]
        hi = d.astype(BF16)
        lo = (d - hi.astype(F32)).astype(BF16)
        r = lax.broadcasted_iota(jnp.int32, (512, LANES), 0)
        c = lax.broadcasted_iota(jnp.int32, (512, LANES), 1)
        fold = jnp.where(lax.shift_right_logical(r, 6) == c, 1.0, 0.0).astype(BF16)
        o_ref[...] = _dot(hi, fold, 1, 0) + _dot(lo, fold, 1, 0)

    return pl.pallas_call(body, name="fold_spatial_bias", out_shape=jax.ShapeDtypeStruct((CHUNK, LANES), F32))(dbias_full)


def _halo_specs(tm, n_halo_blocks, col):
    r = tm // CONV_HALO
    prev = pl.BlockSpec((CONV_HALO, LANES), lambda j, i: (jnp.maximum(i * r - 1, 0), col + j))
    cur = pl.BlockSpec((tm, LANES), lambda j, i: (i, col + j))
    nxt = pl.BlockSpec((CONV_HALO, LANES), lambda j, i: (jnp.minimum((i + 1) * r, n_halo_blocks - 1), col + j))
    return [prev, cur, nxt]


def _fill_halo(scr, prev, cur, nxt, tm, i, n_i):
    scr[0:CONV_HALO, :] = jnp.where(i > 0, prev, 0.0)
    scr[CONV_HALO:CONV_HALO + tm, :] = cur
    scr[CONV_HALO + tm:2 * CONV_HALO + tm, :] = jnp.where(i < n_i - 1, nxt, 0.0)


def _glu_conv_fwd(name, z, cw, cb3, layer):
    S = z.shape[0]
    tm = _tile(S, 512)
    n_i = S // tm
    pad = CONV_WIDTH // 2

    def body(vp, vc, vn, gp, gc, gn, w_ref, b_ref, out_ref, scr):
        i = pl.program_id(1)
        glu = lambda a, b: a[...] * jax.nn.sigmoid(b[...])
        _fill_halo(scr, glu(vp, gp), glu(vc, gc), glu(vn, gn), tm, i, n_i)
        taps = [w_ref[j:j + 1, :] for j in range(CONV_WIDTH)]
        for c0 in range(0, tm, CONV_CHUNK):
            acc = jnp.zeros((CONV_CHUNK, LANES), F32) + b_ref[...]
            for j in range(CONV_WIDTH):
                acc = acc + taps[j] * scr[pl.ds(c0 + CONV_HALO - pad + j, CONV_CHUNK), :]
            out_ref[pl.ds(c0, CONV_CHUNK), :] = acc

    return pl.pallas_call(
        body, name=name, grid=(4, n_i),
        in_specs=_halo_specs(tm, S // CONV_HALO, 8) + _halo_specs(tm, S // CONV_HALO, 12)
        + [pl.BlockSpec((None, CONV_WIDTH, LANES), lambda j, i: (j, 0, 0)),
           pl.BlockSpec((None, 1, LANES), lambda j, i: (layer, 0, j))],
        out_specs=pl.BlockSpec((tm, LANES), lambda j, i: (i, j)),
        out_shape=jax.ShapeDtypeStruct((S, 4 * LANES), F32),
        scratch_shapes=[pltpu.VMEM((tm + 2 * CONV_HALO, LANES), F32)],
        compiler_params=_cp("parallel", "parallel"),
    )(z, z, z, z, z, z, cw, cb3)


def _glu_conv_bwd(name, z, dgconv, dz, cw):
    S = z.shape[0]
    tm = _tile(S, 512)
    n_i = S // tm
    pad = CONV_WIDTH // 2

    def body(vp, vc, vn, gp, gc, gn, dp, dc, dn, w_ref, dz_in, dz_ref, gb_ref, db_ref, g_scr, d_scr, gf_ref):
        i = pl.program_id(1)
        sig = jax.nn.sigmoid(gc[...])
        _fill_halo(g_scr, vp[...] * jax.nn.sigmoid(gp[...]), vc[...] * sig, vn[...] * jax.nn.sigmoid(gn[...]), tm, i, n_i)
        _fill_halo(d_scr, dp[...], dc[...], dn[...], tm, i, n_i)

        @pl.when(i == 0)
        def _():
            gf_ref[...] = jnp.zeros_like(gf_ref)
            db_ref[...] = jnp.zeros_like(db_ref)

        taps = [w_ref[j:j + 1, :] for j in range(CONV_WIDTH)]
        dw = [jnp.zeros((8, LANES), F32) for _ in range(CONV_WIDTH)]
        db = jnp.zeros((8, LANES), F32)
        fold8 = lambda t: jnp.sum(t.reshape(CONV_CHUNK // 8, 8, LANES), axis=0)
        for c0 in range(0, tm, CONV_CHUNK):
            rows = pl.ds(c0, CONV_CHUNK)
            d_cur = dc[rows, :]
            dglu = jnp.zeros((CONV_CHUNK, LANES), F32)
            for j in range(CONV_WIDTH):
                dglu = dglu + taps[j] * d_scr[pl.ds(c0 + CONV_HALO + pad - j, CONV_CHUNK), :]
                dw[j] = dw[j] + fold8(d_cur * g_scr[pl.ds(c0 + CONV_HALO - pad + j, CONV_CHUNK), :])
            db = db + fold8(d_cur)
            sig_c = jax.nn.sigmoid(gc[rows, :])
            dz_ref[0, rows, :] = dglu * sig_c
            dz_ref[1, rows, :] = dglu * vc[rows, :] * sig_c * (1.0 - sig_c)
        for j in range(CONV_WIDTH):
            gf_ref[j:j + 1, :] += jnp.sum(dw[j], axis=0, keepdims=True)
        db_ref[...] += jnp.sum(db, axis=0, keepdims=True)

        @pl.when(i == n_i - 1)
        def _():
            gb_ref[...] = gf_ref[...].astype(BF16)

    w_spec = pl.BlockSpec((None, CONV_WIDTH, LANES), lambda j, i: (j, 0, 0))
    return pl.pallas_call(
        body, name=name, grid=(4, n_i),
        in_specs=_halo_specs(tm, S // CONV_HALO, 8) + _halo_specs(tm, S // CONV_HALO, 12)
        + _halo_specs(tm, S // CONV_HALO, 0) + [w_spec, ANY],
        out_specs=[pl.BlockSpec((2, tm, LANES), lambda j, i: (1, i, j)),
                   w_spec, pl.BlockSpec((1, LANES), lambda j, i: (0, j))],
        out_shape=[jax.ShapeDtypeStruct(dz.shape, F32), jax.ShapeDtypeStruct(cw.shape, BF16),
                   jax.ShapeDtypeStruct((1, 4 * LANES), F32)],
        input_output_aliases={10: 0},
        scratch_shapes=[pltpu.VMEM((tm + 2 * CONV_HALO, LANES), F32)] * 2 + [pltpu.VMEM((CONV_WIDTH, LANES), F32)],
        compiler_params=_cp("parallel", "arbitrary"),
    )(z, z, z, z, z, z, dgconv, dgconv, dgconv, cw, dz)


def _seg_matrix(scale):
    r = lax.broadcasted_iota(jnp.int32, (LANES, LANES), 0)
    c = lax.broadcasted_iota(jnp.int32, (LANES, LANES), 1)
    return jnp.where(lax.shift_right_logical(r, 6) == lax.shift_right_logical(c, 6), scale, 0.0).astype(BF16)


def _seg_sum(x, seg):
    hi = x.astype(BF16)
    lo = (x - hi.astype(F32)).astype(BF16)
    return _dot(hi, seg, 1, 0) + _dot(lo, seg, 1, 0)


def _rope_tables(S):
    pos = jnp.arange(S, dtype=F32)
    inv_freq = ROPE_THETA ** (-jnp.arange(0, ROT_DIM, 2, dtype=F32) / ROT_DIM)
    ang = pos[:, None] * inv_freq[None, :]
    cos, sin = jnp.cos(ang), jnp.sin(ang)
    half = ROT_DIM // 2
    rest = HEAD_DIM - ROT_DIM
    one, zero = jnp.ones((S, rest), F32), jnp.zeros((S, rest), F32)
    zh = jnp.zeros((S, half), F32)
    c = jnp.concatenate([cos, cos, one], axis=1)
    sa = jnp.concatenate([-sin, zh, zero], axis=1)
    sb = jnp.concatenate([zh, sin, zero], axis=1)
    return [jnp.tile(t, (1, 2)) for t in (c, sa, sb)]


QK_CHUNK = 64


def _qk_fwd(name, qkv, gq, gk, tables):
    S = qkv.shape[0]
    W = N_HEADS * HEAD_DIM
    tm = _tile(S, 256)
    half = ROT_DIM // 2

    def body(q_ref, k_ref, gq_ref, gk_ref, c_ref, sa_ref, sb_ref, qn_ref, kn_ref):
        seg = _seg_matrix(1.0 / HEAD_DIM)
        for r0 in range(0, tm, QK_CHUNK):
            rows = pl.ds(r0, QK_CHUNK)
            c, sa, sb = c_ref[rows, :], sa_ref[rows, :], sb_ref[rows, :]
            for t_ref, g_ref, o_ref in ((q_ref, gq_ref, qn_ref), (k_ref, gk_ref, kn_ref)):
                for blk in range(W // LANES):
                    cols = slice(blk * LANES, (blk + 1) * LANES)
                    t = t_ref[rows, cols]
                    y = t * lax.rsqrt(_seg_sum(t * t, seg) + EPS) * g_ref[...]
                    o_ref[rows, cols] = y * c + pltpu.roll(y, LANES - half, 1) * sa + pltpu.roll(y, half, 1) * sb

    row = lambda k: pl.BlockSpec((tm, W), lambda m: (m, k))
    gain = pl.BlockSpec((1, LANES), lambda m: (0, 0))
    tab = pl.BlockSpec((tm, LANES), lambda m: (m, 0))
    return pl.pallas_call(
        body, name=name, grid=(S // tm,),
        in_specs=[row(0), row(1), gain, gain, tab, tab, tab], out_specs=[row(0)] * 2,
        out_shape=[jax.ShapeDtypeStruct((S, W), F32)] * 2, compiler_params=_cp("parallel"),
    )(qkv, qkv, gq, gk, *tables)


def _qk_bwd(name, qkv, gq, gk, tables, dqs, dks, dvs):
    S = qkv.shape[0]
    W = N_HEADS * HEAD_DIM
    tm = _tile(S, 256)
    half = ROT_DIM // 2
    n_p = len(dqs)

    def body(q_ref, k_ref, gq_ref, gk_ref, c_ref, sa_ref, sb_ref, *rest):
        dq_refs, dk_refs, dv_refs = rest[:n_p], rest[n_p:2 * n_p], rest[2 * n_p:3 * n_p]
        dqkv_ref, dgq_ref, dgk_ref = rest[3 * n_p:]

        @pl.when(pl.program_id(0) == 0)
        def _():
            dgq_ref[...] = jnp.zeros_like(dgq_ref)
            dgk_ref[...] = jnp.zeros_like(dgk_ref)

        seg = _seg_matrix(1.0 / HEAD_DIM)
        r_i = lax.broadcasted_iota(jnp.int32, (LANES, LANES), 0)
        c_i = lax.broadcasted_iota(jnp.int32, (LANES, LANES), 1)
        same_dim = jnp.where((r_i & (HEAD_DIM - 1)) == (c_i & (HEAD_DIM - 1)), 1.0, 0.0).astype(BF16)
        dgs = [jnp.zeros((8, LANES), F32), jnp.zeros((8, LANES), F32)]
        fold8 = lambda t: jnp.sum(t.reshape(QK_CHUNK // 8, 8, LANES), axis=0)
        for r0 in range(0, tm, QK_CHUNK):
            rows = pl.ds(r0, QK_CHUNK)
            c, sa, sb = c_ref[rows, :], sa_ref[rows, :], sb_ref[rows, :]
            for idx, (t_ref, g_ref, d_refs) in enumerate(((q_ref, gq_ref, dq_refs), (k_ref, gk_ref, dk_refs))):
                for blk in range(W // LANES):
                    cols = slice(blk * LANES, (blk + 1) * LANES)
                    dout = d_refs[0][rows, cols]
                    for r in d_refs[1:]:
                        dout = dout + r[rows, cols]
                    dy = dout * c + pltpu.roll(dout * sa, half, 1) + pltpu.roll(dout * sb, LANES - half, 1)
                    t = t_ref[rows, cols]
                    r_ = lax.rsqrt(_seg_sum(t * t, seg) + EPS)
                    xhat = t * r_
                    dgs[idx] = dgs[idx] + fold8(dy * xhat)
                    dxhat = dy * g_ref[...]
                    dt = r_ * (dxhat - xhat * _seg_sum(dxhat * xhat, seg))
                    dqkv_ref[rows, idx * W + blk * LANES: idx * W + (blk + 1) * LANES] = dt.astype(BF16)
            dv = dv_refs[0][rows, :]
            for r in dv_refs[1:]:
                dv = dv + r[rows, :]
            dqkv_ref[rows, 2 * W:] = dv.astype(BF16)
        for dg, dg_ref in zip(dgs, (dgq_ref, dgk_ref)):
            dg_ref[...] += jnp.sum(_seg_sum(dg, same_dim), axis=0, keepdims=True)

    row = lambda k: pl.BlockSpec((tm, W), lambda m: (m, k))
    gain = pl.BlockSpec((1, LANES), lambda m: (0, 0))
    tab = pl.BlockSpec((tm, LANES), lambda m: (m, 0))
    return pl.pallas_call(
        body, name=name, grid=(S // tm,),
        in_specs=[row(0), row(1), gain, gain, tab, tab, tab] + [row(0)] * (3 * n_p),
        out_specs=[pl.BlockSpec((tm, 3 * W), lambda m: (m, 0)), gain, gain],
        out_shape=[jax.ShapeDtypeStruct((S, 3 * W), BF16), jax.ShapeDtypeStruct((1, LANES), F32),
                   jax.ShapeDtypeStruct((1, LANES), F32)],
        compiler_params=_cp("arbitrary"),
    )(qkv, qkv, gq, gk, *tables, *dqs, *dks, *dvs)


ATTN_BQ = 2 * BAND
ATTN_ROWS = 16 * ATTN_BQ
V_COL = 2 * N_HEADS * HEAD_DIM // LANES


def _attn_geometry(S, d):
    rows = min(ATTN_ROWS, S)
    halo = BAND * d
    assert rows % (ATTN_BQ * d) == 0 and S % rows == 0, (S, d)
    return rows, halo, rows // (ATTN_BQ * d)


def _attn_specs(S, d, col):
    rows, halo, _ = _attn_geometry(S, d)
    r = rows // halo
    n_h = S // halo
    prev = pl.BlockSpec((halo, LANES), lambda j, i: (jnp.maximum(i * r - 1, 0), col + j))
    cur = pl.BlockSpec((rows, LANES), lambda j, i: (i, col + j))
    nxt = pl.BlockSpec((halo, LANES), lambda j, i: (jnp.minimum((i + 1) * r, n_h - 1), col + j))
    return [prev, cur, nxt]


def _fill_window(scr, prev, cur, nxt, rows, halo):
    scr[0:halo, :] = prev[...]
    scr[halo:halo + rows, :] = cur[...]
    scr[halo + rows:2 * halo + rows, :] = nxt[...]


def _chain_groups(n_sb, d, size):
    chains = [(sb, r) for sb in range(n_sb) for r in range(d)]
    return [chains[j:j + size] for j in range(0, len(chains), size)]


def _strided(ref, start, size, d):
    return ref[pl.ds(start, size, stride=d) if d > 1 else pl.ds(start, size), :]


def _band_mask(i, S, d, sb):
    rows, _, _ = _attn_geometry(S, d)
    L = S // d
    base = i * (rows // d) + sb * ATTN_BQ
    wk = ATTN_BQ + 2 * BAND
    row = lax.broadcasted_iota(jnp.int32, (ATTN_BQ, wk), 0)
    col = lax.broadcasted_iota(jnp.int32, (ATTN_BQ, wk), 1)
    lj = base - BAND + col
    return (jnp.abs(col - BAND - row) <= BAND) & (lj >= 0) & (lj < L)


def _attn_fwd(name, q, k, v, v_col, d):
    S, W = q.shape
    rows, halo, n_sb = _attn_geometry(S, d)
    wk = ATTN_BQ + 2 * BAND
    scale = HEAD_DIM ** -0.5

    def body(q_ref, kp, kc, kn, vp, vc, vn, o_ref, lse_ref, kw, vw):
        i = pl.program_id(1)
        _fill_window(kw, kp, kc, kn, rows, halo)
        _fill_window(vw, vp, vc, vn, rows, halo)
        first = _first_head((ATTN_BQ, LANES))
        heads = (first, jnp.logical_not(first))
        zero = jnp.zeros((), BF16)
        for group in _chain_groups(n_sb, d, 4):
            masks = {sb: _band_mask(i, S, d, sb) for sb in sorted({sb for sb, _ in group})}
            starts = [r + d * sb * ATTN_BQ for sb, r in group]
            qs = [_strided(q_ref, st, ATTN_BQ, d).astype(BF16) for st in starts]
            ks = [_strided(kw, st, wk, d).astype(BF16) for st in starts]
            vs = [_strided(vw, st, wk, d).astype(BF16) for st in starts]
            s_all = [[_dot(jnp.where(hm, qv, zero), kv, 1, 1) for hm in heads] for qv, kv in zip(qs, ks)]
            p_all, den_all, lse_all = [], [], []
            for (sb, _), s_h in zip(group, s_all):
                s_h = [jnp.where(masks[sb], s * scale, NEG) for s in s_h]
                mx_h = [jnp.max(s, axis=-1, keepdims=True) for s in s_h]
                p_h = [jnp.exp(s - mx) for s, mx in zip(s_h, mx_h)]
                den_h = [jnp.sum(p, axis=-1, keepdims=True) for p in p_h]
                p_all.append([p.astype(BF16) for p in p_h])
                den_all.append(den_h)
                lse_all.append([mx + jnp.log(den) for mx, den in zip(mx_h, den_h)])
            o_all = [[_dot(p, vv, 1, 0) for p in p_h] for p_h, vv in zip(p_all, vs)]
            for st, o_h, den_h, lse_h in zip(starts, o_all, den_all, lse_all):
                dst = pl.ds(st, ATTN_BQ, stride=d) if d > 1 else pl.ds(st, ATTN_BQ)
                o_ref[dst, :] = jnp.where(first, o_h[0] / den_h[0], o_h[1] / den_h[1])
                lse_ref[dst, :] = jnp.where(first, lse_h[0], lse_h[1])

    cur = _attn_specs(S, d, 0)[1]
    return pl.pallas_call(
        body, name=name, grid=(W // LANES, S // rows),
        in_specs=[cur] + _attn_specs(S, d, 0) + _attn_specs(S, d, v_col), out_specs=[cur, cur],
        out_shape=[jax.ShapeDtypeStruct((S, W), F32)] * 2,
        scratch_shapes=[pltpu.VMEM((rows + 2 * halo, LANES), F32)] * 2,
        compiler_params=_cp("parallel", "parallel"),
    )(q, k, k, k, v, v, v)


def _merge_out_proj(name, os, lses, w, x, g3, layer):
    S, W = os[0].shape
    tm = _tile(S, 256)
    n_p = len(os)

    def body(*refs):
        o_refs, l_refs = refs[:n_p], refs[n_p:2 * n_p]
        w_ref, x_ref, g_ref, xo_ref, h_ref, o_ref, lt_ref = refs[2 * n_p:]
        ls = [r[...] for r in l_refs]
        mx = functools.reduce(jnp.maximum, ls)
        es = [jnp.exp(l - mx) for l in ls]
        den = functools.reduce(lambda a, b: a + b, es)
        acc = es[0] * o_refs[0][...]
        for e, r in zip(es[1:], o_refs[1:]):
            acc = acc + e * r[...]
        o = (acc / den).astype(BF16)
        o_ref[...] = o
        lt_ref[...] = mx + jnp.log(den)
        y, h = _add_norm_epilogue(_dot(o, w_ref[0], 1, 0), x_ref[...], g_ref[...])
        xo_ref[...] = y
        h_ref[...] = h.astype(BF16)

    row = pl.BlockSpec((tm, W), lambda m: (m, 0))
    return pl.pallas_call(
        body, name=name, grid=(S // tm,),
        in_specs=[row] * (2 * n_p) + [pl.BlockSpec(w.shape, lambda m: (0, 0, 0)), row,
                                      pl.BlockSpec((None, 1, W), lambda m: (layer, 0, 0))],
        out_specs=[row] * 4,
        out_shape=[jax.ShapeDtypeStruct((S, W), dt) for dt in (F32, BF16, BF16, F32)],
        compiler_params=_cp("parallel"),
    )(*os, *lses, w, x, g3)


def _delta_epilogue(do, o):
    seg = _seg_matrix(1.0)
    prod = do * o.astype(F32)
    delta = [_seg_sum(prod[:, blk * LANES:(blk + 1) * LANES], seg) for blk in range(do.shape[1] // LANES)]
    return do, jnp.concatenate(delta, axis=1)


def _attn_bwd_q(name, q, k, v, v_col, do, lse, delta, d):
    S, W = q.shape
    rows, halo, n_sb = _attn_geometry(S, d)
    wk = ATTN_BQ + 2 * BAND
    scale = HEAD_DIM ** -0.5

    def body(q_ref, do_ref, l_ref, dl_ref, kp, kc, kn, vp, vc, vn, dq_ref, kw, vw):
        i = pl.program_id(1)
        _fill_window(kw, kp, kc, kn, rows, halo)
        _fill_window(vw, vp, vc, vn, rows, halo)
        first = _first_head((ATTN_BQ, LANES))
        heads = (first, jnp.logical_not(first))
        zero = jnp.zeros((), BF16)
        wide = lambda t: jnp.concatenate([t] * (wk // LANES), axis=1)
        for group in _chain_groups(n_sb, d, 4):
            masks = {sb: _band_mask(i, S, d, sb) for sb in sorted({sb for sb, _ in group})}
            starts = [r + d * sb * ATTN_BQ for sb, r in group]
            qs = [_strided(q_ref, st, ATTN_BQ, d).astype(BF16) for st in starts]
            dos = [_strided(do_ref, st, ATTN_BQ, d).astype(BF16) for st in starts]
            ks = [_strided(kw, st, wk, d).astype(BF16) for st in starts]
            vs = [_strided(vw, st, wk, d).astype(BF16) for st in starts]
            s_all = [[_dot(jnp.where(hm, qv, zero), kv, 1, 1) for hm in heads] for qv, kv in zip(qs, ks)]
            dp_all = [[_dot(jnp.where(hm, dov, zero), vv, 1, 1) for hm in heads] for dov, vv in zip(dos, vs)]
            ds_all = []
            for (sb, _), st, s_h, dp_h in zip(group, starts, s_all, dp_all):
                lv, dlv = _strided(l_ref, st, ATTN_BQ, d), _strided(dl_ref, st, ATTN_BQ, d)
                l_sw, dl_sw = pltpu.roll(lv, HEAD_DIM, 1), pltpu.roll(dlv, HEAD_DIM, 1)
                ds_h = []
                for hm, s, dp in zip(heads, s_h, dp_h):
                    p = jnp.exp(jnp.where(masks[sb], s * scale, NEG) - wide(jnp.where(hm, lv, l_sw)))
                    ds_h.append((p * (dp - wide(jnp.where(hm, dlv, dl_sw))) * scale).astype(BF16))
                ds_all.append(ds_h)
            dq_all = [[_dot(ds, kv, 1, 0) for ds in ds_h] for ds_h, kv in zip(ds_all, ks)]
            for st, dq_h in zip(starts, dq_all):
                dst = pl.ds(st, ATTN_BQ, stride=d) if d > 1 else pl.ds(st, ATTN_BQ)
                dq_ref[dst, :] = jnp.where(first, dq_h[0], dq_h[1])

    cur = _attn_specs(S, d, 0)[1]
    return pl.pallas_call(
        body, name=name, grid=(W // LANES, S // rows),
        in_specs=[cur] * 4 + _attn_specs(S, d, 0) + _attn_specs(S, d, v_col), out_specs=cur,
        out_shape=jax.ShapeDtypeStruct((S, W), F32),
        scratch_shapes=[pltpu.VMEM((rows + 2 * halo, LANES), F32)] * 2,
        compiler_params=_cp("parallel", "parallel"),
    )(q, do, lse, delta, k, k, k, v, v, v)


def _attn_bwd_kv(name, q, k, v, v_col, do, lse, delta, d):
    S, W = q.shape
    rows, halo, n_sb = _attn_geometry(S, d)
    wk = ATTN_BQ + 2 * BAND
    scale = HEAD_DIM ** -0.5

    def body(k_ref, v_ref, qp, qc, qn, dop, doc, don, lp, lc, ln, dlp, dlc, dln, dk_ref, dv_ref, qw, dow, lw, dlw):
        i = pl.program_id(1)
        _fill_window(qw, qp, qc, qn, rows, halo)
        _fill_window(dow, dop, doc, don, rows, halo)
        _fill_window(lw, lp, lc, ln, rows, halo)
        _fill_window(dlw, dlp, dlc, dln, rows, halo)
        first = _first_head((ATTN_BQ, LANES))
        heads = (first, jnp.logical_not(first))
        zero = jnp.zeros((), BF16)
        for group in _chain_groups(n_sb, d, 2):
            masks = {sb: _band_mask(i, S, d, sb) for sb in sorted({sb for sb, _ in group})}
            starts = [r + d * sb * ATTN_BQ for sb, r in group]
            ks = [_strided(k_ref, st, ATTN_BQ, d).astype(BF16) for st in starts]
            vs = [_strided(v_ref, st, ATTN_BQ, d).astype(BF16) for st in starts]
            qs = [_strided(qw, st, wk, d).astype(BF16) for st in starts]
            dos = [_strided(dow, st, wk, d).astype(BF16) for st in starts]
            s_all = [[_dot(jnp.where(hm, kv, zero), qv, 1, 1) for hm in heads] for kv, qv in zip(ks, qs)]
            dp_all = [[_dot(jnp.where(hm, vv, zero), dov, 1, 1) for hm in heads] for vv, dov in zip(vs, dos)]
            p_all, ds_all = [], []
            for (sb, _), st, s_h, dp_h in zip(group, starts, s_all, dp_all):
                l_t, dl_t = _strided(lw, st, wk, d).T, _strided(dlw, st, wk, d).T
                p_h = [jnp.exp(jnp.where(masks[sb], s * scale, NEG) - l_t[hh * HEAD_DIM:hh * HEAD_DIM + 1, :])
                       for hh, s in enumerate(s_h)]
                ds_all.append([(p * (dp - dl_t[hh * HEAD_DIM:hh * HEAD_DIM + 1, :]) * scale).astype(BF16)
                               for hh, (p, dp) in enumerate(zip(p_h, dp_h))])
                p_all.append([p.astype(BF16) for p in p_h])
            dv_all = [[_dot(p, dov, 1, 0) for p in p_h] for p_h, dov in zip(p_all, dos)]
            dk_all = [[_dot(ds, qv, 1, 0) for ds in ds_h] for ds_h, qv in zip(ds_all, qs)]
            for st, dk_h, dv_h in zip(starts, dk_all, dv_all):
                dst = pl.ds(st, ATTN_BQ, stride=d) if d > 1 else pl.ds(st, ATTN_BQ)
                dk_ref[dst, :] = jnp.where(first, dk_h[0], dk_h[1])
                dv_ref[dst, :] = jnp.where(first, dv_h[0], dv_h[1])

    cur = _attn_specs(S, d, 0)[1]
    win = _attn_specs(S, d, 0)
    return pl.pallas_call(
        body, name=name, grid=(W // LANES, S // rows),
        in_specs=[cur, _attn_specs(S, d, v_col)[1]] + win * 4, out_specs=[cur, cur],
        out_shape=[jax.ShapeDtypeStruct((S, W), F32)] * 2,
        scratch_shapes=[pltpu.VMEM((rows + 2 * halo, LANES), F32)] * 4,
        compiler_params=_cp("parallel", "parallel"),
    )(k, v, q, q, q, do, do, do, lse, lse, lse, delta, delta, delta)


def _place():
    x, y, c = lax.axis_index("x"), lax.axis_index("y"), lax.axis_index("c")
    chips = [(1 - x, y), (x, 1 - y), (1 - x, 1 - y)]
    return x, y, c, chips


HBM = pl.BlockSpec(memory_space=pltpu.HBM)
SEM = pl.BlockSpec(memory_space=pltpu.SEMAPHORE)
DATAFLOW = pltpu.SideEffectType.DATAFLOW_SIDE_EFFECTING


N_PEERS = {"gather": 3, "scatter": 7, "allgather": 7}


def _exchange_copies(kind, srcs, dsts, send_sems, recv_sems):
    x, y, c, chips = _place()
    mine = 2 * x + y
    n_peers = N_PEERS[kind]
    cps = []
    for t in range(len(srcs)):
        for k in range(n_peers):
            if kind == "gather":
                (px, py), pc = chips[k], c
                src, dst = srcs[t], dsts[t].at[mine]
            else:
                bits = k + 1
                px, py, pc = (1 - x if bits & 4 else x), (1 - y if bits & 2 else y), (1 - c if bits & 1 else c)
                src, dst = (srcs[t].at[2 * px + py] if kind == "scatter" else srcs[t]), dsts[t].at[k]
            cps.append(pltpu.make_async_remote_copy(
                src_ref=src, dst_ref=dst, send_sem=send_sems.at[n_peers * t + k], recv_sem=recv_sems.at[n_peers * t + k],
                device_id=(px, py, pc), device_id_type=MESH))
    return cps


def _exchange_start(name, kind, groups):
    sizes = [len(g) for g in groups]
    n, n_g = sum(sizes), len(groups)

    def body(*refs):
        srcs, dsts = refs[:n], refs[n:2 * n]
        sems = refs[2 * n:2 * n + 2 * n_g]
        token = refs[4 * n + 2 * n_g]
        off = 0
        for gi, size in enumerate(sizes):
            for cp in _exchange_copies(kind, srcs[off:off + size], dsts[off:off + size], sems[2 * gi], sems[2 * gi + 1]):
                cp.start()
            off += size
        token[...] = jnp.zeros_like(token)

    arrays = [pltpu.with_memory_space_constraint(a, pltpu.HBM) for a in
              [s for g in groups for s, _ in g] + [d for g in groups for _, d in g]]
    sem_shapes = []
    for size in sizes:
        sem_shapes += [pltpu.SemaphoreType.DMA((N_PEERS[kind] * size,))] * 2
    outs = pl.pallas_call(
        body, name=name,
        in_specs=[HBM] * (2 * n),
        out_specs=[SEM] * (2 * n_g) + [HBM] * (2 * n) + [pl.BlockSpec(memory_space=pltpu.VMEM)],
        out_shape=sem_shapes + [pltpu.HBM(a.shape, a.dtype) for a in arrays] + [jax.ShapeDtypeStruct((8, LANES), F32)],
        input_output_aliases={t: 2 * n_g + t for t in range(2 * n)},
        compiler_params=pltpu.CompilerParams(has_side_effects=DATAFLOW),
    )(*arrays)
    sems, thru, token = outs[:2 * n_g], outs[2 * n_g:-1], outs[-1]
    handles, off = [], 0
    for gi, size in enumerate(sizes):
        handles.append((sems[2 * gi], sems[2 * gi + 1], thru[off:off + size], thru[n + off:n + off + size]))
        off += size
    return handles, token


def _exchange_wait(name, kind, handle, after):
    send_sems, recv_sems, srcs, dsts = handle
    n = len(srcs)

    def body(*refs):
        for cp in _exchange_copies(kind, refs[:n], refs[n:2 * n], refs[2 * n], refs[2 * n + 1]):
            cp.wait_send()
            cp.wait_recv()

    outs = pl.pallas_call(
        body, name=name,
        in_specs=[HBM] * (2 * n) + [SEM, SEM, ANY], out_specs=[HBM] * (2 * n),
        out_shape=[pltpu.HBM(a.shape, a.dtype) for a in (*srcs, *dsts)],
        input_output_aliases={t: t for t in range(2 * n)},
        compiler_params=pltpu.CompilerParams(has_side_effects=DATAFLOW),
    )(*srcs, *dsts, send_sems, recv_sems, after)
    return outs[:n], outs[n:]


def _prepare_shard(name, w, idx, dtype, mine, anchor=None):
    _, R, C = w.shape
    tr = _row_tile(R)
    anchors = [] if anchor is None else [anchor]

    def body(mine_ref, w_ref, *rest):
        src_ref, land_ref = rest[len(anchors):]
        val = w_ref[...].astype(dtype)
        src_ref[...] = val
        land_ref[...] = val

    return pl.pallas_call(
        body, name=name,
        grid_spec=pltpu.PrefetchScalarGridSpec(
            num_scalar_prefetch=1, grid=(R // tr,),
            in_specs=[pl.BlockSpec((None, tr, C), lambda i, s: (idx, i, 0))]
            + [pl.BlockSpec((8, LANES), lambda i, s: (0, 0))] * len(anchors),
            out_specs=[pl.BlockSpec((tr, C), lambda i, s: (i, 0)), pl.BlockSpec((None, tr, C), lambda i, s: (s[0], i, 0))]),
        out_shape=[jax.ShapeDtypeStruct((R, C), dtype), jax.ShapeDtypeStruct((N_SHARDS, R, C), dtype)],
        compiler_params=_cp("parallel"),
    )(mine, w, *anchors)


def _ordered_sum(name, own, recv, order):
    rows, C = own.shape
    tr = _row_tile(rows)
    n_s = len(order)

    def body(*refs):
        me = refs[1][0]
        own_ref, theirs, out_ref = refs[n_s], refs[n_s + 1:n_s + 9], refs[n_s + 9]
        g = None
        for p in range(8):
            term = jnp.where(me == p, own_ref[...], theirs[p][...])
            g = term if g is None else g + term
        out_ref[...] = g

    blk = pl.BlockSpec((tr, C), lambda i, *s: (i, 0))
    slot = lambda p: pl.BlockSpec((None, tr, C), lambda i, *s: (jnp.maximum(s[2 + p][0], 0), i, 0))
    return pl.pallas_call(
        body, name=name,
        grid_spec=pltpu.PrefetchScalarGridSpec(num_scalar_prefetch=n_s, grid=(rows // tr,),
                                               in_specs=[blk] + [slot(p) for p in range(8)], out_specs=blk),
        out_shape=jax.ShapeDtypeStruct((rows, C), F32), compiler_params=_cp("parallel"),
    )(*order, own, *([recv] * 8))


MM_TM_K = 512
WGRAD_TM = 2048


def _rows_merged(w):
    return w.reshape(1, w.shape[0] * w.shape[1], w.shape[2])


def _sq_relu_epilogue(acc):
    r = jnp.maximum(acc, 0.0)
    return acc, r * r


def _add_epilogue(acc, x):
    return (acc + x,)


def _add_loss_epilogue(acc, x, target):
    e = acc + x - target
    D = e.shape[1]
    share = (0.5 / D) * jnp.sum(jnp.sum(e * e, axis=1, keepdims=True), axis=0, keepdims=True)
    return e * (1.0 / D), jnp.broadcast_to(share, (1, D))


def _add_norm_epilogue(acc, x, g):
    y = acc + x
    r = lax.rsqrt(jnp.mean(y * y, axis=-1, keepdims=True) + EPS)
    return y, y * r * g


def _norm_bwd_epilogue(dh, x, dres, g):
    r = lax.rsqrt(jnp.mean(x * x, axis=-1, keepdims=True) + EPS)
    xhat = x * r
    dxhat = dh * g
    dx = dres + r * (dxhat - xhat * jnp.mean(dxhat * xhat, axis=-1, keepdims=True))
    return dx, jnp.sum(dh * xhat, axis=0, keepdims=True)


def _sq_relu_grad_epilogue(acc, a):
    return (acc * (2.0 * jnp.maximum(a.astype(F32), 0.0)),)


STAGES = ("mixer_in", "mixer_out", "mlp")


def _stage_tensors(layer, stage):
    i = layer // 2
    if stage == "mlp":
        return [("mlp_w1", layer), ("mlp_w2", layer)]
    if stage == "mixer_in":
        return [("ab_w_in", i)] if layer % 2 == 0 else [("c_w_qkv", i)]
    return [("b_conv_w", i), ("ab_w_out", i)] if layer % 2 == 0 else [("c_w_out", i)]


def _local_step(x, target, p, weights_of, grads_done):
    S, D = x.shape
    depth = p["mix_norm_g"].shape[0]
    n_even = (depth + 1) // 2
    mix_g3 = p["mix_norm_g"].reshape(depth, 1, D)
    mlp_g3 = p["mlp_norm_g"].reshape(depth, 1, D)
    vec3 = lambda t: t.reshape(t.shape[0], 1, t.shape[1])
    spw16 = p["a_spatial_w"].astype(BF16)
    spw16_t = jnp.swapaxes(spw16, 2, 3)
    bias_full = jnp.repeat(jnp.swapaxes(p["a_spatial_b"], 1, 2), HEAD_DIM, axis=2)
    vn_g, vn_b, cn_g, cn_b, cb3 = (vec3(p[k]) for k in ("a_vnorm_g", "a_vnorm_b", "b_norm_g", "b_norm_b", "b_conv_b"))
    tables = _rope_tables(S)
    gq = jnp.tile(p["c_q_norm_g"], (1, 2))
    gk = jnp.tile(p["c_k_norm_g"], (1, 2))

    saved = []
    h = _rms_fwd("mix_norm_0", x, mix_g3, 0)
    for layer in range(depth):
        i = layer // 2
        wl = dict(weights_of(layer, "mixer_in", x))
        rec = {"x_mix": x, "w": wl, "h_mix": h}
        if layer % 2 == 0:
            (z,) = _mm_ngroup(f"ab_in_{layer}", h, wl["ab_w_in"], nt=False, tm=MM_TM_K, out_dtypes=[F32])
            wl.update(weights_of(layer, "mixer_out", z))
            gconv = _glu_conv_fwd(f"glu_conv_{layer}", z, wl["b_conv_w"], cb3, i)
            x, h, cat = _ab_tail_out_proj(f"ab_out_{layer}", z, gconv, spw16, bias_full, vn_g, vn_b, cn_g, cn_b, i,
                                          _rows_merged(wl["ab_w_out"]), x, mlp_g3, layer)
            rec.update(z=z, gconv=gconv, cat=cat)
        else:
            (qkv,) = _mm_ngroup(f"c_qkv_{layer}", h, wl["c_w_qkv"], nt=False, tm=MM_TM_K, out_dtypes=[F32])
            wl.update(weights_of(layer, "mixer_out", qkv))
            qn, kn = _qk_fwd(f"qk_norm_rope_{layer}", qkv, gq[i:i + 1], gk[i:i + 1], tables)
            os, lses = zip(*[_attn_fwd(f"attn_d{d}_{layer}", qn, kn, qkv, V_COL, d) for d in PATTERN_DILATIONS])
            x, h, o, lse = _merge_out_proj(f"c_out_{layer}", os, lses, _rows_merged(wl["c_w_out"]), x, mlp_g3, layer)
            rec.update(qkv=qkv, qn=qn, kn=kn, o=o, lse=lse)
        rec["x_mlp"] = x
        wl.update(weights_of(layer, "mlp", x))
        a, hsq = _mm_ngroup(f"mlp_up_{layer}", h, wl["mlp_w1"], nt=False, tm=MM_TM_K, out_dtypes=[BF16, BF16],
                            epilogue=_sq_relu_epilogue)
        rec.update(h_mlp=h, a=a, hsq=hsq)
        if layer + 1 < depth:
            x, h = _mm_kgroup(f"mlp_down_{layer}", hsq, _rows_merged(wl["mlp_w2"]), nt=False, tm=MM_TM_K,
                              out_dtypes=[F32, BF16], extras=(x,), vecs=[(mix_g3, layer + 1)], epilogue=_add_norm_epilogue)
        else:
            dx, loss_row = _mm_kgroup(f"mlp_down_{layer}", hsq, _rows_merged(wl["mlp_w2"]), nt=False, tm=MM_TM_K,
                                      out_dtypes=[F32], extras=(x, target), n_sums=1, epilogue=_add_loss_epilogue)
        saved.append(rec)

    small = {k: [None] * v.shape[0] for k, v in p.items()}
    token = None
    for layer in reversed(range(depth)):
        i = layer // 2
        rec = saved[layer]
        wl = rec["w"]
        g = {}
        (da,) = _mm_ngroup(f"mlp_down_dgrad_{layer}", dx, wl["mlp_w2"], nt=True, tm=MM_TM_K, out_dtypes=[BF16],
                           extras=(rec["a"],), epilogue=_sq_relu_grad_epilogue, anchor=token)
        g["mlp_w2"] = _wgrad(f"mlp_down_wgrad_{layer}", rec["hsq"], dx, wl["mlp_w2"].shape, a_group=True, tm=WGRAD_TM)
        g["mlp_w1"] = _wgrad(f"mlp_up_wgrad_{layer}", rec["h_mlp"], da, wl["mlp_w1"].shape, a_group=False, tm=WGRAD_TM)
        dx, small["mlp_norm_g"][layer] = _mm_kgroup(
            f"mlp_up_dgrad_{layer}", da, wl["mlp_w1"], nt=True, tm=MM_TM_K, out_dtypes=[F32], extras=(rec["x_mlp"], dx),
            vecs=[(mlp_g3, layer)], n_sums=1, epilogue=_norm_bwd_epilogue)
        token = grads_done(layer, "mlp", g)
        g = {}
        if layer % 2 == 0:
            w_out = _rows_merged(wl["ab_w_out"])
            (dcat,) = _mm_ngroup(f"ab_out_dgrad_{layer}", dx, w_out, nt=True, tm=MM_TM_K, out_dtypes=[F32], anchor=token)
            g["ab_w_out"] = _wgrad(f"ab_out_wgrad_{layer}", rec["cat"], dx, w_out.shape, a_group=True,
                                   tm=WGRAD_TM).reshape(wl["ab_w_out"].shape)
            dz, dgconv, dspw, dbias, dvg, dvb, dcg, dcb = _ab_tail_bwd(
                f"ab_tail_bwd_{layer}", rec["z"], rec["gconv"], dcat, spw16, spw16_t, bias_full, vn_g, vn_b, cn_g, cn_b, i)
            dz, g["b_conv_w"], dcbias = _glu_conv_bwd(f"glu_conv_bwd_{layer}", rec["z"], dgconv, dz, wl["b_conv_w"])
            token = grads_done(layer, "mixer_out", g)
            g = {}
            small["a_spatial_w"][i] = dspw
            small["a_spatial_b"][i] = _fold_bias(dbias)[:, :A_GROUPS].T
            for k, val in (("a_vnorm_g", dvg), ("a_vnorm_b", dvb), ("b_norm_g", dcg), ("b_norm_b", dcb), ("b_conv_b", dcbias)):
                small[k][i] = val
            g["ab_w_in"] = _wgrad(f"ab_in_wgrad_{layer}", rec["h_mix"], dz, wl["ab_w_in"].shape, a_group=False, tm=WGRAD_TM,
                                  anchor=token)
            dgrad = (f"ab_in_dgrad_{layer}", dz, wl["ab_w_in"])
        else:
            w_out = _rows_merged(wl["c_w_out"])
            do, delta = _mm_ngroup(f"c_out_dgrad_{layer}", dx, w_out, nt=True, tm=MM_TM_K, out_dtypes=[F32, F32],
                                   extras=(rec["o"],), epilogue=_delta_epilogue, anchor=token)
            g["c_w_out"] = _wgrad(f"c_out_wgrad_{layer}", rec["o"], dx, w_out.shape, a_group=True,
                                  tm=WGRAD_TM).reshape(wl["c_w_out"].shape)
            token = grads_done(layer, "mixer_out", g)
            g = {}
            attn_args = (rec["qn"], rec["kn"], rec["qkv"], V_COL, do, rec["lse"], delta)
            dqs = [_attn_bwd_q(f"attn_bwd_q_d{d}_{layer}", *attn_args, d) for d in PATTERN_DILATIONS]
            dks, dvs = zip(*[_attn_bwd_kv(f"attn_bwd_kv_d{d}_{layer}", *attn_args, d) for d in PATTERN_DILATIONS])
            dqkv, dgq, dgk = _qk_bwd(f"qk_norm_rope_bwd_{layer}", rec["qkv"], gq[i:i + 1], gk[i:i + 1], tables, dqs, dks, dvs)
            small["c_q_norm_g"][i] = dgq[:, :HEAD_DIM]
            small["c_k_norm_g"][i] = dgk[:, :HEAD_DIM]
            g["c_w_qkv"] = _wgrad(f"c_qkv_wgrad_{layer}", rec["h_mix"], dqkv, wl["c_w_qkv"].shape, a_group=False, tm=WGRAD_TM,
                                  anchor=token)
            dgrad = (f"c_qkv_dgrad_{layer}", dqkv, wl["c_w_qkv"])
        token = grads_done(layer, "mixer_in", g)
        dx, small["mix_norm_g"][layer] = _mm_kgroup(
            *dgrad, nt=True, tm=MM_TM_K, out_dtypes=[F32], extras=(rec["x_mix"], dx), vecs=[(mix_g3, layer)], n_sums=1,
            epilogue=_norm_bwd_epilogue, anchor=token)

    small = {k: jnp.stack([t.reshape(p[k].shape[1:]) for t in v]) for k, v in small.items()}
    return loss_row, dx, small


SHARDED = ("mlp_w1", "mlp_w2", "ab_w_in", "b_conv_w", "ab_w_out", "c_w_qkv", "c_w_out")
SMALL = ("mix_norm_g", "mlp_norm_g", "a_spatial_w", "a_spatial_b", "a_vnorm_g", "a_vnorm_b", "b_conv_b", "b_norm_g",
         "b_norm_b", "c_q_norm_g", "c_k_norm_g")
WEIGHTS = ("mix_norm_g", "mlp_norm_g", "mlp_w1", "mlp_w2", "ab_w_in", "a_spatial_w", "a_spatial_b", "a_vnorm_g",
           "a_vnorm_b", "b_conv_w", "b_conv_b", "b_norm_g", "b_norm_b", "ab_w_out", "c_w_qkv", "c_q_norm_g",
           "c_k_norm_g", "c_w_out")


def _pack(parts):
    flat = jnp.concatenate([parts[k].reshape(-1) for k in SMALL])
    rows = -(-flat.shape[0] // (256 * LANES)) * 256
    return jnp.pad(flat, (0, rows * LANES - flat.shape[0])).reshape(rows, LANES)


def _unpack(packed, like):
    flat = packed.reshape(-1)
    out, off = {}, 0
    for k in SMALL:
        n = like[k].size
        out[k] = flat[off:off + n].reshape(like[k].shape)
        off += n
    return out


def kernel(x, mix_norm_g, mlp_norm_g, mlp_w1, mlp_w2, ab_w_in, a_spatial_w, a_spatial_b, a_vnorm_g, a_vnorm_b, b_conv_w, b_conv_b, b_norm_g, b_norm_b, ab_w_out, c_w_qkv, c_q_norm_g, c_k_norm_g, c_w_out, loss_target, m_mix_norm_g, m_mlp_norm_g, m_mlp_w1, m_mlp_w2, m_ab_w_in, m_a_spatial_w, m_a_spatial_b, m_a_vnorm_g, m_a_vnorm_b, m_b_conv_w, m_b_conv_b, m_b_norm_g, m_b_norm_b, m_ab_w_out, m_c_w_qkv, m_c_q_norm_g, m_c_k_norm_g, m_c_w_out, v_mix_norm_g, v_mlp_norm_g, v_mlp_w1, v_mlp_w2, v_ab_w_in, v_a_spatial_w, v_a_spatial_b, v_a_vnorm_g, v_a_vnorm_b, v_b_conv_w, v_b_conv_b, v_b_norm_g, v_b_norm_b, v_ab_w_out, v_c_w_qkv, v_c_q_norm_g, v_c_k_norm_g, v_c_w_out):
    w = dict(mix_norm_g=mix_norm_g, mlp_norm_g=mlp_norm_g, mlp_w1=mlp_w1, mlp_w2=mlp_w2, ab_w_in=ab_w_in,
             a_spatial_w=a_spatial_w, a_spatial_b=a_spatial_b, a_vnorm_g=a_vnorm_g, a_vnorm_b=a_vnorm_b,
             b_conv_w=b_conv_w, b_conv_b=b_conv_b, b_norm_g=b_norm_g, b_norm_b=b_norm_b, ab_w_out=ab_w_out,
             c_w_qkv=c_w_qkv, c_q_norm_g=c_q_norm_g, c_k_norm_g=c_k_norm_g, c_w_out=c_w_out)
    m = dict(mix_norm_g=m_mix_norm_g, mlp_norm_g=m_mlp_norm_g, mlp_w1=m_mlp_w1, mlp_w2=m_mlp_w2, ab_w_in=m_ab_w_in,
             a_spatial_w=m_a_spatial_w, a_spatial_b=m_a_spatial_b, a_vnorm_g=m_a_vnorm_g, a_vnorm_b=m_a_vnorm_b,
             b_conv_w=m_b_conv_w, b_conv_b=m_b_conv_b, b_norm_g=m_b_norm_g, b_norm_b=m_b_norm_b, ab_w_out=m_ab_w_out,
             c_w_qkv=m_c_w_qkv, c_q_norm_g=m_c_q_norm_g, c_k_norm_g=m_c_k_norm_g, c_w_out=m_c_w_out)
    v = dict(mix_norm_g=v_mix_norm_g, mlp_norm_g=v_mlp_norm_g, mlp_w1=v_mlp_w1, mlp_w2=v_mlp_w2, ab_w_in=v_ab_w_in,
             a_spatial_w=v_a_spatial_w, a_spatial_b=v_a_spatial_b, a_vnorm_g=v_a_vnorm_g, a_vnorm_b=v_a_vnorm_b,
             b_conv_w=v_b_conv_w, b_conv_b=v_b_conv_b, b_norm_g=v_b_norm_g, b_norm_b=v_b_norm_b, ab_w_out=v_ab_w_out,
             c_w_qkv=v_c_w_qkv, c_q_norm_g=v_c_q_norm_g, c_k_norm_g=v_c_k_norm_g, c_w_out=v_c_w_out)

    S, D = x.shape[1], x.shape[2]
    depth = mix_norm_g.shape[0]
    mine = (2 * lax.axis_index("x") + lax.axis_index("y")).astype(jnp.int32).reshape(1)

    stages = [(layer, stage) for layer in range(depth) for stage in STAGES]

    def start_gather(name, some_stages, anchor):
        groups = [[_prepare_shard(f"prepare_{k}_{i}", w[k], i, F32 if k == "b_conv_w" else BF16, mine, anchor)
                   for k, i in _stage_tensors(*st)] for st in some_stages]
        return _exchange_start(name, "gather", groups)

    first, rest = stages[:len(STAGES)], stages[len(STAGES):]
    handles_first, token_first = start_gather("gather_weights_start_first", first, None)
    handles_rest, gather_token = start_gather("gather_weights_start_rest", rest, token_first)
    handles = dict(zip(first + rest, handles_first + handles_rest))

    def weights_of(layer, stage, after):
        _, got = _exchange_wait(f"gather_weights_wait_{layer}_{stage}", "gather", handles[layer, stage],
                                gather_token if (layer, stage) == stages[0] else after)
        return {k: a for (k, _), a in zip(_stage_tensors(layer, stage), got)}

    scattered = {}

    def grads_done(layer, stage, g):
        names = [k for k, _ in _stage_tensors(layer, stage)]
        group = [(g[k], lax.empty((N_PEERS["scatter"],) + g[k].shape[1:], BF16)) for k in names]
        (scattered[layer, stage],), token = _exchange_start(f"scatter_grads_start_{layer}_{stage}", "scatter", [group])
        return token

    small_params = {k: w[k] for k in SMALL}
    loss_row, dx, small_grads = _local_step(x.reshape(S, D), loss_target.reshape(S, D), small_params, weights_of, grads_done)

    loss = lax.psum(loss_row[0, 0], ("x", "y", "c"))

    packed = _pack(small_grads)
    (small_handle,), small_token = _exchange_start(
        "allgather_small_grads_start", "allgather", [[(packed, lax.empty((N_PEERS["allgather"],) + packed.shape, F32))]])

    order = _sum_order()
    stacked = {k: [lax.empty(w[k].shape, F32) for _ in range(4)] for k in SHARDED}
    for layer, stage in reversed(stages):
        gbs, recvs = _exchange_wait(f"scatter_grads_wait_{layer}_{stage}", "scatter", scattered[layer, stage], small_token)
        for (k, i), gb, recv in zip(_stage_tensors(layer, stage), gbs, recvs):
            stacked[k] = _adamw_layer(f"adamw_{k}_{i}", w[k], m[k], v[k], i, gb, recv, order, stacked[k])
    grads, deltas, new_m, new_v = ({k: stacked[k][j] for k in SHARDED} for j in range(4))

    last_updated = stacked[_stage_tensors(*stages[0])[-1][0]][0]
    (packed,), (recv,) = _exchange_wait("allgather_small_grads_wait", "allgather", small_handle, last_updated)
    g_small = _ordered_sum("sum_small_grads", packed, recv, order)
    outs = _adamw("adamw_small", _pack(small_params), _pack({k: m[k] for k in SMALL}), _pack({k: v[k] for k in SMALL}), g_small)
    for d_, packed in zip((grads, deltas, new_m, new_v), outs):
        d_.update(_unpack(packed, small_params))

    return (loss, dx.reshape(1, S, D), *[grads[k] for k in WEIGHTS], *[deltas[k] for k in WEIGHTS],
            *[new_m[k] for k in WEIGHTS], *[new_v[k] for k in WEIGHTS])
```

```python
import jax
import jax.numpy as jnp
from jax import lax
from jax.experimental import pallas as pl
from jax.experimental.pallas import tpu as pltpu

F32, BF16 = jnp.float32, jnp.bfloat16
MESH = pl.DeviceIdType.MESH
ANY = pl.BlockSpec(memory_space=pl.ANY)

VMEM_LIMIT_BYTES = 56 * 1024 * 1024
LANES = 128
ELEMENTWISE_ROWS = 256

EPS = 1e-6
NEG = -1e30
HEAD_DIM = 64
N_HEADS = 16
CHUNK = 128
A_GROUPS = 8
CONV_WIDTH = 31
CONV_HALO = 16
CONV_CHUNK = 64
BAND = 64
PATTERN_DILATIONS = (1, 4, 16)
ROT_DIM = 16
ROPE_THETA = 500000.0
N_SHARDS = 4

ADAM_LR, ADAM_B1, ADAM_B2, ADAM_EPS, ADAM_WD, ADAM_STEP = 0.001, 0.9, 0.999, 1e-08, 0.01, 10


def _cp(*sem):
    return pltpu.CompilerParams(dimension_semantics=sem, vmem_limit_bytes=VMEM_LIMIT_BYTES)


def _tile(n, pref):
    t = min(n, pref)
    assert n % t == 0, (n, pref)
    return t


def _dot(a, b, ca, cb):
    return lax.dot_general(a, b, (((ca,), (cb,)), ((), ())), preferred_element_type=F32)


def _mm_ngroup(name, a, w, *, nt, tm, out_dtypes, extras=(), epilogue=None, anchor=None):
    M, K = a.shape
    G, R, C = w.shape
    nw = R if nt else C
    assert K == (C if nt else R)
    tm = _tile(M, tm)
    n_ex = len(extras)
    anchors = [] if anchor is None else [anchor]

    def body(a_ref, w_ref, *rest):
        rest = rest[len(anchors):]
        av = a_ref[...].astype(BF16)
        for g in range(G):
            cols = slice(g * nw, (g + 1) * nw)
            acc = _dot(av, w_ref[g], 1, 1 if nt else 0)
            res = epilogue(acc, *[e[:, cols] for e in rest[:n_ex]]) if epilogue else (acc,)
            for o_ref, r in zip(rest[n_ex:], res):
                o_ref[:, cols] = r.astype(o_ref.dtype)

    blk = pl.BlockSpec((tm, G * nw), lambda m: (m, 0))
    return pl.pallas_call(
        body, name=name, grid=(M // tm,),
        in_specs=[pl.BlockSpec((tm, K), lambda m: (m, 0)), pl.BlockSpec((G, R, C), lambda m: (0, 0, 0))]
        + [pl.BlockSpec((8, LANES), lambda m: (0, 0))] * len(anchors) + [blk] * n_ex,
        out_specs=[blk] * len(out_dtypes),
        out_shape=[jax.ShapeDtypeStruct((M, G * nw), dt) for dt in out_dtypes],
        compiler_params=_cp("parallel"),
    )(a, w, *anchors, *extras)


def _mm_kgroup(name, a, w, *, nt, tm, out_dtypes, extras=(), vecs=(), n_sums=0, epilogue=None, anchor=None):
    G, R, C = w.shape
    kw, N = (C, R) if nt else (R, C)
    if a.ndim == 3:
        M = a.shape[1]
        assert a.shape[0] == G and a.shape[2] == kw
    else:
        M = a.shape[0]
        assert a.shape[1] == G * kw
    tm = _tile(M, tm)
    n_ex = len(extras)
    a_spec = (pl.BlockSpec((G, tm, kw), lambda m: (0, m, 0)) if a.ndim == 3 else pl.BlockSpec((tm, G * kw), lambda m: (m, 0)))
    anchors = [] if anchor is None else [anchor]

    def body(a_ref, w_ref, *rest):
        rest = rest[len(anchors):]
        acc = None
        for g in range(G):
            a_g = a_ref[g] if a.ndim == 3 else a_ref[:, g * kw:(g + 1) * kw]
            part = _dot(a_g.astype(BF16), w_ref[g], 1, 1 if nt else 0)
            acc = part if acc is None else acc + part
        n_in = n_ex + len(vecs)
        res = epilogue(acc, *[e[...] for e in rest[:n_in]]) if epilogue else (acc,)
        outs = rest[n_in:]
        n_tiles = len(outs) - n_sums
        for o_ref, r in zip(outs[:n_tiles], res[:n_tiles]):
            o_ref[...] = r.astype(o_ref.dtype)
        if n_sums:
            @pl.when(pl.program_id(0) == 0)
            def _():
                for s_ref in outs[n_tiles:]:
                    s_ref[...] = jnp.zeros_like(s_ref)

            for s_ref, r in zip(outs[n_tiles:], res[n_tiles:]):
                s_ref[...] += r

    blk = pl.BlockSpec((tm, N), lambda m: (m, 0))
    row = pl.BlockSpec((1, N), lambda m: (0, 0))
    return pl.pallas_call(
        body, name=name, grid=(M // tm,),
        in_specs=[a_spec, pl.BlockSpec((G, R, C), lambda m: (0, 0, 0))]
        + [pl.BlockSpec((8, LANES), lambda m: (0, 0))] * len(anchors) + [blk] * n_ex
        + [pl.BlockSpec((None, 1, N), lambda m, i=i: (i, 0, 0)) for _, i in vecs],
        out_specs=[blk] * len(out_dtypes) + [row] * n_sums,
        out_shape=[jax.ShapeDtypeStruct((M, N), dt) for dt in out_dtypes] + [jax.ShapeDtypeStruct((1, N), F32)] * n_sums,
        compiler_params=_cp("arbitrary" if n_sums else "parallel"),
    )(a, w, *anchors, *extras, *[v for v, _ in vecs])


def _wgrad(name, a, b, shape, *, a_group, tm, anchor=None):
    G, R, C = shape
    M = a.shape[0]
    tm = _tile(M, tm)
    n_m = M // tm
    anchors = [] if anchor is None else [anchor]

    def body(a_ref, b_ref, *rest):
        gb_ref, gf_ref = rest[len(anchors):]
        m = pl.program_id(1)
        part = _dot(a_ref[...].astype(BF16), b_ref[...].astype(BF16), 0, 0)

        @pl.when(m == 0)
        def _():
            gf_ref[...] = part

        @pl.when(m > 0)
        def _():
            gf_ref[...] += part

        @pl.when(m == n_m - 1)
        def _():
            gb_ref[...] = gf_ref[...].astype(BF16)

    a_spec = pl.BlockSpec((tm, R), (lambda g, m: (m, g)) if a_group else (lambda g, m: (m, 0)))
    if b.ndim == 3:
        assert not a_group
        b_spec = pl.BlockSpec((None, tm, C), lambda g, m: (g, m, 0))
    else:
        b_spec = pl.BlockSpec((tm, C), (lambda g, m: (m, 0)) if a_group else (lambda g, m: (m, g)))
    o_spec = pl.BlockSpec((None, R, C), lambda g, m: (g, 0, 0))
    return pl.pallas_call(
        body, name=name, grid=(G, n_m),
        in_specs=[a_spec, b_spec] + [pl.BlockSpec((8, LANES), lambda g, m: (0, 0))] * len(anchors), out_specs=o_spec,
        out_shape=jax.ShapeDtypeStruct(shape, BF16), scratch_shapes=[pltpu.VMEM((R, C), F32)],
        compiler_params=_cp("parallel", "arbitrary"),
    )(a, b, *anchors)


def _rms_fwd(name, x, g3, layer):
    S, D = x.shape
    tm = _tile(S, 512)

    def body(x_ref, g_ref, h_ref):
        xv = x_ref[...]
        r = lax.rsqrt(jnp.mean(xv * xv, axis=-1, keepdims=True) + EPS)
        h_ref[...] = (xv * r * g_ref[...]).astype(BF16)

    row = pl.BlockSpec((tm, D), lambda m: (m, 0))
    return pl.pallas_call(
        body, name=name, grid=(S // tm,),
        in_specs=[row, pl.BlockSpec((None, 1, D), lambda m: (layer, 0, 0))], out_specs=row,
        out_shape=jax.ShapeDtypeStruct((S, D), BF16), compiler_params=_cp("parallel"),
    )(x, g3)


def _adamw_math(w, m, v, g):
    m2 = ADAM_B1 * m + (1.0 - ADAM_B1) * g
    v2 = ADAM_B2 * v + (1.0 - ADAM_B2) * jnp.square(g)
    m_hat = m2 / (1.0 - ADAM_B1 ** ADAM_STEP)
    v_hat = v2 / (1.0 - ADAM_B2 ** ADAM_STEP)
    return g, -ADAM_LR * (m_hat / (jnp.sqrt(v_hat) + ADAM_EPS) + ADAM_WD * w), m2, v2


def _row_tile(rows):
    return _tile(rows, ELEMENTWISE_ROWS) if rows % ELEMENTWISE_ROWS == 0 else rows


def _adamw(name, w, m, v, g):
    rows, C = w.shape
    tr = _row_tile(rows)

    def body(w_ref, m_ref, v_ref, g_in, g_ref, d_ref, nm_ref, nv_ref):
        for o_ref, val in zip((g_ref, d_ref, nm_ref, nv_ref), _adamw_math(w_ref[...], m_ref[...], v_ref[...], g_in[...])):
            o_ref[...] = val

    blk = pl.BlockSpec((tr, C), lambda i: (i, 0))
    return pl.pallas_call(
        body, name=name, grid=(rows // tr,), in_specs=[blk] * 4, out_specs=[blk] * 4,
        out_shape=[jax.ShapeDtypeStruct((rows, C), F32)] * 4, compiler_params=_cp("parallel"),
    )(w, m, v, g)


def _sum_order():
    x, y, c = lax.axis_index("x"), lax.axis_index("y"), lax.axis_index("c")
    differs = lambda bit, coord: bit + coord - 2 * bit * coord
    slots = [4 * differs(p >> 2 & 1, x) + 2 * differs(p >> 1 & 1, y) + differs(p & 1, c) - 1 for p in range(8)]
    return [jnp.asarray(s, jnp.int32).reshape(1) for s in [2 * x + y, 4 * x + 2 * y + c] + slots]


def _adamw_layer(name, w, m, v, layer, gb, recv, order, outs):
    _, R, C = w.shape
    tr = _row_tile(R)
    n_s = len(order)

    def body(*refs):
        me = refs[1][0]
        w_ref, m_ref, v_ref, own_ref = refs[n_s:n_s + 4]
        theirs, outs_ = refs[n_s + 4:n_s + 12], refs[n_s + 16:]
        g = None
        for p in range(8):
            term = jnp.where(me == p, own_ref[...], theirs[p][...]).astype(F32)
            g = term if g is None else g + term
        for o_ref, val in zip(outs_, _adamw_math(w_ref[...], m_ref[...], v_ref[...], g)):
            o_ref[...] = val

    st = pl.BlockSpec((None, tr, C), lambda i, *s: (layer, i, 0))
    slot = lambda p: pl.BlockSpec((None, tr, C), lambda i, *s: (jnp.maximum(s[2 + p][0], 0), i, 0))
    return pl.pallas_call(
        body, name=name,
        grid_spec=pltpu.PrefetchScalarGridSpec(
            num_scalar_prefetch=n_s, grid=(R // tr,),
            in_specs=[st] * 3 + [pl.BlockSpec((None, tr, C), lambda i, *s: (s[0][0], i, 0))] + [slot(p) for p in range(8)]
            + [ANY] * 4,
            out_specs=[st] * 4),
        out_shape=[jax.ShapeDtypeStruct(w.shape, F32)] * 4, input_output_aliases={n_s + 12 + j: j for j in range(4)},
        compiler_params=_cp("parallel"),
    )(*order, w, m, v, gb, *([recv] * 8), *outs)


def _gelu(x):
    return x * (0.5 * (1.0 + jnp.tanh(0.7978845608028654 * (x + 0.044715 * (x * x * x)))))


def _layernorm(t, g, b):
    mu = jnp.mean(t, axis=-1, keepdims=True)
    var = jnp.mean(jnp.square(t - mu), axis=-1, keepdims=True)
    return (t - mu) * lax.rsqrt(var + EPS) * g + b


def _silu(x):
    return x * jax.nn.sigmoid(x)


def _a_value(zv, g, b):
    return _layernorm(_gelu(zv), g, b)


def _b_tail(gc, g, b):
    return _silu(_layernorm(gc, g, b))


def _first_head(shape):
    return lax.broadcasted_iota(jnp.int32, shape, len(shape) - 1) < HEAD_DIM


def _spatial_mix(spw_ref, vb, tm):
    first = _first_head((CHUNK, LANES))
    rows = []
    for n in range(tm // CHUNK):
        blocks = []
        for j in range(A_GROUPS // 2):
            vblk = vb[n * CHUNK:(n + 1) * CHUNK, j * LANES:(j + 1) * LANES]
            r0 = _dot(spw_ref[2 * j], vblk, 1, 0)
            r1 = _dot(spw_ref[2 * j + 1], vblk, 1, 0)
            blocks.append(jnp.where(first, r0, r1))
        rows.append(jnp.concatenate(blocks, axis=1))
    return jnp.concatenate(rows, axis=0) if len(rows) > 1 else rows[0]


def _ab_tail_out_proj(name, z, gconv, spw, bias_full, vn_g, vn_b, cn_g, cn_b, layer, w, x, g3, g_layer):
    S = z.shape[0]
    AW = 512
    tm = _tile(S, 256)

    def body(zu_ref, zv_ref, gc_ref, spw_ref, bias_ref, vg_ref, vb_ref, cg_ref, cb_ref, w_ref, x_ref, g_ref,
             xo_ref, h_ref, cat_ref):
        u = _gelu(zu_ref[...])
        v = _a_value(zv_ref[...], vg_ref[...], vb_ref[...])
        sv = _spatial_mix(spw_ref, v.astype(BF16), tm) + jnp.tile(bias_ref[...], (tm // CHUNK, 1))
        cat = jnp.concatenate([(u * sv).astype(BF16), _b_tail(gc_ref[...], cg_ref[...], cb_ref[...]).astype(BF16)], axis=1)
        cat_ref[...] = cat
        y, h = _add_norm_epilogue(_dot(cat, w_ref[0], 1, 0), x_ref[...], g_ref[...])
        xo_ref[...] = y
        h_ref[...] = h.astype(BF16)

    vec = pl.BlockSpec((None, 1, AW), lambda m: (layer, 0, 0))
    row = pl.BlockSpec((tm, 2 * AW), lambda m: (m, 0))
    return pl.pallas_call(
        body, name=name, grid=(S // tm,),
        in_specs=[pl.BlockSpec((tm, AW), lambda m: (m, 0)), pl.BlockSpec((tm, AW), lambda m: (m, 1)),
                  pl.BlockSpec((tm, AW), lambda m: (m, 0)),
                  pl.BlockSpec((None, A_GROUPS, CHUNK, CHUNK), lambda m: (layer, 0, 0, 0)),
                  pl.BlockSpec((None, CHUNK, AW), lambda m: (layer, 0, 0)), vec, vec, vec, vec,
                  pl.BlockSpec(w.shape, lambda m: (0, 0, 0)), row, pl.BlockSpec((None, 1, 2 * AW), lambda m: (g_layer, 0, 0))],
        out_specs=[row] * 3,
        out_shape=[jax.ShapeDtypeStruct((S, 2 * AW), dt) for dt in (F32, BF16, BF16)], compiler_params=_cp("parallel"),
    )(z, z, gconv, spw, bias_full, vn_g, vn_b, cn_g, cn_b, w, x, g3)


def _ab_tail_bwd(name, z, gconv, dcat, spw, spw_t, bias_full, vn_g, vn_b, cn_g, cn_b, layer):
    S = z.shape[0]
    AW = 512
    tm = _tile(S, 256)
    n_chunks = tm // CHUNK

    def body(zu_ref, zv_ref, gc_ref, dcat_ref, spw_ref, spwt_ref, bias_ref, vg_ref, vb_ref, cg_ref, cb_ref,
             dz_ref, dgc_ref, dspw_ref, dbias_ref, dvg_ref, dvb_ref, dcg_ref, dcb_ref):
        @pl.when(pl.program_id(0) == 0)
        def _():
            for r in (dspw_ref, dbias_ref, dvg_ref, dvb_ref, dcg_ref, dcb_ref):
                r[...] = jnp.zeros_like(r)

        dya = dcat_ref[:, :AW]
        dyb = dcat_ref[:, AW:]
        u, u_vjp = jax.vjp(_gelu, zu_ref[...])
        v, v_vjp = jax.vjp(_a_value, zv_ref[...], vg_ref[...], vb_ref[...])
        vb16 = v.astype(BF16)
        sv = _spatial_mix(spw_ref, vb16, tm) + jnp.tile(bias_ref[...], (n_chunks, 1))
        (dzu,) = u_vjp(dya * sv)
        dsv = dya * u
        dsv16 = dsv.astype(BF16)
        dv = _spatial_mix(spwt_ref, dsv16, tm)
        dzv, dvg, dvb = v_vjp(dv)
        dz_ref[0] = dzu
        dz_ref[1] = dzv
        dvg_ref[...] += dvg
        dvb_ref[...] += dvb

        first = _first_head((CHUNK, LANES))
        zero = jnp.zeros((), BF16)
        dbias = jnp.zeros((CHUNK, AW), F32)
        for n in range(n_chunks):
            rows = slice(n * CHUNK, (n + 1) * CHUNK)
            dbias = dbias + dsv[rows]
            for j in range(A_GROUPS // 2):
                cols = slice(j * LANES, (j + 1) * LANES)
                dblk, vblk = dsv16[rows, cols], vb16[rows, cols]
                dspw_ref[2 * j] += _dot(jnp.where(first, dblk, zero), vblk, 1, 1)
                dspw_ref[2 * j + 1] += _dot(jnp.where(first, zero, dblk), vblk, 1, 1)
        dbias_ref[...] += dbias

        _, t_vjp = jax.vjp(_b_tail, gc_ref[...], cg_ref[...], cb_ref[...])
        dgc, dcg, dcb = t_vjp(dyb)
        dgc_ref[...] = dgc
        dcg_ref[...] += dcg
        dcb_ref[...] += dcb

    vec = pl.BlockSpec((None, 1, AW), lambda m: (layer, 0, 0))
    spw_spec = pl.BlockSpec((None, A_GROUPS, CHUNK, CHUNK), lambda m: (layer, 0, 0, 0))
    ovec = pl.BlockSpec((1, AW), lambda m: (0, 0))
    return pl.pallas_call(
        body, name=name, grid=(S // tm,),
        in_specs=[pl.BlockSpec((tm, AW), lambda m: (m, 0)), pl.BlockSpec((tm, AW), lambda m: (m, 1)),
                  pl.BlockSpec((tm, AW), lambda m: (m, 0)), pl.BlockSpec((tm, 2 * AW), lambda m: (m, 0)),
                  spw_spec, spw_spec, pl.BlockSpec((None, CHUNK, AW), lambda m: (layer, 0, 0)), vec, vec, vec, vec],
        out_specs=[pl.BlockSpec((2, tm, AW), lambda m: (0, m, 0)), pl.BlockSpec((tm, AW), lambda m: (m, 0)),
                   pl.BlockSpec((A_GROUPS, CHUNK, CHUNK), lambda m: (0, 0, 0)),
                   pl.BlockSpec((CHUNK, AW), lambda m: (0, 0)), ovec, ovec, ovec, ovec],
        out_shape=[jax.ShapeDtypeStruct((4, S, AW), F32), jax.ShapeDtypeStruct((S, AW), F32),
                   jax.ShapeDtypeStruct((A_GROUPS, CHUNK, CHUNK), F32), jax.ShapeDtypeStruct((CHUNK, AW), F32)]
                  + [jax.ShapeDtypeStruct((1, AW), F32)] * 4,
        compiler_params=_cp("arbitrary"),
    )(z, z, gconv, dcat, spw, spw_t, bias_full, vn_g, vn_b, cn_g, cn_b)


def _fold_bias(dbias_full):
    def body(d_ref, o_ref):
        d = d_ref[...]
        hi = d.astype(BF16)
        lo = (d - hi.astype(F32)).astype(BF16)
        r = lax.broadcasted_iota(jnp.int32, (512, LANES), 0)
        c = lax.broadcasted_iota(jnp.int32, (512, LANES), 1)
        fold = jnp.where(lax.shift_right_logical(r, 6) == c, 1.0, 0.0).astype(BF16)
        o_ref[...] = _dot(hi, fold, 1, 0) + _dot(lo, fold, 1, 0)

    return pl.pallas_call(body, name="fold_spatial_bias", out_shape=jax.ShapeDtypeStruct((CHUNK, LANES), F32))(dbias_full)


def _halo_specs(tm, n_halo_blocks, col):
    r = tm // CONV_HALO
    prev = pl.BlockSpec((CONV_HALO, LANES), lambda j, i: (jnp.maximum(i * r - 1, 0), col + j))
    cur = pl.BlockSpec((tm, LANES), lambda j, i: (i, col + j))
    nxt = pl.BlockSpec((CONV_HALO, LANES), lambda j, i: (jnp.minimum((i + 1) * r, n_halo_blocks - 1), col + j))
    return [prev, cur, nxt]


def _fill_halo(scr, prev, cur, nxt, tm, i, n_i):
    scr[0:CONV_HALO, :] = jnp.where(i > 0, prev, 0.0)
    scr[CONV_HALO:CONV_HALO + tm, :] = cur
    scr[CONV_HALO + tm:2 * CONV_HALO + tm, :] = jnp.where(i < n_i - 1, nxt, 0.0)


def _glu_conv_fwd(name, z, cw, cb3, layer):
    S = z.shape[0]
    tm = _tile(S, 512)
    n_i = S // tm
    pad = CONV_WIDTH // 2

    def body(vp, vc, vn, gp, gc, gn, w_ref, b_ref, out_ref, scr):
        i = pl.program_id(1)
        glu = lambda a, b: a[...] * jax.nn.sigmoid(b[...])
        _fill_halo(scr, glu(vp, gp), glu(vc, gc), glu(vn, gn), tm, i, n_i)
        taps = [w_ref[j:j + 1, :] for j in range(CONV_WIDTH)]
        for c0 in range(0, tm, CONV_CHUNK):
            acc = jnp.zeros((CONV_CHUNK, LANES), F32) + b_ref[...]
            for j in range(CONV_WIDTH):
                acc = acc + taps[j] * scr[pl.ds(c0 + CONV_HALO - pad + j, CONV_CHUNK), :]
            out_ref[pl.ds(c0, CONV_CHUNK), :] = acc

    return pl.pallas_call(
        body, name=name, grid=(4, n_i),
        in_specs=_halo_specs(tm, S // CONV_HALO, 8) + _halo_specs(tm, S // CONV_HALO, 12)
        + [pl.BlockSpec((None, CONV_WIDTH, LANES), lambda j, i: (j, 0, 0)),
           pl.BlockSpec((None, 1, LANES), lambda j, i: (layer, 0, j))],
        out_specs=pl.BlockSpec((tm, LANES), lambda j, i: (i, j)),
        out_shape=jax.ShapeDtypeStruct((S, 4 * LANES), F32),
        scratch_shapes=[pltpu.VMEM((tm + 2 * CONV_HALO, LANES), F32)],
        compiler_params=_cp("parallel", "parallel"),
    )(z, z, z, z, z, z, cw, cb3)


def _glu_conv_bwd(name, z, dgconv, dz, cw):
    S = z.shape[0]
    tm = _tile(S, 512)
    n_i = S // tm
    pad = CONV_WIDTH // 2

    def body(vp, vc, vn, gp, gc, gn, dp, dc, dn, w_ref, dz_in, dz_ref, gb_ref, db_ref, g_scr, d_scr, gf_ref):
        i = pl.program_id(1)
        sig = jax.nn.sigmoid(gc[...])
        _fill_halo(g_scr, vp[...] * jax.nn.sigmoid(gp[...]), vc[...] * sig, vn[...] * jax.nn.sigmoid(gn[...]), tm, i, n_i)
        _fill_halo(d_scr, dp[...], dc[...], dn[...], tm, i, n_i)

        @pl.when(i == 0)
        def _():
            gf_ref[...] = jnp.zeros_like(gf_ref)
            db_ref[...] = jnp.zeros_like(db_ref)

        taps = [w_ref[j:j + 1, :] for j in range(CONV_WIDTH)]
        dw = [jnp.zeros((8, LANES), F32) for _ in range(CONV_WIDTH)]
        db = jnp.zeros((8, LANES), F32)
        fold8 = lambda t: jnp.sum(t.reshape(CONV_CHUNK // 8, 8, LANES), axis=0)
        for c0 in range(0, tm, CONV_CHUNK):
            rows = pl.ds(c0, CONV_CHUNK)
            d_cur = dc[rows, :]
            dglu = jnp.zeros((CONV_CHUNK, LANES), F32)
            for j in range(CONV_WIDTH):
                dglu = dglu + taps[j] * d_scr[pl.ds(c0 + CONV_HALO + pad - j, CONV_CHUNK), :]
                dw[j] = dw[j] + fold8(d_cur * g_scr[pl.ds(c0 + CONV_HALO - pad + j, CONV_CHUNK), :])
            db = db + fold8(d_cur)
            sig_c = jax.nn.sigmoid(gc[rows, :])
            dz_ref[0, rows, :] = dglu * sig_c
            dz_ref[1, rows, :] = dglu * vc[rows, :] * sig_c * (1.0 - sig_c)
        for j in range(CONV_WIDTH):
            gf_ref[j:j + 1, :] += jnp.sum(dw[j], axis=0, keepdims=True)
        db_ref[...] += jnp.sum(db, axis=0, keepdims=True)

        @pl.when(i == n_i - 1)
        def _():
            gb_ref[...] = gf_ref[...].astype(BF16)

    w_spec = pl.BlockSpec((None, CONV_WIDTH, LANES), lambda j, i: (j, 0, 0))
    return pl.pallas_call(
        body, name=name, grid=(4, n_i),
        in_specs=_halo_specs(tm, S // CONV_HALO, 8) + _halo_specs(tm, S // CONV_HALO, 12)
        + _halo_specs(tm, S // CONV_HALO, 0) + [w_spec, ANY],
        out_specs=[pl.BlockSpec((2, tm, LANES), lambda j, i: (1, i, j)),
                   w_spec, pl.BlockSpec((1, LANES), lambda j, i: (0, j))],
        out_shape=[jax.ShapeDtypeStruct(dz.shape, F32), jax.ShapeDtypeStruct(cw.shape, BF16),
                   jax.ShapeDtypeStruct((1, 4 * LANES), F32)],
        input_output_aliases={10: 0},
        scratch_shapes=[pltpu.VMEM((tm + 2 * CONV_HALO, LANES), F32)] * 2 + [pltpu.VMEM((CONV_WIDTH, LANES), F32)],
        compiler_params=_cp("parallel", "arbitrary"),
    )(z, z, z, z, z, z, dgconv, dgconv, dgconv, cw, dz)


def _seg_matrix(scale):
    r = lax.broadcasted_iota(jnp.int32, (LANES, LANES), 0)
    c = lax.broadcasted_iota(jnp.int32, (LANES, LANES), 1)
    return jnp.where(lax.shift_right_logical(r, 6) == lax.shift_right_logical(c, 6), scale, 0.0).astype(BF16)


def _seg_sum(x, seg):
    hi = x.astype(BF16)
    lo = (x - hi.astype(F32)).astype(BF16)
    return _dot(hi, seg, 1, 0) + _dot(lo, seg, 1, 0)


def _rope_tables(S):
    pos = jnp.arange(S, dtype=F32)
    inv_freq = ROPE_THETA ** (-jnp.arange(0, ROT_DIM, 2, dtype=F32) / ROT_DIM)
    ang = pos[:, None] * inv_freq[None, :]
    cos, sin = jnp.cos(ang), jnp.sin(ang)
    half = ROT_DIM // 2
    rest = HEAD_DIM - ROT_DIM
    one, zero = jnp.ones((S, rest), F32), jnp.zeros((S, rest), F32)
    zh = jnp.zeros((S, half), F32)
    c = jnp.concatenate([cos, cos, one], axis=1)
    sa = jnp.concatenate([-sin, zh, zero], axis=1)
    sb = jnp.concatenate([zh, sin, zero], axis=1)
    return [jnp.tile(t, (1, 2)) for t in (c, sa, sb)]


QK_CHUNK = 64


def _qk_fwd(name, qkv, gq, gk, tables):
    S = qkv.shape[0]
    W = N_HEADS * HEAD_DIM
    tm = _tile(S, 256)
    half = ROT_DIM // 2

    def body(q_ref, k_ref, gq_ref, gk_ref, c_ref, sa_ref, sb_ref, qn_ref, kn_ref):
        seg = _seg_matrix(1.0 / HEAD_DIM)
        for r0 in range(0, tm, QK_CHUNK):
            rows = pl.ds(r0, QK_CHUNK)
            c, sa, sb = c_ref[rows, :], sa_ref[rows, :], sb_ref[rows, :]
            for t_ref, g_ref, o_ref in ((q_ref, gq_ref, qn_ref), (k_ref, gk_ref, kn_ref)):
                for blk in range(W // LANES):
                    cols = slice(blk * LANES, (blk + 1) * LANES)
                    t = t_ref[rows, cols]
                    y = t * lax.rsqrt(_seg_sum(t * t, seg) + EPS) * g_ref[...]
                    o_ref[rows, cols] = y * c + pltpu.roll(y, LANES - half, 1) * sa + pltpu.roll(y, half, 1) * sb

    row = lambda k: pl.BlockSpec((tm, W), lambda m: (m, k))
    gain = pl.BlockSpec((1, LANES), lambda m: (0, 0))
    tab = pl.BlockSpec((tm, LANES), lambda m: (m, 0))
    return pl.pallas_call(
        body, name=name, grid=(S // tm,),
        in_specs=[row(0), row(1), gain, gain, tab, tab, tab], out_specs=[row(0)] * 2,
        out_shape=[jax.ShapeDtypeStruct((S, W), F32)] * 2, compiler_params=_cp("parallel"),
    )(qkv, qkv, gq, gk, *tables)


def _qk_bwd(name, qkv, gq, gk, tables, dqs, dks, dvs):
    S = qkv.shape[0]
    W = N_HEADS * HEAD_DIM
    tm = _tile(S, 256)
    half = ROT_DIM // 2
    n_p = len(dqs)

    def body(q_ref, k_ref, gq_ref, gk_ref, c_ref, sa_ref, sb_ref, *rest):
        dq_refs, dk_refs, dv_refs = rest[:n_p], rest[n_p:2 * n_p], rest[2 * n_p:3 * n_p]
        dqkv_ref, dgq_ref, dgk_ref = rest[3 * n_p:]

        @pl.when(pl.program_id(0) == 0)
        def _():
            dgq_ref[...] = jnp.zeros_like(dgq_ref)
            dgk_ref[...] = jnp.zeros_like(dgk_ref)

        seg = _seg_matrix(1.0 / HEAD_DIM)
        r_i = lax.broadcasted_iota(jnp.int32, (LANES, LANES), 0)
        c_i = lax.broadcasted_iota(jnp.int32, (LANES, LANES), 1)
        same_dim = jnp.where((r_i & (HEAD_DIM - 1)) == (c_i & (HEAD_DIM - 1)), 1.0, 0.0).astype(BF16)
        dgs = [jnp.zeros((8, LANES), F32), jnp.zeros((8, LANES), F32)]
        fold8 = lambda t: jnp.sum(t.reshape(QK_CHUNK // 8, 8, LANES), axis=0)
        for r0 in range(0, tm, QK_CHUNK):
            rows = pl.ds(r0, QK_CHUNK)
            c, sa, sb = c_ref[rows, :], sa_ref[rows, :], sb_ref[rows, :]
            for idx, (t_ref, g_ref, d_refs) in enumerate(((q_ref, gq_ref, dq_refs), (k_ref, gk_ref, dk_refs))):
                for blk in range(W // LANES):
                    cols = slice(blk * LANES, (blk + 1) * LANES)
                    dout = d_refs[0][rows, cols]
                    for r in d_refs[1:]:
                        dout = dout + r[rows, cols]
                    dy = dout * c + pltpu.roll(dout * sa, half, 1) + pltpu.roll(dout * sb, LANES - half, 1)
                    t = t_ref[rows, cols]
                    r_ = lax.rsqrt(_seg_sum(t * t, seg) + EPS)
                    xhat = t * r_
                    dgs[idx] = dgs[idx] + fold8(dy * xhat)
                    dxhat = dy * g_ref[...]
                    dt = r_ * (dxhat - xhat * _seg_sum(dxhat * xhat, seg))
                    dqkv_ref[rows, idx * W + blk * LANES: idx * W + (blk + 1) * LANES] = dt.astype(BF16)
            dv = dv_refs[0][rows, :]
            for r in dv_refs[1:]:
                dv = dv + r[rows, :]
            dqkv_ref[rows, 2 * W:] = dv.astype(BF16)
        for dg, dg_ref in zip(dgs, (dgq_ref, dgk_ref)):
            dg_ref[...] += jnp.sum(_seg_sum(dg, same_dim), axis=0, keepdims=True)

    row = lambda k: pl.BlockSpec((tm, W), lambda m: (m, k))
    gain = pl.BlockSpec((1, LANES), lambda m: (0, 0))
    tab = pl.BlockSpec((tm, LANES), lambda m: (m, 0))
    return pl.pallas_call(
        body, name=name, grid=(S // tm,),
        in_specs=[row(0), row(1), gain, gain, tab, tab, tab] + [row(0)] * (3 * n_p),
        out_specs=[pl.BlockSpec((tm, 3 * W), lambda m: (m, 0)), gain, gain],
        out_shape=[jax.ShapeDtypeStruct((S, 3 * W), BF16), jax.ShapeDtypeStruct((1, LANES), F32),
                   jax.ShapeDtypeStruct((1, LANES), F32)],
        compiler_params=_cp("arbitrary"),
    )(qkv, qkv, gq, gk, *tables, *dqs, *dks, *dvs)


ATTN_BQ = 2 * BAND
ATTN_ROWS = 16 * ATTN_BQ
V_COL = 2 * N_HEADS * HEAD_DIM // LANES


def _attn_geometry(S, d):
    rows = min(ATTN_ROWS, S)
    halo = BAND * d
    assert rows % (ATTN_BQ * d) == 0 and S % rows == 0, (S, d)
    return rows, halo, rows // (ATTN_BQ * d)


def _attn_specs(S, d, col):
    rows, halo, _ = _attn_geometry(S, d)
    r = rows // halo
    n_h = S // halo
    prev = pl.BlockSpec((halo, LANES), lambda j, i: (jnp.maximum(i * r - 1, 0), col + j))
    cur = pl.BlockSpec((rows, LANES), lambda j, i: (i, col + j))
    nxt = pl.BlockSpec((halo, LANES), lambda j, i: (jnp.minimum((i + 1) * r, n_h - 1), col + j))
    return [prev, cur, nxt]


def _fill_window(scr, prev, cur, nxt, rows, halo):
    scr[0:halo, :] = prev[...]
    scr[halo:halo + rows, :] = cur[...]
    scr[halo + rows:2 * halo + rows, :] = nxt[...]


def _chain_groups(n_sb, d, size):
    chains = [(sb, r) for sb in range(n_sb) for r in range(d)]
    return [chains[j:j + size] for j in range(0, len(chains), size)]


def _strided(ref, start, size, d):
    return ref[pl.ds(start, size, stride=d) if d > 1 else pl.ds(start, size), :]


def _band_mask(i, S, d, sb):
    rows, _, _ = _attn_geometry(S, d)
    L = S // d
    base = i * (rows // d) + sb * ATTN_BQ
    wk = ATTN_BQ + 2 * BAND
    row = lax.broadcasted_iota(jnp.int32, (ATTN_BQ, wk), 0)
    col = lax.broadcasted_iota(jnp.int32, (ATTN_BQ, wk), 1)
    lj = base - BAND + col
    return (jnp.abs(col - BAND - row) <= BAND) & (lj >= 0) & (lj < L)


def _attn_fwd(name, q, k, v, v_col, d, acc=None):
    S, W = q.shape
    rows, halo, n_sb = _attn_geometry(S, d)
    wk = ATTN_BQ + 2 * BAND
    scale = HEAD_DIM ** -0.5
    n_acc = 0 if acc is None else 2

    def body(q_ref, kp, kc, kn, vp, vc, vn, *rest):
        acc_refs, (o_ref, lse_ref, kw, vw) = rest[:n_acc], rest[n_acc:]
        i = pl.program_id(1)
        _fill_window(kw, kp, kc, kn, rows, halo)
        _fill_window(vw, vp, vc, vn, rows, halo)
        first = _first_head((ATTN_BQ, LANES))
        heads = (first, jnp.logical_not(first))
        zero = jnp.zeros((), BF16)
        for group in _chain_groups(n_sb, d, 4):
            masks = {sb: _band_mask(i, S, d, sb) for sb in sorted({sb for sb, _ in group})}
            starts = [r + d * sb * ATTN_BQ for sb, r in group]
            qs = [_strided(q_ref, st, ATTN_BQ, d).astype(BF16) for st in starts]
            ks = [_strided(kw, st, wk, d).astype(BF16) for st in starts]
            vs = [_strided(vw, st, wk, d).astype(BF16) for st in starts]
            s_all = [[_dot(jnp.where(hm, qv, zero), kv, 1, 1) for hm in heads] for qv, kv in zip(qs, ks)]
            p_all, den_all, lse_all = [], [], []
            for (sb, _), s_h in zip(group, s_all):
                s_h = [jnp.where(masks[sb], s * scale, NEG) for s in s_h]
                mx_h = [jnp.max(s, axis=-1, keepdims=True) for s in s_h]
                p_h = [jnp.exp(s - mx) for s, mx in zip(s_h, mx_h)]
                den_h = [jnp.sum(p, axis=-1, keepdims=True) for p in p_h]
                p_all.append([p.astype(BF16) for p in p_h])
                den_all.append(den_h)
                lse_all.append([mx + jnp.log(den) for mx, den in zip(mx_h, den_h)])
            o_all = [[_dot(p, vv, 1, 0) for p in p_h] for p_h, vv in zip(p_all, vs)]
            for st, o_h, den_h, lse_h in zip(starts, o_all, den_all, lse_all):
                dst = pl.ds(st, ATTN_BQ, stride=d) if d > 1 else pl.ds(st, ATTN_BQ)
                o_new = jnp.where(first, o_h[0] / den_h[0], o_h[1] / den_h[1])
                lse_new = jnp.where(first, lse_h[0], lse_h[1])
                if acc is not None:
                    o_old, lse_old = (_strided(r, st, ATTN_BQ, d) for r in acc_refs)
                    top = jnp.maximum(lse_old, lse_new)
                    e_old, e_new = jnp.exp(lse_old - top), jnp.exp(lse_new - top)
                    o_new = (e_old * o_old + e_new * o_new) / (e_old + e_new)
                    lse_new = top + jnp.log(e_old + e_new)
                o_ref[dst, :] = o_new
                lse_ref[dst, :] = lse_new

    cur = _attn_specs(S, d, 0)[1]
    return pl.pallas_call(
        body, name=name, grid=(W // LANES, S // rows),
        in_specs=[cur] + _attn_specs(S, d, 0) + _attn_specs(S, d, v_col) + [cur] * n_acc, out_specs=[cur, cur],
        out_shape=[jax.ShapeDtypeStruct((S, W), F32)] * 2,
        scratch_shapes=[pltpu.VMEM((rows + 2 * halo, LANES), F32)] * 2,
        compiler_params=_cp("parallel", "parallel"),
    )(q, k, k, k, v, v, v, *(acc or ()))


def _delta_epilogue(do, o):
    seg = _seg_matrix(1.0)
    prod = do * o.astype(F32)
    delta = [_seg_sum(prod[:, blk * LANES:(blk + 1) * LANES], seg) for blk in range(do.shape[1] // LANES)]
    return do, jnp.concatenate(delta, axis=1)


def _attn_bwd_q(name, q, k, v, v_col, do, lse, delta, d, acc=None):
    S, W = q.shape
    rows, halo, n_sb = _attn_geometry(S, d)
    wk = ATTN_BQ + 2 * BAND
    scale = HEAD_DIM ** -0.5
    accs = [] if acc is None else [acc]

    def body(q_ref, do_ref, l_ref, dl_ref, kp, kc, kn, vp, vc, vn, *rest):
        acc_refs, (dq_ref, kw, vw) = rest[:len(accs)], rest[len(accs):]
        i = pl.program_id(1)
        _fill_window(kw, kp, kc, kn, rows, halo)
        _fill_window(vw, vp, vc, vn, rows, halo)
        first = _first_head((ATTN_BQ, LANES))
        heads = (first, jnp.logical_not(first))
        zero = jnp.zeros((), BF16)
        wide = lambda t: jnp.concatenate([t] * (wk // LANES), axis=1)
        for group in _chain_groups(n_sb, d, 4):
            masks = {sb: _band_mask(i, S, d, sb) for sb in sorted({sb for sb, _ in group})}
            starts = [r + d * sb * ATTN_BQ for sb, r in group]
            qs = [_strided(q_ref, st, ATTN_BQ, d).astype(BF16) for st in starts]
            dos = [_strided(do_ref, st, ATTN_BQ, d).astype(BF16) for st in starts]
            ks = [_strided(kw, st, wk, d).astype(BF16) for st in starts]
            vs = [_strided(vw, st, wk, d).astype(BF16) for st in starts]
            s_all = [[_dot(jnp.where(hm, qv, zero), kv, 1, 1) for hm in heads] for qv, kv in zip(qs, ks)]
            dp_all = [[_dot(jnp.where(hm, dov, zero), vv, 1, 1) for hm in heads] for dov, vv in zip(dos, vs)]
            ds_all = []
            for (sb, _), st, s_h, dp_h in zip(group, starts, s_all, dp_all):
                lv, dlv = _strided(l_ref, st, ATTN_BQ, d), _strided(dl_ref, st, ATTN_BQ, d)
                l_sw, dl_sw = pltpu.roll(lv, HEAD_DIM, 1), pltpu.roll(dlv, HEAD_DIM, 1)
                ds_h = []
                for hm, s, dp in zip(heads, s_h, dp_h):
                    p = jnp.exp(jnp.where(masks[sb], s * scale, NEG) - wide(jnp.where(hm, lv, l_sw)))
                    ds_h.append((p * (dp - wide(jnp.where(hm, dlv, dl_sw))) * scale).astype(BF16))
                ds_all.append(ds_h)
            dq_all = [[_dot(ds, kv, 1, 0) for ds in ds_h] for ds_h, kv in zip(ds_all, ks)]
            for st, dq_h in zip(starts, dq_all):
                dst = pl.ds(st, ATTN_BQ, stride=d) if d > 1 else pl.ds(st, ATTN_BQ)
                dq = jnp.where(first, dq_h[0], dq_h[1])
                dq_ref[dst, :] = dq + _strided(acc_refs[0], st, ATTN_BQ, d) if accs else dq

    cur = _attn_specs(S, d, 0)[1]
    return pl.pallas_call(
        body, name=name, grid=(W // LANES, S // rows),
        in_specs=[cur] * 4 + _attn_specs(S, d, 0) + _attn_specs(S, d, v_col) + [cur] * len(accs), out_specs=cur,
        out_shape=jax.ShapeDtypeStruct((S, W), F32),
        scratch_shapes=[pltpu.VMEM((rows + 2 * halo, LANES), F32)] * 2,
        compiler_params=_cp("parallel", "parallel"),
    )(q, do, lse, delta, k, k, k, v, v, v, *accs)


def _attn_bwd_kv(name, q, k, v, v_col, do, lse, delta, d, acc=None):
    S, W = q.shape
    rows, halo, n_sb = _attn_geometry(S, d)
    wk = ATTN_BQ + 2 * BAND
    scale = HEAD_DIM ** -0.5
    n_acc = 0 if acc is None else 2

    def body(k_ref, v_ref, qp, qc, qn, dop, doc, don, lp, lc, ln, dlp, dlc, dln, *rest):
        acc_refs, (dk_ref, dv_ref, qw, dow, lw, dlw) = rest[:n_acc], rest[n_acc:]
        i = pl.program_id(1)
        _fill_window(qw, qp, qc, qn, rows, halo)
        _fill_window(dow, dop, doc, don, rows, halo)
        _fill_window(lw, lp, lc, ln, rows, halo)
        _fill_window(dlw, dlp, dlc, dln, rows, halo)
        first = _first_head((ATTN_BQ, LANES))
        heads = (first, jnp.logical_not(first))
        zero = jnp.zeros((), BF16)
        for group in _chain_groups(n_sb, d, 2):
            masks = {sb: _band_mask(i, S, d, sb) for sb in sorted({sb for sb, _ in group})}
            starts = [r + d * sb * ATTN_BQ for sb, r in group]
            ks = [_strided(k_ref, st, ATTN_BQ, d).astype(BF16) for st in starts]
            vs = [_strided(v_ref, st, ATTN_BQ, d).astype(BF16) for st in starts]
            qs = [_strided(qw, st, wk, d).astype(BF16) for st in starts]
            dos = [_strided(dow, st, wk, d).astype(BF16) for st in starts]
            s_all = [[_dot(jnp.where(hm, kv, zero), qv, 1, 1) for hm in heads] for kv, qv in zip(ks, qs)]
            dp_all = [[_dot(jnp.where(hm, vv, zero), dov, 1, 1) for hm in heads] for vv, dov in zip(vs, dos)]
            p_all, ds_all = [], []
            for (sb, _), st, s_h, dp_h in zip(group, starts, s_all, dp_all):
                l_t, dl_t = _strided(lw, st, wk, d).T, _strided(dlw, st, wk, d).T
                p_h = [jnp.exp(jnp.where(masks[sb], s * scale, NEG) - l_t[hh * HEAD_DIM:hh * HEAD_DIM + 1, :])
                       for hh, s in enumerate(s_h)]
                ds_all.append([(p * (dp - dl_t[hh * HEAD_DIM:hh * HEAD_DIM + 1, :]) * scale).astype(BF16)
                               for hh, (p, dp) in enumerate(zip(p_h, dp_h))])
                p_all.append([p.astype(BF16) for p in p_h])
            dv_all = [[_dot(p, dov, 1, 0) for p in p_h] for p_h, dov in zip(p_all, dos)]
            dk_all = [[_dot(ds, qv, 1, 0) for ds in ds_h] for ds_h, qv in zip(ds_all, qs)]
            for st, dk_h, dv_h in zip(starts, dk_all, dv_all):
                dst = pl.ds(st, ATTN_BQ, stride=d) if d > 1 else pl.ds(st, ATTN_BQ)
                dk, dv = jnp.where(first, dk_h[0], dk_h[1]), jnp.where(first, dv_h[0], dv_h[1])
                if acc is not None:
                    dk, dv = dk + _strided(acc_refs[0], st, ATTN_BQ, d), dv + _strided(acc_refs[1], st, ATTN_BQ, d)
                dk_ref[dst, :] = dk
                dv_ref[dst, :] = dv

    cur = _attn_specs(S, d, 0)[1]
    win = _attn_specs(S, d, 0)
    return pl.pallas_call(
        body, name=name, grid=(W // LANES, S // rows),
        in_specs=[cur, _attn_specs(S, d, v_col)[1]] + win * 4 + [cur] * n_acc, out_specs=[cur, cur],
        out_shape=[jax.ShapeDtypeStruct((S, W), F32)] * 2,
        scratch_shapes=[pltpu.VMEM((rows + 2 * halo, LANES), F32)] * 4,
        compiler_params=_cp("parallel", "parallel"),
    )(k, v, q, q, q, do, do, do, lse, lse, lse, delta, delta, delta, *(acc or ()))


def _place():
    x, y, c = lax.axis_index("x"), lax.axis_index("y"), lax.axis_index("c")
    chips = [(1 - x, y), (x, 1 - y), (1 - x, 1 - y)]
    return x, y, c, chips


HBM = pl.BlockSpec(memory_space=pltpu.HBM)
SEM = pl.BlockSpec(memory_space=pltpu.SEMAPHORE)
DATAFLOW = pltpu.SideEffectType.DATAFLOW_SIDE_EFFECTING


N_PEERS = {"gather": 3, "scatter": 7, "allgather": 7}


def _exchange_copies(kind, srcs, dsts, send_sems, recv_sems):
    x, y, c, chips = _place()
    mine = 2 * x + y
    n_peers = N_PEERS[kind]
    cps = []
    for t in range(len(srcs)):
        for k in range(n_peers):
            if kind == "gather":
                (px, py), pc = chips[k], c
                src, dst = srcs[t], dsts[t].at[mine]
            else:
                bits = k + 1
                px, py, pc = (1 - x if bits & 4 else x), (1 - y if bits & 2 else y), (1 - c if bits & 1 else c)
                src, dst = (srcs[t].at[2 * px + py] if kind == "scatter" else srcs[t]), dsts[t].at[k]
            cps.append(pltpu.make_async_remote_copy(
                src_ref=src, dst_ref=dst, send_sem=send_sems.at[n_peers * t + k], recv_sem=recv_sems.at[n_peers * t + k],
                device_id=(px, py, pc), device_id_type=MESH))
    return cps


def _exchange_start(name, kind, groups):
    sizes = [len(g) for g in groups]
    n, n_g = sum(sizes), len(groups)

    def body(*refs):
        srcs, dsts = refs[:n], refs[n:2 * n]
        sems = refs[2 * n:2 * n + 2 * n_g]
        token = refs[4 * n + 2 * n_g]
        off = 0
        for gi, size in enumerate(sizes):
            for cp in _exchange_copies(kind, srcs[off:off + size], dsts[off:off + size], sems[2 * gi], sems[2 * gi + 1]):
                cp.start()
            off += size
        token[...] = jnp.zeros_like(token)

    arrays = [pltpu.with_memory_space_constraint(a, pltpu.HBM) for a in
              [s for g in groups for s, _ in g] + [d for g in groups for _, d in g]]
    sem_shapes = []
    for size in sizes:
        sem_shapes += [pltpu.SemaphoreType.DMA((N_PEERS[kind] * size,))] * 2
    outs = pl.pallas_call(
        body, name=name,
        in_specs=[HBM] * (2 * n),
        out_specs=[SEM] * (2 * n_g) + [HBM] * (2 * n) + [pl.BlockSpec(memory_space=pltpu.VMEM)],
        out_shape=sem_shapes + [pltpu.HBM(a.shape, a.dtype) for a in arrays] + [jax.ShapeDtypeStruct((8, LANES), F32)],
        input_output_aliases={t: 2 * n_g + t for t in range(2 * n)},
        compiler_params=pltpu.CompilerParams(has_side_effects=DATAFLOW),
    )(*arrays)
    sems, thru, token = outs[:2 * n_g], outs[2 * n_g:-1], outs[-1]
    handles, off = [], 0
    for gi, size in enumerate(sizes):
        handles.append((sems[2 * gi], sems[2 * gi + 1], thru[off:off + size], thru[n + off:n + off + size]))
        off += size
    return handles, token


def _exchange_wait(name, kind, handle, after):
    send_sems, recv_sems, srcs, dsts = handle
    n = len(srcs)

    def body(*refs):
        for cp in _exchange_copies(kind, refs[:n], refs[n:2 * n], refs[2 * n], refs[2 * n + 1]):
            cp.wait_send()
            cp.wait_recv()

    outs = pl.pallas_call(
        body, name=name,
        in_specs=[HBM] * (2 * n) + [SEM, SEM, ANY], out_specs=[HBM] * (2 * n),
        out_shape=[pltpu.HBM(a.shape, a.dtype) for a in (*srcs, *dsts)],
        input_output_aliases={t: t for t in range(2 * n)},
        compiler_params=pltpu.CompilerParams(has_side_effects=DATAFLOW),
    )(*srcs, *dsts, send_sems, recv_sems, after)
    return outs[:n], outs[n:]


def _prepare_shard(name, w, idx, dtype, mine, anchor=None):
    _, R, C = w.shape
    tr = _row_tile(R)
    anchors = [] if anchor is None else [anchor]

    def body(mine_ref, w_ref, *rest):
        src_ref, land_ref = rest[len(anchors):]
        val = w_ref[...].astype(dtype)
        src_ref[...] = val
        land_ref[...] = val

    return pl.pallas_call(
        body, name=name,
        grid_spec=pltpu.PrefetchScalarGridSpec(
            num_scalar_prefetch=1, grid=(R // tr,),
            in_specs=[pl.BlockSpec((None, tr, C), lambda i, s: (idx, i, 0))]
            + [pl.BlockSpec((8, LANES), lambda i, s: (0, 0))] * len(anchors),
            out_specs=[pl.BlockSpec((tr, C), lambda i, s: (i, 0)), pl.BlockSpec((None, tr, C), lambda i, s: (s[0], i, 0))]),
        out_shape=[jax.ShapeDtypeStruct((R, C), dtype), jax.ShapeDtypeStruct((N_SHARDS, R, C), dtype)],
        compiler_params=_cp("parallel"),
    )(mine, w, *anchors)


def _ordered_sum(name, own, recv, order):
    rows, C = own.shape
    tr = _row_tile(rows)
    n_s = len(order)

    def body(*refs):
        me = refs[1][0]
        own_ref, theirs, out_ref = refs[n_s], refs[n_s + 1:n_s + 9], refs[n_s + 9]
        g = None
        for p in range(8):
            term = jnp.where(me == p, own_ref[...], theirs[p][...])
            g = term if g is None else g + term
        out_ref[...] = g

    blk = pl.BlockSpec((tr, C), lambda i, *s: (i, 0))
    slot = lambda p: pl.BlockSpec((None, tr, C), lambda i, *s: (jnp.maximum(s[2 + p][0], 0), i, 0))
    return pl.pallas_call(
        body, name=name,
        grid_spec=pltpu.PrefetchScalarGridSpec(num_scalar_prefetch=n_s, grid=(rows // tr,),
                                               in_specs=[blk] + [slot(p) for p in range(8)], out_specs=blk),
        out_shape=jax.ShapeDtypeStruct((rows, C), F32), compiler_params=_cp("parallel"),
    )(*order, own, *([recv] * 8))


MM_TM_K = 512
WGRAD_TM = 2048


def _rows_merged(w):
    return w.reshape(1, w.shape[0] * w.shape[1], w.shape[2])


def _sq_relu_epilogue(acc):
    r = jnp.maximum(acc, 0.0)
    return acc, r * r


def _add_epilogue(acc, x):
    return (acc + x,)


def _add_loss_epilogue(acc, x, target):
    e = acc + x - target
    D = e.shape[1]
    share = (0.5 / D) * jnp.sum(jnp.sum(e * e, axis=1, keepdims=True), axis=0, keepdims=True)
    return e * (1.0 / D), jnp.broadcast_to(share, (1, D))


def _add_norm_epilogue(acc, x, g):
    y = acc + x
    r = lax.rsqrt(jnp.mean(y * y, axis=-1, keepdims=True) + EPS)
    return y, y * r * g


def _norm_bwd_epilogue(dh, x, dres, g):
    r = lax.rsqrt(jnp.mean(x * x, axis=-1, keepdims=True) + EPS)
    xhat = x * r
    dxhat = dh * g
    dx = dres + r * (dxhat - xhat * jnp.mean(dxhat * xhat, axis=-1, keepdims=True))
    return dx, jnp.sum(dh * xhat, axis=0, keepdims=True)


def _sq_relu_grad_epilogue(acc, a):
    return (acc * (2.0 * jnp.maximum(a.astype(F32), 0.0)),)


STAGES = ("mixer_in", "mixer_out", "mlp")


def _stage_tensors(layer, stage):
    i = layer // 2
    if stage == "mlp":
        return [("mlp_w1", layer), ("mlp_w2", layer)]
    if stage == "mixer_in":
        return [("ab_w_in", i)] if layer % 2 == 0 else [("c_w_qkv", i)]
    return [("b_conv_w", i), ("ab_w_out", i)] if layer % 2 == 0 else [("c_w_out", i)]


def _local_step(x, target, p, weights_of, grads_done):
    S, D = x.shape
    depth = p["mix_norm_g"].shape[0]
    n_even = (depth + 1) // 2
    mix_g3 = p["mix_norm_g"].reshape(depth, 1, D)
    mlp_g3 = p["mlp_norm_g"].reshape(depth, 1, D)
    vec3 = lambda t: t.reshape(t.shape[0], 1, t.shape[1])
    spw16 = p["a_spatial_w"].astype(BF16)
    spw16_t = jnp.swapaxes(spw16, 2, 3)
    bias_full = jnp.repeat(jnp.swapaxes(p["a_spatial_b"], 1, 2), HEAD_DIM, axis=2)
    vn_g, vn_b, cn_g, cn_b, cb3 = (vec3(p[k]) for k in ("a_vnorm_g", "a_vnorm_b", "b_norm_g", "b_norm_b", "b_conv_b"))
    tables = _rope_tables(S)
    gq = jnp.tile(p["c_q_norm_g"], (1, 2))
    gk = jnp.tile(p["c_k_norm_g"], (1, 2))

    saved = []
    h = _rms_fwd("mix_norm_0", x, mix_g3, 0)
    for layer in range(depth):
        i = layer // 2
        wl = dict(weights_of(layer, "mixer_in", x))
        rec = {"x_mix": x, "w": wl, "h_mix": h}
        if layer % 2 == 0:
            (z,) = _mm_ngroup(f"ab_in_{layer}", h, wl["ab_w_in"], nt=False, tm=MM_TM_K, out_dtypes=[F32])
            wl.update(weights_of(layer, "mixer_out", z))
            gconv = _glu_conv_fwd(f"glu_conv_{layer}", z, wl["b_conv_w"], cb3, i)
            x, h, cat = _ab_tail_out_proj(f"ab_out_{layer}", z, gconv, spw16, bias_full, vn_g, vn_b, cn_g, cn_b, i,
                                          _rows_merged(wl["ab_w_out"]), x, mlp_g3, layer)
            rec.update(z=z, gconv=gconv, cat=cat)
        else:
            (qkv,) = _mm_ngroup(f"c_qkv_{layer}", h, wl["c_w_qkv"], nt=False, tm=MM_TM_K, out_dtypes=[F32])
            wl.update(weights_of(layer, "mixer_out", qkv))
            qn, kn = _qk_fwd(f"qk_norm_rope_{layer}", qkv, gq[i:i + 1], gk[i:i + 1], tables)
            merged = None
            for d in PATTERN_DILATIONS:
                merged = _attn_fwd(f"attn_d{d}_{layer}", qn, kn, qkv, V_COL, d, merged)
            o, lse = merged
            rec.update(qkv=qkv, qn=qn, kn=kn, o=o, lse=lse)
            x, h = _mm_kgroup(f"c_out_{layer}", o, _rows_merged(wl["c_w_out"]), nt=False, tm=MM_TM_K,
                              out_dtypes=[F32, BF16], extras=(x,), vecs=[(mlp_g3, layer)], epilogue=_add_norm_epilogue)
        rec["x_mlp"] = x
        wl.update(weights_of(layer, "mlp", x))
        a, hsq = _mm_ngroup(f"mlp_up_{layer}", h, wl["mlp_w1"], nt=False, tm=MM_TM_K, out_dtypes=[BF16, BF16],
                            epilogue=_sq_relu_epilogue)
        rec.update(h_mlp=h, a=a, hsq=hsq)
        if layer + 1 < depth:
            x, h = _mm_kgroup(f"mlp_down_{layer}", hsq, _rows_merged(wl["mlp_w2"]), nt=False, tm=MM_TM_K,
                              out_dtypes=[F32, BF16], extras=(x,), vecs=[(mix_g3, layer + 1)], epilogue=_add_norm_epilogue)
        else:
            dx, loss_row = _mm_kgroup(f"mlp_down_{layer}", hsq, _rows_merged(wl["mlp_w2"]), nt=False, tm=MM_TM_K,
                                      out_dtypes=[F32], extras=(x, target), n_sums=1, epilogue=_add_loss_epilogue)
        saved.append(rec)

    small = {k: [None] * v.shape[0] for k, v in p.items()}
    token = None
    for layer in reversed(range(depth)):
        i = layer // 2
        rec = saved[layer]
        wl = rec["w"]
        g = {}
        (da,) = _mm_ngroup(f"mlp_down_dgrad_{layer}", dx, wl["mlp_w2"], nt=True, tm=MM_TM_K, out_dtypes=[BF16],
                           extras=(rec["a"],), epilogue=_sq_relu_grad_epilogue, anchor=token)
        g["mlp_w2"] = _wgrad(f"mlp_down_wgrad_{layer}", rec["hsq"], dx, wl["mlp_w2"].shape, a_group=True, tm=WGRAD_TM)
        g["mlp_w1"] = _wgrad(f"mlp_up_wgrad_{layer}", rec["h_mlp"], da, wl["mlp_w1"].shape, a_group=False, tm=WGRAD_TM)
        dx, small["mlp_norm_g"][layer] = _mm_kgroup(
            f"mlp_up_dgrad_{layer}", da, wl["mlp_w1"], nt=True, tm=MM_TM_K, out_dtypes=[F32], extras=(rec["x_mlp"], dx),
            vecs=[(mlp_g3, layer)], n_sums=1, epilogue=_norm_bwd_epilogue)
        token = grads_done(layer, "mlp", g)
        g = {}
        if layer % 2 == 0:
            w_out = _rows_merged(wl["ab_w_out"])
            (dcat,) = _mm_ngroup(f"ab_out_dgrad_{layer}", dx, w_out, nt=True, tm=MM_TM_K, out_dtypes=[F32], anchor=token)
            g["ab_w_out"] = _wgrad(f"ab_out_wgrad_{layer}", rec["cat"], dx, w_out.shape, a_group=True,
                                   tm=WGRAD_TM).reshape(wl["ab_w_out"].shape)
            dz, dgconv, dspw, dbias, dvg, dvb, dcg, dcb = _ab_tail_bwd(
                f"ab_tail_bwd_{layer}", rec["z"], rec["gconv"], dcat, spw16, spw16_t, bias_full, vn_g, vn_b, cn_g, cn_b, i)
            dz, g["b_conv_w"], dcbias = _glu_conv_bwd(f"glu_conv_bwd_{layer}", rec["z"], dgconv, dz, wl["b_conv_w"])
            token = grads_done(layer, "mixer_out", g)
            g = {}
            small["a_spatial_w"][i] = dspw
            small["a_spatial_b"][i] = _fold_bias(dbias)[:, :A_GROUPS].T
            for k, val in (("a_vnorm_g", dvg), ("a_vnorm_b", dvb), ("b_norm_g", dcg), ("b_norm_b", dcb), ("b_conv_b", dcbias)):
                small[k][i] = val
            g["ab_w_in"] = _wgrad(f"ab_in_wgrad_{layer}", rec["h_mix"], dz, wl["ab_w_in"].shape, a_group=False, tm=WGRAD_TM,
                                  anchor=token)
            dgrad = (f"ab_in_dgrad_{layer}", dz, wl["ab_w_in"])
        else:
            w_out = _rows_merged(wl["c_w_out"])
            do, delta = _mm_ngroup(f"c_out_dgrad_{layer}", dx, w_out, nt=True, tm=MM_TM_K, out_dtypes=[F32, F32],
                                   extras=(rec["o"],), epilogue=_delta_epilogue, anchor=token)
            g["c_w_out"] = _wgrad(f"c_out_wgrad_{layer}", rec["o"], dx, w_out.shape, a_group=True,
                                  tm=WGRAD_TM).reshape(wl["c_w_out"].shape)
            token = grads_done(layer, "mixer_out", g)
            g = {}
            attn_args = (rec["qn"], rec["kn"], rec["qkv"], V_COL, do, rec["lse"], delta)
            dq = dkv = None
            for d in PATTERN_DILATIONS:
                dq = _attn_bwd_q(f"attn_bwd_q_d{d}_{layer}", *attn_args, d, dq)
                dkv = _attn_bwd_kv(f"attn_bwd_kv_d{d}_{layer}", *attn_args, d, dkv)
            dqkv, dgq, dgk = _qk_bwd(f"qk_norm_rope_bwd_{layer}", rec["qkv"], gq[i:i + 1], gk[i:i + 1], tables,
                                     [dq], [dkv[0]], [dkv[1]])
            small["c_q_norm_g"][i] = dgq[:, :HEAD_DIM]
            small["c_k_norm_g"][i] = dgk[:, :HEAD_DIM]
            g["c_w_qkv"] = _wgrad(f"c_qkv_wgrad_{layer}", rec["h_mix"], dqkv, wl["c_w_qkv"].shape, a_group=False, tm=WGRAD_TM,
                                  anchor=token)
            dgrad = (f"c_qkv_dgrad_{layer}", dqkv, wl["c_w_qkv"])
        token = grads_done(layer, "mixer_in", g)
        dx, small["mix_norm_g"][layer] = _mm_kgroup(
            *dgrad, nt=True, tm=MM_TM_K, out_dtypes=[F32], extras=(rec["x_mix"], dx), vecs=[(mix_g3, layer)], n_sums=1,
            epilogue=_norm_bwd_epilogue, anchor=token)

    small = {k: jnp.stack([t.reshape(p[k].shape[1:]) for t in v]) for k, v in small.items()}
    return loss_row, dx, small


SHARDED = ("mlp_w1", "mlp_w2", "ab_w_in", "b_conv_w", "ab_w_out", "c_w_qkv", "c_w_out")
SMALL = ("mix_norm_g", "mlp_norm_g", "a_spatial_w", "a_spatial_b", "a_vnorm_g", "a_vnorm_b", "b_conv_b", "b_norm_g",
         "b_norm_b", "c_q_norm_g", "c_k_norm_g")
WEIGHTS = ("mix_norm_g", "mlp_norm_g", "mlp_w1", "mlp_w2", "ab_w_in", "a_spatial_w", "a_spatial_b", "a_vnorm_g",
           "a_vnorm_b", "b_conv_w", "b_conv_b", "b_norm_g", "b_norm_b", "ab_w_out", "c_w_qkv", "c_q_norm_g",
           "c_k_norm_g", "c_w_out")


def _pack(parts):
    flat = jnp.concatenate([parts[k].reshape(-1) for k in SMALL])
    rows = -(-flat.shape[0] // (256 * LANES)) * 256
    return jnp.pad(flat, (0, rows * LANES - flat.shape[0])).reshape(rows, LANES)


def _unpack(packed, like):
    flat = packed.reshape(-1)
    out, off = {}, 0
    for k in SMALL:
        n = like[k].size
        out[k] = flat[off:off + n].reshape(like[k].shape)
        off += n
    return out


def kernel(x, mix_norm_g, mlp_norm_g, mlp_w1, mlp_w2, ab_w_in, a_spatial_w, a_spatial_b, a_vnorm_g, a_vnorm_b, b_conv_w, b_conv_b, b_norm_g, b_norm_b, ab_w_out, c_w_qkv, c_q_norm_g, c_k_norm_g, c_w_out, loss_target, m_mix_norm_g, m_mlp_norm_g, m_mlp_w1, m_mlp_w2, m_ab_w_in, m_a_spatial_w, m_a_spatial_b, m_a_vnorm_g, m_a_vnorm_b, m_b_conv_w, m_b_conv_b, m_b_norm_g, m_b_norm_b, m_ab_w_out, m_c_w_qkv, m_c_q_norm_g, m_c_k_norm_g, m_c_w_out, v_mix_norm_g, v_mlp_norm_g, v_mlp_w1, v_mlp_w2, v_ab_w_in, v_a_spatial_w, v_a_spatial_b, v_a_vnorm_g, v_a_vnorm_b, v_b_conv_w, v_b_conv_b, v_b_norm_g, v_b_norm_b, v_ab_w_out, v_c_w_qkv, v_c_q_norm_g, v_c_k_norm_g, v_c_w_out):
    w = dict(mix_norm_g=mix_norm_g, mlp_norm_g=mlp_norm_g, mlp_w1=mlp_w1, mlp_w2=mlp_w2, ab_w_in=ab_w_in,
             a_spatial_w=a_spatial_w, a_spatial_b=a_spatial_b, a_vnorm_g=a_vnorm_g, a_vnorm_b=a_vnorm_b,
             b_conv_w=b_conv_w, b_conv_b=b_conv_b, b_norm_g=b_norm_g, b_norm_b=b_norm_b, ab_w_out=ab_w_out,
             c_w_qkv=c_w_qkv, c_q_norm_g=c_q_norm_g, c_k_norm_g=c_k_norm_g, c_w_out=c_w_out)
    m = dict(mix_norm_g=m_mix_norm_g, mlp_norm_g=m_mlp_norm_g, mlp_w1=m_mlp_w1, mlp_w2=m_mlp_w2, ab_w_in=m_ab_w_in,
             a_spatial_w=m_a_spatial_w, a_spatial_b=m_a_spatial_b, a_vnorm_g=m_a_vnorm_g, a_vnorm_b=m_a_vnorm_b,
             b_conv_w=m_b_conv_w, b_conv_b=m_b_conv_b, b_norm_g=m_b_norm_g, b_norm_b=m_b_norm_b, ab_w_out=m_ab_w_out,
             c_w_qkv=m_c_w_qkv, c_q_norm_g=m_c_q_norm_g, c_k_norm_g=m_c_k_norm_g, c_w_out=m_c_w_out)
    v = dict(mix_norm_g=v_mix_norm_g, mlp_norm_g=v_mlp_norm_g, mlp_w1=v_mlp_w1, mlp_w2=v_mlp_w2, ab_w_in=v_ab_w_in,
             a_spatial_w=v_a_spatial_w, a_spatial_b=v_a_spatial_b, a_vnorm_g=v_a_vnorm_g, a_vnorm_b=v_a_vnorm_b,
             b_conv_w=v_b_conv_w, b_conv_b=v_b_conv_b, b_norm_g=v_b_norm_g, b_norm_b=v_b_norm_b, ab_w_out=v_ab_w_out,
             c_w_qkv=v_c_w_qkv, c_q_norm_g=v_c_q_norm_g, c_k_norm_g=v_c_k_norm_g, c_w_out=v_c_w_out)

    S, D = x.shape[1], x.shape[2]
    depth = mix_norm_g.shape[0]
    mine = (2 * lax.axis_index("x") + lax.axis_index("y")).astype(jnp.int32).reshape(1)

    stages = [(layer, stage) for layer in range(depth) for stage in STAGES]

    def start_gather(name, some_stages, anchor):
        groups = [[_prepare_shard(f"prepare_{k}_{i}", w[k], i, F32 if k == "b_conv_w" else BF16, mine, anchor)
                   for k, i in _stage_tensors(*st)] for st in some_stages]
        return _exchange_start(name, "gather", groups)

    first, rest = stages[:len(STAGES)], stages[len(STAGES):]
    handles_first, token_first = start_gather("gather_weights_start_first", first, None)
    handles_rest, gather_token = start_gather("gather_weights_start_rest", rest, token_first)
    handles = dict(zip(first + rest, handles_first + handles_rest))

    def weights_of(layer, stage, after):
        _, got = _exchange_wait(f"gather_weights_wait_{layer}_{stage}", "gather", handles[layer, stage],
                                gather_token if (layer, stage) == stages[0] else after)
        return {k: a for (k, _), a in zip(_stage_tensors(layer, stage), got)}

    scattered = {}

    def grads_done(layer, stage, g):
        names = [k for k, _ in _stage_tensors(layer, stage)]
        group = [(g[k], lax.empty((N_PEERS["scatter"],) + g[k].shape[1:], BF16)) for k in names]
        (scattered[layer, stage],), token = _exchange_start(f"scatter_grads_start_{layer}_{stage}", "scatter", [group])
        return token

    small_params = {k: w[k] for k in SMALL}
    loss_row, dx, small_grads = _local_step(x.reshape(S, D), loss_target.reshape(S, D), small_params, weights_of, grads_done)

    loss = lax.psum(loss_row[0, 0], ("x", "y", "c"))

    packed = _pack(small_grads)
    (small_handle,), small_token = _exchange_start(
        "allgather_small_grads_start", "allgather", [[(packed, lax.empty((N_PEERS["allgather"],) + packed.shape, F32))]])

    order = _sum_order()
    stacked = {k: [lax.empty(w[k].shape, F32) for _ in range(4)] for k in SHARDED}
    for layer, stage in reversed(stages):
        gbs, recvs = _exchange_wait(f"scatter_grads_wait_{layer}_{stage}", "scatter", scattered[layer, stage], small_token)
        for (k, i), gb, recv in zip(_stage_tensors(layer, stage), gbs, recvs):
            stacked[k] = _adamw_layer(f"adamw_{k}_{i}", w[k], m[k], v[k], i, gb, recv, order, stacked[k])
    grads, deltas, new_m, new_v = ({k: stacked[k][j] for k in SHARDED} for j in range(4))

    last_updated = stacked[_stage_tensors(*stages[0])[-1][0]][0]
    (packed,), (recv,) = _exchange_wait("allgather_small_grads_wait", "allgather", small_handle, last_updated)
    g_small = _ordered_sum("sum_small_grads", packed, recv, order)
    outs = _adamw("adamw_small", _pack(small_params), _pack({k: m[k] for k in SMALL}), _pack({k: v[k] for k in SMALL}), g_small)
    for d_, packed in zip((grads, deltas, new_m, new_v), outs):
        d_.update(_unpack(packed, small_params))

    return (loss, dx.reshape(1, S, D), *[grads[k] for k in WEIGHTS], *[deltas[k] for k in WEIGHTS],
            *[new_m[k] for k in WEIGHTS], *[new_v[k] for k in WEIGHTS])
```

```python
import jax
import jax.numpy as jnp
from jax import lax
from jax.experimental import pallas as pl
from jax.experimental.pallas import tpu as pltpu

F32, BF16 = jnp.float32, jnp.bfloat16
MESH = pl.DeviceIdType.MESH
ANY = pl.BlockSpec(memory_space=pl.ANY)

VMEM_LIMIT_BYTES = 56 * 1024 * 1024
LANES = 128
ELEMENTWISE_ROWS = 256

EPS = 1e-6
NEG = -1e30
HEAD_DIM = 64
N_HEADS = 16
CHUNK = 128
A_GROUPS = 8
CONV_WIDTH = 31
CONV_HALO = 16
CONV_CHUNK = 64
BAND = 64
PATTERN_DILATIONS = (1, 4, 16)
ROT_DIM = 16
ROPE_THETA = 500000.0
N_SHARDS = 4

ADAM_LR, ADAM_B1, ADAM_B2, ADAM_EPS, ADAM_WD, ADAM_STEP = 0.001, 0.9, 0.999, 1e-08, 0.01, 10


def _cp(*sem):
    return pltpu.CompilerParams(dimension_semantics=sem, vmem_limit_bytes=VMEM_LIMIT_BYTES)


def _tile(n, pref):
    t = min(n, pref)
    assert n % t == 0, (n, pref)
    return t


def _dot(a, b, ca, cb):
    return lax.dot_general(a, b, (((ca,), (cb,)), ((), ())), preferred_element_type=F32)


def _mm_ngroup(name, a, w, *, nt, tm, out_dtypes, extras=(), epilogue=None, anchor=None):
    M, K = a.shape
    G, R, C = w.shape
    nw = R if nt else C
    assert K == (C if nt else R)
    tm = _tile(M, tm)
    n_ex = len(extras)
    anchors = [] if anchor is None else [anchor]

    def body(a_ref, w_ref, *rest):
        rest = rest[len(anchors):]
        av = a_ref[...].astype(BF16)
        for g in range(G):
            cols = slice(g * nw, (g + 1) * nw)
            acc = _dot(av, w_ref[g], 1, 1 if nt else 0)
            res = epilogue(acc, *[e[:, cols] for e in rest[:n_ex]]) if epilogue else (acc,)
            for o_ref, r in zip(rest[n_ex:], res):
                o_ref[:, cols] = r.astype(o_ref.dtype)

    blk = pl.BlockSpec((tm, G * nw), lambda m: (m, 0))
    return pl.pallas_call(
        body, name=name, grid=(M // tm,),
        in_specs=[pl.BlockSpec((tm, K), lambda m: (m, 0)), pl.BlockSpec((G, R, C), lambda m: (0, 0, 0))]
        + [pl.BlockSpec((8, LANES), lambda m: (0, 0))] * len(anchors) + [blk] * n_ex,
        out_specs=[blk] * len(out_dtypes),
        out_shape=[jax.ShapeDtypeStruct((M, G * nw), dt) for dt in out_dtypes],
        compiler_params=_cp("parallel"),
    )(a, w, *anchors, *extras)


def _mm_kgroup(name, a, w, *, nt, tm, out_dtypes, extras=(), vecs=(), n_sums=0, epilogue=None, anchor=None):
    G, R, C = w.shape
    kw, N = (C, R) if nt else (R, C)
    if a.ndim == 3:
        M = a.shape[1]
        assert a.shape[0] == G and a.shape[2] == kw
    else:
        M = a.shape[0]
        assert a.shape[1] == G * kw
    tm = _tile(M, tm)
    n_ex = len(extras)
    a_spec = (pl.BlockSpec((G, tm, kw), lambda m: (0, m, 0)) if a.ndim == 3 else pl.BlockSpec((tm, G * kw), lambda m: (m, 0)))
    anchors = [] if anchor is None else [anchor]

    def body(a_ref, w_ref, *rest):
        rest = rest[len(anchors):]
        acc = None
        for g in range(G):
            a_g = a_ref[g] if a.ndim == 3 else a_ref[:, g * kw:(g + 1) * kw]
            part = _dot(a_g.astype(BF16), w_ref[g], 1, 1 if nt else 0)
            acc = part if acc is None else acc + part
        n_in = n_ex + len(vecs)
        res = epilogue(acc, *[e[...] for e in rest[:n_in]]) if epilogue else (acc,)
        outs = rest[n_in:]
        n_tiles = len(outs) - n_sums
        for o_ref, r in zip(outs[:n_tiles], res[:n_tiles]):
            o_ref[...] = r.astype(o_ref.dtype)
        if n_sums:
            @pl.when(pl.program_id(0) == 0)
            def _():
                for s_ref in outs[n_tiles:]:
                    s_ref[...] = jnp.zeros_like(s_ref)

            for s_ref, r in zip(outs[n_tiles:], res[n_tiles:]):
                s_ref[...] += r

    blk = pl.BlockSpec((tm, N), lambda m: (m, 0))
    row = pl.BlockSpec((1, N), lambda m: (0, 0))
    return pl.pallas_call(
        body, name=name, grid=(M // tm,),
        in_specs=[a_spec, pl.BlockSpec((G, R, C), lambda m: (0, 0, 0))]
        + [pl.BlockSpec((8, LANES), lambda m: (0, 0))] * len(anchors) + [blk] * n_ex
        + [pl.BlockSpec((None, 1, N), lambda m, i=i: (i, 0, 0)) for _, i in vecs],
        out_specs=[blk] * len(out_dtypes) + [row] * n_sums,
        out_shape=[jax.ShapeDtypeStruct((M, N), dt) for dt in out_dtypes] + [jax.ShapeDtypeStruct((1, N), F32)] * n_sums,
        compiler_params=_cp("arbitrary" if n_sums else "parallel"),
    )(a, w, *anchors, *extras, *[v for v, _ in vecs])


def _wgrad(name, a, b, shape, *, a_group, tm, anchor=None):
    G, R, C = shape
    M = a.shape[0]
    tm = _tile(M, tm)
    n_m = M // tm
    anchors = [] if anchor is None else [anchor]

    def body(a_ref, b_ref, *rest):
        gb_ref, gf_ref = rest[len(anchors):]
        m = pl.program_id(1)
        part = _dot(a_ref[...].astype(BF16), b_ref[...].astype(BF16), 0, 0)

        @pl.when(m == 0)
        def _():
            gf_ref[...] = part

        @pl.when(m > 0)
        def _():
            gf_ref[...] += part

        @pl.when(m == n_m - 1)
        def _():
            gb_ref[...] = gf_ref[...].astype(BF16)

    a_spec = pl.BlockSpec((tm, R), (lambda g, m: (m, g)) if a_group else (lambda g, m: (m, 0)))
    if b.ndim == 3:
        assert not a_group
        b_spec = pl.BlockSpec((None, tm, C), lambda g, m: (g, m, 0))
    else:
        b_spec = pl.BlockSpec((tm, C), (lambda g, m: (m, 0)) if a_group else (lambda g, m: (m, g)))
    o_spec = pl.BlockSpec((None, R, C), lambda g, m: (g, 0, 0))
    return pl.pallas_call(
        body, name=name, grid=(G, n_m),
        in_specs=[a_spec, b_spec] + [pl.BlockSpec((8, LANES), lambda g, m: (0, 0))] * len(anchors), out_specs=o_spec,
        out_shape=jax.ShapeDtypeStruct(shape, BF16), scratch_shapes=[pltpu.VMEM((R, C), F32)],
        compiler_params=_cp("parallel", "arbitrary"),
    )(a, b, *anchors)


def _rms_fwd(name, x, g3, layer):
    S, D = x.shape
    tm = _tile(S, 512)

    def body(x_ref, g_ref, h_ref):
        xv = x_ref[...]
        r = lax.rsqrt(jnp.mean(xv * xv, axis=-1, keepdims=True) + EPS)
        h_ref[...] = (xv * r * g_ref[...]).astype(BF16)

    row = pl.BlockSpec((tm, D), lambda m: (m, 0))
    return pl.pallas_call(
        body, name=name, grid=(S // tm,),
        in_specs=[row, pl.BlockSpec((None, 1, D), lambda m: (layer, 0, 0))], out_specs=row,
        out_shape=jax.ShapeDtypeStruct((S, D), BF16), compiler_params=_cp("parallel"),
    )(x, g3)


def _adamw_math(w, m, v, g):
    m2 = ADAM_B1 * m + (1.0 - ADAM_B1) * g
    v2 = ADAM_B2 * v + (1.0 - ADAM_B2) * jnp.square(g)
    m_hat = m2 / (1.0 - ADAM_B1 ** ADAM_STEP)
    v_hat = v2 / (1.0 - ADAM_B2 ** ADAM_STEP)
    return g, -ADAM_LR * (m_hat / (jnp.sqrt(v_hat) + ADAM_EPS) + ADAM_WD * w), m2, v2


def _row_tile(rows):
    return _tile(rows, ELEMENTWISE_ROWS) if rows % ELEMENTWISE_ROWS == 0 else rows


def _adamw(name, w, m, v, g):
    rows, C = w.shape
    tr = _row_tile(rows)

    def body(w_ref, m_ref, v_ref, g_in, g_ref, d_ref, nm_ref, nv_ref):
        for o_ref, val in zip((g_ref, d_ref, nm_ref, nv_ref), _adamw_math(w_ref[...], m_ref[...], v_ref[...], g_in[...])):
            o_ref[...] = val

    blk = pl.BlockSpec((tr, C), lambda i: (i, 0))
    return pl.pallas_call(
        body, name=name, grid=(rows // tr,), in_specs=[blk] * 4, out_specs=[blk] * 4,
        out_shape=[jax.ShapeDtypeStruct((rows, C), F32)] * 4, compiler_params=_cp("parallel"),
    )(w, m, v, g)


def _sum_order():
    x, y, c = lax.axis_index("x"), lax.axis_index("y"), lax.axis_index("c")
    differs = lambda bit, coord: bit + coord - 2 * bit * coord
    slots = [4 * differs(p >> 2 & 1, x) + 2 * differs(p >> 1 & 1, y) + differs(p & 1, c) - 1 for p in range(8)]
    return [jnp.asarray(s, jnp.int32).reshape(1) for s in [2 * x + y, 4 * x + 2 * y + c] + slots]


def _adamw_layer(name, w, m, v, layer, gb, recv, order, outs):
    _, R, C = w.shape
    tr = _row_tile(R)
    n_s = len(order)

    def body(*refs):
        me = refs[1][0]
        w_ref, m_ref, v_ref, own_ref = refs[n_s:n_s + 4]
        theirs, outs_ = refs[n_s + 4:n_s + 12], refs[n_s + 16:]
        g = None
        for p in range(8):
            term = jnp.where(me == p, own_ref[...], theirs[p][...]).astype(F32)
            g = term if g is None else g + term
        for o_ref, val in zip(outs_, _adamw_math(w_ref[...], m_ref[...], v_ref[...], g)):
            o_ref[...] = val

    st = pl.BlockSpec((None, tr, C), lambda i, *s: (layer, i, 0))
    slot = lambda p: pl.BlockSpec((None, tr, C), lambda i, *s: (jnp.maximum(s[2 + p][0], 0), i, 0))
    return pl.pallas_call(
        body, name=name,
        grid_spec=pltpu.PrefetchScalarGridSpec(
            num_scalar_prefetch=n_s, grid=(R // tr,),
            in_specs=[st] * 3 + [pl.BlockSpec((None, tr, C), lambda i, *s: (s[0][0], i, 0))] + [slot(p) for p in range(8)]
            + [ANY] * 4,
            out_specs=[st] * 4),
        out_shape=[jax.ShapeDtypeStruct(w.shape, F32)] * 4, input_output_aliases={n_s + 12 + j: j for j in range(4)},
        compiler_params=_cp("parallel"),
    )(*order, w, m, v, gb, *([recv] * 8), *outs)


def _gelu(x):
    return x * (0.5 * (1.0 + jnp.tanh(0.7978845608028654 * (x + 0.044715 * (x * x * x)))))


def _layernorm(t, g, b):
    mu = jnp.mean(t, axis=-1, keepdims=True)
    var = jnp.mean(jnp.square(t - mu), axis=-1, keepdims=True)
    return (t - mu) * lax.rsqrt(var + EPS) * g + b


def _silu(x):
    return x * jax.nn.sigmoid(x)


def _a_value(zv, g, b):
    return _layernorm(_gelu(zv), g, b)


def _b_tail(gc, g, b):
    return _silu(_layernorm(gc, g, b))


def _first_head(shape):
    return lax.broadcasted_iota(jnp.int32, shape, len(shape) - 1) < HEAD_DIM


def _spatial_mix(spw_ref, vb, tm):
    first = _first_head((CHUNK, LANES))
    rows = []
    for n in range(tm // CHUNK):
        blocks = []
        for j in range(A_GROUPS // 2):
            vblk = vb[n * CHUNK:(n + 1) * CHUNK, j * LANES:(j + 1) * LANES]
            r0 = _dot(spw_ref[2 * j], vblk, 1, 0)
            r1 = _dot(spw_ref[2 * j + 1], vblk, 1, 0)
            blocks.append(jnp.where(first, r0, r1))
        rows.append(jnp.concatenate(blocks, axis=1))
    return jnp.concatenate(rows, axis=0) if len(rows) > 1 else rows[0]


def _ab_tail_out_proj(name, z, gconv, spw, bias_full, vn_g, vn_b, cn_g, cn_b, layer, w, x, g3, g_layer):
    S = z.shape[0]
    AW = 512
    tm = _tile(S, 256)

    def body(zu_ref, zv_ref, gc_ref, spw_ref, bias_ref, vg_ref, vb_ref, cg_ref, cb_ref, w_ref, x_ref, g_ref,
             xo_ref, h_ref, cat_ref):
        u = _gelu(zu_ref[...])
        v = _a_value(zv_ref[...], vg_ref[...], vb_ref[...])
        sv = _spatial_mix(spw_ref, v.astype(BF16), tm) + jnp.tile(bias_ref[...], (tm // CHUNK, 1))
        cat = jnp.concatenate([(u * sv).astype(BF16), _b_tail(gc_ref[...], cg_ref[...], cb_ref[...]).astype(BF16)], axis=1)
        cat_ref[...] = cat
        y, h = _add_norm_epilogue(_dot(cat, w_ref[0], 1, 0), x_ref[...], g_ref[...])
        xo_ref[...] = y
        h_ref[...] = h.astype(BF16)

    vec = pl.BlockSpec((None, 1, AW), lambda m: (layer, 0, 0))
    row = pl.BlockSpec((tm, 2 * AW), lambda m: (m, 0))
    return pl.pallas_call(
        body, name=name, grid=(S // tm,),
        in_specs=[pl.BlockSpec((tm, AW), lambda m: (m, 0)), pl.BlockSpec((tm, AW), lambda m: (m, 1)),
                  pl.BlockSpec((tm, AW), lambda m: (m, 0)),
                  pl.BlockSpec((None, A_GROUPS, CHUNK, CHUNK), lambda m: (layer, 0, 0, 0)),
                  pl.BlockSpec((None, CHUNK, AW), lambda m: (layer, 0, 0)), vec, vec, vec, vec,
                  pl.BlockSpec(w.shape, lambda m: (0, 0, 0)), row, pl.BlockSpec((None, 1, 2 * AW), lambda m: (g_layer, 0, 0))],
        out_specs=[row] * 3,
        out_shape=[jax.ShapeDtypeStruct((S, 2 * AW), dt) for dt in (F32, BF16, BF16)], compiler_params=_cp("parallel"),
    )(z, z, gconv, spw, bias_full, vn_g, vn_b, cn_g, cn_b, w, x, g3)


def _ab_tail_bwd(name, z, gconv, dcat, spw, spw_t, bias_full, vn_g, vn_b, cn_g, cn_b, layer):
    S = z.shape[0]
    AW = 512
    tm = _tile(S, 256)
    n_chunks = tm // CHUNK

    def body(zu_ref, zv_ref, gc_ref, dcat_ref, spw_ref, spwt_ref, bias_ref, vg_ref, vb_ref, cg_ref, cb_ref,
             dz_ref, dgc_ref, dspw_ref, dbias_ref, dvg_ref, dvb_ref, dcg_ref, dcb_ref):
        @pl.when(pl.program_id(0) == 0)
        def _():
            for r in (dspw_ref, dbias_ref, dvg_ref, dvb_ref, dcg_ref, dcb_ref):
                r[...] = jnp.zeros_like(r)

        dya = dcat_ref[:, :AW]
        dyb = dcat_ref[:, AW:]
        u, u_vjp = jax.vjp(_gelu, zu_ref[...])
        v, v_vjp = jax.vjp(_a_value, zv_ref[...], vg_ref[...], vb_ref[...])
        vb16 = v.astype(BF16)
        sv = _spatial_mix(spw_ref, vb16, tm) + jnp.tile(bias_ref[...], (n_chunks, 1))
        (dzu,) = u_vjp(dya * sv)
        dsv = dya * u
        dsv16 = dsv.astype(BF16)
        dv = _spatial_mix(spwt_ref, dsv16, tm)
        dzv, dvg, dvb = v_vjp(dv)
        dz_ref[0] = dzu
        dz_ref[1] = dzv
        dvg_ref[...] += dvg
        dvb_ref[...] += dvb

        first = _first_head((CHUNK, LANES))
        zero = jnp.zeros((), BF16)
        dbias = jnp.zeros((CHUNK, AW), F32)
        for n in range(n_chunks):
            rows = slice(n * CHUNK, (n + 1) * CHUNK)
            dbias = dbias + dsv[rows]
            for j in range(A_GROUPS // 2):
                cols = slice(j * LANES, (j + 1) * LANES)
                dblk, vblk = dsv16[rows, cols], vb16[rows, cols]
                dspw_ref[2 * j] += _dot(jnp.where(first, dblk, zero), vblk, 1, 1)
                dspw_ref[2 * j + 1] += _dot(jnp.where(first, zero, dblk), vblk, 1, 1)
        dbias_ref[...] += dbias

        _, t_vjp = jax.vjp(_b_tail, gc_ref[...], cg_ref[...], cb_ref[...])
        dgc, dcg, dcb = t_vjp(dyb)
        dgc_ref[...] = dgc
        dcg_ref[...] += dcg
        dcb_ref[...] += dcb

    vec = pl.BlockSpec((None, 1, AW), lambda m: (layer, 0, 0))
    spw_spec = pl.BlockSpec((None, A_GROUPS, CHUNK, CHUNK), lambda m: (layer, 0, 0, 0))
    ovec = pl.BlockSpec((1, AW), lambda m: (0, 0))
    return pl.pallas_call(
        body, name=name, grid=(S // tm,),
        in_specs=[pl.BlockSpec((tm, AW), lambda m: (m, 0)), pl.BlockSpec((tm, AW), lambda m: (m, 1)),
                  pl.BlockSpec((tm, AW), lambda m: (m, 0)), pl.BlockSpec((tm, 2 * AW), lambda m: (m, 0)),
                  spw_spec, spw_spec, pl.BlockSpec((None, CHUNK, AW), lambda m: (layer, 0, 0)), vec, vec, vec, vec],
        out_specs=[pl.BlockSpec((2, tm, AW), lambda m: (0, m, 0)), pl.BlockSpec((tm, AW), lambda m: (m, 0)),
                   pl.BlockSpec((A_GROUPS, CHUNK, CHUNK), lambda m: (0, 0, 0)),
                   pl.BlockSpec((CHUNK, AW), lambda m: (0, 0)), ovec, ovec, ovec, ovec],
        out_shape=[jax.ShapeDtypeStruct((4, S, AW), F32), jax.ShapeDtypeStruct((S, AW), F32),
                   jax.ShapeDtypeStruct((A_GROUPS, CHUNK, CHUNK), F32), jax.ShapeDtypeStruct((CHUNK, AW), F32)]
                  + [jax.ShapeDtypeStruct((1, AW), F32)] * 4,
        compiler_params=_cp("arbitrary"),
    )(z, z, gconv, dcat, spw, spw_t, bias_full, vn_g, vn_b, cn_g, cn_b)


def _fold_bias(dbias_full):
    def body(d_ref, o_ref):
        d = d_ref[...]
        hi = d.astype(BF16)
        lo = (d - hi.astype(F32)).astype(BF16)
        r = lax.broadcasted_iota(jnp.int32, (512, LANES), 0)
        c = lax.broadcasted_iota(jnp.int32, (512, LANES), 1)
        fold = jnp.where(lax.shift_right_logical(r, 6) == c, 1.0, 0.0).astype(BF16)
        o_ref[...] = _dot(hi, fold, 1, 0) + _dot(lo, fold, 1, 0)

    return pl.pallas_call(body, name="fold_spatial_bias", out_shape=jax.ShapeDtypeStruct((CHUNK, LANES), F32))(dbias_full)


def _halo_specs(tm, n_halo_blocks, col):
    r = tm // CONV_HALO
    prev = pl.BlockSpec((CONV_HALO, LANES), lambda j, i: (jnp.maximum(i * r - 1, 0), col + j))
    cur = pl.BlockSpec((tm, LANES), lambda j, i: (i, col + j))
    nxt = pl.BlockSpec((CONV_HALO, LANES), lambda j, i: (jnp.minimum((i + 1) * r, n_halo_blocks - 1), col + j))
    return [prev, cur, nxt]


def _fill_halo(scr, prev, cur, nxt, tm, i, n_i):
    scr[0:CONV_HALO, :] = jnp.where(i > 0, prev, 0.0)
    scr[CONV_HALO:CONV_HALO + tm, :] = cur
    scr[CONV_HALO + tm:2 * CONV_HALO + tm, :] = jnp.where(i < n_i - 1, nxt, 0.0)


def _glu_conv_fwd(name, z, cw, cb3, layer):
    S = z.shape[0]
    tm = _tile(S, 512)
    n_i = S // tm
    pad = CONV_WIDTH // 2

    def body(vp, vc, vn, gp, gc, gn, w_ref, b_ref, out_ref, scr):
        i = pl.program_id(1)
        glu = lambda a, b: a[...] * jax.nn.sigmoid(b[...])
        _fill_halo(scr, glu(vp, gp), glu(vc, gc), glu(vn, gn), tm, i, n_i)
        taps = [w_ref[j:j + 1, :] for j in range(CONV_WIDTH)]
        for c0 in range(0, tm, CONV_CHUNK):
            acc = jnp.zeros((CONV_CHUNK, LANES), F32) + b_ref[...]
            for j in range(CONV_WIDTH):
                acc = acc + taps[j] * scr[pl.ds(c0 + CONV_HALO - pad + j, CONV_CHUNK), :]
            out_ref[pl.ds(c0, CONV_CHUNK), :] = acc

    return pl.pallas_call(
        body, name=name, grid=(4, n_i),
        in_specs=_halo_specs(tm, S // CONV_HALO, 8) + _halo_specs(tm, S // CONV_HALO, 12)
        + [pl.BlockSpec((None, CONV_WIDTH, LANES), lambda j, i: (j, 0, 0)),
           pl.BlockSpec((None, 1, LANES), lambda j, i: (layer, 0, j))],
        out_specs=pl.BlockSpec((tm, LANES), lambda j, i: (i, j)),
        out_shape=jax.ShapeDtypeStruct((S, 4 * LANES), F32),
        scratch_shapes=[pltpu.VMEM((tm + 2 * CONV_HALO, LANES), F32)],
        compiler_params=_cp("parallel", "parallel"),
    )(z, z, z, z, z, z, cw, cb3)


def _glu_conv_bwd(name, z, dgconv, dz, cw):
    S = z.shape[0]
    tm = _tile(S, 512)
    n_i = S // tm
    pad = CONV_WIDTH // 2

    def body(vp, vc, vn, gp, gc, gn, dp, dc, dn, w_ref, dz_in, dz_ref, gb_ref, db_ref, g_scr, d_scr, gf_ref):
        i = pl.program_id(1)
        sig = jax.nn.sigmoid(gc[...])
        _fill_halo(g_scr, vp[...] * jax.nn.sigmoid(gp[...]), vc[...] * sig, vn[...] * jax.nn.sigmoid(gn[...]), tm, i, n_i)
        _fill_halo(d_scr, dp[...], dc[...], dn[...], tm, i, n_i)

        @pl.when(i == 0)
        def _():
            gf_ref[...] = jnp.zeros_like(gf_ref)
            db_ref[...] = jnp.zeros_like(db_ref)

        taps = [w_ref[j:j + 1, :] for j in range(CONV_WIDTH)]
        dw = [jnp.zeros((8, LANES), F32) for _ in range(CONV_WIDTH)]
        db = jnp.zeros((8, LANES), F32)
        fold8 = lambda t: jnp.sum(t.reshape(CONV_CHUNK // 8, 8, LANES), axis=0)
        for c0 in range(0, tm, CONV_CHUNK):
            rows = pl.ds(c0, CONV_CHUNK)
            d_cur = dc[rows, :]
            dglu = jnp.zeros((CONV_CHUNK, LANES), F32)
            for j in range(CONV_WIDTH):
                dglu = dglu + taps[j] * d_scr[pl.ds(c0 + CONV_HALO + pad - j, CONV_CHUNK), :]
                dw[j] = dw[j] + fold8(d_cur * g_scr[pl.ds(c0 + CONV_HALO - pad + j, CONV_CHUNK), :])
            db = db + fold8(d_cur)
            sig_c = jax.nn.sigmoid(gc[rows, :])
            dz_ref[0, rows, :] = dglu * sig_c
            dz_ref[1, rows, :] = dglu * vc[rows, :] * sig_c * (1.0 - sig_c)
        for j in range(CONV_WIDTH):
            gf_ref[j:j + 1, :] += jnp.sum(dw[j], axis=0, keepdims=True)
        db_ref[...] += jnp.sum(db, axis=0, keepdims=True)

        @pl.when(i == n_i - 1)
        def _():
            gb_ref[...] = gf_ref[...].astype(BF16)

    w_spec = pl.BlockSpec((None, CONV_WIDTH, LANES), lambda j, i: (j, 0, 0))
    return pl.pallas_call(
        body, name=name, grid=(4, n_i),
        in_specs=_halo_specs(tm, S // CONV_HALO, 8) + _halo_specs(tm, S // CONV_HALO, 12)
        + _halo_specs(tm, S // CONV_HALO, 0) + [w_spec, ANY],
        out_specs=[pl.BlockSpec((2, tm, LANES), lambda j, i: (1, i, j)),
                   w_spec, pl.BlockSpec((1, LANES), lambda j, i: (0, j))],
        out_shape=[jax.ShapeDtypeStruct(dz.shape, F32), jax.ShapeDtypeStruct(cw.shape, BF16),
                   jax.ShapeDtypeStruct((1, 4 * LANES), F32)],
        input_output_aliases={10: 0},
        scratch_shapes=[pltpu.VMEM((tm + 2 * CONV_HALO, LANES), F32)] * 2 + [pltpu.VMEM((CONV_WIDTH, LANES), F32)],
        compiler_params=_cp("parallel", "arbitrary"),
    )(z, z, z, z, z, z, dgconv, dgconv, dgconv, cw, dz)


def _seg_matrix(scale):
    r = lax.broadcasted_iota(jnp.int32, (LANES, LANES), 0)
    c = lax.broadcasted_iota(jnp.int32, (LANES, LANES), 1)
    return jnp.where(lax.shift_right_logical(r, 6) == lax.shift_right_logical(c, 6), scale, 0.0).astype(BF16)


def _seg_sum(x, seg):
    hi = x.astype(BF16)
    lo = (x - hi.astype(F32)).astype(BF16)
    return _dot(hi, seg, 1, 0) + _dot(lo, seg, 1, 0)


def _rope_tables(S):
    pos = jnp.arange(S, dtype=F32)
    inv_freq = ROPE_THETA ** (-jnp.arange(0, ROT_DIM, 2, dtype=F32) / ROT_DIM)
    ang = pos[:, None] * inv_freq[None, :]
    cos, sin = jnp.cos(ang), jnp.sin(ang)
    half = ROT_DIM // 2
    rest = HEAD_DIM - ROT_DIM
    one, zero = jnp.ones((S, rest), F32), jnp.zeros((S, rest), F32)
    zh = jnp.zeros((S, half), F32)
    c = jnp.concatenate([cos, cos, one], axis=1)
    sa = jnp.concatenate([-sin, zh, zero], axis=1)
    sb = jnp.concatenate([zh, sin, zero], axis=1)
    return [jnp.tile(t, (1, 2)) for t in (c, sa, sb)]


QK_CHUNK = 64


def _qk_fwd(name, qkv, gq, gk, tables):
    S = qkv.shape[0]
    W = N_HEADS * HEAD_DIM
    tm = _tile(S, 256)
    half = ROT_DIM // 2

    def body(q_ref, k_ref, gq_ref, gk_ref, c_ref, sa_ref, sb_ref, qn_ref, kn_ref):
        seg = _seg_matrix(1.0 / HEAD_DIM)
        for r0 in range(0, tm, QK_CHUNK):
            rows = pl.ds(r0, QK_CHUNK)
            c, sa, sb = c_ref[rows, :], sa_ref[rows, :], sb_ref[rows, :]
            for t_ref, g_ref, o_ref in ((q_ref, gq_ref, qn_ref), (k_ref, gk_ref, kn_ref)):
                for blk in range(W // LANES):
                    cols = slice(blk * LANES, (blk + 1) * LANES)
                    t = t_ref[rows, cols]
                    y = t * lax.rsqrt(_seg_sum(t * t, seg) + EPS) * g_ref[...]
                    o_ref[rows, cols] = y * c + pltpu.roll(y, LANES - half, 1) * sa + pltpu.roll(y, half, 1) * sb

    row = lambda k: pl.BlockSpec((tm, W), lambda m: (m, k))
    gain = pl.BlockSpec((1, LANES), lambda m: (0, 0))
    tab = pl.BlockSpec((tm, LANES), lambda m: (m, 0))
    return pl.pallas_call(
        body, name=name, grid=(S // tm,),
        in_specs=[row(0), row(1), gain, gain, tab, tab, tab], out_specs=[row(0)] * 2,
        out_shape=[jax.ShapeDtypeStruct((S, W), F32)] * 2, compiler_params=_cp("parallel"),
    )(qkv, qkv, gq, gk, *tables)


def _qk_bwd(name, qkv, gq, gk, tables, dqs, dks, dvs):
    S = qkv.shape[0]
    W = N_HEADS * HEAD_DIM
    tm = _tile(S, 256)
    half = ROT_DIM // 2
    n_p = len(dqs)

    def body(q_ref, k_ref, gq_ref, gk_ref, c_ref, sa_ref, sb_ref, *rest):
        dq_refs, dk_refs, dv_refs = rest[:n_p], rest[n_p:2 * n_p], rest[2 * n_p:3 * n_p]
        dqkv_ref, dgq_ref, dgk_ref = rest[3 * n_p:]

        @pl.when(pl.program_id(0) == 0)
        def _():
            dgq_ref[...] = jnp.zeros_like(dgq_ref)
            dgk_ref[...] = jnp.zeros_like(dgk_ref)

        seg = _seg_matrix(1.0 / HEAD_DIM)
        r_i = lax.broadcasted_iota(jnp.int32, (LANES, LANES), 0)
        c_i = lax.broadcasted_iota(jnp.int32, (LANES, LANES), 1)
        same_dim = jnp.where((r_i & (HEAD_DIM - 1)) == (c_i & (HEAD_DIM - 1)), 1.0, 0.0).astype(BF16)
        dgs = [jnp.zeros((8, LANES), F32), jnp.zeros((8, LANES), F32)]
        fold8 = lambda t: jnp.sum(t.reshape(QK_CHUNK // 8, 8, LANES), axis=0)
        for r0 in range(0, tm, QK_CHUNK):
            rows = pl.ds(r0, QK_CHUNK)
            c, sa, sb = c_ref[rows, :], sa_ref[rows, :], sb_ref[rows, :]
            for idx, (t_ref, g_ref, d_refs) in enumerate(((q_ref, gq_ref, dq_refs), (k_ref, gk_ref, dk_refs))):
                for blk in range(W // LANES):
                    cols = slice(blk * LANES, (blk + 1) * LANES)
                    dout = d_refs[0][rows, cols]
                    for r in d_refs[1:]:
                        dout = dout + r[rows, cols]
                    dy = dout * c + pltpu.roll(dout * sa, half, 1) + pltpu.roll(dout * sb, LANES - half, 1)
                    t = t_ref[rows, cols]
                    r_ = lax.rsqrt(_seg_sum(t * t, seg) + EPS)
                    xhat = t * r_
                    dgs[idx] = dgs[idx] + fold8(dy * xhat)
                    dxhat = dy * g_ref[...]
                    dt = r_ * (dxhat - xhat * _seg_sum(dxhat * xhat, seg))
                    dqkv_ref[rows, idx * W + blk * LANES: idx * W + (blk + 1) * LANES] = dt.astype(BF16)
            dv = dv_refs[0][rows, :]
            for r in dv_refs[1:]:
                dv = dv + r[rows, :]
            dqkv_ref[rows, 2 * W:] = dv.astype(BF16)
        for dg, dg_ref in zip(dgs, (dgq_ref, dgk_ref)):
            dg_ref[...] += jnp.sum(_seg_sum(dg, same_dim), axis=0, keepdims=True)

    row = lambda k: pl.BlockSpec((tm, W), lambda m: (m, k))
    gain = pl.BlockSpec((1, LANES), lambda m: (0, 0))
    tab = pl.BlockSpec((tm, LANES), lambda m: (m, 0))
    return pl.pallas_call(
        body, name=name, grid=(S // tm,),
        in_specs=[row(0), row(1), gain, gain, tab, tab, tab] + [row(0)] * (3 * n_p),
        out_specs=[pl.BlockSpec((tm, 3 * W), lambda m: (m, 0)), gain, gain],
        out_shape=[jax.ShapeDtypeStruct((S, 3 * W), BF16), jax.ShapeDtypeStruct((1, LANES), F32),
                   jax.ShapeDtypeStruct((1, LANES), F32)],
        compiler_params=_cp("arbitrary"),
    )(qkv, qkv, gq, gk, *tables, *dqs, *dks, *dvs)


ATTN_BQ = 2 * BAND
ATTN_ROWS = 16 * ATTN_BQ
V_COL = 2 * N_HEADS * HEAD_DIM // LANES


def _attn_geometry(S, d):
    rows = min(ATTN_ROWS, S)
    halo = BAND * d
    assert rows % (ATTN_BQ * d) == 0 and S % rows == 0, (S, d)
    return rows, halo, rows // (ATTN_BQ * d)


def _attn_specs(S, d, col):
    rows, halo, _ = _attn_geometry(S, d)
    r = rows // halo
    n_h = S // halo
    prev = pl.BlockSpec((halo, LANES), lambda j, i: (jnp.maximum(i * r - 1, 0), col + j))
    cur = pl.BlockSpec((rows, LANES), lambda j, i: (i, col + j))
    nxt = pl.BlockSpec((halo, LANES), lambda j, i: (jnp.minimum((i + 1) * r, n_h - 1), col + j))
    return [prev, cur, nxt]


def _fill_window(scr, prev, cur, nxt, rows, halo):
    scr[0:halo, :] = prev[...]
    scr[halo:halo + rows, :] = cur[...]
    scr[halo + rows:2 * halo + rows, :] = nxt[...]


def _chain_groups(n_sb, d, size):
    chains = [(sb, r) for sb in range(n_sb) for r in range(d)]
    return [chains[j:j + size] for j in range(0, len(chains), size)]


def _strided(ref, start, size, d):
    return ref[pl.ds(start, size, stride=d) if d > 1 else pl.ds(start, size), :]


def _band_mask(i, S, d, sb):
    rows, _, _ = _attn_geometry(S, d)
    L = S // d
    base = i * (rows // d) + sb * ATTN_BQ
    wk = ATTN_BQ + 2 * BAND
    row = lax.broadcasted_iota(jnp.int32, (ATTN_BQ, wk), 0)
    col = lax.broadcasted_iota(jnp.int32, (ATTN_BQ, wk), 1)
    lj = base - BAND + col
    return (jnp.abs(col - BAND - row) <= BAND) & (lj >= 0) & (lj < L)


def _attn_fwd(name, q, k, v, v_col, d, acc=None):
    S, W = q.shape
    rows, halo, n_sb = _attn_geometry(S, d)
    wk = ATTN_BQ + 2 * BAND
    scale = HEAD_DIM ** -0.5
    n_acc = 0 if acc is None else 2

    def body(q_ref, kp, kc, kn, vp, vc, vn, *rest):
        acc_refs, (o_ref, lse_ref, kw, vw) = rest[:n_acc], rest[n_acc:]
        i = pl.program_id(1)
        _fill_window(kw, kp, kc, kn, rows, halo)
        _fill_window(vw, vp, vc, vn, rows, halo)
        first = _first_head((ATTN_BQ, LANES))
        heads = (first, jnp.logical_not(first))
        zero = jnp.zeros((), BF16)
        for group in _chain_groups(n_sb, d, 4):
            masks = {sb: _band_mask(i, S, d, sb) for sb in sorted({sb for sb, _ in group})}
            starts = [r + d * sb * ATTN_BQ for sb, r in group]
            qs = [_strided(q_ref, st, ATTN_BQ, d).astype(BF16) for st in starts]
            ks = [_strided(kw, st, wk, d).astype(BF16) for st in starts]
            vs = [_strided(vw, st, wk, d).astype(BF16) for st in starts]
            s_all = [[_dot(jnp.where(hm, qv, zero), kv, 1, 1) for hm in heads] for qv, kv in zip(qs, ks)]
            p_all, den_all, lse_all = [], [], []
            for (sb, _), s_h in zip(group, s_all):
                s_h = [jnp.where(masks[sb], s * scale, NEG) for s in s_h]
                mx_h = [jnp.max(s, axis=-1, keepdims=True) for s in s_h]
                p_h = [jnp.exp(s - mx) for s, mx in zip(s_h, mx_h)]
                den_h = [jnp.sum(p, axis=-1, keepdims=True) for p in p_h]
                p_all.append([p.astype(BF16) for p in p_h])
                den_all.append(den_h)
                lse_all.append([mx + jnp.log(den) for mx, den in zip(mx_h, den_h)])
            o_all = [[_dot(p, vv, 1, 0) for p in p_h] for p_h, vv in zip(p_all, vs)]
            for st, o_h, den_h, lse_h in zip(starts, o_all, den_all, lse_all):
                dst = pl.ds(st, ATTN_BQ, stride=d) if d > 1 else pl.ds(st, ATTN_BQ)
                o_new = jnp.where(first, o_h[0] / den_h[0], o_h[1] / den_h[1])
                lse_new = jnp.where(first, lse_h[0], lse_h[1])
                if acc is not None:
                    o_old, lse_old = (_strided(r, st, ATTN_BQ, d) for r in acc_refs)
                    top = jnp.maximum(lse_old, lse_new)
                    e_old, e_new = jnp.exp(lse_old - top), jnp.exp(lse_new - top)
                    o_new = (e_old * o_old + e_new * o_new) / (e_old + e_new)
                    lse_new = top + jnp.log(e_old + e_new)
                o_ref[dst, :] = o_new
                lse_ref[dst, :] = lse_new

    cur = _attn_specs(S, d, 0)[1]
    return pl.pallas_call(
        body, name=name, grid=(W // LANES, S // rows),
        in_specs=[cur] + _attn_specs(S, d, 0) + _attn_specs(S, d, v_col) + [cur] * n_acc, out_specs=[cur, cur],
        out_shape=[jax.ShapeDtypeStruct((S, W), F32)] * 2,
        scratch_shapes=[pltpu.VMEM((rows + 2 * halo, LANES), F32)] * 2,
        compiler_params=_cp("parallel", "parallel"),
    )(q, k, k, k, v, v, v, *(acc or ()))


def _delta_epilogue(do, o):
    seg = _seg_matrix(1.0)
    prod = do * o.astype(F32)
    delta = [_seg_sum(prod[:, blk * LANES:(blk + 1) * LANES], seg) for blk in range(do.shape[1] // LANES)]
    return do, jnp.concatenate(delta, axis=1)


def _attn_bwd_q(name, q, k, v, v_col, do, lse, delta, d):
    S, W = q.shape
    rows, halo, n_sb = _attn_geometry(S, d)
    wk = ATTN_BQ + 2 * BAND
    scale = HEAD_DIM ** -0.5

    def body(q_ref, do_ref, l_ref, dl_ref, kp, kc, kn, vp, vc, vn, dq_ref, kw, vw):
        i = pl.program_id(1)
        _fill_window(kw, kp, kc, kn, rows, halo)
        _fill_window(vw, vp, vc, vn, rows, halo)
        first = _first_head((ATTN_BQ, LANES))
        heads = (first, jnp.logical_not(first))
        zero = jnp.zeros((), BF16)
        wide = lambda t: jnp.concatenate([t] * (wk // LANES), axis=1)
        for group in _chain_groups(n_sb, d, 4):
            masks = {sb: _band_mask(i, S, d, sb) for sb in sorted({sb for sb, _ in group})}
            starts = [r + d * sb * ATTN_BQ for sb, r in group]
            qs = [_strided(q_ref, st, ATTN_BQ, d).astype(BF16) for st in starts]
            dos = [_strided(do_ref, st, ATTN_BQ, d).astype(BF16) for st in starts]
            ks = [_strided(kw, st, wk, d).astype(BF16) for st in starts]
            vs = [_strided(vw, st, wk, d).astype(BF16) for st in starts]
            s_all = [[_dot(jnp.where(hm, qv, zero), kv, 1, 1) for hm in heads] for qv, kv in zip(qs, ks)]
            dp_all = [[_dot(jnp.where(hm, dov, zero), vv, 1, 1) for hm in heads] for dov, vv in zip(dos, vs)]
            ds_all = []
            for (sb, _), st, s_h, dp_h in zip(group, starts, s_all, dp_all):
                lv, dlv = _strided(l_ref, st, ATTN_BQ, d), _strided(dl_ref, st, ATTN_BQ, d)
                l_sw, dl_sw = pltpu.roll(lv, HEAD_DIM, 1), pltpu.roll(dlv, HEAD_DIM, 1)
                ds_h = []
                for hm, s, dp in zip(heads, s_h, dp_h):
                    p = jnp.exp(jnp.where(masks[sb], s * scale, NEG) - wide(jnp.where(hm, lv, l_sw)))
                    ds_h.append((p * (dp - wide(jnp.where(hm, dlv, dl_sw))) * scale).astype(BF16))
                ds_all.append(ds_h)
            dq_all = [[_dot(ds, kv, 1, 0) for ds in ds_h] for ds_h, kv in zip(ds_all, ks)]
            for st, dq_h in zip(starts, dq_all):
                dst = pl.ds(st, ATTN_BQ, stride=d) if d > 1 else pl.ds(st, ATTN_BQ)
                dq_ref[dst, :] = jnp.where(first, dq_h[0], dq_h[1])

    cur = _attn_specs(S, d, 0)[1]
    return pl.pallas_call(
        body, name=name, grid=(W // LANES, S // rows),
        in_specs=[cur] * 4 + _attn_specs(S, d, 0) + _attn_specs(S, d, v_col), out_specs=cur,
        out_shape=jax.ShapeDtypeStruct((S, W), F32),
        scratch_shapes=[pltpu.VMEM((rows + 2 * halo, LANES), F32)] * 2,
        compiler_params=_cp("parallel", "parallel"),
    )(q, do, lse, delta, k, k, k, v, v, v)


def _attn_bwd_kv(name, q, k, v, v_col, do, lse, delta, d):
    S, W = q.shape
    rows, halo, n_sb = _attn_geometry(S, d)
    wk = ATTN_BQ + 2 * BAND
    scale = HEAD_DIM ** -0.5

    def body(k_ref, v_ref, qp, qc, qn, dop, doc, don, lp, lc, ln, dlp, dlc, dln, dk_ref, dv_ref, qw, dow, lw, dlw):
        i = pl.program_id(1)
        _fill_window(qw, qp, qc, qn, rows, halo)
        _fill_window(dow, dop, doc, don, rows, halo)
        _fill_window(lw, lp, lc, ln, rows, halo)
        _fill_window(dlw, dlp, dlc, dln, rows, halo)
        first = _first_head((ATTN_BQ, LANES))
        heads = (first, jnp.logical_not(first))
        zero = jnp.zeros((), BF16)
        for group in _chain_groups(n_sb, d, 2):
            masks = {sb: _band_mask(i, S, d, sb) for sb in sorted({sb for sb, _ in group})}
            starts = [r + d * sb * ATTN_BQ for sb, r in group]
            ks = [_strided(k_ref, st, ATTN_BQ, d).astype(BF16) for st in starts]
            vs = [_strided(v_ref, st, ATTN_BQ, d).astype(BF16) for st in starts]
            qs = [_strided(qw, st, wk, d).astype(BF16) for st in starts]
            dos = [_strided(dow, st, wk, d).astype(BF16) for st in starts]
            s_all = [[_dot(jnp.where(hm, kv, zero), qv, 1, 1) for hm in heads] for kv, qv in zip(ks, qs)]
            dp_all = [[_dot(jnp.where(hm, vv, zero), dov, 1, 1) for hm in heads] for vv, dov in zip(vs, dos)]
            p_all, ds_all = [], []
            for (sb, _), st, s_h, dp_h in zip(group, starts, s_all, dp_all):
                l_t, dl_t = _strided(lw, st, wk, d).T, _strided(dlw, st, wk, d).T
                p_h = [jnp.exp(jnp.where(masks[sb], s * scale, NEG) - l_t[hh * HEAD_DIM:hh * HEAD_DIM + 1, :])
                       for hh, s in enumerate(s_h)]
                ds_all.append([(p * (dp - dl_t[hh * HEAD_DIM:hh * HEAD_DIM + 1, :]) * scale).astype(BF16)
                               for hh, (p, dp) in enumerate(zip(p_h, dp_h))])
                p_all.append([p.astype(BF16) for p in p_h])
            dv_all = [[_dot(p, dov, 1, 0) for p in p_h] for p_h, dov in zip(p_all, dos)]
            dk_all = [[_dot(ds, qv, 1, 0) for ds in ds_h] for ds_h, qv in zip(ds_all, qs)]
            for st, dk_h, dv_h in zip(starts, dk_all, dv_all):
                dst = pl.ds(st, ATTN_BQ, stride=d) if d > 1 else pl.ds(st, ATTN_BQ)
                dk_ref[dst, :] = jnp.where(first, dk_h[0], dk_h[1])
                dv_ref[dst, :] = jnp.where(first, dv_h[0], dv_h[1])

    cur = _attn_specs(S, d, 0)[1]
    win = _attn_specs(S, d, 0)
    return pl.pallas_call(
        body, name=name, grid=(W // LANES, S // rows),
        in_specs=[cur, _attn_specs(S, d, v_col)[1]] + win * 4, out_specs=[cur, cur],
        out_shape=[jax.ShapeDtypeStruct((S, W), F32)] * 2,
        scratch_shapes=[pltpu.VMEM((rows + 2 * halo, LANES), F32)] * 4,
        compiler_params=_cp("parallel", "parallel"),
    )(k, v, q, q, q, do, do, do, lse, lse, lse, delta, delta, delta)


def _place():
    x, y, c = lax.axis_index("x"), lax.axis_index("y"), lax.axis_index("c")
    chips = [(1 - x, y), (x, 1 - y), (1 - x, 1 - y)]
    return x, y, c, chips


HBM = pl.BlockSpec(memory_space=pltpu.HBM)
SEM = pl.BlockSpec(memory_space=pltpu.SEMAPHORE)
DATAFLOW = pltpu.SideEffectType.DATAFLOW_SIDE_EFFECTING


N_PEERS = {"gather": 3, "scatter": 7, "allgather": 7}


def _exchange_copies(kind, srcs, dsts, send_sems, recv_sems):
    x, y, c, chips = _place()
    mine = 2 * x + y
    n_peers = N_PEERS[kind]
    cps = []
    for t in range(len(srcs)):
        for k in range(n_peers):
            if kind == "gather":
                (px, py), pc = chips[k], c
                src, dst = srcs[t], dsts[t].at[mine]
            else:
                bits = k + 1
                px, py, pc = (1 - x if bits & 4 else x), (1 - y if bits & 2 else y), (1 - c if bits & 1 else c)
                src, dst = (srcs[t].at[2 * px + py] if kind == "scatter" else srcs[t]), dsts[t].at[k]
            cps.append(pltpu.make_async_remote_copy(
                src_ref=src, dst_ref=dst, send_sem=send_sems.at[n_peers * t + k], recv_sem=recv_sems.at[n_peers * t + k],
                device_id=(px, py, pc), device_id_type=MESH))
    return cps


def _exchange_start(name, kind, groups):
    sizes = [len(g) for g in groups]
    n, n_g = sum(sizes), len(groups)

    def body(*refs):
        srcs, dsts = refs[:n], refs[n:2 * n]
        sems = refs[2 * n:2 * n + 2 * n_g]
        token = refs[4 * n + 2 * n_g]
        off = 0
        for gi, size in enumerate(sizes):
            for cp in _exchange_copies(kind, srcs[off:off + size], dsts[off:off + size], sems[2 * gi], sems[2 * gi + 1]):
                cp.start()
            off += size
        token[...] = jnp.zeros_like(token)

    arrays = [pltpu.with_memory_space_constraint(a, pltpu.HBM) for a in
              [s for g in groups for s, _ in g] + [d for g in groups for _, d in g]]
    sem_shapes = []
    for size in sizes:
        sem_shapes += [pltpu.SemaphoreType.DMA((N_PEERS[kind] * size,))] * 2
    outs = pl.pallas_call(
        body, name=name,
        in_specs=[HBM] * (2 * n),
        out_specs=[SEM] * (2 * n_g) + [HBM] * (2 * n) + [pl.BlockSpec(memory_space=pltpu.VMEM)],
        out_shape=sem_shapes + [pltpu.HBM(a.shape, a.dtype) for a in arrays] + [jax.ShapeDtypeStruct((8, LANES), F32)],
        input_output_aliases={t: 2 * n_g + t for t in range(2 * n)},
        compiler_params=pltpu.CompilerParams(has_side_effects=DATAFLOW),
    )(*arrays)
    sems, thru, token = outs[:2 * n_g], outs[2 * n_g:-1], outs[-1]
    handles, off = [], 0
    for gi, size in enumerate(sizes):
        handles.append((sems[2 * gi], sems[2 * gi + 1], thru[off:off + size], thru[n + off:n + off + size]))
        off += size
    return handles, token


def _exchange_wait(name, kind, handle, after):
    send_sems, recv_sems, srcs, dsts = handle
    n = len(srcs)

    def body(*refs):
        for cp in _exchange_copies(kind, refs[:n], refs[n:2 * n], refs[2 * n], refs[2 * n + 1]):
            cp.wait_send()
            cp.wait_recv()

    outs = pl.pallas_call(
        body, name=name,
        in_specs=[HBM] * (2 * n) + [SEM, SEM, ANY], out_specs=[HBM] * (2 * n),
        out_shape=[pltpu.HBM(a.shape, a.dtype) for a in (*srcs, *dsts)],
        input_output_aliases={t: t for t in range(2 * n)},
        compiler_params=pltpu.CompilerParams(has_side_effects=DATAFLOW),
    )(*srcs, *dsts, send_sems, recv_sems, after)
    return outs[:n], outs[n:]


def _prepare_shard(name, w, idx, dtype, mine, anchor=None):
    _, R, C = w.shape
    tr = _row_tile(R)
    anchors = [] if anchor is None else [anchor]

    def body(mine_ref, w_ref, *rest):
        src_ref, land_ref = rest[len(anchors):]
        val = w_ref[...].astype(dtype)
        src_ref[...] = val
        land_ref[...] = val

    return pl.pallas_call(
        body, name=name,
        grid_spec=pltpu.PrefetchScalarGridSpec(
            num_scalar_prefetch=1, grid=(R // tr,),
            in_specs=[pl.BlockSpec((None, tr, C), lambda i, s: (idx, i, 0))]
            + [pl.BlockSpec((8, LANES), lambda i, s: (0, 0))] * len(anchors),
            out_specs=[pl.BlockSpec((tr, C), lambda i, s: (i, 0)), pl.BlockSpec((None, tr, C), lambda i, s: (s[0], i, 0))]),
        out_shape=[jax.ShapeDtypeStruct((R, C), dtype), jax.ShapeDtypeStruct((N_SHARDS, R, C), dtype)],
        compiler_params=_cp("parallel"),
    )(mine, w, *anchors)


def _ordered_sum(name, own, recv, order):
    rows, C = own.shape
    tr = _row_tile(rows)
    n_s = len(order)

    def body(*refs):
        me = refs[1][0]
        own_ref, theirs, out_ref = refs[n_s], refs[n_s + 1:n_s + 9], refs[n_s + 9]
        g = None
        for p in range(8):
            term = jnp.where(me == p, own_ref[...], theirs[p][...])
            g = term if g is None else g + term
        out_ref[...] = g

    blk = pl.BlockSpec((tr, C), lambda i, *s: (i, 0))
    slot = lambda p: pl.BlockSpec((None, tr, C), lambda i, *s: (jnp.maximum(s[2 + p][0], 0), i, 0))
    return pl.pallas_call(
        body, name=name,
        grid_spec=pltpu.PrefetchScalarGridSpec(num_scalar_prefetch=n_s, grid=(rows // tr,),
                                               in_specs=[blk] + [slot(p) for p in range(8)], out_specs=blk),
        out_shape=jax.ShapeDtypeStruct((rows, C), F32), compiler_params=_cp("parallel"),
    )(*order, own, *([recv] * 8))


MM_TM_K = 512
WGRAD_TM = 2048


def _rows_merged(w):
    return w.reshape(1, w.shape[0] * w.shape[1], w.shape[2])


def _sq_relu_epilogue(acc):
    r = jnp.maximum(acc, 0.0)
    return acc, r * r


def _add_epilogue(acc, x):
    return (acc + x,)


def _add_loss_epilogue(acc, x, target):
    e = acc + x - target
    D = e.shape[1]
    share = (0.5 / D) * jnp.sum(jnp.sum(e * e, axis=1, keepdims=True), axis=0, keepdims=True)
    return e * (1.0 / D), jnp.broadcast_to(share, (1, D))


def _add_norm_epilogue(acc, x, g):
    y = acc + x
    r = lax.rsqrt(jnp.mean(y * y, axis=-1, keepdims=True) + EPS)
    return y, y * r * g


def _norm_bwd_epilogue(dh, x, dres, g):
    r = lax.rsqrt(jnp.mean(x * x, axis=-1, keepdims=True) + EPS)
    xhat = x * r
    dxhat = dh * g
    dx = dres + r * (dxhat - xhat * jnp.mean(dxhat * xhat, axis=-1, keepdims=True))
    return dx, jnp.sum(dh * xhat, axis=0, keepdims=True)


def _sq_relu_grad_epilogue(acc, a):
    return (acc * (2.0 * jnp.maximum(a.astype(F32), 0.0)),)


STAGES = ("mixer_in", "mixer_out", "mlp")


def _stage_tensors(layer, stage):
    i = layer // 2
    if stage == "mlp":
        return [("mlp_w1", layer), ("mlp_w2", layer)]
    if stage == "mixer_in":
        return [("ab_w_in", i)] if layer % 2 == 0 else [("c_w_qkv", i)]
    return [("b_conv_w", i), ("ab_w_out", i)] if layer % 2 == 0 else [("c_w_out", i)]


def _local_step(x, target, p, weights_of, grads_done):
    S, D = x.shape
    depth = p["mix_norm_g"].shape[0]
    n_even = (depth + 1) // 2
    mix_g3 = p["mix_norm_g"].reshape(depth, 1, D)
    mlp_g3 = p["mlp_norm_g"].reshape(depth, 1, D)
    vec3 = lambda t: t.reshape(t.shape[0], 1, t.shape[1])
    spw16 = p["a_spatial_w"].astype(BF16)
    spw16_t = jnp.swapaxes(spw16, 2, 3)
    bias_full = jnp.repeat(jnp.swapaxes(p["a_spatial_b"], 1, 2), HEAD_DIM, axis=2)
    vn_g, vn_b, cn_g, cn_b, cb3 = (vec3(p[k]) for k in ("a_vnorm_g", "a_vnorm_b", "b_norm_g", "b_norm_b", "b_conv_b"))
    tables = _rope_tables(S)
    gq = jnp.tile(p["c_q_norm_g"], (1, 2))
    gk = jnp.tile(p["c_k_norm_g"], (1, 2))

    saved = []
    h = _rms_fwd("mix_norm_0", x, mix_g3, 0)
    for layer in range(depth):
        i = layer // 2
        wl = dict(weights_of(layer, "mixer_in", x))
        rec = {"x_mix": x, "w": wl, "h_mix": h}
        if layer % 2 == 0:
            (z,) = _mm_ngroup(f"ab_in_{layer}", h, wl["ab_w_in"], nt=False, tm=MM_TM_K, out_dtypes=[F32])
            wl.update(weights_of(layer, "mixer_out", z))
            gconv = _glu_conv_fwd(f"glu_conv_{layer}", z, wl["b_conv_w"], cb3, i)
            x, h, cat = _ab_tail_out_proj(f"ab_out_{layer}", z, gconv, spw16, bias_full, vn_g, vn_b, cn_g, cn_b, i,
                                          _rows_merged(wl["ab_w_out"]), x, mlp_g3, layer)
            rec.update(z=z, gconv=gconv, cat=cat)
        else:
            (qkv,) = _mm_ngroup(f"c_qkv_{layer}", h, wl["c_w_qkv"], nt=False, tm=MM_TM_K, out_dtypes=[F32])
            wl.update(weights_of(layer, "mixer_out", qkv))
            qn, kn = _qk_fwd(f"qk_norm_rope_{layer}", qkv, gq[i:i + 1], gk[i:i + 1], tables)
            merged = None
            for d in PATTERN_DILATIONS:
                merged = _attn_fwd(f"attn_d{d}_{layer}", qn, kn, qkv, V_COL, d, merged)
            o, lse = merged
            rec.update(qkv=qkv, qn=qn, kn=kn, o=o, lse=lse)
            x, h = _mm_kgroup(f"c_out_{layer}", o, _rows_merged(wl["c_w_out"]), nt=False, tm=MM_TM_K,
                              out_dtypes=[F32, BF16], extras=(x,), vecs=[(mlp_g3, layer)], epilogue=_add_norm_epilogue)
        rec["x_mlp"] = x
        wl.update(weights_of(layer, "mlp", x))
        a, hsq = _mm_ngroup(f"mlp_up_{layer}", h, wl["mlp_w1"], nt=False, tm=MM_TM_K, out_dtypes=[BF16, BF16],
                            epilogue=_sq_relu_epilogue)
        rec.update(h_mlp=h, a=a, hsq=hsq)
        if layer + 1 < depth:
            x, h = _mm_kgroup(f"mlp_down_{layer}", hsq, _rows_merged(wl["mlp_w2"]), nt=False, tm=MM_TM_K,
                              out_dtypes=[F32, BF16], extras=(x,), vecs=[(mix_g3, layer + 1)], epilogue=_add_norm_epilogue)
        else:
            dx, loss_row = _mm_kgroup(f"mlp_down_{layer}", hsq, _rows_merged(wl["mlp_w2"]), nt=False, tm=MM_TM_K,
                                      out_dtypes=[F32], extras=(x, target), n_sums=1, epilogue=_add_loss_epilogue)
        saved.append(rec)

    small = {k: [None] * v.shape[0] for k, v in p.items()}
    token = None
    for layer in reversed(range(depth)):
        i = layer // 2
        rec = saved[layer]
        wl = rec["w"]
        g = {}
        (da,) = _mm_ngroup(f"mlp_down_dgrad_{layer}", dx, wl["mlp_w2"], nt=True, tm=MM_TM_K, out_dtypes=[BF16],
                           extras=(rec["a"],), epilogue=_sq_relu_grad_epilogue, anchor=token)
        g["mlp_w2"] = _wgrad(f"mlp_down_wgrad_{layer}", rec["hsq"], dx, wl["mlp_w2"].shape, a_group=True, tm=WGRAD_TM)
        g["mlp_w1"] = _wgrad(f"mlp_up_wgrad_{layer}", rec["h_mlp"], da, wl["mlp_w1"].shape, a_group=False, tm=WGRAD_TM)
        dx, small["mlp_norm_g"][layer] = _mm_kgroup(
            f"mlp_up_dgrad_{layer}", da, wl["mlp_w1"], nt=True, tm=MM_TM_K, out_dtypes=[F32], extras=(rec["x_mlp"], dx),
            vecs=[(mlp_g3, layer)], n_sums=1, epilogue=_norm_bwd_epilogue)
        token = grads_done(layer, "mlp", g)
        g = {}
        if layer % 2 == 0:
            w_out = _rows_merged(wl["ab_w_out"])
            (dcat,) = _mm_ngroup(f"ab_out_dgrad_{layer}", dx, w_out, nt=True, tm=MM_TM_K, out_dtypes=[F32], anchor=token)
            g["ab_w_out"] = _wgrad(f"ab_out_wgrad_{layer}", rec["cat"], dx, w_out.shape, a_group=True,
                                   tm=WGRAD_TM).reshape(wl["ab_w_out"].shape)
            dz, dgconv, dspw, dbias, dvg, dvb, dcg, dcb = _ab_tail_bwd(
                f"ab_tail_bwd_{layer}", rec["z"], rec["gconv"], dcat, spw16, spw16_t, bias_full, vn_g, vn_b, cn_g, cn_b, i)
            dz, g["b_conv_w"], dcbias = _glu_conv_bwd(f"glu_conv_bwd_{layer}", rec["z"], dgconv, dz, wl["b_conv_w"])
            token = grads_done(layer, "mixer_out", g)
            g = {}
            small["a_spatial_w"][i] = dspw
            small["a_spatial_b"][i] = _fold_bias(dbias)[:, :A_GROUPS].T
            for k, val in (("a_vnorm_g", dvg), ("a_vnorm_b", dvb), ("b_norm_g", dcg), ("b_norm_b", dcb), ("b_conv_b", dcbias)):
                small[k][i] = val
            g["ab_w_in"] = _wgrad(f"ab_in_wgrad_{layer}", rec["h_mix"], dz, wl["ab_w_in"].shape, a_group=False, tm=WGRAD_TM,
                                  anchor=token)
            dgrad = (f"ab_in_dgrad_{layer}", dz, wl["ab_w_in"])
        else:
            w_out = _rows_merged(wl["c_w_out"])
            do, delta = _mm_ngroup(f"c_out_dgrad_{layer}", dx, w_out, nt=True, tm=MM_TM_K, out_dtypes=[F32, F32],
                                   extras=(rec["o"],), epilogue=_delta_epilogue, anchor=token)
            g["c_w_out"] = _wgrad(f"c_out_wgrad_{layer}", rec["o"], dx, w_out.shape, a_group=True,
                                  tm=WGRAD_TM).reshape(wl["c_w_out"].shape)
            token = grads_done(layer, "mixer_out", g)
            g = {}
            attn_args = (rec["qn"], rec["kn"], rec["qkv"], V_COL, do, rec["lse"], delta)
            dqs = [_attn_bwd_q(f"attn_bwd_q_d{d}_{layer}", *attn_args, d) for d in PATTERN_DILATIONS]
            dks, dvs = zip(*[_attn_bwd_kv(f"attn_bwd_kv_d{d}_{layer}", *attn_args, d) for d in PATTERN_DILATIONS])
            dqkv, dgq, dgk = _qk_bwd(f"qk_norm_rope_bwd_{layer}", rec["qkv"], gq[i:i + 1], gk[i:i + 1], tables, dqs, dks, dvs)
            small["c_q_norm_g"][i] = dgq[:, :HEAD_DIM]
            small["c_k_norm_g"][i] = dgk[:, :HEAD_DIM]
            g["c_w_qkv"] = _wgrad(f"c_qkv_wgrad_{layer}", rec["h_mix"], dqkv, wl["c_w_qkv"].shape, a_group=False, tm=WGRAD_TM,
                                  anchor=token)
            dgrad = (f"c_qkv_dgrad_{layer}", dqkv, wl["c_w_qkv"])
        token = grads_done(layer, "mixer_in", g)
        dx, small["mix_norm_g"][layer] = _mm_kgroup(
            *dgrad, nt=True, tm=MM_TM_K, out_dtypes=[F32], extras=(rec["x_mix"], dx), vecs=[(mix_g3, layer)], n_sums=1,
            epilogue=_norm_bwd_epilogue, anchor=token)

    small = {k: jnp.stack([t.reshape(p[k].shape[1:]) for t in v]) for k, v in small.items()}
    return loss_row, dx, small


SHARDED = ("mlp_w1", "mlp_w2", "ab_w_in", "b_conv_w", "ab_w_out", "c_w_qkv", "c_w_out")
SMALL = ("mix_norm_g", "mlp_norm_g", "a_spatial_w", "a_spatial_b", "a_vnorm_g", "a_vnorm_b", "b_conv_b", "b_norm_g",
         "b_norm_b", "c_q_norm_g", "c_k_norm_g")
WEIGHTS = ("mix_norm_g", "mlp_norm_g", "mlp_w1", "mlp_w2", "ab_w_in", "a_spatial_w", "a_spatial_b", "a_vnorm_g",
           "a_vnorm_b", "b_conv_w", "b_conv_b", "b_norm_g", "b_norm_b", "ab_w_out", "c_w_qkv", "c_q_norm_g",
           "c_k_norm_g", "c_w_out")


def _pack(parts):
    flat = jnp.concatenate([parts[k].reshape(-1) for k in SMALL])
    rows = -(-flat.shape[0] // (256 * LANES)) * 256
    return jnp.pad(flat, (0, rows * LANES - flat.shape[0])).reshape(rows, LANES)


def _unpack(packed, like):
    flat = packed.reshape(-1)
    out, off = {}, 0
    for k in SMALL:
        n = like[k].size
        out[k] = flat[off:off + n].reshape(like[k].shape)
        off += n
    return out


def kernel(x, mix_norm_g, mlp_norm_g, mlp_w1, mlp_w2, ab_w_in, a_spatial_w, a_spatial_b, a_vnorm_g, a_vnorm_b, b_conv_w, b_conv_b, b_norm_g, b_norm_b, ab_w_out, c_w_qkv, c_q_norm_g, c_k_norm_g, c_w_out, loss_target, m_mix_norm_g, m_mlp_norm_g, m_mlp_w1, m_mlp_w2, m_ab_w_in, m_a_spatial_w, m_a_spatial_b, m_a_vnorm_g, m_a_vnorm_b, m_b_conv_w, m_b_conv_b, m_b_norm_g, m_b_norm_b, m_ab_w_out, m_c_w_qkv, m_c_q_norm_g, m_c_k_norm_g, m_c_w_out, v_mix_norm_g, v_mlp_norm_g, v_mlp_w1, v_mlp_w2, v_ab_w_in, v_a_spatial_w, v_a_spatial_b, v_a_vnorm_g, v_a_vnorm_b, v_b_conv_w, v_b_conv_b, v_b_norm_g, v_b_norm_b, v_ab_w_out, v_c_w_qkv, v_c_q_norm_g, v_c_k_norm_g, v_c_w_out):
    w = dict(mix_norm_g=mix_norm_g, mlp_norm_g=mlp_norm_g, mlp_w1=mlp_w1, mlp_w2=mlp_w2, ab_w_in=ab_w_in,
             a_spatial_w=a_spatial_w, a_spatial_b=a_spatial_b, a_vnorm_g=a_vnorm_g, a_vnorm_b=a_vnorm_b,
             b_conv_w=b_conv_w, b_conv_b=b_conv_b, b_norm_g=b_norm_g, b_norm_b=b_norm_b, ab_w_out=ab_w_out,
             c_w_qkv=c_w_qkv, c_q_norm_g=c_q_norm_g, c_k_norm_g=c_k_norm_g, c_w_out=c_w_out)
    m = dict(mix_norm_g=m_mix_norm_g, mlp_norm_g=m_mlp_norm_g, mlp_w1=m_mlp_w1, mlp_w2=m_mlp_w2, ab_w_in=m_ab_w_in,
             a_spatial_w=m_a_spatial_w, a_spatial_b=m_a_spatial_b, a_vnorm_g=m_a_vnorm_g, a_vnorm_b=m_a_vnorm_b,
             b_conv_w=m_b_conv_w, b_conv_b=m_b_conv_b, b_norm_g=m_b_norm_g, b_norm_b=m_b_norm_b, ab_w_out=m_ab_w_out,
             c_w_qkv=m_c_w_qkv, c_q_norm_g=m_c_q_norm_g, c_k_norm_g=m_c_k_norm_g, c_w_out=m_c_w_out)
    v = dict(mix_norm_g=v_mix_norm_g, mlp_norm_g=v_mlp_norm_g, mlp_w1=v_mlp_w1, mlp_w2=v_mlp_w2, ab_w_in=v_ab_w_in,
             a_spatial_w=v_a_spatial_w, a_spatial_b=v_a_spatial_b, a_vnorm_g=v_a_vnorm_g, a_vnorm_b=v_a_vnorm_b,
             b_conv_w=v_b_conv_w, b_conv_b=v_b_conv_b, b_norm_g=v_b_norm_g, b_norm_b=v_b_norm_b, ab_w_out=v_ab_w_out,
             c_w_qkv=v_c_w_qkv, c_q_norm_g=v_c_q_norm_g, c_k_norm_g=v_c_k_norm_g, c_w_out=v_c_w_out)

    S, D = x.shape[1], x.shape[2]
    depth = mix_norm_g.shape[0]
    mine = (2 * lax.axis_index("x") + lax.axis_index("y")).astype(jnp.int32).reshape(1)

    stages = [(layer, stage) for layer in range(depth) for stage in STAGES]

    def start_gather(name, some_stages, anchor):
        groups = [[_prepare_shard(f"prepare_{k}_{i}", w[k], i, F32 if k == "b_conv_w" else BF16, mine, anchor)
                   for k, i in _stage_tensors(*st)] for st in some_stages]
        return _exchange_start(name, "gather", groups)

    first, rest = stages[:len(STAGES)], stages[len(STAGES):]
    handles_first, token_first = start_gather("gather_weights_start_first", first, None)
    handles_rest, gather_token = start_gather("gather_weights_start_rest", rest, token_first)
    handles = dict(zip(first + rest, handles_first + handles_rest))

    def weights_of(layer, stage, after):
        _, got = _exchange_wait(f"gather_weights_wait_{layer}_{stage}", "gather", handles[layer, stage],
                                gather_token if (layer, stage) == stages[0] else after)
        return {k: a for (k, _), a in zip(_stage_tensors(layer, stage), got)}

    scattered = {}

    def grads_done(layer, stage, g):
        names = [k for k, _ in _stage_tensors(layer, stage)]
        group = [(g[k], lax.empty((N_PEERS["scatter"],) + g[k].shape[1:], BF16)) for k in names]
        (scattered[layer, stage],), token = _exchange_start(f"scatter_grads_start_{layer}_{stage}", "scatter", [group])
        return token

    small_params = {k: w[k] for k in SMALL}
    loss_row, dx, small_grads = _local_step(x.reshape(S, D), loss_target.reshape(S, D), small_params, weights_of, grads_done)

    loss = lax.psum(loss_row[0, 0], ("x", "y", "c"))

    packed = _pack(small_grads)
    (small_handle,), small_token = _exchange_start(
        "allgather_small_grads_start", "allgather", [[(packed, lax.empty((N_PEERS["allgather"],) + packed.shape, F32))]])

    order = _sum_order()
    stacked = {k: [lax.empty(w[k].shape, F32) for _ in range(4)] for k in SHARDED}
    for layer, stage in reversed(stages):
        gbs, recvs = _exchange_wait(f"scatter_grads_wait_{layer}_{stage}", "scatter", scattered[layer, stage], small_token)
        for (k, i), gb, recv in zip(_stage_tensors(layer, stage), gbs, recvs):
            stacked[k] = _adamw_layer(f"adamw_{k}_{i}", w[k], m[k], v[k], i, gb, recv, order, stacked[k])
    grads, deltas, new_m, new_v = ({k: stacked[k][j] for k in SHARDED} for j in range(4))

    last_updated = stacked[_stage_tensors(*stages[0])[-1][0]][0]
    (packed,), (recv,) = _exchange_wait("allgather_small_grads_wait", "allgather", small_handle, last_updated)
    g_small = _ordered_sum("sum_small_grads", packed, recv, order)
    outs = _adamw("adamw_small", _pack(small_params), _pack({k: m[k] for k in SMALL}), _pack({k: v[k] for k in SMALL}), g_small)
    for d_, packed in zip((grads, deltas, new_m, new_v), outs):
        d_.update(_unpack(packed, small_params))

    return (loss, dx.reshape(1, S, D), *[grads[k] for k in WEIGHTS], *[deltas[k] for k in WEIGHTS],
            *[new_m[k] for k in WEIGHTS], *[new_v[k] for k in WEIGHTS])
```

```python
import functools

import jax
import jax.numpy as jnp
from jax import lax
from jax.experimental import pallas as pl
from jax.experimental.pallas import tpu as pltpu

F32, BF16 = jnp.float32, jnp.bfloat16
MESH = pl.DeviceIdType.MESH
ANY = pl.BlockSpec(memory_space=pl.ANY)

VMEM_LIMIT_BYTES = 56 * 1024 * 1024
LANES = 128
ELEMENTWISE_ROWS = 256

EPS = 1e-6
NEG = -1e30
HEAD_DIM = 64
N_HEADS = 16
CHUNK = 128
A_GROUPS = 8
CONV_WIDTH = 31
CONV_HALO = 16
CONV_CHUNK = 64
BAND = 64
PATTERN_DILATIONS = (1, 4, 16)
ROT_DIM = 16
ROPE_THETA = 500000.0
N_SHARDS = 4

ADAM_LR, ADAM_B1, ADAM_B2, ADAM_EPS, ADAM_WD, ADAM_STEP = 0.001, 0.9, 0.999, 1e-08, 0.01, 10


def _cp(*sem):
    return pltpu.CompilerParams(dimension_semantics=sem, vmem_limit_bytes=VMEM_LIMIT_BYTES)


def _tile(n, pref):
    t = min(n, pref)
    assert n % t == 0, (n, pref)
    return t


def _dot(a, b, ca, cb):
    return lax.dot_general(a, b, (((ca,), (cb,)), ((), ())), preferred_element_type=F32)


def _mm_ngroup(name, a, w, *, nt, tm, out_dtypes, extras=(), epilogue=None, anchor=None):
    M, K = a.shape
    G, R, C = w.shape
    nw = R if nt else C
    assert K == (C if nt else R)
    tm = _tile(M, tm)
    n_ex = len(extras)
    anchors = [] if anchor is None else [anchor]

    def body(a_ref, w_ref, *rest):
        rest = rest[len(anchors):]
        av = a_ref[...].astype(BF16)
        for g in range(G):
            cols = slice(g * nw, (g + 1) * nw)
            acc = _dot(av, w_ref[g], 1, 1 if nt else 0)
            res = epilogue(acc, *[e[:, cols] for e in rest[:n_ex]]) if epilogue else (acc,)
            for o_ref, r in zip(rest[n_ex:], res):
                o_ref[:, cols] = r.astype(o_ref.dtype)

    blk = pl.BlockSpec((tm, G * nw), lambda m: (m, 0))
    return pl.pallas_call(
        body, name=name, grid=(M // tm,),
        in_specs=[pl.BlockSpec((tm, K), lambda m: (m, 0)), pl.BlockSpec((G, R, C), lambda m: (0, 0, 0))]
        + [pl.BlockSpec((8, LANES), lambda m: (0, 0))] * len(anchors) + [blk] * n_ex,
        out_specs=[blk] * len(out_dtypes),
        out_shape=[jax.ShapeDtypeStruct((M, G * nw), dt) for dt in out_dtypes],
        compiler_params=_cp("parallel"),
    )(a, w, *anchors, *extras)


def _mm_kgroup(name, a, w, *, nt, tm, out_dtypes, extras=(), vecs=(), n_sums=0, epilogue=None, anchor=None):
    G, R, C = w.shape
    kw, N = (C, R) if nt else (R, C)
    if a.ndim == 3:
        M = a.shape[1]
        assert a.shape[0] == G and a.shape[2] == kw
    else:
        M = a.shape[0]
        assert a.shape[1] == G * kw
    tm = _tile(M, tm)
    n_ex = len(extras)
    a_spec = (pl.BlockSpec((G, tm, kw), lambda m: (0, m, 0)) if a.ndim == 3 else pl.BlockSpec((tm, G * kw), lambda m: (m, 0)))
    anchors = [] if anchor is None else [anchor]

    def body(a_ref, w_ref, *rest):
        rest = rest[len(anchors):]
        acc = None
        for g in range(G):
            a_g = a_ref[g] if a.ndim == 3 else a_ref[:, g * kw:(g + 1) * kw]
            part = _dot(a_g.astype(BF16), w_ref[g], 1, 1 if nt else 0)
            acc = part if acc is None else acc + part
        n_in = n_ex + len(vecs)
        res = epilogue(acc, *[e[...] for e in rest[:n_in]]) if epilogue else (acc,)
        outs = rest[n_in:]
        n_tiles = len(outs) - n_sums
        for o_ref, r in zip(outs[:n_tiles], res[:n_tiles]):
            o_ref[...] = r.astype(o_ref.dtype)
        if n_sums:
            @pl.when(pl.program_id(0) == 0)
            def _():
                for s_ref in outs[n_tiles:]:
                    s_ref[...] = jnp.zeros_like(s_ref)

            for s_ref, r in zip(outs[n_tiles:], res[n_tiles:]):
                s_ref[...] += r

    blk = pl.BlockSpec((tm, N), lambda m: (m, 0))
    row = pl.BlockSpec((1, N), lambda m: (0, 0))
    return pl.pallas_call(
        body, name=name, grid=(M // tm,),
        in_specs=[a_spec, pl.BlockSpec((G, R, C), lambda m: (0, 0, 0))]
        + [pl.BlockSpec((8, LANES), lambda m: (0, 0))] * len(anchors) + [blk] * n_ex
        + [pl.BlockSpec((None, 1, N), lambda m, i=i: (i, 0, 0)) for _, i in vecs],
        out_specs=[blk] * len(out_dtypes) + [row] * n_sums,
        out_shape=[jax.ShapeDtypeStruct((M, N), dt) for dt in out_dtypes] + [jax.ShapeDtypeStruct((1, N), F32)] * n_sums,
        compiler_params=_cp("arbitrary" if n_sums else "parallel"),
    )(a, w, *anchors, *extras, *[v for v, _ in vecs])


def _wgrad(name, a, b, shape, *, a_group, tm, anchor=None):
    G, R, C = shape
    M = a.shape[0]
    tm = _tile(M, tm)
    n_m = M // tm
    anchors = [] if anchor is None else [anchor]

    def body(a_ref, b_ref, *rest):
        gb_ref, gf_ref = rest[len(anchors):]
        m = pl.program_id(1)
        part = _dot(a_ref[...].astype(BF16), b_ref[...].astype(BF16), 0, 0)

        @pl.when(m == 0)
        def _():
            gf_ref[...] = part

        @pl.when(m > 0)
        def _():
            gf_ref[...] += part

        @pl.when(m == n_m - 1)
        def _():
            gb_ref[...] = gf_ref[...].astype(BF16)

    a_spec = pl.BlockSpec((tm, R), (lambda g, m: (m, g)) if a_group else (lambda g, m: (m, 0)))
    if b.ndim == 3:
        assert not a_group
        b_spec = pl.BlockSpec((None, tm, C), lambda g, m: (g, m, 0))
    else:
        b_spec = pl.BlockSpec((tm, C), (lambda g, m: (m, 0)) if a_group else (lambda g, m: (m, g)))
    o_spec = pl.BlockSpec((None, R, C), lambda g, m: (g, 0, 0))
    return pl.pallas_call(
        body, name=name, grid=(G, n_m),
        in_specs=[a_spec, b_spec] + [pl.BlockSpec((8, LANES), lambda g, m: (0, 0))] * len(anchors), out_specs=o_spec,
        out_shape=jax.ShapeDtypeStruct(shape, BF16), scratch_shapes=[pltpu.VMEM((R, C), F32)],
        compiler_params=_cp("parallel", "arbitrary"),
    )(a, b, *anchors)


def _rms_fwd(name, x, g3, layer):
    S, D = x.shape
    tm = _tile(S, 512)

    def body(x_ref, g_ref, h_ref):
        xv = x_ref[...]
        r = lax.rsqrt(jnp.mean(xv * xv, axis=-1, keepdims=True) + EPS)
        h_ref[...] = (xv * r * g_ref[...]).astype(BF16)

    row = pl.BlockSpec((tm, D), lambda m: (m, 0))
    return pl.pallas_call(
        body, name=name, grid=(S // tm,),
        in_specs=[row, pl.BlockSpec((None, 1, D), lambda m: (layer, 0, 0))], out_specs=row,
        out_shape=jax.ShapeDtypeStruct((S, D), BF16), compiler_params=_cp("parallel"),
    )(x, g3)


def _adamw_math(w, m, v, g):
    m2 = ADAM_B1 * m + (1.0 - ADAM_B1) * g
    v2 = ADAM_B2 * v + (1.0 - ADAM_B2) * jnp.square(g)
    m_hat = m2 / (1.0 - ADAM_B1 ** ADAM_STEP)
    v_hat = v2 / (1.0 - ADAM_B2 ** ADAM_STEP)
    return g, -ADAM_LR * (m_hat / (jnp.sqrt(v_hat) + ADAM_EPS) + ADAM_WD * w), m2, v2


def _row_tile(rows):
    return _tile(rows, ELEMENTWISE_ROWS) if rows % ELEMENTWISE_ROWS == 0 else rows


def _adamw(name, w, m, v, g):
    rows, C = w.shape
    tr = _row_tile(rows)

    def body(w_ref, m_ref, v_ref, g_in, g_ref, d_ref, nm_ref, nv_ref):
        for o_ref, val in zip((g_ref, d_ref, nm_ref, nv_ref), _adamw_math(w_ref[...], m_ref[...], v_ref[...], g_in[...])):
            o_ref[...] = val

    blk = pl.BlockSpec((tr, C), lambda i: (i, 0))
    return pl.pallas_call(
        body, name=name, grid=(rows // tr,), in_specs=[blk] * 4, out_specs=[blk] * 4,
        out_shape=[jax.ShapeDtypeStruct((rows, C), F32)] * 4, compiler_params=_cp("parallel"),
    )(w, m, v, g)


def _sum_order():
    x, y, c = lax.axis_index("x"), lax.axis_index("y"), lax.axis_index("c")
    differs = lambda bit, coord: bit + coord - 2 * bit * coord
    slots = [4 * differs(p >> 2 & 1, x) + 2 * differs(p >> 1 & 1, y) + differs(p & 1, c) - 1 for p in range(8)]
    return [jnp.asarray(s, jnp.int32).reshape(1) for s in [2 * x + y, 4 * x + 2 * y + c] + slots]


def _adamw_layer(name, w, m, v, layer, gb, recv, order, outs):
    _, R, C = w.shape
    tr = _row_tile(R)
    n_s = len(order)

    def body(*refs):
        me = refs[1][0]
        w_ref, m_ref, v_ref, own_ref = refs[n_s:n_s + 4]
        theirs, outs_ = refs[n_s + 4:n_s + 12], refs[n_s + 16:]
        g = None
        for p in range(8):
            term = jnp.where(me == p, own_ref[...], theirs[p][...]).astype(F32)
            g = term if g is None else g + term
        for o_ref, val in zip(outs_, _adamw_math(w_ref[...], m_ref[...], v_ref[...], g)):
            o_ref[...] = val

    st = pl.BlockSpec((None, tr, C), lambda i, *s: (layer, i, 0))
    slot = lambda p: pl.BlockSpec((None, tr, C), lambda i, *s: (jnp.maximum(s[2 + p][0], 0), i, 0))
    return pl.pallas_call(
        body, name=name,
        grid_spec=pltpu.PrefetchScalarGridSpec(
            num_scalar_prefetch=n_s, grid=(R // tr,),
            in_specs=[st] * 3 + [pl.BlockSpec((None, tr, C), lambda i, *s: (s[0][0], i, 0))] + [slot(p) for p in range(8)]
            + [ANY] * 4,
            out_specs=[st] * 4),
        out_shape=[jax.ShapeDtypeStruct(w.shape, F32)] * 4, input_output_aliases={n_s + 12 + j: j for j in range(4)},
        compiler_params=_cp("parallel"),
    )(*order, w, m, v, gb, *([recv] * 8), *outs)


def _gelu(x):
    return x * (0.5 * (1.0 + jnp.tanh(0.7978845608028654 * (x + 0.044715 * (x * x * x)))))


def _layernorm(t, g, b):
    mu = jnp.mean(t, axis=-1, keepdims=True)
    var = jnp.mean(jnp.square(t - mu), axis=-1, keepdims=True)
    return (t - mu) * lax.rsqrt(var + EPS) * g + b


def _silu(x):
    return x * jax.nn.sigmoid(x)


def _a_value(zv, g, b):
    return _layernorm(_gelu(zv), g, b)


def _b_tail(gc, g, b):
    return _silu(_layernorm(gc, g, b))


def _first_head(shape):
    return lax.broadcasted_iota(jnp.int32, shape, len(shape) - 1) < HEAD_DIM


def _spatial_mix(spw_ref, vb, tm):
    first = _first_head((CHUNK, LANES))
    rows = []
    for n in range(tm // CHUNK):
        blocks = []
        for j in range(A_GROUPS // 2):
            vblk = vb[n * CHUNK:(n + 1) * CHUNK, j * LANES:(j + 1) * LANES]
            r0 = _dot(spw_ref[2 * j], vblk, 1, 0)
            r1 = _dot(spw_ref[2 * j + 1], vblk, 1, 0)
            blocks.append(jnp.where(first, r0, r1))
        rows.append(jnp.concatenate(blocks, axis=1))
    return jnp.concatenate(rows, axis=0) if len(rows) > 1 else rows[0]


def _ab_tail_out_proj(name, z, gconv, spw, bias_full, vn_g, vn_b, cn_g, cn_b, layer, w, x, g3, g_layer):
    S = z.shape[0]
    AW = 512
    tm = _tile(S, 256)

    def body(zu_ref, zv_ref, gc_ref, spw_ref, bias_ref, vg_ref, vb_ref, cg_ref, cb_ref, w_ref, x_ref, g_ref,
             xo_ref, h_ref, cat_ref):
        u = _gelu(zu_ref[...])
        v = _a_value(zv_ref[...], vg_ref[...], vb_ref[...])
        sv = _spatial_mix(spw_ref, v.astype(BF16), tm) + jnp.tile(bias_ref[...], (tm // CHUNK, 1))
        cat = jnp.concatenate([(u * sv).astype(BF16), _b_tail(gc_ref[...], cg_ref[...], cb_ref[...]).astype(BF16)], axis=1)
        cat_ref[...] = cat
        y, h = _add_norm_epilogue(_dot(cat, w_ref[0], 1, 0), x_ref[...], g_ref[...])
        xo_ref[...] = y
        h_ref[...] = h.astype(BF16)

    vec = pl.BlockSpec((None, 1, AW), lambda m: (layer, 0, 0))
    row = pl.BlockSpec((tm, 2 * AW), lambda m: (m, 0))
    return pl.pallas_call(
        body, name=name, grid=(S // tm,),
        in_specs=[pl.BlockSpec((tm, AW), lambda m: (m, 0)), pl.BlockSpec((tm, AW), lambda m: (m, 1)),
                  pl.BlockSpec((tm, AW), lambda m: (m, 0)),
                  pl.BlockSpec((None, A_GROUPS, CHUNK, CHUNK), lambda m: (layer, 0, 0, 0)),
                  pl.BlockSpec((None, CHUNK, AW), lambda m: (layer, 0, 0)), vec, vec, vec, vec,
                  pl.BlockSpec(w.shape, lambda m: (0, 0, 0)), row, pl.BlockSpec((None, 1, 2 * AW), lambda m: (g_layer, 0, 0))],
        out_specs=[row] * 3,
        out_shape=[jax.ShapeDtypeStruct((S, 2 * AW), dt) for dt in (F32, BF16, BF16)], compiler_params=_cp("parallel"),
    )(z, z, gconv, spw, bias_full, vn_g, vn_b, cn_g, cn_b, w, x, g3)


def _ab_tail_bwd(name, z, gconv, dcat, spw, spw_t, bias_full, vn_g, vn_b, cn_g, cn_b, layer):
    S = z.shape[0]
    AW = 512
    tm = _tile(S, 256)
    n_chunks = tm // CHUNK

    def body(zu_ref, zv_ref, gc_ref, dcat_ref, spw_ref, spwt_ref, bias_ref, vg_ref, vb_ref, cg_ref, cb_ref,
             dz_ref, dgc_ref, dspw_ref, dbias_ref, dvg_ref, dvb_ref, dcg_ref, dcb_ref):
        @pl.when(pl.program_id(0) == 0)
        def _():
            for r in (dspw_ref, dbias_ref, dvg_ref, dvb_ref, dcg_ref, dcb_ref):
                r[...] = jnp.zeros_like(r)

        dya = dcat_ref[:, :AW]
        dyb = dcat_ref[:, AW:]
        u, u_vjp = jax.vjp(_gelu, zu_ref[...])
        v, v_vjp = jax.vjp(_a_value, zv_ref[...], vg_ref[...], vb_ref[...])
        vb16 = v.astype(BF16)
        sv = _spatial_mix(spw_ref, vb16, tm) + jnp.tile(bias_ref[...], (n_chunks, 1))
        (dzu,) = u_vjp(dya * sv)
        dsv = dya * u
        dsv16 = dsv.astype(BF16)
        dv = _spatial_mix(spwt_ref, dsv16, tm)
        dzv, dvg, dvb = v_vjp(dv)
        dz_ref[0] = dzu
        dz_ref[1] = dzv
        dvg_ref[...] += dvg
        dvb_ref[...] += dvb

        first = _first_head((CHUNK, LANES))
        zero = jnp.zeros((), BF16)
        dbias = jnp.zeros((CHUNK, AW), F32)
        for n in range(n_chunks):
            rows = slice(n * CHUNK, (n + 1) * CHUNK)
            dbias = dbias + dsv[rows]
            for j in range(A_GROUPS // 2):
                cols = slice(j * LANES, (j + 1) * LANES)
                dblk, vblk = dsv16[rows, cols], vb16[rows, cols]
                dspw_ref[2 * j] += _dot(jnp.where(first, dblk, zero), vblk, 1, 1)
                dspw_ref[2 * j + 1] += _dot(jnp.where(first, zero, dblk), vblk, 1, 1)
        dbias_ref[...] += dbias

        _, t_vjp = jax.vjp(_b_tail, gc_ref[...], cg_ref[...], cb_ref[...])
        dgc, dcg, dcb = t_vjp(dyb)
        dgc_ref[...] = dgc
        dcg_ref[...] += dcg
        dcb_ref[...] += dcb

    vec = pl.BlockSpec((None, 1, AW), lambda m: (layer, 0, 0))
    spw_spec = pl.BlockSpec((None, A_GROUPS, CHUNK, CHUNK), lambda m: (layer, 0, 0, 0))
    ovec = pl.BlockSpec((1, AW), lambda m: (0, 0))
    return pl.pallas_call(
        body, name=name, grid=(S // tm,),
        in_specs=[pl.BlockSpec((tm, AW), lambda m: (m, 0)), pl.BlockSpec((tm, AW), lambda m: (m, 1)),
                  pl.BlockSpec((tm, AW), lambda m: (m, 0)), pl.BlockSpec((tm, 2 * AW), lambda m: (m, 0)),
                  spw_spec, spw_spec, pl.BlockSpec((None, CHUNK, AW), lambda m: (layer, 0, 0)), vec, vec, vec, vec],
        out_specs=[pl.BlockSpec((2, tm, AW), lambda m: (0, m, 0)), pl.BlockSpec((tm, AW), lambda m: (m, 0)),
                   pl.BlockSpec((A_GROUPS, CHUNK, CHUNK), lambda m: (0, 0, 0)),
                   pl.BlockSpec((CHUNK, AW), lambda m: (0, 0)), ovec, ovec, ovec, ovec],
        out_shape=[jax.ShapeDtypeStruct((4, S, AW), F32), jax.ShapeDtypeStruct((S, AW), F32),
                   jax.ShapeDtypeStruct((A_GROUPS, CHUNK, CHUNK), F32), jax.ShapeDtypeStruct((CHUNK, AW), F32)]
                  + [jax.ShapeDtypeStruct((1, AW), F32)] * 4,
        compiler_params=_cp("arbitrary"),
    )(z, z, gconv, dcat, spw, spw_t, bias_full, vn_g, vn_b, cn_g, cn_b)


def _fold_bias(dbias_full):
    def body(d_ref, o_ref):
        d = d_ref[...]
        hi = d.astype(BF16)
        lo = (d - hi.astype(F32)).astype(BF16)
        r = lax.broadcasted_iota(jnp.int32, (512, LANES), 0)
        c = lax.broadcasted_iota(jnp.int32, (512, LANES), 1)
        fold = jnp.where(lax.shift_right_logical(r, 6) == c, 1.0, 0.0).astype(BF16)
        o_ref[...] = _dot(hi, fold, 1, 0) + _dot(lo, fold, 1, 0)

    return pl.pallas_call(body, name="fold_spatial_bias", out_shape=jax.ShapeDtypeStruct((CHUNK, LANES), F32))(dbias_full)


def _halo_specs(tm, n_halo_blocks, col):
    r = tm // CONV_HALO
    prev = pl.BlockSpec((CONV_HALO, LANES), lambda j, i: (jnp.maximum(i * r - 1, 0), col + j))
    cur = pl.BlockSpec((tm, LANES), lambda j, i: (i, col + j))
    nxt = pl.BlockSpec((CONV_HALO, LANES), lambda j, i: (jnp.minimum((i + 1) * r, n_halo_blocks - 1), col + j))
    return [prev, cur, nxt]


def _fill_halo(scr, prev, cur, nxt, tm, i, n_i):
    scr[0:CONV_HALO, :] = jnp.where(i > 0, prev, 0.0)
    scr[CONV_HALO:CONV_HALO + tm, :] = cur
    scr[CONV_HALO + tm:2 * CONV_HALO + tm, :] = jnp.where(i < n_i - 1, nxt, 0.0)


def _glu_conv_fwd(name, z, cw, cb3, layer):
    S = z.shape[0]
    tm = _tile(S, 512)
    n_i = S // tm
    pad = CONV_WIDTH // 2

    def body(vp, vc, vn, gp, gc, gn, w_ref, b_ref, out_ref, scr):
        i = pl.program_id(1)
        glu = lambda a, b: a[...] * jax.nn.sigmoid(b[...])
        _fill_halo(scr, glu(vp, gp), glu(vc, gc), glu(vn, gn), tm, i, n_i)
        taps = [w_ref[j:j + 1, :] for j in range(CONV_WIDTH)]
        for c0 in range(0, tm, CONV_CHUNK):
            acc = jnp.zeros((CONV_CHUNK, LANES), F32) + b_ref[...]
            for j in range(CONV_WIDTH):
                acc = acc + taps[j] * scr[pl.ds(c0 + CONV_HALO - pad + j, CONV_CHUNK), :]
            out_ref[pl.ds(c0, CONV_CHUNK), :] = acc

    return pl.pallas_call(
        body, name=name, grid=(4, n_i),
        in_specs=_halo_specs(tm, S // CONV_HALO, 8) + _halo_specs(tm, S // CONV_HALO, 12)
        + [pl.BlockSpec((None, CONV_WIDTH, LANES), lambda j, i: (j, 0, 0)),
           pl.BlockSpec((None, 1, LANES), lambda j, i: (layer, 0, j))],
        out_specs=pl.BlockSpec((tm, LANES), lambda j, i: (i, j)),
        out_shape=jax.ShapeDtypeStruct((S, 4 * LANES), F32),
        scratch_shapes=[pltpu.VMEM((tm + 2 * CONV_HALO, LANES), F32)],
        compiler_params=_cp("parallel", "parallel"),
    )(z, z, z, z, z, z, cw, cb3)


def _glu_conv_bwd(name, z, dgconv, dz, cw):
    S = z.shape[0]
    tm = _tile(S, 512)
    n_i = S // tm
    pad = CONV_WIDTH // 2

    def body(vp, vc, vn, gp, gc, gn, dp, dc, dn, w_ref, dz_in, dz_ref, gb_ref, db_ref, g_scr, d_scr, gf_ref):
        i = pl.program_id(1)
        sig = jax.nn.sigmoid(gc[...])
        _fill_halo(g_scr, vp[...] * jax.nn.sigmoid(gp[...]), vc[...] * sig, vn[...] * jax.nn.sigmoid(gn[...]), tm, i, n_i)
        _fill_halo(d_scr, dp[...], dc[...], dn[...], tm, i, n_i)

        @pl.when(i == 0)
        def _():
            gf_ref[...] = jnp.zeros_like(gf_ref)
            db_ref[...] = jnp.zeros_like(db_ref)

        taps = [w_ref[j:j + 1, :] for j in range(CONV_WIDTH)]
        dw = [jnp.zeros((8, LANES), F32) for _ in range(CONV_WIDTH)]
        db = jnp.zeros((8, LANES), F32)
        fold8 = lambda t: jnp.sum(t.reshape(CONV_CHUNK // 8, 8, LANES), axis=0)
        for c0 in range(0, tm, CONV_CHUNK):
            rows = pl.ds(c0, CONV_CHUNK)
            d_cur = dc[rows, :]
            dglu = jnp.zeros((CONV_CHUNK, LANES), F32)
            for j in range(CONV_WIDTH):
                dglu = dglu + taps[j] * d_scr[pl.ds(c0 + CONV_HALO + pad - j, CONV_CHUNK), :]
                dw[j] = dw[j] + fold8(d_cur * g_scr[pl.ds(c0 + CONV_HALO - pad + j, CONV_CHUNK), :])
            db = db + fold8(d_cur)
            sig_c = jax.nn.sigmoid(gc[rows, :])
            dz_ref[0, rows, :] = dglu * sig_c
            dz_ref[1, rows, :] = dglu * vc[rows, :] * sig_c * (1.0 - sig_c)
        for j in range(CONV_WIDTH):
            gf_ref[j:j + 1, :] += jnp.sum(dw[j], axis=0, keepdims=True)
        db_ref[...] += jnp.sum(db, axis=0, keepdims=True)

        @pl.when(i == n_i - 1)
        def _():
            gb_ref[...] = gf_ref[...].astype(BF16)

    w_spec = pl.BlockSpec((None, CONV_WIDTH, LANES), lambda j, i: (j, 0, 0))
    return pl.pallas_call(
        body, name=name, grid=(4, n_i),
        in_specs=_halo_specs(tm, S // CONV_HALO, 8) + _halo_specs(tm, S // CONV_HALO, 12)
        + _halo_specs(tm, S // CONV_HALO, 0) + [w_spec, ANY],
        out_specs=[pl.BlockSpec((2, tm, LANES), lambda j, i: (1, i, j)),
                   w_spec, pl.BlockSpec((1, LANES), lambda j, i: (0, j))],
        out_shape=[jax.ShapeDtypeStruct(dz.shape, F32), jax.ShapeDtypeStruct(cw.shape, BF16),
                   jax.ShapeDtypeStruct((1, 4 * LANES), F32)],
        input_output_aliases={10: 0},
        scratch_shapes=[pltpu.VMEM((tm + 2 * CONV_HALO, LANES), F32)] * 2 + [pltpu.VMEM((CONV_WIDTH, LANES), F32)],
        compiler_params=_cp("parallel", "arbitrary"),
    )(z, z, z, z, z, z, dgconv, dgconv, dgconv, cw, dz)


def _seg_matrix(scale):
    r = lax.broadcasted_iota(jnp.int32, (LANES, LANES), 0)
    c = lax.broadcasted_iota(jnp.int32, (LANES, LANES), 1)
    return jnp.where(lax.shift_right_logical(r, 6) == lax.shift_right_logical(c, 6), scale, 0.0).astype(BF16)


def _seg_sum(x, seg):
    hi = x.astype(BF16)
    lo = (x - hi.astype(F32)).astype(BF16)
    return _dot(hi, seg, 1, 0) + _dot(lo, seg, 1, 0)


def _rope_tables(S):
    pos = jnp.arange(S, dtype=F32)
    inv_freq = ROPE_THETA ** (-jnp.arange(0, ROT_DIM, 2, dtype=F32) / ROT_DIM)
    ang = pos[:, None] * inv_freq[None, :]
    cos, sin = jnp.cos(ang), jnp.sin(ang)
    half = ROT_DIM // 2
    rest = HEAD_DIM - ROT_DIM
    one, zero = jnp.ones((S, rest), F32), jnp.zeros((S, rest), F32)
    zh = jnp.zeros((S, half), F32)
    c = jnp.concatenate([cos, cos, one], axis=1)
    sa = jnp.concatenate([-sin, zh, zero], axis=1)
    sb = jnp.concatenate([zh, sin, zero], axis=1)
    return [jnp.tile(t, (1, 2)) for t in (c, sa, sb)]


QK_CHUNK = 64


def _qk_fwd(name, qkv, gq, gk, tables):
    S = qkv.shape[0]
    W = N_HEADS * HEAD_DIM
    tm = _tile(S, 256)
    half = ROT_DIM // 2

    def body(q_ref, k_ref, gq_ref, gk_ref, c_ref, sa_ref, sb_ref, qn_ref, kn_ref):
        seg = _seg_matrix(1.0 / HEAD_DIM)
        for r0 in range(0, tm, QK_CHUNK):
            rows = pl.ds(r0, QK_CHUNK)
            c, sa, sb = c_ref[rows, :], sa_ref[rows, :], sb_ref[rows, :]
            for t_ref, g_ref, o_ref in ((q_ref, gq_ref, qn_ref), (k_ref, gk_ref, kn_ref)):
                for blk in range(W // LANES):
                    cols = slice(blk * LANES, (blk + 1) * LANES)
                    t = t_ref[rows, cols]
                    y = t * lax.rsqrt(_seg_sum(t * t, seg) + EPS) * g_ref[...]
                    o_ref[rows, cols] = y * c + pltpu.roll(y, LANES - half, 1) * sa + pltpu.roll(y, half, 1) * sb

    row = lambda k: pl.BlockSpec((tm, W), lambda m: (m, k))
    gain = pl.BlockSpec((1, LANES), lambda m: (0, 0))
    tab = pl.BlockSpec((tm, LANES), lambda m: (m, 0))
    return pl.pallas_call(
        body, name=name, grid=(S // tm,),
        in_specs=[row(0), row(1), gain, gain, tab, tab, tab], out_specs=[row(0)] * 2,
        out_shape=[jax.ShapeDtypeStruct((S, W), F32)] * 2, compiler_params=_cp("parallel"),
    )(qkv, qkv, gq, gk, *tables)


def _qk_bwd(name, qkv, gq, gk, tables, dqs, dks, dvs):
    S = qkv.shape[0]
    W = N_HEADS * HEAD_DIM
    tm = _tile(S, 256)
    half = ROT_DIM // 2
    n_p = len(dqs)

    def body(q_ref, k_ref, gq_ref, gk_ref, c_ref, sa_ref, sb_ref, *rest):
        dq_refs, dk_refs, dv_refs = rest[:n_p], rest[n_p:2 * n_p], rest[2 * n_p:3 * n_p]
        dqkv_ref, dgq_ref, dgk_ref = rest[3 * n_p:]

        @pl.when(pl.program_id(0) == 0)
        def _():
            dgq_ref[...] = jnp.zeros_like(dgq_ref)
            dgk_ref[...] = jnp.zeros_like(dgk_ref)

        seg = _seg_matrix(1.0 / HEAD_DIM)
        r_i = lax.broadcasted_iota(jnp.int32, (LANES, LANES), 0)
        c_i = lax.broadcasted_iota(jnp.int32, (LANES, LANES), 1)
        same_dim = jnp.where((r_i & (HEAD_DIM - 1)) == (c_i & (HEAD_DIM - 1)), 1.0, 0.0).astype(BF16)
        dgs = [jnp.zeros((8, LANES), F32), jnp.zeros((8, LANES), F32)]
        fold8 = lambda t: jnp.sum(t.reshape(QK_CHUNK // 8, 8, LANES), axis=0)
        for r0 in range(0, tm, QK_CHUNK):
            rows = pl.ds(r0, QK_CHUNK)
            c, sa, sb = c_ref[rows, :], sa_ref[rows, :], sb_ref[rows, :]
            for idx, (t_ref, g_ref, d_refs) in enumerate(((q_ref, gq_ref, dq_refs), (k_ref, gk_ref, dk_refs))):
                for blk in range(W // LANES):
                    cols = slice(blk * LANES, (blk + 1) * LANES)
                    dout = d_refs[0][rows, cols]
                    for r in d_refs[1:]:
                        dout = dout + r[rows, cols]
                    dy = dout * c + pltpu.roll(dout * sa, half, 1) + pltpu.roll(dout * sb, LANES - half, 1)
                    t = t_ref[rows, cols]
                    r_ = lax.rsqrt(_seg_sum(t * t, seg) + EPS)
                    xhat = t * r_
                    dgs[idx] = dgs[idx] + fold8(dy * xhat)
                    dxhat = dy * g_ref[...]
                    dt = r_ * (dxhat - xhat * _seg_sum(dxhat * xhat, seg))
                    dqkv_ref[rows, idx * W + blk * LANES: idx * W + (blk + 1) * LANES] = dt.astype(BF16)
            dv = dv_refs[0][rows, :]
            for r in dv_refs[1:]:
                dv = dv + r[rows, :]
            dqkv_ref[rows, 2 * W:] = dv.astype(BF16)
        for dg, dg_ref in zip(dgs, (dgq_ref, dgk_ref)):
            dg_ref[...] += jnp.sum(_seg_sum(dg, same_dim), axis=0, keepdims=True)

    row = lambda k: pl.BlockSpec((tm, W), lambda m: (m, k))
    gain = pl.BlockSpec((1, LANES), lambda m: (0, 0))
    tab = pl.BlockSpec((tm, LANES), lambda m: (m, 0))
    return pl.pallas_call(
        body, name=name, grid=(S // tm,),
        in_specs=[row(0), row(1), gain, gain, tab, tab, tab] + [row(0)] * (3 * n_p),
        out_specs=[pl.BlockSpec((tm, 3 * W), lambda m: (m, 0)), gain, gain],
        out_shape=[jax.ShapeDtypeStruct((S, 3 * W), BF16), jax.ShapeDtypeStruct((1, LANES), F32),
                   jax.ShapeDtypeStruct((1, LANES), F32)],
        compiler_params=_cp("arbitrary"),
    )(qkv, qkv, gq, gk, *tables, *dqs, *dks, *dvs)


ATTN_BQ = 2 * BAND
ATTN_ROWS = 16 * ATTN_BQ
V_COL = 2 * N_HEADS * HEAD_DIM // LANES


def _attn_geometry(S, d):
    rows = min(ATTN_ROWS, S)
    halo = BAND * d
    assert rows % (ATTN_BQ * d) == 0 and S % rows == 0, (S, d)
    return rows, halo, rows // (ATTN_BQ * d)


def _attn_specs(S, d, col):
    rows, halo, _ = _attn_geometry(S, d)
    r = rows // halo
    n_h = S // halo
    prev = pl.BlockSpec((halo, LANES), lambda j, i: (jnp.maximum(i * r - 1, 0), col + j))
    cur = pl.BlockSpec((rows, LANES), lambda j, i: (i, col + j))
    nxt = pl.BlockSpec((halo, LANES), lambda j, i: (jnp.minimum((i + 1) * r, n_h - 1), col + j))
    return [prev, cur, nxt]


def _fill_window(scr, prev, cur, nxt, rows, halo):
    scr[0:halo, :] = prev[...]
    scr[halo:halo + rows, :] = cur[...]
    scr[halo + rows:2 * halo + rows, :] = nxt[...]


def _chain_groups(n_sb, d, size):
    chains = [(sb, r) for sb in range(n_sb) for r in range(d)]
    return [chains[j:j + size] for j in range(0, len(chains), size)]


def _strided(ref, start, size, d):
    return ref[pl.ds(start, size, stride=d) if d > 1 else pl.ds(start, size), :]


def _band_mask(i, S, d, sb):
    rows, _, _ = _attn_geometry(S, d)
    L = S // d
    base = i * (rows // d) + sb * ATTN_BQ
    wk = ATTN_BQ + 2 * BAND
    row = lax.broadcasted_iota(jnp.int32, (ATTN_BQ, wk), 0)
    col = lax.broadcasted_iota(jnp.int32, (ATTN_BQ, wk), 1)
    lj = base - BAND + col
    return (jnp.abs(col - BAND - row) <= BAND) & (lj >= 0) & (lj < L)


def _attn_fwd(name, q, k, v, v_col, d):
    S, W = q.shape
    rows, halo, n_sb = _attn_geometry(S, d)
    wk = ATTN_BQ + 2 * BAND
    scale = HEAD_DIM ** -0.5

    def body(q_ref, kp, kc, kn, vp, vc, vn, o_ref, lse_ref, kw, vw):
        i = pl.program_id(1)
        _fill_window(kw, kp, kc, kn, rows, halo)
        _fill_window(vw, vp, vc, vn, rows, halo)
        first = _first_head((ATTN_BQ, LANES))
        heads = (first, jnp.logical_not(first))
        zero = jnp.zeros((), BF16)
        for group in _chain_groups(n_sb, d, 4):
            masks = {sb: _band_mask(i, S, d, sb) for sb in sorted({sb for sb, _ in group})}
            starts = [r + d * sb * ATTN_BQ for sb, r in group]
            qs = [_strided(q_ref, st, ATTN_BQ, d).astype(BF16) for st in starts]
            ks = [_strided(kw, st, wk, d).astype(BF16) for st in starts]
            vs = [_strided(vw, st, wk, d).astype(BF16) for st in starts]
            s_all = [[_dot(jnp.where(hm, qv, zero), kv, 1, 1) for hm in heads] for qv, kv in zip(qs, ks)]
            p_all, den_all, lse_all = [], [], []
            for (sb, _), s_h in zip(group, s_all):
                s_h = [jnp.where(masks[sb], s * scale, NEG) for s in s_h]
                mx_h = [jnp.max(s, axis=-1, keepdims=True) for s in s_h]
                p_h = [jnp.exp(s - mx) for s, mx in zip(s_h, mx_h)]
                den_h = [jnp.sum(p, axis=-1, keepdims=True) for p in p_h]
                p_all.append([p.astype(BF16) for p in p_h])
                den_all.append(den_h)
                lse_all.append([mx + jnp.log(den) for mx, den in zip(mx_h, den_h)])
            o_all = [[_dot(p, vv, 1, 0) for p in p_h] for p_h, vv in zip(p_all, vs)]
            for st, o_h, den_h, lse_h in zip(starts, o_all, den_all, lse_all):
                dst = pl.ds(st, ATTN_BQ, stride=d) if d > 1 else pl.ds(st, ATTN_BQ)
                o_ref[dst, :] = jnp.where(first, o_h[0] / den_h[0], o_h[1] / den_h[1])
                lse_ref[dst, :] = jnp.where(first, lse_h[0], lse_h[1])

    cur = _attn_specs(S, d, 0)[1]
    return pl.pallas_call(
        body, name=name, grid=(W // LANES, S // rows),
        in_specs=[cur] + _attn_specs(S, d, 0) + _attn_specs(S, d, v_col), out_specs=[cur, cur],
        out_shape=[jax.ShapeDtypeStruct((S, W), F32)] * 2,
        scratch_shapes=[pltpu.VMEM((rows + 2 * halo, LANES), F32)] * 2,
        compiler_params=_cp("parallel", "parallel"),
    )(q, k, k, k, v, v, v)


def _attn_fwd_by_residue(name, q, k, v, v_col, d):
    S, W = q.shape
    L = S // d
    rows, halo, n_sb = _attn_geometry(L, 1)
    assert rows == L and d % 8 == 0
    wk = ATTN_BQ + 2 * BAND
    scale = HEAD_DIM ** -0.5
    n_steps = (W // LANES) * d
    view = lambda t: t.reshape(L, d // 8, 8, t.shape[1])

    def body(q_hbm, k_hbm, v_hbm, o_hbm, l_hbm, qbuf, kwin, vwin, obuf, lbuf, in_sems, out_sems):
        t = pl.program_id(0)

        def rows_of(ref, step, col0):
            r = step % d
            col = pl.multiple_of((col0 + step // d) * LANES, LANES)
            return ref.at[:, r // 8, r % 8, pl.ds(col, LANES)]

        def loads(step, slot):
            return [pltpu.make_async_copy(rows_of(q_hbm, step, 0), qbuf.at[slot], in_sems.at[slot, 0]),
                    pltpu.make_async_copy(rows_of(k_hbm, step, 0), kwin.at[slot, pl.ds(halo, L)], in_sems.at[slot, 1]),
                    pltpu.make_async_copy(rows_of(v_hbm, step, v_col), vwin.at[slot, pl.ds(halo, L)], in_sems.at[slot, 2])]

        def stores(step, slot):
            return [pltpu.make_async_copy(obuf.at[slot], rows_of(o_hbm, step, 0), out_sems.at[slot, 0]),
                    pltpu.make_async_copy(lbuf.at[slot], rows_of(l_hbm, step, 0), out_sems.at[slot, 1])]

        slot = t % 2

        @pl.when(t == 0)
        def _():
            for s in range(2):
                for win in (kwin, vwin):
                    win[s, 0:halo, :] = jnp.zeros((halo, LANES), F32)
                    win[s, halo + L:2 * halo + L, :] = jnp.zeros((halo, LANES), F32)
            for cp in loads(0, 0):
                cp.start()

        @pl.when(t + 1 < n_steps)
        def _():
            for cp in loads(t + 1, 1 - slot):
                cp.start()

        for cp in loads(t, slot):
            cp.wait()

        @pl.when(t >= 2)
        def _():
            for cp in stores(t - 2, slot):
                cp.wait()

        q_ref, kw, vw, o_ref, lse_ref = qbuf.at[slot], kwin.at[slot], vwin.at[slot], obuf.at[slot], lbuf.at[slot]
        first = _first_head((ATTN_BQ, LANES))
        heads = (first, jnp.logical_not(first))
        zero = jnp.zeros((), BF16)
        for group in _chain_groups(n_sb, 1, 4):
            starts = [sb * ATTN_BQ for sb, _ in group]
            qs = [q_ref[pl.ds(st, ATTN_BQ), :].astype(BF16) for st in starts]
            ks = [kw[pl.ds(st, wk), :].astype(BF16) for st in starts]
            vs = [vw[pl.ds(st, wk), :].astype(BF16) for st in starts]
            s_all = [[_dot(jnp.where(hm, qv, zero), kv, 1, 1) for hm in heads] for qv, kv in zip(qs, ks)]
            p_all, den_all, lse_all = [], [], []
            for (sb, _), s_h in zip(group, s_all):
                valid = _band_mask(0, L, 1, sb)
                s_h = [jnp.where(valid, s * scale, NEG) for s in s_h]
                mx_h = [jnp.max(s, axis=-1, keepdims=True) for s in s_h]
                p_h = [jnp.exp(s - mx) for s, mx in zip(s_h, mx_h)]
                den_h = [jnp.sum(p, axis=-1, keepdims=True) for p in p_h]
                p_all.append([p.astype(BF16) for p in p_h])
                den_all.append(den_h)
                lse_all.append([mx + jnp.log(den) for mx, den in zip(mx_h, den_h)])
            o_all = [[_dot(p, vv, 1, 0) for p in p_h] for p_h, vv in zip(p_all, vs)]
            for st, o_h, den_h, lse_h in zip(starts, o_all, den_all, lse_all):
                o_ref[pl.ds(st, ATTN_BQ), :] = jnp.where(first, o_h[0] / den_h[0], o_h[1] / den_h[1])
                lse_ref[pl.ds(st, ATTN_BQ), :] = jnp.where(first, lse_h[0], lse_h[1])

        for cp in stores(t, slot):
            cp.start()

        @pl.when(t == n_steps - 1)
        def _():
            if n_steps >= 2:
                for cp in stores(t - 1, 1 - slot):
                    cp.wait()
            for cp in stores(t, slot):
                cp.wait()

    outs = pl.pallas_call(
        body, name=name, grid=(n_steps,), in_specs=[ANY] * 3, out_specs=[ANY] * 2,
        out_shape=[jax.ShapeDtypeStruct((L, d // 8, 8, W), F32)] * 2,
        scratch_shapes=[pltpu.VMEM((2, L, LANES), F32), pltpu.VMEM((2, L + 2 * halo, LANES), F32),
                        pltpu.VMEM((2, L + 2 * halo, LANES), F32), pltpu.VMEM((2, L, LANES), F32),
                        pltpu.VMEM((2, L, LANES), F32), pltpu.SemaphoreType.DMA((2, 3)), pltpu.SemaphoreType.DMA((2, 2))],
        compiler_params=_cp("arbitrary"),
    )(view(q), view(k), view(v))
    return [o.reshape(S, W) for o in outs]


def _merge_out_proj(name, os, lses, w, x, g3, layer):
    S, W = os[0].shape
    tm = _tile(S, 256)
    n_p = len(os)

    def body(*refs):
        o_refs, l_refs = refs[:n_p], refs[n_p:2 * n_p]
        w_ref, x_ref, g_ref, xo_ref, h_ref, o_ref, lt_ref = refs[2 * n_p:]
        ls = [r[...] for r in l_refs]
        mx = functools.reduce(jnp.maximum, ls)
        es = [jnp.exp(l - mx) for l in ls]
        den = functools.reduce(lambda a, b: a + b, es)
        acc = es[0] * o_refs[0][...]
        for e, r in zip(es[1:], o_refs[1:]):
            acc = acc + e * r[...]
        o = (acc / den).astype(BF16)
        o_ref[...] = o
        lt_ref[...] = mx + jnp.log(den)
        y, h = _add_norm_epilogue(_dot(o, w_ref[0], 1, 0), x_ref[...], g_ref[...])
        xo_ref[...] = y
        h_ref[...] = h.astype(BF16)

    row = pl.BlockSpec((tm, W), lambda m: (m, 0))
    return pl.pallas_call(
        body, name=name, grid=(S // tm,),
        in_specs=[row] * (2 * n_p) + [pl.BlockSpec(w.shape, lambda m: (0, 0, 0)), row,
                                      pl.BlockSpec((None, 1, W), lambda m: (layer, 0, 0))],
        out_specs=[row] * 4,
        out_shape=[jax.ShapeDtypeStruct((S, W), dt) for dt in (F32, BF16, BF16, F32)],
        compiler_params=_cp("parallel"),
    )(*os, *lses, w, x, g3)


def _delta_epilogue(do, o):
    seg = _seg_matrix(1.0)
    prod = do * o.astype(F32)
    delta = [_seg_sum(prod[:, blk * LANES:(blk + 1) * LANES], seg) for blk in range(do.shape[1] // LANES)]
    return do, jnp.concatenate(delta, axis=1)


def _attn_bwd_q(name, q, k, v, v_col, do, lse, delta, d):
    S, W = q.shape
    rows, halo, n_sb = _attn_geometry(S, d)
    wk = ATTN_BQ + 2 * BAND
    scale = HEAD_DIM ** -0.5

    def body(q_ref, do_ref, l_ref, dl_ref, kp, kc, kn, vp, vc, vn, dq_ref, kw, vw):
        i = pl.program_id(1)
        _fill_window(kw, kp, kc, kn, rows, halo)
        _fill_window(vw, vp, vc, vn, rows, halo)
        first = _first_head((ATTN_BQ, LANES))
        heads = (first, jnp.logical_not(first))
        zero = jnp.zeros((), BF16)
        wide = lambda t: jnp.concatenate([t] * (wk // LANES), axis=1)
        for group in _chain_groups(n_sb, d, 4):
            masks = {sb: _band_mask(i, S, d, sb) for sb in sorted({sb for sb, _ in group})}
            starts = [r + d * sb * ATTN_BQ for sb, r in group]
            qs = [_strided(q_ref, st, ATTN_BQ, d).astype(BF16) for st in starts]
            dos = [_strided(do_ref, st, ATTN_BQ, d).astype(BF16) for st in starts]
            ks = [_strided(kw, st, wk, d).astype(BF16) for st in starts]
            vs = [_strided(vw, st, wk, d).astype(BF16) for st in starts]
            s_all = [[_dot(jnp.where(hm, qv, zero), kv, 1, 1) for hm in heads] for qv, kv in zip(qs, ks)]
            dp_all = [[_dot(jnp.where(hm, dov, zero), vv, 1, 1) for hm in heads] for dov, vv in zip(dos, vs)]
            ds_all = []
            for (sb, _), st, s_h, dp_h in zip(group, starts, s_all, dp_all):
                lv, dlv = _strided(l_ref, st, ATTN_BQ, d), _strided(dl_ref, st, ATTN_BQ, d)
                l_sw, dl_sw = pltpu.roll(lv, HEAD_DIM, 1), pltpu.roll(dlv, HEAD_DIM, 1)
                ds_h = []
                for hm, s, dp in zip(heads, s_h, dp_h):
                    p = jnp.exp(jnp.where(masks[sb], s * scale, NEG) - wide(jnp.where(hm, lv, l_sw)))
                    ds_h.append((p * (dp - wide(jnp.where(hm, dlv, dl_sw))) * scale).astype(BF16))
                ds_all.append(ds_h)
            dq_all = [[_dot(ds, kv, 1, 0) for ds in ds_h] for ds_h, kv in zip(ds_all, ks)]
            for st, dq_h in zip(starts, dq_all):
                dst = pl.ds(st, ATTN_BQ, stride=d) if d > 1 else pl.ds(st, ATTN_BQ)
                dq_ref[dst, :] = jnp.where(first, dq_h[0], dq_h[1])

    cur = _attn_specs(S, d, 0)[1]
    return pl.pallas_call(
        body, name=name, grid=(W // LANES, S // rows),
        in_specs=[cur] * 4 + _attn_specs(S, d, 0) + _attn_specs(S, d, v_col), out_specs=cur,
        out_shape=jax.ShapeDtypeStruct((S, W), F32),
        scratch_shapes=[pltpu.VMEM((rows + 2 * halo, LANES), F32)] * 2,
        compiler_params=_cp("parallel", "parallel"),
    )(q, do, lse, delta, k, k, k, v, v, v)


def _attn_bwd_kv(name, q, k, v, v_col, do, lse, delta, d):
    S, W = q.shape
    rows, halo, n_sb = _attn_geometry(S, d)
    wk = ATTN_BQ + 2 * BAND
    scale = HEAD_DIM ** -0.5

    def body(k_ref, v_ref, qp, qc, qn, dop, doc, don, lp, lc, ln, dlp, dlc, dln, dk_ref, dv_ref, qw, dow, lw, dlw):
        i = pl.program_id(1)
        _fill_window(qw, qp, qc, qn, rows, halo)
        _fill_window(dow, dop, doc, don, rows, halo)
        _fill_window(lw, lp, lc, ln, rows, halo)
        _fill_window(dlw, dlp, dlc, dln, rows, halo)
        first = _first_head((ATTN_BQ, LANES))
        heads = (first, jnp.logical_not(first))
        zero = jnp.zeros((), BF16)
        for group in _chain_groups(n_sb, d, 2):
            masks = {sb: _band_mask(i, S, d, sb) for sb in sorted({sb for sb, _ in group})}
            starts = [r + d * sb * ATTN_BQ for sb, r in group]
            ks = [_strided(k_ref, st, ATTN_BQ, d).astype(BF16) for st in starts]
            vs = [_strided(v_ref, st, ATTN_BQ, d).astype(BF16) for st in starts]
            qs = [_strided(qw, st, wk, d).astype(BF16) for st in starts]
            dos = [_strided(dow, st, wk, d).astype(BF16) for st in starts]
            s_all = [[_dot(jnp.where(hm, kv, zero), qv, 1, 1) for hm in heads] for kv, qv in zip(ks, qs)]
            dp_all = [[_dot(jnp.where(hm, vv, zero), dov, 1, 1) for hm in heads] for vv, dov in zip(vs, dos)]
            p_all, ds_all = [], []
            for (sb, _), st, s_h, dp_h in zip(group, starts, s_all, dp_all):
                l_t, dl_t = _strided(lw, st, wk, d).T, _strided(dlw, st, wk, d).T
                p_h = [jnp.exp(jnp.where(masks[sb], s * scale, NEG) - l_t[hh * HEAD_DIM:hh * HEAD_DIM + 1, :])
                       for hh, s in enumerate(s_h)]
                ds_all.append([(p * (dp - dl_t[hh * HEAD_DIM:hh * HEAD_DIM + 1, :]) * scale).astype(BF16)
                               for hh, (p, dp) in enumerate(zip(p_h, dp_h))])
                p_all.append([p.astype(BF16) for p in p_h])
            dv_all = [[_dot(p, dov, 1, 0) for p in p_h] for p_h, dov in zip(p_all, dos)]
            dk_all = [[_dot(ds, qv, 1, 0) for ds in ds_h] for ds_h, qv in zip(ds_all, qs)]
            for st, dk_h, dv_h in zip(starts, dk_all, dv_all):
                dst = pl.ds(st, ATTN_BQ, stride=d) if d > 1 else pl.ds(st, ATTN_BQ)
                dk_ref[dst, :] = jnp.where(first, dk_h[0], dk_h[1])
                dv_ref[dst, :] = jnp.where(first, dv_h[0], dv_h[1])

    cur = _attn_specs(S, d, 0)[1]
    win = _attn_specs(S, d, 0)
    return pl.pallas_call(
        body, name=name, grid=(W // LANES, S // rows),
        in_specs=[cur, _attn_specs(S, d, v_col)[1]] + win * 4, out_specs=[cur, cur],
        out_shape=[jax.ShapeDtypeStruct((S, W), F32)] * 2,
        scratch_shapes=[pltpu.VMEM((rows + 2 * halo, LANES), F32)] * 4,
        compiler_params=_cp("parallel", "parallel"),
    )(k, v, q, q, q, do, do, do, lse, lse, lse, delta, delta, delta)


def _place():
    x, y, c = lax.axis_index("x"), lax.axis_index("y"), lax.axis_index("c")
    chips = [(1 - x, y), (x, 1 - y), (1 - x, 1 - y)]
    return x, y, c, chips


HBM = pl.BlockSpec(memory_space=pltpu.HBM)
SEM = pl.BlockSpec(memory_space=pltpu.SEMAPHORE)
DATAFLOW = pltpu.SideEffectType.DATAFLOW_SIDE_EFFECTING


N_PEERS = {"gather": 3, "scatter": 7, "allgather": 7}


def _exchange_copies(kind, srcs, dsts, send_sems, recv_sems):
    x, y, c, chips = _place()
    mine = 2 * x + y
    n_peers = N_PEERS[kind]
    cps = []
    for t in range(len(srcs)):
        for k in range(n_peers):
            if kind == "gather":
                (px, py), pc = chips[k], c
                src, dst = srcs[t], dsts[t].at[mine]
            else:
                bits = k + 1
                px, py, pc = (1 - x if bits & 4 else x), (1 - y if bits & 2 else y), (1 - c if bits & 1 else c)
                src, dst = (srcs[t].at[2 * px + py] if kind == "scatter" else srcs[t]), dsts[t].at[k]
            cps.append(pltpu.make_async_remote_copy(
                src_ref=src, dst_ref=dst, send_sem=send_sems.at[n_peers * t + k], recv_sem=recv_sems.at[n_peers * t + k],
                device_id=(px, py, pc), device_id_type=MESH))
    return cps


def _exchange_start(name, kind, groups):
    sizes = [len(g) for g in groups]
    n, n_g = sum(sizes), len(groups)

    def body(*refs):
        srcs, dsts = refs[:n], refs[n:2 * n]
        sems = refs[2 * n:2 * n + 2 * n_g]
        token = refs[4 * n + 2 * n_g]
        off = 0
        for gi, size in enumerate(sizes):
            for cp in _exchange_copies(kind, srcs[off:off + size], dsts[off:off + size], sems[2 * gi], sems[2 * gi + 1]):
                cp.start()
            off += size
        token[...] = jnp.zeros_like(token)

    arrays = [pltpu.with_memory_space_constraint(a, pltpu.HBM) for a in
              [s for g in groups for s, _ in g] + [d for g in groups for _, d in g]]
    sem_shapes = []
    for size in sizes:
        sem_shapes += [pltpu.SemaphoreType.DMA((N_PEERS[kind] * size,))] * 2
    outs = pl.pallas_call(
        body, name=name,
        in_specs=[HBM] * (2 * n),
        out_specs=[SEM] * (2 * n_g) + [HBM] * (2 * n) + [pl.BlockSpec(memory_space=pltpu.VMEM)],
        out_shape=sem_shapes + [pltpu.HBM(a.shape, a.dtype) for a in arrays] + [jax.ShapeDtypeStruct((8, LANES), F32)],
        input_output_aliases={t: 2 * n_g + t for t in range(2 * n)},
        compiler_params=pltpu.CompilerParams(has_side_effects=DATAFLOW),
    )(*arrays)
    sems, thru, token = outs[:2 * n_g], outs[2 * n_g:-1], outs[-1]
    handles, off = [], 0
    for gi, size in enumerate(sizes):
        handles.append((sems[2 * gi], sems[2 * gi + 1], thru[off:off + size], thru[n + off:n + off + size]))
        off += size
    return handles, token


def _exchange_wait(name, kind, handle, after):
    send_sems, recv_sems, srcs, dsts = handle
    n = len(srcs)

    def body(*refs):
        for cp in _exchange_copies(kind, refs[:n], refs[n:2 * n], refs[2 * n], refs[2 * n + 1]):
            cp.wait_send()
            cp.wait_recv()

    outs = pl.pallas_call(
        body, name=name,
        in_specs=[HBM] * (2 * n) + [SEM, SEM, ANY], out_specs=[HBM] * (2 * n),
        out_shape=[pltpu.HBM(a.shape, a.dtype) for a in (*srcs, *dsts)],
        input_output_aliases={t: t for t in range(2 * n)},
        compiler_params=pltpu.CompilerParams(has_side_effects=DATAFLOW),
    )(*srcs, *dsts, send_sems, recv_sems, after)
    return outs[:n], outs[n:]


def _prepare_shard(name, w, idx, dtype, mine, anchor=None):
    _, R, C = w.shape
    tr = _row_tile(R)
    anchors = [] if anchor is None else [anchor]

    def body(mine_ref, w_ref, *rest):
        src_ref, land_ref = rest[len(anchors):]
        val = w_ref[...].astype(dtype)
        src_ref[...] = val
        land_ref[...] = val

    return pl.pallas_call(
        body, name=name,
        grid_spec=pltpu.PrefetchScalarGridSpec(
            num_scalar_prefetch=1, grid=(R // tr,),
            in_specs=[pl.BlockSpec((None, tr, C), lambda i, s: (idx, i, 0))]
            + [pl.BlockSpec((8, LANES), lambda i, s: (0, 0))] * len(anchors),
            out_specs=[pl.BlockSpec((tr, C), lambda i, s: (i, 0)), pl.BlockSpec((None, tr, C), lambda i, s: (s[0], i, 0))]),
        out_shape=[jax.ShapeDtypeStruct((R, C), dtype), jax.ShapeDtypeStruct((N_SHARDS, R, C), dtype)],
        compiler_params=_cp("parallel"),
    )(mine, w, *anchors)


def _ordered_sum(name, own, recv, order):
    rows, C = own.shape
    tr = _row_tile(rows)
    n_s = len(order)

    def body(*refs):
        me = refs[1][0]
        own_ref, theirs, out_ref = refs[n_s], refs[n_s + 1:n_s + 9], refs[n_s + 9]
        g = None
        for p in range(8):
            term = jnp.where(me == p, own_ref[...], theirs[p][...])
            g = term if g is None else g + term
        out_ref[...] = g

    blk = pl.BlockSpec((tr, C), lambda i, *s: (i, 0))
    slot = lambda p: pl.BlockSpec((None, tr, C), lambda i, *s: (jnp.maximum(s[2 + p][0], 0), i, 0))
    return pl.pallas_call(
        body, name=name,
        grid_spec=pltpu.PrefetchScalarGridSpec(num_scalar_prefetch=n_s, grid=(rows // tr,),
                                               in_specs=[blk] + [slot(p) for p in range(8)], out_specs=blk),
        out_shape=jax.ShapeDtypeStruct((rows, C), F32), compiler_params=_cp("parallel"),
    )(*order, own, *([recv] * 8))


MM_TM_K = 512
WGRAD_TM = 2048


def _rows_merged(w):
    return w.reshape(1, w.shape[0] * w.shape[1], w.shape[2])


def _sq_relu_epilogue(acc):
    r = jnp.maximum(acc, 0.0)
    return acc, r * r


def _add_epilogue(acc, x):
    return (acc + x,)


def _add_loss_epilogue(acc, x, target):
    e = acc + x - target
    D = e.shape[1]
    share = (0.5 / D) * jnp.sum(jnp.sum(e * e, axis=1, keepdims=True), axis=0, keepdims=True)
    return e * (1.0 / D), jnp.broadcast_to(share, (1, D))


def _add_norm_epilogue(acc, x, g):
    y = acc + x
    r = lax.rsqrt(jnp.mean(y * y, axis=-1, keepdims=True) + EPS)
    return y, y * r * g


def _norm_bwd_epilogue(dh, x, dres, g):
    r = lax.rsqrt(jnp.mean(x * x, axis=-1, keepdims=True) + EPS)
    xhat = x * r
    dxhat = dh * g
    dx = dres + r * (dxhat - xhat * jnp.mean(dxhat * xhat, axis=-1, keepdims=True))
    return dx, jnp.sum(dh * xhat, axis=0, keepdims=True)


def _sq_relu_grad_epilogue(acc, a):
    return (acc * (2.0 * jnp.maximum(a.astype(F32), 0.0)),)


STAGES = ("mixer_in", "mixer_out", "mlp")


def _stage_tensors(layer, stage):
    i = layer // 2
    if stage == "mlp":
        return [("mlp_w1", layer), ("mlp_w2", layer)]
    if stage == "mixer_in":
        return [("ab_w_in", i)] if layer % 2 == 0 else [("c_w_qkv", i)]
    return [("b_conv_w", i), ("ab_w_out", i)] if layer % 2 == 0 else [("c_w_out", i)]


def _local_step(x, target, p, weights_of, grads_done):
    S, D = x.shape
    depth = p["mix_norm_g"].shape[0]
    n_even = (depth + 1) // 2
    mix_g3 = p["mix_norm_g"].reshape(depth, 1, D)
    mlp_g3 = p["mlp_norm_g"].reshape(depth, 1, D)
    vec3 = lambda t: t.reshape(t.shape[0], 1, t.shape[1])
    spw16 = p["a_spatial_w"].astype(BF16)
    spw16_t = jnp.swapaxes(spw16, 2, 3)
    bias_full = jnp.repeat(jnp.swapaxes(p["a_spatial_b"], 1, 2), HEAD_DIM, axis=2)
    vn_g, vn_b, cn_g, cn_b, cb3 = (vec3(p[k]) for k in ("a_vnorm_g", "a_vnorm_b", "b_norm_g", "b_norm_b", "b_conv_b"))
    tables = _rope_tables(S)
    gq = jnp.tile(p["c_q_norm_g"], (1, 2))
    gk = jnp.tile(p["c_k_norm_g"], (1, 2))

    saved = []
    h = _rms_fwd("mix_norm_0", x, mix_g3, 0)
    for layer in range(depth):
        i = layer // 2
        wl = dict(weights_of(layer, "mixer_in", x))
        rec = {"x_mix": x, "w": wl, "h_mix": h}
        if layer % 2 == 0:
            (z,) = _mm_ngroup(f"ab_in_{layer}", h, wl["ab_w_in"], nt=False, tm=MM_TM_K, out_dtypes=[F32])
            wl.update(weights_of(layer, "mixer_out", z))
            gconv = _glu_conv_fwd(f"glu_conv_{layer}", z, wl["b_conv_w"], cb3, i)
            x, h, cat = _ab_tail_out_proj(f"ab_out_{layer}", z, gconv, spw16, bias_full, vn_g, vn_b, cn_g, cn_b, i,
                                          _rows_merged(wl["ab_w_out"]), x, mlp_g3, layer)
            rec.update(z=z, gconv=gconv, cat=cat)
        else:
            (qkv,) = _mm_ngroup(f"c_qkv_{layer}", h, wl["c_w_qkv"], nt=False, tm=MM_TM_K, out_dtypes=[F32])
            wl.update(weights_of(layer, "mixer_out", qkv))
            qn, kn = _qk_fwd(f"qk_norm_rope_{layer}", qkv, gq[i:i + 1], gk[i:i + 1], tables)
            os, lses = zip(*[(_attn_fwd_by_residue if d % 8 == 0 else _attn_fwd)(f"attn_d{d}_{layer}", qn, kn, qkv, V_COL, d)
                             for d in PATTERN_DILATIONS])
            x, h, o, lse = _merge_out_proj(f"c_out_{layer}", os, lses, _rows_merged(wl["c_w_out"]), x, mlp_g3, layer)
            rec.update(qkv=qkv, qn=qn, kn=kn, o=o, lse=lse)
        rec["x_mlp"] = x
        wl.update(weights_of(layer, "mlp", x))
        a, hsq = _mm_ngroup(f"mlp_up_{layer}", h, wl["mlp_w1"], nt=False, tm=MM_TM_K, out_dtypes=[BF16, BF16],
                            epilogue=_sq_relu_epilogue)
        rec.update(h_mlp=h, a=a, hsq=hsq)
        if layer + 1 < depth:
            x, h = _mm_kgroup(f"mlp_down_{layer}", hsq, _rows_merged(wl["mlp_w2"]), nt=False, tm=MM_TM_K,
                              out_dtypes=[F32, BF16], extras=(x,), vecs=[(mix_g3, layer + 1)], epilogue=_add_norm_epilogue)
        else:
            dx, loss_row = _mm_kgroup(f"mlp_down_{layer}", hsq, _rows_merged(wl["mlp_w2"]), nt=False, tm=MM_TM_K,
                                      out_dtypes=[F32], extras=(x, target), n_sums=1, epilogue=_add_loss_epilogue)
        saved.append(rec)

    small = {k: [None] * v.shape[0] for k, v in p.items()}
    token = None
    for layer in reversed(range(depth)):
        i = layer // 2
        rec = saved[layer]
        wl = rec["w"]
        g = {}
        (da,) = _mm_ngroup(f"mlp_down_dgrad_{layer}", dx, wl["mlp_w2"], nt=True, tm=MM_TM_K, out_dtypes=[BF16],
                           extras=(rec["a"],), epilogue=_sq_relu_grad_epilogue, anchor=token)
        g["mlp_w2"] = _wgrad(f"mlp_down_wgrad_{layer}", rec["hsq"], dx, wl["mlp_w2"].shape, a_group=True, tm=WGRAD_TM)
        g["mlp_w1"] = _wgrad(f"mlp_up_wgrad_{layer}", rec["h_mlp"], da, wl["mlp_w1"].shape, a_group=False, tm=WGRAD_TM)
        dx, small["mlp_norm_g"][layer] = _mm_kgroup(
            f"mlp_up_dgrad_{layer}", da, wl["mlp_w1"], nt=True, tm=MM_TM_K, out_dtypes=[F32], extras=(rec["x_mlp"], dx),
            vecs=[(mlp_g3, layer)], n_sums=1, epilogue=_norm_bwd_epilogue)
        token = grads_done(layer, "mlp", g)
        g = {}
        if layer % 2 == 0:
            w_out = _rows_merged(wl["ab_w_out"])
            (dcat,) = _mm_ngroup(f"ab_out_dgrad_{layer}", dx, w_out, nt=True, tm=MM_TM_K, out_dtypes=[F32], anchor=token)
            g["ab_w_out"] = _wgrad(f"ab_out_wgrad_{layer}", rec["cat"], dx, w_out.shape, a_group=True,
                                   tm=WGRAD_TM).reshape(wl["ab_w_out"].shape)
            dz, dgconv, dspw, dbias, dvg, dvb, dcg, dcb = _ab_tail_bwd(
                f"ab_tail_bwd_{layer}", rec["z"], rec["gconv"], dcat, spw16, spw16_t, bias_full, vn_g, vn_b, cn_g, cn_b, i)
            dz, g["b_conv_w"], dcbias = _glu_conv_bwd(f"glu_conv_bwd_{layer}", rec["z"], dgconv, dz, wl["b_conv_w"])
            token = grads_done(layer, "mixer_out", g)
            g = {}
            small["a_spatial_w"][i] = dspw
            small["a_spatial_b"][i] = _fold_bias(dbias)[:, :A_GROUPS].T
            for k, val in (("a_vnorm_g", dvg), ("a_vnorm_b", dvb), ("b_norm_g", dcg), ("b_norm_b", dcb), ("b_conv_b", dcbias)):
                small[k][i] = val
            g["ab_w_in"] = _wgrad(f"ab_in_wgrad_{layer}", rec["h_mix"], dz, wl["ab_w_in"].shape, a_group=False, tm=WGRAD_TM,
                                  anchor=token)
            dgrad = (f"ab_in_dgrad_{layer}", dz, wl["ab_w_in"])
        else:
            w_out = _rows_merged(wl["c_w_out"])
            do, delta = _mm_ngroup(f"c_out_dgrad_{layer}", dx, w_out, nt=True, tm=MM_TM_K, out_dtypes=[F32, F32],
                                   extras=(rec["o"],), epilogue=_delta_epilogue, anchor=token)
            g["c_w_out"] = _wgrad(f"c_out_wgrad_{layer}", rec["o"], dx, w_out.shape, a_group=True,
                                  tm=WGRAD_TM).reshape(wl["c_w_out"].shape)
            token = grads_done(layer, "mixer_out", g)
            g = {}
            attn_args = (rec["qn"], rec["kn"], rec["qkv"], V_COL, do, rec["lse"], delta)
            dqs = [_attn_bwd_q(f"attn_bwd_q_d{d}_{layer}", *attn_args, d) for d in PATTERN_DILATIONS]
            dks, dvs = zip(*[_attn_bwd_kv(f"attn_bwd_kv_d{d}_{layer}", *attn_args, d) for d in PATTERN_DILATIONS])
            dqkv, dgq, dgk = _qk_bwd(f"qk_norm_rope_bwd_{layer}", rec["qkv"], gq[i:i + 1], gk[i:i + 1], tables, dqs, dks, dvs)
            small["c_q_norm_g"][i] = dgq[:, :HEAD_DIM]
            small["c_k_norm_g"][i] = dgk[:, :HEAD_DIM]
            g["c_w_qkv"] = _wgrad(f"c_qkv_wgrad_{layer}", rec["h_mix"], dqkv, wl["c_w_qkv"].shape, a_group=False, tm=WGRAD_TM,
                                  anchor=token)
            dgrad = (f"c_qkv_dgrad_{layer}", dqkv, wl["c_w_qkv"])
        token = grads_done(layer, "mixer_in", g)
        dx, small["mix_norm_g"][layer] = _mm_kgroup(
            *dgrad, nt=True, tm=MM_TM_K, out_dtypes=[F32], extras=(rec["x_mix"], dx), vecs=[(mix_g3, layer)], n_sums=1,
            epilogue=_norm_bwd_epilogue, anchor=token)

    small = {k: jnp.stack([t.reshape(p[k].shape[1:]) for t in v]) for k, v in small.items()}
    return loss_row, dx, small


SHARDED = ("mlp_w1", "mlp_w2", "ab_w_in", "b_conv_w", "ab_w_out", "c_w_qkv", "c_w_out")
SMALL = ("mix_norm_g", "mlp_norm_g", "a_spatial_w", "a_spatial_b", "a_vnorm_g", "a_vnorm_b", "b_conv_b", "b_norm_g",
         "b_norm_b", "c_q_norm_g", "c_k_norm_g")
WEIGHTS = ("mix_norm_g", "mlp_norm_g", "mlp_w1", "mlp_w2", "ab_w_in", "a_spatial_w", "a_spatial_b", "a_vnorm_g",
           "a_vnorm_b", "b_conv_w", "b_conv_b", "b_norm_g", "b_norm_b", "ab_w_out", "c_w_qkv", "c_q_norm_g",
           "c_k_norm_g", "c_w_out")


def _pack(parts):
    flat = jnp.concatenate([parts[k].reshape(-1) for k in SMALL])
    rows = -(-flat.shape[0] // (256 * LANES)) * 256
    return jnp.pad(flat, (0, rows * LANES - flat.shape[0])).reshape(rows, LANES)


def _unpack(packed, like):
    flat = packed.reshape(-1)
    out, off = {}, 0
    for k in SMALL:
        n = like[k].size
        out[k] = flat[off:off + n].reshape(like[k].shape)
        off += n
    return out


def kernel(x, mix_norm_g, mlp_norm_g, mlp_w1, mlp_w2, ab_w_in, a_spatial_w, a_spatial_b, a_vnorm_g, a_vnorm_b, b_conv_w, b_conv_b, b_norm_g, b_norm_b, ab_w_out, c_w_qkv, c_q_norm_g, c_k_norm_g, c_w_out, loss_target, m_mix_norm_g, m_mlp_norm_g, m_mlp_w1, m_mlp_w2, m_ab_w_in, m_a_spatial_w, m_a_spatial_b, m_a_vnorm_g, m_a_vnorm_b, m_b_conv_w, m_b_conv_b, m_b_norm_g, m_b_norm_b, m_ab_w_out, m_c_w_qkv, m_c_q_norm_g, m_c_k_norm_g, m_c_w_out, v_mix_norm_g, v_mlp_norm_g, v_mlp_w1, v_mlp_w2, v_ab_w_in, v_a_spatial_w, v_a_spatial_b, v_a_vnorm_g, v_a_vnorm_b, v_b_conv_w, v_b_conv_b, v_b_norm_g, v_b_norm_b, v_ab_w_out, v_c_w_qkv, v_c_q_norm_g, v_c_k_norm_g, v_c_w_out):
    w = dict(mix_norm_g=mix_norm_g, mlp_norm_g=mlp_norm_g, mlp_w1=mlp_w1, mlp_w2=mlp_w2, ab_w_in=ab_w_in,
             a_spatial_w=a_spatial_w, a_spatial_b=a_spatial_b, a_vnorm_g=a_vnorm_g, a_vnorm_b=a_vnorm_b,
             b_conv_w=b_conv_w, b_conv_b=b_conv_b, b_norm_g=b_norm_g, b_norm_b=b_norm_b, ab_w_out=ab_w_out,
             c_w_qkv=c_w_qkv, c_q_norm_g=c_q_norm_g, c_k_norm_g=c_k_norm_g, c_w_out=c_w_out)
    m = dict(mix_norm_g=m_mix_norm_g, mlp_norm_g=m_mlp_norm_g, mlp_w1=m_mlp_w1, mlp_w2=m_mlp_w2, ab_w_in=m_ab_w_in,
             a_spatial_w=m_a_spatial_w, a_spatial_b=m_a_spatial_b, a_vnorm_g=m_a_vnorm_g, a_vnorm_b=m_a_vnorm_b,
             b_conv_w=m_b_conv_w, b_conv_b=m_b_conv_b, b_norm_g=m_b_norm_g, b_norm_b=m_b_norm_b, ab_w_out=m_ab_w_out,
             c_w_qkv=m_c_w_qkv, c_q_norm_g=m_c_q_norm_g, c_k_norm_g=m_c_k_norm_g, c_w_out=m_c_w_out)
    v = dict(mix_norm_g=v_mix_norm_g, mlp_norm_g=v_mlp_norm_g, mlp_w1=v_mlp_w1, mlp_w2=v_mlp_w2, ab_w_in=v_ab_w_in,
             a_spatial_w=v_a_spatial_w, a_spatial_b=v_a_spatial_b, a_vnorm_g=v_a_vnorm_g, a_vnorm_b=v_a_vnorm_b,
             b_conv_w=v_b_conv_w, b_conv_b=v_b_conv_b, b_norm_g=v_b_norm_g, b_norm_b=v_b_norm_b, ab_w_out=v_ab_w_out,
             c_w_qkv=v_c_w_qkv, c_q_norm_g=v_c_q_norm_g, c_k_norm_g=v_c_k_norm_g, c_w_out=v_c_w_out)

    S, D = x.shape[1], x.shape[2]
    depth = mix_norm_g.shape[0]
    mine = (2 * lax.axis_index("x") + lax.axis_index("y")).astype(jnp.int32).reshape(1)

    stages = [(layer, stage) for layer in range(depth) for stage in STAGES]

    def start_gather(name, some_stages, anchor):
        groups = [[_prepare_shard(f"prepare_{k}_{i}", w[k], i, F32 if k == "b_conv_w" else BF16, mine, anchor)
                   for k, i in _stage_tensors(*st)] for st in some_stages]
        return _exchange_start(name, "gather", groups)

    first, rest = stages[:len(STAGES)], stages[len(STAGES):]
    handles_first, token_first = start_gather("gather_weights_start_first", first, None)
    handles_rest, gather_token = start_gather("gather_weights_start_rest", rest, token_first)
    handles = dict(zip(first + rest, handles_first + handles_rest))

    def weights_of(layer, stage, after):
        _, got = _exchange_wait(f"gather_weights_wait_{layer}_{stage}", "gather", handles[layer, stage],
                                gather_token if (layer, stage) == stages[0] else after)
        return {k: a for (k, _), a in zip(_stage_tensors(layer, stage), got)}

    scattered = {}

    def grads_done(layer, stage, g):
        names = [k for k, _ in _stage_tensors(layer, stage)]
        group = [(g[k], lax.empty((N_PEERS["scatter"],) + g[k].shape[1:], BF16)) for k in names]
        (scattered[layer, stage],), token = _exchange_start(f"scatter_grads_start_{layer}_{stage}", "scatter", [group])
        return token

    small_params = {k: w[k] for k in SMALL}
    loss_row, dx, small_grads = _local_step(x.reshape(S, D), loss_target.reshape(S, D), small_params, weights_of, grads_done)

    loss = lax.psum(loss_row[0, 0], ("x", "y", "c"))

    packed = _pack(small_grads)
    (small_handle,), small_token = _exchange_start(
        "allgather_small_grads_start", "allgather", [[(packed, lax.empty((N_PEERS["allgather"],) + packed.shape, F32))]])

    order = _sum_order()
    stacked = {k: [lax.empty(w[k].shape, F32) for _ in range(4)] for k in SHARDED}
    for layer, stage in reversed(stages):
        gbs, recvs = _exchange_wait(f"scatter_grads_wait_{layer}_{stage}", "scatter", scattered[layer, stage], small_token)
        for (k, i), gb, recv in zip(_stage_tensors(layer, stage), gbs, recvs):
            stacked[k] = _adamw_layer(f"adamw_{k}_{i}", w[k], m[k], v[k], i, gb, recv, order, stacked[k])
    grads, deltas, new_m, new_v = ({k: stacked[k][j] for k in SHARDED} for j in range(4))

    last_updated = stacked[_stage_tensors(*stages[0])[-1][0]][0]
    (packed,), (recv,) = _exchange_wait("allgather_small_grads_wait", "allgather", small_handle, last_updated)
    g_small = _ordered_sum("sum_small_grads", packed, recv, order)
    outs = _adamw("adamw_small", _pack(small_params), _pack({k: m[k] for k in SMALL}), _pack({k: v[k] for k in SMALL}), g_small)
    for d_, packed in zip((grads, deltas, new_m, new_v), outs):
        d_.update(_unpack(packed, small_params))

    return (loss, dx.reshape(1, S, D), *[grads[k] for k in WEIGHTS], *[deltas[k] for k in WEIGHTS],
            *[new_m[k] for k in WEIGHTS], *[new_v[k] for k in WEIGHTS])
```

```python
import functools

import jax
import jax.numpy as jnp
from jax import lax
from jax.experimental import pallas as pl
from jax.experimental.pallas import tpu as pltpu

F32, BF16 = jnp.float32, jnp.bfloat16
MESH = pl.DeviceIdType.MESH
ANY = pl.BlockSpec(memory_space=pl.ANY)

VMEM_LIMIT_BYTES = 56 * 1024 * 1024
LANES = 128
ELEMENTWISE_ROWS = 256

EPS = 1e-6
NEG = -1e30
HEAD_DIM = 64
N_HEADS = 16
CHUNK = 128
A_GROUPS = 8
CONV_WIDTH = 31
CONV_HALO = 16
CONV_CHUNK = 64
BAND = 64
PATTERN_DILATIONS = (1, 4, 16)
ROT_DIM = 16
ROPE_THETA = 500000.0
N_SHARDS = 4

ADAM_LR, ADAM_B1, ADAM_B2, ADAM_EPS, ADAM_WD, ADAM_STEP = 0.001, 0.9, 0.999, 1e-08, 0.01, 10


def _cp(*sem):
    return pltpu.CompilerParams(dimension_semantics=sem, vmem_limit_bytes=VMEM_LIMIT_BYTES)


def _tile(n, pref):
    t = min(n, pref)
    assert n % t == 0, (n, pref)
    return t


def _dot(a, b, ca, cb):
    return lax.dot_general(a, b, (((ca,), (cb,)), ((), ())), preferred_element_type=F32)


def _mm_ngroup(name, a, w, *, nt, tm, out_dtypes, extras=(), epilogue=None, anchor=None):
    M, K = a.shape
    G, R, C = w.shape
    nw = R if nt else C
    assert K == (C if nt else R)
    tm = _tile(M, tm)
    n_ex = len(extras)
    anchors = [] if anchor is None else [anchor]

    def body(a_ref, w_ref, *rest):
        rest = rest[len(anchors):]
        av = a_ref[...].astype(BF16)
        for g in range(G):
            cols = slice(g * nw, (g + 1) * nw)
            acc = _dot(av, w_ref[g], 1, 1 if nt else 0)
            res = epilogue(acc, *[e[:, cols] for e in rest[:n_ex]]) if epilogue else (acc,)
            for o_ref, r in zip(rest[n_ex:], res):
                o_ref[:, cols] = r.astype(o_ref.dtype)

    blk = pl.BlockSpec((tm, G * nw), lambda m: (m, 0))
    return pl.pallas_call(
        body, name=name, grid=(M // tm,),
        in_specs=[pl.BlockSpec((tm, K), lambda m: (m, 0)), pl.BlockSpec((G, R, C), lambda m: (0, 0, 0))]
        + [pl.BlockSpec((8, LANES), lambda m: (0, 0))] * len(anchors) + [blk] * n_ex,
        out_specs=[blk] * len(out_dtypes),
        out_shape=[jax.ShapeDtypeStruct((M, G * nw), dt) for dt in out_dtypes],
        compiler_params=_cp("parallel"),
    )(a, w, *anchors, *extras)


def _mm_kgroup(name, a, w, *, nt, tm, out_dtypes, extras=(), vecs=(), n_sums=0, epilogue=None, anchor=None):
    G, R, C = w.shape
    kw, N = (C, R) if nt else (R, C)
    if a.ndim == 3:
        M = a.shape[1]
        assert a.shape[0] == G and a.shape[2] == kw
    else:
        M = a.shape[0]
        assert a.shape[1] == G * kw
    tm = _tile(M, tm)
    n_ex = len(extras)
    a_spec = (pl.BlockSpec((G, tm, kw), lambda m: (0, m, 0)) if a.ndim == 3 else pl.BlockSpec((tm, G * kw), lambda m: (m, 0)))
    anchors = [] if anchor is None else [anchor]

    def body(a_ref, w_ref, *rest):
        rest = rest[len(anchors):]
        acc = None
        for g in range(G):
            a_g = a_ref[g] if a.ndim == 3 else a_ref[:, g * kw:(g + 1) * kw]
            part = _dot(a_g.astype(BF16), w_ref[g], 1, 1 if nt else 0)
            acc = part if acc is None else acc + part
        n_in = n_ex + len(vecs)
        res = epilogue(acc, *[e[...] for e in rest[:n_in]]) if epilogue else (acc,)
        outs = rest[n_in:]
        n_tiles = len(outs) - n_sums
        for o_ref, r in zip(outs[:n_tiles], res[:n_tiles]):
            o_ref[...] = r.astype(o_ref.dtype)
        if n_sums:
            @pl.when(pl.program_id(0) == 0)
            def _():
                for s_ref in outs[n_tiles:]:
                    s_ref[...] = jnp.zeros_like(s_ref)

            for s_ref, r in zip(outs[n_tiles:], res[n_tiles:]):
                s_ref[...] += r

    blk = pl.BlockSpec((tm, N), lambda m: (m, 0))
    row = pl.BlockSpec((1, N), lambda m: (0, 0))
    return pl.pallas_call(
        body, name=name, grid=(M // tm,),
        in_specs=[a_spec, pl.BlockSpec((G, R, C), lambda m: (0, 0, 0))]
        + [pl.BlockSpec((8, LANES), lambda m: (0, 0))] * len(anchors) + [blk] * n_ex
        + [pl.BlockSpec((None, 1, N), lambda m, i=i: (i, 0, 0)) for _, i in vecs],
        out_specs=[blk] * len(out_dtypes) + [row] * n_sums,
        out_shape=[jax.ShapeDtypeStruct((M, N), dt) for dt in out_dtypes] + [jax.ShapeDtypeStruct((1, N), F32)] * n_sums,
        compiler_params=_cp("arbitrary" if n_sums else "parallel"),
    )(a, w, *anchors, *extras, *[v for v, _ in vecs])


def _wgrad(name, a, b, shape, *, a_group, tm, anchor=None):
    G, R, C = shape
    M = a.shape[0]
    tm = _tile(M, tm)
    n_m = M // tm
    anchors = [] if anchor is None else [anchor]

    def body(a_ref, b_ref, *rest):
        gb_ref, gf_ref = rest[len(anchors):]
        m = pl.program_id(1)
        part = _dot(a_ref[...].astype(BF16), b_ref[...].astype(BF16), 0, 0)

        @pl.when(m == 0)
        def _():
            gf_ref[...] = part

        @pl.when(m > 0)
        def _():
            gf_ref[...] += part

        @pl.when(m == n_m - 1)
        def _():
            gb_ref[...] = gf_ref[...].astype(BF16)

    a_spec = pl.BlockSpec((tm, R), (lambda g, m: (m, g)) if a_group else (lambda g, m: (m, 0)))
    if b.ndim == 3:
        assert not a_group
        b_spec = pl.BlockSpec((None, tm, C), lambda g, m: (g, m, 0))
    else:
        b_spec = pl.BlockSpec((tm, C), (lambda g, m: (m, 0)) if a_group else (lambda g, m: (m, g)))
    o_spec = pl.BlockSpec((None, R, C), lambda g, m: (g, 0, 0))
    return pl.pallas_call(
        body, name=name, grid=(G, n_m),
        in_specs=[a_spec, b_spec] + [pl.BlockSpec((8, LANES), lambda g, m: (0, 0))] * len(anchors), out_specs=o_spec,
        out_shape=jax.ShapeDtypeStruct(shape, BF16), scratch_shapes=[pltpu.VMEM((R, C), F32)],
        compiler_params=_cp("parallel", "arbitrary"),
    )(a, b, *anchors)


def _rms_fwd(name, x, g3, layer):
    S, D = x.shape
    tm = _tile(S, 512)

    def body(x_ref, g_ref, h_ref):
        xv = x_ref[...]
        r = lax.rsqrt(jnp.mean(xv * xv, axis=-1, keepdims=True) + EPS)
        h_ref[...] = (xv * r * g_ref[...]).astype(BF16)

    row = pl.BlockSpec((tm, D), lambda m: (m, 0))
    return pl.pallas_call(
        body, name=name, grid=(S // tm,),
        in_specs=[row, pl.BlockSpec((None, 1, D), lambda m: (layer, 0, 0))], out_specs=row,
        out_shape=jax.ShapeDtypeStruct((S, D), BF16), compiler_params=_cp("parallel"),
    )(x, g3)


def _adamw_math(w, m, v, g):
    m2 = ADAM_B1 * m + (1.0 - ADAM_B1) * g
    v2 = ADAM_B2 * v + (1.0 - ADAM_B2) * jnp.square(g)
    m_hat = m2 / (1.0 - ADAM_B1 ** ADAM_STEP)
    v_hat = v2 / (1.0 - ADAM_B2 ** ADAM_STEP)
    return g, -ADAM_LR * (m_hat / (jnp.sqrt(v_hat) + ADAM_EPS) + ADAM_WD * w), m2, v2


def _row_tile(rows):
    return _tile(rows, ELEMENTWISE_ROWS) if rows % ELEMENTWISE_ROWS == 0 else rows


def _adamw(name, w, m, v, g):
    rows, C = w.shape
    tr = _row_tile(rows)

    def body(w_ref, m_ref, v_ref, g_in, g_ref, d_ref, nm_ref, nv_ref):
        for o_ref, val in zip((g_ref, d_ref, nm_ref, nv_ref), _adamw_math(w_ref[...], m_ref[...], v_ref[...], g_in[...])):
            o_ref[...] = val

    blk = pl.BlockSpec((tr, C), lambda i: (i, 0))
    return pl.pallas_call(
        body, name=name, grid=(rows // tr,), in_specs=[blk] * 4, out_specs=[blk] * 4,
        out_shape=[jax.ShapeDtypeStruct((rows, C), F32)] * 4, compiler_params=_cp("parallel"),
    )(w, m, v, g)


def _sum_order():
    x, y, c = lax.axis_index("x"), lax.axis_index("y"), lax.axis_index("c")
    differs = lambda bit, coord: bit + coord - 2 * bit * coord
    slots = [4 * differs(p >> 2 & 1, x) + 2 * differs(p >> 1 & 1, y) + differs(p & 1, c) - 1 for p in range(8)]
    return [jnp.asarray(s, jnp.int32).reshape(1) for s in [2 * x + y, 4 * x + 2 * y + c] + slots]


def _adamw_layer(name, w, m, v, layer, gb, recv, order, outs):
    _, R, C = w.shape
    tr = _row_tile(R)
    n_s = len(order)

    def body(*refs):
        me = refs[1][0]
        w_ref, m_ref, v_ref, own_ref = refs[n_s:n_s + 4]
        theirs, outs_ = refs[n_s + 4:n_s + 12], refs[n_s + 16:]
        g = None
        for p in range(8):
            term = jnp.where(me == p, own_ref[...], theirs[p][...]).astype(F32)
            g = term if g is None else g + term
        for o_ref, val in zip(outs_, _adamw_math(w_ref[...], m_ref[...], v_ref[...], g)):
            o_ref[...] = val

    st = pl.BlockSpec((None, tr, C), lambda i, *s: (layer, i, 0))
    slot = lambda p: pl.BlockSpec((None, tr, C), lambda i, *s: (jnp.maximum(s[2 + p][0], 0), i, 0))
    return pl.pallas_call(
        body, name=name,
        grid_spec=pltpu.PrefetchScalarGridSpec(
            num_scalar_prefetch=n_s, grid=(R // tr,),
            in_specs=[st] * 3 + [pl.BlockSpec((None, tr, C), lambda i, *s: (s[0][0], i, 0))] + [slot(p) for p in range(8)]
            + [ANY] * 4,
            out_specs=[st] * 4),
        out_shape=[jax.ShapeDtypeStruct(w.shape, F32)] * 4, input_output_aliases={n_s + 12 + j: j for j in range(4)},
        compiler_params=_cp("parallel"),
    )(*order, w, m, v, gb, *([recv] * 8), *outs)


def _gelu(x):
    return x * (0.5 * (1.0 + jnp.tanh(0.7978845608028654 * (x + 0.044715 * (x * x * x)))))


def _layernorm(t, g, b):
    mu = jnp.mean(t, axis=-1, keepdims=True)
    var = jnp.mean(jnp.square(t - mu), axis=-1, keepdims=True)
    return (t - mu) * lax.rsqrt(var + EPS) * g + b


def _silu(x):
    return x * jax.nn.sigmoid(x)


def _a_value(zv, g, b):
    return _layernorm(_gelu(zv), g, b)


def _b_tail(gc, g, b):
    return _silu(_layernorm(gc, g, b))


def _first_head(shape):
    return lax.broadcasted_iota(jnp.int32, shape, len(shape) - 1) < HEAD_DIM


def _spatial_mix(spw_ref, vb, tm):
    first = _first_head((CHUNK, LANES))
    rows = []
    for n in range(tm // CHUNK):
        blocks = []
        for j in range(A_GROUPS // 2):
            vblk = vb[n * CHUNK:(n + 1) * CHUNK, j * LANES:(j + 1) * LANES]
            r0 = _dot(spw_ref[2 * j], vblk, 1, 0)
            r1 = _dot(spw_ref[2 * j + 1], vblk, 1, 0)
            blocks.append(jnp.where(first, r0, r1))
        rows.append(jnp.concatenate(blocks, axis=1))
    return jnp.concatenate(rows, axis=0) if len(rows) > 1 else rows[0]


def _ab_tail_out_proj(name, z, gconv, spw, bias_full, vn_g, vn_b, cn_g, cn_b, layer, w, x, g3, g_layer):
    S = z.shape[0]
    AW = 512
    tm = _tile(S, 256)

    def body(zu_ref, zv_ref, gc_ref, spw_ref, bias_ref, vg_ref, vb_ref, cg_ref, cb_ref, w_ref, x_ref, g_ref,
             xo_ref, h_ref, cat_ref):
        u = _gelu(zu_ref[...])
        v = _a_value(zv_ref[...], vg_ref[...], vb_ref[...])
        sv = _spatial_mix(spw_ref, v.astype(BF16), tm) + jnp.tile(bias_ref[...], (tm // CHUNK, 1))
        cat = jnp.concatenate([(u * sv).astype(BF16), _b_tail(gc_ref[...], cg_ref[...], cb_ref[...]).astype(BF16)], axis=1)
        cat_ref[...] = cat
        y, h = _add_norm_epilogue(_dot(cat, w_ref[0], 1, 0), x_ref[...], g_ref[...])
        xo_ref[...] = y
        h_ref[...] = h.astype(BF16)

    vec = pl.BlockSpec((None, 1, AW), lambda m: (layer, 0, 0))
    row = pl.BlockSpec((tm, 2 * AW), lambda m: (m, 0))
    return pl.pallas_call(
        body, name=name, grid=(S // tm,),
        in_specs=[pl.BlockSpec((tm, AW), lambda m: (m, 0)), pl.BlockSpec((tm, AW), lambda m: (m, 1)),
                  pl.BlockSpec((tm, AW), lambda m: (m, 0)),
                  pl.BlockSpec((None, A_GROUPS, CHUNK, CHUNK), lambda m: (layer, 0, 0, 0)),
                  pl.BlockSpec((None, CHUNK, AW), lambda m: (layer, 0, 0)), vec, vec, vec, vec,
                  pl.BlockSpec(w.shape, lambda m: (0, 0, 0)), row, pl.BlockSpec((None, 1, 2 * AW), lambda m: (g_layer, 0, 0))],
        out_specs=[row] * 3,
        out_shape=[jax.ShapeDtypeStruct((S, 2 * AW), dt) for dt in (F32, BF16, BF16)], compiler_params=_cp("parallel"),
    )(z, z, gconv, spw, bias_full, vn_g, vn_b, cn_g, cn_b, w, x, g3)


def _ab_tail_bwd(name, z, gconv, dcat, spw, spw_t, bias_full, vn_g, vn_b, cn_g, cn_b, layer):
    S = z.shape[0]
    AW = 512
    tm = _tile(S, 256)
    n_chunks = tm // CHUNK

    def body(zu_ref, zv_ref, gc_ref, dcat_ref, spw_ref, spwt_ref, bias_ref, vg_ref, vb_ref, cg_ref, cb_ref,
             dz_ref, dgc_ref, dspw_ref, dbias_ref, dvg_ref, dvb_ref, dcg_ref, dcb_ref):
        @pl.when(pl.program_id(0) == 0)
        def _():
            for r in (dspw_ref, dbias_ref, dvg_ref, dvb_ref, dcg_ref, dcb_ref):
                r[...] = jnp.zeros_like(r)

        dya = dcat_ref[:, :AW]
        dyb = dcat_ref[:, AW:]
        u, u_vjp = jax.vjp(_gelu, zu_ref[...])
        v, v_vjp = jax.vjp(_a_value, zv_ref[...], vg_ref[...], vb_ref[...])
        vb16 = v.astype(BF16)
        sv = _spatial_mix(spw_ref, vb16, tm) + jnp.tile(bias_ref[...], (n_chunks, 1))
        (dzu,) = u_vjp(dya * sv)
        dsv = dya * u
        dsv16 = dsv.astype(BF16)
        dv = _spatial_mix(spwt_ref, dsv16, tm)
        dzv, dvg, dvb = v_vjp(dv)
        dz_ref[0] = dzu
        dz_ref[1] = dzv
        dvg_ref[...] += dvg
        dvb_ref[...] += dvb

        first = _first_head((CHUNK, LANES))
        zero = jnp.zeros((), BF16)
        dbias = jnp.zeros((CHUNK, AW), F32)
        for n in range(n_chunks):
            rows = slice(n * CHUNK, (n + 1) * CHUNK)
            dbias = dbias + dsv[rows]
            for j in range(A_GROUPS // 2):
                cols = slice(j * LANES, (j + 1) * LANES)
                dblk, vblk = dsv16[rows, cols], vb16[rows, cols]
                dspw_ref[2 * j] += _dot(jnp.where(first, dblk, zero), vblk, 1, 1)
                dspw_ref[2 * j + 1] += _dot(jnp.where(first, zero, dblk), vblk, 1, 1)
        dbias_ref[...] += dbias

        _, t_vjp = jax.vjp(_b_tail, gc_ref[...], cg_ref[...], cb_ref[...])
        dgc, dcg, dcb = t_vjp(dyb)
        dgc_ref[...] = dgc
        dcg_ref[...] += dcg
        dcb_ref[...] += dcb

    vec = pl.BlockSpec((None, 1, AW), lambda m: (layer, 0, 0))
    spw_spec = pl.BlockSpec((None, A_GROUPS, CHUNK, CHUNK), lambda m: (layer, 0, 0, 0))
    ovec = pl.BlockSpec((1, AW), lambda m: (0, 0))
    return pl.pallas_call(
        body, name=name, grid=(S // tm,),
        in_specs=[pl.BlockSpec((tm, AW), lambda m: (m, 0)), pl.BlockSpec((tm, AW), lambda m: (m, 1)),
                  pl.BlockSpec((tm, AW), lambda m: (m, 0)), pl.BlockSpec((tm, 2 * AW), lambda m: (m, 0)),
                  spw_spec, spw_spec, pl.BlockSpec((None, CHUNK, AW), lambda m: (layer, 0, 0)), vec, vec, vec, vec],
        out_specs=[pl.BlockSpec((2, tm, AW), lambda m: (0, m, 0)), pl.BlockSpec((tm, AW), lambda m: (m, 0)),
                   pl.BlockSpec((A_GROUPS, CHUNK, CHUNK), lambda m: (0, 0, 0)),
                   pl.BlockSpec((CHUNK, AW), lambda m: (0, 0)), ovec, ovec, ovec, ovec],
        out_shape=[jax.ShapeDtypeStruct((4, S, AW), F32), jax.ShapeDtypeStruct((S, AW), F32),
                   jax.ShapeDtypeStruct((A_GROUPS, CHUNK, CHUNK), F32), jax.ShapeDtypeStruct((CHUNK, AW), F32)]
                  + [jax.ShapeDtypeStruct((1, AW), F32)] * 4,
        compiler_params=_cp("arbitrary"),
    )(z, z, gconv, dcat, spw, spw_t, bias_full, vn_g, vn_b, cn_g, cn_b)


def _fold_bias(dbias_full):
    def body(d_ref, o_ref):
        d = d_ref[...]
        hi = d.astype(BF16)
        lo = (d - hi.astype(F32)).astype(BF16)
        r = lax.broadcasted_iota(jnp.int32, (512, LANES), 0)
        c = lax.broadcasted_iota(jnp.int32, (512, LANES), 1)
        fold = jnp.where(lax.shift_right_logical(r, 6) == c, 1.0, 0.0).astype(BF16)
        o_ref[...] = _dot(hi, fold, 1, 0) + _dot(lo, fold, 1, 0)

    return pl.pallas_call(body, name="fold_spatial_bias", out_shape=jax.ShapeDtypeStruct((CHUNK, LANES), F32))(dbias_full)


def _halo_specs(tm, n_halo_blocks, col):
    r = tm // CONV_HALO
    prev = pl.BlockSpec((CONV_HALO, LANES), lambda j, i: (jnp.maximum(i * r - 1, 0), col + j))
    cur = pl.BlockSpec((tm, LANES), lambda j, i: (i, col + j))
    nxt = pl.BlockSpec((CONV_HALO, LANES), lambda j, i: (jnp.minimum((i + 1) * r, n_halo_blocks - 1), col + j))
    return [prev, cur, nxt]


def _fill_halo(scr, prev, cur, nxt, tm, i, n_i):
    scr[0:CONV_HALO, :] = jnp.where(i > 0, prev, 0.0)
    scr[CONV_HALO:CONV_HALO + tm, :] = cur
    scr[CONV_HALO + tm:2 * CONV_HALO + tm, :] = jnp.where(i < n_i - 1, nxt, 0.0)


def _glu_conv_fwd(name, z, cw, cb3, layer):
    S = z.shape[0]
    tm = _tile(S, 512)
    n_i = S // tm
    pad = CONV_WIDTH // 2

    def body(vp, vc, vn, gp, gc, gn, w_ref, b_ref, out_ref, scr):
        i = pl.program_id(1)
        glu = lambda a, b: a[...] * jax.nn.sigmoid(b[...])
        _fill_halo(scr, glu(vp, gp), glu(vc, gc), glu(vn, gn), tm, i, n_i)
        taps = [w_ref[j:j + 1, :] for j in range(CONV_WIDTH)]
        for c0 in range(0, tm, CONV_CHUNK):
            acc = jnp.zeros((CONV_CHUNK, LANES), F32) + b_ref[...]
            for j in range(CONV_WIDTH):
                acc = acc + taps[j] * scr[pl.ds(c0 + CONV_HALO - pad + j, CONV_CHUNK), :]
            out_ref[pl.ds(c0, CONV_CHUNK), :] = acc

    return pl.pallas_call(
        body, name=name, grid=(4, n_i),
        in_specs=_halo_specs(tm, S // CONV_HALO, 8) + _halo_specs(tm, S // CONV_HALO, 12)
        + [pl.BlockSpec((None, CONV_WIDTH, LANES), lambda j, i: (j, 0, 0)),
           pl.BlockSpec((None, 1, LANES), lambda j, i: (layer, 0, j))],
        out_specs=pl.BlockSpec((tm, LANES), lambda j, i: (i, j)),
        out_shape=jax.ShapeDtypeStruct((S, 4 * LANES), F32),
        scratch_shapes=[pltpu.VMEM((tm + 2 * CONV_HALO, LANES), F32)],
        compiler_params=_cp("parallel", "parallel"),
    )(z, z, z, z, z, z, cw, cb3)


def _glu_conv_bwd(name, z, dgconv, dz, cw):
    S = z.shape[0]
    tm = _tile(S, 512)
    n_i = S // tm
    pad = CONV_WIDTH // 2

    def body(vp, vc, vn, gp, gc, gn, dp, dc, dn, w_ref, dz_in, dz_ref, gb_ref, db_ref, g_scr, d_scr, gf_ref):
        i = pl.program_id(1)
        sig = jax.nn.sigmoid(gc[...])
        _fill_halo(g_scr, vp[...] * jax.nn.sigmoid(gp[...]), vc[...] * sig, vn[...] * jax.nn.sigmoid(gn[...]), tm, i, n_i)
        _fill_halo(d_scr, dp[...], dc[...], dn[...], tm, i, n_i)

        @pl.when(i == 0)
        def _():
            gf_ref[...] = jnp.zeros_like(gf_ref)
            db_ref[...] = jnp.zeros_like(db_ref)

        taps = [w_ref[j:j + 1, :] for j in range(CONV_WIDTH)]
        dw = [jnp.zeros((8, LANES), F32) for _ in range(CONV_WIDTH)]
        db = jnp.zeros((8, LANES), F32)
        fold8 = lambda t: jnp.sum(t.reshape(CONV_CHUNK // 8, 8, LANES), axis=0)
        for c0 in range(0, tm, CONV_CHUNK):
            rows = pl.ds(c0, CONV_CHUNK)
            d_cur = dc[rows, :]
            dglu = jnp.zeros((CONV_CHUNK, LANES), F32)
            for j in range(CONV_WIDTH):
                dglu = dglu + taps[j] * d_scr[pl.ds(c0 + CONV_HALO + pad - j, CONV_CHUNK), :]
                dw[j] = dw[j] + fold8(d_cur * g_scr[pl.ds(c0 + CONV_HALO - pad + j, CONV_CHUNK), :])
            db = db + fold8(d_cur)
            sig_c = jax.nn.sigmoid(gc[rows, :])
            dz_ref[0, rows, :] = dglu * sig_c
            dz_ref[1, rows, :] = dglu * vc[rows, :] * sig_c * (1.0 - sig_c)
        for j in range(CONV_WIDTH):
            gf_ref[j:j + 1, :] += jnp.sum(dw[j], axis=0, keepdims=True)
        db_ref[...] += jnp.sum(db, axis=0, keepdims=True)

        @pl.when(i == n_i - 1)
        def _():
            gb_ref[...] = gf_ref[...].astype(BF16)

    w_spec = pl.BlockSpec((None, CONV_WIDTH, LANES), lambda j, i: (j, 0, 0))
    return pl.pallas_call(
        body, name=name, grid=(4, n_i),
        in_specs=_halo_specs(tm, S // CONV_HALO, 8) + _halo_specs(tm, S // CONV_HALO, 12)
        + _halo_specs(tm, S // CONV_HALO, 0) + [w_spec, ANY],
        out_specs=[pl.BlockSpec((2, tm, LANES), lambda j, i: (1, i, j)),
                   w_spec, pl.BlockSpec((1, LANES), lambda j, i: (0, j))],
        out_shape=[jax.ShapeDtypeStruct(dz.shape, F32), jax.ShapeDtypeStruct(cw.shape, BF16),
                   jax.ShapeDtypeStruct((1, 4 * LANES), F32)],
        input_output_aliases={10: 0},
        scratch_shapes=[pltpu.VMEM((tm + 2 * CONV_HALO, LANES), F32)] * 2 + [pltpu.VMEM((CONV_WIDTH, LANES), F32)],
        compiler_params=_cp("parallel", "arbitrary"),
    )(z, z, z, z, z, z, dgconv, dgconv, dgconv, cw, dz)


def _seg_matrix(scale):
    r = lax.broadcasted_iota(jnp.int32, (LANES, LANES), 0)
    c = lax.broadcasted_iota(jnp.int32, (LANES, LANES), 1)
    return jnp.where(lax.shift_right_logical(r, 6) == lax.shift_right_logical(c, 6), scale, 0.0).astype(BF16)


def _seg_sum(x, seg):
    hi = x.astype(BF16)
    lo = (x - hi.astype(F32)).astype(BF16)
    return _dot(hi, seg, 1, 0) + _dot(lo, seg, 1, 0)


def _rope_tables(S):
    pos = jnp.arange(S, dtype=F32)
    inv_freq = ROPE_THETA ** (-jnp.arange(0, ROT_DIM, 2, dtype=F32) / ROT_DIM)
    ang = pos[:, None] * inv_freq[None, :]
    cos, sin = jnp.cos(ang), jnp.sin(ang)
    half = ROT_DIM // 2
    rest = HEAD_DIM - ROT_DIM
    one, zero = jnp.ones((S, rest), F32), jnp.zeros((S, rest), F32)
    zh = jnp.zeros((S, half), F32)
    c = jnp.concatenate([cos, cos, one], axis=1)
    sa = jnp.concatenate([-sin, zh, zero], axis=1)
    sb = jnp.concatenate([zh, sin, zero], axis=1)
    return [jnp.tile(t, (1, 2)) for t in (c, sa, sb)]


QK_CHUNK = 64


def _qk_fwd(name, qkv, gq, gk, tables):
    S = qkv.shape[0]
    W = N_HEADS * HEAD_DIM
    tm = _tile(S, 256)
    half = ROT_DIM // 2

    def body(q_ref, k_ref, gq_ref, gk_ref, c_ref, sa_ref, sb_ref, qn_ref, kn_ref):
        seg = _seg_matrix(1.0 / HEAD_DIM)
        for r0 in range(0, tm, QK_CHUNK):
            rows = pl.ds(r0, QK_CHUNK)
            c, sa, sb = c_ref[rows, :], sa_ref[rows, :], sb_ref[rows, :]
            for t_ref, g_ref, o_ref in ((q_ref, gq_ref, qn_ref), (k_ref, gk_ref, kn_ref)):
                for blk in range(W // LANES):
                    cols = slice(blk * LANES, (blk + 1) * LANES)
                    t = t_ref[rows, cols]
                    y = t * lax.rsqrt(_seg_sum(t * t, seg) + EPS) * g_ref[...]
                    o_ref[rows, cols] = y * c + pltpu.roll(y, LANES - half, 1) * sa + pltpu.roll(y, half, 1) * sb

    row = lambda k: pl.BlockSpec((tm, W), lambda m: (m, k))
    gain = pl.BlockSpec((1, LANES), lambda m: (0, 0))
    tab = pl.BlockSpec((tm, LANES), lambda m: (m, 0))
    return pl.pallas_call(
        body, name=name, grid=(S // tm,),
        in_specs=[row(0), row(1), gain, gain, tab, tab, tab], out_specs=[row(0)] * 2,
        out_shape=[jax.ShapeDtypeStruct((S, W), F32)] * 2, compiler_params=_cp("parallel"),
    )(qkv, qkv, gq, gk, *tables)


def _qk_bwd(name, qkv, gq, gk, tables, dqs, dks, dvs):
    S = qkv.shape[0]
    W = N_HEADS * HEAD_DIM
    tm = _tile(S, 256)
    half = ROT_DIM // 2
    n_p = len(dqs)

    def body(q_ref, k_ref, gq_ref, gk_ref, c_ref, sa_ref, sb_ref, *rest):
        dq_refs, dk_refs, dv_refs = rest[:n_p], rest[n_p:2 * n_p], rest[2 * n_p:3 * n_p]
        dqkv_ref, dgq_ref, dgk_ref = rest[3 * n_p:]

        @pl.when(pl.program_id(0) == 0)
        def _():
            dgq_ref[...] = jnp.zeros_like(dgq_ref)
            dgk_ref[...] = jnp.zeros_like(dgk_ref)

        seg = _seg_matrix(1.0 / HEAD_DIM)
        r_i = lax.broadcasted_iota(jnp.int32, (LANES, LANES), 0)
        c_i = lax.broadcasted_iota(jnp.int32, (LANES, LANES), 1)
        same_dim = jnp.where((r_i & (HEAD_DIM - 1)) == (c_i & (HEAD_DIM - 1)), 1.0, 0.0).astype(BF16)
        dgs = [jnp.zeros((8, LANES), F32), jnp.zeros((8, LANES), F32)]
        fold8 = lambda t: jnp.sum(t.reshape(QK_CHUNK // 8, 8, LANES), axis=0)
        for r0 in range(0, tm, QK_CHUNK):
            rows = pl.ds(r0, QK_CHUNK)
            c, sa, sb = c_ref[rows, :], sa_ref[rows, :], sb_ref[rows, :]
            for idx, (t_ref, g_ref, d_refs) in enumerate(((q_ref, gq_ref, dq_refs), (k_ref, gk_ref, dk_refs))):
                for blk in range(W // LANES):
                    cols = slice(blk * LANES, (blk + 1) * LANES)
                    dout = d_refs[0][rows, cols]
                    for r in d_refs[1:]:
                        dout = dout + r[rows, cols]
                    dy = dout * c + pltpu.roll(dout * sa, half, 1) + pltpu.roll(dout * sb, LANES - half, 1)
                    t = t_ref[rows, cols]
                    r_ = lax.rsqrt(_seg_sum(t * t, seg) + EPS)
                    xhat = t * r_
                    dgs[idx] = dgs[idx] + fold8(dy * xhat)
                    dxhat = dy * g_ref[...]
                    dt = r_ * (dxhat - xhat * _seg_sum(dxhat * xhat, seg))
                    dqkv_ref[rows, idx * W + blk * LANES: idx * W + (blk + 1) * LANES] = dt.astype(BF16)
            dv = dv_refs[0][rows, :]
            for r in dv_refs[1:]:
                dv = dv + r[rows, :]
            dqkv_ref[rows, 2 * W:] = dv.astype(BF16)
        for dg, dg_ref in zip(dgs, (dgq_ref, dgk_ref)):
            dg_ref[...] += jnp.sum(_seg_sum(dg, same_dim), axis=0, keepdims=True)

    row = lambda k: pl.BlockSpec((tm, W), lambda m: (m, k))
    gain = pl.BlockSpec((1, LANES), lambda m: (0, 0))
    tab = pl.BlockSpec((tm, LANES), lambda m: (m, 0))
    return pl.pallas_call(
        body, name=name, grid=(S // tm,),
        in_specs=[row(0), row(1), gain, gain, tab, tab, tab] + [row(0)] * (3 * n_p),
        out_specs=[pl.BlockSpec((tm, 3 * W), lambda m: (m, 0)), gain, gain],
        out_shape=[jax.ShapeDtypeStruct((S, 3 * W), BF16), jax.ShapeDtypeStruct((1, LANES), F32),
                   jax.ShapeDtypeStruct((1, LANES), F32)],
        compiler_params=_cp("arbitrary"),
    )(qkv, qkv, gq, gk, *tables, *dqs, *dks, *dvs)


ATTN_BQ = 2 * BAND
ATTN_ROWS = 16 * ATTN_BQ
V_COL = 2 * N_HEADS * HEAD_DIM // LANES


def _attn_geometry(S, d):
    rows = min(ATTN_ROWS, S)
    halo = BAND * d
    assert rows % (ATTN_BQ * d) == 0 and S % rows == 0, (S, d)
    return rows, halo, rows // (ATTN_BQ * d)


def _attn_specs(S, d, col):
    rows, halo, _ = _attn_geometry(S, d)
    r = rows // halo
    n_h = S // halo
    prev = pl.BlockSpec((halo, LANES), lambda j, i: (jnp.maximum(i * r - 1, 0), col + j))
    cur = pl.BlockSpec((rows, LANES), lambda j, i: (i, col + j))
    nxt = pl.BlockSpec((halo, LANES), lambda j, i: (jnp.minimum((i + 1) * r, n_h - 1), col + j))
    return [prev, cur, nxt]


def _fill_window(scr, prev, cur, nxt, rows, halo):
    scr[0:halo, :] = prev[...]
    scr[halo:halo + rows, :] = cur[...]
    scr[halo + rows:2 * halo + rows, :] = nxt[...]


def _chain_groups(n_sb, d, size):
    chains = [(sb, r) for sb in range(n_sb) for r in range(d)]
    return [chains[j:j + size] for j in range(0, len(chains), size)]


def _strided(ref, start, size, d):
    return ref[pl.ds(start, size, stride=d) if d > 1 else pl.ds(start, size), :]


def _band_mask(i, S, d, sb):
    rows, _, _ = _attn_geometry(S, d)
    L = S // d
    base = i * (rows // d) + sb * ATTN_BQ
    wk = ATTN_BQ + 2 * BAND
    row = lax.broadcasted_iota(jnp.int32, (ATTN_BQ, wk), 0)
    col = lax.broadcasted_iota(jnp.int32, (ATTN_BQ, wk), 1)
    lj = base - BAND + col
    return (jnp.abs(col - BAND - row) <= BAND) & (lj >= 0) & (lj < L)


def _attn_fwd(name, q, k, v, v_col, d):
    S, W = q.shape
    rows, halo, n_sb = _attn_geometry(S, d)
    wk = ATTN_BQ + 2 * BAND
    scale = HEAD_DIM ** -0.5

    def body(q_ref, kp, kc, kn, vp, vc, vn, o_ref, lse_ref, kw, vw):
        i = pl.program_id(1)
        _fill_window(kw, kp, kc, kn, rows, halo)
        _fill_window(vw, vp, vc, vn, rows, halo)
        first = _first_head((ATTN_BQ, LANES))
        heads = (first, jnp.logical_not(first))
        zero = jnp.zeros((), BF16)
        for group in _chain_groups(n_sb, d, 4):
            masks = {sb: _band_mask(i, S, d, sb) for sb in sorted({sb for sb, _ in group})}
            starts = [r + d * sb * ATTN_BQ for sb, r in group]
            qs = [_strided(q_ref, st, ATTN_BQ, d).astype(BF16) for st in starts]
            ks = [_strided(kw, st, wk, d).astype(BF16) for st in starts]
            vs = [_strided(vw, st, wk, d).astype(BF16) for st in starts]
            s_all = [[_dot(jnp.where(hm, qv, zero), kv, 1, 1) for hm in heads] for qv, kv in zip(qs, ks)]
            p_all, den_all, lse_all = [], [], []
            for (sb, _), s_h in zip(group, s_all):
                s_h = [jnp.where(masks[sb], s * scale, NEG) for s in s_h]
                mx_h = [jnp.max(s, axis=-1, keepdims=True) for s in s_h]
                p_h = [jnp.exp(s - mx) for s, mx in zip(s_h, mx_h)]
                den_h = [jnp.sum(p, axis=-1, keepdims=True) for p in p_h]
                p_all.append([p.astype(BF16) for p in p_h])
                den_all.append(den_h)
                lse_all.append([mx + jnp.log(den) for mx, den in zip(mx_h, den_h)])
            o_all = [[_dot(p, vv, 1, 0) for p in p_h] for p_h, vv in zip(p_all, vs)]
            for st, o_h, den_h, lse_h in zip(starts, o_all, den_all, lse_all):
                dst = pl.ds(st, ATTN_BQ, stride=d) if d > 1 else pl.ds(st, ATTN_BQ)
                o_ref[dst, :] = jnp.where(first, o_h[0] / den_h[0], o_h[1] / den_h[1])
                lse_ref[dst, :] = jnp.where(first, lse_h[0], lse_h[1])

    cur = _attn_specs(S, d, 0)[1]
    return pl.pallas_call(
        body, name=name, grid=(W // LANES, S // rows),
        in_specs=[cur] + _attn_specs(S, d, 0) + _attn_specs(S, d, v_col), out_specs=[cur, cur],
        out_shape=[jax.ShapeDtypeStruct((S, W), F32)] * 2,
        scratch_shapes=[pltpu.VMEM((rows + 2 * halo, LANES), F32)] * 2,
        compiler_params=_cp("parallel", "parallel"),
    )(q, k, k, k, v, v, v)


RESIDUE_BLOCKS = 2


def _attn_fwd_by_residue(name, q, k, v, v_col, d):
    S, W = q.shape
    L = S // d
    rows, halo, n_sb = _attn_geometry(L, 1)
    assert rows == L and d % 8 == 0
    wk = ATTN_BQ + 2 * BAND
    scale = HEAD_DIM ** -0.5
    width = RESIDUE_BLOCKS * LANES
    n_steps = (W // width) * d
    view = lambda t: t.reshape(L, d // 8, 8, t.shape[1])

    def body(q_hbm, k_hbm, v_hbm, o_hbm, l_hbm, qbuf, kwin, vwin, obuf, lbuf, in_sems, out_sems):
        t = pl.program_id(0)

        def rows_of(ref, step, col0):
            r = step % d
            col = pl.multiple_of(col0 * LANES + (step // d) * width, LANES)
            return ref.at[:, r // 8, r % 8, pl.ds(col, width)]

        def loads(step, slot):
            return [pltpu.make_async_copy(rows_of(q_hbm, step, 0), qbuf.at[slot], in_sems.at[slot, 0]),
                    pltpu.make_async_copy(rows_of(k_hbm, step, 0), kwin.at[slot, pl.ds(halo, L)], in_sems.at[slot, 1]),
                    pltpu.make_async_copy(rows_of(v_hbm, step, v_col), vwin.at[slot, pl.ds(halo, L)], in_sems.at[slot, 2])]

        def stores(step, slot):
            return [pltpu.make_async_copy(obuf.at[slot], rows_of(o_hbm, step, 0), out_sems.at[slot, 0]),
                    pltpu.make_async_copy(lbuf.at[slot], rows_of(l_hbm, step, 0), out_sems.at[slot, 1])]

        slot = t % 2

        @pl.when(t == 0)
        def _():
            for s in range(2):
                for win in (kwin, vwin):
                    win[s, 0:halo, :] = jnp.zeros((halo, width), F32)
                    win[s, halo + L:2 * halo + L, :] = jnp.zeros((halo, width), F32)
            for cp in loads(0, 0):
                cp.start()

        @pl.when(t + 1 < n_steps)
        def _():
            for cp in loads(t + 1, 1 - slot):
                cp.start()

        for cp in loads(t, slot):
            cp.wait()

        @pl.when(t >= 2)
        def _():
            for cp in stores(t - 2, slot):
                cp.wait()

        q_ref, kw, vw, o_ref, lse_ref = qbuf.at[slot], kwin.at[slot], vwin.at[slot], obuf.at[slot], lbuf.at[slot]
        first = _first_head((ATTN_BQ, LANES))
        heads = (first, jnp.logical_not(first))
        zero = jnp.zeros((), BF16)
        for cb, group in [(cb, g) for cb in range(RESIDUE_BLOCKS) for g in _chain_groups(n_sb, 1, 4)]:
            lanes = slice(cb * LANES, (cb + 1) * LANES)
            starts = [sb * ATTN_BQ for sb, _ in group]
            qs = [q_ref[pl.ds(st, ATTN_BQ), lanes].astype(BF16) for st in starts]
            ks = [kw[pl.ds(st, wk), lanes].astype(BF16) for st in starts]
            vs = [vw[pl.ds(st, wk), lanes].astype(BF16) for st in starts]
            s_all = [[_dot(jnp.where(hm, qv, zero), kv, 1, 1) for hm in heads] for qv, kv in zip(qs, ks)]
            p_all, den_all, lse_all = [], [], []
            for (sb, _), s_h in zip(group, s_all):
                valid = _band_mask(0, L, 1, sb)
                s_h = [jnp.where(valid, s * scale, NEG) for s in s_h]
                mx_h = [jnp.max(s, axis=-1, keepdims=True) for s in s_h]
                p_h = [jnp.exp(s - mx) for s, mx in zip(s_h, mx_h)]
                den_h = [jnp.sum(p, axis=-1, keepdims=True) for p in p_h]
                p_all.append([p.astype(BF16) for p in p_h])
                den_all.append(den_h)
                lse_all.append([mx + jnp.log(den) for mx, den in zip(mx_h, den_h)])
            o_all = [[_dot(p, vv, 1, 0) for p in p_h] for p_h, vv in zip(p_all, vs)]
            for st, o_h, den_h, lse_h in zip(starts, o_all, den_all, lse_all):
                o_ref[pl.ds(st, ATTN_BQ), lanes] = jnp.where(first, o_h[0] / den_h[0], o_h[1] / den_h[1])
                lse_ref[pl.ds(st, ATTN_BQ), lanes] = jnp.where(first, lse_h[0], lse_h[1])

        for cp in stores(t, slot):
            cp.start()

        @pl.when(t == n_steps - 1)
        def _():
            if n_steps >= 2:
                for cp in stores(t - 1, 1 - slot):
                    cp.wait()
            for cp in stores(t, slot):
                cp.wait()

    outs = pl.pallas_call(
        body, name=name, grid=(n_steps,), in_specs=[ANY] * 3, out_specs=[ANY] * 2,
        out_shape=[jax.ShapeDtypeStruct((L, d // 8, 8, W), F32)] * 2,
        scratch_shapes=[pltpu.VMEM((2, L, width), F32), pltpu.VMEM((2, L + 2 * halo, width), F32),
                        pltpu.VMEM((2, L + 2 * halo, width), F32), pltpu.VMEM((2, L, width), F32),
                        pltpu.VMEM((2, L, width), F32), pltpu.SemaphoreType.DMA((2, 3)), pltpu.SemaphoreType.DMA((2, 2))],
        compiler_params=_cp("arbitrary"),
    )(view(q), view(k), view(v))
    return [o.reshape(S, W) for o in outs]


def _merge_out_proj(name, os, lses, w, x, g3, layer):
    S, W = os[0].shape
    tm = _tile(S, 256)
    n_p = len(os)

    def body(*refs):
        o_refs, l_refs = refs[:n_p], refs[n_p:2 * n_p]
        w_ref, x_ref, g_ref, xo_ref, h_ref, o_ref, lt_ref = refs[2 * n_p:]
        ls = [r[...] for r in l_refs]
        mx = functools.reduce(jnp.maximum, ls)
        es = [jnp.exp(l - mx) for l in ls]
        den = functools.reduce(lambda a, b: a + b, es)
        acc = es[0] * o_refs[0][...]
        for e, r in zip(es[1:], o_refs[1:]):
            acc = acc + e * r[...]
        o = (acc / den).astype(BF16)
        o_ref[...] = o
        lt_ref[...] = mx + jnp.log(den)
        y, h = _add_norm_epilogue(_dot(o, w_ref[0], 1, 0), x_ref[...], g_ref[...])
        xo_ref[...] = y
        h_ref[...] = h.astype(BF16)

    row = pl.BlockSpec((tm, W), lambda m: (m, 0))
    return pl.pallas_call(
        body, name=name, grid=(S // tm,),
        in_specs=[row] * (2 * n_p) + [pl.BlockSpec(w.shape, lambda m: (0, 0, 0)), row,
                                      pl.BlockSpec((None, 1, W), lambda m: (layer, 0, 0))],
        out_specs=[row] * 4,
        out_shape=[jax.ShapeDtypeStruct((S, W), dt) for dt in (F32, BF16, BF16, F32)],
        compiler_params=_cp("parallel"),
    )(*os, *lses, w, x, g3)


def _delta_epilogue(do, o):
    seg = _seg_matrix(1.0)
    prod = do * o.astype(F32)
    delta = [_seg_sum(prod[:, blk * LANES:(blk + 1) * LANES], seg) for blk in range(do.shape[1] // LANES)]
    return do, jnp.concatenate(delta, axis=1)


def _attn_bwd_q(name, q, k, v, v_col, do, lse, delta, d):
    S, W = q.shape
    rows, halo, n_sb = _attn_geometry(S, d)
    wk = ATTN_BQ + 2 * BAND
    scale = HEAD_DIM ** -0.5

    def body(q_ref, do_ref, l_ref, dl_ref, kp, kc, kn, vp, vc, vn, dq_ref, kw, vw):
        i = pl.program_id(1)
        _fill_window(kw, kp, kc, kn, rows, halo)
        _fill_window(vw, vp, vc, vn, rows, halo)
        first = _first_head((ATTN_BQ, LANES))
        heads = (first, jnp.logical_not(first))
        zero = jnp.zeros((), BF16)
        wide = lambda t: jnp.concatenate([t] * (wk // LANES), axis=1)
        for group in _chain_groups(n_sb, d, 4):
            masks = {sb: _band_mask(i, S, d, sb) for sb in sorted({sb for sb, _ in group})}
            starts = [r + d * sb * ATTN_BQ for sb, r in group]
            qs = [_strided(q_ref, st, ATTN_BQ, d).astype(BF16) for st in starts]
            dos = [_strided(do_ref, st, ATTN_BQ, d).astype(BF16) for st in starts]
            ks = [_strided(kw, st, wk, d).astype(BF16) for st in starts]
            vs = [_strided(vw, st, wk, d).astype(BF16) for st in starts]
            s_all = [[_dot(jnp.where(hm, qv, zero), kv, 1, 1) for hm in heads] for qv, kv in zip(qs, ks)]
            dp_all = [[_dot(jnp.where(hm, dov, zero), vv, 1, 1) for hm in heads] for dov, vv in zip(dos, vs)]
            ds_all = []
            for (sb, _), st, s_h, dp_h in zip(group, starts, s_all, dp_all):
                lv, dlv = _strided(l_ref, st, ATTN_BQ, d), _strided(dl_ref, st, ATTN_BQ, d)
                l_sw, dl_sw = pltpu.roll(lv, HEAD_DIM, 1), pltpu.roll(dlv, HEAD_DIM, 1)
                ds_h = []
                for hm, s, dp in zip(heads, s_h, dp_h):
                    p = jnp.exp(jnp.where(masks[sb], s * scale, NEG) - wide(jnp.where(hm, lv, l_sw)))
                    ds_h.append((p * (dp - wide(jnp.where(hm, dlv, dl_sw))) * scale).astype(BF16))
                ds_all.append(ds_h)
            dq_all = [[_dot(ds, kv, 1, 0) for ds in ds_h] for ds_h, kv in zip(ds_all, ks)]
            for st, dq_h in zip(starts, dq_all):
                dst = pl.ds(st, ATTN_BQ, stride=d) if d > 1 else pl.ds(st, ATTN_BQ)
                dq_ref[dst, :] = jnp.where(first, dq_h[0], dq_h[1])

    cur = _attn_specs(S, d, 0)[1]
    return pl.pallas_call(
        body, name=name, grid=(W // LANES, S // rows),
        in_specs=[cur] * 4 + _attn_specs(S, d, 0) + _attn_specs(S, d, v_col), out_specs=cur,
        out_shape=jax.ShapeDtypeStruct((S, W), F32),
        scratch_shapes=[pltpu.VMEM((rows + 2 * halo, LANES), F32)] * 2,
        compiler_params=_cp("parallel", "parallel"),
    )(q, do, lse, delta, k, k, k, v, v, v)


def _attn_bwd_kv(name, q, k, v, v_col, do, lse, delta, d):
    S, W = q.shape
    rows, halo, n_sb = _attn_geometry(S, d)
    wk = ATTN_BQ + 2 * BAND
    scale = HEAD_DIM ** -0.5

    def body(k_ref, v_ref, qp, qc, qn, dop, doc, don, lp, lc, ln, dlp, dlc, dln, dk_ref, dv_ref, qw, dow, lw, dlw):
        i = pl.program_id(1)
        _fill_window(qw, qp, qc, qn, rows, halo)
        _fill_window(dow, dop, doc, don, rows, halo)
        _fill_window(lw, lp, lc, ln, rows, halo)
        _fill_window(dlw, dlp, dlc, dln, rows, halo)
        first = _first_head((ATTN_BQ, LANES))
        heads = (first, jnp.logical_not(first))
        zero = jnp.zeros((), BF16)
        for group in _chain_groups(n_sb, d, 2):
            masks = {sb: _band_mask(i, S, d, sb) for sb in sorted({sb for sb, _ in group})}
            starts = [r + d * sb * ATTN_BQ for sb, r in group]
            ks = [_strided(k_ref, st, ATTN_BQ, d).astype(BF16) for st in starts]
            vs = [_strided(v_ref, st, ATTN_BQ, d).astype(BF16) for st in starts]
            qs = [_strided(qw, st, wk, d).astype(BF16) for st in starts]
            dos = [_strided(dow, st, wk, d).astype(BF16) for st in starts]
            s_all = [[_dot(jnp.where(hm, kv, zero), qv, 1, 1) for hm in heads] for kv, qv in zip(ks, qs)]
            dp_all = [[_dot(jnp.where(hm, vv, zero), dov, 1, 1) for hm in heads] for vv, dov in zip(vs, dos)]
            p_all, ds_all = [], []
            for (sb, _), st, s_h, dp_h in zip(group, starts, s_all, dp_all):
                l_t, dl_t = _strided(lw, st, wk, d).T, _strided(dlw, st, wk, d).T
                p_h = [jnp.exp(jnp.where(masks[sb], s * scale, NEG) - l_t[hh * HEAD_DIM:hh * HEAD_DIM + 1, :])
                       for hh, s in enumerate(s_h)]
                ds_all.append([(p * (dp - dl_t[hh * HEAD_DIM:hh * HEAD_DIM + 1, :]) * scale).astype(BF16)
                               for hh, (p, dp) in enumerate(zip(p_h, dp_h))])
                p_all.append([p.astype(BF16) for p in p_h])
            dv_all = [[_dot(p, dov, 1, 0) for p in p_h] for p_h, dov in zip(p_all, dos)]
            dk_all = [[_dot(ds, qv, 1, 0) for ds in ds_h] for ds_h, qv in zip(ds_all, qs)]
            for st, dk_h, dv_h in zip(starts, dk_all, dv_all):
                dst = pl.ds(st, ATTN_BQ, stride=d) if d > 1 else pl.ds(st, ATTN_BQ)
                dk_ref[dst, :] = jnp.where(first, dk_h[0], dk_h[1])
                dv_ref[dst, :] = jnp.where(first, dv_h[0], dv_h[1])

    cur = _attn_specs(S, d, 0)[1]
    win = _attn_specs(S, d, 0)
    return pl.pallas_call(
        body, name=name, grid=(W // LANES, S // rows),
        in_specs=[cur, _attn_specs(S, d, v_col)[1]] + win * 4, out_specs=[cur, cur],
        out_shape=[jax.ShapeDtypeStruct((S, W), F32)] * 2,
        scratch_shapes=[pltpu.VMEM((rows + 2 * halo, LANES), F32)] * 4,
        compiler_params=_cp("parallel", "parallel"),
    )(k, v, q, q, q, do, do, do, lse, lse, lse, delta, delta, delta)


def _place():
    x, y, c = lax.axis_index("x"), lax.axis_index("y"), lax.axis_index("c")
    chips = [(1 - x, y), (x, 1 - y), (1 - x, 1 - y)]
    return x, y, c, chips


HBM = pl.BlockSpec(memory_space=pltpu.HBM)
SEM = pl.BlockSpec(memory_space=pltpu.SEMAPHORE)
DATAFLOW = pltpu.SideEffectType.DATAFLOW_SIDE_EFFECTING


N_PEERS = {"gather": 3, "scatter": 7, "allgather": 7}


def _exchange_copies(kind, srcs, dsts, send_sems, recv_sems):
    x, y, c, chips = _place()
    mine = 2 * x + y
    n_peers = N_PEERS[kind]
    cps = []
    for t in range(len(srcs)):
        for k in range(n_peers):
            if kind == "gather":
                (px, py), pc = chips[k], c
                src, dst = srcs[t], dsts[t].at[mine]
            else:
                bits = k + 1
                px, py, pc = (1 - x if bits & 4 else x), (1 - y if bits & 2 else y), (1 - c if bits & 1 else c)
                src, dst = (srcs[t].at[2 * px + py] if kind == "scatter" else srcs[t]), dsts[t].at[k]
            cps.append(pltpu.make_async_remote_copy(
                src_ref=src, dst_ref=dst, send_sem=send_sems.at[n_peers * t + k], recv_sem=recv_sems.at[n_peers * t + k],
                device_id=(px, py, pc), device_id_type=MESH))
    return cps


def _exchange_start(name, kind, groups):
    sizes = [len(g) for g in groups]
    n, n_g = sum(sizes), len(groups)

    def body(*refs):
        srcs, dsts = refs[:n], refs[n:2 * n]
        sems = refs[2 * n:2 * n + 2 * n_g]
        token = refs[4 * n + 2 * n_g]
        off = 0
        for gi, size in enumerate(sizes):
            for cp in _exchange_copies(kind, srcs[off:off + size], dsts[off:off + size], sems[2 * gi], sems[2 * gi + 1]):
                cp.start()
            off += size
        token[...] = jnp.zeros_like(token)

    arrays = [pltpu.with_memory_space_constraint(a, pltpu.HBM) for a in
              [s for g in groups for s, _ in g] + [d for g in groups for _, d in g]]
    sem_shapes = []
    for size in sizes:
        sem_shapes += [pltpu.SemaphoreType.DMA((N_PEERS[kind] * size,))] * 2
    outs = pl.pallas_call(
        body, name=name,
        in_specs=[HBM] * (2 * n),
        out_specs=[SEM] * (2 * n_g) + [HBM] * (2 * n) + [pl.BlockSpec(memory_space=pltpu.VMEM)],
        out_shape=sem_shapes + [pltpu.HBM(a.shape, a.dtype) for a in arrays] + [jax.ShapeDtypeStruct((8, LANES), F32)],
        input_output_aliases={t: 2 * n_g + t for t in range(2 * n)},
        compiler_params=pltpu.CompilerParams(has_side_effects=DATAFLOW),
    )(*arrays)
    sems, thru, token = outs[:2 * n_g], outs[2 * n_g:-1], outs[-1]
    handles, off = [], 0
    for gi, size in enumerate(sizes):
        handles.append((sems[2 * gi], sems[2 * gi + 1], thru[off:off + size], thru[n + off:n + off + size]))
        off += size
    return handles, token


def _exchange_wait(name, kind, handle, after):
    send_sems, recv_sems, srcs, dsts = handle
    n = len(srcs)

    def body(*refs):
        for cp in _exchange_copies(kind, refs[:n], refs[n:2 * n], refs[2 * n], refs[2 * n + 1]):
            cp.wait_send()
            cp.wait_recv()

    outs = pl.pallas_call(
        body, name=name,
        in_specs=[HBM] * (2 * n) + [SEM, SEM, ANY], out_specs=[HBM] * (2 * n),
        out_shape=[pltpu.HBM(a.shape, a.dtype) for a in (*srcs, *dsts)],
        input_output_aliases={t: t for t in range(2 * n)},
        compiler_params=pltpu.CompilerParams(has_side_effects=DATAFLOW),
    )(*srcs, *dsts, send_sems, recv_sems, after)
    return outs[:n], outs[n:]


def _prepare_shard(name, w, idx, dtype, mine, anchor=None):
    _, R, C = w.shape
    tr = _row_tile(R)
    anchors = [] if anchor is None else [anchor]

    def body(mine_ref, w_ref, *rest):
        src_ref, land_ref = rest[len(anchors):]
        val = w_ref[...].astype(dtype)
        src_ref[...] = val
        land_ref[...] = val

    return pl.pallas_call(
        body, name=name,
        grid_spec=pltpu.PrefetchScalarGridSpec(
            num_scalar_prefetch=1, grid=(R // tr,),
            in_specs=[pl.BlockSpec((None, tr, C), lambda i, s: (idx, i, 0))]
            + [pl.BlockSpec((8, LANES), lambda i, s: (0, 0))] * len(anchors),
            out_specs=[pl.BlockSpec((tr, C), lambda i, s: (i, 0)), pl.BlockSpec((None, tr, C), lambda i, s: (s[0], i, 0))]),
        out_shape=[jax.ShapeDtypeStruct((R, C), dtype), jax.ShapeDtypeStruct((N_SHARDS, R, C), dtype)],
        compiler_params=_cp("parallel"),
    )(mine, w, *anchors)


def _ordered_sum(name, own, recv, order):
    rows, C = own.shape
    tr = _row_tile(rows)
    n_s = len(order)

    def body(*refs):
        me = refs[1][0]
        own_ref, theirs, out_ref = refs[n_s], refs[n_s + 1:n_s + 9], refs[n_s + 9]
        g = None
        for p in range(8):
            term = jnp.where(me == p, own_ref[...], theirs[p][...])
            g = term if g is None else g + term
        out_ref[...] = g

    blk = pl.BlockSpec((tr, C), lambda i, *s: (i, 0))
    slot = lambda p: pl.BlockSpec((None, tr, C), lambda i, *s: (jnp.maximum(s[2 + p][0], 0), i, 0))
    return pl.pallas_call(
        body, name=name,
        grid_spec=pltpu.PrefetchScalarGridSpec(num_scalar_prefetch=n_s, grid=(rows // tr,),
                                               in_specs=[blk] + [slot(p) for p in range(8)], out_specs=blk),
        out_shape=jax.ShapeDtypeStruct((rows, C), F32), compiler_params=_cp("parallel"),
    )(*order, own, *([recv] * 8))


MM_TM_K = 512
WGRAD_TM = 2048


def _rows_merged(w):
    return w.reshape(1, w.shape[0] * w.shape[1], w.shape[2])


def _sq_relu_epilogue(acc):
    r = jnp.maximum(acc, 0.0)
    return acc, r * r


def _add_epilogue(acc, x):
    return (acc + x,)


def _add_loss_epilogue(acc, x, target):
    e = acc + x - target
    D = e.shape[1]
    share = (0.5 / D) * jnp.sum(jnp.sum(e * e, axis=1, keepdims=True), axis=0, keepdims=True)
    return e * (1.0 / D), jnp.broadcast_to(share, (1, D))


def _add_norm_epilogue(acc, x, g):
    y = acc + x
    r = lax.rsqrt(jnp.mean(y * y, axis=-1, keepdims=True) + EPS)
    return y, y * r * g


def _norm_bwd_epilogue(dh, x, dres, g):
    r = lax.rsqrt(jnp.mean(x * x, axis=-1, keepdims=True) + EPS)
    xhat = x * r
    dxhat = dh * g
    dx = dres + r * (dxhat - xhat * jnp.mean(dxhat * xhat, axis=-1, keepdims=True))
    return dx, jnp.sum(dh * xhat, axis=0, keepdims=True)


def _sq_relu_grad_epilogue(acc, a):
    return (acc * (2.0 * jnp.maximum(a.astype(F32), 0.0)),)


STAGES = ("mixer_in", "mixer_out", "mlp")


def _stage_tensors(layer, stage):
    i = layer // 2
    if stage == "mlp":
        return [("mlp_w1", layer), ("mlp_w2", layer)]
    if stage == "mixer_in":
        return [("ab_w_in", i)] if layer % 2 == 0 else [("c_w_qkv", i)]
    return [("b_conv_w", i), ("ab_w_out", i)] if layer % 2 == 0 else [("c_w_out", i)]


def _local_step(x, target, p, weights_of, grads_done):
    S, D = x.shape
    depth = p["mix_norm_g"].shape[0]
    n_even = (depth + 1) // 2
    mix_g3 = p["mix_norm_g"].reshape(depth, 1, D)
    mlp_g3 = p["mlp_norm_g"].reshape(depth, 1, D)
    vec3 = lambda t: t.reshape(t.shape[0], 1, t.shape[1])
    spw16 = p["a_spatial_w"].astype(BF16)
    spw16_t = jnp.swapaxes(spw16, 2, 3)
    bias_full = jnp.repeat(jnp.swapaxes(p["a_spatial_b"], 1, 2), HEAD_DIM, axis=2)
    vn_g, vn_b, cn_g, cn_b, cb3 = (vec3(p[k]) for k in ("a_vnorm_g", "a_vnorm_b", "b_norm_g", "b_norm_b", "b_conv_b"))
    tables = _rope_tables(S)
    gq = jnp.tile(p["c_q_norm_g"], (1, 2))
    gk = jnp.tile(p["c_k_norm_g"], (1, 2))

    saved = []
    h = _rms_fwd("mix_norm_0", x, mix_g3, 0)
    for layer in range(depth):
        i = layer // 2
        wl = dict(weights_of(layer, "mixer_in", x))
        rec = {"x_mix": x, "w": wl, "h_mix": h}
        if layer % 2 == 0:
            (z,) = _mm_ngroup(f"ab_in_{layer}", h, wl["ab_w_in"], nt=False, tm=MM_TM_K, out_dtypes=[F32])
            wl.update(weights_of(layer, "mixer_out", z))
            gconv = _glu_conv_fwd(f"glu_conv_{layer}", z, wl["b_conv_w"], cb3, i)
            x, h, cat = _ab_tail_out_proj(f"ab_out_{layer}", z, gconv, spw16, bias_full, vn_g, vn_b, cn_g, cn_b, i,
                                          _rows_merged(wl["ab_w_out"]), x, mlp_g3, layer)
            rec.update(z=z, gconv=gconv, cat=cat)
        else:
            (qkv,) = _mm_ngroup(f"c_qkv_{layer}", h, wl["c_w_qkv"], nt=False, tm=MM_TM_K, out_dtypes=[F32])
            wl.update(weights_of(layer, "mixer_out", qkv))
            qn, kn = _qk_fwd(f"qk_norm_rope_{layer}", qkv, gq[i:i + 1], gk[i:i + 1], tables)
            os, lses = zip(*[(_attn_fwd_by_residue if d % 8 == 0 else _attn_fwd)(f"attn_d{d}_{layer}", qn, kn, qkv, V_COL, d)
                             for d in PATTERN_DILATIONS])
            x, h, o, lse = _merge_out_proj(f"c_out_{layer}", os, lses, _rows_merged(wl["c_w_out"]), x, mlp_g3, layer)
            rec.update(qkv=qkv, qn=qn, kn=kn, o=o, lse=lse)
        rec["x_mlp"] = x
        wl.update(weights_of(layer, "mlp", x))
        a, hsq = _mm_ngroup(f"mlp_up_{layer}", h, wl["mlp_w1"], nt=False, tm=MM_TM_K, out_dtypes=[BF16, BF16],
                            epilogue=_sq_relu_epilogue)
        rec.update(h_mlp=h, a=a, hsq=hsq)
        if layer + 1 < depth:
            x, h = _mm_kgroup(f"mlp_down_{layer}", hsq, _rows_merged(wl["mlp_w2"]), nt=False, tm=MM_TM_K,
                              out_dtypes=[F32, BF16], extras=(x,), vecs=[(mix_g3, layer + 1)], epilogue=_add_norm_epilogue)
        else:
            dx, loss_row = _mm_kgroup(f"mlp_down_{layer}", hsq, _rows_merged(wl["mlp_w2"]), nt=False, tm=MM_TM_K,
                                      out_dtypes=[F32], extras=(x, target), n_sums=1, epilogue=_add_loss_epilogue)
        saved.append(rec)

    small = {k: [None] * v.shape[0] for k, v in p.items()}
    token = None
    for layer in reversed(range(depth)):
        i = layer // 2
        rec = saved[layer]
        wl = rec["w"]
        g = {}
        (da,) = _mm_ngroup(f"mlp_down_dgrad_{layer}", dx, wl["mlp_w2"], nt=True, tm=MM_TM_K, out_dtypes=[BF16],
                           extras=(rec["a"],), epilogue=_sq_relu_grad_epilogue, anchor=token)
        g["mlp_w2"] = _wgrad(f"mlp_down_wgrad_{layer}", rec["hsq"], dx, wl["mlp_w2"].shape, a_group=True, tm=WGRAD_TM)
        g["mlp_w1"] = _wgrad(f"mlp_up_wgrad_{layer}", rec["h_mlp"], da, wl["mlp_w1"].shape, a_group=False, tm=WGRAD_TM)
        dx, small["mlp_norm_g"][layer] = _mm_kgroup(
            f"mlp_up_dgrad_{layer}", da, wl["mlp_w1"], nt=True, tm=MM_TM_K, out_dtypes=[F32], extras=(rec["x_mlp"], dx),
            vecs=[(mlp_g3, layer)], n_sums=1, epilogue=_norm_bwd_epilogue)
        token = grads_done(layer, "mlp", g)
        g = {}
        if layer % 2 == 0:
            w_out = _rows_merged(wl["ab_w_out"])
            (dcat,) = _mm_ngroup(f"ab_out_dgrad_{layer}", dx, w_out, nt=True, tm=MM_TM_K, out_dtypes=[F32], anchor=token)
            g["ab_w_out"] = _wgrad(f"ab_out_wgrad_{layer}", rec["cat"], dx, w_out.shape, a_group=True,
                                   tm=WGRAD_TM).reshape(wl["ab_w_out"].shape)
            dz, dgconv, dspw, dbias, dvg, dvb, dcg, dcb = _ab_tail_bwd(
                f"ab_tail_bwd_{layer}", rec["z"], rec["gconv"], dcat, spw16, spw16_t, bias_full, vn_g, vn_b, cn_g, cn_b, i)
            dz, g["b_conv_w"], dcbias = _glu_conv_bwd(f"glu_conv_bwd_{layer}", rec["z"], dgconv, dz, wl["b_conv_w"])
            token = grads_done(layer, "mixer_out", g)
            g = {}
            small["a_spatial_w"][i] = dspw
            small["a_spatial_b"][i] = _fold_bias(dbias)[:, :A_GROUPS].T
            for k, val in (("a_vnorm_g", dvg), ("a_vnorm_b", dvb), ("b_norm_g", dcg), ("b_norm_b", dcb), ("b_conv_b", dcbias)):
                small[k][i] = val
            g["ab_w_in"] = _wgrad(f"ab_in_wgrad_{layer}", rec["h_mix"], dz, wl["ab_w_in"].shape, a_group=False, tm=WGRAD_TM,
                                  anchor=token)
            dgrad = (f"ab_in_dgrad_{layer}", dz, wl["ab_w_in"])
        else:
            w_out = _rows_merged(wl["c_w_out"])
            do, delta = _mm_ngroup(f"c_out_dgrad_{layer}", dx, w_out, nt=True, tm=MM_TM_K, out_dtypes=[F32, F32],
                                   extras=(rec["o"],), epilogue=_delta_epilogue, anchor=token)
            g["c_w_out"] = _wgrad(f"c_out_wgrad_{layer}", rec["o"], dx, w_out.shape, a_group=True,
                                  tm=WGRAD_TM).reshape(wl["c_w_out"].shape)
            token = grads_done(layer, "mixer_out", g)
            g = {}
            attn_args = (rec["qn"], rec["kn"], rec["qkv"], V_COL, do, rec["lse"], delta)
            dqs = [_attn_bwd_q(f"attn_bwd_q_d{d}_{layer}", *attn_args, d) for d in PATTERN_DILATIONS]
            dks, dvs = zip(*[_attn_bwd_kv(f"attn_bwd_kv_d{d}_{layer}", *attn_args, d) for d in PATTERN_DILATIONS])
            dqkv, dgq, dgk = _qk_bwd(f"qk_norm_rope_bwd_{layer}", rec["qkv"], gq[i:i + 1], gk[i:i + 1], tables, dqs, dks, dvs)
            small["c_q_norm_g"][i] = dgq[:, :HEAD_DIM]
            small["c_k_norm_g"][i] = dgk[:, :HEAD_DIM]
            g["c_w_qkv"] = _wgrad(f"c_qkv_wgrad_{layer}", rec["h_mix"], dqkv, wl["c_w_qkv"].shape, a_group=False, tm=WGRAD_TM,
                                  anchor=token)
            dgrad = (f"c_qkv_dgrad_{layer}", dqkv, wl["c_w_qkv"])
        token = grads_done(layer, "mixer_in", g)
        dx, small["mix_norm_g"][layer] = _mm_kgroup(
            *dgrad, nt=True, tm=MM_TM_K, out_dtypes=[F32], extras=(rec["x_mix"], dx), vecs=[(mix_g3, layer)], n_sums=1,
            epilogue=_norm_bwd_epilogue, anchor=token)

    small = {k: jnp.stack([t.reshape(p[k].shape[1:]) for t in v]) for k, v in small.items()}
    return loss_row, dx, small


SHARDED = ("mlp_w1", "mlp_w2", "ab_w_in", "b_conv_w", "ab_w_out", "c_w_qkv", "c_w_out")
SMALL = ("mix_norm_g", "mlp_norm_g", "a_spatial_w", "a_spatial_b", "a_vnorm_g", "a_vnorm_b", "b_conv_b", "b_norm_g",
         "b_norm_b", "c_q_norm_g", "c_k_norm_g")
WEIGHTS = ("mix_norm_g", "mlp_norm_g", "mlp_w1", "mlp_w2", "ab_w_in", "a_spatial_w", "a_spatial_b", "a_vnorm_g",
           "a_vnorm_b", "b_conv_w", "b_conv_b", "b_norm_g", "b_norm_b", "ab_w_out", "c_w_qkv", "c_q_norm_g",
           "c_k_norm_g", "c_w_out")


def _pack(parts):
    flat = jnp.concatenate([parts[k].reshape(-1) for k in SMALL])
    rows = -(-flat.shape[0] // (256 * LANES)) * 256
    return jnp.pad(flat, (0, rows * LANES - flat.shape[0])).reshape(rows, LANES)


def _unpack(packed, like):
    flat = packed.reshape(-1)
    out, off = {}, 0
    for k in SMALL:
        n = like[k].size
        out[k] = flat[off:off + n].reshape(like[k].shape)
        off += n
    return out


def kernel(x, mix_norm_g, mlp_norm_g, mlp_w1, mlp_w2, ab_w_in, a_spatial_w, a_spatial_b, a_vnorm_g, a_vnorm_b, b_conv_w, b_conv_b, b_norm_g, b_norm_b, ab_w_out, c_w_qkv, c_q_norm_g, c_k_norm_g, c_w_out, loss_target, m_mix_norm_g, m_mlp_norm_g, m_mlp_w1, m_mlp_w2, m_ab_w_in, m_a_spatial_w, m_a_spatial_b, m_a_vnorm_g, m_a_vnorm_b, m_b_conv_w, m_b_conv_b, m_b_norm_g, m_b_norm_b, m_ab_w_out, m_c_w_qkv, m_c_q_norm_g, m_c_k_norm_g, m_c_w_out, v_mix_norm_g, v_mlp_norm_g, v_mlp_w1, v_mlp_w2, v_ab_w_in, v_a_spatial_w, v_a_spatial_b, v_a_vnorm_g, v_a_vnorm_b, v_b_conv_w, v_b_conv_b, v_b_norm_g, v_b_norm_b, v_ab_w_out, v_c_w_qkv, v_c_q_norm_g, v_c_k_norm_g, v_c_w_out):
    w = dict(mix_norm_g=mix_norm_g, mlp_norm_g=mlp_norm_g, mlp_w1=mlp_w1, mlp_w2=mlp_w2, ab_w_in=ab_w_in,
             a_spatial_w=a_spatial_w, a_spatial_b=a_spatial_b, a_vnorm_g=a_vnorm_g, a_vnorm_b=a_vnorm_b,
             b_conv_w=b_conv_w, b_conv_b=b_conv_b, b_norm_g=b_norm_g, b_norm_b=b_norm_b, ab_w_out=ab_w_out,
             c_w_qkv=c_w_qkv, c_q_norm_g=c_q_norm_g, c_k_norm_g=c_k_norm_g, c_w_out=c_w_out)
    m = dict(mix_norm_g=m_mix_norm_g, mlp_norm_g=m_mlp_norm_g, mlp_w1=m_mlp_w1, mlp_w2=m_mlp_w2, ab_w_in=m_ab_w_in,
             a_spatial_w=m_a_spatial_w, a_spatial_b=m_a_spatial_b, a_vnorm_g=m_a_vnorm_g, a_vnorm_b=m_a_vnorm_b,
             b_conv_w=m_b_conv_w, b_conv_b=m_b_conv_b, b_norm_g=m_b_norm_g, b_norm_b=m_b_norm_b, ab_w_out=m_ab_w_out,
             c_w_qkv=m_c_w_qkv, c_q_norm_g=m_c_q_norm_g, c_k_norm_g=m_c_k_norm_g, c_w_out=m_c_w_out)
    v = dict(mix_norm_g=v_mix_norm_g, mlp_norm_g=v_mlp_norm_g, mlp_w1=v_mlp_w1, mlp_w2=v_mlp_w2, ab_w_in=v_ab_w_in,
             a_spatial_w=v_a_spatial_w, a_spatial_b=v_a_spatial_b, a_vnorm_g=v_a_vnorm_g, a_vnorm_b=v_a_vnorm_b,
             b_conv_w=v_b_conv_w, b_conv_b=v_b_conv_b, b_norm_g=v_b_norm_g, b_norm_b=v_b_norm_b, ab_w_out=v_ab_w_out,
             c_w_qkv=v_c_w_qkv, c_q_norm_g=v_c_q_norm_g, c_k_norm_g=v_c_k_norm_g, c_w_out=v_c_w_out)

    S, D = x.shape[1], x.shape[2]
    depth = mix_norm_g.shape[0]
    mine = (2 * lax.axis_index("x") + lax.axis_index("y")).astype(jnp.int32).reshape(1)

    stages = [(layer, stage) for layer in range(depth) for stage in STAGES]

    def start_gather(name, some_stages, anchor):
        groups = [[_prepare_shard(f"prepare_{k}_{i}", w[k], i, F32 if k == "b_conv_w" else BF16, mine, anchor)
                   for k, i in _stage_tensors(*st)] for st in some_stages]
        return _exchange_start(name, "gather", groups)

    first, rest = stages[:len(STAGES)], stages[len(STAGES):]
    handles_first, token_first = start_gather("gather_weights_start_first", first, None)
    handles_rest, gather_token = start_gather("gather_weights_start_rest", rest, token_first)
    handles = dict(zip(first + rest, handles_first + handles_rest))

    def weights_of(layer, stage, after):
        _, got = _exchange_wait(f"gather_weights_wait_{layer}_{stage}", "gather", handles[layer, stage],
                                gather_token if (layer, stage) == stages[0] else after)
        return {k: a for (k, _), a in zip(_stage_tensors(layer, stage), got)}

    scattered = {}

    def grads_done(layer, stage, g):
        names = [k for k, _ in _stage_tensors(layer, stage)]
        group = [(g[k], lax.empty((N_PEERS["scatter"],) + g[k].shape[1:], BF16)) for k in names]
        (scattered[layer, stage],), token = _exchange_start(f"scatter_grads_start_{layer}_{stage}", "scatter", [group])
        return token

    small_params = {k: w[k] for k in SMALL}
    loss_row, dx, small_grads = _local_step(x.reshape(S, D), loss_target.reshape(S, D), small_params, weights_of, grads_done)

    loss = lax.psum(loss_row[0, 0], ("x", "y", "c"))

    packed = _pack(small_grads)
    (small_handle,), small_token = _exchange_start(
        "allgather_small_grads_start", "allgather", [[(packed, lax.empty((N_PEERS["allgather"],) + packed.shape, F32))]])

    order = _sum_order()
    stacked = {k: [lax.empty(w[k].shape, F32) for _ in range(4)] for k in SHARDED}
    for layer, stage in reversed(stages):
        gbs, recvs = _exchange_wait(f"scatter_grads_wait_{layer}_{stage}", "scatter", scattered[layer, stage], small_token)
        for (k, i), gb, recv in zip(_stage_tensors(layer, stage), gbs, recvs):
            stacked[k] = _adamw_layer(f"adamw_{k}_{i}", w[k], m[k], v[k], i, gb, recv, order, stacked[k])
    grads, deltas, new_m, new_v = ({k: stacked[k][j] for k in SHARDED} for j in range(4))

    last_updated = stacked[_stage_tensors(*stages[0])[-1][0]][0]
    (packed,), (recv,) = _exchange_wait("allgather_small_grads_wait", "allgather", small_handle, last_updated)
    g_small = _ordered_sum("sum_small_grads", packed, recv, order)
    outs = _adamw("adamw_small", _pack(small_params), _pack({k: m[k] for k in SMALL}), _pack({k: v[k] for k in SMALL}), g_small)
    for d_, packed in zip((grads, deltas, new_m, new_v), outs):
        d_.update(_unpack(packed, small_params))

    return (loss, dx.reshape(1, S, D), *[grads[k] for k in WEIGHTS], *[deltas[k] for k in WEIGHTS],
            *[new_m[k] for k in WEIGHTS], *[new_v[k] for k in WEIGHTS])
```
